```python
import jax, jax.numpy as jnp
from jax import lax
import numpy as np

D_MODEL = 2048
BATCH = 8
SEQ = 4096
DEPTH = 2

GRID_W = 64
CTX_LEN = 256
HEAD_DIM = 128
AXIS_DIM = HEAD_DIM // 2
ATT_Q_HEADS = 12
ATT_KV_HEADS = 4
ATT_GROUPS = ATT_Q_HEADS // ATT_KV_HEADS
ATT_WIDTH = ATT_Q_HEADS * HEAD_DIM
KV_WIDTH = ATT_KV_HEADS * HEAD_DIM
AUX_WIDTH = D_MODEL - ATT_WIDTH
AUX_GROUPS = 4
AUX_GROUP_DIM = AUX_WIDTH // AUX_GROUPS
CONV_WIDTH = 3
POOL_WINDOWS = (2, 4, 8, 16)
WINDOW = 128
Q_BLOCK = 128
BAND = Q_BLOCK + 2 * WINDOW
ROPE_THETA = 10000.0
FFN_HIDDEN = -(-8 * D_MODEL // (3 * 256)) * 256
N_MOD = 6
EPS = 1e-6
NEG_INF = -1e30
ATT_SCALE = HEAD_DIM ** -0.5
IN_WIDTHS_AB = (ATT_WIDTH, KV_WIDTH, KV_WIDTH, AUX_WIDTH, AUX_WIDTH, AUX_WIDTH)
IN_WIDTHS_CD = (ATT_WIDTH, KV_WIDTH, KV_WIDTH, AUX_WIDTH)

kernel_name = "hybrid_prefix_dit_block"


def _split_points(widths):
    return [int(v) for v in np.cumsum(widths)[:-1]]


def _rms_norm(x, g):
    xf = x.astype(jnp.float32)
    y = xf * lax.rsqrt(jnp.mean(xf * xf, axis=-1, keepdims=True) + EPS)
    return (y * g.astype(jnp.float32)).astype(x.dtype)


def _modulate(h, shift, scale):
    return h * (1 + scale) + shift


def _axial_rope(n):
    rows = n // GRID_W
    row = jnp.broadcast_to(jnp.arange(rows, dtype=jnp.float32)[:, None], (rows, GRID_W)).reshape(-1)
    col = jnp.broadcast_to(jnp.arange(GRID_W, dtype=jnp.float32)[None, :], (rows, GRID_W)).reshape(-1)
    inv = jnp.power(ROPE_THETA, -jnp.arange(0, AXIS_DIM, 2, dtype=jnp.float32) / AXIS_DIM)
    ang_r = row[:, None] * inv
    ang_c = col[:, None] * inv
    return (jnp.cos(ang_r), jnp.sin(ang_r), jnp.cos(ang_c), jnp.sin(ang_c))


def _rotate(x, cos, sin):
    x1, x2 = jnp.split(x, 2, axis=-1)
    c = cos[:, None, :].astype(x.dtype)
    s = sin[:, None, :].astype(x.dtype)
    return jnp.concatenate([x1 * c - x2 * s, x2 * c + x1 * s], axis=-1)


def _apply_axial_rope(x, rope):
    cr, sr, cc, sc = rope
    return jnp.concatenate([_rotate(x[..., :AXIS_DIM], cr, sr), _rotate(x[..., AXIS_DIM:], cc, sc)], axis=-1)


def _softmax_attend(q, k, v, bias=None, sink=None):
    s = jnp.einsum('bqkgd,bnkd->bkgqn', q, k).astype(jnp.float32) * ATT_SCALE
    if bias is not None:
        s = s + bias
    if sink is not None:
        sink_col = jnp.broadcast_to(sink.astype(jnp.float32)[None, :, :, None, None], s.shape[:-1] + (1,))
        p = jax.nn.softmax(jnp.concatenate([s, sink_col], axis=-1), axis=-1)[..., :-1]
    else:
        p = jax.nn.softmax(s, axis=-1)
    return jnp.einsum('bkgqn,bnkd->bqkgd', p.astype(v.dtype), v)


def _dense_latent_attention(q, k, v, kc, vc):
    B, S = q.shape[0], q.shape[1]
    nb = S // Q_BLOCK
    k_all = jnp.concatenate([kc, k], axis=1)
    v_all = jnp.concatenate([vc, v], axis=1)
    qb = jnp.moveaxis(q.reshape(B, nb, Q_BLOCK, ATT_KV_HEADS, ATT_GROUPS, HEAD_DIM), 1, 0)
    o = lax.map(lambda qi: _softmax_attend(qi, k_all, v_all), qb)
    return jnp.moveaxis(o, 0, 1).reshape(B, S, ATT_WIDTH)


def _window_latent_attention(q, k, v, kc, vc, sink):
    B, S = q.shape[0], q.shape[1]
    nb = S // Q_BLOCK
    pad = ((0, 0), (WINDOW, WINDOW), (0, 0), (0, 0))
    kp = jnp.pad(k, pad)
    vp = jnp.pad(v, pad)
    qb = jnp.moveaxis(q.reshape(B, nb, Q_BLOCK, ATT_KV_HEADS, ATT_GROUPS, HEAD_DIM), 1, 0)
    ctx_bias = jnp.zeros((Q_BLOCK, kc.shape[1]), jnp.float32)

    def block(args):
        qi, bi = args
        start = bi * Q_BLOCK
        kb = lax.dynamic_slice_in_dim(kp, start, BAND, axis=1)
        vb = lax.dynamic_slice_in_dim(vp, start, BAND, axis=1)
        kpos = start - WINDOW + jnp.arange(BAND, dtype=jnp.int32)
        qpos = start + jnp.arange(Q_BLOCK, dtype=jnp.int32)
        ok = (jnp.abs(kpos[None, :] - qpos[:, None]) <= WINDOW) & (kpos[None, :] >= 0) & (kpos[None, :] < S)
        band_bias = jnp.where(ok, jnp.float32(0.0), jnp.float32(NEG_INF))
        bias = jnp.concatenate([ctx_bias, band_bias], axis=-1)
        return _softmax_attend(qi, jnp.concatenate([kc, kb], axis=1), jnp.concatenate([vc, vb], axis=1), bias, sink)

    o = lax.map(block, (qb, jnp.arange(nb, dtype=jnp.int32)))
    return jnp.moveaxis(o, 0, 1).reshape(B, S, ATT_WIDTH)


def _short_conv(u, w):
    L = u.shape[1]
    half = CONV_WIDTH // 2
    up = jnp.pad(u, ((0, 0), (half, half), (0, 0)))
    out = w[0] * up[:, 0:L]
    for j in range(1, CONV_WIDTH):
        out = out + w[j] * up[:, j:j + L]
    return out


def _multiscale_pool(u, pool_w, pool_scale):
    B, L, _ = u.shape
    ug = u.reshape(B, L, AUX_GROUPS, AUX_GROUP_DIM)
    t = jnp.arange(L, dtype=jnp.int32)
    outs = []
    for g, w in enumerate(POOL_WINDOWS):
        ui = ug[:, :, g].astype(jnp.float32)
        cs = jnp.pad(jnp.cumsum(ui, axis=1), ((0, 0), (1, 0), (0, 0)))
        lo = jnp.clip(t - w // 2, 0, L)
        hi = jnp.clip(t - w // 2 + w, 0, L)
        mean = (cs[:, hi] - cs[:, lo]) / (hi - lo).astype(jnp.float32)[None, :, None]
        outs.append((mean - ui).astype(u.dtype))
    pooled = jnp.stack(outs, axis=2)
    mixed = jnp.einsum('blgc,gcd->blgd', pooled, pool_w).reshape(B, L, AUX_WIDTH)
    return mixed * pool_scale


def _heads(t, n_heads):
    return t.reshape(t.shape[0], t.shape[1], n_heads, HEAD_DIM)


def _ctx_kv(cn, w_in, k_g):
    kv = cn @ w_in[:, ATT_WIDTH:ATT_WIDTH + 2 * KV_WIDTH]
    kc, vc = jnp.split(kv, 2, axis=-1)
    return _rms_norm(_heads(kc, ATT_KV_HEADS), k_g), _heads(vc, ATT_KV_HEADS)


def _mixer_ab(xn, cn, p, rope, need_ctx):
    B, S, _ = xn.shape
    Lc = cn.shape[1]
    q, k, v, gb, gc, u = jnp.split(xn @ p['w_in'], _split_points(IN_WIDTHS_AB), axis=-1)
    q = _apply_axial_rope(_rms_norm(_heads(q, ATT_Q_HEADS), p['q_g']), rope)
    k = _apply_axial_rope(_rms_norm(_heads(k, ATT_KV_HEADS), p['k_g']), rope)
    v = _heads(v, ATT_KV_HEADS)
    kc, vc = _ctx_kv(cn, p['w_in'], p['k_g'])
    attn = _dense_latent_attention(q, k, v, kc, vc)
    conv = gb * _short_conv(gc * u, p['conv_w'])
    y = jnp.concatenate([attn, conv], axis=-1) @ p['w_out']
    yc = None
    if need_ctx:
        qc, _, _, gbc, gcc, uc = jnp.split(cn @ p['w_in'], _split_points(IN_WIDTHS_AB), axis=-1)
        qc = _rms_norm(_heads(qc, ATT_Q_HEADS), p['q_g']).reshape(B, Lc, ATT_KV_HEADS, ATT_GROUPS, HEAD_DIM)
        attn_c = _softmax_attend(qc, kc, vc).reshape(B, Lc, ATT_WIDTH)
        conv_c = gbc * _short_conv(gcc * uc, p['conv_w'])
        yc = jnp.concatenate([attn_c, conv_c], axis=-1) @ p['w_out']
    return y, yc


def _mixer_cd(xn, cn, p, rope, need_ctx):
    B, S, _ = xn.shape
    Lc = cn.shape[1]
    sink = p['sink'].reshape(ATT_KV_HEADS, ATT_GROUPS)
    q, k, v, u = jnp.split(xn @ p['w_in'], _split_points(IN_WIDTHS_CD), axis=-1)
    q = _apply_axial_rope(_rms_norm(_heads(q, ATT_Q_HEADS), p['q_g']), rope)
    k = _apply_axial_rope(_rms_norm(_heads(k, ATT_KV_HEADS), p['k_g']), rope)
    v = _heads(v, ATT_KV_HEADS)
    kc, vc = _ctx_kv(cn, p['w_in'], p['k_g'])
    attn = _window_latent_attention(q, k, v, kc, vc, sink)
    pool = _multiscale_pool(u, p['pool_w'], p['pool_scale'])
    y = jnp.concatenate([attn, pool], axis=-1) @ p['w_out']
    yc = None
    if need_ctx:
        qc, _, _, uc = jnp.split(cn @ p['w_in'], _split_points(IN_WIDTHS_CD), axis=-1)
        qc = _rms_norm(_heads(qc, ATT_Q_HEADS), p['q_g']).reshape(B, Lc, ATT_KV_HEADS, ATT_GROUPS, HEAD_DIM)
        attn_c = _softmax_attend(qc, kc, vc, None, sink).reshape(B, Lc, ATT_WIDTH)
        pool_c = _multiscale_pool(uc, p['pool_w'], p['pool_scale'])
        yc = jnp.concatenate([attn_c, pool_c], axis=-1) @ p['w_out']
    return y, yc


def _swiglu(h, w_gate, w_up, w_down):
    return (jax.nn.silu(h @ w_gate) * (h @ w_up)) @ w_down


def _fwd_setup_inputs(seed: int = 0) -> dict:
    key = jax.random.key(seed)
    keys = iter(jax.random.split(key, 64))
    f32 = jnp.float32

    def nrm(shape, scale):
        return jax.random.normal(next(keys), shape, f32) * scale

    def gain(n):
        return jnp.ones((n,), f32) + nrm((n,), 0.02)

    d_in_ab = sum(IN_WIDTHS_AB)
    d_in_cd = sum(IN_WIDTHS_CD)
    inp = {}
    inp['x'] = nrm((BATCH, SEQ, D_MODEL), 1.0)
    inp['c'] = nrm((BATCH, D_MODEL), 1.0)
    inp['ctx'] = nrm((BATCH, CTX_LEN, D_MODEL), 1.0)
    inp['c_ctx'] = nrm((D_MODEL,), 1.0)
    inp['l0_norm1_g'] = gain(D_MODEL)
    inp['l0_w_mod'] = nrm((D_MODEL, N_MOD * D_MODEL), 0.5 * D_MODEL ** -0.5)
    inp['l0_b_mod'] = nrm((N_MOD * D_MODEL,), 0.02)
    inp['l0_w_in'] = nrm((D_MODEL, d_in_ab), D_MODEL ** -0.5)
    inp['l0_q_norm_g'] = gain(HEAD_DIM)
    inp['l0_k_norm_g'] = gain(HEAD_DIM)
    inp['l0_conv_w'] = nrm((CONV_WIDTH, AUX_WIDTH), CONV_WIDTH ** -0.5)
    inp['l0_w_out'] = nrm((D_MODEL, D_MODEL), D_MODEL ** -0.5)
    inp['l0_norm2_g'] = gain(D_MODEL)
    inp['l0_w_gate'] = nrm((D_MODEL, FFN_HIDDEN), D_MODEL ** -0.5)
    inp['l0_w_up'] = nrm((D_MODEL, FFN_HIDDEN), D_MODEL ** -0.5)
    inp['l0_w_down'] = nrm((FFN_HIDDEN, D_MODEL), FFN_HIDDEN ** -0.5)
    inp['l1_norm1_g'] = gain(D_MODEL)
    inp['l1_w_mod'] = nrm((D_MODEL, N_MOD * D_MODEL), 0.5 * D_MODEL ** -0.5)
    inp['l1_b_mod'] = nrm((N_MOD * D_MODEL,), 0.02)
    inp['l1_w_in'] = nrm((D_MODEL, d_in_cd), D_MODEL ** -0.5)
    inp['l1_q_norm_g'] = gain(HEAD_DIM)
    inp['l1_k_norm_g'] = gain(HEAD_DIM)
    inp['l1_sink'] = nrm((ATT_Q_HEADS,), 0.5)
    inp['l1_pool_w'] = nrm((AUX_GROUPS, AUX_GROUP_DIM, AUX_GROUP_DIM), AUX_GROUP_DIM ** -0.5)
    inp['l1_pool_scale'] = gain(AUX_WIDTH)
    inp['l1_w_out'] = nrm((D_MODEL, D_MODEL), D_MODEL ** -0.5)
    inp['l1_norm2_g'] = gain(D_MODEL)
    inp['l1_w_gate'] = nrm((D_MODEL, FFN_HIDDEN), D_MODEL ** -0.5)
    inp['l1_w_up'] = nrm((D_MODEL, FFN_HIDDEN), D_MODEL ** -0.5)
    inp['l1_w_down'] = nrm((FFN_HIDDEN, D_MODEL), FFN_HIDDEN ** -0.5)
    inp['final_norm_g'] = gain(D_MODEL)
    return inp


def _fwd_reference(x, c, ctx, c_ctx,
              l0_norm1_g, l0_w_mod, l0_b_mod, l0_w_in, l0_q_norm_g, l0_k_norm_g, l0_conv_w, l0_w_out,
              l0_norm2_g, l0_w_gate, l0_w_up, l0_w_down,
              l1_norm1_g, l1_w_mod, l1_b_mod, l1_w_in, l1_q_norm_g, l1_k_norm_g, l1_sink, l1_pool_w,
              l1_pool_scale, l1_w_out, l1_norm2_g, l1_w_gate, l1_w_up, l1_w_down,
              final_norm_g):
    layers = [
        dict(norm1_g=l0_norm1_g, w_mod=l0_w_mod, b_mod=l0_b_mod, w_in=l0_w_in, q_g=l0_q_norm_g,
             k_g=l0_k_norm_g, conv_w=l0_conv_w, w_out=l0_w_out, norm2_g=l0_norm2_g,
             w_gate=l0_w_gate, w_up=l0_w_up, w_down=l0_w_down),
        dict(norm1_g=l1_norm1_g, w_mod=l1_w_mod, b_mod=l1_b_mod, w_in=l1_w_in, q_g=l1_q_norm_g,
             k_g=l1_k_norm_g, sink=l1_sink, pool_w=l1_pool_w, pool_scale=l1_pool_scale, w_out=l1_w_out,
             norm2_g=l1_norm2_g, w_gate=l1_w_gate, w_up=l1_w_up, w_down=l1_w_down),
    ]
    rope = _axial_rope(x.shape[1])
    h = x
    hc = ctx
    sc = jax.nn.silu(c)
    sc_ctx = jax.nn.silu(c_ctx)
    for i in range(DEPTH):
        p = layers[i]
        need_ctx = i < DEPTH - 1
        shift1, scale1, gate1, shift2, scale2, gate2 = jnp.split(sc @ p['w_mod'] + p['b_mod'], N_MOD, axis=-1)
        cshift1, cscale1, cgate1, cshift2, cscale2, cgate2 = jnp.split(sc_ctx @ p['w_mod'] + p['b_mod'], N_MOD, axis=-1)
        xn = _modulate(_rms_norm(h, p['norm1_g']), shift1[:, None], scale1[:, None])
        cn = _modulate(_rms_norm(hc, p['norm1_g']), cshift1, cscale1)
        if i % 2 == 0:
            y, yc = _mixer_ab(xn, cn, p, rope, need_ctx)
        else:
            y, yc = _mixer_cd(xn, cn, p, rope, need_ctx)
        h = h + gate1[:, None] * y
        hn = _modulate(_rms_norm(h, p['norm2_g']), shift2[:, None], scale2[:, None])
        h = h + gate2[:, None] * _swiglu(hn, p['w_gate'], p['w_up'], p['w_down'])
        if need_ctx:
            hc = hc + cgate1 * yc
            hcn = _modulate(_rms_norm(hc, p['norm2_g']), cshift2, cscale2)
            hc = hc + cgate2 * _swiglu(hcn, p['w_gate'], p['w_up'], p['w_down'])
    return _rms_norm(h, final_norm_g)


import jax as _jax
import jax.numpy as _jnp

TWIN_FORMAT = 'train_step'
FWD_PARAMS = ['x', 'c', 'ctx', 'c_ctx', 'l0_norm1_g', 'l0_w_mod', 'l0_b_mod', 'l0_w_in', 'l0_q_norm_g', 'l0_k_norm_g', 'l0_conv_w', 'l0_w_out', 'l0_norm2_g', 'l0_w_gate', 'l0_w_up', 'l0_w_down', 'l1_norm1_g', 'l1_w_mod', 'l1_b_mod', 'l1_w_in', 'l1_q_norm_g', 'l1_k_norm_g', 'l1_sink', 'l1_pool_w', 'l1_pool_scale', 'l1_w_out', 'l1_norm2_g', 'l1_w_gate', 'l1_w_up', 'l1_w_down', 'final_norm_g']
TWIN_WEIGHTS = ['c_ctx', 'l0_norm1_g', 'l0_w_mod', 'l0_b_mod', 'l0_w_in', 'l0_q_norm_g', 'l0_k_norm_g', 'l0_conv_w', 'l0_w_out', 'l0_norm2_g', 'l0_w_gate', 'l0_w_up', 'l0_w_down', 'l1_norm1_g', 'l1_w_mod', 'l1_b_mod', 'l1_w_in', 'l1_q_norm_g', 'l1_k_norm_g', 'l1_sink', 'l1_pool_w', 'l1_pool_scale', 'l1_w_out', 'l1_norm2_g', 'l1_w_gate', 'l1_w_up', 'l1_w_down', 'final_norm_g']
TWIN_DIFF_INPUT = 'x'
TWIN_INPUTS = ['x', 'c', 'ctx', 'c_ctx', 'l0_norm1_g', 'l0_w_mod', 'l0_b_mod', 'l0_w_in', 'l0_q_norm_g', 'l0_k_norm_g', 'l0_conv_w', 'l0_w_out', 'l0_norm2_g', 'l0_w_gate', 'l0_w_up', 'l0_w_down', 'l1_norm1_g', 'l1_w_mod', 'l1_b_mod', 'l1_w_in', 'l1_q_norm_g', 'l1_k_norm_g', 'l1_sink', 'l1_pool_w', 'l1_pool_scale', 'l1_w_out', 'l1_norm2_g', 'l1_w_gate', 'l1_w_up', 'l1_w_down', 'final_norm_g', 'loss_target', 'm_c_ctx', 'm_l0_norm1_g', 'm_l0_w_mod', 'm_l0_b_mod', 'm_l0_w_in', 'm_l0_q_norm_g', 'm_l0_k_norm_g', 'm_l0_conv_w', 'm_l0_w_out', 'm_l0_norm2_g', 'm_l0_w_gate', 'm_l0_w_up', 'm_l0_w_down', 'm_l1_norm1_g', 'm_l1_w_mod', 'm_l1_b_mod', 'm_l1_w_in', 'm_l1_q_norm_g', 'm_l1_k_norm_g', 'm_l1_sink', 'm_l1_pool_w', 'm_l1_pool_scale', 'm_l1_w_out', 'm_l1_norm2_g', 'm_l1_w_gate', 'm_l1_w_up', 'm_l1_w_down', 'm_final_norm_g', 'v_c_ctx', 'v_l0_norm1_g', 'v_l0_w_mod', 'v_l0_b_mod', 'v_l0_w_in', 'v_l0_q_norm_g', 'v_l0_k_norm_g', 'v_l0_conv_w', 'v_l0_w_out', 'v_l0_norm2_g', 'v_l0_w_gate', 'v_l0_w_up', 'v_l0_w_down', 'v_l1_norm1_g', 'v_l1_w_mod', 'v_l1_b_mod', 'v_l1_w_in', 'v_l1_q_norm_g', 'v_l1_k_norm_g', 'v_l1_sink', 'v_l1_pool_w', 'v_l1_pool_scale', 'v_l1_w_out', 'v_l1_norm2_g', 'v_l1_w_gate', 'v_l1_w_up', 'v_l1_w_down', 'v_final_norm_g']
TWIN_OUTPUTS = ['loss', 'grad_x', 'grad_c_ctx', 'grad_l0_norm1_g', 'grad_l0_w_mod', 'grad_l0_b_mod', 'grad_l0_w_in', 'grad_l0_q_norm_g', 'grad_l0_k_norm_g', 'grad_l0_conv_w', 'grad_l0_w_out', 'grad_l0_norm2_g', 'grad_l0_w_gate', 'grad_l0_w_up', 'grad_l0_w_down', 'grad_l1_norm1_g', 'grad_l1_w_mod', 'grad_l1_b_mod', 'grad_l1_w_in', 'grad_l1_q_norm_g', 'grad_l1_k_norm_g', 'grad_l1_sink', 'grad_l1_pool_w', 'grad_l1_pool_scale', 'grad_l1_w_out', 'grad_l1_norm2_g', 'grad_l1_w_gate', 'grad_l1_w_up', 'grad_l1_w_down', 'grad_final_norm_g', 'delta_c_ctx', 'delta_l0_norm1_g', 'delta_l0_w_mod', 'delta_l0_b_mod', 'delta_l0_w_in', 'delta_l0_q_norm_g', 'delta_l0_k_norm_g', 'delta_l0_conv_w', 'delta_l0_w_out', 'delta_l0_norm2_g', 'delta_l0_w_gate', 'delta_l0_w_up', 'delta_l0_w_down', 'delta_l1_norm1_g', 'delta_l1_w_mod', 'delta_l1_b_mod', 'delta_l1_w_in', 'delta_l1_q_norm_g', 'delta_l1_k_norm_g', 'delta_l1_sink', 'delta_l1_pool_w', 'delta_l1_pool_scale', 'delta_l1_w_out', 'delta_l1_norm2_g', 'delta_l1_w_gate', 'delta_l1_w_up', 'delta_l1_w_down', 'delta_final_norm_g', 'new_m_c_ctx', 'new_m_l0_norm1_g', 'new_m_l0_w_mod', 'new_m_l0_b_mod', 'new_m_l0_w_in', 'new_m_l0_q_norm_g', 'new_m_l0_k_norm_g', 'new_m_l0_conv_w', 'new_m_l0_w_out', 'new_m_l0_norm2_g', 'new_m_l0_w_gate', 'new_m_l0_w_up', 'new_m_l0_w_down', 'new_m_l1_norm1_g', 'new_m_l1_w_mod', 'new_m_l1_b_mod', 'new_m_l1_w_in', 'new_m_l1_q_norm_g', 'new_m_l1_k_norm_g', 'new_m_l1_sink', 'new_m_l1_pool_w', 'new_m_l1_pool_scale', 'new_m_l1_w_out', 'new_m_l1_norm2_g', 'new_m_l1_w_gate', 'new_m_l1_w_up', 'new_m_l1_w_down', 'new_m_final_norm_g', 'new_v_c_ctx', 'new_v_l0_norm1_g', 'new_v_l0_w_mod', 'new_v_l0_b_mod', 'new_v_l0_w_in', 'new_v_l0_q_norm_g', 'new_v_l0_k_norm_g', 'new_v_l0_conv_w', 'new_v_l0_w_out', 'new_v_l0_norm2_g', 'new_v_l0_w_gate', 'new_v_l0_w_up', 'new_v_l0_w_down', 'new_v_l1_norm1_g', 'new_v_l1_w_mod', 'new_v_l1_b_mod', 'new_v_l1_w_in', 'new_v_l1_q_norm_g', 'new_v_l1_k_norm_g', 'new_v_l1_sink', 'new_v_l1_pool_w', 'new_v_l1_pool_scale', 'new_v_l1_w_out', 'new_v_l1_norm2_g', 'new_v_l1_w_gate', 'new_v_l1_w_up', 'new_v_l1_w_down', 'new_v_final_norm_g']
TWIN_LEAF_KINDS = {'loss': 'loss', 'grad_x': 'grad_x', 'grad_c_ctx': 'grad_w', 'grad_l0_norm1_g': 'grad_w', 'grad_l0_w_mod': 'grad_w', 'grad_l0_b_mod': 'grad_w', 'grad_l0_w_in': 'grad_w', 'grad_l0_q_norm_g': 'grad_w', 'grad_l0_k_norm_g': 'grad_w', 'grad_l0_conv_w': 'grad_w', 'grad_l0_w_out': 'grad_w', 'grad_l0_norm2_g': 'grad_w', 'grad_l0_w_gate': 'grad_w', 'grad_l0_w_up': 'grad_w', 'grad_l0_w_down': 'grad_w', 'grad_l1_norm1_g': 'grad_w', 'grad_l1_w_mod': 'grad_w', 'grad_l1_b_mod': 'grad_w', 'grad_l1_w_in': 'grad_w', 'grad_l1_q_norm_g': 'grad_w', 'grad_l1_k_norm_g': 'grad_w', 'grad_l1_sink': 'grad_w', 'grad_l1_pool_w': 'grad_w', 'grad_l1_pool_scale': 'grad_w', 'grad_l1_w_out': 'grad_w', 'grad_l1_norm2_g': 'grad_w', 'grad_l1_w_gate': 'grad_w', 'grad_l1_w_up': 'grad_w', 'grad_l1_w_down': 'grad_w', 'grad_final_norm_g': 'grad_w', 'delta_c_ctx': 'delta_w', 'delta_l0_norm1_g': 'delta_w', 'delta_l0_w_mod': 'delta_w', 'delta_l0_b_mod': 'delta_w', 'delta_l0_w_in': 'delta_w', 'delta_l0_q_norm_g': 'delta_w', 'delta_l0_k_norm_g': 'delta_w', 'delta_l0_conv_w': 'delta_w', 'delta_l0_w_out': 'delta_w', 'delta_l0_norm2_g': 'delta_w', 'delta_l0_w_gate': 'delta_w', 'delta_l0_w_up': 'delta_w', 'delta_l0_w_down': 'delta_w', 'delta_l1_norm1_g': 'delta_w', 'delta_l1_w_mod': 'delta_w', 'delta_l1_b_mod': 'delta_w', 'delta_l1_w_in': 'delta_w', 'delta_l1_q_norm_g': 'delta_w', 'delta_l1_k_norm_g': 'delta_w', 'delta_l1_sink': 'delta_w', 'delta_l1_pool_w': 'delta_w', 'delta_l1_pool_scale': 'delta_w', 'delta_l1_w_out': 'delta_w', 'delta_l1_norm2_g': 'delta_w', 'delta_l1_w_gate': 'delta_w', 'delta_l1_w_up': 'delta_w', 'delta_l1_w_down': 'delta_w', 'delta_final_norm_g': 'delta_w', 'new_m_c_ctx': 'new_m', 'new_m_l0_norm1_g': 'new_m', 'new_m_l0_w_mod': 'new_m', 'new_m_l0_b_mod': 'new_m', 'new_m_l0_w_in': 'new_m', 'new_m_l0_q_norm_g': 'new_m', 'new_m_l0_k_norm_g': 'new_m', 'new_m_l0_conv_w': 'new_m', 'new_m_l0_w_out': 'new_m', 'new_m_l0_norm2_g': 'new_m', 'new_m_l0_w_gate': 'new_m', 'new_m_l0_w_up': 'new_m', 'new_m_l0_w_down': 'new_m', 'new_m_l1_norm1_g': 'new_m', 'new_m_l1_w_mod': 'new_m', 'new_m_l1_b_mod': 'new_m', 'new_m_l1_w_in': 'new_m', 'new_m_l1_q_norm_g': 'new_m', 'new_m_l1_k_norm_g': 'new_m', 'new_m_l1_sink': 'new_m', 'new_m_l1_pool_w': 'new_m', 'new_m_l1_pool_scale': 'new_m', 'new_m_l1_w_out': 'new_m', 'new_m_l1_norm2_g': 'new_m', 'new_m_l1_w_gate': 'new_m', 'new_m_l1_w_up': 'new_m', 'new_m_l1_w_down': 'new_m', 'new_m_final_norm_g': 'new_m', 'new_v_c_ctx': 'new_v', 'new_v_l0_norm1_g': 'new_v', 'new_v_l0_w_mod': 'new_v', 'new_v_l0_b_mod': 'new_v', 'new_v_l0_w_in': 'new_v', 'new_v_l0_q_norm_g': 'new_v', 'new_v_l0_k_norm_g': 'new_v', 'new_v_l0_conv_w': 'new_v', 'new_v_l0_w_out': 'new_v', 'new_v_l0_norm2_g': 'new_v', 'new_v_l0_w_gate': 'new_v', 'new_v_l0_w_up': 'new_v', 'new_v_l0_w_down': 'new_v', 'new_v_l1_norm1_g': 'new_v', 'new_v_l1_w_mod': 'new_v', 'new_v_l1_b_mod': 'new_v', 'new_v_l1_w_in': 'new_v', 'new_v_l1_q_norm_g': 'new_v', 'new_v_l1_k_norm_g': 'new_v', 'new_v_l1_sink': 'new_v', 'new_v_l1_pool_w': 'new_v', 'new_v_l1_pool_scale': 'new_v', 'new_v_l1_w_out': 'new_v', 'new_v_l1_norm2_g': 'new_v', 'new_v_l1_w_gate': 'new_v', 'new_v_l1_w_up': 'new_v', 'new_v_l1_w_down': 'new_v', 'new_v_final_norm_g': 'new_v'}


def _forward(args):
    return _fwd_reference(*[args[k] for k in FWD_PARAMS])


def _output_shape():
    def fwd():
        inp = _fwd_setup_inputs(0)
        return _fwd_reference(*[inp[k] for k in FWD_PARAMS])
    out = _jax.eval_shape(fwd)
    return out.shape, out.dtype

N_MICROBATCH = 1
ADAM_LR = 0.001
ADAM_B1 = 0.9
ADAM_B2 = 0.999
ADAM_EPS = 1e-08
ADAM_WD = 0.01
ADAM_STEP = 10
PER_EXAMPLE_BATCH_AXIS = {'x': 0, 'c': 0, 'ctx': 0, 'loss_target': 0}
SHARED_INPUTS = []
_WEIGHT_DTYPES = {'c_ctx': _jnp.float32, 'l0_norm1_g': _jnp.float32, 'l0_w_mod': _jnp.float32, 'l0_b_mod': _jnp.float32, 'l0_w_in': _jnp.float32, 'l0_q_norm_g': _jnp.float32, 'l0_k_norm_g': _jnp.float32, 'l0_conv_w': _jnp.float32, 'l0_w_out': _jnp.float32, 'l0_norm2_g': _jnp.float32, 'l0_w_gate': _jnp.float32, 'l0_w_up': _jnp.float32, 'l0_w_down': _jnp.float32, 'l1_norm1_g': _jnp.float32, 'l1_w_mod': _jnp.float32, 'l1_b_mod': _jnp.float32, 'l1_w_in': _jnp.float32, 'l1_q_norm_g': _jnp.float32, 'l1_k_norm_g': _jnp.float32, 'l1_sink': _jnp.float32, 'l1_pool_w': _jnp.float32, 'l1_pool_scale': _jnp.float32, 'l1_w_out': _jnp.float32, 'l1_norm2_g': _jnp.float32, 'l1_w_gate': _jnp.float32, 'l1_w_up': _jnp.float32, 'l1_w_down': _jnp.float32, 'final_norm_g': _jnp.float32}
MOMENT_SCALE = {'c_ctx': 5.348432e-03, 'l0_norm1_g': 3.129712e-02, 'l0_w_mod': 2.627438e-02, 'l0_b_mod': 4.570591e-02, 'l0_w_in': 2.170746e-02, 'l0_q_norm_g': 7.891910e-03, 'l0_k_norm_g': 7.860764e-03, 'l0_conv_w': 3.538245e-02, 'l0_w_out': 1.853201e-02, 'l0_norm2_g': 2.517447e-02, 'l0_w_gate': 1.139747e-02, 'l0_w_up': 1.104385e-02, 'l0_w_down': 1.828839e-02, 'l1_norm1_g': 1.213812e-02, 'l1_w_mod': 2.094020e-02, 'l1_b_mod': 3.670639e-02, 'l1_w_in': 1.082712e-02, 'l1_q_norm_g': 7.343921e-03, 'l1_k_norm_g': 7.496136e-03, 'l1_sink': 2.325506e-04, 'l1_pool_w': 2.384257e-02, 'l1_pool_scale': 2.575515e-02, 'l1_w_out': 1.308486e-02, 'l1_norm2_g': 2.484072e-02, 'l1_w_gate': 1.110453e-02, 'l1_w_up': 1.073488e-02, 'l1_w_down': 1.780627e-02, 'final_norm_g': 1.599694e+01}


def _to_microbatches(a, axis):
    t = _jnp.moveaxis(a, axis, 0)
    t = t.reshape((N_MICROBATCH, t.shape[0] // N_MICROBATCH) + t.shape[1:])
    return _jnp.moveaxis(t, 1, axis + 1)


def setup_inputs(seed: int = 0) -> dict:
    inp = _fwd_setup_inputs(seed)
    key = _jax.random.fold_in(_jax.random.key(seed), 7919)
    shape, _ = _output_shape()
    out = dict(inp)
    out["loss_target"] = _jax.random.normal(_jax.random.fold_in(key, 0), shape, _jnp.float32)
    for i, name in enumerate(TWIN_WEIGHTS):
        w = inp[name].astype(_jnp.float32)
        if MOMENT_SCALE is None:
            s = _jnp.sqrt(_jnp.mean(_jnp.square(w)) + 1e-30)
        else:
            s = MOMENT_SCALE[name]
        km, kv = _jax.random.split(_jax.random.fold_in(key, i + 1))
        out[name] = w
        out["m_" + name] = s * _jax.random.normal(km, w.shape, _jnp.float32)
        out["v_" + name] = (s * s) * _jax.random.uniform(kv, w.shape, _jnp.float32, 0.5, 1.5)
    if N_MICROBATCH > 1:
        for name, axis in PER_EXAMPLE_BATCH_AXIS.items():
            out[name] = _to_microbatches(out[name], axis)
    return {'x': out['x'], 'c': out['c'], 'ctx': out['ctx'], 'c_ctx': out['c_ctx'], 'l0_norm1_g': out['l0_norm1_g'], 'l0_w_mod': out['l0_w_mod'], 'l0_b_mod': out['l0_b_mod'], 'l0_w_in': out['l0_w_in'], 'l0_q_norm_g': out['l0_q_norm_g'], 'l0_k_norm_g': out['l0_k_norm_g'], 'l0_conv_w': out['l0_conv_w'], 'l0_w_out': out['l0_w_out'], 'l0_norm2_g': out['l0_norm2_g'], 'l0_w_gate': out['l0_w_gate'], 'l0_w_up': out['l0_w_up'], 'l0_w_down': out['l0_w_down'], 'l1_norm1_g': out['l1_norm1_g'], 'l1_w_mod': out['l1_w_mod'], 'l1_b_mod': out['l1_b_mod'], 'l1_w_in': out['l1_w_in'], 'l1_q_norm_g': out['l1_q_norm_g'], 'l1_k_norm_g': out['l1_k_norm_g'], 'l1_sink': out['l1_sink'], 'l1_pool_w': out['l1_pool_w'], 'l1_pool_scale': out['l1_pool_scale'], 'l1_w_out': out['l1_w_out'], 'l1_norm2_g': out['l1_norm2_g'], 'l1_w_gate': out['l1_w_gate'], 'l1_w_up': out['l1_w_up'], 'l1_w_down': out['l1_w_down'], 'final_norm_g': out['final_norm_g'], 'loss_target': out['loss_target'], 'm_c_ctx': out['m_c_ctx'], 'm_l0_norm1_g': out['m_l0_norm1_g'], 'm_l0_w_mod': out['m_l0_w_mod'], 'm_l0_b_mod': out['m_l0_b_mod'], 'm_l0_w_in': out['m_l0_w_in'], 'm_l0_q_norm_g': out['m_l0_q_norm_g'], 'm_l0_k_norm_g': out['m_l0_k_norm_g'], 'm_l0_conv_w': out['m_l0_conv_w'], 'm_l0_w_out': out['m_l0_w_out'], 'm_l0_norm2_g': out['m_l0_norm2_g'], 'm_l0_w_gate': out['m_l0_w_gate'], 'm_l0_w_up': out['m_l0_w_up'], 'm_l0_w_down': out['m_l0_w_down'], 'm_l1_norm1_g': out['m_l1_norm1_g'], 'm_l1_w_mod': out['m_l1_w_mod'], 'm_l1_b_mod': out['m_l1_b_mod'], 'm_l1_w_in': out['m_l1_w_in'], 'm_l1_q_norm_g': out['m_l1_q_norm_g'], 'm_l1_k_norm_g': out['m_l1_k_norm_g'], 'm_l1_sink': out['m_l1_sink'], 'm_l1_pool_w': out['m_l1_pool_w'], 'm_l1_pool_scale': out['m_l1_pool_scale'], 'm_l1_w_out': out['m_l1_w_out'], 'm_l1_norm2_g': out['m_l1_norm2_g'], 'm_l1_w_gate': out['m_l1_w_gate'], 'm_l1_w_up': out['m_l1_w_up'], 'm_l1_w_down': out['m_l1_w_down'], 'm_final_norm_g': out['m_final_norm_g'], 'v_c_ctx': out['v_c_ctx'], 'v_l0_norm1_g': out['v_l0_norm1_g'], 'v_l0_w_mod': out['v_l0_w_mod'], 'v_l0_b_mod': out['v_l0_b_mod'], 'v_l0_w_in': out['v_l0_w_in'], 'v_l0_q_norm_g': out['v_l0_q_norm_g'], 'v_l0_k_norm_g': out['v_l0_k_norm_g'], 'v_l0_conv_w': out['v_l0_conv_w'], 'v_l0_w_out': out['v_l0_w_out'], 'v_l0_norm2_g': out['v_l0_norm2_g'], 'v_l0_w_gate': out['v_l0_w_gate'], 'v_l0_w_up': out['v_l0_w_up'], 'v_l0_w_down': out['v_l0_w_down'], 'v_l1_norm1_g': out['v_l1_norm1_g'], 'v_l1_w_mod': out['v_l1_w_mod'], 'v_l1_b_mod': out['v_l1_b_mod'], 'v_l1_w_in': out['v_l1_w_in'], 'v_l1_q_norm_g': out['v_l1_q_norm_g'], 'v_l1_k_norm_g': out['v_l1_k_norm_g'], 'v_l1_sink': out['v_l1_sink'], 'v_l1_pool_w': out['v_l1_pool_w'], 'v_l1_pool_scale': out['v_l1_pool_scale'], 'v_l1_w_out': out['v_l1_w_out'], 'v_l1_norm2_g': out['v_l1_norm2_g'], 'v_l1_w_gate': out['v_l1_w_gate'], 'v_l1_w_up': out['v_l1_w_up'], 'v_l1_w_down': out['v_l1_w_down'], 'v_final_norm_g': out['v_final_norm_g']}


def _loss(weights, diff, rest, loss_target):
    with _jax.named_scope("forward"):
        args = {**rest, TWIN_DIFF_INPUT: diff, **{k: w.astype(_WEIGHT_DTYPES[k]) for k, w in weights.items()}}
        y = _forward(args)
    with _jax.named_scope("loss_head"):
        err = _jnp.square(y.astype(_jnp.float32) - loss_target)
        return 0.5 * _jnp.sum(_jnp.mean(err, axis=-1)) if err.ndim else 0.5 * err


def _adamw(w, g, m, v):
    m = ADAM_B1 * m + (1.0 - ADAM_B1) * g
    v = ADAM_B2 * v + (1.0 - ADAM_B2) * _jnp.square(g)
    m_hat = m / (1.0 - ADAM_B1 ** ADAM_STEP)
    v_hat = v / (1.0 - ADAM_B2 ** ADAM_STEP)
    delta = -ADAM_LR * (m_hat / (_jnp.sqrt(v_hat) + ADAM_EPS) + ADAM_WD * w)
    return delta, m, v


def reference(x, c, ctx, c_ctx, l0_norm1_g, l0_w_mod, l0_b_mod, l0_w_in, l0_q_norm_g, l0_k_norm_g, l0_conv_w, l0_w_out, l0_norm2_g, l0_w_gate, l0_w_up, l0_w_down, l1_norm1_g, l1_w_mod, l1_b_mod, l1_w_in, l1_q_norm_g, l1_k_norm_g, l1_sink, l1_pool_w, l1_pool_scale, l1_w_out, l1_norm2_g, l1_w_gate, l1_w_up, l1_w_down, final_norm_g, loss_target, m_c_ctx, m_l0_norm1_g, m_l0_w_mod, m_l0_b_mod, m_l0_w_in, m_l0_q_norm_g, m_l0_k_norm_g, m_l0_conv_w, m_l0_w_out, m_l0_norm2_g, m_l0_w_gate, m_l0_w_up, m_l0_w_down, m_l1_norm1_g, m_l1_w_mod, m_l1_b_mod, m_l1_w_in, m_l1_q_norm_g, m_l1_k_norm_g, m_l1_sink, m_l1_pool_w, m_l1_pool_scale, m_l1_w_out, m_l1_norm2_g, m_l1_w_gate, m_l1_w_up, m_l1_w_down, m_final_norm_g, v_c_ctx, v_l0_norm1_g, v_l0_w_mod, v_l0_b_mod, v_l0_w_in, v_l0_q_norm_g, v_l0_k_norm_g, v_l0_conv_w, v_l0_w_out, v_l0_norm2_g, v_l0_w_gate, v_l0_w_up, v_l0_w_down, v_l1_norm1_g, v_l1_w_mod, v_l1_b_mod, v_l1_w_in, v_l1_q_norm_g, v_l1_k_norm_g, v_l1_sink, v_l1_pool_w, v_l1_pool_scale, v_l1_w_out, v_l1_norm2_g, v_l1_w_gate, v_l1_w_up, v_l1_w_down, v_final_norm_g):
    given = dict(x=x, c=c, ctx=ctx, c_ctx=c_ctx, l0_norm1_g=l0_norm1_g, l0_w_mod=l0_w_mod, l0_b_mod=l0_b_mod, l0_w_in=l0_w_in, l0_q_norm_g=l0_q_norm_g, l0_k_norm_g=l0_k_norm_g, l0_conv_w=l0_conv_w, l0_w_out=l0_w_out, l0_norm2_g=l0_norm2_g, l0_w_gate=l0_w_gate, l0_w_up=l0_w_up, l0_w_down=l0_w_down, l1_norm1_g=l1_norm1_g, l1_w_mod=l1_w_mod, l1_b_mod=l1_b_mod, l1_w_in=l1_w_in, l1_q_norm_g=l1_q_norm_g, l1_k_norm_g=l1_k_norm_g, l1_sink=l1_sink, l1_pool_w=l1_pool_w, l1_pool_scale=l1_pool_scale, l1_w_out=l1_w_out, l1_norm2_g=l1_norm2_g, l1_w_gate=l1_w_gate, l1_w_up=l1_w_up, l1_w_down=l1_w_down, final_norm_g=final_norm_g, loss_target=loss_target, m_c_ctx=m_c_ctx, m_l0_norm1_g=m_l0_norm1_g, m_l0_w_mod=m_l0_w_mod, m_l0_b_mod=m_l0_b_mod, m_l0_w_in=m_l0_w_in, m_l0_q_norm_g=m_l0_q_norm_g, m_l0_k_norm_g=m_l0_k_norm_g, m_l0_conv_w=m_l0_conv_w, m_l0_w_out=m_l0_w_out, m_l0_norm2_g=m_l0_norm2_g, m_l0_w_gate=m_l0_w_gate, m_l0_w_up=m_l0_w_up, m_l0_w_down=m_l0_w_down, m_l1_norm1_g=m_l1_norm1_g, m_l1_w_mod=m_l1_w_mod, m_l1_b_mod=m_l1_b_mod, m_l1_w_in=m_l1_w_in, m_l1_q_norm_g=m_l1_q_norm_g, m_l1_k_norm_g=m_l1_k_norm_g, m_l1_sink=m_l1_sink, m_l1_pool_w=m_l1_pool_w, m_l1_pool_scale=m_l1_pool_scale, m_l1_w_out=m_l1_w_out, m_l1_norm2_g=m_l1_norm2_g, m_l1_w_gate=m_l1_w_gate, m_l1_w_up=m_l1_w_up, m_l1_w_down=m_l1_w_down, m_final_norm_g=m_final_norm_g, v_c_ctx=v_c_ctx, v_l0_norm1_g=v_l0_norm1_g, v_l0_w_mod=v_l0_w_mod, v_l0_b_mod=v_l0_b_mod, v_l0_w_in=v_l0_w_in, v_l0_q_norm_g=v_l0_q_norm_g, v_l0_k_norm_g=v_l0_k_norm_g, v_l0_conv_w=v_l0_conv_w, v_l0_w_out=v_l0_w_out, v_l0_norm2_g=v_l0_norm2_g, v_l0_w_gate=v_l0_w_gate, v_l0_w_up=v_l0_w_up, v_l0_w_down=v_l0_w_down, v_l1_norm1_g=v_l1_norm1_g, v_l1_w_mod=v_l1_w_mod, v_l1_b_mod=v_l1_b_mod, v_l1_w_in=v_l1_w_in, v_l1_q_norm_g=v_l1_q_norm_g, v_l1_k_norm_g=v_l1_k_norm_g, v_l1_sink=v_l1_sink, v_l1_pool_w=v_l1_pool_w, v_l1_pool_scale=v_l1_pool_scale, v_l1_w_out=v_l1_w_out, v_l1_norm2_g=v_l1_norm2_g, v_l1_w_gate=v_l1_w_gate, v_l1_w_up=v_l1_w_up, v_l1_w_down=v_l1_w_down, v_final_norm_g=v_final_norm_g)
    weights = {n: given[n] for n in TWIN_WEIGHTS}
    shared = {n: given[n] for n in SHARED_INPUTS}
    per_example = {n: given[n] for n in ['x', 'c', 'ctx']}
    grad_fn = _jax.value_and_grad(_loss, argnums=(0, 1))

    def one_microbatch(ex, loss_target):
        ex = dict(ex)
        diff = ex.pop(TWIN_DIFF_INPUT)
        return grad_fn(weights, diff, {**shared, **ex}, loss_target)

    if N_MICROBATCH == 1:
        loss, (grad_w, grad_x) = one_microbatch(per_example, given["loss_target"])
    else:
        def body(carry, xs):
            loss_sum, grad_sum = carry
            l_k, (gw_k, gx_k) = one_microbatch(xs[0], xs[1])
            with _jax.named_scope("update"):
                return (loss_sum + l_k, _jax.tree.map(_jnp.add, grad_sum, gw_k)), gx_k

        init = (_jnp.zeros((), _jnp.float32), _jax.tree.map(_jnp.zeros_like, weights))
        (loss, grad_w), grad_x = _jax.lax.scan(body, init, (per_example, given["loss_target"]))
    with _jax.named_scope("update"):
        delta_w, new_m, new_v = {}, {}, {}
        for n in TWIN_WEIGHTS:
            delta_w[n], new_m[n], new_v[n] = _adamw(weights[n], grad_w[n], given["m_" + n], given["v_" + n])
    return (loss, grad_x, *[grad_w[n] for n in TWIN_WEIGHTS], *[delta_w[n] for n in TWIN_WEIGHTS],
            *[new_m[n] for n in TWIN_WEIGHTS], *[new_v[n] for n in TWIN_WEIGHTS])
```

```python
import functools
import math

import numpy as np
import jax
import jax.numpy as jnp
from jax import lax
from jax.experimental import pallas as pl
from jax.experimental.pallas import tpu as pltpu

F32 = jnp.float32
BF16 = jnp.bfloat16
HEAD_DIM = 128
AUX_WIDTH = 512
AUX_GROUPS = 4
POOL_HALF = (1, 2, 4, 8)
WINDOW = 128
GRID_W = 64
ROPE_THETA = 10000.0
EPS = 1e-6
NEG_INF = -1e30
ATT_SCALE = HEAD_DIM ** -0.5
N_MOD = 6
N_DEV = 8
ROW_TILE = 256
BAND = ROW_TILE + 2 * WINDOW
ADAM_LR, ADAM_B1, ADAM_B2, ADAM_EPS, ADAM_WD, ADAM_STEP = 0.001, 0.9, 0.999, 1e-08, 0.01, 10
VMEM_LIMIT_MB = 56
MESH = pl.DeviceIdType.MESH

WEIGHT_NAMES = ['c_ctx', 'l0_norm1_g', 'l0_w_mod', 'l0_b_mod', 'l0_w_in', 'l0_q_norm_g', 'l0_k_norm_g', 'l0_conv_w', 'l0_w_out', 'l0_norm2_g', 'l0_w_gate', 'l0_w_up', 'l0_w_down', 'l1_norm1_g', 'l1_w_mod', 'l1_b_mod', 'l1_w_in', 'l1_q_norm_g', 'l1_k_norm_g', 'l1_sink', 'l1_pool_w', 'l1_pool_scale', 'l1_w_out', 'l1_norm2_g', 'l1_w_gate', 'l1_w_up', 'l1_w_down', 'final_norm_g']
INPUT_NAMES = (['x', 'c', 'ctx'] + WEIGHT_NAMES + ['loss_target'] + ['m_' + n for n in WEIGHT_NAMES]
               + ['v_' + n for n in WEIGHT_NAMES])
BIG_WEIGHTS = ('w_in', 'w_out', 'w_gate', 'w_up', 'w_down')


def _params(vmem_mb=VMEM_LIMIT_MB):
    return pltpu.CompilerParams(vmem_limit_bytes=vmem_mb << 20)


def _row_tile(n, cap):
    best = None
    for t in range(16, min(n, cap) + 1, 16):
        if n % t == 0:
            best = t
    assert best is not None, (n, cap)
    return best


def _col_tile(n, cap):
    best = n
    for t in range(128, min(n, cap) + 1, 128):
        if n % t == 0:
            best = t
    return best if best <= cap or n % 128 else n


def _dot(a, b, ca, cb):
    return lax.dot_general(a, b, (((ca,), (cb,)), ((), ())), preferred_element_type=F32)


def _nn(a, b):
    return _dot(a, b, 1, 0)


def _nt(a, b):
    return _dot(a, b, 1, 1)


def _tn(a, b):
    return _dot(a, b, 0, 0)


def _bf(x):
    return x.astype(BF16)


def _exchange(name, arrs, scatter):
    n = len(arrs)
    if scatter:
        out_shape = [jax.ShapeDtypeStruct(a.shape, a.dtype) for a in arrs]
    else:
        out_shape = [jax.ShapeDtypeStruct((N_DEV,) + a.shape, a.dtype) for a in arrs]

    def body(*refs):
        ins, outs = refs[:n], refs[n:2 * n]
        send_sems, recv_sems, local_sems = refs[2 * n:]
        x, y, c = lax.axis_index("x"), lax.axis_index("y"), lax.axis_index("c")
        me = 4 * x + 2 * y + c
        local, remote = [], []
        for a in range(n):
            own = ins[a].at[me] if scatter else ins[a]
            cp = pltpu.make_async_copy(own, outs[a].at[me], local_sems.at[a])
            cp.start()
            local.append(cp)
            for r in range(1, N_DEV):
                px = 1 - x if r & 4 else x
                py = 1 - y if r & 2 else y
                pc = 1 - c if r & 1 else c
                src = ins[a].at[4 * px + 2 * py + pc] if scatter else ins[a]
                cp = pltpu.make_async_remote_copy(
                    src_ref=src, dst_ref=outs[a].at[me], send_sem=send_sems.at[a, r - 1],
                    recv_sem=recv_sems.at[a, r - 1], device_id=(px, py, pc), device_id_type=MESH)
                cp.start()
                remote.append(cp)
        for cp in remote:
            cp.wait()
        for cp in local:
            cp.wait()

    any_spec = pl.BlockSpec(memory_space=pl.ANY)
    return pl.pallas_call(
        body, name=name, out_shape=out_shape,
        in_specs=[any_spec] * n, out_specs=[any_spec] * n,
        scratch_shapes=[pltpu.SemaphoreType.DMA((n, N_DEV - 1)), pltpu.SemaphoreType.DMA((n, N_DEV - 1)),
                        pltpu.SemaphoreType.DMA((n,))],
    )(*arrs)


def _mm_step(name, fn, ins, in_specs, out_shape, out_spec, grid):
    n = len(ins)

    def body(*refs):
        o_ref = refs[n]
        o_ref[...] = fn(*refs[:n]).astype(o_ref.dtype)

    return pl.pallas_call(body, name=name, grid=grid, in_specs=in_specs, out_specs=out_spec,
                          out_shape=out_shape, compiler_params=_params())(*ins)


def _mm_tn(name, a, b, a_spec, b_spec, out_shape, out_spec, acc_shape, grid):
    nk = grid[-1]
    kax = len(grid) - 1

    def body(a_ref, b_ref, o_ref, acc_ref):
        k = pl.program_id(kax)

        @pl.when(k == 0)
        def _():
            acc_ref[...] = jnp.zeros_like(acc_ref)

        acc_ref[...] += _tn(_bf(a_ref[...]), _bf(b_ref[...]))

        @pl.when(k == nk - 1)
        def _():
            o_ref[...] = acc_ref[...].astype(o_ref.dtype)

    return pl.pallas_call(body, name=name, grid=grid, in_specs=[a_spec, b_spec], out_specs=out_spec,
                          out_shape=out_shape, scratch_shapes=[pltpu.VMEM(acc_shape, F32)],
                          compiler_params=_params())(a, b)


def _mm_cols(name, a, w3, out_dtype=F32):
    M, K = a.shape
    J, _, n = w3.shape
    tm = _row_tile(M, 1088)
    return _mm_step(
        name, lambda a_ref, w_ref: _nn(_bf(a_ref[...]), w_ref[...]), [a, w3],
        [pl.BlockSpec((tm, K), lambda j, i: (i, 0)), pl.BlockSpec((None, K, n), lambda j, i: (j, 0, 0))],
        jax.ShapeDtypeStruct((M, J * n), out_dtype), pl.BlockSpec((tm, n), lambda j, i: (i, j)), (J, M // tm))


def _mm_plain(name, a, b, transpose_b, out_dtype=F32, tn=512):
    M, K = a.shape
    N = b.shape[0] if transpose_b else b.shape[1]
    tm = _row_tile(M, 1088)
    tn = _col_tile(N, tn)
    if transpose_b:
        b_spec = pl.BlockSpec((tn, K), lambda j, i: (j, 0))
        fn = lambda a_ref, b_ref: _nt(_bf(a_ref[...]), _bf(b_ref[...]))
    else:
        b_spec = pl.BlockSpec((K, tn), lambda j, i: (0, j))
        fn = lambda a_ref, b_ref: _nn(_bf(a_ref[...]), _bf(b_ref[...]))
    return _mm_step(name, fn, [a, b], [pl.BlockSpec((tm, K), lambda j, i: (i, 0)), b_spec],
                    jax.ShapeDtypeStruct((M, N), out_dtype), pl.BlockSpec((tm, tn), lambda j, i: (i, j)),
                    (N // tn, M // tm))


def _mm_shards_nn(name, a3, w3, tn=512):
    J, M, k = a3.shape
    N = w3.shape[2]
    tm = _row_tile(M, 544)
    tn = _col_tile(N, tn)

    def fn(a_ref, w_ref):
        acc = _nn(a_ref[0], w_ref[0])
        for j in range(1, J):
            acc += _nn(a_ref[j], w_ref[j])
        return acc

    return _mm_step(name, fn, [a3, w3],
                    [pl.BlockSpec((J, tm, k), lambda jn, i: (0, i, 0)), pl.BlockSpec((J, k, tn), lambda jn, i: (0, 0, jn))],
                    jax.ShapeDtypeStruct((M, N), F32), pl.BlockSpec((tm, tn), lambda jn, i: (i, jn)), (N // tn, M // tm))


def _mm_shards_nt2(name, a3, w3a, b3, w3b, tn=512):
    J, M, k = a3.shape
    N = w3a.shape[1]
    tm = _row_tile(M, 544)
    tn = _col_tile(N, tn)

    def fn(a_ref, wa_ref, b_ref, wb_ref):
        acc = _nt(a_ref[0], wa_ref[0]) + _nt(b_ref[0], wb_ref[0])
        for j in range(1, J):
            acc += _nt(a_ref[j], wa_ref[j]) + _nt(b_ref[j], wb_ref[j])
        return acc

    act = pl.BlockSpec((J, tm, k), lambda jn, i: (0, i, 0))
    wsp = pl.BlockSpec((J, tn, k), lambda jn, i: (0, jn, 0))
    return _mm_step(name, fn, [a3, w3a, b3, w3b], [act, wsp, act, wsp],
                    jax.ShapeDtypeStruct((M, N), F32), pl.BlockSpec((tm, tn), lambda jn, i: (i, jn)), (N // tn, M // tm))


def _mm_cols_nt(name, a, w3, tn=512):
    M = a.shape[0]
    J, N, n = w3.shape
    tm = _row_tile(M, 544)
    tn = _col_tile(N, tn)

    def fn(a_ref, w_ref):
        acc = _nt(a_ref[:, 0:n], w_ref[0])
        for j in range(1, J):
            acc += _nt(a_ref[:, j * n:(j + 1) * n], w_ref[j])
        return acc

    return _mm_step(name, fn, [a, w3],
                    [pl.BlockSpec((tm, J * n), lambda jn, i: (i, 0)), pl.BlockSpec((J, tn, n), lambda jn, i: (0, jn, 0))],
                    jax.ShapeDtypeStruct((M, N), F32), pl.BlockSpec((tm, tn), lambda jn, i: (i, jn)), (N // tn, M // tm))


def _wgrad_cols(name, a, b, J):
    T, K = a.shape
    n = b.shape[1] // J
    tt = _row_tile(T, 1088)
    return _mm_tn(name, a, b, pl.BlockSpec((tt, K), lambda j, t: (t, 0)), pl.BlockSpec((tt, n), lambda j, t: (t, j)),
                  jax.ShapeDtypeStruct((J, K, n), BF16), pl.BlockSpec((None, K, n), lambda j, t: (j, 0, 0)), (K, n), (J, T // tt))


def _wgrad_rows(name, a, b, tk=512):
    T, K = a.shape
    N = b.shape[1]
    tt = _row_tile(T, 1088)
    tk = _col_tile(K, tk)
    return _mm_tn(name, a, b, pl.BlockSpec((tt, tk), lambda kb, t: (t, kb)), pl.BlockSpec((tt, N), lambda kb, t: (t, 0)),
                  jax.ShapeDtypeStruct((K, N), BF16), pl.BlockSpec((tk, N), lambda kb, t: (kb, 0)), (tk, N), (K // tk, T // tt))


def _wgrad_up(name, a, b3):
    T, K = a.shape
    J, _, k = b3.shape
    tt = _row_tile(T, 1088)
    return _mm_tn(name, a, b3, pl.BlockSpec((tt, K), lambda j, t: (t, 0)), pl.BlockSpec((None, tt, k), lambda j, t: (j, t, 0)),
                  jax.ShapeDtypeStruct((J, K, k), BF16), pl.BlockSpec((None, K, k), lambda j, t: (j, 0, 0)), (K, k), (J, T // tt))


def _wgrad_down(name, a3, b):
    J, T, k = a3.shape
    N = b.shape[1]
    tt = _row_tile(T, 1088)
    return _mm_tn(name, a3, b, pl.BlockSpec((None, tt, k), lambda j, t: (j, t, 0)), pl.BlockSpec((tt, N), lambda j, t: (t, 0)),
                  jax.ShapeDtypeStruct((J, k, N), BF16), pl.BlockSpec((None, k, N), lambda j, t: (j, 0, 0)), (k, N), (J, T // tt))


def _seg(i):
    return jnp.minimum(i, 1)


def _rstd(x):
    return lax.rsqrt(jnp.mean(x * x, axis=-1, keepdims=True) + EPS)


def _norm_mod(name, h, g, mod, which):
    T, D = h.shape

    def body(h_ref, g_ref, mod_ref, o_ref):
        x = h_ref[...]
        n = x * _rstd(x) * g_ref[...]
        shift = mod_ref[3 * which:3 * which + 1, :]
        scale = mod_ref[3 * which + 1:3 * which + 2, :]
        o_ref[...] = (n * (1 + scale) + shift).astype(o_ref.dtype)

    row = pl.BlockSpec((ROW_TILE, D), lambda i: (i, 0))
    return pl.pallas_call(
        body, name=name, grid=(T // ROW_TILE,),
        in_specs=[row, pl.BlockSpec((1, D), lambda i: (0, 0)), pl.BlockSpec((None, 8, D), lambda i: (_seg(i), 0, 0))],
        out_specs=row, out_shape=jax.ShapeDtypeStruct((T, D), BF16), compiler_params=_params())(h, g, mod)


def _norm_mod_bwd(name, dxn, h, g, mod, which, dres):
    T, D = h.shape

    def body(dxn_ref, h_ref, g_ref, mod_ref, dres_ref, dh_ref, dmod_ref, dg_ref):
        i = pl.program_id(0)
        x = h_ref[...]
        r = _rstd(x)
        xhat = x * r
        g = g_ref[...]
        n = xhat * g
        scale = mod_ref[3 * which + 1:3 * which + 2, :]
        dxn = dxn_ref[...]
        dn = dxn * (1 + scale)
        dxh = dn * g
        dh_ref[...] = dres_ref[...] + r * (dxh - xhat * jnp.mean(dxh * xhat, axis=-1, keepdims=True))

        @pl.when(i <= 1)
        def _():
            dmod_ref[...] = jnp.zeros_like(dmod_ref)

        @pl.when(i == 0)
        def _():
            dg_ref[...] = jnp.zeros_like(dg_ref)

        dmod_ref[3 * which:3 * which + 1, :] += jnp.sum(dxn, axis=0, keepdims=True)
        dmod_ref[3 * which + 1:3 * which + 2, :] += jnp.sum(dxn * n, axis=0, keepdims=True)
        dg_ref[0:1, :] += jnp.sum(dn * xhat, axis=0, keepdims=True)

    row = pl.BlockSpec((ROW_TILE, D), lambda i: (i, 0))
    modspec = pl.BlockSpec((None, 8, D), lambda i: (_seg(i), 0, 0))
    return pl.pallas_call(
        body, name=name, grid=(T // ROW_TILE,),
        in_specs=[row, row, pl.BlockSpec((1, D), lambda i: (0, 0)), modspec, row],
        out_specs=[row, modspec, pl.BlockSpec((8, D), lambda i: (0, 0))],
        out_shape=[jax.ShapeDtypeStruct((T, D), F32), jax.ShapeDtypeStruct((2, 8, D), F32), jax.ShapeDtypeStruct((8, D), F32)],
        compiler_params=_params())(dxn, h, g, mod, dres)


def _gate_res(name, h, y, mod, row_idx):
    T, D = h.shape

    def body(h_ref, y_ref, mod_ref, o_ref):
        o_ref[...] = h_ref[...] + mod_ref[row_idx:row_idx + 1, :] * y_ref[...]

    row = pl.BlockSpec((ROW_TILE, D), lambda i: (i, 0))
    return pl.pallas_call(
        body, name=name, grid=(T // ROW_TILE,),
        in_specs=[row, row, pl.BlockSpec((None, 8, D), lambda i: (_seg(i), 0, 0))],
        out_specs=row, out_shape=jax.ShapeDtypeStruct((T, D), F32), compiler_params=_params())(h, y, mod)


def _gate_bwd(name, dh, y, mod, row_idx):
    T, D = dh.shape

    def body(dh_ref, y_ref, mod_ref, dy_ref, dmod_ref):
        i = pl.program_id(0)
        dh = dh_ref[...]
        dy_ref[...] = (dh * mod_ref[row_idx:row_idx + 1, :]).astype(dy_ref.dtype)

        @pl.when(i <= 1)
        def _():
            dmod_ref[...] = jnp.zeros_like(dmod_ref)

        dmod_ref[row_idx:row_idx + 1, :] += jnp.sum(dh * y_ref[...], axis=0, keepdims=True)

    row = pl.BlockSpec((ROW_TILE, D), lambda i: (i, 0))
    modspec = pl.BlockSpec((None, 8, D), lambda i: (_seg(i), 0, 0))
    return pl.pallas_call(
        body, name=name, grid=(T // ROW_TILE,), in_specs=[row, row, modspec], out_specs=[row, modspec],
        out_shape=[jax.ShapeDtypeStruct((T, D), BF16), jax.ShapeDtypeStruct((2, 8, D), F32)],
        compiler_params=_params())(dh, y, mod)


def _rot(y):
    lane = lax.broadcasted_iota(jnp.int32, y.shape, 1)
    return jnp.where((lane & 32) == 0, pltpu.roll(y, 96, 1), pltpu.roll(y, 32, 1))


def _qk_prep(name, P, q_g, k_g, rope_c, rope_s, cfg):
    T = P.shape[0]
    ATT, KVW = cfg['ATT'], cfg['KVW']

    def body(q_ref, k_ref, v_ref, qg_ref, kg_ref, c_ref, s_ref, qo_ref, ko_ref, vo_ref):
        cc, ss = c_ref[...], s_ref[...]

        def head(x, g):
            y = x * _rstd(x) * g
            return y * cc + _rot(y) * ss

        for hh in range(ATT // HEAD_DIM):
            sl = slice(hh * HEAD_DIM, (hh + 1) * HEAD_DIM)
            qo_ref[:, sl] = head(q_ref[:, sl], qg_ref[...]).astype(qo_ref.dtype)
        for hh in range(KVW // HEAD_DIM):
            sl = slice(hh * HEAD_DIM, (hh + 1) * HEAD_DIM)
            ko_ref[:, sl] = head(k_ref[:, sl], kg_ref[...]).astype(ko_ref.dtype)
        vo_ref[...] = v_ref[...].astype(vo_ref.dtype)

    kb = ATT // KVW
    gain = pl.BlockSpec((1, HEAD_DIM), lambda i: (0, 0))
    tab = pl.BlockSpec((ROW_TILE, HEAD_DIM), lambda i: (i, 0))
    qs = pl.BlockSpec((ROW_TILE, ATT), lambda i: (i, 0))
    ks = pl.BlockSpec((ROW_TILE, KVW), lambda i: (i, 0))
    return pl.pallas_call(
        body, name=name, grid=(T // ROW_TILE,),
        in_specs=[qs, pl.BlockSpec((ROW_TILE, KVW), lambda i: (i, kb)), pl.BlockSpec((ROW_TILE, KVW), lambda i: (i, kb + 1)),
                  gain, gain, tab, tab],
        out_specs=[qs, ks, ks],
        out_shape=[jax.ShapeDtypeStruct((T, ATT), BF16), jax.ShapeDtypeStruct((T, KVW), BF16), jax.ShapeDtypeStruct((T, KVW), BF16)],
        compiler_params=_params())(P, P, P, q_g, k_g, rope_c, rope_s)


def _qk_prep_bwd(name, dqr, dkr, P, q_g, k_g, rope_c, rope_s, cfg):
    T = P.shape[0]
    ATT, KVW = cfg['ATT'], cfg['KVW']

    def body(dq_ref, dk_ref, q_ref, k_ref, qg_ref, kg_ref, c_ref, s_ref, dqo_ref, dko_ref, dqg_ref, dkg_ref):
        i = pl.program_id(0)
        cc, ss = c_ref[...], s_ref[...]

        @pl.when(i == 0)
        def _():
            dqg_ref[...] = jnp.zeros_like(dqg_ref)
            dkg_ref[...] = jnp.zeros_like(dkg_ref)

        def head(x, g, dout):
            dy = dout * cc + _rot(dout * ss)
            r = _rstd(x)
            xhat = x * r
            dxh = dy * g
            dx = r * (dxh - xhat * jnp.mean(dxh * xhat, axis=-1, keepdims=True))
            return dx, jnp.sum(dy * xhat, axis=0, keepdims=True)

        dg = jnp.zeros((1, HEAD_DIM), F32)
        for hh in range(ATT // HEAD_DIM):
            sl = slice(hh * HEAD_DIM, (hh + 1) * HEAD_DIM)
            dx, d = head(q_ref[:, sl], qg_ref[...], dq_ref[:, sl])
            dqo_ref[:, sl] = dx.astype(dqo_ref.dtype)
            dg += d
        dqg_ref[0:1, :] += dg
        dg = jnp.zeros((1, HEAD_DIM), F32)
        for hh in range(KVW // HEAD_DIM):
            sl = slice(hh * HEAD_DIM, (hh + 1) * HEAD_DIM)
            dx, d = head(k_ref[:, sl], kg_ref[...], dk_ref[:, sl])
            dko_ref[:, sl] = dx.astype(dko_ref.dtype)
            dg += d
        dkg_ref[0:1, :] += dg

    kb = ATT // KVW
    gain = pl.BlockSpec((1, HEAD_DIM), lambda i: (0, 0))
    dgain = pl.BlockSpec((8, HEAD_DIM), lambda i: (0, 0))
    tab = pl.BlockSpec((ROW_TILE, HEAD_DIM), lambda i: (i, 0))
    qs = pl.BlockSpec((ROW_TILE, ATT), lambda i: (i, 0))
    ks = pl.BlockSpec((ROW_TILE, KVW), lambda i: (i, 0))
    return pl.pallas_call(
        body, name=name, grid=(T // ROW_TILE,),
        in_specs=[qs, ks, qs, pl.BlockSpec((ROW_TILE, KVW), lambda i: (i, kb)), gain, gain, tab, tab],
        out_specs=[qs, ks, dgain, dgain],
        out_shape=[jax.ShapeDtypeStruct((T, ATT), BF16), jax.ShapeDtypeStruct((T, KVW), BF16),
                   jax.ShapeDtypeStruct((8, HEAD_DIM), F32), jax.ShapeDtypeStruct((8, HEAD_DIM), F32)],
        compiler_params=_params())(dqr, dkr, P, P, q_g, k_g, rope_c, rope_s)


def _att_specs(T, G):
    qs = pl.BlockSpec((ROW_TILE, G * HEAD_DIM), lambda h, i: (i, h))
    kvs = pl.BlockSpec((T, HEAD_DIM), lambda h, i: (0, h))
    return qs, kvs


def _dense_bias(i, T, Lc):
    col = lax.broadcasted_iota(jnp.int32, (1, T), 1)
    return jnp.where(col < jnp.where(i == 0, Lc, T), 0.0, NEG_INF).astype(F32)


def _attn_dense_fwd(name, q, k, v, cfg):
    T, G, Lc = q.shape[0], cfg['G'], cfg['Lc']

    def body(q_ref, k_ref, v_ref, o_ref, lse_ref):
        kk, vv = k_ref[...], v_ref[...]
        bias = _dense_bias(pl.program_id(1), T, Lc)
        for g in range(G):
            sl = slice(g * HEAD_DIM, (g + 1) * HEAD_DIM)
            s = _nt(q_ref[:, sl], kk) * ATT_SCALE + bias
            m = jnp.max(s, axis=1, keepdims=True)
            p = jnp.exp(s - m)
            l = jnp.sum(p, axis=1, keepdims=True)
            o_ref[:, sl] = _nn(_bf(p), vv) / l
            lse_ref[:, sl] = jnp.broadcast_to(m + jnp.log(l), (ROW_TILE, HEAD_DIM))

    qs, kvs = _att_specs(T, G)
    return pl.pallas_call(
        body, name=name, grid=(cfg['NKV'], T // ROW_TILE), in_specs=[qs, kvs, kvs], out_specs=[qs, qs],
        out_shape=[jax.ShapeDtypeStruct(q.shape, F32), jax.ShapeDtypeStruct(q.shape, F32)],
        compiler_params=_params())(q, k, v)


def _attn_dense_bwd(name, q, k, v, o, lse, dmix, cfg):
    T, G, Lc = q.shape[0], cfg['G'], cfg['Lc']

    def body(q_ref, k_ref, v_ref, o_ref, lse_ref, do_ref, dq_ref, dk_ref, dv_ref):
        i = pl.program_id(1)
        kk, vv = k_ref[...], v_ref[...]
        bias = _dense_bias(i, T, Lc)

        @pl.when(i == 0)
        def _():
            dk_ref[...] = jnp.zeros_like(dk_ref)
            dv_ref[...] = jnp.zeros_like(dv_ref)

        for g in range(G):
            sl = slice(g * HEAD_DIM, (g + 1) * HEAD_DIM)
            qg, do = q_ref[:, sl], do_ref[:, sl]
            delta = jnp.sum(do * o_ref[:, sl], axis=1, keepdims=True)
            p = jnp.exp(_nt(qg, kk) * ATT_SCALE + bias - lse_ref[:, g * HEAD_DIM:g * HEAD_DIM + 1])
            dob = _bf(do)
            dv_ref[...] += _tn(_bf(p), dob)
            ds = _bf(p * (_nt(dob, vv) - delta) * ATT_SCALE)
            dq_ref[:, sl] = _nn(ds, kk)
            dk_ref[...] += _tn(ds, qg)

    qs, kvs = _att_specs(T, G)
    return pl.pallas_call(
        body, name=name, grid=(cfg['NKV'], T // ROW_TILE), in_specs=[qs, kvs, kvs, qs, qs, qs], out_specs=[qs, kvs, kvs],
        out_shape=[jax.ShapeDtypeStruct(q.shape, F32), jax.ShapeDtypeStruct(k.shape, F32), jax.ShapeDtypeStruct(k.shape, F32)],
        compiler_params=_params())(q, k, v, o, lse, dmix)


def _band(i, T, Lc):
    start = pl.multiple_of(jnp.clip(WINDOW + (i - 1) * ROW_TILE, 0, T - BAND), WINDOW)
    qpos = (i - 1) * ROW_TILE + lax.broadcasted_iota(jnp.int32, (ROW_TILE, 1), 0)
    kpos = start - Lc + lax.broadcasted_iota(jnp.int32, (1, BAND), 1)
    ok = (jnp.abs(kpos - qpos) <= WINDOW) & (kpos >= 0) & (i > 0)
    return start, jnp.where(ok, 0.0, NEG_INF).astype(F32)


def _attn_win_fwd(name, q, k, v, sink, cfg):
    T, G, Lc = q.shape[0], cfg['G'], cfg['Lc']

    def body(sink_ref, q_ref, k_ref, v_ref, o_ref, lse_ref):
        h, i = pl.program_id(0), pl.program_id(1)
        start, bias = _band(i, T, Lc)
        kc, vc = k_ref[0:Lc, :], v_ref[0:Lc, :]
        kb, vb = k_ref[pl.ds(start, BAND), :], v_ref[pl.ds(start, BAND), :]
        for g in range(G):
            sl = slice(g * HEAD_DIM, (g + 1) * HEAD_DIM)
            qg = q_ref[:, sl]
            sk = sink_ref[h * G + g]
            sc = _nt(qg, kc) * ATT_SCALE
            sb = _nt(qg, kb) * ATT_SCALE + bias
            m = jnp.maximum(jnp.maximum(jnp.max(sc, axis=1, keepdims=True), jnp.max(sb, axis=1, keepdims=True)), sk)
            pc, pb = jnp.exp(sc - m), jnp.exp(sb - m)
            l = jnp.sum(pc, axis=1, keepdims=True) + jnp.sum(pb, axis=1, keepdims=True) + jnp.exp(sk - m)
            o_ref[:, sl] = (_nn(_bf(pc), vc) + _nn(_bf(pb), vb)) / l
            lse_ref[:, sl] = jnp.broadcast_to(m + jnp.log(l), (ROW_TILE, HEAD_DIM))

    qs, kvs = _att_specs(T, G)
    return pl.pallas_call(
        body, name=name, grid=(cfg['NKV'], T // ROW_TILE),
        in_specs=[pl.BlockSpec(memory_space=pltpu.SMEM), qs, kvs, kvs], out_specs=[qs, qs],
        out_shape=[jax.ShapeDtypeStruct(q.shape, F32), jax.ShapeDtypeStruct(q.shape, F32)],
        compiler_params=_params())(sink, q, k, v)


def _attn_win_bwd(name, q, k, v, o, lse, dmix, sink, cfg):
    T, G, Lc = q.shape[0], cfg['G'], cfg['Lc']

    def body(sink_ref, q_ref, k_ref, v_ref, o_ref, lse_ref, do_ref, dq_ref, dk_ref, dv_ref, dsink_ref):
        h, i = pl.program_id(0), pl.program_id(1)
        start, bias = _band(i, T, Lc)
        kc, vc = k_ref[0:Lc, :], v_ref[0:Lc, :]
        kb, vb = k_ref[pl.ds(start, BAND), :], v_ref[pl.ds(start, BAND), :]

        @pl.when(i == 0)
        def _():
            dk_ref[...] = jnp.zeros_like(dk_ref)
            dv_ref[...] = jnp.zeros_like(dv_ref)
            dsink_ref[...] = jnp.zeros_like(dsink_ref)

        for g in range(G):
            sl = slice(g * HEAD_DIM, (g + 1) * HEAD_DIM)
            qg, do = q_ref[:, sl], do_ref[:, sl]
            lse = lse_ref[:, g * HEAD_DIM:g * HEAD_DIM + 1]
            delta = jnp.sum(do * o_ref[:, sl], axis=1, keepdims=True)
            pc = jnp.exp(_nt(qg, kc) * ATT_SCALE - lse)
            pb = jnp.exp(_nt(qg, kb) * ATT_SCALE + bias - lse)
            ps = jnp.exp(sink_ref[h * G + g] - lse)
            dob = _bf(do)
            dv_ref[0:Lc, :] += _tn(_bf(pc), dob)
            dv_ref[pl.ds(start, BAND), :] += _tn(_bf(pb), dob)
            dsc = _bf(pc * (_nt(dob, vc) - delta) * ATT_SCALE)
            dsb = _bf(pb * (_nt(dob, vb) - delta) * ATT_SCALE)
            dq_ref[:, sl] = _nn(dsc, kc) + _nn(dsb, kb)
            dk_ref[0:Lc, :] += _tn(dsc, qg)
            dk_ref[pl.ds(start, BAND), :] += _tn(dsb, qg)
            dsk = jnp.where(i > 0, -jnp.sum(ps * delta, axis=0, keepdims=True), 0.0)
            dsink_ref[:, sl] += jnp.broadcast_to(dsk, (8, HEAD_DIM))

    qs, kvs = _att_specs(T, G)
    return pl.pallas_call(
        body, name=name, grid=(cfg['NKV'], T // ROW_TILE),
        in_specs=[pl.BlockSpec(memory_space=pltpu.SMEM), qs, kvs, kvs, qs, qs, qs],
        out_specs=[qs, kvs, kvs, pl.BlockSpec((None, 8, G * HEAD_DIM), lambda h, i: (h, 0, 0))],
        out_shape=[jax.ShapeDtypeStruct(q.shape, F32), jax.ShapeDtypeStruct(k.shape, F32), jax.ShapeDtypeStruct(k.shape, F32),
                   jax.ShapeDtypeStruct((cfg['NKV'], 8, G * HEAD_DIM), F32)],
        compiler_params=_params())(sink, q, k, v, o, lse, dmix)


def _seq_pos(T, Lc):
    row = lax.broadcasted_iota(jnp.int32, (T, 1), 0)
    return jnp.where(row < Lc, row, row - Lc), jnp.where(row < Lc, Lc, T - Lc)


def _fw(x, k, pos, seglen):
    return jnp.where(pos + k < seglen, pltpu.roll(x, x.shape[0] - k, 0), 0.0)


def _bw(x, k, pos):
    return jnp.where(pos - k >= 0, pltpu.roll(x, k, 0), 0.0)


def _conv_fwd(name, P, conv_w8, cfg):
    T, Lc = P.shape[0], cfg['Lc']
    cb = (cfg['ATT'] + 2 * cfg['KVW']) // HEAD_DIM
    na = AUX_WIDTH // HEAD_DIM

    def body(gb_ref, gc_ref, u_ref, w_ref, o_ref):
        pos, seglen = _seq_pos(T, Lc)
        z = gc_ref[...] * u_ref[...]
        conv = w_ref[0:1, :] * _bw(z, 1, pos) + w_ref[1:2, :] * z + w_ref[2:3, :] * _fw(z, 1, pos, seglen)
        o_ref[...] = gb_ref[...] * conv

    col = lambda off: pl.BlockSpec((T, HEAD_DIM), lambda c: (0, cb + off + c))
    return pl.pallas_call(
        body, name=name, grid=(na,),
        in_specs=[col(0), col(na), col(2 * na), pl.BlockSpec((8, HEAD_DIM), lambda c: (0, c))],
        out_specs=pl.BlockSpec((T, HEAD_DIM), lambda c: (0, c)),
        out_shape=jax.ShapeDtypeStruct((T, AUX_WIDTH), F32), compiler_params=_params())(P, P, P, conv_w8)


def _conv_bwd(name, P, conv_w8, dmix, cfg):
    T, Lc = P.shape[0], cfg['Lc']
    cb = (cfg['ATT'] + 2 * cfg['KVW']) // HEAD_DIM
    ob = cfg['ATT'] // HEAD_DIM
    na = AUX_WIDTH // HEAD_DIM

    def body(gb_ref, gc_ref, u_ref, w_ref, do_ref, dgb_ref, dgc_ref, du_ref, dw_ref):
        pos, seglen = _seq_pos(T, Lc)
        gc, u, do = gc_ref[...], u_ref[...], do_ref[...]
        z = gc * u
        zm, zp = _bw(z, 1, pos), _fw(z, 1, pos, seglen)
        w0, w1, w2 = w_ref[0:1, :], w_ref[1:2, :], w_ref[2:3, :]
        dgb_ref[...] = (do * (w0 * zm + w1 * z + w2 * zp)).astype(dgb_ref.dtype)
        dc = do * gb_ref[...]
        dz = w0 * _fw(dc, 1, pos, seglen) + w1 * dc + w2 * _bw(dc, 1, pos)
        dgc_ref[...] = (dz * u).astype(dgc_ref.dtype)
        du_ref[...] = (dz * gc).astype(du_ref.dtype)
        dw_ref[...] = jnp.zeros_like(dw_ref)
        dw_ref[0:1, :] = jnp.sum(dc * zm, axis=0, keepdims=True)
        dw_ref[1:2, :] = jnp.sum(dc * z, axis=0, keepdims=True)
        dw_ref[2:3, :] = jnp.sum(dc * zp, axis=0, keepdims=True)

    col = lambda off: pl.BlockSpec((T, HEAD_DIM), lambda c: (0, cb + off + c))
    wspec = pl.BlockSpec((8, HEAD_DIM), lambda c: (0, c))
    ocol = lambda off: pl.BlockSpec((T, HEAD_DIM), lambda c: (0, off + c))
    return pl.pallas_call(
        body, name=name, grid=(na,),
        in_specs=[col(0), col(na), col(2 * na), wspec, ocol(ob)],
        out_specs=[ocol(0), ocol(0), ocol(0), wspec],
        out_shape=[jax.ShapeDtypeStruct((T, AUX_WIDTH), BF16)] * 3 + [jax.ShapeDtypeStruct((8, AUX_WIDTH), F32)],
        compiler_params=_params())(P, P, P, conv_w8, dmix)


def _window_sums(x, half, pos, seglen):
    fwd, bwd = x, x
    s = 1
    while s < half:
        fwd = fwd + _fw(fwd, s, pos, seglen)
        bwd = bwd + _bw(bwd, s, pos)
        s *= 2
    return fwd, bwd


def _pooled(u, half, pos, seglen):
    fwd, bwd = _window_sums(u, half, pos, seglen)
    cnt = (jnp.minimum(pos + half, seglen) - jnp.maximum(pos - half, 0)).astype(F32)
    return (fwd + _bw(bwd, 1, pos)) / cnt - u, cnt


def _pool_fwd(name, P, pool_w, pool_scale, cfg):
    T, Lc = P.shape[0], cfg['Lc']
    cb = (cfg['ATT'] + 2 * cfg['KVW']) // HEAD_DIM

    def body(u_ref, w_ref, s_ref, o_ref):
        g = pl.program_id(0)
        pos, seglen = _seq_pos(T, Lc)
        for k, half in enumerate(POOL_HALF):
            @pl.when(g == k)
            def _(half=half):
                pooled, _ = _pooled(u_ref[...], half, pos, seglen)
                o_ref[...] = _nn(_bf(pooled), _bf(w_ref[...])) * s_ref[...]

    return pl.pallas_call(
        body, name=name, grid=(AUX_GROUPS,),
        in_specs=[pl.BlockSpec((T, HEAD_DIM), lambda g: (0, cb + g)), pl.BlockSpec((None, HEAD_DIM, HEAD_DIM), lambda g: (g, 0, 0)),
                  pl.BlockSpec((1, HEAD_DIM), lambda g: (0, g))],
        out_specs=pl.BlockSpec((T, HEAD_DIM), lambda g: (0, g)),
        out_shape=jax.ShapeDtypeStruct((T, AUX_WIDTH), F32), compiler_params=_params())(P, pool_w, pool_scale)


def _pool_bwd(name, P, pool_w, pool_scale, dmix, cfg):
    T, Lc = P.shape[0], cfg['Lc']
    cb = (cfg['ATT'] + 2 * cfg['KVW']) // HEAD_DIM
    ob = cfg['ATT'] // HEAD_DIM

    def body(u_ref, w_ref, s_ref, do_ref, du_ref, dw_ref, ds_ref):
        g = pl.program_id(0)
        pos, seglen = _seq_pos(T, Lc)
        for k, half in enumerate(POOL_HALF):
            @pl.when(g == k)
            def _(half=half):
                do = do_ref[...]
                pooled, cnt = _pooled(u_ref[...], half, pos, seglen)
                wb = _bf(w_ref[...])
                mixed = _nn(_bf(pooled), wb)
                ds_ref[...] = jnp.broadcast_to(jnp.sum(do * mixed, axis=0, keepdims=True), ds_ref.shape)
                dmixed = _bf(do * s_ref[...])
                dw_ref[...] = _tn(_bf(pooled), dmixed)
                dpooled = _nt(dmixed, wb)
                e = dpooled / cnt
                fwd, bwd = _window_sums(e, half, pos, seglen)
                adj = fwd + _fw(e, half, pos, seglen) + _bw(bwd, 1, pos) - _bw(e, half, pos)
                du_ref[...] = (adj - dpooled).astype(du_ref.dtype)

    wspec = pl.BlockSpec((None, HEAD_DIM, HEAD_DIM), lambda g: (g, 0, 0))
    return pl.pallas_call(
        body, name=name, grid=(AUX_GROUPS,),
        in_specs=[pl.BlockSpec((T, HEAD_DIM), lambda g: (0, cb + g)), wspec, pl.BlockSpec((1, HEAD_DIM), lambda g: (0, g)),
                  pl.BlockSpec((T, HEAD_DIM), lambda g: (0, ob + g))],
        out_specs=[pl.BlockSpec((T, HEAD_DIM), lambda g: (0, g)), wspec, pl.BlockSpec((8, HEAD_DIM), lambda g: (0, g))],
        out_shape=[jax.ShapeDtypeStruct((T, AUX_WIDTH), BF16), jax.ShapeDtypeStruct(pool_w.shape, F32),
                   jax.ShapeDtypeStruct((8, AUX_WIDTH), F32)],
        compiler_params=_params())(P, pool_w, pool_scale, dmix)


def _ffn_up(name, hn, wg3, wu3):
    T, D = hn.shape
    J, _, k = wg3.shape
    tm = _row_tile(T, 1088)

    def body(x_ref, wg_ref, wu_ref, g_ref, u_ref, a_ref):
        x = x_ref[...]
        g, u = _nn(x, wg_ref[...]), _nn(x, wu_ref[...])
        g_ref[...] = g
        u_ref[...] = u
        a_ref[...] = (g * jax.nn.sigmoid(g) * u).astype(a_ref.dtype)

    wspec = pl.BlockSpec((None, D, k), lambda j, i: (j, 0, 0))
    ospec = pl.BlockSpec((None, tm, k), lambda j, i: (j, i, 0))
    return pl.pallas_call(
        body, name=name, grid=(J, T // tm), in_specs=[pl.BlockSpec((tm, D), lambda j, i: (i, 0)), wspec, wspec],
        out_specs=[ospec, ospec, ospec],
        out_shape=[jax.ShapeDtypeStruct((J, T, k), F32), jax.ShapeDtypeStruct((J, T, k), F32), jax.ShapeDtypeStruct((J, T, k), BF16)],
        compiler_params=_params())(hn, wg3, wu3)


def _ffn_dact(name, dF, wd3, G, U):
    T, D = dF.shape
    J, k, _ = wd3.shape
    tm = _row_tile(T, 1088)

    def body(df_ref, wd_ref, g_ref, u_ref, dg_ref, du_ref):
        da = _nt(df_ref[...], wd_ref[...])
        g = g_ref[...]
        sig = jax.nn.sigmoid(g)
        du_ref[...] = (da * g * sig).astype(du_ref.dtype)
        dg_ref[...] = (da * u_ref[...] * (sig * (1 + g * (1 - sig)))).astype(dg_ref.dtype)

    aspec = pl.BlockSpec((None, tm, k), lambda j, i: (j, i, 0))
    return pl.pallas_call(
        body, name=name, grid=(J, T // tm),
        in_specs=[pl.BlockSpec((tm, D), lambda j, i: (i, 0)), pl.BlockSpec((None, k, D), lambda j, i: (j, 0, 0)), aspec, aspec],
        out_specs=[aspec, aspec],
        out_shape=[jax.ShapeDtypeStruct((J, T, k), BF16), jax.ShapeDtypeStruct((J, T, k), BF16)],
        compiler_params=_params())(dF, wd3, G, U)


def _loss_head(name, h, g, target, cfg):
    T, D = h.shape

    def body(h_ref, g_ref, t_ref, dh_ref, loss_ref, dg_ref):
        i = pl.program_id(0)

        @pl.when(i == 0)
        def _():
            dh_ref[...] = jnp.zeros_like(dh_ref)
            loss_ref[...] = jnp.zeros_like(loss_ref)
            dg_ref[...] = jnp.zeros_like(dg_ref)

        @pl.when(i > 0)
        def _():
            x = h_ref[...]
            r = _rstd(x)
            xhat = x * r
            gg = g_ref[...]
            err = xhat * gg - t_ref[...]
            loss_ref[...] += 0.5 * jnp.sum(jnp.sum(err * err, axis=1, keepdims=True) / D, axis=0, keepdims=True)
            dy = err / D
            dg_ref[0:1, :] += jnp.sum(dy * xhat, axis=0, keepdims=True)
            dxh = dy * gg
            dh_ref[...] = r * (dxh - xhat * jnp.mean(dxh * xhat, axis=-1, keepdims=True))

    row = pl.BlockSpec((ROW_TILE, D), lambda i: (i, 0))
    return pl.pallas_call(
        body, name=name, grid=(T // ROW_TILE,),
        in_specs=[row, pl.BlockSpec((1, D), lambda i: (0, 0)), pl.BlockSpec((ROW_TILE, D), lambda i: (jnp.maximum(i - 1, 0), 0))],
        out_specs=[row, pl.BlockSpec((8, 128), lambda i: (0, 0)), pl.BlockSpec((8, D), lambda i: (0, 0))],
        out_shape=[jax.ShapeDtypeStruct((T, D), F32), jax.ShapeDtypeStruct((8, 128), F32), jax.ShapeDtypeStruct((8, D), F32)],
        compiler_params=_params())(h, g, target)


def _adamw(name, parts, w, m, v):
    R, C = w.shape
    n_parts = parts.shape[0]
    tr = _row_tile(R, max(16, (1 << 18) // C)) if R % 16 == 0 else R
    bc1 = 1.0 - ADAM_B1 ** ADAM_STEP
    bc2 = 1.0 - ADAM_B2 ** ADAM_STEP

    def body(p_ref, w_ref, m_ref, v_ref, g_ref, d_ref, nm_ref, nv_ref):
        g = p_ref[0].astype(F32)
        for k in range(1, n_parts):
            g = g + p_ref[k].astype(F32)
        nm = ADAM_B1 * m_ref[...] + (1.0 - ADAM_B1) * g
        nv = ADAM_B2 * v_ref[...] + (1.0 - ADAM_B2) * (g * g)
        g_ref[...] = g
        nm_ref[...] = nm
        nv_ref[...] = nv
        d_ref[...] = -ADAM_LR * ((nm / bc1) / (jnp.sqrt(nv / bc2) + ADAM_EPS) + ADAM_WD * w_ref[...])

    blk = pl.BlockSpec((tr, C), lambda i: (i, 0))
    return pl.pallas_call(
        body, name=name, grid=(R // tr,), in_specs=[pl.BlockSpec((n_parts, tr, C), lambda i: (0, i, 0)), blk, blk, blk],
        out_specs=[blk] * 4, out_shape=[jax.ShapeDtypeStruct((R, C), F32)] * 4, compiler_params=_params())(parts, w, m, v)


def _layer_fwd(l, h, p, W, mod, rope, conv_w8, cfg):
    nm = f"l{l}_"
    xn = _norm_mod(nm + "norm1", h, p['norm1_g'], mod, 0)
    P = _mm_cols(nm + "w_in", xn, W['w_in'])
    qr, kr, vb = _qk_prep(nm + "qk_prep", P, p['q_norm_g'], p['k_norm_g'], rope[0], rope[1], cfg)
    if l == 0:
        o, lse = _attn_dense_fwd(nm + "attn", qr, kr, vb, cfg)
        aux = _conv_fwd(nm + "conv", P, conv_w8, cfg)
    else:
        o, lse = _attn_win_fwd(nm + "attn", qr, kr, vb, p['sink'], cfg)
        aux = _pool_fwd(nm + "pool", P, p['pool_w'], p['pool_scale'], cfg)
    mix = jnp.concatenate([o, aux], axis=1).astype(BF16)
    y = _mm_plain(nm + "w_out", mix, W['w_out'], False)
    h2 = _gate_res(nm + "res1", h, y, mod, 2)
    hn = _norm_mod(nm + "norm2", h2, p['norm2_g'], mod, 1)
    G, U, A = _ffn_up(nm + "ffn_up", hn, W['w_gate'], W['w_up'])
    F = _mm_shards_nn(nm + "w_down", A, W['w_down'])
    h3 = _gate_res(nm + "res2", h2, F, mod, 5)
    saved = dict(h=h, xn=xn, P=P, qr=qr, kr=kr, vb=vb, o=o, lse=lse, mix=mix, y=y, h2=h2, hn=hn, G=G, U=U, A=A, F=F)
    return h3, saved


def _layer_bwd(l, dh3, s, p, W, mod, rope, conv_w8, cfg):
    nm = f"l{l}_bwd_"
    J = N_DEV
    dF, dmod = _gate_bwd(nm + "res2", dh3, s['F'], mod, 5)
    dG, dU = _ffn_dact(nm + "ffn_act", dF, W['w_down'], s['G'], s['U'])
    big = {'w_down': _wgrad_down(nm + "dw_down", s['A'], dF),
           'w_gate': _wgrad_up(nm + "dw_gate", s['hn'], dG),
           'w_up': _wgrad_up(nm + "dw_up", s['hn'], dU)}
    dhn = _mm_shards_nt2(nm + "dhn", dG, W['w_gate'], dU, W['w_up'])
    dh2, dm, dg2 = _norm_mod_bwd(nm + "norm2", dhn, s['h2'], p['norm2_g'], mod, 1, dh3)
    dmod += dm
    dY, dm = _gate_bwd(nm + "res1", dh2, s['y'], mod, 2)
    dmod += dm
    dwo = _wgrad_rows(nm + "dw_out", s['mix'], dY)
    big['w_out'] = dwo.reshape((J, dwo.shape[0] // J, dwo.shape[1]))
    dmix = _mm_plain(nm + "dmix", dY, W['w_out'], True)
    small = {'norm2_g': dg2[0]}
    if l == 0:
        dqr, dkr, dv = _attn_dense_bwd(nm + "attn", s['qr'], s['kr'], s['vb'], s['o'], s['lse'], dmix, cfg)
        *daux, dcw = _conv_bwd(nm + "conv", s['P'], conv_w8, dmix, cfg)
        small['conv_w'] = dcw[0:3]
    else:
        dqr, dkr, dv, dsk = _attn_win_bwd(nm + "attn", s['qr'], s['kr'], s['vb'], s['o'], s['lse'], dmix, p['sink'], cfg)
        du, dpw, dps = _pool_bwd(nm + "pool", s['P'], p['pool_w'], p['pool_scale'], dmix, cfg)
        daux = [du]
        small.update(sink=dsk[:, 0, ::HEAD_DIM].reshape(-1), pool_w=dpw, pool_scale=dps[0])
    dq, dk, dqg, dkg = _qk_prep_bwd(nm + "qk_prep", dqr, dkr, s['P'], p['q_norm_g'], p['k_norm_g'], rope[0], rope[1], cfg)
    small.update(q_norm_g=dqg[0], k_norm_g=dkg[0])
    dP = jnp.concatenate([dq, dk, dv.astype(BF16), *daux], axis=1)
    big['w_in'] = _wgrad_cols(nm + "dw_in", s['xn'], dP, J)
    dxn = _mm_cols_nt(nm + "dxn", dP, W['w_in'])
    dh, dm, dg1 = _norm_mod_bwd(nm + "norm1", dxn, s['h'], p['norm1_g'], mod, 0, dh2)
    dmod += dm
    small['norm1_g'] = dg1[0]
    return dh, big, dmod, small


def _rope_tables(S, Lc):
    half = HEAD_DIM // 4
    pos = np.arange(S)
    inv = ROPE_THETA ** (-np.arange(0, 2 * half, 2, dtype=np.float32) / (2 * half))
    inv = jnp.asarray(inv, F32)
    ang_r = jnp.asarray(pos // GRID_W, F32)[:, None] * inv
    ang_c = jnp.asarray(pos % GRID_W, F32)[:, None] * inv
    cos = jnp.concatenate([jnp.cos(ang_r)] * 2 + [jnp.cos(ang_c)] * 2, axis=1)
    sin = jnp.concatenate([-jnp.sin(ang_r), jnp.sin(ang_r), -jnp.sin(ang_c), jnp.sin(ang_c)], axis=1)
    return (jnp.concatenate([jnp.ones((Lc, HEAD_DIM), F32), cos], axis=0),
            jnp.concatenate([jnp.zeros((Lc, HEAD_DIM), F32), sin], axis=0))


def _pad_rows(a, rows):
    return jnp.concatenate([a, jnp.zeros((rows - a.shape[0],) + a.shape[1:], a.dtype)], axis=0)


def _flat128(a, nlead):
    lead = a.shape[:nlead]
    f = a.reshape(lead + (-1,))
    pad = (-f.shape[-1]) % 128
    if pad:
        f = jnp.concatenate([f, jnp.zeros(lead + (pad,), f.dtype)], axis=-1)
    return f.reshape(lead + (-1, 128))


def _pack(named, nlead=0):
    rows, layout, at = [], {}, 0
    for name, a in named:
        f = _flat128(a, nlead)
        layout[name] = (at, f.shape[-2], a.shape[nlead:])
        rows.append(f)
        at += f.shape[-2]
    pad = (-at) % 8
    if pad:
        rows.append(jnp.zeros(rows[0].shape[:nlead] + (pad, 128), rows[0].dtype))
    return jnp.concatenate(rows, axis=-2), layout


def _unpack(arr, layout, name):
    at, n, shape = layout[name]
    return arr[..., at:at + n, :].reshape(arr.shape[:-2] + (-1,))[..., :math.prod(shape)].reshape(arr.shape[:-2] + tuple(shape))


def kernel(*args):
    A = dict(zip(INPUT_NAMES, args, strict=True))
    x, ctx = A['x'][0], A['ctx'][0]
    S, D = x.shape
    Lc = ctx.shape[0]
    T = Lc + S
    ATT = D - AUX_WIDTH
    KVW = (A['l1_w_in'].shape[1] * N_DEV - ATT - AUX_WIDTH) // 2
    cfg = dict(ATT=ATT, KVW=KVW, NKV=KVW // HEAD_DIM, G=ATT // KVW, Lc=Lc)
    assert Lc == ROW_TILE and S % ROW_TILE == 0 and T >= BAND and S % GRID_W == 0
    cw = A['l0_conv_w'].shape[1]
    me = 4 * lax.axis_index("x") + 2 * lax.axis_index("y") + lax.axis_index("c")

    def layer_params(l):
        pre = f"l{l}_"
        return {k[len(pre):]: (v.reshape(1, -1) if v.ndim == 1 and k != 'l1_sink' else v) for k, v in A.items() if k.startswith(pre)}

    params = [layer_params(0), layer_params(1)]

    sc_own = jax.nn.silu(A['c'])
    first, lay0 = _pack([('sc', sc_own), ('conv_w', A['l0_conv_w'])])
    first_all = _exchange("gather_cond", [first], False)[0]
    sc_all = _unpack(first_all, lay0, 'sc')[:, 0]
    conv_w = _unpack(first_all, lay0, 'conv_w').transpose(1, 0, 2).reshape(3, N_DEV * cw)
    conv_w8 = _pad_rows(conv_w, 8)
    sc_ctx = jax.nn.silu(A['c_ctx'])
    s16 = _pad_rows(jnp.concatenate([sc_all, sc_ctx[None]], axis=0), 16)

    nmod = A['l0_w_mod'].shape[1]
    modp = jnp.concatenate([_mm_plain(f"l{l}_mod", s16, A[f'l{l}_w_mod'], False) for l in range(2)], axis=1)
    modp_all = _exchange("gather_mod", [modp], False)[0]
    mods = []
    for l in range(2):
        full = modp_all[:, :, l * nmod:(l + 1) * nmod].transpose(1, 0, 2).reshape(16, N_MOD * D) + A[f'l{l}_b_mod'][None]
        both = jnp.stack([full[8], lax.dynamic_index_in_dim(full, me, 0, keepdims=False)]).reshape(2, N_MOD, D)
        mods.append(jnp.concatenate([both, jnp.zeros((2, 8 - N_MOD, D), F32)], axis=1))

    W = []
    for l in range(2):
        got = _exchange(f"gather_w{l}", [A[f'l{l}_{n}'].astype(BF16) for n in BIG_WEIGHTS], False)
        Wl = dict(zip(BIG_WEIGHTS, got))
        Wl['w_out'] = Wl['w_out'].reshape(D, D)
        W.append(Wl)

    rope = _rope_tables(S, Lc)
    h = jnp.concatenate([ctx, x], axis=0)
    saved = []
    for l in range(2):
        h, s = _layer_fwd(l, h, params[l], W[l], mods[l], rope, conv_w8, cfg)
        saved.append(s)

    dh, loss_blk, dgf = _loss_head("loss_head", h, A['final_norm_g'].reshape(1, -1), A['loss_target'][0], cfg)
    loss = lax.psum(loss_blk[0, 0], ("x", "y", "c"))

    grads, small, dmods = {}, {'final_norm_g': dgf[0]}, [None, None]
    for l in (1, 0):
        dh, big, dmods[l], sm = _layer_bwd(l, dh, saved[l], params[l], W[l], mods[l], rope, conv_w8, cfg)
        landed = _exchange(f"scatter_dw{l}", [big[n] for n in BIG_WEIGHTS], True)
        for n, parts in zip(BIG_WEIGHTS, landed):
            w = A[f'l{l}_{n}']
            grads[f'l{l}_{n}'] = (parts.reshape((N_DEV,) + w.shape), None)
        small.update({f'l{l}_{k}': v for k, v in sm.items()})
    grad_x = dh[Lc:][None]

    small_names = [n for n in WEIGHT_NAMES if n in small]
    pieces = [(n, small[n]) for n in small_names]
    for l in range(2):
        pieces += [(f'dmod{l}', dmods[l][1, :N_MOD]), (f'dcmod{l}', dmods[l][0, :N_MOD])]
    second, lay1 = _pack(pieces)
    second_all = _exchange("gather_small", [second], False)[0]

    dsc_part = jnp.zeros((16, D), F32)
    for l in range(2):
        dm16 = _pad_rows(jnp.concatenate([_unpack(second_all, lay1, f'dmod{l}').reshape(N_DEV, N_MOD * D),
                                          jnp.sum(_unpack(second_all, lay1, f'dcmod{l}'), axis=0).reshape(1, N_MOD * D)], axis=0), 16)
        mine = lax.dynamic_slice_in_dim(dm16, me * nmod, nmod, axis=1)
        tk = _col_tile(D, 512)
        gw = _mm_tn(f"l{l}_dw_mod", s16, mine, pl.BlockSpec((16, tk), lambda kb, t: (0, kb)), pl.BlockSpec((16, nmod), lambda kb, t: (0, 0)),
                    jax.ShapeDtypeStruct((D, nmod), F32), pl.BlockSpec((tk, nmod), lambda kb, t: (kb, 0)), (tk, nmod), (D // tk, 1))
        grads[f'l{l}_w_mod'] = (gw[None], None)
        dsc_part += _mm_plain(f"l{l}_dsc", mine, A[f'l{l}_w_mod'], True)
        dmod_dev = _unpack(second_all, lay1, f'dmod{l}') + _unpack(second_all, lay1, f'dcmod{l}')
        grads[f'l{l}_b_mod'] = (dmod_dev.reshape(N_DEV, N_MOD * D), None)
    dsig = jax.nn.sigmoid(A['c_ctx'])
    dsilu = dsig * (1 + A['c_ctx'] * (1 - dsig))
    third_all = _exchange("gather_dsc", [dsc_part[8:9]], False)[0]
    grads['c_ctx'] = (third_all[:, 0] * dsilu[None], None)
    for n in small_names:
        g8 = _unpack(second_all, lay1, n)
        if n == 'l0_conv_w':
            g8 = lax.dynamic_slice_in_dim(g8, me * cw, cw, axis=2)
        grads[n] = (g8, None)

    out = {}
    big_names = [n for n in WEIGHT_NAMES if n[3:] in BIG_WEIGHTS + ('w_mod',)]
    for n in big_names:
        parts = grads[n][0]
        res = _adamw("adamw_" + n, parts, A[n], A['m_' + n], A['v_' + n])
        out[n] = res
    rest = [n for n in WEIGHT_NAMES if n not in big_names]
    wp, layw = _pack([(n, A[n]) for n in rest])
    mp, _ = _pack([(n, A['m_' + n]) for n in rest])
    vp, _ = _pack([(n, A['v_' + n]) for n in rest])
    gp, _ = _pack([(n, grads[n][0]) for n in rest], nlead=1)
    res = _adamw("adamw_small", gp, wp, mp, vp)
    for n in rest:
        out[n] = tuple(_unpack(r, layw, n) for r in res)

    outs = [loss, grad_x]
    for k in range(4):
        outs += [out[n][k] for n in WEIGHT_NAMES]
    return tuple(outs)
```

```python
import functools
import math

import numpy as np
import jax
import jax.numpy as jnp
from jax import lax
from jax.experimental import pallas as pl
from jax.experimental.pallas import tpu as pltpu

F32 = jnp.float32
BF16 = jnp.bfloat16
HEAD_DIM = 128
AUX_WIDTH = 512
AUX_GROUPS = 4
POOL_HALF = (1, 2, 4, 8)
WINDOW = 128
GRID_W = 64
ROPE_THETA = 10000.0
EPS = 1e-6
NEG_INF = -1e30
ATT_SCALE = HEAD_DIM ** -0.5
N_MOD = 6
N_DEV = 8
ROW_TILE = 256
BAND = ROW_TILE + 2 * WINDOW
ADAM_LR, ADAM_B1, ADAM_B2, ADAM_EPS, ADAM_WD, ADAM_STEP = 0.001, 0.9, 0.999, 1e-08, 0.01, 10
VMEM_LIMIT_MB = 56
MESH = pl.DeviceIdType.MESH

WEIGHT_NAMES = ['c_ctx', 'l0_norm1_g', 'l0_w_mod', 'l0_b_mod', 'l0_w_in', 'l0_q_norm_g', 'l0_k_norm_g', 'l0_conv_w', 'l0_w_out', 'l0_norm2_g', 'l0_w_gate', 'l0_w_up', 'l0_w_down', 'l1_norm1_g', 'l1_w_mod', 'l1_b_mod', 'l1_w_in', 'l1_q_norm_g', 'l1_k_norm_g', 'l1_sink', 'l1_pool_w', 'l1_pool_scale', 'l1_w_out', 'l1_norm2_g', 'l1_w_gate', 'l1_w_up', 'l1_w_down', 'final_norm_g']
INPUT_NAMES = (['x', 'c', 'ctx'] + WEIGHT_NAMES + ['loss_target'] + ['m_' + n for n in WEIGHT_NAMES]
               + ['v_' + n for n in WEIGHT_NAMES])
MIXER_WEIGHTS = ('w_out', 'w_in')
FFN_WEIGHTS = ('w_down', 'w_gate', 'w_up')
BIG_WEIGHTS = MIXER_WEIGHTS + FFN_WEIGHTS


def _params(vmem_mb=VMEM_LIMIT_MB):
    return pltpu.CompilerParams(vmem_limit_bytes=vmem_mb << 20)


def _row_tile(n, cap):
    best = None
    for t in range(16, min(n, cap) + 1, 16):
        if n % t == 0:
            best = t
    assert best is not None, (n, cap)
    return best


def _col_tile(n, cap):
    best = n
    for t in range(128, min(n, cap) + 1, 128):
        if n % t == 0:
            best = t
    return best if best <= cap or n % 128 else n


def _dot(a, b, ca, cb):
    return lax.dot_general(a, b, (((ca,), (cb,)), ((), ())), preferred_element_type=F32)


def _nn(a, b):
    return _dot(a, b, 1, 0)


def _nt(a, b):
    return _dot(a, b, 1, 1)


def _tn(a, b):
    return _dot(a, b, 0, 0)


def _bf(x):
    return x.astype(BF16)


def _exchange(name, arrs, scatter):
    n = len(arrs)
    if scatter:
        out_shape = [jax.ShapeDtypeStruct(a.shape, a.dtype) for a in arrs]
    else:
        out_shape = [jax.ShapeDtypeStruct((N_DEV,) + a.shape, a.dtype) for a in arrs]

    def body(*refs):
        ins, outs = refs[:n], refs[n:2 * n]
        send_sems, recv_sems, local_sems = refs[2 * n:]
        x, y, c = lax.axis_index("x"), lax.axis_index("y"), lax.axis_index("c")
        me = 4 * x + 2 * y + c
        local, remote = [], []
        for a in range(n):
            own = ins[a].at[me] if scatter else ins[a]
            cp = pltpu.make_async_copy(own, outs[a].at[me], local_sems.at[a])
            cp.start()
            local.append(cp)
            for r in range(1, N_DEV):
                px = 1 - x if r & 4 else x
                py = 1 - y if r & 2 else y
                pc = 1 - c if r & 1 else c
                src = ins[a].at[4 * px + 2 * py + pc] if scatter else ins[a]
                cp = pltpu.make_async_remote_copy(
                    src_ref=src, dst_ref=outs[a].at[me], send_sem=send_sems.at[a, r - 1],
                    recv_sem=recv_sems.at[a, r - 1], device_id=(px, py, pc), device_id_type=MESH)
                cp.start()
                remote.append(cp)
        for cp in remote:
            cp.wait()
        for cp in local:
            cp.wait()

    any_spec = pl.BlockSpec(memory_space=pl.ANY)
    return pl.pallas_call(
        body, name=name, out_shape=out_shape,
        in_specs=[any_spec] * n, out_specs=[any_spec] * n,
        scratch_shapes=[pltpu.SemaphoreType.DMA((n, N_DEV - 1)), pltpu.SemaphoreType.DMA((n, N_DEV - 1)),
                        pltpu.SemaphoreType.DMA((n,))],
    )(*arrs)


HBM_SPEC = pl.BlockSpec(memory_space=pltpu.HBM)
SEM_SPEC = pl.BlockSpec(memory_space=pltpu.SEMAPHORE)
EFFECT = pltpu.SideEffectType.DATAFLOW_SIDE_EFFECTING


def _split_copies(srcs, lands, send_sems, recv_sems, local_sems, scatter):
    x, y, c = lax.axis_index("x"), lax.axis_index("y"), lax.axis_index("c")
    me = 4 * x + 2 * y + c
    local, remote = [], []
    for a in range(len(srcs)):
        own = srcs[a].at[me] if scatter else srcs[a]
        local.append(pltpu.make_async_copy(own, lands[a].at[me], local_sems.at[a]))
        for r in range(1, N_DEV):
            px = 1 - x if r & 4 else x
            py = 1 - y if r & 2 else y
            pc = 1 - c if r & 1 else c
            src = srcs[a].at[4 * px + 2 * py + pc] if scatter else srcs[a]
            remote.append(pltpu.make_async_remote_copy(
                src_ref=src, dst_ref=lands[a].at[me], send_sem=send_sems.at[a * (N_DEV - 1) + r - 1],
                recv_sem=recv_sems.at[a * (N_DEV - 1) + r - 1], device_id=(px, py, pc), device_id_type=MESH))
    return local, remote


def _exchange_start(name, arrs, scatter):
    n = len(arrs)
    shapes = [a.shape if scatter else (N_DEV,) + a.shape for a in arrs]
    lands = [pltpu.with_memory_space_constraint(lax.empty(s, a.dtype), pltpu.HBM) for s, a in zip(shapes, arrs)]
    srcs = [pltpu.with_memory_space_constraint(a, pltpu.HBM) for a in arrs]

    def body(*refs):
        src_refs, land_refs = refs[:n], refs[n:2 * n]
        send_sems, recv_sems, local_sems = refs[2 * n:2 * n + 3]
        token = refs[-1]
        local, remote = _split_copies(src_refs, land_refs, send_sems, recv_sems, local_sems, scatter)
        for cp in local + remote:
            cp.start()
        token[...] = jnp.zeros_like(token)

    res = pl.pallas_call(
        body, name=name,
        out_shape=[pltpu.SemaphoreType.DMA((n * (N_DEV - 1),)), pltpu.SemaphoreType.DMA((n * (N_DEV - 1),)), pltpu.SemaphoreType.DMA((n,))]
        + [pltpu.HBM(a.shape, a.dtype) for a in arrs] + [pltpu.HBM(s, a.dtype) for s, a in zip(shapes, arrs)]
        + [jax.ShapeDtypeStruct((8, 128), F32)],
        in_specs=[HBM_SPEC] * (2 * n), out_specs=[SEM_SPEC] * 3 + [HBM_SPEC] * (2 * n) + [pl.BlockSpec(memory_space=pltpu.VMEM)],
        input_output_aliases={i: 3 + i for i in range(2 * n)},
        compiler_params=pltpu.CompilerParams(has_side_effects=EFFECT),
    )(*srcs, *lands)
    return (scatter, res[:3], res[3:3 + n], res[3 + n:3 + 2 * n]), res[-1]


def _exchange_wait(name, handle, after):
    scatter, sems, srcs, lands = handle
    n = len(srcs)

    def body(*refs):
        src_refs, land_refs = refs[:n], refs[n:2 * n]
        send_sems, recv_sems, local_sems = refs[2 * n:2 * n + 3]
        local, remote = _split_copies(src_refs, land_refs, send_sems, recv_sems, local_sems, scatter)
        for cp in remote:
            cp.wait_send()
            cp.wait_recv()
        for cp in local:
            cp.wait()

    res = pl.pallas_call(
        body, name=name,
        out_shape=[pltpu.HBM(a.shape, a.dtype) for a in srcs] + [pltpu.HBM(a.shape, a.dtype) for a in lands],
        in_specs=[HBM_SPEC] * (2 * n) + [SEM_SPEC] * 3 + [pl.BlockSpec(memory_space=pl.ANY)], out_specs=[HBM_SPEC] * (2 * n),
        input_output_aliases={i: i for i in range(2 * n)},
        compiler_params=pltpu.CompilerParams(has_side_effects=EFFECT),
    )(*srcs, *lands, *sems, after)
    return list(res[n:])


def _mm_step(name, fn, ins, in_specs, out_shape, out_spec, grid):
    n = len(ins)

    def body(*refs):
        o_ref = refs[n]
        o_ref[...] = fn(*refs[:n]).astype(o_ref.dtype)

    return pl.pallas_call(body, name=name, grid=grid, in_specs=in_specs, out_specs=out_spec,
                          out_shape=out_shape, compiler_params=_params())(*ins)


def _mm_tn(name, a, b, a_spec, b_spec, out_shape, out_spec, acc_shape, grid):
    nk = grid[-1]
    kax = len(grid) - 1

    def body(a_ref, b_ref, o_ref, acc_ref):
        k = pl.program_id(kax)

        @pl.when(k == 0)
        def _():
            acc_ref[...] = jnp.zeros_like(acc_ref)

        acc_ref[...] += _tn(_bf(a_ref[...]), _bf(b_ref[...]))

        @pl.when(k == nk - 1)
        def _():
            o_ref[...] = acc_ref[...].astype(o_ref.dtype)

    return pl.pallas_call(body, name=name, grid=grid, in_specs=[a_spec, b_spec], out_specs=out_spec,
                          out_shape=out_shape, scratch_shapes=[pltpu.VMEM(acc_shape, F32)],
                          compiler_params=_params())(a, b)


def _mm_cols(name, a, w3, out_dtype=F32):
    M, K = a.shape
    J, _, n = w3.shape
    tm = _row_tile(M, 1088)
    return _mm_step(
        name, lambda a_ref, w_ref: _nn(_bf(a_ref[...]), w_ref[...]), [a, w3],
        [pl.BlockSpec((tm, K), lambda j, i: (i, 0)), pl.BlockSpec((None, K, n), lambda j, i: (j, 0, 0))],
        jax.ShapeDtypeStruct((M, J * n), out_dtype), pl.BlockSpec((tm, n), lambda j, i: (i, j)), (J, M // tm))


def _mm_plain(name, a, b, transpose_b, out_dtype=F32, tn=512):
    M, K = a.shape
    N = b.shape[0] if transpose_b else b.shape[1]
    tm = _row_tile(M, 1088)
    tn = _col_tile(N, tn)
    if transpose_b:
        b_spec = pl.BlockSpec((tn, K), lambda j, i: (j, 0))
        fn = lambda a_ref, b_ref: _nt(_bf(a_ref[...]), _bf(b_ref[...]))
    else:
        b_spec = pl.BlockSpec((K, tn), lambda j, i: (0, j))
        fn = lambda a_ref, b_ref: _nn(_bf(a_ref[...]), _bf(b_ref[...]))
    return _mm_step(name, fn, [a, b], [pl.BlockSpec((tm, K), lambda j, i: (i, 0)), b_spec],
                    jax.ShapeDtypeStruct((M, N), out_dtype), pl.BlockSpec((tm, tn), lambda j, i: (i, j)),
                    (N // tn, M // tm))


def _mm_shards_nn(name, a3, w3, tn=512):
    J, M, k = a3.shape
    N = w3.shape[2]
    tm = _row_tile(M, 544)
    tn = _col_tile(N, tn)

    def fn(a_ref, w_ref):
        acc = _nn(a_ref[0], w_ref[0])
        for j in range(1, J):
            acc += _nn(a_ref[j], w_ref[j])
        return acc

    return _mm_step(name, fn, [a3, w3],
                    [pl.BlockSpec((J, tm, k), lambda jn, i: (0, i, 0)), pl.BlockSpec((J, k, tn), lambda jn, i: (0, 0, jn))],
                    jax.ShapeDtypeStruct((M, N), F32), pl.BlockSpec((tm, tn), lambda jn, i: (i, jn)), (N // tn, M // tm))


def _mm_shards_nt2(name, a3, w3a, b3, w3b, tn=512):
    J, M, k = a3.shape
    N = w3a.shape[1]
    tm = _row_tile(M, 544)
    tn = _col_tile(N, tn)

    def fn(a_ref, wa_ref, b_ref, wb_ref):
        acc = _nt(a_ref[0], wa_ref[0]) + _nt(b_ref[0], wb_ref[0])
        for j in range(1, J):
            acc += _nt(a_ref[j], wa_ref[j]) + _nt(b_ref[j], wb_ref[j])
        return acc

    act = pl.BlockSpec((J, tm, k), lambda jn, i: (0, i, 0))
    wsp = pl.BlockSpec((J, tn, k), lambda jn, i: (0, jn, 0))
    return _mm_step(name, fn, [a3, w3a, b3, w3b], [act, wsp, act, wsp],
                    jax.ShapeDtypeStruct((M, N), F32), pl.BlockSpec((tm, tn), lambda jn, i: (i, jn)), (N // tn, M // tm))


def _mm_cols_nt(name, a, w3, tn=512):
    M = a.shape[0]
    J, N, n = w3.shape
    tm = _row_tile(M, 544)
    tn = _col_tile(N, tn)

    def fn(a_ref, w_ref):
        acc = _nt(a_ref[:, 0:n], w_ref[0])
        for j in range(1, J):
            acc += _nt(a_ref[:, j * n:(j + 1) * n], w_ref[j])
        return acc

    return _mm_step(name, fn, [a, w3],
                    [pl.BlockSpec((tm, J * n), lambda jn, i: (i, 0)), pl.BlockSpec((J, tn, n), lambda jn, i: (0, jn, 0))],
                    jax.ShapeDtypeStruct((M, N), F32), pl.BlockSpec((tm, tn), lambda jn, i: (i, jn)), (N // tn, M // tm))


def _wgrad_cols(name, a, b, J):
    T, K = a.shape
    n = b.shape[1] // J
    tt = _row_tile(T, 1088)
    return _mm_tn(name, a, b, pl.BlockSpec((tt, K), lambda j, t: (t, 0)), pl.BlockSpec((tt, n), lambda j, t: (t, j)),
                  jax.ShapeDtypeStruct((J, K, n), BF16), pl.BlockSpec((None, K, n), lambda j, t: (j, 0, 0)), (K, n), (J, T // tt))


def _wgrad_rows(name, a, b, tk=512):
    T, K = a.shape
    N = b.shape[1]
    tt = _row_tile(T, 1088)
    tk = _col_tile(K, tk)
    return _mm_tn(name, a, b, pl.BlockSpec((tt, tk), lambda kb, t: (t, kb)), pl.BlockSpec((tt, N), lambda kb, t: (t, 0)),
                  jax.ShapeDtypeStruct((K, N), BF16), pl.BlockSpec((tk, N), lambda kb, t: (kb, 0)), (tk, N), (K // tk, T // tt))


def _wgrad_up(name, a, b3):
    T, K = a.shape
    J, _, k = b3.shape
    tt = _row_tile(T, 1088)
    return _mm_tn(name, a, b3, pl.BlockSpec((tt, K), lambda j, t: (t, 0)), pl.BlockSpec((None, tt, k), lambda j, t: (j, t, 0)),
                  jax.ShapeDtypeStruct((J, K, k), BF16), pl.BlockSpec((None, K, k), lambda j, t: (j, 0, 0)), (K, k), (J, T // tt))


def _wgrad_down(name, a3, b):
    J, T, k = a3.shape
    N = b.shape[1]
    tt = _row_tile(T, 1088)
    return _mm_tn(name, a3, b, pl.BlockSpec((None, tt, k), lambda j, t: (j, t, 0)), pl.BlockSpec((tt, N), lambda j, t: (t, 0)),
                  jax.ShapeDtypeStruct((J, k, N), BF16), pl.BlockSpec((None, k, N), lambda j, t: (j, 0, 0)), (k, N), (J, T // tt))


def _seg(i):
    return jnp.minimum(i, 1)


def _rstd(x):
    return lax.rsqrt(jnp.mean(x * x, axis=-1, keepdims=True) + EPS)


def _norm_mod(name, h, g, mod, which):
    T, D = h.shape

    def body(h_ref, g_ref, mod_ref, o_ref):
        x = h_ref[...]
        n = x * _rstd(x) * g_ref[...]
        shift = mod_ref[3 * which:3 * which + 1, :]
        scale = mod_ref[3 * which + 1:3 * which + 2, :]
        o_ref[...] = (n * (1 + scale) + shift).astype(o_ref.dtype)

    row = pl.BlockSpec((ROW_TILE, D), lambda i: (i, 0))
    return pl.pallas_call(
        body, name=name, grid=(T // ROW_TILE,),
        in_specs=[row, pl.BlockSpec((1, D), lambda i: (0, 0)), pl.BlockSpec((None, 8, D), lambda i: (_seg(i), 0, 0))],
        out_specs=row, out_shape=jax.ShapeDtypeStruct((T, D), BF16), compiler_params=_params())(h, g, mod)


def _norm_mod_bwd(name, dxn, h, g, mod, which, dres):
    T, D = h.shape

    def body(dxn_ref, h_ref, g_ref, mod_ref, dres_ref, dh_ref, dmod_ref, dg_ref):
        i = pl.program_id(0)
        x = h_ref[...]
        r = _rstd(x)
        xhat = x * r
        g = g_ref[...]
        n = xhat * g
        scale = mod_ref[3 * which + 1:3 * which + 2, :]
        dxn = dxn_ref[...]
        dn = dxn * (1 + scale)
        dxh = dn * g
        dh_ref[...] = dres_ref[...] + r * (dxh - xhat * jnp.mean(dxh * xhat, axis=-1, keepdims=True))

        @pl.when(i <= 1)
        def _():
            dmod_ref[...] = jnp.zeros_like(dmod_ref)

        @pl.when(i == 0)
        def _():
            dg_ref[...] = jnp.zeros_like(dg_ref)

        dmod_ref[3 * which:3 * which + 1, :] += jnp.sum(dxn, axis=0, keepdims=True)
        dmod_ref[3 * which + 1:3 * which + 2, :] += jnp.sum(dxn * n, axis=0, keepdims=True)
        dg_ref[0:1, :] += jnp.sum(dn * xhat, axis=0, keepdims=True)

    row = pl.BlockSpec((ROW_TILE, D), lambda i: (i, 0))
    modspec = pl.BlockSpec((None, 8, D), lambda i: (_seg(i), 0, 0))
    return pl.pallas_call(
        body, name=name, grid=(T // ROW_TILE,),
        in_specs=[row, row, pl.BlockSpec((1, D), lambda i: (0, 0)), modspec, row],
        out_specs=[row, modspec, pl.BlockSpec((8, D), lambda i: (0, 0))],
        out_shape=[jax.ShapeDtypeStruct((T, D), F32), jax.ShapeDtypeStruct((2, 8, D), F32), jax.ShapeDtypeStruct((8, D), F32)],
        compiler_params=_params())(dxn, h, g, mod, dres)


def _gate_res(name, h, y, mod, row_idx):
    T, D = h.shape

    def body(h_ref, y_ref, mod_ref, o_ref):
        o_ref[...] = h_ref[...] + mod_ref[row_idx:row_idx + 1, :] * y_ref[...]

    row = pl.BlockSpec((ROW_TILE, D), lambda i: (i, 0))
    return pl.pallas_call(
        body, name=name, grid=(T // ROW_TILE,),
        in_specs=[row, row, pl.BlockSpec((None, 8, D), lambda i: (_seg(i), 0, 0))],
        out_specs=row, out_shape=jax.ShapeDtypeStruct((T, D), F32), compiler_params=_params())(h, y, mod)


def _gate_bwd(name, dh, y, mod, row_idx):
    T, D = dh.shape

    def body(dh_ref, y_ref, mod_ref, dy_ref, dmod_ref):
        i = pl.program_id(0)
        dh = dh_ref[...]
        dy_ref[...] = (dh * mod_ref[row_idx:row_idx + 1, :]).astype(dy_ref.dtype)

        @pl.when(i <= 1)
        def _():
            dmod_ref[...] = jnp.zeros_like(dmod_ref)

        dmod_ref[row_idx:row_idx + 1, :] += jnp.sum(dh * y_ref[...], axis=0, keepdims=True)

    row = pl.BlockSpec((ROW_TILE, D), lambda i: (i, 0))
    modspec = pl.BlockSpec((None, 8, D), lambda i: (_seg(i), 0, 0))
    return pl.pallas_call(
        body, name=name, grid=(T // ROW_TILE,), in_specs=[row, row, modspec], out_specs=[row, modspec],
        out_shape=[jax.ShapeDtypeStruct((T, D), BF16), jax.ShapeDtypeStruct((2, 8, D), F32)],
        compiler_params=_params())(dh, y, mod)


def _rot(y):
    lane = lax.broadcasted_iota(jnp.int32, y.shape, 1)
    return jnp.where((lane & 32) == 0, pltpu.roll(y, 96, 1), pltpu.roll(y, 32, 1))


def _qk_prep(name, P, q_g, k_g, rope_c, rope_s, cfg):
    T = P.shape[0]
    ATT, KVW = cfg['ATT'], cfg['KVW']

    def body(q_ref, k_ref, v_ref, qg_ref, kg_ref, c_ref, s_ref, qo_ref, ko_ref, vo_ref):
        cc, ss = c_ref[...], s_ref[...]

        def head(x, g):
            y = x * _rstd(x) * g
            return y * cc + _rot(y) * ss

        for hh in range(ATT // HEAD_DIM):
            sl = slice(hh * HEAD_DIM, (hh + 1) * HEAD_DIM)
            qo_ref[:, sl] = head(q_ref[:, sl], qg_ref[...]).astype(qo_ref.dtype)
        for hh in range(KVW // HEAD_DIM):
            sl = slice(hh * HEAD_DIM, (hh + 1) * HEAD_DIM)
            ko_ref[:, sl] = head(k_ref[:, sl], kg_ref[...]).astype(ko_ref.dtype)
        vo_ref[...] = v_ref[...].astype(vo_ref.dtype)

    kb = ATT // KVW
    gain = pl.BlockSpec((1, HEAD_DIM), lambda i: (0, 0))
    tab = pl.BlockSpec((ROW_TILE, HEAD_DIM), lambda i: (i, 0))
    qs = pl.BlockSpec((ROW_TILE, ATT), lambda i: (i, 0))
    ks = pl.BlockSpec((ROW_TILE, KVW), lambda i: (i, 0))
    return pl.pallas_call(
        body, name=name, grid=(T // ROW_TILE,),
        in_specs=[qs, pl.BlockSpec((ROW_TILE, KVW), lambda i: (i, kb)), pl.BlockSpec((ROW_TILE, KVW), lambda i: (i, kb + 1)),
                  gain, gain, tab, tab],
        out_specs=[qs, ks, ks],
        out_shape=[jax.ShapeDtypeStruct((T, ATT), BF16), jax.ShapeDtypeStruct((T, KVW), BF16), jax.ShapeDtypeStruct((T, KVW), BF16)],
        compiler_params=_params())(P, P, P, q_g, k_g, rope_c, rope_s)


def _qk_prep_bwd(name, dqr, dkr, P, q_g, k_g, rope_c, rope_s, cfg):
    T = P.shape[0]
    ATT, KVW = cfg['ATT'], cfg['KVW']

    def body(dq_ref, dk_ref, q_ref, k_ref, qg_ref, kg_ref, c_ref, s_ref, dqo_ref, dko_ref, dqg_ref, dkg_ref):
        i = pl.program_id(0)
        cc, ss = c_ref[...], s_ref[...]

        @pl.when(i == 0)
        def _():
            dqg_ref[...] = jnp.zeros_like(dqg_ref)
            dkg_ref[...] = jnp.zeros_like(dkg_ref)

        def head(x, g, dout):
            dy = dout * cc + _rot(dout * ss)
            r = _rstd(x)
            xhat = x * r
            dxh = dy * g
            dx = r * (dxh - xhat * jnp.mean(dxh * xhat, axis=-1, keepdims=True))
            return dx, jnp.sum(dy * xhat, axis=0, keepdims=True)

        dg = jnp.zeros((1, HEAD_DIM), F32)
        for hh in range(ATT // HEAD_DIM):
            sl = slice(hh * HEAD_DIM, (hh + 1) * HEAD_DIM)
            dx, d = head(q_ref[:, sl], qg_ref[...], dq_ref[:, sl])
            dqo_ref[:, sl] = dx.astype(dqo_ref.dtype)
            dg += d
        dqg_ref[0:1, :] += dg
        dg = jnp.zeros((1, HEAD_DIM), F32)
        for hh in range(KVW // HEAD_DIM):
            sl = slice(hh * HEAD_DIM, (hh + 1) * HEAD_DIM)
            dx, d = head(k_ref[:, sl], kg_ref[...], dk_ref[:, sl])
            dko_ref[:, sl] = dx.astype(dko_ref.dtype)
            dg += d
        dkg_ref[0:1, :] += dg

    kb = ATT // KVW
    gain = pl.BlockSpec((1, HEAD_DIM), lambda i: (0, 0))
    dgain = pl.BlockSpec((8, HEAD_DIM), lambda i: (0, 0))
    tab = pl.BlockSpec((ROW_TILE, HEAD_DIM), lambda i: (i, 0))
    qs = pl.BlockSpec((ROW_TILE, ATT), lambda i: (i, 0))
    ks = pl.BlockSpec((ROW_TILE, KVW), lambda i: (i, 0))
    return pl.pallas_call(
        body, name=name, grid=(T // ROW_TILE,),
        in_specs=[qs, ks, qs, pl.BlockSpec((ROW_TILE, KVW), lambda i: (i, kb)), gain, gain, tab, tab],
        out_specs=[qs, ks, dgain, dgain],
        out_shape=[jax.ShapeDtypeStruct((T, ATT), BF16), jax.ShapeDtypeStruct((T, KVW), BF16),
                   jax.ShapeDtypeStruct((8, HEAD_DIM), F32), jax.ShapeDtypeStruct((8, HEAD_DIM), F32)],
        compiler_params=_params())(dqr, dkr, P, P, q_g, k_g, rope_c, rope_s)


def _att_specs(T, G):
    qs = pl.BlockSpec((ROW_TILE, G * HEAD_DIM), lambda h, i: (i, h))
    kvs = pl.BlockSpec((T, HEAD_DIM), lambda h, i: (0, h))
    return qs, kvs


def _dense_bias(i, T, Lc):
    col = lax.broadcasted_iota(jnp.int32, (1, T), 1)
    return jnp.where(col < jnp.where(i == 0, Lc, T), 0.0, NEG_INF).astype(F32)


def _attn_dense_fwd(name, q, k, v, cfg):
    T, G, Lc = q.shape[0], cfg['G'], cfg['Lc']

    def body(q_ref, k_ref, v_ref, o_ref, lse_ref):
        kk, vv = k_ref[...], v_ref[...]
        bias = _dense_bias(pl.program_id(1), T, Lc)
        for g in range(G):
            sl = slice(g * HEAD_DIM, (g + 1) * HEAD_DIM)
            s = _nt(q_ref[:, sl], kk) * ATT_SCALE + bias
            m = jnp.max(s, axis=1, keepdims=True)
            p = jnp.exp(s - m)
            l = jnp.sum(p, axis=1, keepdims=True)
            o_ref[:, sl] = _nn(_bf(p), vv) / l
            lse_ref[:, sl] = jnp.broadcast_to(m + jnp.log(l), (ROW_TILE, HEAD_DIM))

    qs, kvs = _att_specs(T, G)
    return pl.pallas_call(
        body, name=name, grid=(cfg['NKV'], T // ROW_TILE), in_specs=[qs, kvs, kvs], out_specs=[qs, qs],
        out_shape=[jax.ShapeDtypeStruct(q.shape, F32), jax.ShapeDtypeStruct(q.shape, F32)],
        compiler_params=_params())(q, k, v)


def _attn_dense_bwd(name, q, k, v, o, lse, dmix, cfg):
    T, G, Lc = q.shape[0], cfg['G'], cfg['Lc']

    def body(q_ref, k_ref, v_ref, o_ref, lse_ref, do_ref, dq_ref, dk_ref, dv_ref):
        i = pl.program_id(1)
        kk, vv = k_ref[...], v_ref[...]
        bias = _dense_bias(i, T, Lc)

        @pl.when(i == 0)
        def _():
            dk_ref[...] = jnp.zeros_like(dk_ref)
            dv_ref[...] = jnp.zeros_like(dv_ref)

        for g in range(G):
            sl = slice(g * HEAD_DIM, (g + 1) * HEAD_DIM)
            qg, do = q_ref[:, sl], do_ref[:, sl]
            delta = jnp.sum(do * o_ref[:, sl], axis=1, keepdims=True)
            p = jnp.exp(_nt(qg, kk) * ATT_SCALE + bias - lse_ref[:, g * HEAD_DIM:g * HEAD_DIM + 1])
            dob = _bf(do)
            dv_ref[...] += _tn(_bf(p), dob)
            ds = _bf(p * (_nt(dob, vv) - delta) * ATT_SCALE)
            dq_ref[:, sl] = _nn(ds, kk)
            dk_ref[...] += _tn(ds, qg)

    qs, kvs = _att_specs(T, G)
    return pl.pallas_call(
        body, name=name, grid=(cfg['NKV'], T // ROW_TILE), in_specs=[qs, kvs, kvs, qs, qs, qs], out_specs=[qs, kvs, kvs],
        out_shape=[jax.ShapeDtypeStruct(q.shape, F32), jax.ShapeDtypeStruct(k.shape, F32), jax.ShapeDtypeStruct(k.shape, F32)],
        compiler_params=_params())(q, k, v, o, lse, dmix)


def _band(i, T, Lc):
    start = pl.multiple_of(jnp.clip(WINDOW + (i - 1) * ROW_TILE, 0, T - BAND), WINDOW)
    qpos = (i - 1) * ROW_TILE + lax.broadcasted_iota(jnp.int32, (ROW_TILE, 1), 0)
    kpos = start - Lc + lax.broadcasted_iota(jnp.int32, (1, BAND), 1)
    ok = (jnp.abs(kpos - qpos) <= WINDOW) & (kpos >= 0) & (i > 0)
    return start, jnp.where(ok, 0.0, NEG_INF).astype(F32)


def _attn_win_fwd(name, q, k, v, sink, cfg):
    T, G, Lc = q.shape[0], cfg['G'], cfg['Lc']

    def body(sink_ref, q_ref, k_ref, v_ref, o_ref, lse_ref):
        h, i = pl.program_id(0), pl.program_id(1)
        start, bias = _band(i, T, Lc)
        kc, vc = k_ref[0:Lc, :], v_ref[0:Lc, :]
        kb, vb = k_ref[pl.ds(start, BAND), :], v_ref[pl.ds(start, BAND), :]
        for g in range(G):
            sl = slice(g * HEAD_DIM, (g + 1) * HEAD_DIM)
            qg = q_ref[:, sl]
            sk = sink_ref[h * G + g]
            sc = _nt(qg, kc) * ATT_SCALE
            sb = _nt(qg, kb) * ATT_SCALE + bias
            m = jnp.maximum(jnp.maximum(jnp.max(sc, axis=1, keepdims=True), jnp.max(sb, axis=1, keepdims=True)), sk)
            pc, pb = jnp.exp(sc - m), jnp.exp(sb - m)
            l = jnp.sum(pc, axis=1, keepdims=True) + jnp.sum(pb, axis=1, keepdims=True) + jnp.exp(sk - m)
            o_ref[:, sl] = (_nn(_bf(pc), vc) + _nn(_bf(pb), vb)) / l
            lse_ref[:, sl] = jnp.broadcast_to(m + jnp.log(l), (ROW_TILE, HEAD_DIM))

    qs, kvs = _att_specs(T, G)
    return pl.pallas_call(
        body, name=name, grid=(cfg['NKV'], T // ROW_TILE),
        in_specs=[pl.BlockSpec(memory_space=pltpu.SMEM), qs, kvs, kvs], out_specs=[qs, qs],
        out_shape=[jax.ShapeDtypeStruct(q.shape, F32), jax.ShapeDtypeStruct(q.shape, F32)],
        compiler_params=_params())(sink, q, k, v)


def _attn_win_bwd(name, q, k, v, o, lse, dmix, sink, cfg):
    T, G, Lc = q.shape[0], cfg['G'], cfg['Lc']

    def body(sink_ref, q_ref, k_ref, v_ref, o_ref, lse_ref, do_ref, dq_ref, dk_ref, dv_ref, dsink_ref):
        h, i = pl.program_id(0), pl.program_id(1)
        start, bias = _band(i, T, Lc)
        kc, vc = k_ref[0:Lc, :], v_ref[0:Lc, :]
        kb, vb = k_ref[pl.ds(start, BAND), :], v_ref[pl.ds(start, BAND), :]

        @pl.when(i == 0)
        def _():
            dk_ref[...] = jnp.zeros_like(dk_ref)
            dv_ref[...] = jnp.zeros_like(dv_ref)
            dsink_ref[...] = jnp.zeros_like(dsink_ref)

        for g in range(G):
            sl = slice(g * HEAD_DIM, (g + 1) * HEAD_DIM)
            qg, do = q_ref[:, sl], do_ref[:, sl]
            lse = lse_ref[:, g * HEAD_DIM:g * HEAD_DIM + 1]
            delta = jnp.sum(do * o_ref[:, sl], axis=1, keepdims=True)
            pc = jnp.exp(_nt(qg, kc) * ATT_SCALE - lse)
            pb = jnp.exp(_nt(qg, kb) * ATT_SCALE + bias - lse)
            ps = jnp.exp(sink_ref[h * G + g] - lse)
            dob = _bf(do)
            dv_ref[0:Lc, :] += _tn(_bf(pc), dob)
            dv_ref[pl.ds(start, BAND), :] += _tn(_bf(pb), dob)
            dsc = _bf(pc * (_nt(dob, vc) - delta) * ATT_SCALE)
            dsb = _bf(pb * (_nt(dob, vb) - delta) * ATT_SCALE)
            dq_ref[:, sl] = _nn(dsc, kc) + _nn(dsb, kb)
            dk_ref[0:Lc, :] += _tn(dsc, qg)
            dk_ref[pl.ds(start, BAND), :] += _tn(dsb, qg)
            dsk = jnp.where(i > 0, -jnp.sum(ps * delta, axis=0, keepdims=True), 0.0)
            dsink_ref[:, sl] += jnp.broadcast_to(dsk, (8, HEAD_DIM))

    qs, kvs = _att_specs(T, G)
    return pl.pallas_call(
        body, name=name, grid=(cfg['NKV'], T // ROW_TILE),
        in_specs=[pl.BlockSpec(memory_space=pltpu.SMEM), qs, kvs, kvs, qs, qs, qs],
        out_specs=[qs, kvs, kvs, pl.BlockSpec((None, 8, G * HEAD_DIM), lambda h, i: (h, 0, 0))],
        out_shape=[jax.ShapeDtypeStruct(q.shape, F32), jax.ShapeDtypeStruct(k.shape, F32), jax.ShapeDtypeStruct(k.shape, F32),
                   jax.ShapeDtypeStruct((cfg['NKV'], 8, G * HEAD_DIM), F32)],
        compiler_params=_params())(sink, q, k, v, o, lse, dmix)


def _seq_pos(T, Lc):
    row = lax.broadcasted_iota(jnp.int32, (T, 1), 0)
    return jnp.where(row < Lc, row, row - Lc), jnp.where(row < Lc, Lc, T - Lc)


def _fw(x, k, pos, seglen):
    return jnp.where(pos + k < seglen, pltpu.roll(x, x.shape[0] - k, 0), 0.0)


def _bw(x, k, pos):
    return jnp.where(pos - k >= 0, pltpu.roll(x, k, 0), 0.0)


def _conv_fwd(name, P, conv_w8, cfg):
    T, Lc = P.shape[0], cfg['Lc']
    cb = (cfg['ATT'] + 2 * cfg['KVW']) // HEAD_DIM
    na = AUX_WIDTH // HEAD_DIM

    def body(gb_ref, gc_ref, u_ref, w_ref, o_ref):
        pos, seglen = _seq_pos(T, Lc)
        z = gc_ref[...] * u_ref[...]
        conv = w_ref[0:1, :] * _bw(z, 1, pos) + w_ref[1:2, :] * z + w_ref[2:3, :] * _fw(z, 1, pos, seglen)
        o_ref[...] = gb_ref[...] * conv

    col = lambda off: pl.BlockSpec((T, HEAD_DIM), lambda c: (0, cb + off + c))
    return pl.pallas_call(
        body, name=name, grid=(na,),
        in_specs=[col(0), col(na), col(2 * na), pl.BlockSpec((8, HEAD_DIM), lambda c: (0, c))],
        out_specs=pl.BlockSpec((T, HEAD_DIM), lambda c: (0, c)),
        out_shape=jax.ShapeDtypeStruct((T, AUX_WIDTH), F32), compiler_params=_params())(P, P, P, conv_w8)


def _conv_bwd(name, P, conv_w8, dmix, cfg):
    T, Lc = P.shape[0], cfg['Lc']
    cb = (cfg['ATT'] + 2 * cfg['KVW']) // HEAD_DIM
    ob = cfg['ATT'] // HEAD_DIM
    na = AUX_WIDTH // HEAD_DIM

    def body(gb_ref, gc_ref, u_ref, w_ref, do_ref, dgb_ref, dgc_ref, du_ref, dw_ref):
        pos, seglen = _seq_pos(T, Lc)
        gc, u, do = gc_ref[...], u_ref[...], do_ref[...]
        z = gc * u
        zm, zp = _bw(z, 1, pos), _fw(z, 1, pos, seglen)
        w0, w1, w2 = w_ref[0:1, :], w_ref[1:2, :], w_ref[2:3, :]
        dgb_ref[...] = (do * (w0 * zm + w1 * z + w2 * zp)).astype(dgb_ref.dtype)
        dc = do * gb_ref[...]
        dz = w0 * _fw(dc, 1, pos, seglen) + w1 * dc + w2 * _bw(dc, 1, pos)
        dgc_ref[...] = (dz * u).astype(dgc_ref.dtype)
        du_ref[...] = (dz * gc).astype(du_ref.dtype)
        dw_ref[...] = jnp.zeros_like(dw_ref)
        dw_ref[0:1, :] = jnp.sum(dc * zm, axis=0, keepdims=True)
        dw_ref[1:2, :] = jnp.sum(dc * z, axis=0, keepdims=True)
        dw_ref[2:3, :] = jnp.sum(dc * zp, axis=0, keepdims=True)

    col = lambda off: pl.BlockSpec((T, HEAD_DIM), lambda c: (0, cb + off + c))
    wspec = pl.BlockSpec((8, HEAD_DIM), lambda c: (0, c))
    ocol = lambda off: pl.BlockSpec((T, HEAD_DIM), lambda c: (0, off + c))
    return pl.pallas_call(
        body, name=name, grid=(na,),
        in_specs=[col(0), col(na), col(2 * na), wspec, ocol(ob)],
        out_specs=[ocol(0), ocol(0), ocol(0), wspec],
        out_shape=[jax.ShapeDtypeStruct((T, AUX_WIDTH), BF16)] * 3 + [jax.ShapeDtypeStruct((8, AUX_WIDTH), F32)],
        compiler_params=_params())(P, P, P, conv_w8, dmix)


def _window_sums(x, half, pos, seglen):
    fwd, bwd = x, x
    s = 1
    while s < half:
        fwd = fwd + _fw(fwd, s, pos, seglen)
        bwd = bwd + _bw(bwd, s, pos)
        s *= 2
    return fwd, bwd


def _pooled(u, half, pos, seglen):
    fwd, bwd = _window_sums(u, half, pos, seglen)
    cnt = (jnp.minimum(pos + half, seglen) - jnp.maximum(pos - half, 0)).astype(F32)
    return (fwd + _bw(bwd, 1, pos)) / cnt - u, cnt


def _pool_fwd(name, P, pool_w, pool_scale, cfg):
    T, Lc = P.shape[0], cfg['Lc']
    cb = (cfg['ATT'] + 2 * cfg['KVW']) // HEAD_DIM

    def body(u_ref, w_ref, s_ref, o_ref):
        g = pl.program_id(0)
        pos, seglen = _seq_pos(T, Lc)
        for k, half in enumerate(POOL_HALF):
            @pl.when(g == k)
            def _(half=half):
                pooled, _ = _pooled(u_ref[...], half, pos, seglen)
                o_ref[...] = _nn(_bf(pooled), _bf(w_ref[...])) * s_ref[...]

    return pl.pallas_call(
        body, name=name, grid=(AUX_GROUPS,),
        in_specs=[pl.BlockSpec((T, HEAD_DIM), lambda g: (0, cb + g)), pl.BlockSpec((None, HEAD_DIM, HEAD_DIM), lambda g: (g, 0, 0)),
                  pl.BlockSpec((1, HEAD_DIM), lambda g: (0, g))],
        out_specs=pl.BlockSpec((T, HEAD_DIM), lambda g: (0, g)),
        out_shape=jax.ShapeDtypeStruct((T, AUX_WIDTH), F32), compiler_params=_params())(P, pool_w, pool_scale)


def _pool_bwd(name, P, pool_w, pool_scale, dmix, cfg):
    T, Lc = P.shape[0], cfg['Lc']
    cb = (cfg['ATT'] + 2 * cfg['KVW']) // HEAD_DIM
    ob = cfg['ATT'] // HEAD_DIM

    def body(u_ref, w_ref, s_ref, do_ref, du_ref, dw_ref, ds_ref):
        g = pl.program_id(0)
        pos, seglen = _seq_pos(T, Lc)
        for k, half in enumerate(POOL_HALF):
            @pl.when(g == k)
            def _(half=half):
                do = do_ref[...]
                pooled, cnt = _pooled(u_ref[...], half, pos, seglen)
                wb = _bf(w_ref[...])
                mixed = _nn(_bf(pooled), wb)
                ds_ref[...] = jnp.broadcast_to(jnp.sum(do * mixed, axis=0, keepdims=True), ds_ref.shape)
                dmixed = _bf(do * s_ref[...])
                dw_ref[...] = _tn(_bf(pooled), dmixed)
                dpooled = _nt(dmixed, wb)
                e = dpooled / cnt
                fwd, bwd = _window_sums(e, half, pos, seglen)
                adj = fwd + _fw(e, half, pos, seglen) + _bw(bwd, 1, pos) - _bw(e, half, pos)
                du_ref[...] = (adj - dpooled).astype(du_ref.dtype)

    wspec = pl.BlockSpec((None, HEAD_DIM, HEAD_DIM), lambda g: (g, 0, 0))
    return pl.pallas_call(
        body, name=name, grid=(AUX_GROUPS,),
        in_specs=[pl.BlockSpec((T, HEAD_DIM), lambda g: (0, cb + g)), wspec, pl.BlockSpec((1, HEAD_DIM), lambda g: (0, g)),
                  pl.BlockSpec((T, HEAD_DIM), lambda g: (0, ob + g))],
        out_specs=[pl.BlockSpec((T, HEAD_DIM), lambda g: (0, g)), wspec, pl.BlockSpec((8, HEAD_DIM), lambda g: (0, g))],
        out_shape=[jax.ShapeDtypeStruct((T, AUX_WIDTH), BF16), jax.ShapeDtypeStruct(pool_w.shape, F32),
                   jax.ShapeDtypeStruct((8, AUX_WIDTH), F32)],
        compiler_params=_params())(P, pool_w, pool_scale, dmix)


def _ffn_up(name, hn, wg3, wu3):
    T, D = hn.shape
    J, _, k = wg3.shape
    tm = _row_tile(T, 1088)

    def body(x_ref, wg_ref, wu_ref, g_ref, u_ref, a_ref):
        x = x_ref[...]
        g, u = _nn(x, wg_ref[...]), _nn(x, wu_ref[...])
        g_ref[...] = g
        u_ref[...] = u
        a_ref[...] = (g * jax.nn.sigmoid(g) * u).astype(a_ref.dtype)

    wspec = pl.BlockSpec((None, D, k), lambda j, i: (j, 0, 0))
    ospec = pl.BlockSpec((None, tm, k), lambda j, i: (j, i, 0))
    return pl.pallas_call(
        body, name=name, grid=(J, T // tm), in_specs=[pl.BlockSpec((tm, D), lambda j, i: (i, 0)), wspec, wspec],
        out_specs=[ospec, ospec, ospec],
        out_shape=[jax.ShapeDtypeStruct((J, T, k), F32), jax.ShapeDtypeStruct((J, T, k), F32), jax.ShapeDtypeStruct((J, T, k), BF16)],
        compiler_params=_params())(hn, wg3, wu3)


def _ffn_dact(name, dF, wd3, G, U):
    T, D = dF.shape
    J, k, _ = wd3.shape
    tm = _row_tile(T, 1088)

    def body(df_ref, wd_ref, g_ref, u_ref, dg_ref, du_ref):
        da = _nt(df_ref[...], wd_ref[...])
        g = g_ref[...]
        sig = jax.nn.sigmoid(g)
        du_ref[...] = (da * g * sig).astype(du_ref.dtype)
        dg_ref[...] = (da * u_ref[...] * (sig * (1 + g * (1 - sig)))).astype(dg_ref.dtype)

    aspec = pl.BlockSpec((None, tm, k), lambda j, i: (j, i, 0))
    return pl.pallas_call(
        body, name=name, grid=(J, T // tm),
        in_specs=[pl.BlockSpec((tm, D), lambda j, i: (i, 0)), pl.BlockSpec((None, k, D), lambda j, i: (j, 0, 0)), aspec, aspec],
        out_specs=[aspec, aspec],
        out_shape=[jax.ShapeDtypeStruct((J, T, k), BF16), jax.ShapeDtypeStruct((J, T, k), BF16)],
        compiler_params=_params())(dF, wd3, G, U)


def _loss_head(name, h, g, target, cfg):
    T, D = h.shape

    def body(h_ref, g_ref, t_ref, dh_ref, loss_ref, dg_ref):
        i = pl.program_id(0)

        @pl.when(i == 0)
        def _():
            dh_ref[...] = jnp.zeros_like(dh_ref)
            loss_ref[...] = jnp.zeros_like(loss_ref)
            dg_ref[...] = jnp.zeros_like(dg_ref)

        @pl.when(i > 0)
        def _():
            x = h_ref[...]
            r = _rstd(x)
            xhat = x * r
            gg = g_ref[...]
            err = xhat * gg - t_ref[...]
            loss_ref[...] += 0.5 * jnp.sum(jnp.sum(err * err, axis=1, keepdims=True) / D, axis=0, keepdims=True)
            dy = err / D
            dg_ref[0:1, :] += jnp.sum(dy * xhat, axis=0, keepdims=True)
            dxh = dy * gg
            dh_ref[...] = r * (dxh - xhat * jnp.mean(dxh * xhat, axis=-1, keepdims=True))

    row = pl.BlockSpec((ROW_TILE, D), lambda i: (i, 0))
    return pl.pallas_call(
        body, name=name, grid=(T // ROW_TILE,),
        in_specs=[row, pl.BlockSpec((1, D), lambda i: (0, 0)), pl.BlockSpec((ROW_TILE, D), lambda i: (jnp.maximum(i - 1, 0), 0))],
        out_specs=[row, pl.BlockSpec((8, 128), lambda i: (0, 0)), pl.BlockSpec((8, D), lambda i: (0, 0))],
        out_shape=[jax.ShapeDtypeStruct((T, D), F32), jax.ShapeDtypeStruct((8, 128), F32), jax.ShapeDtypeStruct((8, D), F32)],
        compiler_params=_params())(h, g, target)


def _adamw(name, parts, w, m, v):
    R, C = w.shape
    n_parts = parts.shape[0]
    tr = _row_tile(R, max(16, (1 << 18) // C)) if R % 16 == 0 else R
    bc1 = 1.0 - ADAM_B1 ** ADAM_STEP
    bc2 = 1.0 - ADAM_B2 ** ADAM_STEP

    def body(p_ref, w_ref, m_ref, v_ref, g_ref, d_ref, nm_ref, nv_ref):
        g = p_ref[0].astype(F32)
        for k in range(1, n_parts):
            g = g + p_ref[k].astype(F32)
        nm = ADAM_B1 * m_ref[...] + (1.0 - ADAM_B1) * g
        nv = ADAM_B2 * v_ref[...] + (1.0 - ADAM_B2) * (g * g)
        g_ref[...] = g
        nm_ref[...] = nm
        nv_ref[...] = nv
        d_ref[...] = -ADAM_LR * ((nm / bc1) / (jnp.sqrt(nv / bc2) + ADAM_EPS) + ADAM_WD * w_ref[...])

    blk = pl.BlockSpec((tr, C), lambda i: (i, 0))
    return pl.pallas_call(
        body, name=name, grid=(R // tr,), in_specs=[pl.BlockSpec((n_parts, tr, C), lambda i: (0, i, 0)), blk, blk, blk],
        out_specs=[blk] * 4, out_shape=[jax.ShapeDtypeStruct((R, C), F32)] * 4, compiler_params=_params())(parts, w, m, v)


def _layer_fwd(l, h, p, weights, mod, rope, conv_w8, cfg):
    nm = f"l{l}_"
    xn = _norm_mod(nm + "norm1", h, p['norm1_g'], mod, 0)
    W = weights(MIXER_WEIGHTS, xn)
    P = _mm_cols(nm + "w_in", xn, W['w_in'])
    qr, kr, vb = _qk_prep(nm + "qk_prep", P, p['q_norm_g'], p['k_norm_g'], rope[0], rope[1], cfg)
    if l == 0:
        o, lse = _attn_dense_fwd(nm + "attn", qr, kr, vb, cfg)
        aux = _conv_fwd(nm + "conv", P, conv_w8, cfg)
    else:
        o, lse = _attn_win_fwd(nm + "attn", qr, kr, vb, p['sink'], cfg)
        aux = _pool_fwd(nm + "pool", P, p['pool_w'], p['pool_scale'], cfg)
    mix = jnp.concatenate([o, aux], axis=1).astype(BF16)
    y = _mm_plain(nm + "w_out", mix, W['w_out'], False)
    h2 = _gate_res(nm + "res1", h, y, mod, 2)
    hn = _norm_mod(nm + "norm2", h2, p['norm2_g'], mod, 1)
    W.update(weights(FFN_WEIGHTS, hn))
    G, U, A = _ffn_up(nm + "ffn_up", hn, W['w_gate'], W['w_up'])
    F = _mm_shards_nn(nm + "w_down", A, W['w_down'])
    h3 = _gate_res(nm + "res2", h2, F, mod, 5)
    saved = dict(h=h, xn=xn, P=P, qr=qr, kr=kr, vb=vb, o=o, lse=lse, mix=mix, y=y, h2=h2, hn=hn, G=G, U=U, A=A, F=F)
    return h3, saved, W


def _layer_bwd(l, dh3, s, p, W, mod, rope, conv_w8, cfg):
    nm = f"l{l}_bwd_"
    J = N_DEV
    dF, dmod = _gate_bwd(nm + "res2", dh3, s['F'], mod, 5)
    dG, dU = _ffn_dact(nm + "ffn_act", dF, W['w_down'], s['G'], s['U'])
    big = {'w_down': _wgrad_down(nm + "dw_down", s['A'], dF),
           'w_gate': _wgrad_up(nm + "dw_gate", s['hn'], dG),
           'w_up': _wgrad_up(nm + "dw_up", s['hn'], dU)}
    ffn_handle, token = _exchange_start(f"scatter_ffn{l}_start", [big[n] for n in FFN_WEIGHTS], True)
    mod = mod + token[0, 0]
    dhn = _mm_shards_nt2(nm + "dhn", dG, W['w_gate'], dU, W['w_up'])
    dh2, dm, dg2 = _norm_mod_bwd(nm + "norm2", dhn, s['h2'], p['norm2_g'], mod, 1, dh3)
    dmod += dm
    dY, dm = _gate_bwd(nm + "res1", dh2, s['y'], mod, 2)
    dmod += dm
    dwo = _wgrad_rows(nm + "dw_out", s['mix'], dY)
    big['w_out'] = dwo.reshape((J, dwo.shape[0] // J, dwo.shape[1]))
    dmix = _mm_plain(nm + "dmix", dY, W['w_out'], True)
    small = {'norm2_g': dg2[0]}
    if l == 0:
        dqr, dkr, dv = _attn_dense_bwd(nm + "attn", s['qr'], s['kr'], s['vb'], s['o'], s['lse'], dmix, cfg)
        *daux, dcw = _conv_bwd(nm + "conv", s['P'], conv_w8, dmix, cfg)
        small['conv_w'] = dcw[0:3]
    else:
        dqr, dkr, dv, dsk = _attn_win_bwd(nm + "attn", s['qr'], s['kr'], s['vb'], s['o'], s['lse'], dmix, p['sink'], cfg)
        du, dpw, dps = _pool_bwd(nm + "pool", s['P'], p['pool_w'], p['pool_scale'], dmix, cfg)
        daux = [du]
        small.update(sink=dsk[:, 0, ::HEAD_DIM].reshape(-1), pool_w=dpw, pool_scale=dps[0])
    dq, dk, dqg, dkg = _qk_prep_bwd(nm + "qk_prep", dqr, dkr, s['P'], p['q_norm_g'], p['k_norm_g'], rope[0], rope[1], cfg)
    small.update(q_norm_g=dqg[0], k_norm_g=dkg[0])
    dP = jnp.concatenate([dq, dk, dv.astype(BF16), *daux], axis=1)
    big['w_in'] = _wgrad_cols(nm + "dw_in", s['xn'], dP, J)
    dxn = _mm_cols_nt(nm + "dxn", dP, W['w_in'])
    dh, dm, dg1 = _norm_mod_bwd(nm + "norm1", dxn, s['h'], p['norm1_g'], mod, 0, dh2)
    dmod += dm
    small['norm1_g'] = dg1[0]
    mixer_handle, token = _exchange_start(f"scatter_mixer{l}_start", [big[n] for n in MIXER_WEIGHTS], True)
    return dh, dmod, small, (ffn_handle, mixer_handle), token


def _rope_tables(S, Lc):
    half = HEAD_DIM // 4
    pos = np.arange(S)
    inv = ROPE_THETA ** (-np.arange(0, 2 * half, 2, dtype=np.float32) / (2 * half))
    inv = jnp.asarray(inv, F32)
    ang_r = jnp.asarray(pos // GRID_W, F32)[:, None] * inv
    ang_c = jnp.asarray(pos % GRID_W, F32)[:, None] * inv
    cos = jnp.concatenate([jnp.cos(ang_r)] * 2 + [jnp.cos(ang_c)] * 2, axis=1)
    sin = jnp.concatenate([-jnp.sin(ang_r), jnp.sin(ang_r), -jnp.sin(ang_c), jnp.sin(ang_c)], axis=1)
    return (jnp.concatenate([jnp.ones((Lc, HEAD_DIM), F32), cos], axis=0),
            jnp.concatenate([jnp.zeros((Lc, HEAD_DIM), F32), sin], axis=0))


def _pad_rows(a, rows):
    return jnp.concatenate([a, jnp.zeros((rows - a.shape[0],) + a.shape[1:], a.dtype)], axis=0)


def _flat128(a, nlead):
    lead = a.shape[:nlead]
    f = a.reshape(lead + (-1,))
    pad = (-f.shape[-1]) % 128
    if pad:
        f = jnp.concatenate([f, jnp.zeros(lead + (pad,), f.dtype)], axis=-1)
    return f.reshape(lead + (-1, 128))


def _pack(named, nlead=0):
    rows, layout, at = [], {}, 0
    for name, a in named:
        f = _flat128(a, nlead)
        n = f.shape[-2]
        pad = (-n) % 8
        if pad:
            f = jnp.concatenate([f, jnp.zeros(f.shape[:-2] + (pad, 128), f.dtype)], axis=-2)
        layout[name] = (at, n, a.shape[nlead:])
        rows.append(f)
        at += n + pad
    return jnp.concatenate(rows, axis=-2), layout


def _unpack(arr, layout, name):
    at, n, shape = layout[name]
    return arr[..., at:at + n, :].reshape(arr.shape[:-2] + (-1,))[..., :math.prod(shape)].reshape(arr.shape[:-2] + tuple(shape))


def kernel(*args):
    A = dict(zip(INPUT_NAMES, args, strict=True))
    x, ctx = A['x'][0], A['ctx'][0]
    S, D = x.shape
    Lc = ctx.shape[0]
    T = Lc + S
    ATT = D - AUX_WIDTH
    KVW = (A['l1_w_in'].shape[1] * N_DEV - ATT - AUX_WIDTH) // 2
    cfg = dict(ATT=ATT, KVW=KVW, NKV=KVW // HEAD_DIM, G=ATT // KVW, Lc=Lc)
    assert Lc == ROW_TILE and S % ROW_TILE == 0 and T >= BAND and S % GRID_W == 0
    cw = A['l0_conv_w'].shape[1]
    me = 4 * lax.axis_index("x") + 2 * lax.axis_index("y") + lax.axis_index("c")

    def layer_params(l):
        pre = f"l{l}_"
        return {k[len(pre):]: (v.reshape(1, -1) if v.ndim == 1 and k != 'l1_sink' else v) for k, v in A.items() if k.startswith(pre)}

    params = [layer_params(0), layer_params(1)]

    gathers, token = {}, None
    for l in range(2):
        for group in (MIXER_WEIGHTS, FFN_WEIGHTS):
            tag = "mixer" if group is MIXER_WEIGHTS else "ffn"
            gathers[l, group], token = _exchange_start(f"gather_{tag}{l}_start", [A[f'l{l}_{n}'].astype(BF16) for n in group], False)

    def weights_of(l):
        def weights(group, after):
            tag = "mixer" if group is MIXER_WEIGHTS else "ffn"
            got = dict(zip(group, _exchange_wait(f"gather_{tag}{l}_wait", gathers[l, group], after)))
            if 'w_out' in got:
                got['w_out'] = got['w_out'].reshape(D, D)
            return got
        return weights

    sc_own = jax.nn.silu(A['c']) + token[0, 0]
    first, lay0 = _pack([('sc', sc_own), ('conv_w', A['l0_conv_w'])])
    first_all = _exchange("gather_cond", [first], False)[0]
    sc_all = _unpack(first_all, lay0, 'sc')[:, 0]
    conv_w = _unpack(first_all, lay0, 'conv_w').transpose(1, 0, 2).reshape(3, N_DEV * cw)
    conv_w8 = _pad_rows(conv_w, 8)
    sc_ctx = jax.nn.silu(A['c_ctx'])
    s16 = _pad_rows(jnp.concatenate([sc_all, sc_ctx[None]], axis=0), 16)

    nmod = A['l0_w_mod'].shape[1]
    modp = jnp.concatenate([_mm_plain(f"l{l}_mod", s16, A[f'l{l}_w_mod'], False) for l in range(2)], axis=1)
    modp_all = _exchange("gather_mod", [modp], False)[0]
    mods = []
    for l in range(2):
        full = modp_all[:, :, l * nmod:(l + 1) * nmod].transpose(1, 0, 2).reshape(16, N_MOD * D) + A[f'l{l}_b_mod'][None]
        both = jnp.stack([full[8], lax.dynamic_index_in_dim(full, me, 0, keepdims=False)]).reshape(2, N_MOD, D)
        mods.append(jnp.concatenate([both, jnp.zeros((2, 8 - N_MOD, D), F32)], axis=1))

    rope = _rope_tables(S, Lc)
    h = jnp.concatenate([ctx, x], axis=0)
    saved, W = [], []
    for l in range(2):
        h, s, Wl = _layer_fwd(l, h, params[l], weights_of(l), mods[l], rope, conv_w8, cfg)
        saved.append(s)
        W.append(Wl)

    dh, loss_blk, dgf = _loss_head("loss_head", h, A['final_norm_g'].reshape(1, -1), A['loss_target'][0], cfg)
    loss = lax.psum(loss_blk[0, 0], ("x", "y", "c"))

    grads, small, dmods, scatters = {}, {'final_norm_g': dgf[0]}, [None, None], [None, None]
    token = jnp.zeros((8, 128), F32)
    for l in (1, 0):
        dh, dmods[l], sm, scatters[l], token = _layer_bwd(l, dh, saved[l], params[l], W[l], mods[l] + token[0, 0], rope, conv_w8, cfg)
        small.update({f'l{l}_{k}': v for k, v in sm.items()})
    grad_x = dh[Lc:][None]

    def landed(l, k, after):
        group = (FFN_WEIGHTS, MIXER_WEIGHTS)[k]
        tag = ("ffn", "mixer")[k]
        for n, parts in zip(group, _exchange_wait(f"scatter_{tag}{l}_wait", scatters[l][k], after)):
            grads[f'l{l}_{n}'] = (parts.reshape((N_DEV,) + A[f'l{l}_{n}'].shape), None)

    small_names = [n for n in WEIGHT_NAMES if n in small]
    pieces = [(n, small[n]) for n in small_names]
    for l in range(2):
        pieces += [(f'dmod{l}', dmods[l][1, :N_MOD]), (f'dcmod{l}', dmods[l][0, :N_MOD])]
    second, lay1 = _pack(pieces)
    second_all = _exchange("gather_small", [second], False)[0]

    dsc_part = jnp.zeros((16, D), F32)
    for l in range(2):
        dm16 = _pad_rows(jnp.concatenate([_unpack(second_all, lay1, f'dmod{l}').reshape(N_DEV, N_MOD * D),
                                          jnp.sum(_unpack(second_all, lay1, f'dcmod{l}'), axis=0).reshape(1, N_MOD * D)], axis=0), 16)
        mine = lax.dynamic_slice_in_dim(dm16, me * nmod, nmod, axis=1)
        tk = _col_tile(D, 512)
        gw = _mm_tn(f"l{l}_dw_mod", s16, mine, pl.BlockSpec((16, tk), lambda kb, t: (0, kb)), pl.BlockSpec((16, nmod), lambda kb, t: (0, 0)),
                    jax.ShapeDtypeStruct((D, nmod), F32), pl.BlockSpec((tk, nmod), lambda kb, t: (kb, 0)), (tk, nmod), (D // tk, 1))
        grads[f'l{l}_w_mod'] = (gw[None], None)
        dsc_part += _mm_plain(f"l{l}_dsc", mine, A[f'l{l}_w_mod'], True)
        dmod_dev = _unpack(second_all, lay1, f'dmod{l}') + _unpack(second_all, lay1, f'dcmod{l}')
        grads[f'l{l}_b_mod'] = (dmod_dev.reshape(N_DEV, N_MOD * D), None)
    dsig = jax.nn.sigmoid(A['c_ctx'])
    dsilu = dsig * (1 + A['c_ctx'] * (1 - dsig))
    third_all = _exchange("gather_dsc", [dsc_part[8:9]], False)[0]
    grads['c_ctx'] = (third_all[:, 0] * dsilu[None], None)
    for n in small_names:
        g8 = _unpack(second_all, lay1, n)
        if n == 'l0_conv_w':
            g8 = lax.dynamic_slice_in_dim(g8, me * cw, cw, axis=2)
        grads[n] = (g8, None)

    out = {}
    big_names = [n for n in WEIGHT_NAMES if n[3:] in BIG_WEIGHTS + ('w_mod',)]

    def adam(n):
        out[n] = _adamw("adamw_" + n, grads[n][0], A[n], A['m_' + n], A['v_' + n])
        return out[n][1]

    landed(1, 0, second_all)
    landed(1, 1, second_all)
    for n in ['l0_w_mod', 'l1_w_mod'] + ['l1_' + n for n in BIG_WEIGHTS]:
        last = adam(n)
    landed(0, 0, last)
    for n in FFN_WEIGHTS:
        last = adam('l0_' + n)
    landed(0, 1, last)
    for n in MIXER_WEIGHTS:
        adam('l0_' + n)
    rest = [n for n in WEIGHT_NAMES if n not in big_names]
    wp, layw = _pack([(n, A[n]) for n in rest])
    mp, _ = _pack([(n, A['m_' + n]) for n in rest])
    vp, _ = _pack([(n, A['v_' + n]) for n in rest])
    gp, _ = _pack([(n, grads[n][0]) for n in rest], nlead=1)
    res = _adamw("adamw_small", gp, wp, mp, vp)
    for n in rest:
        out[n] = tuple(_unpack(r, layw, n) for r in res)

    outs = [loss, grad_x]
    for k in range(4):
        outs += [out[n][k] for n in WEIGHT_NAMES]
    return tuple(outs)
```

```python
import functools
import math

import numpy as np
import jax
import jax.numpy as jnp
from jax import lax
from jax.experimental import pallas as pl
from jax.experimental.pallas import tpu as pltpu

F32 = jnp.float32
BF16 = jnp.bfloat16
HEAD_DIM = 128
AUX_WIDTH = 512
AUX_GROUPS = 4
POOL_HALF = (1, 2, 4, 8)
WINDOW = 128
GRID_W = 64
ROPE_THETA = 10000.0
EPS = 1e-6
NEG_INF = -1e30
ATT_SCALE = HEAD_DIM ** -0.5
N_MOD = 6
N_DEV = 8
ROW_TILE = 256
BAND = ROW_TILE + 2 * WINDOW
ADAM_LR, ADAM_B1, ADAM_B2, ADAM_EPS, ADAM_WD, ADAM_STEP = 0.001, 0.9, 0.999, 1e-08, 0.01, 10
VMEM_LIMIT_MB = 56
MESH = pl.DeviceIdType.MESH

WEIGHT_NAMES = ['c_ctx', 'l0_norm1_g', 'l0_w_mod', 'l0_b_mod', 'l0_w_in', 'l0_q_norm_g', 'l0_k_norm_g', 'l0_conv_w', 'l0_w_out', 'l0_norm2_g', 'l0_w_gate', 'l0_w_up', 'l0_w_down', 'l1_norm1_g', 'l1_w_mod', 'l1_b_mod', 'l1_w_in', 'l1_q_norm_g', 'l1_k_norm_g', 'l1_sink', 'l1_pool_w', 'l1_pool_scale', 'l1_w_out', 'l1_norm2_g', 'l1_w_gate', 'l1_w_up', 'l1_w_down', 'final_norm_g']
INPUT_NAMES = (['x', 'c', 'ctx'] + WEIGHT_NAMES + ['loss_target'] + ['m_' + n for n in WEIGHT_NAMES]
               + ['v_' + n for n in WEIGHT_NAMES])
MIXER_WEIGHTS = ('w_out', 'w_in')
FFN_WEIGHTS = ('w_down', 'w_gate', 'w_up')
TRANSPOSED = ('w_gate', 'w_up')
BIG_WEIGHTS = MIXER_WEIGHTS + FFN_WEIGHTS


def _params(vmem_mb=VMEM_LIMIT_MB):
    return pltpu.CompilerParams(vmem_limit_bytes=vmem_mb << 20)


def _row_tile(n, cap):
    best = None
    for t in range(16, min(n, cap) + 1, 16):
        if n % t == 0:
            best = t
    assert best is not None, (n, cap)
    return best


def _col_tile(n, cap):
    best = n
    for t in range(128, min(n, cap) + 1, 128):
        if n % t == 0:
            best = t
    return best if best <= cap or n % 128 else n


def _dot(a, b, ca, cb):
    return lax.dot_general(a, b, (((ca,), (cb,)), ((), ())), preferred_element_type=F32)


def _nn(a, b):
    return _dot(a, b, 1, 0)


def _nt(a, b):
    return _dot(a, b, 1, 1)


def _tn(a, b):
    return _dot(a, b, 0, 0)


def _bf(x):
    return x.astype(BF16)


def _exchange(name, arrs, scatter):
    n = len(arrs)
    if scatter:
        out_shape = [jax.ShapeDtypeStruct(a.shape, a.dtype) for a in arrs]
    else:
        out_shape = [jax.ShapeDtypeStruct((N_DEV,) + a.shape, a.dtype) for a in arrs]

    def body(*refs):
        ins, outs = refs[:n], refs[n:2 * n]
        send_sems, recv_sems, local_sems = refs[2 * n:]
        x, y, c = lax.axis_index("x"), lax.axis_index("y"), lax.axis_index("c")
        me = 4 * x + 2 * y + c
        local, remote = [], []
        for a in range(n):
            own = ins[a].at[me] if scatter else ins[a]
            cp = pltpu.make_async_copy(own, outs[a].at[me], local_sems.at[a])
            cp.start()
            local.append(cp)
            for r in range(1, N_DEV):
                px = 1 - x if r & 4 else x
                py = 1 - y if r & 2 else y
                pc = 1 - c if r & 1 else c
                src = ins[a].at[4 * px + 2 * py + pc] if scatter else ins[a]
                cp = pltpu.make_async_remote_copy(
                    src_ref=src, dst_ref=outs[a].at[me], send_sem=send_sems.at[a, r - 1],
                    recv_sem=recv_sems.at[a, r - 1], device_id=(px, py, pc), device_id_type=MESH)
                cp.start()
                remote.append(cp)
        for cp in remote:
            cp.wait()
        for cp in local:
            cp.wait()

    any_spec = pl.BlockSpec(memory_space=pl.ANY)
    return pl.pallas_call(
        body, name=name, out_shape=out_shape,
        in_specs=[any_spec] * n, out_specs=[any_spec] * n,
        scratch_shapes=[pltpu.SemaphoreType.DMA((n, N_DEV - 1)), pltpu.SemaphoreType.DMA((n, N_DEV - 1)),
                        pltpu.SemaphoreType.DMA((n,))],
    )(*arrs)


HBM_SPEC = pl.BlockSpec(memory_space=pltpu.HBM)
SEM_SPEC = pl.BlockSpec(memory_space=pltpu.SEMAPHORE)
EFFECT = pltpu.SideEffectType.DATAFLOW_SIDE_EFFECTING


def _split_copies(srcs, lands, send_sems, recv_sems, local_sems, scatter):
    x, y, c = lax.axis_index("x"), lax.axis_index("y"), lax.axis_index("c")
    me = 4 * x + 2 * y + c
    local, remote = [], []
    for a in range(len(srcs)):
        own = srcs[a].at[me] if scatter else srcs[a]
        local.append(pltpu.make_async_copy(own, lands[a].at[me], local_sems.at[a]))
        for r in range(1, N_DEV):
            px = 1 - x if r & 4 else x
            py = 1 - y if r & 2 else y
            pc = 1 - c if r & 1 else c
            src = srcs[a].at[4 * px + 2 * py + pc] if scatter else srcs[a]
            remote.append(pltpu.make_async_remote_copy(
                src_ref=src, dst_ref=lands[a].at[me], send_sem=send_sems.at[a * (N_DEV - 1) + r - 1],
                recv_sem=recv_sems.at[a * (N_DEV - 1) + r - 1], device_id=(px, py, pc), device_id_type=MESH))
    return local, remote


def _exchange_start(name, arrs, scatter, after=None):
    n = len(arrs)
    extra = [] if after is None else [after]
    shapes = [a.shape if scatter else (N_DEV,) + a.shape for a in arrs]
    lands = [pltpu.with_memory_space_constraint(lax.empty(s, a.dtype), pltpu.HBM) for s, a in zip(shapes, arrs)]
    srcs = [pltpu.with_memory_space_constraint(a, pltpu.HBM) for a in arrs]

    def body(*refs):
        src_refs, land_refs = refs[:n], refs[n:2 * n]
        send_sems, recv_sems, local_sems = refs[2 * n + len(extra):2 * n + len(extra) + 3]
        token = refs[-1]
        local, remote = _split_copies(src_refs, land_refs, send_sems, recv_sems, local_sems, scatter)
        for cp in local + remote:
            cp.start()
        token[...] = jnp.zeros_like(token)

    res = pl.pallas_call(
        body, name=name,
        out_shape=[pltpu.SemaphoreType.DMA((n * (N_DEV - 1),)), pltpu.SemaphoreType.DMA((n * (N_DEV - 1),)), pltpu.SemaphoreType.DMA((n,))]
        + [pltpu.HBM(a.shape, a.dtype) for a in arrs] + [pltpu.HBM(s, a.dtype) for s, a in zip(shapes, arrs)]
        + [jax.ShapeDtypeStruct((8, 128), F32)],
        in_specs=[HBM_SPEC] * (2 * n) + [pl.BlockSpec(memory_space=pl.ANY)] * len(extra),
        out_specs=[SEM_SPEC] * 3 + [HBM_SPEC] * (2 * n) + [pl.BlockSpec(memory_space=pltpu.VMEM)],
        input_output_aliases={i: 3 + i for i in range(2 * n)},
        compiler_params=pltpu.CompilerParams(has_side_effects=EFFECT),
    )(*srcs, *lands, *extra)
    return (scatter, res[:3], res[3:3 + n], res[3 + n:3 + 2 * n]), res[-1]


def _exchange_wait(name, handle, after):
    scatter, sems, srcs, lands = handle
    n = len(srcs)

    def body(*refs):
        src_refs, land_refs = refs[:n], refs[n:2 * n]
        send_sems, recv_sems, local_sems = refs[2 * n:2 * n + 3]
        local, remote = _split_copies(src_refs, land_refs, send_sems, recv_sems, local_sems, scatter)
        for cp in remote:
            cp.wait_send()
            cp.wait_recv()
        for cp in local:
            cp.wait()

    res = pl.pallas_call(
        body, name=name,
        out_shape=[pltpu.HBM(a.shape, a.dtype) for a in srcs] + [pltpu.HBM(a.shape, a.dtype) for a in lands],
        in_specs=[HBM_SPEC] * (2 * n) + [SEM_SPEC] * 3 + [pl.BlockSpec(memory_space=pl.ANY)], out_specs=[HBM_SPEC] * (2 * n),
        input_output_aliases={i: i for i in range(2 * n)},
        compiler_params=pltpu.CompilerParams(has_side_effects=EFFECT),
    )(*srcs, *lands, *sems, after)
    return list(res[n:])


def _mm_step(name, fn, ins, in_specs, out_shape, out_spec, grid):
    n = len(ins)

    def body(*refs):
        o_ref = refs[n]
        o_ref[...] = fn(*refs[:n]).astype(o_ref.dtype)

    return pl.pallas_call(body, name=name, grid=grid, in_specs=in_specs, out_specs=out_spec,
                          out_shape=out_shape, compiler_params=_params())(*ins)


def _mm_tn(name, a, b, a_spec, b_spec, out_shape, out_spec, acc_shape, grid):
    nk = grid[-1]
    kax = len(grid) - 1

    def body(a_ref, b_ref, o_ref, acc_ref):
        k = pl.program_id(kax)

        @pl.when(k == 0)
        def _():
            acc_ref[...] = jnp.zeros_like(acc_ref)

        acc_ref[...] += _tn(_bf(a_ref[...]), _bf(b_ref[...]))

        @pl.when(k == nk - 1)
        def _():
            o_ref[...] = acc_ref[...].astype(o_ref.dtype)

    return pl.pallas_call(body, name=name, grid=grid, in_specs=[a_spec, b_spec], out_specs=out_spec,
                          out_shape=out_shape, scratch_shapes=[pltpu.VMEM(acc_shape, F32)],
                          compiler_params=_params())(a, b)


def _mm_cols(name, a, w3, out_dtype=F32):
    M, K = a.shape
    J, _, n = w3.shape
    tm = _row_tile(M, 1088)
    return _mm_step(
        name, lambda a_ref, w_ref: _nn(_bf(a_ref[...]), w_ref[...]), [a, w3],
        [pl.BlockSpec((tm, K), lambda j, i: (i, 0)), pl.BlockSpec((None, K, n), lambda j, i: (j, 0, 0))],
        jax.ShapeDtypeStruct((M, J * n), out_dtype), pl.BlockSpec((tm, n), lambda j, i: (i, j)), (J, M // tm))


def _mm_plain(name, a, b, transpose_b, out_dtype=F32, tn=512):
    M, K = a.shape
    N = b.shape[0] if transpose_b else b.shape[1]
    tm = _row_tile(M, 1088)
    tn = _col_tile(N, tn)
    if transpose_b:
        b_spec = pl.BlockSpec((tn, K), lambda j, i: (j, 0))
        fn = lambda a_ref, b_ref: _nt(_bf(a_ref[...]), _bf(b_ref[...]))
    else:
        b_spec = pl.BlockSpec((K, tn), lambda j, i: (0, j))
        fn = lambda a_ref, b_ref: _nn(_bf(a_ref[...]), _bf(b_ref[...]))
    return _mm_step(name, fn, [a, b], [pl.BlockSpec((tm, K), lambda j, i: (i, 0)), b_spec],
                    jax.ShapeDtypeStruct((M, N), out_dtype), pl.BlockSpec((tm, tn), lambda j, i: (i, j)),
                    (N // tn, M // tm))


def _mm_shards_nn(name, a3, w3, tn=512):
    J, M, k = a3.shape
    N = w3.shape[2]
    tm = _row_tile(M, 544)
    tn = _col_tile(N, tn)

    def fn(a_ref, w_ref):
        acc = _nn(a_ref[0], w_ref[0])
        for j in range(1, J):
            acc += _nn(a_ref[j], w_ref[j])
        return acc

    return _mm_step(name, fn, [a3, w3],
                    [pl.BlockSpec((J, tm, k), lambda jn, i: (0, i, 0)), pl.BlockSpec((J, k, tn), lambda jn, i: (0, 0, jn))],
                    jax.ShapeDtypeStruct((M, N), F32), pl.BlockSpec((tm, tn), lambda jn, i: (i, jn)), (N // tn, M // tm))


def _mm_shards_nn2(name, a3, w3a, b3, w3b, tn=512):
    J, M, k = a3.shape
    N = w3a.shape[2]
    tm = _row_tile(M, 544)
    tn = _col_tile(N, tn)

    def fn(a_ref, wa_ref, b_ref, wb_ref):
        acc = _nn(a_ref[0], wa_ref[0]) + _nn(b_ref[0], wb_ref[0])
        for j in range(1, J):
            acc += _nn(a_ref[j], wa_ref[j]) + _nn(b_ref[j], wb_ref[j])
        return acc

    act = pl.BlockSpec((J, tm, k), lambda jn, i: (0, i, 0))
    wsp = pl.BlockSpec((J, k, tn), lambda jn, i: (0, 0, jn))
    return _mm_step(name, fn, [a3, w3a, b3, w3b], [act, wsp, act, wsp],
                    jax.ShapeDtypeStruct((M, N), F32), pl.BlockSpec((tm, tn), lambda jn, i: (i, jn)), (N // tn, M // tm))


def _mm_cols_nt(name, a, w3, tn=512):
    M = a.shape[0]
    J, N, n = w3.shape
    tm = _row_tile(M, 544)
    tn = _col_tile(N, tn)

    def fn(a_ref, w_ref):
        acc = _nt(a_ref[:, 0:n], w_ref[0])
        for j in range(1, J):
            acc += _nt(a_ref[:, j * n:(j + 1) * n], w_ref[j])
        return acc

    return _mm_step(name, fn, [a, w3],
                    [pl.BlockSpec((tm, J * n), lambda jn, i: (i, 0)), pl.BlockSpec((J, tn, n), lambda jn, i: (0, jn, 0))],
                    jax.ShapeDtypeStruct((M, N), F32), pl.BlockSpec((tm, tn), lambda jn, i: (i, jn)), (N // tn, M // tm))


def _wgrad_cols(name, a, b, J):
    T, K = a.shape
    n = b.shape[1] // J
    tt = _row_tile(T, 1088)
    return _mm_tn(name, a, b, pl.BlockSpec((tt, K), lambda j, t: (t, 0)), pl.BlockSpec((tt, n), lambda j, t: (t, j)),
                  jax.ShapeDtypeStruct((J, K, n), BF16), pl.BlockSpec((None, K, n), lambda j, t: (j, 0, 0)), (K, n), (J, T // tt))


def _wgrad_rows(name, a, b, tk=512):
    T, K = a.shape
    N = b.shape[1]
    tt = _row_tile(T, 1088)
    tk = _col_tile(K, tk)
    return _mm_tn(name, a, b, pl.BlockSpec((tt, tk), lambda kb, t: (t, kb)), pl.BlockSpec((tt, N), lambda kb, t: (t, 0)),
                  jax.ShapeDtypeStruct((K, N), BF16), pl.BlockSpec((tk, N), lambda kb, t: (kb, 0)), (tk, N), (K // tk, T // tt))


def _wgrad_up(name, a, b3):
    T, K = a.shape
    J, _, k = b3.shape
    tt = _row_tile(T, 1088)
    return _mm_tn(name, a, b3, pl.BlockSpec((tt, K), lambda j, t: (t, 0)), pl.BlockSpec((None, tt, k), lambda j, t: (j, t, 0)),
                  jax.ShapeDtypeStruct((J, K, k), BF16), pl.BlockSpec((None, K, k), lambda j, t: (j, 0, 0)), (K, k), (J, T // tt))


def _wgrad_down(name, a3, b):
    J, T, k = a3.shape
    N = b.shape[1]
    tt = _row_tile(T, 1088)
    return _mm_tn(name, a3, b, pl.BlockSpec((None, tt, k), lambda j, t: (j, t, 0)), pl.BlockSpec((tt, N), lambda j, t: (t, 0)),
                  jax.ShapeDtypeStruct((J, k, N), BF16), pl.BlockSpec((None, k, N), lambda j, t: (j, 0, 0)), (k, N), (J, T // tt))


def _seg(i):
    return jnp.minimum(i, 1)


def _rstd(x):
    return lax.rsqrt(jnp.mean(x * x, axis=-1, keepdims=True) + EPS)


def _norm_mod(name, h, g, mod, which):
    T, D = h.shape

    def body(h_ref, g_ref, mod_ref, o_ref):
        x = h_ref[...]
        n = x * _rstd(x) * g_ref[...]
        shift = mod_ref[3 * which:3 * which + 1, :]
        scale = mod_ref[3 * which + 1:3 * which + 2, :]
        o_ref[...] = (n * (1 + scale) + shift).astype(o_ref.dtype)

    row = pl.BlockSpec((ROW_TILE, D), lambda i: (i, 0))
    return pl.pallas_call(
        body, name=name, grid=(T // ROW_TILE,),
        in_specs=[row, pl.BlockSpec((1, D), lambda i: (0, 0)), pl.BlockSpec((None, 8, D), lambda i: (_seg(i), 0, 0))],
        out_specs=row, out_shape=jax.ShapeDtypeStruct((T, D), BF16), compiler_params=_params())(h, g, mod)


def _norm_mod_bwd(name, dxn, h, g, mod, which, dres):
    T, D = h.shape

    def body(dxn_ref, h_ref, g_ref, mod_ref, dres_ref, dh_ref, dmod_ref, dg_ref):
        i = pl.program_id(0)
        x = h_ref[...]
        r = _rstd(x)
        xhat = x * r
        g = g_ref[...]
        n = xhat * g
        scale = mod_ref[3 * which + 1:3 * which + 2, :]
        dxn = dxn_ref[...]
        dn = dxn * (1 + scale)
        dxh = dn * g
        dh_ref[...] = dres_ref[...] + r * (dxh - xhat * jnp.mean(dxh * xhat, axis=-1, keepdims=True))

        @pl.when(i <= 1)
        def _():
            dmod_ref[...] = jnp.zeros_like(dmod_ref)

        @pl.when(i == 0)
        def _():
            dg_ref[...] = jnp.zeros_like(dg_ref)

        dmod_ref[3 * which:3 * which + 1, :] += jnp.sum(dxn, axis=0, keepdims=True)
        dmod_ref[3 * which + 1:3 * which + 2, :] += jnp.sum(dxn * n, axis=0, keepdims=True)
        dg_ref[0:1, :] += jnp.sum(dn * xhat, axis=0, keepdims=True)

    row = pl.BlockSpec((ROW_TILE, D), lambda i: (i, 0))
    modspec = pl.BlockSpec((None, 8, D), lambda i: (_seg(i), 0, 0))
    return pl.pallas_call(
        body, name=name, grid=(T // ROW_TILE,),
        in_specs=[row, row, pl.BlockSpec((1, D), lambda i: (0, 0)), modspec, row],
        out_specs=[row, modspec, pl.BlockSpec((8, D), lambda i: (0, 0))],
        out_shape=[jax.ShapeDtypeStruct((T, D), F32), jax.ShapeDtypeStruct((2, 8, D), F32), jax.ShapeDtypeStruct((8, D), F32)],
        compiler_params=_params())(dxn, h, g, mod, dres)


def _gate_res(name, h, y, mod, row_idx):
    T, D = h.shape

    def body(h_ref, y_ref, mod_ref, o_ref):
        o_ref[...] = h_ref[...] + mod_ref[row_idx:row_idx + 1, :] * y_ref[...]

    row = pl.BlockSpec((ROW_TILE, D), lambda i: (i, 0))
    return pl.pallas_call(
        body, name=name, grid=(T // ROW_TILE,),
        in_specs=[row, row, pl.BlockSpec((None, 8, D), lambda i: (_seg(i), 0, 0))],
        out_specs=row, out_shape=jax.ShapeDtypeStruct((T, D), F32), compiler_params=_params())(h, y, mod)


def _gate_bwd(name, dh, y, mod, row_idx):
    T, D = dh.shape

    def body(dh_ref, y_ref, mod_ref, dy_ref, dmod_ref):
        i = pl.program_id(0)
        dh = dh_ref[...]
        dy_ref[...] = (dh * mod_ref[row_idx:row_idx + 1, :]).astype(dy_ref.dtype)

        @pl.when(i <= 1)
        def _():
            dmod_ref[...] = jnp.zeros_like(dmod_ref)

        dmod_ref[row_idx:row_idx + 1, :] += jnp.sum(dh * y_ref[...], axis=0, keepdims=True)

    row = pl.BlockSpec((ROW_TILE, D), lambda i: (i, 0))
    modspec = pl.BlockSpec((None, 8, D), lambda i: (_seg(i), 0, 0))
    return pl.pallas_call(
        body, name=name, grid=(T // ROW_TILE,), in_specs=[row, row, modspec], out_specs=[row, modspec],
        out_shape=[jax.ShapeDtypeStruct((T, D), BF16), jax.ShapeDtypeStruct((2, 8, D), F32)],
        compiler_params=_params())(dh, y, mod)


def _rot(y):
    lane = lax.broadcasted_iota(jnp.int32, y.shape, 1)
    return jnp.where((lane & 32) == 0, pltpu.roll(y, 96, 1), pltpu.roll(y, 32, 1))


def _qk_prep(name, P, q_g, k_g, rope_c, rope_s, cfg):
    T = P.shape[0]
    ATT, KVW = cfg['ATT'], cfg['KVW']

    def body(q_ref, k_ref, v_ref, qg_ref, kg_ref, c_ref, s_ref, qo_ref, ko_ref, vo_ref):
        cc, ss = c_ref[...], s_ref[...]

        def head(x, g):
            y = x * _rstd(x) * g
            return y * cc + _rot(y) * ss

        for hh in range(ATT // HEAD_DIM):
            sl = slice(hh * HEAD_DIM, (hh + 1) * HEAD_DIM)
            qo_ref[:, sl] = head(q_ref[:, sl], qg_ref[...]).astype(qo_ref.dtype)
        for hh in range(KVW // HEAD_DIM):
            sl = slice(hh * HEAD_DIM, (hh + 1) * HEAD_DIM)
            ko_ref[:, sl] = head(k_ref[:, sl], kg_ref[...]).astype(ko_ref.dtype)
        vo_ref[...] = v_ref[...].astype(vo_ref.dtype)

    kb = ATT // KVW
    gain = pl.BlockSpec((1, HEAD_DIM), lambda i: (0, 0))
    tab = pl.BlockSpec((ROW_TILE, HEAD_DIM), lambda i: (i, 0))
    qs = pl.BlockSpec((ROW_TILE, ATT), lambda i: (i, 0))
    ks = pl.BlockSpec((ROW_TILE, KVW), lambda i: (i, 0))
    return pl.pallas_call(
        body, name=name, grid=(T // ROW_TILE,),
        in_specs=[qs, pl.BlockSpec((ROW_TILE, KVW), lambda i: (i, kb)), pl.BlockSpec((ROW_TILE, KVW), lambda i: (i, kb + 1)),
                  gain, gain, tab, tab],
        out_specs=[qs, ks, ks],
        out_shape=[jax.ShapeDtypeStruct((T, ATT), BF16), jax.ShapeDtypeStruct((T, KVW), BF16), jax.ShapeDtypeStruct((T, KVW), BF16)],
        compiler_params=_params())(P, P, P, q_g, k_g, rope_c, rope_s)


def _qk_prep_bwd(name, dqr, dkr, P, q_g, k_g, rope_c, rope_s, cfg):
    T = P.shape[0]
    ATT, KVW = cfg['ATT'], cfg['KVW']

    def body(dq_ref, dk_ref, q_ref, k_ref, qg_ref, kg_ref, c_ref, s_ref, dqo_ref, dko_ref, dqg_ref, dkg_ref):
        i = pl.program_id(0)
        cc, ss = c_ref[...], s_ref[...]

        @pl.when(i == 0)
        def _():
            dqg_ref[...] = jnp.zeros_like(dqg_ref)
            dkg_ref[...] = jnp.zeros_like(dkg_ref)

        def head(x, g, dout):
            dy = dout * cc + _rot(dout * ss)
            r = _rstd(x)
            xhat = x * r
            dxh = dy * g
            dx = r * (dxh - xhat * jnp.mean(dxh * xhat, axis=-1, keepdims=True))
            return dx, jnp.sum(dy * xhat, axis=0, keepdims=True)

        dg = jnp.zeros((1, HEAD_DIM), F32)
        for hh in range(ATT // HEAD_DIM):
            sl = slice(hh * HEAD_DIM, (hh + 1) * HEAD_DIM)
            dx, d = head(q_ref[:, sl], qg_ref[...], dq_ref[:, sl])
            dqo_ref[:, sl] = dx.astype(dqo_ref.dtype)
            dg += d
        dqg_ref[0:1, :] += dg
        dg = jnp.zeros((1, HEAD_DIM), F32)
        for hh in range(KVW // HEAD_DIM):
            sl = slice(hh * HEAD_DIM, (hh + 1) * HEAD_DIM)
            dx, d = head(k_ref[:, sl], kg_ref[...], dk_ref[:, sl])
            dko_ref[:, sl] = dx.astype(dko_ref.dtype)
            dg += d
        dkg_ref[0:1, :] += dg

    kb = ATT // KVW
    gain = pl.BlockSpec((1, HEAD_DIM), lambda i: (0, 0))
    dgain = pl.BlockSpec((8, HEAD_DIM), lambda i: (0, 0))
    tab = pl.BlockSpec((ROW_TILE, HEAD_DIM), lambda i: (i, 0))
    qs = pl.BlockSpec((ROW_TILE, ATT), lambda i: (i, 0))
    ks = pl.BlockSpec((ROW_TILE, KVW), lambda i: (i, 0))
    return pl.pallas_call(
        body, name=name, grid=(T // ROW_TILE,),
        in_specs=[qs, ks, qs, pl.BlockSpec((ROW_TILE, KVW), lambda i: (i, kb)), gain, gain, tab, tab],
        out_specs=[qs, ks, dgain, dgain],
        out_shape=[jax.ShapeDtypeStruct((T, ATT), BF16), jax.ShapeDtypeStruct((T, KVW), BF16),
                   jax.ShapeDtypeStruct((8, HEAD_DIM), F32), jax.ShapeDtypeStruct((8, HEAD_DIM), F32)],
        compiler_params=_params())(dqr, dkr, P, P, q_g, k_g, rope_c, rope_s)


def _att_specs(T, G):
    qs = pl.BlockSpec((ROW_TILE, G * HEAD_DIM), lambda h, i: (i, h))
    kvs = pl.BlockSpec((T, HEAD_DIM), lambda h, i: (0, h))
    return qs, kvs


def _dense_bias(i, T, Lc):
    col = lax.broadcasted_iota(jnp.int32, (1, T), 1)
    return jnp.where(col < jnp.where(i == 0, Lc, T), 0.0, NEG_INF).astype(F32)


def _attn_dense_fwd(name, q, k, v, cfg):
    T, G, Lc = q.shape[0], cfg['G'], cfg['Lc']

    def body(q_ref, k_ref, v_ref, o_ref, lse_ref):
        kk, vv = k_ref[...], v_ref[...]
        bias = _dense_bias(pl.program_id(1), T, Lc)
        for g in range(G):
            sl = slice(g * HEAD_DIM, (g + 1) * HEAD_DIM)
            s = _nt(q_ref[:, sl], kk) * ATT_SCALE + bias
            m = jnp.max(s, axis=1, keepdims=True)
            p = jnp.exp(s - m)
            l = jnp.sum(p, axis=1, keepdims=True)
            o_ref[:, sl] = _nn(_bf(p), vv) / l
            lse_ref[:, sl] = jnp.broadcast_to(m + jnp.log(l), (ROW_TILE, HEAD_DIM))

    qs, kvs = _att_specs(T, G)
    return pl.pallas_call(
        body, name=name, grid=(cfg['NKV'], T // ROW_TILE), in_specs=[qs, kvs, kvs], out_specs=[qs, qs],
        out_shape=[jax.ShapeDtypeStruct(q.shape, F32), jax.ShapeDtypeStruct(q.shape, F32)],
        compiler_params=_params())(q, k, v)


def _attn_dense_bwd(name, q, k, v, o, lse, dmix, cfg):
    T, G, Lc = q.shape[0], cfg['G'], cfg['Lc']

    def body(q_ref, k_ref, v_ref, o_ref, lse_ref, do_ref, dq_ref, dk_ref, dv_ref):
        i = pl.program_id(1)
        kk, vv = k_ref[...], v_ref[...]
        bias = _dense_bias(i, T, Lc)

        @pl.when(i == 0)
        def _():
            dk_ref[...] = jnp.zeros_like(dk_ref)
            dv_ref[...] = jnp.zeros_like(dv_ref)

        for g in range(G):
            sl = slice(g * HEAD_DIM, (g + 1) * HEAD_DIM)
            qg, do = q_ref[:, sl], do_ref[:, sl]
            delta = jnp.sum(do * o_ref[:, sl], axis=1, keepdims=True)
            p = jnp.exp(_nt(qg, kk) * ATT_SCALE + bias - lse_ref[:, g * HEAD_DIM:g * HEAD_DIM + 1])
            dob = _bf(do)
            dv_ref[...] += _tn(_bf(p), dob)
            ds = _bf(p * (_nt(dob, vv) - delta) * ATT_SCALE)
            dq_ref[:, sl] = _nn(ds, kk)
            dk_ref[...] += _tn(ds, qg)

    qs, kvs = _att_specs(T, G)
    return pl.pallas_call(
        body, name=name, grid=(cfg['NKV'], T // ROW_TILE), in_specs=[qs, kvs, kvs, qs, qs, qs], out_specs=[qs, kvs, kvs],
        out_shape=[jax.ShapeDtypeStruct(q.shape, F32), jax.ShapeDtypeStruct(k.shape, F32), jax.ShapeDtypeStruct(k.shape, F32)],
        compiler_params=_params())(q, k, v, o, lse, dmix)


def _band(i, T, Lc):
    start = pl.multiple_of(jnp.clip(WINDOW + (i - 1) * ROW_TILE, 0, T - BAND), WINDOW)
    qpos = (i - 1) * ROW_TILE + lax.broadcasted_iota(jnp.int32, (ROW_TILE, 1), 0)
    kpos = start - Lc + lax.broadcasted_iota(jnp.int32, (1, BAND), 1)
    ok = (jnp.abs(kpos - qpos) <= WINDOW) & (kpos >= 0) & (i > 0)
    return start, jnp.where(ok, 0.0, NEG_INF).astype(F32)


def _attn_win_fwd(name, q, k, v, sink, cfg):
    T, G, Lc = q.shape[0], cfg['G'], cfg['Lc']

    def body(sink_ref, q_ref, k_ref, v_ref, o_ref, lse_ref):
        h, i = pl.program_id(0), pl.program_id(1)
        start, bias = _band(i, T, Lc)
        kc, vc = k_ref[0:Lc, :], v_ref[0:Lc, :]
        kb, vb = k_ref[pl.ds(start, BAND), :], v_ref[pl.ds(start, BAND), :]
        for g in range(G):
            sl = slice(g * HEAD_DIM, (g + 1) * HEAD_DIM)
            qg = q_ref[:, sl]
            sk = sink_ref[h * G + g]
            sc = _nt(qg, kc) * ATT_SCALE
            sb = _nt(qg, kb) * ATT_SCALE + bias
            m = jnp.maximum(jnp.maximum(jnp.max(sc, axis=1, keepdims=True), jnp.max(sb, axis=1, keepdims=True)), sk)
            pc, pb = jnp.exp(sc - m), jnp.exp(sb - m)
            l = jnp.sum(pc, axis=1, keepdims=True) + jnp.sum(pb, axis=1, keepdims=True) + jnp.exp(sk - m)
            o_ref[:, sl] = (_nn(_bf(pc), vc) + _nn(_bf(pb), vb)) / l
            lse_ref[:, sl] = jnp.broadcast_to(m + jnp.log(l), (ROW_TILE, HEAD_DIM))

    qs, kvs = _att_specs(T, G)
    return pl.pallas_call(
        body, name=name, grid=(cfg['NKV'], T // ROW_TILE),
        in_specs=[pl.BlockSpec(memory_space=pltpu.SMEM), qs, kvs, kvs], out_specs=[qs, qs],
        out_shape=[jax.ShapeDtypeStruct(q.shape, F32), jax.ShapeDtypeStruct(q.shape, F32)],
        compiler_params=_params())(sink, q, k, v)


def _attn_win_bwd(name, q, k, v, o, lse, dmix, sink, cfg):
    T, G, Lc = q.shape[0], cfg['G'], cfg['Lc']

    def body(sink_ref, q_ref, k_ref, v_ref, o_ref, lse_ref, do_ref, dq_ref, dk_ref, dv_ref, dsink_ref):
        h, i = pl.program_id(0), pl.program_id(1)
        start, bias = _band(i, T, Lc)
        kc, vc = k_ref[0:Lc, :], v_ref[0:Lc, :]
        kb, vb = k_ref[pl.ds(start, BAND), :], v_ref[pl.ds(start, BAND), :]

        @pl.when(i == 0)
        def _():
            dk_ref[...] = jnp.zeros_like(dk_ref)
            dv_ref[...] = jnp.zeros_like(dv_ref)
            dsink_ref[...] = jnp.zeros_like(dsink_ref)

        for g in range(G):
            sl = slice(g * HEAD_DIM, (g + 1) * HEAD_DIM)
            qg, do = q_ref[:, sl], do_ref[:, sl]
            lse = lse_ref[:, g * HEAD_DIM:g * HEAD_DIM + 1]
            delta = jnp.sum(do * o_ref[:, sl], axis=1, keepdims=True)
            pc = jnp.exp(_nt(qg, kc) * ATT_SCALE - lse)
            pb = jnp.exp(_nt(qg, kb) * ATT_SCALE + bias - lse)
            ps = jnp.exp(sink_ref[h * G + g] - lse)
            dob = _bf(do)
            dv_ref[0:Lc, :] += _tn(_bf(pc), dob)
            dv_ref[pl.ds(start, BAND), :] += _tn(_bf(pb), dob)
            dsc = _bf(pc * (_nt(dob, vc) - delta) * ATT_SCALE)
            dsb = _bf(pb * (_nt(dob, vb) - delta) * ATT_SCALE)
            dq_ref[:, sl] = _nn(dsc, kc) + _nn(dsb, kb)
            dk_ref[0:Lc, :] += _tn(dsc, qg)
            dk_ref[pl.ds(start, BAND), :] += _tn(dsb, qg)
            dsk = jnp.where(i > 0, -jnp.sum(ps * delta, axis=0, keepdims=True), 0.0)
            dsink_ref[:, sl] += jnp.broadcast_to(dsk, (8, HEAD_DIM))

    qs, kvs = _att_specs(T, G)
    return pl.pallas_call(
        body, name=name, grid=(cfg['NKV'], T // ROW_TILE),
        in_specs=[pl.BlockSpec(memory_space=pltpu.SMEM), qs, kvs, kvs, qs, qs, qs],
        out_specs=[qs, kvs, kvs, pl.BlockSpec((None, 8, G * HEAD_DIM), lambda h, i: (h, 0, 0))],
        out_shape=[jax.ShapeDtypeStruct(q.shape, F32), jax.ShapeDtypeStruct(k.shape, F32), jax.ShapeDtypeStruct(k.shape, F32),
                   jax.ShapeDtypeStruct((cfg['NKV'], 8, G * HEAD_DIM), F32)],
        compiler_params=_params())(sink, q, k, v, o, lse, dmix)


def _seq_pos(T, Lc):
    row = lax.broadcasted_iota(jnp.int32, (T, 1), 0)
    return jnp.where(row < Lc, row, row - Lc), jnp.where(row < Lc, Lc, T - Lc)


def _fw(x, k, pos, seglen):
    return jnp.where(pos + k < seglen, pltpu.roll(x, x.shape[0] - k, 0), 0.0)


def _bw(x, k, pos):
    return jnp.where(pos - k >= 0, pltpu.roll(x, k, 0), 0.0)


def _conv_fwd(name, P, conv_w8, cfg):
    T, Lc = P.shape[0], cfg['Lc']
    cb = (cfg['ATT'] + 2 * cfg['KVW']) // HEAD_DIM
    na = AUX_WIDTH // HEAD_DIM

    def body(gb_ref, gc_ref, u_ref, w_ref, o_ref):
        pos, seglen = _seq_pos(T, Lc)
        z = gc_ref[...] * u_ref[...]
        conv = w_ref[0:1, :] * _bw(z, 1, pos) + w_ref[1:2, :] * z + w_ref[2:3, :] * _fw(z, 1, pos, seglen)
        o_ref[...] = gb_ref[...] * conv

    col = lambda off: pl.BlockSpec((T, HEAD_DIM), lambda c: (0, cb + off + c))
    return pl.pallas_call(
        body, name=name, grid=(na,),
        in_specs=[col(0), col(na), col(2 * na), pl.BlockSpec((8, HEAD_DIM), lambda c: (0, c))],
        out_specs=pl.BlockSpec((T, HEAD_DIM), lambda c: (0, c)),
        out_shape=jax.ShapeDtypeStruct((T, AUX_WIDTH), F32), compiler_params=_params())(P, P, P, conv_w8)


def _conv_bwd(name, P, conv_w8, dmix, cfg):
    T, Lc = P.shape[0], cfg['Lc']
    cb = (cfg['ATT'] + 2 * cfg['KVW']) // HEAD_DIM
    ob = cfg['ATT'] // HEAD_DIM
    na = AUX_WIDTH // HEAD_DIM

    def body(gb_ref, gc_ref, u_ref, w_ref, do_ref, dgb_ref, dgc_ref, du_ref, dw_ref):
        pos, seglen = _seq_pos(T, Lc)
        gc, u, do = gc_ref[...], u_ref[...], do_ref[...]
        z = gc * u
        zm, zp = _bw(z, 1, pos), _fw(z, 1, pos, seglen)
        w0, w1, w2 = w_ref[0:1, :], w_ref[1:2, :], w_ref[2:3, :]
        dgb_ref[...] = (do * (w0 * zm + w1 * z + w2 * zp)).astype(dgb_ref.dtype)
        dc = do * gb_ref[...]
        dz = w0 * _fw(dc, 1, pos, seglen) + w1 * dc + w2 * _bw(dc, 1, pos)
        dgc_ref[...] = (dz * u).astype(dgc_ref.dtype)
        du_ref[...] = (dz * gc).astype(du_ref.dtype)
        dw_ref[...] = jnp.zeros_like(dw_ref)
        dw_ref[0:1, :] = jnp.sum(dc * zm, axis=0, keepdims=True)
        dw_ref[1:2, :] = jnp.sum(dc * z, axis=0, keepdims=True)
        dw_ref[2:3, :] = jnp.sum(dc * zp, axis=0, keepdims=True)

    col = lambda off: pl.BlockSpec((T, HEAD_DIM), lambda c: (0, cb + off + c))
    wspec = pl.BlockSpec((8, HEAD_DIM), lambda c: (0, c))
    ocol = lambda off: pl.BlockSpec((T, HEAD_DIM), lambda c: (0, off + c))
    return pl.pallas_call(
        body, name=name, grid=(na,),
        in_specs=[col(0), col(na), col(2 * na), wspec, ocol(ob)],
        out_specs=[ocol(0), ocol(0), ocol(0), wspec],
        out_shape=[jax.ShapeDtypeStruct((T, AUX_WIDTH), BF16)] * 3 + [jax.ShapeDtypeStruct((8, AUX_WIDTH), F32)],
        compiler_params=_params())(P, P, P, conv_w8, dmix)


def _window_sums(x, half, pos, seglen):
    fwd, bwd = x, x
    s = 1
    while s < half:
        fwd = fwd + _fw(fwd, s, pos, seglen)
        bwd = bwd + _bw(bwd, s, pos)
        s *= 2
    return fwd, bwd


def _pooled(u, half, pos, seglen):
    fwd, bwd = _window_sums(u, half, pos, seglen)
    cnt = (jnp.minimum(pos + half, seglen) - jnp.maximum(pos - half, 0)).astype(F32)
    return (fwd + _bw(bwd, 1, pos)) / cnt - u, cnt


def _pool_fwd(name, P, pool_w, pool_scale, cfg):
    T, Lc = P.shape[0], cfg['Lc']
    cb = (cfg['ATT'] + 2 * cfg['KVW']) // HEAD_DIM

    def body(u_ref, w_ref, s_ref, o_ref):
        g = pl.program_id(0)
        pos, seglen = _seq_pos(T, Lc)
        for k, half in enumerate(POOL_HALF):
            @pl.when(g == k)
            def _(half=half):
                pooled, _ = _pooled(u_ref[...], half, pos, seglen)
                o_ref[...] = _nn(_bf(pooled), _bf(w_ref[...])) * s_ref[...]

    return pl.pallas_call(
        body, name=name, grid=(AUX_GROUPS,),
        in_specs=[pl.BlockSpec((T, HEAD_DIM), lambda g: (0, cb + g)), pl.BlockSpec((None, HEAD_DIM, HEAD_DIM), lambda g: (g, 0, 0)),
                  pl.BlockSpec((1, HEAD_DIM), lambda g: (0, g))],
        out_specs=pl.BlockSpec((T, HEAD_DIM), lambda g: (0, g)),
        out_shape=jax.ShapeDtypeStruct((T, AUX_WIDTH), F32), compiler_params=_params())(P, pool_w, pool_scale)


def _pool_bwd(name, P, pool_w, pool_scale, dmix, cfg):
    T, Lc = P.shape[0], cfg['Lc']
    cb = (cfg['ATT'] + 2 * cfg['KVW']) // HEAD_DIM
    ob = cfg['ATT'] // HEAD_DIM

    def body(u_ref, w_ref, s_ref, do_ref, du_ref, dw_ref, ds_ref):
        g = pl.program_id(0)
        pos, seglen = _seq_pos(T, Lc)
        for k, half in enumerate(POOL_HALF):
            @pl.when(g == k)
            def _(half=half):
                do = do_ref[...]
                pooled, cnt = _pooled(u_ref[...], half, pos, seglen)
                wb = _bf(w_ref[...])
                mixed = _nn(_bf(pooled), wb)
                ds_ref[...] = jnp.broadcast_to(jnp.sum(do * mixed, axis=0, keepdims=True), ds_ref.shape)
                dmixed = _bf(do * s_ref[...])
                dw_ref[...] = _tn(_bf(pooled), dmixed)
                dpooled = _nt(dmixed, wb)
                e = dpooled / cnt
                fwd, bwd = _window_sums(e, half, pos, seglen)
                adj = fwd + _fw(e, half, pos, seglen) + _bw(bwd, 1, pos) - _bw(e, half, pos)
                du_ref[...] = (adj - dpooled).astype(du_ref.dtype)

    wspec = pl.BlockSpec((None, HEAD_DIM, HEAD_DIM), lambda g: (g, 0, 0))
    return pl.pallas_call(
        body, name=name, grid=(AUX_GROUPS,),
        in_specs=[pl.BlockSpec((T, HEAD_DIM), lambda g: (0, cb + g)), wspec, pl.BlockSpec((1, HEAD_DIM), lambda g: (0, g)),
                  pl.BlockSpec((T, HEAD_DIM), lambda g: (0, ob + g))],
        out_specs=[pl.BlockSpec((T, HEAD_DIM), lambda g: (0, g)), wspec, pl.BlockSpec((8, HEAD_DIM), lambda g: (0, g))],
        out_shape=[jax.ShapeDtypeStruct((T, AUX_WIDTH), BF16), jax.ShapeDtypeStruct(pool_w.shape, F32),
                   jax.ShapeDtypeStruct((8, AUX_WIDTH), F32)],
        compiler_params=_params())(P, pool_w, pool_scale, dmix)


def _ffn_up(name, hn, wg3, wu3):
    T, D = hn.shape
    J, k, _ = wg3.shape
    tm = _row_tile(T, 1088)

    def body(x_ref, wg_ref, wu_ref, g_ref, u_ref, a_ref):
        x = x_ref[...]
        g, u = _nt(x, wg_ref[...]), _nt(x, wu_ref[...])
        g_ref[...] = g
        u_ref[...] = u
        a_ref[...] = (g * jax.nn.sigmoid(g) * u).astype(a_ref.dtype)

    wspec = pl.BlockSpec((None, k, D), lambda j, i: (j, 0, 0))
    ospec = pl.BlockSpec((None, tm, k), lambda j, i: (j, i, 0))
    return pl.pallas_call(
        body, name=name, grid=(J, T // tm), in_specs=[pl.BlockSpec((tm, D), lambda j, i: (i, 0)), wspec, wspec],
        out_specs=[ospec, ospec, ospec],
        out_shape=[jax.ShapeDtypeStruct((J, T, k), F32), jax.ShapeDtypeStruct((J, T, k), F32), jax.ShapeDtypeStruct((J, T, k), BF16)],
        compiler_params=_params())(hn, wg3, wu3)


def _ffn_dact(name, dF, wd3, G, U):
    T, D = dF.shape
    J, k, _ = wd3.shape
    tm = _row_tile(T, 1088)

    def body(df_ref, wd_ref, g_ref, u_ref, dg_ref, du_ref):
        da = _nt(df_ref[...], wd_ref[...])
        g = g_ref[...]
        sig = jax.nn.sigmoid(g)
        du_ref[...] = (da * g * sig).astype(du_ref.dtype)
        dg_ref[...] = (da * u_ref[...] * (sig * (1 + g * (1 - sig)))).astype(dg_ref.dtype)

    aspec = pl.BlockSpec((None, tm, k), lambda j, i: (j, i, 0))
    return pl.pallas_call(
        body, name=name, grid=(J, T // tm),
        in_specs=[pl.BlockSpec((tm, D), lambda j, i: (i, 0)), pl.BlockSpec((None, k, D), lambda j, i: (j, 0, 0)), aspec, aspec],
        out_specs=[aspec, aspec],
        out_shape=[jax.ShapeDtypeStruct((J, T, k), BF16), jax.ShapeDtypeStruct((J, T, k), BF16)],
        compiler_params=_params())(dF, wd3, G, U)


def _loss_head(name, h, g, target, cfg):
    T, D = h.shape

    def body(h_ref, g_ref, t_ref, dh_ref, loss_ref, dg_ref):
        i = pl.program_id(0)

        @pl.when(i == 0)
        def _():
            dh_ref[...] = jnp.zeros_like(dh_ref)
            loss_ref[...] = jnp.zeros_like(loss_ref)
            dg_ref[...] = jnp.zeros_like(dg_ref)

        @pl.when(i > 0)
        def _():
            x = h_ref[...]
            r = _rstd(x)
            xhat = x * r
            gg = g_ref[...]
            err = xhat * gg - t_ref[...]
            loss_ref[...] += 0.5 * jnp.sum(jnp.sum(err * err, axis=1, keepdims=True) / D, axis=0, keepdims=True)
            dy = err / D
            dg_ref[0:1, :] += jnp.sum(dy * xhat, axis=0, keepdims=True)
            dxh = dy * gg
            dh_ref[...] = r * (dxh - xhat * jnp.mean(dxh * xhat, axis=-1, keepdims=True))

    row = pl.BlockSpec((ROW_TILE, D), lambda i: (i, 0))
    return pl.pallas_call(
        body, name=name, grid=(T // ROW_TILE,),
        in_specs=[row, pl.BlockSpec((1, D), lambda i: (0, 0)), pl.BlockSpec((ROW_TILE, D), lambda i: (jnp.maximum(i - 1, 0), 0))],
        out_specs=[row, pl.BlockSpec((8, 128), lambda i: (0, 0)), pl.BlockSpec((8, D), lambda i: (0, 0))],
        out_shape=[jax.ShapeDtypeStruct((T, D), F32), jax.ShapeDtypeStruct((8, 128), F32), jax.ShapeDtypeStruct((8, D), F32)],
        compiler_params=_params())(h, g, target)


def _adamw(name, parts, w, m, v):
    R, C = w.shape
    n_parts = parts.shape[0]
    tr = _row_tile(R, max(16, (1 << 18) // C)) if R % 16 == 0 else R
    bc1 = 1.0 - ADAM_B1 ** ADAM_STEP
    bc2 = 1.0 - ADAM_B2 ** ADAM_STEP

    def body(p_ref, w_ref, m_ref, v_ref, g_ref, d_ref, nm_ref, nv_ref):
        g = p_ref[0].astype(F32)
        for k in range(1, n_parts):
            g = g + p_ref[k].astype(F32)
        nm = ADAM_B1 * m_ref[...] + (1.0 - ADAM_B1) * g
        nv = ADAM_B2 * v_ref[...] + (1.0 - ADAM_B2) * (g * g)
        g_ref[...] = g
        nm_ref[...] = nm
        nv_ref[...] = nv
        d_ref[...] = -ADAM_LR * ((nm / bc1) / (jnp.sqrt(nv / bc2) + ADAM_EPS) + ADAM_WD * w_ref[...])

    blk = pl.BlockSpec((tr, C), lambda i: (i, 0))
    return pl.pallas_call(
        body, name=name, grid=(R // tr,), in_specs=[pl.BlockSpec((n_parts, tr, C), lambda i: (0, i, 0)), blk, blk, blk],
        out_specs=[blk] * 4, out_shape=[jax.ShapeDtypeStruct((R, C), F32)] * 4, compiler_params=_params())(parts, w, m, v)


class _WeightStream:
    def __init__(self, cast):
        self.cast, self.handles = cast, {}

    @staticmethod
    def _tag(l, group):
        return ("mixer" if group is MIXER_WEIGHTS else "ffn") + str(l)

    def start(self, l, group, after=None):
        self.handles[l, group], token = _exchange_start(
            f"gather_{self._tag(l, group)}_start", [self.cast(l, n) for n in group], False, after)
        return token

    def get(self, l, group, after):
        got = dict(zip(group, _exchange_wait(f"gather_{self._tag(l, group)}_wait", self.handles[l, group], after)))
        if 'w_out' in got:
            rows, cols = got['w_out'].shape[1:]
            got['w_out'] = got['w_out'].reshape(N_DEV * rows, cols)
        return got


def _layer_fwd(l, h, p, stream, mod, rope, conv_w8, cfg):
    nm = f"l{l}_"
    xn = _norm_mod(nm + "norm1", h, p['norm1_g'], mod, 0)
    W = stream.get(l, MIXER_WEIGHTS, xn)
    if l == 0:
        stream.start(0, FFN_WEIGHTS, after=W['w_in'])
    P = _mm_cols(nm + "w_in", xn, W['w_in'])
    qr, kr, vb = _qk_prep(nm + "qk_prep", P, p['q_norm_g'], p['k_norm_g'], rope[0], rope[1], cfg)
    if l == 0:
        o, lse = _attn_dense_fwd(nm + "attn", qr, kr, vb, cfg)
        aux = _conv_fwd(nm + "conv", P, conv_w8, cfg)
    else:
        o, lse = _attn_win_fwd(nm + "attn", qr, kr, vb, p['sink'], cfg)
        aux = _pool_fwd(nm + "pool", P, p['pool_w'], p['pool_scale'], cfg)
    mix = jnp.concatenate([o, aux], axis=1).astype(BF16)
    y = _mm_plain(nm + "w_out", mix, W['w_out'], False)
    h2 = _gate_res(nm + "res1", h, y, mod, 2)
    hn = _norm_mod(nm + "norm2", h2, p['norm2_g'], mod, 1)
    W.update(stream.get(l, FFN_WEIGHTS, hn))
    if l == 0:
        stream.start(1, MIXER_WEIGHTS, after=W['w_down'])
    G, U, A = _ffn_up(nm + "ffn_up", hn, W['w_gate'], W['w_up'])
    if l == 0:
        stream.start(1, FFN_WEIGHTS, after=A)
    F = _mm_shards_nn(nm + "w_down", A, W['w_down'])
    h3 = _gate_res(nm + "res2", h2, F, mod, 5)
    saved = dict(h=h, xn=xn, P=P, qr=qr, kr=kr, vb=vb, o=o, lse=lse, mix=mix, y=y, h2=h2, hn=hn, G=G, U=U, A=A, F=F)
    return h3, saved, W


def _layer_bwd(l, dh3, s, p, W, mod, rope, conv_w8, cfg):
    nm = f"l{l}_bwd_"
    J = N_DEV
    dF, dmod = _gate_bwd(nm + "res2", dh3, s['F'], mod, 5)
    dG, dU = _ffn_dact(nm + "ffn_act", dF, W['w_down'], s['G'], s['U'])
    big = {'w_down': _wgrad_down(nm + "dw_down", s['A'], dF),
           'w_gate': _wgrad_down(nm + "dw_gate", dG, s['hn']),
           'w_up': _wgrad_down(nm + "dw_up", dU, s['hn'])}
    handles = {}
    handles['ffn'], token = _exchange_start(f"scatter_ffn{l}_start", [big[n] for n in FFN_WEIGHTS], True)
    mod = mod + token[0, 0]
    dhn = _mm_shards_nn2(nm + "dhn", dG, W['w_gate'], dU, W['w_up'])
    dh2, dm, dg2 = _norm_mod_bwd(nm + "norm2", dhn, s['h2'], p['norm2_g'], mod, 1, dh3)
    dmod += dm
    dY, dm = _gate_bwd(nm + "res1", dh2, s['y'], mod, 2)
    dmod += dm
    dwo = _wgrad_rows(nm + "dw_out", s['mix'], dY)
    handles['w_out'], token = _exchange_start(f"scatter_w_out{l}_start", [dwo.reshape((J, dwo.shape[0] // J, dwo.shape[1]))], True)
    mod = mod + token[0, 0]
    dmix = _mm_plain(nm + "dmix", dY, W['w_out'], True)
    small = {'norm2_g': dg2[0]}
    if l == 0:
        dqr, dkr, dv = _attn_dense_bwd(nm + "attn", s['qr'], s['kr'], s['vb'], s['o'], s['lse'], dmix, cfg)
        *daux, dcw = _conv_bwd(nm + "conv", s['P'], conv_w8, dmix, cfg)
        small['conv_w'] = dcw[0:3]
    else:
        dqr, dkr, dv, dsk = _attn_win_bwd(nm + "attn", s['qr'], s['kr'], s['vb'], s['o'], s['lse'], dmix, p['sink'], cfg)
        du, dpw, dps = _pool_bwd(nm + "pool", s['P'], p['pool_w'], p['pool_scale'], dmix, cfg)
        daux = [du]
        small.update(sink=dsk[:, 0, ::HEAD_DIM].reshape(-1), pool_w=dpw, pool_scale=dps[0])
    dq, dk, dqg, dkg = _qk_prep_bwd(nm + "qk_prep", dqr, dkr, s['P'], p['q_norm_g'], p['k_norm_g'], rope[0], rope[1], cfg)
    small.update(q_norm_g=dqg[0], k_norm_g=dkg[0])
    dP = jnp.concatenate([dq, dk, dv.astype(BF16), *daux], axis=1)
    handles['w_in'], token = _exchange_start(f"scatter_w_in{l}_start", [_wgrad_cols(nm + "dw_in", s['xn'], dP, J)], True)
    dxn = _mm_cols_nt(nm + "dxn", dP, W['w_in'])
    dh, dm, dg1 = _norm_mod_bwd(nm + "norm1", dxn, s['h'], p['norm1_g'], mod + token[0, 0], 0, dh2)
    dmod += dm
    small['norm1_g'] = dg1[0]
    return dh, dmod, small, handles, token


def _rope_tables(S, Lc):
    half = HEAD_DIM // 4
    pos = np.arange(S)
    inv = ROPE_THETA ** (-np.arange(0, 2 * half, 2, dtype=np.float32) / (2 * half))
    inv = jnp.asarray(inv, F32)
    ang_r = jnp.asarray(pos // GRID_W, F32)[:, None] * inv
    ang_c = jnp.asarray(pos % GRID_W, F32)[:, None] * inv
    cos = jnp.concatenate([jnp.cos(ang_r)] * 2 + [jnp.cos(ang_c)] * 2, axis=1)
    sin = jnp.concatenate([-jnp.sin(ang_r), jnp.sin(ang_r), -jnp.sin(ang_c), jnp.sin(ang_c)], axis=1)
    return (jnp.concatenate([jnp.ones((Lc, HEAD_DIM), F32), cos], axis=0),
            jnp.concatenate([jnp.zeros((Lc, HEAD_DIM), F32), sin], axis=0))


def _pad_rows(a, rows):
    return jnp.concatenate([a, jnp.zeros((rows - a.shape[0],) + a.shape[1:], a.dtype)], axis=0)


def _flat128(a, nlead):
    lead = a.shape[:nlead]
    f = a.reshape(lead + (-1,))
    pad = (-f.shape[-1]) % 128
    if pad:
        f = jnp.concatenate([f, jnp.zeros(lead + (pad,), f.dtype)], axis=-1)
    return f.reshape(lead + (-1, 128))


def _pack(named, nlead=0):
    rows, layout, at = [], {}, 0
    for name, a in named:
        f = _flat128(a, nlead)
        n = f.shape[-2]
        pad = (-n) % 8
        if pad:
            f = jnp.concatenate([f, jnp.zeros(f.shape[:-2] + (pad, 128), f.dtype)], axis=-2)
        layout[name] = (at, n, a.shape[nlead:])
        rows.append(f)
        at += n + pad
    return jnp.concatenate(rows, axis=-2), layout


def _unpack(arr, layout, name):
    at, n, shape = layout[name]
    return arr[..., at:at + n, :].reshape(arr.shape[:-2] + (-1,))[..., :math.prod(shape)].reshape(arr.shape[:-2] + tuple(shape))


def kernel(*args):
    A = dict(zip(INPUT_NAMES, args, strict=True))
    x, ctx = A['x'][0], A['ctx'][0]
    S, D = x.shape
    Lc = ctx.shape[0]
    T = Lc + S
    ATT = D - AUX_WIDTH
    KVW = (A['l1_w_in'].shape[1] * N_DEV - ATT - AUX_WIDTH) // 2
    cfg = dict(ATT=ATT, KVW=KVW, NKV=KVW // HEAD_DIM, G=ATT // KVW, Lc=Lc)
    assert Lc == ROW_TILE and S % ROW_TILE == 0 and T >= BAND and S % GRID_W == 0
    cw = A['l0_conv_w'].shape[1]
    me = 4 * lax.axis_index("x") + 2 * lax.axis_index("y") + lax.axis_index("c")

    def layer_params(l):
        pre = f"l{l}_"
        return {k[len(pre):]: (v.reshape(1, -1) if v.ndim == 1 and k != 'l1_sink' else v) for k, v in A.items() if k.startswith(pre)}

    params = [layer_params(0), layer_params(1)]

    def cast(l, n):
        w = A[f'l{l}_{n}']
        return (w.T if n in TRANSPOSED else w).astype(BF16)

    stream = _WeightStream(cast)
    token = stream.start(0, MIXER_WEIGHTS)

    sc_own = jax.nn.silu(A['c']) + token[0, 0]
    first, lay0 = _pack([('sc', sc_own), ('conv_w', A['l0_conv_w'])])
    first_all = _exchange("gather_cond", [first], False)[0]
    sc_all = _unpack(first_all, lay0, 'sc')[:, 0]
    conv_w = _unpack(first_all, lay0, 'conv_w').transpose(1, 0, 2).reshape(3, N_DEV * cw)
    conv_w8 = _pad_rows(conv_w, 8)
    sc_ctx = jax.nn.silu(A['c_ctx'])
    s16 = _pad_rows(jnp.concatenate([sc_all, sc_ctx[None]], axis=0), 16)

    nmod = A['l0_w_mod'].shape[1]
    modp = jnp.concatenate([_mm_plain(f"l{l}_mod", s16, A[f'l{l}_w_mod'], False) for l in range(2)], axis=1)
    modp_all = _exchange("gather_mod", [modp], False)[0]
    mods = []
    for l in range(2):
        full = modp_all[:, :, l * nmod:(l + 1) * nmod].transpose(1, 0, 2).reshape(16, N_MOD * D) + A[f'l{l}_b_mod'][None]
        both = jnp.stack([full[8], lax.dynamic_index_in_dim(full, me, 0, keepdims=False)]).reshape(2, N_MOD, D)
        mods.append(jnp.concatenate([both, jnp.zeros((2, 8 - N_MOD, D), F32)], axis=1))

    rope = _rope_tables(S, Lc)
    h = jnp.concatenate([ctx, x], axis=0)
    saved, W = [], []
    for l in range(2):
        h, s, Wl = _layer_fwd(l, h, params[l], stream, mods[l], rope, conv_w8, cfg)
        saved.append(s)
        W.append(Wl)

    dh, loss_blk, dgf = _loss_head("loss_head", h, A['final_norm_g'].reshape(1, -1), A['loss_target'][0], cfg)
    loss = lax.psum(loss_blk[0, 0], ("x", "y", "c"))

    grads, small, dmods, scatters = {}, {'final_norm_g': dgf[0]}, [None, None], [None, None]
    token = jnp.zeros((8, 128), F32)
    for l in (1, 0):
        dh, dmods[l], sm, scatters[l], token = _layer_bwd(l, dh, saved[l], params[l], W[l], mods[l] + token[0, 0], rope, conv_w8, cfg)
        small.update({f'l{l}_{k}': v for k, v in sm.items()})
    grad_x = dh[Lc:][None]

    def landed(l, key, after):
        group = FFN_WEIGHTS if key == 'ffn' else (key,)
        for n, parts in zip(group, _exchange_wait(f"scatter_{key}{l}_wait", scatters[l][key], after)):
            shape = A[f'l{l}_{n}'].shape
            grads[f'l{l}_{n}'] = (parts.reshape((N_DEV,) + (shape[::-1] if n in TRANSPOSED else shape)), None)

    small_names = [n for n in WEIGHT_NAMES if n in small]
    pieces = [(n, small[n]) for n in small_names]
    for l in range(2):
        pieces += [(f'dmod{l}', dmods[l][1, :N_MOD]), (f'dcmod{l}', dmods[l][0, :N_MOD])]
    second, lay1 = _pack(pieces)
    second_all = _exchange("gather_small", [second], False)[0]

    dsc_part = jnp.zeros((16, D), F32)
    for l in range(2):
        dm16 = _pad_rows(jnp.concatenate([_unpack(second_all, lay1, f'dmod{l}').reshape(N_DEV, N_MOD * D),
                                          jnp.sum(_unpack(second_all, lay1, f'dcmod{l}'), axis=0).reshape(1, N_MOD * D)], axis=0), 16)
        mine = lax.dynamic_slice_in_dim(dm16, me * nmod, nmod, axis=1)
        tk = _col_tile(D, 512)
        gw = _mm_tn(f"l{l}_dw_mod", s16, mine, pl.BlockSpec((16, tk), lambda kb, t: (0, kb)), pl.BlockSpec((16, nmod), lambda kb, t: (0, 0)),
                    jax.ShapeDtypeStruct((D, nmod), F32), pl.BlockSpec((tk, nmod), lambda kb, t: (kb, 0)), (tk, nmod), (D // tk, 1))
        grads[f'l{l}_w_mod'] = (gw[None], None)
        dsc_part += _mm_plain(f"l{l}_dsc", mine, A[f'l{l}_w_mod'], True)
        dmod_dev = _unpack(second_all, lay1, f'dmod{l}') + _unpack(second_all, lay1, f'dcmod{l}')
        grads[f'l{l}_b_mod'] = (dmod_dev.reshape(N_DEV, N_MOD * D), None)
    dsig = jax.nn.sigmoid(A['c_ctx'])
    dsilu = dsig * (1 + A['c_ctx'] * (1 - dsig))
    third_all = _exchange("gather_dsc", [dsc_part[8:9]], False)[0]
    grads['c_ctx'] = (third_all[:, 0] * dsilu[None], None)
    for n in small_names:
        g8 = _unpack(second_all, lay1, n)
        if n == 'l0_conv_w':
            g8 = lax.dynamic_slice_in_dim(g8, me * cw, cw, axis=2)
        grads[n] = (g8, None)

    out = {}
    big_names = [n for n in WEIGHT_NAMES if n[3:] in BIG_WEIGHTS + ('w_mod',)]

    def adam(n):
        w, m, v = A[n], A['m_' + n], A['v_' + n]
        if n[3:] in TRANSPOSED:
            out[n] = tuple(r.T for r in _adamw("adamw_" + n, grads[n][0], w.T, m.T, v.T))
        else:
            out[n] = _adamw("adamw_" + n, grads[n][0], w, m, v)
        return out[n][1]

    last = third_all
    for n in ['l0_w_mod', 'l1_w_mod']:
        adam(n)
    for l in (1, 0):
        for key in ('ffn', 'w_out', 'w_in'):
            landed(l, key, last)
            for n in (FFN_WEIGHTS if key == 'ffn' else (key,)):
                last = adam(f'l{l}_{n}')
    rest = [n for n in WEIGHT_NAMES if n not in big_names]
    wp, layw = _pack([(n, A[n]) for n in rest])
    mp, _ = _pack([(n, A['m_' + n]) for n in rest])
    vp, _ = _pack([(n, A['v_' + n]) for n in rest])
    gp, _ = _pack([(n, grads[n][0]) for n in rest], nlead=1)
    res = _adamw("adamw_small", gp, wp, mp, vp)
    for n in rest:
        out[n] = tuple(_unpack(r, layw, n) for r in res)

    outs = [loss, grad_x]
    for k in range(4):
        outs += [out[n][k] for n in WEIGHT_NAMES]
    return tuple(outs)
```

```python
import functools
import math

import numpy as np
import jax
import jax.numpy as jnp
from jax import lax
from jax.experimental import pallas as pl
from jax.experimental.pallas import tpu as pltpu

F32 = jnp.float32
BF16 = jnp.bfloat16
HEAD_DIM = 128
AUX_WIDTH = 512
AUX_GROUPS = 4
POOL_HALF = (1, 2, 4, 8)
WINDOW = 128
GRID_W = 64
ROPE_THETA = 10000.0
EPS = 1e-6
NEG_INF = -1e30
ATT_SCALE = HEAD_DIM ** -0.5
N_MOD = 6
N_DEV = 8
ROW_TILE = 256
BAND = ROW_TILE + 2 * WINDOW
ADAM_LR, ADAM_B1, ADAM_B2, ADAM_EPS, ADAM_WD, ADAM_STEP = 0.001, 0.9, 0.999, 1e-08, 0.01, 10
VMEM_LIMIT_MB = 56
MESH = pl.DeviceIdType.MESH

WEIGHT_NAMES = ['c_ctx', 'l0_norm1_g', 'l0_w_mod', 'l0_b_mod', 'l0_w_in', 'l0_q_norm_g', 'l0_k_norm_g', 'l0_conv_w', 'l0_w_out', 'l0_norm2_g', 'l0_w_gate', 'l0_w_up', 'l0_w_down', 'l1_norm1_g', 'l1_w_mod', 'l1_b_mod', 'l1_w_in', 'l1_q_norm_g', 'l1_k_norm_g', 'l1_sink', 'l1_pool_w', 'l1_pool_scale', 'l1_w_out', 'l1_norm2_g', 'l1_w_gate', 'l1_w_up', 'l1_w_down', 'final_norm_g']
INPUT_NAMES = (['x', 'c', 'ctx'] + WEIGHT_NAMES + ['loss_target'] + ['m_' + n for n in WEIGHT_NAMES]
               + ['v_' + n for n in WEIGHT_NAMES])
MIXER_WEIGHTS = ('w_out', 'w_in')
FFN_WEIGHTS = ('w_down', 'w_gate', 'w_up')
TRANSPOSED = ('w_gate', 'w_up')
BIG_WEIGHTS = MIXER_WEIGHTS + FFN_WEIGHTS


def _params(vmem_mb=VMEM_LIMIT_MB):
    return pltpu.CompilerParams(vmem_limit_bytes=vmem_mb << 20)


def _row_tile(n, cap):
    best = None
    for t in range(16, min(n, cap) + 1, 16):
        if n % t == 0:
            best = t
    assert best is not None, (n, cap)
    return best


def _col_tile(n, cap):
    best = n
    for t in range(128, min(n, cap) + 1, 128):
        if n % t == 0:
            best = t
    return best if best <= cap or n % 128 else n


def _dot(a, b, ca, cb):
    return lax.dot_general(a, b, (((ca,), (cb,)), ((), ())), preferred_element_type=F32)


def _nn(a, b):
    return _dot(a, b, 1, 0)


def _nt(a, b):
    return _dot(a, b, 1, 1)


def _tn(a, b):
    return _dot(a, b, 0, 0)


def _bf(x):
    return x.astype(BF16)


def _exchange(name, arrs, scatter):
    n = len(arrs)
    if scatter:
        out_shape = [jax.ShapeDtypeStruct(a.shape, a.dtype) for a in arrs]
    else:
        out_shape = [jax.ShapeDtypeStruct((N_DEV,) + a.shape, a.dtype) for a in arrs]

    def body(*refs):
        ins, outs = refs[:n], refs[n:2 * n]
        send_sems, recv_sems, local_sems = refs[2 * n:]
        x, y, c = lax.axis_index("x"), lax.axis_index("y"), lax.axis_index("c")
        me = 4 * x + 2 * y + c
        local, remote = [], []
        for a in range(n):
            own = ins[a].at[me] if scatter else ins[a]
            cp = pltpu.make_async_copy(own, outs[a].at[me], local_sems.at[a])
            cp.start()
            local.append(cp)
            for r in range(1, N_DEV):
                px = 1 - x if r & 4 else x
                py = 1 - y if r & 2 else y
                pc = 1 - c if r & 1 else c
                src = ins[a].at[4 * px + 2 * py + pc] if scatter else ins[a]
                cp = pltpu.make_async_remote_copy(
                    src_ref=src, dst_ref=outs[a].at[me], send_sem=send_sems.at[a, r - 1],
                    recv_sem=recv_sems.at[a, r - 1], device_id=(px, py, pc), device_id_type=MESH)
                cp.start()
                remote.append(cp)
        for cp in remote:
            cp.wait()
        for cp in local:
            cp.wait()

    any_spec = pl.BlockSpec(memory_space=pl.ANY)
    return pl.pallas_call(
        body, name=name, out_shape=out_shape,
        in_specs=[any_spec] * n, out_specs=[any_spec] * n,
        scratch_shapes=[pltpu.SemaphoreType.DMA((n, N_DEV - 1)), pltpu.SemaphoreType.DMA((n, N_DEV - 1)),
                        pltpu.SemaphoreType.DMA((n,))],
    )(*arrs)


HBM_SPEC = pl.BlockSpec(memory_space=pltpu.HBM)
SEM_SPEC = pl.BlockSpec(memory_space=pltpu.SEMAPHORE)
EFFECT = pltpu.SideEffectType.DATAFLOW_SIDE_EFFECTING


def _split_copies(srcs, lands, send_sems, recv_sems, local_sems, scatter):
    x, y, c = lax.axis_index("x"), lax.axis_index("y"), lax.axis_index("c")
    me = 4 * x + 2 * y + c
    local, remote = [], []
    for a in range(len(srcs)):
        own = srcs[a].at[me] if scatter else srcs[a]
        local.append(pltpu.make_async_copy(own, lands[a].at[me], local_sems.at[a]))
        for r in range(1, N_DEV):
            px = 1 - x if r & 4 else x
            py = 1 - y if r & 2 else y
            pc = 1 - c if r & 1 else c
            src = srcs[a].at[4 * px + 2 * py + pc] if scatter else srcs[a]
            remote.append(pltpu.make_async_remote_copy(
                src_ref=src, dst_ref=lands[a].at[me], send_sem=send_sems.at[a * (N_DEV - 1) + r - 1],
                recv_sem=recv_sems.at[a * (N_DEV - 1) + r - 1], device_id=(px, py, pc), device_id_type=MESH))
    return local, remote


def _exchange_start(name, arrs, scatter, after=None):
    n = len(arrs)
    extra = [] if after is None else [after]
    shapes = [a.shape if scatter else (N_DEV,) + a.shape for a in arrs]
    lands = [pltpu.with_memory_space_constraint(lax.empty(s, a.dtype), pltpu.HBM) for s, a in zip(shapes, arrs)]
    srcs = [pltpu.with_memory_space_constraint(a, pltpu.HBM) for a in arrs]

    def body(*refs):
        src_refs, land_refs = refs[:n], refs[n:2 * n]
        send_sems, recv_sems, local_sems = refs[2 * n + len(extra):2 * n + len(extra) + 3]
        token = refs[-1]
        local, remote = _split_copies(src_refs, land_refs, send_sems, recv_sems, local_sems, scatter)
        for cp in local + remote:
            cp.start()
        token[...] = jnp.zeros_like(token)

    res = pl.pallas_call(
        body, name=name,
        out_shape=[pltpu.SemaphoreType.DMA((n * (N_DEV - 1),)), pltpu.SemaphoreType.DMA((n * (N_DEV - 1),)), pltpu.SemaphoreType.DMA((n,))]
        + [pltpu.HBM(a.shape, a.dtype) for a in arrs] + [pltpu.HBM(s, a.dtype) for s, a in zip(shapes, arrs)]
        + [jax.ShapeDtypeStruct((8, 128), F32)],
        in_specs=[HBM_SPEC] * (2 * n) + [pl.BlockSpec(memory_space=pl.ANY)] * len(extra),
        out_specs=[SEM_SPEC] * 3 + [HBM_SPEC] * (2 * n) + [pl.BlockSpec(memory_space=pltpu.VMEM)],
        input_output_aliases={i: 3 + i for i in range(2 * n)},
        compiler_params=pltpu.CompilerParams(has_side_effects=EFFECT),
    )(*srcs, *lands, *extra)
    return (scatter, res[:3], res[3:3 + n], res[3 + n:3 + 2 * n]), res[-1]


def _exchange_wait(name, handle, after):
    scatter, sems, srcs, lands = handle
    n = len(srcs)

    def body(*refs):
        src_refs, land_refs = refs[:n], refs[n:2 * n]
        send_sems, recv_sems, local_sems = refs[2 * n:2 * n + 3]
        local, remote = _split_copies(src_refs, land_refs, send_sems, recv_sems, local_sems, scatter)
        for cp in remote:
            cp.wait_send()
            cp.wait_recv()
        for cp in local:
            cp.wait()

    res = pl.pallas_call(
        body, name=name,
        out_shape=[pltpu.HBM(a.shape, a.dtype) for a in srcs] + [pltpu.HBM(a.shape, a.dtype) for a in lands],
        in_specs=[HBM_SPEC] * (2 * n) + [SEM_SPEC] * 3 + [pl.BlockSpec(memory_space=pl.ANY)], out_specs=[HBM_SPEC] * (2 * n),
        input_output_aliases={i: i for i in range(2 * n)},
        compiler_params=pltpu.CompilerParams(has_side_effects=EFFECT),
    )(*srcs, *lands, *sems, after)
    return list(res[n:])


def _dep(dep, grid_rank):
    if dep is None:
        return [], []
    return [dep], [pl.BlockSpec((8, 128), (lambda i, j: (0, 0)) if grid_rank == 2 else (lambda i: (0, 0)))]


def _mm_step(name, fn, ins, in_specs, out_shape, out_spec, grid, dep=None):
    n = len(ins)
    dep_ins, dep_specs = _dep(dep, len(grid))

    def body(*refs):
        o_ref = refs[n + len(dep_ins)]
        o_ref[...] = fn(*refs[:n]).astype(o_ref.dtype)

    return pl.pallas_call(body, name=name, grid=grid, in_specs=list(in_specs) + dep_specs, out_specs=out_spec,
                          out_shape=out_shape, compiler_params=_params())(*ins, *dep_ins)


def _mm_tn(name, a, b, a_spec, b_spec, out_shape, out_spec, acc_shape, grid):
    nk = grid[-1]
    kax = len(grid) - 1

    def body(a_ref, b_ref, o_ref, acc_ref):
        k = pl.program_id(kax)

        @pl.when(k == 0)
        def _():
            acc_ref[...] = jnp.zeros_like(acc_ref)

        acc_ref[...] += _tn(_bf(a_ref[...]), _bf(b_ref[...]))

        @pl.when(k == nk - 1)
        def _():
            o_ref[...] = acc_ref[...].astype(o_ref.dtype)

    return pl.pallas_call(body, name=name, grid=grid, in_specs=[a_spec, b_spec], out_specs=out_spec,
                          out_shape=out_shape, scratch_shapes=[pltpu.VMEM(acc_shape, F32)],
                          compiler_params=_params())(a, b)


def _mm_cols(name, a, w3, out_dtype=F32, dep=None):
    M, K = a.shape
    J, _, n = w3.shape
    tm = _row_tile(M, 1088)
    return _mm_step(
        name, lambda a_ref, w_ref: _nn(_bf(a_ref[...]), w_ref[...]), [a, w3],
        [pl.BlockSpec((tm, K), lambda j, i: (i, 0)), pl.BlockSpec((None, K, n), lambda j, i: (j, 0, 0))],
        jax.ShapeDtypeStruct((M, J * n), out_dtype), pl.BlockSpec((tm, n), lambda j, i: (i, j)), (J, M // tm), dep)


def _mm_plain(name, a, b, transpose_b, out_dtype=F32, tn=512, dep=None):
    M, K = a.shape
    N = b.shape[0] if transpose_b else b.shape[1]
    tm = _row_tile(M, 1088)
    tn = _col_tile(N, tn)
    if transpose_b:
        b_spec = pl.BlockSpec((tn, K), lambda j, i: (j, 0))
        fn = lambda a_ref, b_ref: _nt(_bf(a_ref[...]), _bf(b_ref[...]))
    else:
        b_spec = pl.BlockSpec((K, tn), lambda j, i: (0, j))
        fn = lambda a_ref, b_ref: _nn(_bf(a_ref[...]), _bf(b_ref[...]))
    return _mm_step(name, fn, [a, b], [pl.BlockSpec((tm, K), lambda j, i: (i, 0)), b_spec],
                    jax.ShapeDtypeStruct((M, N), out_dtype), pl.BlockSpec((tm, tn), lambda j, i: (i, j)),
                    (N // tn, M // tm), dep)


def _mm_shards_nn(name, a3, w3, tn=512, dep=None):
    J, M, k = a3.shape
    N = w3.shape[2]
    tm = _row_tile(M, 544)
    tn = _col_tile(N, tn)

    def fn(a_ref, w_ref):
        acc = _nn(a_ref[0], w_ref[0])
        for j in range(1, J):
            acc += _nn(a_ref[j], w_ref[j])
        return acc

    return _mm_step(name, fn, [a3, w3],
                    [pl.BlockSpec((J, tm, k), lambda jn, i: (0, i, 0)), pl.BlockSpec((J, k, tn), lambda jn, i: (0, 0, jn))],
                    jax.ShapeDtypeStruct((M, N), F32), pl.BlockSpec((tm, tn), lambda jn, i: (i, jn)), (N // tn, M // tm), dep)


def _mm_shards_nn2(name, a3, w3a, b3, w3b, tn=512):
    J, M, k = a3.shape
    N = w3a.shape[2]
    tm = _row_tile(M, 544)
    tn = _col_tile(N, tn)

    def fn(a_ref, wa_ref, b_ref, wb_ref):
        acc = _nn(a_ref[0], wa_ref[0]) + _nn(b_ref[0], wb_ref[0])
        for j in range(1, J):
            acc += _nn(a_ref[j], wa_ref[j]) + _nn(b_ref[j], wb_ref[j])
        return acc

    act = pl.BlockSpec((J, tm, k), lambda jn, i: (0, i, 0))
    wsp = pl.BlockSpec((J, k, tn), lambda jn, i: (0, 0, jn))
    return _mm_step(name, fn, [a3, w3a, b3, w3b], [act, wsp, act, wsp],
                    jax.ShapeDtypeStruct((M, N), F32), pl.BlockSpec((tm, tn), lambda jn, i: (i, jn)), (N // tn, M // tm))


def _mm_cols_nt(name, a, w3, tn=512):
    M = a.shape[0]
    J, N, n = w3.shape
    tm = _row_tile(M, 544)
    tn = _col_tile(N, tn)

    def fn(a_ref, w_ref):
        acc = _nt(a_ref[:, 0:n], w_ref[0])
        for j in range(1, J):
            acc += _nt(a_ref[:, j * n:(j + 1) * n], w_ref[j])
        return acc

    return _mm_step(name, fn, [a, w3],
                    [pl.BlockSpec((tm, J * n), lambda jn, i: (i, 0)), pl.BlockSpec((J, tn, n), lambda jn, i: (0, jn, 0))],
                    jax.ShapeDtypeStruct((M, N), F32), pl.BlockSpec((tm, tn), lambda jn, i: (i, jn)), (N // tn, M // tm))


def _wgrad_cols(name, a, b, J):
    T, K = a.shape
    n = b.shape[1] // J
    tt = _row_tile(T, 1088)
    return _mm_tn(name, a, b, pl.BlockSpec((tt, K), lambda j, t: (t, 0)), pl.BlockSpec((tt, n), lambda j, t: (t, j)),
                  jax.ShapeDtypeStruct((J, K, n), BF16), pl.BlockSpec((None, K, n), lambda j, t: (j, 0, 0)), (K, n), (J, T // tt))


def _wgrad_rows(name, a, b, tk=512):
    T, K = a.shape
    N = b.shape[1]
    tt = _row_tile(T, 1088)
    tk = _col_tile(K, tk)
    return _mm_tn(name, a, b, pl.BlockSpec((tt, tk), lambda kb, t: (t, kb)), pl.BlockSpec((tt, N), lambda kb, t: (t, 0)),
                  jax.ShapeDtypeStruct((K, N), BF16), pl.BlockSpec((tk, N), lambda kb, t: (kb, 0)), (tk, N), (K // tk, T // tt))


def _wgrad_up(name, a, b3):
    T, K = a.shape
    J, _, k = b3.shape
    tt = _row_tile(T, 1088)
    return _mm_tn(name, a, b3, pl.BlockSpec((tt, K), lambda j, t: (t, 0)), pl.BlockSpec((None, tt, k), lambda j, t: (j, t, 0)),
                  jax.ShapeDtypeStruct((J, K, k), BF16), pl.BlockSpec((None, K, k), lambda j, t: (j, 0, 0)), (K, k), (J, T // tt))


def _wgrad_down(name, a3, b):
    J, T, k = a3.shape
    N = b.shape[1]
    tt = _row_tile(T, 1088)
    return _mm_tn(name, a3, b, pl.BlockSpec((None, tt, k), lambda j, t: (j, t, 0)), pl.BlockSpec((tt, N), lambda j, t: (t, 0)),
                  jax.ShapeDtypeStruct((J, k, N), BF16), pl.BlockSpec((None, k, N), lambda j, t: (j, 0, 0)), (k, N), (J, T // tt))


def _seg(i):
    return jnp.minimum(i, 1)


def _rstd(x):
    return lax.rsqrt(jnp.mean(x * x, axis=-1, keepdims=True) + EPS)


def _norm_mod(name, h, g, mod, which):
    T, D = h.shape

    def body(h_ref, g_ref, mod_ref, o_ref):
        x = h_ref[...]
        n = x * _rstd(x) * g_ref[...]
        shift = mod_ref[3 * which:3 * which + 1, :]
        scale = mod_ref[3 * which + 1:3 * which + 2, :]
        o_ref[...] = (n * (1 + scale) + shift).astype(o_ref.dtype)

    row = pl.BlockSpec((ROW_TILE, D), lambda i: (i, 0))
    return pl.pallas_call(
        body, name=name, grid=(T // ROW_TILE,),
        in_specs=[row, pl.BlockSpec((1, D), lambda i: (0, 0)), pl.BlockSpec((None, 8, D), lambda i: (_seg(i), 0, 0))],
        out_specs=row, out_shape=jax.ShapeDtypeStruct((T, D), BF16), compiler_params=_params())(h, g, mod)


def _norm_mod_bwd(name, dxn, h, g, mod, which, dres):
    T, D = h.shape

    def body(dxn_ref, h_ref, g_ref, mod_ref, dres_ref, dh_ref, dmod_ref, dg_ref):
        i = pl.program_id(0)
        x = h_ref[...]
        r = _rstd(x)
        xhat = x * r
        g = g_ref[...]
        n = xhat * g
        scale = mod_ref[3 * which + 1:3 * which + 2, :]
        dxn = dxn_ref[...]
        dn = dxn * (1 + scale)
        dxh = dn * g
        dh_ref[...] = dres_ref[...] + r * (dxh - xhat * jnp.mean(dxh * xhat, axis=-1, keepdims=True))

        @pl.when(i <= 1)
        def _():
            dmod_ref[...] = jnp.zeros_like(dmod_ref)

        @pl.when(i == 0)
        def _():
            dg_ref[...] = jnp.zeros_like(dg_ref)

        dmod_ref[3 * which:3 * which + 1, :] += jnp.sum(dxn, axis=0, keepdims=True)
        dmod_ref[3 * which + 1:3 * which + 2, :] += jnp.sum(dxn * n, axis=0, keepdims=True)
        dg_ref[0:1, :] += jnp.sum(dn * xhat, axis=0, keepdims=True)

    row = pl.BlockSpec((ROW_TILE, D), lambda i: (i, 0))
    modspec = pl.BlockSpec((None, 8, D), lambda i: (_seg(i), 0, 0))
    return pl.pallas_call(
        body, name=name, grid=(T // ROW_TILE,),
        in_specs=[row, row, pl.BlockSpec((1, D), lambda i: (0, 0)), modspec, row],
        out_specs=[row, modspec, pl.BlockSpec((8, D), lambda i: (0, 0))],
        out_shape=[jax.ShapeDtypeStruct((T, D), F32), jax.ShapeDtypeStruct((2, 8, D), F32), jax.ShapeDtypeStruct((8, D), F32)],
        compiler_params=_params())(dxn, h, g, mod, dres)


def _gate_res(name, h, y, mod, row_idx):
    T, D = h.shape

    def body(h_ref, y_ref, mod_ref, o_ref):
        o_ref[...] = h_ref[...] + mod_ref[row_idx:row_idx + 1, :] * y_ref[...]

    row = pl.BlockSpec((ROW_TILE, D), lambda i: (i, 0))
    return pl.pallas_call(
        body, name=name, grid=(T // ROW_TILE,),
        in_specs=[row, row, pl.BlockSpec((None, 8, D), lambda i: (_seg(i), 0, 0))],
        out_specs=row, out_shape=jax.ShapeDtypeStruct((T, D), F32), compiler_params=_params())(h, y, mod)


def _gate_bwd(name, dh, y, mod, row_idx):
    T, D = dh.shape

    def body(dh_ref, y_ref, mod_ref, dy_ref, dmod_ref):
        i = pl.program_id(0)
        dh = dh_ref[...]
        dy_ref[...] = (dh * mod_ref[row_idx:row_idx + 1, :]).astype(dy_ref.dtype)

        @pl.when(i <= 1)
        def _():
            dmod_ref[...] = jnp.zeros_like(dmod_ref)

        dmod_ref[row_idx:row_idx + 1, :] += jnp.sum(dh * y_ref[...], axis=0, keepdims=True)

    row = pl.BlockSpec((ROW_TILE, D), lambda i: (i, 0))
    modspec = pl.BlockSpec((None, 8, D), lambda i: (_seg(i), 0, 0))
    return pl.pallas_call(
        body, name=name, grid=(T // ROW_TILE,), in_specs=[row, row, modspec], out_specs=[row, modspec],
        out_shape=[jax.ShapeDtypeStruct((T, D), BF16), jax.ShapeDtypeStruct((2, 8, D), F32)],
        compiler_params=_params())(dh, y, mod)


def _rot(y):
    lane = lax.broadcasted_iota(jnp.int32, y.shape, 1)
    return jnp.where((lane & 32) == 0, pltpu.roll(y, 96, 1), pltpu.roll(y, 32, 1))


def _qk_prep(name, P, q_g, k_g, rope_c, rope_s, cfg):
    T = P.shape[0]
    ATT, KVW = cfg['ATT'], cfg['KVW']

    def body(q_ref, k_ref, v_ref, qg_ref, kg_ref, c_ref, s_ref, qo_ref, ko_ref, vo_ref):
        cc, ss = c_ref[...], s_ref[...]

        def head(x, g):
            y = x * _rstd(x) * g
            return y * cc + _rot(y) * ss

        for hh in range(ATT // HEAD_DIM):
            sl = slice(hh * HEAD_DIM, (hh + 1) * HEAD_DIM)
            qo_ref[:, sl] = head(q_ref[:, sl], qg_ref[...]).astype(qo_ref.dtype)
        for hh in range(KVW // HEAD_DIM):
            sl = slice(hh * HEAD_DIM, (hh + 1) * HEAD_DIM)
            ko_ref[:, sl] = head(k_ref[:, sl], kg_ref[...]).astype(ko_ref.dtype)
        vo_ref[...] = v_ref[...].astype(vo_ref.dtype)

    kb = ATT // KVW
    gain = pl.BlockSpec((1, HEAD_DIM), lambda i: (0, 0))
    tab = pl.BlockSpec((ROW_TILE, HEAD_DIM), lambda i: (i, 0))
    qs = pl.BlockSpec((ROW_TILE, ATT), lambda i: (i, 0))
    ks = pl.BlockSpec((ROW_TILE, KVW), lambda i: (i, 0))
    return pl.pallas_call(
        body, name=name, grid=(T // ROW_TILE,),
        in_specs=[qs, pl.BlockSpec((ROW_TILE, KVW), lambda i: (i, kb)), pl.BlockSpec((ROW_TILE, KVW), lambda i: (i, kb + 1)),
                  gain, gain, tab, tab],
        out_specs=[qs, ks, ks],
        out_shape=[jax.ShapeDtypeStruct((T, ATT), BF16), jax.ShapeDtypeStruct((T, KVW), BF16), jax.ShapeDtypeStruct((T, KVW), BF16)],
        compiler_params=_params())(P, P, P, q_g, k_g, rope_c, rope_s)


def _qk_prep_bwd(name, dqr, dkr, P, q_g, k_g, rope_c, rope_s, cfg):
    T = P.shape[0]
    ATT, KVW = cfg['ATT'], cfg['KVW']

    def body(dq_ref, dk_ref, q_ref, k_ref, qg_ref, kg_ref, c_ref, s_ref, dqo_ref, dko_ref, dqg_ref, dkg_ref):
        i = pl.program_id(0)
        cc, ss = c_ref[...], s_ref[...]

        @pl.when(i == 0)
        def _():
            dqg_ref[...] = jnp.zeros_like(dqg_ref)
            dkg_ref[...] = jnp.zeros_like(dkg_ref)

        def head(x, g, dout):
            dy = dout * cc + _rot(dout * ss)
            r = _rstd(x)
            xhat = x * r
            dxh = dy * g
            dx = r * (dxh - xhat * jnp.mean(dxh * xhat, axis=-1, keepdims=True))
            return dx, jnp.sum(dy * xhat, axis=0, keepdims=True)

        dg = jnp.zeros((1, HEAD_DIM), F32)
        for hh in range(ATT // HEAD_DIM):
            sl = slice(hh * HEAD_DIM, (hh + 1) * HEAD_DIM)
            dx, d = head(q_ref[:, sl], qg_ref[...], dq_ref[:, sl])
            dqo_ref[:, sl] = dx.astype(dqo_ref.dtype)
            dg += d
        dqg_ref[0:1, :] += dg
        dg = jnp.zeros((1, HEAD_DIM), F32)
        for hh in range(KVW // HEAD_DIM):
            sl = slice(hh * HEAD_DIM, (hh + 1) * HEAD_DIM)
            dx, d = head(k_ref[:, sl], kg_ref[...], dk_ref[:, sl])
            dko_ref[:, sl] = dx.astype(dko_ref.dtype)
            dg += d
        dkg_ref[0:1, :] += dg

    kb = ATT // KVW
    gain = pl.BlockSpec((1, HEAD_DIM), lambda i: (0, 0))
    dgain = pl.BlockSpec((8, HEAD_DIM), lambda i: (0, 0))
    tab = pl.BlockSpec((ROW_TILE, HEAD_DIM), lambda i: (i, 0))
    qs = pl.BlockSpec((ROW_TILE, ATT), lambda i: (i, 0))
    ks = pl.BlockSpec((ROW_TILE, KVW), lambda i: (i, 0))
    return pl.pallas_call(
        body, name=name, grid=(T // ROW_TILE,),
        in_specs=[qs, ks, qs, pl.BlockSpec((ROW_TILE, KVW), lambda i: (i, kb)), gain, gain, tab, tab],
        out_specs=[qs, ks, dgain, dgain],
        out_shape=[jax.ShapeDtypeStruct((T, ATT), BF16), jax.ShapeDtypeStruct((T, KVW), BF16),
                   jax.ShapeDtypeStruct((8, HEAD_DIM), F32), jax.ShapeDtypeStruct((8, HEAD_DIM), F32)],
        compiler_params=_params())(dqr, dkr, P, P, q_g, k_g, rope_c, rope_s)


def _att_specs(T, G):
    qs = pl.BlockSpec((ROW_TILE, G * HEAD_DIM), lambda h, i: (i, h))
    kvs = pl.BlockSpec((T, HEAD_DIM), lambda h, i: (0, h))
    return qs, kvs


def _dense_bias(i, T, Lc):
    col = lax.broadcasted_iota(jnp.int32, (1, T), 1)
    return jnp.where(col < jnp.where(i == 0, Lc, T), 0.0, NEG_INF).astype(F32)


def _attn_dense_fwd(name, q, k, v, cfg):
    T, G, Lc = q.shape[0], cfg['G'], cfg['Lc']

    def body(q_ref, k_ref, v_ref, o_ref, lse_ref):
        kk, vv = k_ref[...], v_ref[...]
        bias = _dense_bias(pl.program_id(1), T, Lc)
        for g in range(G):
            sl = slice(g * HEAD_DIM, (g + 1) * HEAD_DIM)
            s = _nt(q_ref[:, sl], kk) * ATT_SCALE + bias
            m = jnp.max(s, axis=1, keepdims=True)
            p = jnp.exp(s - m)
            l = jnp.sum(p, axis=1, keepdims=True)
            o_ref[:, sl] = _nn(_bf(p), vv) / l
            lse_ref[:, sl] = jnp.broadcast_to(m + jnp.log(l), (ROW_TILE, HEAD_DIM))

    qs, kvs = _att_specs(T, G)
    return pl.pallas_call(
        body, name=name, grid=(cfg['NKV'], T // ROW_TILE), in_specs=[qs, kvs, kvs], out_specs=[qs, qs],
        out_shape=[jax.ShapeDtypeStruct(q.shape, F32), jax.ShapeDtypeStruct(q.shape, F32)],
        compiler_params=_params())(q, k, v)


def _attn_dense_bwd(name, q, k, v, o, lse, dmix, cfg):
    T, G, Lc = q.shape[0], cfg['G'], cfg['Lc']

    def body(q_ref, k_ref, v_ref, o_ref, lse_ref, do_ref, dq_ref, dk_ref, dv_ref):
        i = pl.program_id(1)
        kk, vv = k_ref[...], v_ref[...]
        bias = _dense_bias(i, T, Lc)

        @pl.when(i == 0)
        def _():
            dk_ref[...] = jnp.zeros_like(dk_ref)
            dv_ref[...] = jnp.zeros_like(dv_ref)

        for g in range(G):
            sl = slice(g * HEAD_DIM, (g + 1) * HEAD_DIM)
            qg, do = q_ref[:, sl], do_ref[:, sl]
            delta = jnp.sum(do * o_ref[:, sl], axis=1, keepdims=True)
            p = jnp.exp(_nt(qg, kk) * ATT_SCALE + bias - lse_ref[:, g * HEAD_DIM:g * HEAD_DIM + 1])
            dob = _bf(do)
            dv_ref[...] += _tn(_bf(p), dob)
            ds = _bf(p * (_nt(dob, vv) - delta) * ATT_SCALE)
            dq_ref[:, sl] = _nn(ds, kk)
            dk_ref[...] += _tn(ds, qg)

    qs, kvs = _att_specs(T, G)
    return pl.pallas_call(
        body, name=name, grid=(cfg['NKV'], T // ROW_TILE), in_specs=[qs, kvs, kvs, qs, qs, qs], out_specs=[qs, kvs, kvs],
        out_shape=[jax.ShapeDtypeStruct(q.shape, F32), jax.ShapeDtypeStruct(k.shape, F32), jax.ShapeDtypeStruct(k.shape, F32)],
        compiler_params=_params())(q, k, v, o, lse, dmix)


def _band(i, T, Lc):
    start = pl.multiple_of(jnp.clip(WINDOW + (i - 1) * ROW_TILE, 0, T - BAND), WINDOW)
    qpos = (i - 1) * ROW_TILE + lax.broadcasted_iota(jnp.int32, (ROW_TILE, 1), 0)
    kpos = start - Lc + lax.broadcasted_iota(jnp.int32, (1, BAND), 1)
    ok = (jnp.abs(kpos - qpos) <= WINDOW) & (kpos >= 0) & (i > 0)
    return start, jnp.where(ok, 0.0, NEG_INF).astype(F32)


def _attn_win_fwd(name, q, k, v, sink, cfg):
    T, G, Lc = q.shape[0], cfg['G'], cfg['Lc']

    def body(sink_ref, q_ref, k_ref, v_ref, o_ref, lse_ref):
        h, i = pl.program_id(0), pl.program_id(1)
        start, bias = _band(i, T, Lc)
        kc, vc = k_ref[0:Lc, :], v_ref[0:Lc, :]
        kb, vb = k_ref[pl.ds(start, BAND), :], v_ref[pl.ds(start, BAND), :]
        for g in range(G):
            sl = slice(g * HEAD_DIM, (g + 1) * HEAD_DIM)
            qg = q_ref[:, sl]
            sk = sink_ref[h * G + g]
            sc = _nt(qg, kc) * ATT_SCALE
            sb = _nt(qg, kb) * ATT_SCALE + bias
            m = jnp.maximum(jnp.maximum(jnp.max(sc, axis=1, keepdims=True), jnp.max(sb, axis=1, keepdims=True)), sk)
            pc, pb = jnp.exp(sc - m), jnp.exp(sb - m)
            l = jnp.sum(pc, axis=1, keepdims=True) + jnp.sum(pb, axis=1, keepdims=True) + jnp.exp(sk - m)
            o_ref[:, sl] = (_nn(_bf(pc), vc) + _nn(_bf(pb), vb)) / l
            lse_ref[:, sl] = jnp.broadcast_to(m + jnp.log(l), (ROW_TILE, HEAD_DIM))

    qs, kvs = _att_specs(T, G)
    return pl.pallas_call(
        body, name=name, grid=(cfg['NKV'], T // ROW_TILE),
        in_specs=[pl.BlockSpec(memory_space=pltpu.SMEM), qs, kvs, kvs], out_specs=[qs, qs],
        out_shape=[jax.ShapeDtypeStruct(q.shape, F32), jax.ShapeDtypeStruct(q.shape, F32)],
        compiler_params=_params())(sink, q, k, v)


def _attn_win_bwd(name, q, k, v, o, lse, dmix, sink, cfg):
    T, G, Lc = q.shape[0], cfg['G'], cfg['Lc']

    def body(sink_ref, q_ref, k_ref, v_ref, o_ref, lse_ref, do_ref, dq_ref, dk_ref, dv_ref, dsink_ref):
        h, i = pl.program_id(0), pl.program_id(1)
        start, bias = _band(i, T, Lc)
        kc, vc = k_ref[0:Lc, :], v_ref[0:Lc, :]
        kb, vb = k_ref[pl.ds(start, BAND), :], v_ref[pl.ds(start, BAND), :]

        @pl.when(i == 0)
        def _():
            dk_ref[...] = jnp.zeros_like(dk_ref)
            dv_ref[...] = jnp.zeros_like(dv_ref)
            dsink_ref[...] = jnp.zeros_like(dsink_ref)

        for g in range(G):
            sl = slice(g * HEAD_DIM, (g + 1) * HEAD_DIM)
            qg, do = q_ref[:, sl], do_ref[:, sl]
            lse = lse_ref[:, g * HEAD_DIM:g * HEAD_DIM + 1]
            delta = jnp.sum(do * o_ref[:, sl], axis=1, keepdims=True)
            pc = jnp.exp(_nt(qg, kc) * ATT_SCALE - lse)
            pb = jnp.exp(_nt(qg, kb) * ATT_SCALE + bias - lse)
            ps = jnp.exp(sink_ref[h * G + g] - lse)
            dob = _bf(do)
            dv_ref[0:Lc, :] += _tn(_bf(pc), dob)
            dv_ref[pl.ds(start, BAND), :] += _tn(_bf(pb), dob)
            dsc = _bf(pc * (_nt(dob, vc) - delta) * ATT_SCALE)
            dsb = _bf(pb * (_nt(dob, vb) - delta) * ATT_SCALE)
            dq_ref[:, sl] = _nn(dsc, kc) + _nn(dsb, kb)
            dk_ref[0:Lc, :] += _tn(dsc, qg)
            dk_ref[pl.ds(start, BAND), :] += _tn(dsb, qg)
            dsk = jnp.where(i > 0, -jnp.sum(ps * delta, axis=0, keepdims=True), 0.0)
            dsink_ref[:, sl] += jnp.broadcast_to(dsk, (8, HEAD_DIM))

    qs, kvs = _att_specs(T, G)
    return pl.pallas_call(
        body, name=name, grid=(cfg['NKV'], T // ROW_TILE),
        in_specs=[pl.BlockSpec(memory_space=pltpu.SMEM), qs, kvs, kvs, qs, qs, qs],
        out_specs=[qs, kvs, kvs, pl.BlockSpec((None, 8, G * HEAD_DIM), lambda h, i: (h, 0, 0))],
        out_shape=[jax.ShapeDtypeStruct(q.shape, F32), jax.ShapeDtypeStruct(k.shape, F32), jax.ShapeDtypeStruct(k.shape, F32),
                   jax.ShapeDtypeStruct((cfg['NKV'], 8, G * HEAD_DIM), F32)],
        compiler_params=_params())(sink, q, k, v, o, lse, dmix)


def _seq_pos(T, Lc):
    row = lax.broadcasted_iota(jnp.int32, (T, 1), 0)
    return jnp.where(row < Lc, row, row - Lc), jnp.where(row < Lc, Lc, T - Lc)


def _fw(x, k, pos, seglen):
    return jnp.where(pos + k < seglen, pltpu.roll(x, x.shape[0] - k, 0), 0.0)


def _bw(x, k, pos):
    return jnp.where(pos - k >= 0, pltpu.roll(x, k, 0), 0.0)


def _conv_fwd(name, P, conv_w8, cfg):
    T, Lc = P.shape[0], cfg['Lc']
    cb = (cfg['ATT'] + 2 * cfg['KVW']) // HEAD_DIM
    na = AUX_WIDTH // HEAD_DIM

    def body(gb_ref, gc_ref, u_ref, w_ref, o_ref):
        pos, seglen = _seq_pos(T, Lc)
        z = gc_ref[...] * u_ref[...]
        conv = w_ref[0:1, :] * _bw(z, 1, pos) + w_ref[1:2, :] * z + w_ref[2:3, :] * _fw(z, 1, pos, seglen)
        o_ref[...] = gb_ref[...] * conv

    col = lambda off: pl.BlockSpec((T, HEAD_DIM), lambda c: (0, cb + off + c))
    return pl.pallas_call(
        body, name=name, grid=(na,),
        in_specs=[col(0), col(na), col(2 * na), pl.BlockSpec((8, HEAD_DIM), lambda c: (0, c))],
        out_specs=pl.BlockSpec((T, HEAD_DIM), lambda c: (0, c)),
        out_shape=jax.ShapeDtypeStruct((T, AUX_WIDTH), F32), compiler_params=_params())(P, P, P, conv_w8)


def _conv_bwd(name, P, conv_w8, dmix, cfg):
    T, Lc = P.shape[0], cfg['Lc']
    cb = (cfg['ATT'] + 2 * cfg['KVW']) // HEAD_DIM
    ob = cfg['ATT'] // HEAD_DIM
    na = AUX_WIDTH // HEAD_DIM

    def body(gb_ref, gc_ref, u_ref, w_ref, do_ref, dgb_ref, dgc_ref, du_ref, dw_ref):
        pos, seglen = _seq_pos(T, Lc)
        gc, u, do = gc_ref[...], u_ref[...], do_ref[...]
        z = gc * u
        zm, zp = _bw(z, 1, pos), _fw(z, 1, pos, seglen)
        w0, w1, w2 = w_ref[0:1, :], w_ref[1:2, :], w_ref[2:3, :]
        dgb_ref[...] = (do * (w0 * zm + w1 * z + w2 * zp)).astype(dgb_ref.dtype)
        dc = do * gb_ref[...]
        dz = w0 * _fw(dc, 1, pos, seglen) + w1 * dc + w2 * _bw(dc, 1, pos)
        dgc_ref[...] = (dz * u).astype(dgc_ref.dtype)
        du_ref[...] = (dz * gc).astype(du_ref.dtype)
        dw_ref[...] = jnp.zeros_like(dw_ref)
        dw_ref[0:1, :] = jnp.sum(dc * zm, axis=0, keepdims=True)
        dw_ref[1:2, :] = jnp.sum(dc * z, axis=0, keepdims=True)
        dw_ref[2:3, :] = jnp.sum(dc * zp, axis=0, keepdims=True)

    col = lambda off: pl.BlockSpec((T, HEAD_DIM), lambda c: (0, cb + off + c))
    wspec = pl.BlockSpec((8, HEAD_DIM), lambda c: (0, c))
    ocol = lambda off: pl.BlockSpec((T, HEAD_DIM), lambda c: (0, off + c))
    return pl.pallas_call(
        body, name=name, grid=(na,),
        in_specs=[col(0), col(na), col(2 * na), wspec, ocol(ob)],
        out_specs=[ocol(0), ocol(0), ocol(0), wspec],
        out_shape=[jax.ShapeDtypeStruct((T, AUX_WIDTH), BF16)] * 3 + [jax.ShapeDtypeStruct((8, AUX_WIDTH), F32)],
        compiler_params=_params())(P, P, P, conv_w8, dmix)


def _window_sums(x, half, pos, seglen):
    fwd, bwd = x, x
    s = 1
    while s < half:
        fwd = fwd + _fw(fwd, s, pos, seglen)
        bwd = bwd + _bw(bwd, s, pos)
        s *= 2
    return fwd, bwd


def _pooled(u, half, pos, seglen):
    fwd, bwd = _window_sums(u, half, pos, seglen)
    cnt = (jnp.minimum(pos + half, seglen) - jnp.maximum(pos - half, 0)).astype(F32)
    return (fwd + _bw(bwd, 1, pos)) / cnt - u, cnt


def _pool_fwd(name, P, pool_w, pool_scale, cfg):
    T, Lc = P.shape[0], cfg['Lc']
    cb = (cfg['ATT'] + 2 * cfg['KVW']) // HEAD_DIM

    def body(u_ref, w_ref, s_ref, o_ref):
        g = pl.program_id(0)
        pos, seglen = _seq_pos(T, Lc)
        for k, half in enumerate(POOL_HALF):
            @pl.when(g == k)
            def _(half=half):
                pooled, _ = _pooled(u_ref[...], half, pos, seglen)
                o_ref[...] = _nn(_bf(pooled), _bf(w_ref[...])) * s_ref[...]

    return pl.pallas_call(
        body, name=name, grid=(AUX_GROUPS,),
        in_specs=[pl.BlockSpec((T, HEAD_DIM), lambda g: (0, cb + g)), pl.BlockSpec((None, HEAD_DIM, HEAD_DIM), lambda g: (g, 0, 0)),
                  pl.BlockSpec((1, HEAD_DIM), lambda g: (0, g))],
        out_specs=pl.BlockSpec((T, HEAD_DIM), lambda g: (0, g)),
        out_shape=jax.ShapeDtypeStruct((T, AUX_WIDTH), F32), compiler_params=_params())(P, pool_w, pool_scale)


def _pool_bwd(name, P, pool_w, pool_scale, dmix, cfg):
    T, Lc = P.shape[0], cfg['Lc']
    cb = (cfg['ATT'] + 2 * cfg['KVW']) // HEAD_DIM
    ob = cfg['ATT'] // HEAD_DIM

    def body(u_ref, w_ref, s_ref, do_ref, du_ref, dw_ref, ds_ref):
        g = pl.program_id(0)
        pos, seglen = _seq_pos(T, Lc)
        for k, half in enumerate(POOL_HALF):
            @pl.when(g == k)
            def _(half=half):
                do = do_ref[...]
                pooled, cnt = _pooled(u_ref[...], half, pos, seglen)
                wb = _bf(w_ref[...])
                mixed = _nn(_bf(pooled), wb)
                ds_ref[...] = jnp.broadcast_to(jnp.sum(do * mixed, axis=0, keepdims=True), ds_ref.shape)
                dmixed = _bf(do * s_ref[...])
                dw_ref[...] = _tn(_bf(pooled), dmixed)
                dpooled = _nt(dmixed, wb)
                e = dpooled / cnt
                fwd, bwd = _window_sums(e, half, pos, seglen)
                adj = fwd + _fw(e, half, pos, seglen) + _bw(bwd, 1, pos) - _bw(e, half, pos)
                du_ref[...] = (adj - dpooled).astype(du_ref.dtype)

    wspec = pl.BlockSpec((None, HEAD_DIM, HEAD_DIM), lambda g: (g, 0, 0))
    return pl.pallas_call(
        body, name=name, grid=(AUX_GROUPS,),
        in_specs=[pl.BlockSpec((T, HEAD_DIM), lambda g: (0, cb + g)), wspec, pl.BlockSpec((1, HEAD_DIM), lambda g: (0, g)),
                  pl.BlockSpec((T, HEAD_DIM), lambda g: (0, ob + g))],
        out_specs=[pl.BlockSpec((T, HEAD_DIM), lambda g: (0, g)), wspec, pl.BlockSpec((8, HEAD_DIM), lambda g: (0, g))],
        out_shape=[jax.ShapeDtypeStruct((T, AUX_WIDTH), BF16), jax.ShapeDtypeStruct(pool_w.shape, F32),
                   jax.ShapeDtypeStruct((8, AUX_WIDTH), F32)],
        compiler_params=_params())(P, pool_w, pool_scale, dmix)


def _ffn_up(name, hn, wg3, wu3, dep=None):
    T, D = hn.shape
    J, k, _ = wg3.shape
    tm = _row_tile(T, 1088)
    dep_ins, dep_specs = _dep(dep, 2)

    def body(x_ref, wg_ref, wu_ref, *rest):
        g_ref, u_ref, a_ref = rest[len(dep_ins):]
        x = x_ref[...]
        g, u = _nt(x, wg_ref[...]), _nt(x, wu_ref[...])
        g_ref[...] = g
        u_ref[...] = u
        a_ref[...] = (g * jax.nn.sigmoid(g) * u).astype(a_ref.dtype)

    wspec = pl.BlockSpec((None, k, D), lambda j, i: (j, 0, 0))
    ospec = pl.BlockSpec((None, tm, k), lambda j, i: (j, i, 0))
    return pl.pallas_call(
        body, name=name, grid=(J, T // tm), in_specs=[pl.BlockSpec((tm, D), lambda j, i: (i, 0)), wspec, wspec] + dep_specs,
        out_specs=[ospec, ospec, ospec],
        out_shape=[jax.ShapeDtypeStruct((J, T, k), F32), jax.ShapeDtypeStruct((J, T, k), F32), jax.ShapeDtypeStruct((J, T, k), BF16)],
        compiler_params=_params())(hn, wg3, wu3, *dep_ins)


def _ffn_dact(name, dF, wd3, G, U):
    T, D = dF.shape
    J, k, _ = wd3.shape
    tm = _row_tile(T, 1088)

    def body(df_ref, wd_ref, g_ref, u_ref, dg_ref, du_ref):
        da = _nt(df_ref[...], wd_ref[...])
        g = g_ref[...]
        sig = jax.nn.sigmoid(g)
        du_ref[...] = (da * g * sig).astype(du_ref.dtype)
        dg_ref[...] = (da * u_ref[...] * (sig * (1 + g * (1 - sig)))).astype(dg_ref.dtype)

    aspec = pl.BlockSpec((None, tm, k), lambda j, i: (j, i, 0))
    return pl.pallas_call(
        body, name=name, grid=(J, T // tm),
        in_specs=[pl.BlockSpec((tm, D), lambda j, i: (i, 0)), pl.BlockSpec((None, k, D), lambda j, i: (j, 0, 0)), aspec, aspec],
        out_specs=[aspec, aspec],
        out_shape=[jax.ShapeDtypeStruct((J, T, k), BF16), jax.ShapeDtypeStruct((J, T, k), BF16)],
        compiler_params=_params())(dF, wd3, G, U)


def _loss_head(name, h, g, target, cfg):
    T, D = h.shape

    def body(h_ref, g_ref, t_ref, dh_ref, loss_ref, dg_ref):
        i = pl.program_id(0)

        @pl.when(i == 0)
        def _():
            dh_ref[...] = jnp.zeros_like(dh_ref)
            loss_ref[...] = jnp.zeros_like(loss_ref)
            dg_ref[...] = jnp.zeros_like(dg_ref)

        @pl.when(i > 0)
        def _():
            x = h_ref[...]
            r = _rstd(x)
            xhat = x * r
            gg = g_ref[...]
            err = xhat * gg - t_ref[...]
            loss_ref[...] += 0.5 * jnp.sum(jnp.sum(err * err, axis=1, keepdims=True) / D, axis=0, keepdims=True)
            dy = err / D
            dg_ref[0:1, :] += jnp.sum(dy * xhat, axis=0, keepdims=True)
            dxh = dy * gg
            dh_ref[...] = r * (dxh - xhat * jnp.mean(dxh * xhat, axis=-1, keepdims=True))

    row = pl.BlockSpec((ROW_TILE, D), lambda i: (i, 0))
    return pl.pallas_call(
        body, name=name, grid=(T // ROW_TILE,),
        in_specs=[row, pl.BlockSpec((1, D), lambda i: (0, 0)), pl.BlockSpec((ROW_TILE, D), lambda i: (jnp.maximum(i - 1, 0), 0))],
        out_specs=[row, pl.BlockSpec((8, 128), lambda i: (0, 0)), pl.BlockSpec((8, D), lambda i: (0, 0))],
        out_shape=[jax.ShapeDtypeStruct((T, D), F32), jax.ShapeDtypeStruct((8, 128), F32), jax.ShapeDtypeStruct((8, D), F32)],
        compiler_params=_params())(h, g, target)


def _adamw(name, parts, w, m, v, dep=None):
    R, C = w.shape
    n_parts = parts.shape[0]
    tr = _row_tile(R, max(16, (1 << 18) // C)) if R % 16 == 0 else R
    bc1 = 1.0 - ADAM_B1 ** ADAM_STEP
    bc2 = 1.0 - ADAM_B2 ** ADAM_STEP
    dep_ins, dep_specs = _dep(dep, 1)

    def body(p_ref, w_ref, m_ref, v_ref, *rest):
        g_ref, d_ref, nm_ref, nv_ref = rest[len(dep_ins):]
        g = p_ref[0].astype(F32)
        for k in range(1, n_parts):
            g = g + p_ref[k].astype(F32)
        nm = ADAM_B1 * m_ref[...] + (1.0 - ADAM_B1) * g
        nv = ADAM_B2 * v_ref[...] + (1.0 - ADAM_B2) * (g * g)
        g_ref[...] = g
        nm_ref[...] = nm
        nv_ref[...] = nv
        d_ref[...] = -ADAM_LR * ((nm / bc1) / (jnp.sqrt(nv / bc2) + ADAM_EPS) + ADAM_WD * w_ref[...])

    blk = pl.BlockSpec((tr, C), lambda i: (i, 0))
    return pl.pallas_call(
        body, name=name, grid=(R // tr,), in_specs=[pl.BlockSpec((n_parts, tr, C), lambda i: (0, i, 0)), blk, blk, blk] + dep_specs,
        out_specs=[blk] * 4, out_shape=[jax.ShapeDtypeStruct((R, C), F32)] * 4, compiler_params=_params())(parts, w, m, v, *dep_ins)


class _WeightStream:
    def __init__(self, cast):
        self.cast, self.handles = cast, {}

    @staticmethod
    def _tag(l, group):
        return ("mixer" if group is MIXER_WEIGHTS else "ffn") + str(l)

    def start(self, l, group, after=None):
        self.handles[l, group], token = _exchange_start(
            f"gather_{self._tag(l, group)}_start", [self.cast(l, n) for n in group], False, after)
        return token

    def get(self, l, group, after):
        got = dict(zip(group, _exchange_wait(f"gather_{self._tag(l, group)}_wait", self.handles[l, group], after)))
        if 'w_out' in got:
            rows, cols = got['w_out'].shape[1:]
            got['w_out'] = got['w_out'].reshape(N_DEV * rows, cols)
        return got


def _layer_fwd(l, h, p, stream, mod, rope, conv_w8, cfg):
    nm = f"l{l}_"
    xn = _norm_mod(nm + "norm1", h, p['norm1_g'], mod, 0)
    W = stream.get(l, MIXER_WEIGHTS, xn)
    token = stream.start(0, FFN_WEIGHTS, after=W['w_in']) if l == 0 else None
    P = _mm_cols(nm + "w_in", xn, W['w_in'], dep=token)
    qr, kr, vb = _qk_prep(nm + "qk_prep", P, p['q_norm_g'], p['k_norm_g'], rope[0], rope[1], cfg)
    if l == 0:
        o, lse = _attn_dense_fwd(nm + "attn", qr, kr, vb, cfg)
        aux = _conv_fwd(nm + "conv", P, conv_w8, cfg)
    else:
        o, lse = _attn_win_fwd(nm + "attn", qr, kr, vb, p['sink'], cfg)
        aux = _pool_fwd(nm + "pool", P, p['pool_w'], p['pool_scale'], cfg)
    mix = jnp.concatenate([o, aux], axis=1).astype(BF16)
    y = _mm_plain(nm + "w_out", mix, W['w_out'], False)
    h2 = _gate_res(nm + "res1", h, y, mod, 2)
    hn = _norm_mod(nm + "norm2", h2, p['norm2_g'], mod, 1)
    W.update(stream.get(l, FFN_WEIGHTS, hn))
    token = stream.start(1, MIXER_WEIGHTS, after=W['w_down']) if l == 0 else None
    G, U, A = _ffn_up(nm + "ffn_up", hn, W['w_gate'], W['w_up'], dep=token)
    token = stream.start(1, FFN_WEIGHTS, after=A) if l == 0 else None
    F = _mm_shards_nn(nm + "w_down", A, W['w_down'], dep=token)
    h3 = _gate_res(nm + "res2", h2, F, mod, 5)
    saved = dict(h=h, xn=xn, P=P, qr=qr, kr=kr, vb=vb, o=o, lse=lse, mix=mix, y=y, h2=h2, hn=hn, G=G, U=U, A=A, F=F)
    return h3, saved, W


def _layer_bwd(l, dh3, s, p, W, mod, rope, conv_w8, cfg):
    nm = f"l{l}_bwd_"
    J = N_DEV
    dF, dmod = _gate_bwd(nm + "res2", dh3, s['F'], mod, 5)
    dG, dU = _ffn_dact(nm + "ffn_act", dF, W['w_down'], s['G'], s['U'])
    big = {'w_down': _wgrad_down(nm + "dw_down", s['A'], dF),
           'w_gate': _wgrad_down(nm + "dw_gate", dG, s['hn']),
           'w_up': _wgrad_down(nm + "dw_up", dU, s['hn'])}
    handles = {}
    handles['ffn'], token = _exchange_start(f"scatter_ffn{l}_start", [big[n] for n in FFN_WEIGHTS], True)
    mod = mod + token[0, 0]
    dhn = _mm_shards_nn2(nm + "dhn", dG, W['w_gate'], dU, W['w_up'])
    dh2, dm, dg2 = _norm_mod_bwd(nm + "norm2", dhn, s['h2'], p['norm2_g'], mod, 1, dh3)
    dmod += dm
    dY, dm = _gate_bwd(nm + "res1", dh2, s['y'], mod, 2)
    dmod += dm
    dwo = _wgrad_rows(nm + "dw_out", s['mix'], dY)
    handles['w_out'], token = _exchange_start(f"scatter_w_out{l}_start", [dwo.reshape((J, dwo.shape[0] // J, dwo.shape[1]))], True)
    dmix = _mm_plain(nm + "dmix", dY, W['w_out'], True, dep=token)
    small = {'norm2_g': dg2[0]}
    if l == 0:
        dqr, dkr, dv = _attn_dense_bwd(nm + "attn", s['qr'], s['kr'], s['vb'], s['o'], s['lse'], dmix, cfg)
        *daux, dcw = _conv_bwd(nm + "conv", s['P'], conv_w8, dmix, cfg)
        small['conv_w'] = dcw[0:3]
    else:
        dqr, dkr, dv, dsk = _attn_win_bwd(nm + "attn", s['qr'], s['kr'], s['vb'], s['o'], s['lse'], dmix, p['sink'], cfg)
        du, dpw, dps = _pool_bwd(nm + "pool", s['P'], p['pool_w'], p['pool_scale'], dmix, cfg)
        daux = [du]
        small.update(sink=dsk[:, 0, ::HEAD_DIM].reshape(-1), pool_w=dpw, pool_scale=dps[0])
    dq, dk, dqg, dkg = _qk_prep_bwd(nm + "qk_prep", dqr, dkr, s['P'], p['q_norm_g'], p['k_norm_g'], rope[0], rope[1], cfg)
    small.update(q_norm_g=dqg[0], k_norm_g=dkg[0])
    dP = jnp.concatenate([dq, dk, dv.astype(BF16), *daux], axis=1)
    dw_in = _wgrad_cols(nm + "dw_in", s['xn'], dP, J)
    if l == 0:
        handles['w_in'] = dw_in
    else:
        handles['w_in'], token = _exchange_start(f"scatter_w_in{l}_start", [dw_in], True)
        mod = mod + token[0, 0]
    dxn = _mm_cols_nt(nm + "dxn", dP, W['w_in'])
    dh, dm, dg1 = _norm_mod_bwd(nm + "norm1", dxn, s['h'], p['norm1_g'], mod, 0, dh2)
    dmod += dm
    small['norm1_g'] = dg1[0]
    return dh, dmod, small, handles, token


def _rope_tables(S, Lc):
    half = HEAD_DIM // 4
    pos = np.arange(S)
    inv = ROPE_THETA ** (-np.arange(0, 2 * half, 2, dtype=np.float32) / (2 * half))
    inv = jnp.asarray(inv, F32)
    ang_r = jnp.asarray(pos // GRID_W, F32)[:, None] * inv
    ang_c = jnp.asarray(pos % GRID_W, F32)[:, None] * inv
    cos = jnp.concatenate([jnp.cos(ang_r)] * 2 + [jnp.cos(ang_c)] * 2, axis=1)
    sin = jnp.concatenate([-jnp.sin(ang_r), jnp.sin(ang_r), -jnp.sin(ang_c), jnp.sin(ang_c)], axis=1)
    return (jnp.concatenate([jnp.ones((Lc, HEAD_DIM), F32), cos], axis=0),
            jnp.concatenate([jnp.zeros((Lc, HEAD_DIM), F32), sin], axis=0))


def _pad_rows(a, rows):
    return jnp.concatenate([a, jnp.zeros((rows - a.shape[0],) + a.shape[1:], a.dtype)], axis=0)


def _flat128(a, nlead):
    lead = a.shape[:nlead]
    f = a.reshape(lead + (-1,))
    pad = (-f.shape[-1]) % 128
    if pad:
        f = jnp.concatenate([f, jnp.zeros(lead + (pad,), f.dtype)], axis=-1)
    return f.reshape(lead + (-1, 128))


def _pack(named, nlead=0):
    rows, layout, at = [], {}, 0
    for name, a in named:
        f = _flat128(a, nlead)
        n = f.shape[-2]
        pad = (-n) % 8
        if pad:
            f = jnp.concatenate([f, jnp.zeros(f.shape[:-2] + (pad, 128), f.dtype)], axis=-2)
        layout[name] = (at, n, a.shape[nlead:])
        rows.append(f)
        at += n + pad
    return jnp.concatenate(rows, axis=-2), layout


def _unpack(arr, layout, name):
    at, n, shape = layout[name]
    return arr[..., at:at + n, :].reshape(arr.shape[:-2] + (-1,))[..., :math.prod(shape)].reshape(arr.shape[:-2] + tuple(shape))


def kernel(*args):
    A = dict(zip(INPUT_NAMES, args, strict=True))
    x, ctx = A['x'][0], A['ctx'][0]
    S, D = x.shape
    Lc = ctx.shape[0]
    T = Lc + S
    ATT = D - AUX_WIDTH
    KVW = (A['l1_w_in'].shape[1] * N_DEV - ATT - AUX_WIDTH) // 2
    cfg = dict(ATT=ATT, KVW=KVW, NKV=KVW // HEAD_DIM, G=ATT // KVW, Lc=Lc)
    assert Lc == ROW_TILE and S % ROW_TILE == 0 and T >= BAND and S % GRID_W == 0
    cw = A['l0_conv_w'].shape[1]
    me = 4 * lax.axis_index("x") + 2 * lax.axis_index("y") + lax.axis_index("c")

    def layer_params(l):
        pre = f"l{l}_"
        return {k[len(pre):]: (v.reshape(1, -1) if v.ndim == 1 and k != 'l1_sink' else v) for k, v in A.items() if k.startswith(pre)}

    params = [layer_params(0), layer_params(1)]

    def cast(l, n):
        w = A[f'l{l}_{n}']
        return (w.T if n in TRANSPOSED else w).astype(BF16)

    stream = _WeightStream(cast)

    sc_own = jax.nn.silu(A['c'])
    first, lay0 = _pack([('sc', sc_own), ('conv_w', A['l0_conv_w'])])
    first_all = _exchange("gather_cond", [first], False)[0]
    sc_all = _unpack(first_all, lay0, 'sc')[:, 0]
    conv_w = _unpack(first_all, lay0, 'conv_w').transpose(1, 0, 2).reshape(3, N_DEV * cw)
    conv_w8 = _pad_rows(conv_w, 8)
    sc_ctx = jax.nn.silu(A['c_ctx'])
    s16 = _pad_rows(jnp.concatenate([sc_all, sc_ctx[None]], axis=0), 16)

    nmod = A['l0_w_mod'].shape[1]
    modp = jnp.concatenate([_mm_plain(f"l{l}_mod", s16, A[f'l{l}_w_mod'], False) for l in range(2)], axis=1)
    modp_all = _exchange("gather_mod", [modp], False)[0]
    mods = []
    for l in range(2):
        full = modp_all[:, :, l * nmod:(l + 1) * nmod].transpose(1, 0, 2).reshape(16, N_MOD * D) + A[f'l{l}_b_mod'][None]
        both = jnp.stack([full[8], lax.dynamic_index_in_dim(full, me, 0, keepdims=False)]).reshape(2, N_MOD, D)
        mods.append(jnp.concatenate([both, jnp.zeros((2, 8 - N_MOD, D), F32)], axis=1))

    token = stream.start(0, MIXER_WEIGHTS, after=modp_all)
    mods[0] = mods[0] + token[0, 0]

    rope = _rope_tables(S, Lc)
    h = jnp.concatenate([ctx, x], axis=0)
    saved, W = [], []
    for l in range(2):
        h, s, Wl = _layer_fwd(l, h, params[l], stream, mods[l], rope, conv_w8, cfg)
        saved.append(s)
        W.append(Wl)

    dh, loss_blk, dgf = _loss_head("loss_head", h, A['final_norm_g'].reshape(1, -1), A['loss_target'][0], cfg)
    loss = lax.psum(loss_blk[0, 0], ("x", "y", "c"))

    grads, small, dmods, scatters = {}, {'final_norm_g': dgf[0]}, [None, None], [None, None]
    token = jnp.zeros((8, 128), F32)
    for l in (1, 0):
        dh, dmods[l], sm, scatters[l], token = _layer_bwd(l, dh, saved[l], params[l], W[l], mods[l] + token[0, 0], rope, conv_w8, cfg)
        small.update({f'l{l}_{k}': v for k, v in sm.items()})
    grad_x = dh[Lc:][None]

    def landed(l, key, after):
        group = FFN_WEIGHTS if key == 'ffn' else (key,)
        for n, parts in zip(group, _exchange_wait(f"scatter_{key}{l}_wait", scatters[l][key], after)):
            shape = A[f'l{l}_{n}'].shape
            grads[f'l{l}_{n}'] = (parts.reshape((N_DEV,) + (shape[::-1] if n in TRANSPOSED else shape)), None)

    small_names = [n for n in WEIGHT_NAMES if n in small]
    pieces = [(n, small[n]) for n in small_names]
    for l in range(2):
        pieces += [(f'dmod{l}', dmods[l][1, :N_MOD]), (f'dcmod{l}', dmods[l][0, :N_MOD])]
    second, lay1 = _pack(pieces)
    second_all = _exchange("gather_small", [second], False)[0]

    dsc_part = jnp.zeros((16, D), F32)
    for l in range(2):
        dm16 = _pad_rows(jnp.concatenate([_unpack(second_all, lay1, f'dmod{l}').reshape(N_DEV, N_MOD * D),
                                          jnp.sum(_unpack(second_all, lay1, f'dcmod{l}'), axis=0).reshape(1, N_MOD * D)], axis=0), 16)
        mine = lax.dynamic_slice_in_dim(dm16, me * nmod, nmod, axis=1)
        tk = _col_tile(D, 512)
        gw = _mm_tn(f"l{l}_dw_mod", s16, mine, pl.BlockSpec((16, tk), lambda kb, t: (0, kb)), pl.BlockSpec((16, nmod), lambda kb, t: (0, 0)),
                    jax.ShapeDtypeStruct((D, nmod), F32), pl.BlockSpec((tk, nmod), lambda kb, t: (kb, 0)), (tk, nmod), (D // tk, 1))
        grads[f'l{l}_w_mod'] = (gw[None], None)
        dsc_part += _mm_plain(f"l{l}_dsc", mine, A[f'l{l}_w_mod'], True)
        dmod_dev = _unpack(second_all, lay1, f'dmod{l}') + _unpack(second_all, lay1, f'dcmod{l}')
        grads[f'l{l}_b_mod'] = (dmod_dev.reshape(N_DEV, N_MOD * D), None)
    dsig = jax.nn.sigmoid(A['c_ctx'])
    dsilu = dsig * (1 + A['c_ctx'] * (1 - dsig))
    third_all = _exchange("gather_dsc", [dsc_part[8:9]], False)[0]
    grads['c_ctx'] = (third_all[:, 0] * dsilu[None], None)
    scatters[0]['w_in'], last_token = _exchange_start("scatter_w_in0_start", [scatters[0]['w_in']], True, after=third_all)
    for n in small_names:
        g8 = _unpack(second_all, lay1, n)
        if n == 'l0_conv_w':
            g8 = lax.dynamic_slice_in_dim(g8, me * cw, cw, axis=2)
        grads[n] = (g8, None)

    out = {}
    big_names = [n for n in WEIGHT_NAMES if n[3:] in BIG_WEIGHTS + ('w_mod',)]

    def adam(n, dep=None):
        w, m, v = A[n], A['m_' + n], A['v_' + n]
        if n[3:] in TRANSPOSED:
            out[n] = tuple(r.T for r in _adamw("adamw_" + n, grads[n][0], w.T, m.T, v.T, dep))
        else:
            out[n] = _adamw("adamw_" + n, grads[n][0], w, m, v, dep)
        return out[n][1]

    last = adam('l0_w_mod', last_token)
    adam('l1_w_mod', last_token)
    for l in (1, 0):
        for key in ('ffn', 'w_out', 'w_in'):
            landed(l, key, last)
            for n in (FFN_WEIGHTS if key == 'ffn' else (key,)):
                last = adam(f'l{l}_{n}')
    rest = [n for n in WEIGHT_NAMES if n not in big_names]
    wp, layw = _pack([(n, A[n]) for n in rest])
    mp, _ = _pack([(n, A['m_' + n]) for n in rest])
    vp, _ = _pack([(n, A['v_' + n]) for n in rest])
    gp, _ = _pack([(n, grads[n][0]) for n in rest], nlead=1)
    res = _adamw("adamw_small", gp, wp, mp, vp)
    for n in rest:
        out[n] = tuple(_unpack(r, layw, n) for r in res)

    outs = [loss, grad_x]
    for k in range(4):
        outs += [out[n][k] for n in WEIGHT_NAMES]
    return tuple(outs)
```

```python
import functools
import math

import numpy as np
import jax
import jax.numpy as jnp
from jax import lax
from jax.experimental import pallas as pl
from jax.experimental.pallas import tpu as pltpu

F32 = jnp.float32
BF16 = jnp.bfloat16
HEAD_DIM = 128
AUX_WIDTH = 512
AUX_GROUPS = 4
POOL_HALF = (1, 2, 4, 8)
WINDOW = 128
GRID_W = 64
ROPE_THETA = 10000.0
EPS = 1e-6
NEG_INF = -1e30
ATT_SCALE = HEAD_DIM ** -0.5
LOG2_E = math.log2(math.e)
Q_SCALE = ATT_SCALE * LOG2_E
N_MOD = 6
N_DEV = 8
ROW_TILE = 256
BAND = ROW_TILE + 2 * WINDOW
ADAM_LR, ADAM_B1, ADAM_B2, ADAM_EPS, ADAM_WD, ADAM_STEP = 0.001, 0.9, 0.999, 1e-08, 0.01, 10
VMEM_LIMIT_MB = 56
MESH = pl.DeviceIdType.MESH

WEIGHT_NAMES = ['c_ctx', 'l0_norm1_g', 'l0_w_mod', 'l0_b_mod', 'l0_w_in', 'l0_q_norm_g', 'l0_k_norm_g', 'l0_conv_w', 'l0_w_out', 'l0_norm2_g', 'l0_w_gate', 'l0_w_up', 'l0_w_down', 'l1_norm1_g', 'l1_w_mod', 'l1_b_mod', 'l1_w_in', 'l1_q_norm_g', 'l1_k_norm_g', 'l1_sink', 'l1_pool_w', 'l1_pool_scale', 'l1_w_out', 'l1_norm2_g', 'l1_w_gate', 'l1_w_up', 'l1_w_down', 'final_norm_g']
INPUT_NAMES = (['x', 'c', 'ctx'] + WEIGHT_NAMES + ['loss_target'] + ['m_' + n for n in WEIGHT_NAMES]
               + ['v_' + n for n in WEIGHT_NAMES])
IN_WEIGHT = ('w_in',)
OUT_WEIGHT = ('w_out',)
MIXER_WEIGHTS = OUT_WEIGHT + IN_WEIGHT
FFN_WEIGHTS = ('w_down', 'w_gate', 'w_up')
TRANSPOSED = ('w_gate', 'w_up')
BIG_WEIGHTS = MIXER_WEIGHTS + FFN_WEIGHTS


def _params(vmem_mb=VMEM_LIMIT_MB):
    return pltpu.CompilerParams(vmem_limit_bytes=vmem_mb << 20)


def _row_tile(n, cap):
    best = None
    for t in range(16, min(n, cap) + 1, 16):
        if n % t == 0:
            best = t
    assert best is not None, (n, cap)
    return best


def _col_tile(n, cap):
    best = n
    for t in range(128, min(n, cap) + 1, 128):
        if n % t == 0:
            best = t
    return best if best <= cap or n % 128 else n


def _dot(a, b, ca, cb):
    return lax.dot_general(a, b, (((ca,), (cb,)), ((), ())), preferred_element_type=F32)


def _nn(a, b):
    return _dot(a, b, 1, 0)


def _nt(a, b):
    return _dot(a, b, 1, 1)


def _tn(a, b):
    return _dot(a, b, 0, 0)


def _bf(x):
    return x.astype(BF16)


def _exchange(name, arrs, scatter):
    n = len(arrs)
    if scatter:
        out_shape = [jax.ShapeDtypeStruct(a.shape, a.dtype) for a in arrs]
    else:
        out_shape = [jax.ShapeDtypeStruct((N_DEV,) + a.shape, a.dtype) for a in arrs]

    def body(*refs):
        ins, outs = refs[:n], refs[n:2 * n]
        send_sems, recv_sems, local_sems = refs[2 * n:]
        x, y, c = lax.axis_index("x"), lax.axis_index("y"), lax.axis_index("c")
        me = 4 * x + 2 * y + c
        local, remote = [], []
        for a in range(n):
            own = ins[a].at[me] if scatter else ins[a]
            cp = pltpu.make_async_copy(own, outs[a].at[me], local_sems.at[a])
            cp.start()
            local.append(cp)
            for r in range(1, N_DEV):
                px = 1 - x if r & 4 else x
                py = 1 - y if r & 2 else y
                pc = 1 - c if r & 1 else c
                src = ins[a].at[4 * px + 2 * py + pc] if scatter else ins[a]
                cp = pltpu.make_async_remote_copy(
                    src_ref=src, dst_ref=outs[a].at[me], send_sem=send_sems.at[a, r - 1],
                    recv_sem=recv_sems.at[a, r - 1], device_id=(px, py, pc), device_id_type=MESH)
                cp.start()
                remote.append(cp)
        for cp in remote:
            cp.wait()
        for cp in local:
            cp.wait()

    any_spec = pl.BlockSpec(memory_space=pl.ANY)
    return pl.pallas_call(
        body, name=name, out_shape=out_shape,
        in_specs=[any_spec] * n, out_specs=[any_spec] * n,
        scratch_shapes=[pltpu.SemaphoreType.DMA((n, N_DEV - 1)), pltpu.SemaphoreType.DMA((n, N_DEV - 1)),
                        pltpu.SemaphoreType.DMA((n,))],
    )(*arrs)


HBM_SPEC = pl.BlockSpec(memory_space=pltpu.HBM)
SEM_SPEC = pl.BlockSpec(memory_space=pltpu.SEMAPHORE)
EFFECT = pltpu.SideEffectType.DATAFLOW_SIDE_EFFECTING


def _split_copies(srcs, lands, send_sems, recv_sems, local_sems, scatter):
    x, y, c = lax.axis_index("x"), lax.axis_index("y"), lax.axis_index("c")
    me = 4 * x + 2 * y + c
    local, remote = [], []
    for a in range(len(srcs)):
        own = srcs[a].at[me] if scatter else srcs[a]
        local.append(pltpu.make_async_copy(own, lands[a].at[me], local_sems.at[a]))
        for r in range(1, N_DEV):
            px = 1 - x if r & 4 else x
            py = 1 - y if r & 2 else y
            pc = 1 - c if r & 1 else c
            src = srcs[a].at[4 * px + 2 * py + pc] if scatter else srcs[a]
            remote.append(pltpu.make_async_remote_copy(
                src_ref=src, dst_ref=lands[a].at[me], send_sem=send_sems.at[a * (N_DEV - 1) + r - 1],
                recv_sem=recv_sems.at[a * (N_DEV - 1) + r - 1], device_id=(px, py, pc), device_id_type=MESH))
    return local, remote


def _exchange_start(name, arrs, scatter, after=None):
    n = len(arrs)
    extra = [] if after is None else [after]
    shapes = [a.shape if scatter else (N_DEV,) + a.shape for a in arrs]
    lands = [pltpu.with_memory_space_constraint(lax.empty(s, a.dtype), pltpu.HBM) for s, a in zip(shapes, arrs)]
    srcs = [pltpu.with_memory_space_constraint(a, pltpu.HBM) for a in arrs]

    def body(*refs):
        src_refs, land_refs = refs[:n], refs[n:2 * n]
        send_sems, recv_sems, local_sems = refs[2 * n + len(extra):2 * n + len(extra) + 3]
        token = refs[-1]
        local, remote = _split_copies(src_refs, land_refs, send_sems, recv_sems, local_sems, scatter)
        for cp in local + remote:
            cp.start()
        token[...] = jnp.zeros_like(token)

    res = pl.pallas_call(
        body, name=name,
        out_shape=[pltpu.SemaphoreType.DMA((n * (N_DEV - 1),)), pltpu.SemaphoreType.DMA((n * (N_DEV - 1),)), pltpu.SemaphoreType.DMA((n,))]
        + [pltpu.HBM(a.shape, a.dtype) for a in arrs] + [pltpu.HBM(s, a.dtype) for s, a in zip(shapes, arrs)]
        + [jax.ShapeDtypeStruct((8, 128), F32)],
        in_specs=[HBM_SPEC] * (2 * n) + [pl.BlockSpec(memory_space=pl.ANY)] * len(extra),
        out_specs=[SEM_SPEC] * 3 + [HBM_SPEC] * (2 * n) + [pl.BlockSpec(memory_space=pltpu.VMEM)],
        input_output_aliases={i: 3 + i for i in range(2 * n)},
        compiler_params=pltpu.CompilerParams(has_side_effects=EFFECT),
    )(*srcs, *lands, *extra)
    return (scatter, res[:3], res[3:3 + n], res[3 + n:3 + 2 * n]), res[-1]


def _exchange_wait(name, handle, after):
    scatter, sems, srcs, lands = handle
    n = len(srcs)

    def body(*refs):
        src_refs, land_refs = refs[:n], refs[n:2 * n]
        send_sems, recv_sems, local_sems = refs[2 * n:2 * n + 3]
        local, remote = _split_copies(src_refs, land_refs, send_sems, recv_sems, local_sems, scatter)
        for cp in remote:
            cp.wait_send()
            cp.wait_recv()
        for cp in local:
            cp.wait()

    res = pl.pallas_call(
        body, name=name,
        out_shape=[pltpu.HBM(a.shape, a.dtype) for a in srcs] + [pltpu.HBM(a.shape, a.dtype) for a in lands],
        in_specs=[HBM_SPEC] * (2 * n) + [SEM_SPEC] * 3 + [pl.BlockSpec(memory_space=pl.ANY)], out_specs=[HBM_SPEC] * (2 * n),
        input_output_aliases={i: i for i in range(2 * n)},
        compiler_params=pltpu.CompilerParams(has_side_effects=EFFECT),
    )(*srcs, *lands, *sems, after)
    return list(res[n:])


def _dep(dep, grid_rank):
    if dep is None:
        return [], []
    return [dep], [pl.BlockSpec((8, 128), (lambda i, j: (0, 0)) if grid_rank == 2 else (lambda i: (0, 0)))]


def _mm_step(name, fn, ins, in_specs, out_shape, out_spec, grid, dep=None):
    n = len(ins)
    dep_ins, dep_specs = _dep(dep, len(grid))

    def body(*refs):
        o_ref = refs[n + len(dep_ins)]
        o_ref[...] = fn(*refs[:n]).astype(o_ref.dtype)

    return pl.pallas_call(body, name=name, grid=grid, in_specs=list(in_specs) + dep_specs, out_specs=out_spec,
                          out_shape=out_shape, compiler_params=_params())(*ins, *dep_ins)


def _mm_tn(name, a, b, a_spec, b_spec, out_shape, out_spec, acc_shape, grid):
    nk = grid[-1]
    kax = len(grid) - 1

    def body(a_ref, b_ref, o_ref, acc_ref):
        k = pl.program_id(kax)

        @pl.when(k == 0)
        def _():
            acc_ref[...] = jnp.zeros_like(acc_ref)

        acc_ref[...] += _tn(_bf(a_ref[...]), _bf(b_ref[...]))

        @pl.when(k == nk - 1)
        def _():
            o_ref[...] = acc_ref[...].astype(o_ref.dtype)

    return pl.pallas_call(body, name=name, grid=grid, in_specs=[a_spec, b_spec], out_specs=out_spec,
                          out_shape=out_shape, scratch_shapes=[pltpu.VMEM(acc_shape, F32)],
                          compiler_params=_params())(a, b)


def _mm_cols(name, a, w3, out_dtype=F32, dep=None):
    M, K = a.shape
    J, _, n = w3.shape
    tm = _row_tile(M, 1088)
    return _mm_step(
        name, lambda a_ref, w_ref: _nn(_bf(a_ref[...]), w_ref[...]), [a, w3],
        [pl.BlockSpec((tm, K), lambda j, i: (i, 0)), pl.BlockSpec((None, K, n), lambda j, i: (j, 0, 0))],
        jax.ShapeDtypeStruct((M, J * n), out_dtype), pl.BlockSpec((tm, n), lambda j, i: (i, j)), (J, M // tm), dep)


def _mm_plain(name, a, b, transpose_b, out_dtype=F32, tn=512, dep=None):
    M, K = a.shape
    N = b.shape[0] if transpose_b else b.shape[1]
    tm = _row_tile(M, 1088)
    tn = _col_tile(N, tn)
    if transpose_b:
        b_spec = pl.BlockSpec((tn, K), lambda j, i: (j, 0))
        fn = lambda a_ref, b_ref: _nt(_bf(a_ref[...]), _bf(b_ref[...]))
    else:
        b_spec = pl.BlockSpec((K, tn), lambda j, i: (0, j))
        fn = lambda a_ref, b_ref: _nn(_bf(a_ref[...]), _bf(b_ref[...]))
    return _mm_step(name, fn, [a, b], [pl.BlockSpec((tm, K), lambda j, i: (i, 0)), b_spec],
                    jax.ShapeDtypeStruct((M, N), out_dtype), pl.BlockSpec((tm, tn), lambda j, i: (i, j)),
                    (N // tn, M // tm), dep)


def _mm_shards_nn(name, a3, w3, tn=512, dep=None):
    J, M, k = a3.shape
    N = w3.shape[2]
    tm = _row_tile(M, 544)
    tn = _col_tile(N, tn)

    def fn(a_ref, w_ref):
        acc = _nn(a_ref[0], w_ref[0])
        for j in range(1, J):
            acc += _nn(a_ref[j], w_ref[j])
        return acc

    return _mm_step(name, fn, [a3, w3],
                    [pl.BlockSpec((J, tm, k), lambda jn, i: (0, i, 0)), pl.BlockSpec((J, k, tn), lambda jn, i: (0, 0, jn))],
                    jax.ShapeDtypeStruct((M, N), F32), pl.BlockSpec((tm, tn), lambda jn, i: (i, jn)), (N // tn, M // tm), dep)


def _mm_shards_nn2(name, a3, w3a, b3, w3b, tn=512):
    J, M, k = a3.shape
    N = w3a.shape[2]
    tm = _row_tile(M, 544)
    tn = _col_tile(N, tn)

    def fn(a_ref, wa_ref, b_ref, wb_ref):
        acc = _nn(a_ref[0], wa_ref[0]) + _nn(b_ref[0], wb_ref[0])
        for j in range(1, J):
            acc += _nn(a_ref[j], wa_ref[j]) + _nn(b_ref[j], wb_ref[j])
        return acc

    act = pl.BlockSpec((J, tm, k), lambda jn, i: (0, i, 0))
    wsp = pl.BlockSpec((J, k, tn), lambda jn, i: (0, 0, jn))
    return _mm_step(name, fn, [a3, w3a, b3, w3b], [act, wsp, act, wsp],
                    jax.ShapeDtypeStruct((M, N), F32), pl.BlockSpec((tm, tn), lambda jn, i: (i, jn)), (N // tn, M // tm))


def _mm_cols_nt(name, a, w3, tn=512):
    M = a.shape[0]
    J, N, n = w3.shape
    tm = _row_tile(M, 544)
    tn = _col_tile(N, tn)

    def fn(a_ref, w_ref):
        acc = _nt(a_ref[:, 0:n], w_ref[0])
        for j in range(1, J):
            acc += _nt(a_ref[:, j * n:(j + 1) * n], w_ref[j])
        return acc

    return _mm_step(name, fn, [a, w3],
                    [pl.BlockSpec((tm, J * n), lambda jn, i: (i, 0)), pl.BlockSpec((J, tn, n), lambda jn, i: (0, jn, 0))],
                    jax.ShapeDtypeStruct((M, N), F32), pl.BlockSpec((tm, tn), lambda jn, i: (i, jn)), (N // tn, M // tm))


def _wgrad_cols(name, a, b, J):
    T, K = a.shape
    n = b.shape[1] // J
    tt = _row_tile(T, 1088)
    return _mm_tn(name, a, b, pl.BlockSpec((tt, K), lambda j, t: (t, 0)), pl.BlockSpec((tt, n), lambda j, t: (t, j)),
                  jax.ShapeDtypeStruct((J, K, n), BF16), pl.BlockSpec((None, K, n), lambda j, t: (j, 0, 0)), (K, n), (J, T // tt))


def _wgrad_rows(name, a, b, tk=512):
    T, K = a.shape
    N = b.shape[1]
    tt = _row_tile(T, 1088)
    tk = _col_tile(K, tk)
    return _mm_tn(name, a, b, pl.BlockSpec((tt, tk), lambda kb, t: (t, kb)), pl.BlockSpec((tt, N), lambda kb, t: (t, 0)),
                  jax.ShapeDtypeStruct((K, N), BF16), pl.BlockSpec((tk, N), lambda kb, t: (kb, 0)), (tk, N), (K // tk, T // tt))


def _wgrad_up(name, a, b3):
    T, K = a.shape
    J, _, k = b3.shape
    tt = _row_tile(T, 1088)
    return _mm_tn(name, a, b3, pl.BlockSpec((tt, K), lambda j, t: (t, 0)), pl.BlockSpec((None, tt, k), lambda j, t: (j, t, 0)),
                  jax.ShapeDtypeStruct((J, K, k), BF16), pl.BlockSpec((None, K, k), lambda j, t: (j, 0, 0)), (K, k), (J, T // tt))


def _wgrad_down(name, a3, b):
    J, T, k = a3.shape
    N = b.shape[1]
    tt = _row_tile(T, 1088)
    return _mm_tn(name, a3, b, pl.BlockSpec((None, tt, k), lambda j, t: (j, t, 0)), pl.BlockSpec((tt, N), lambda j, t: (t, 0)),
                  jax.ShapeDtypeStruct((J, k, N), BF16), pl.BlockSpec((None, k, N), lambda j, t: (j, 0, 0)), (k, N), (J, T // tt))


def _seg(i):
    return jnp.minimum(i, 1)


def _rstd(x):
    return lax.rsqrt(jnp.mean(x * x, axis=-1, keepdims=True) + EPS)


def _norm_mod(name, h, g, mod, which):
    T, D = h.shape

    def body(h_ref, g_ref, mod_ref, o_ref):
        x = h_ref[...]
        n = x * _rstd(x) * g_ref[...]
        shift = mod_ref[3 * which:3 * which + 1, :]
        scale = mod_ref[3 * which + 1:3 * which + 2, :]
        o_ref[...] = (n * (1 + scale) + shift).astype(o_ref.dtype)

    row = pl.BlockSpec((ROW_TILE, D), lambda i: (i, 0))
    return pl.pallas_call(
        body, name=name, grid=(T // ROW_TILE,),
        in_specs=[row, pl.BlockSpec((1, D), lambda i: (0, 0)), pl.BlockSpec((None, 8, D), lambda i: (_seg(i), 0, 0))],
        out_specs=row, out_shape=jax.ShapeDtypeStruct((T, D), BF16), compiler_params=_params())(h, g, mod)


def _norm_mod_bwd(name, dxn, h, g, mod, which, dres):
    T, D = h.shape

    def body(dxn_ref, h_ref, g_ref, mod_ref, dres_ref, dh_ref, dmod_ref, dg_ref):
        i = pl.program_id(0)
        x = h_ref[...]
        r = _rstd(x)
        xhat = x * r
        g = g_ref[...]
        n = xhat * g
        scale = mod_ref[3 * which + 1:3 * which + 2, :]
        dxn = dxn_ref[...]
        dn = dxn * (1 + scale)
        dxh = dn * g
        dh_ref[...] = dres_ref[...] + r * (dxh - xhat * jnp.mean(dxh * xhat, axis=-1, keepdims=True))

        @pl.when(i <= 1)
        def _():
            dmod_ref[...] = jnp.zeros_like(dmod_ref)

        @pl.when(i == 0)
        def _():
            dg_ref[...] = jnp.zeros_like(dg_ref)

        dmod_ref[3 * which:3 * which + 1, :] += jnp.sum(dxn, axis=0, keepdims=True)
        dmod_ref[3 * which + 1:3 * which + 2, :] += jnp.sum(dxn * n, axis=0, keepdims=True)
        dg_ref[0:1, :] += jnp.sum(dn * xhat, axis=0, keepdims=True)

    row = pl.BlockSpec((ROW_TILE, D), lambda i: (i, 0))
    modspec = pl.BlockSpec((None, 8, D), lambda i: (_seg(i), 0, 0))
    return pl.pallas_call(
        body, name=name, grid=(T // ROW_TILE,),
        in_specs=[row, row, pl.BlockSpec((1, D), lambda i: (0, 0)), modspec, row],
        out_specs=[row, modspec, pl.BlockSpec((8, D), lambda i: (0, 0))],
        out_shape=[jax.ShapeDtypeStruct((T, D), F32), jax.ShapeDtypeStruct((2, 8, D), F32), jax.ShapeDtypeStruct((8, D), F32)],
        compiler_params=_params())(dxn, h, g, mod, dres)


def _gate_res(name, h, y, mod, row_idx):
    T, D = h.shape

    def body(h_ref, y_ref, mod_ref, o_ref):
        o_ref[...] = h_ref[...] + mod_ref[row_idx:row_idx + 1, :] * y_ref[...]

    row = pl.BlockSpec((ROW_TILE, D), lambda i: (i, 0))
    return pl.pallas_call(
        body, name=name, grid=(T // ROW_TILE,),
        in_specs=[row, row, pl.BlockSpec((None, 8, D), lambda i: (_seg(i), 0, 0))],
        out_specs=row, out_shape=jax.ShapeDtypeStruct((T, D), F32), compiler_params=_params())(h, y, mod)


def _gate_bwd(name, dh, y, mod, row_idx):
    T, D = dh.shape

    def body(dh_ref, y_ref, mod_ref, dy_ref, dmod_ref):
        i = pl.program_id(0)
        dh = dh_ref[...]
        dy_ref[...] = (dh * mod_ref[row_idx:row_idx + 1, :]).astype(dy_ref.dtype)

        @pl.when(i <= 1)
        def _():
            dmod_ref[...] = jnp.zeros_like(dmod_ref)

        dmod_ref[row_idx:row_idx + 1, :] += jnp.sum(dh * y_ref[...], axis=0, keepdims=True)

    row = pl.BlockSpec((ROW_TILE, D), lambda i: (i, 0))
    modspec = pl.BlockSpec((None, 8, D), lambda i: (_seg(i), 0, 0))
    return pl.pallas_call(
        body, name=name, grid=(T // ROW_TILE,), in_specs=[row, row, modspec], out_specs=[row, modspec],
        out_shape=[jax.ShapeDtypeStruct((T, D), BF16), jax.ShapeDtypeStruct((2, 8, D), F32)],
        compiler_params=_params())(dh, y, mod)


def _rot(y):
    lane = lax.broadcasted_iota(jnp.int32, y.shape, 1)
    return jnp.where((lane & 32) == 0, pltpu.roll(y, 96, 1), pltpu.roll(y, 32, 1))


def _qk_prep(name, P, q_g, k_g, rope_c, rope_s, cfg):
    T = P.shape[0]
    ATT, KVW = cfg['ATT'], cfg['KVW']

    def body(q_ref, k_ref, v_ref, qg_ref, kg_ref, c_ref, s_ref, qo_ref, ko_ref, vo_ref):
        cc, ss = c_ref[...], s_ref[...]

        def head(x, g):
            y = x * _rstd(x) * g
            return y * cc + _rot(y) * ss

        for hh in range(ATT // HEAD_DIM):
            sl = slice(hh * HEAD_DIM, (hh + 1) * HEAD_DIM)
            qo_ref[:, sl] = (head(q_ref[:, sl], qg_ref[...]) * Q_SCALE).astype(qo_ref.dtype)
        for hh in range(KVW // HEAD_DIM):
            sl = slice(hh * HEAD_DIM, (hh + 1) * HEAD_DIM)
            ko_ref[:, sl] = head(k_ref[:, sl], kg_ref[...]).astype(ko_ref.dtype)
        vo_ref[...] = v_ref[...].astype(vo_ref.dtype)

    kb = ATT // KVW
    gain = pl.BlockSpec((1, HEAD_DIM), lambda i: (0, 0))
    tab = pl.BlockSpec((ROW_TILE, HEAD_DIM), lambda i: (i, 0))
    qs = pl.BlockSpec((ROW_TILE, ATT), lambda i: (i, 0))
    ks = pl.BlockSpec((ROW_TILE, KVW), lambda i: (i, 0))
    return pl.pallas_call(
        body, name=name, grid=(T // ROW_TILE,),
        in_specs=[qs, pl.BlockSpec((ROW_TILE, KVW), lambda i: (i, kb)), pl.BlockSpec((ROW_TILE, KVW), lambda i: (i, kb + 1)),
                  gain, gain, tab, tab],
        out_specs=[qs, ks, ks],
        out_shape=[jax.ShapeDtypeStruct((T, ATT), BF16), jax.ShapeDtypeStruct((T, KVW), BF16), jax.ShapeDtypeStruct((T, KVW), BF16)],
        compiler_params=_params())(P, P, P, q_g, k_g, rope_c, rope_s)


def _qk_prep_bwd(name, dqr, dkr, P, q_g, k_g, rope_c, rope_s, cfg):
    T = P.shape[0]
    ATT, KVW = cfg['ATT'], cfg['KVW']

    def body(dq_ref, dk_ref, q_ref, k_ref, qg_ref, kg_ref, c_ref, s_ref, dqo_ref, dko_ref, dqg_ref, dkg_ref):
        i = pl.program_id(0)
        cc, ss = c_ref[...], s_ref[...]

        @pl.when(i == 0)
        def _():
            dqg_ref[...] = jnp.zeros_like(dqg_ref)
            dkg_ref[...] = jnp.zeros_like(dkg_ref)

        def head(x, g, dout):
            dy = dout * cc + _rot(dout * ss)
            r = _rstd(x)
            xhat = x * r
            dxh = dy * g
            dx = r * (dxh - xhat * jnp.mean(dxh * xhat, axis=-1, keepdims=True))
            return dx, jnp.sum(dy * xhat, axis=0, keepdims=True)

        dg = jnp.zeros((1, HEAD_DIM), F32)
        for hh in range(ATT // HEAD_DIM):
            sl = slice(hh * HEAD_DIM, (hh + 1) * HEAD_DIM)
            dx, d = head(q_ref[:, sl], qg_ref[...], dq_ref[:, sl] * ATT_SCALE)
            dqo_ref[:, sl] = dx.astype(dqo_ref.dtype)
            dg += d
        dqg_ref[0:1, :] += dg
        dg = jnp.zeros((1, HEAD_DIM), F32)
        for hh in range(KVW // HEAD_DIM):
            sl = slice(hh * HEAD_DIM, (hh + 1) * HEAD_DIM)
            dx, d = head(k_ref[:, sl], kg_ref[...], dk_ref[:, sl] * (1.0 / LOG2_E))
            dko_ref[:, sl] = dx.astype(dko_ref.dtype)
            dg += d
        dkg_ref[0:1, :] += dg

    kb = ATT // KVW
    gain = pl.BlockSpec((1, HEAD_DIM), lambda i: (0, 0))
    dgain = pl.BlockSpec((8, HEAD_DIM), lambda i: (0, 0))
    tab = pl.BlockSpec((ROW_TILE, HEAD_DIM), lambda i: (i, 0))
    qs = pl.BlockSpec((ROW_TILE, ATT), lambda i: (i, 0))
    ks = pl.BlockSpec((ROW_TILE, KVW), lambda i: (i, 0))
    return pl.pallas_call(
        body, name=name, grid=(T // ROW_TILE,),
        in_specs=[qs, ks, qs, pl.BlockSpec((ROW_TILE, KVW), lambda i: (i, kb)), gain, gain, tab, tab],
        out_specs=[qs, ks, dgain, dgain],
        out_shape=[jax.ShapeDtypeStruct((T, ATT), BF16), jax.ShapeDtypeStruct((T, KVW), BF16),
                   jax.ShapeDtypeStruct((8, HEAD_DIM), F32), jax.ShapeDtypeStruct((8, HEAD_DIM), F32)],
        compiler_params=_params())(dqr, dkr, P, P, q_g, k_g, rope_c, rope_s)


def _att_specs(T, G):
    qs = pl.BlockSpec((ROW_TILE, G * HEAD_DIM), lambda h, i: (i, h))
    kvs = pl.BlockSpec((T, HEAD_DIM), lambda h, i: (0, h))
    return qs, kvs


def _attn_dense_fwd(name, q, k, v, cfg):
    T, G, Lc = q.shape[0], cfg['G'], cfg['Lc']

    def body(q_ref, k_ref, v_ref, o_ref, lse_ref):
        def attend(rows):
            kk, vv = k_ref[0:rows, :], v_ref[0:rows, :]
            for g in range(G):
                sl = slice(g * HEAD_DIM, (g + 1) * HEAD_DIM)
                s = _nt(q_ref[:, sl], kk)
                m = jnp.max(s, axis=1, keepdims=True)
                p = jnp.exp2(s - m)
                l = jnp.sum(p, axis=1, keepdims=True)
                o_ref[:, sl] = _nn(_bf(p), vv) / l
                lse_ref[:, sl] = jnp.broadcast_to(m + jnp.log2(l), (ROW_TILE, HEAD_DIM))

        @pl.when(pl.program_id(1) == 0)
        def _():
            attend(Lc)

        @pl.when(pl.program_id(1) > 0)
        def _():
            attend(T)

    qs, kvs = _att_specs(T, G)
    return pl.pallas_call(
        body, name=name, grid=(cfg['NKV'], T // ROW_TILE), in_specs=[qs, kvs, kvs], out_specs=[qs, qs],
        out_shape=[jax.ShapeDtypeStruct(q.shape, F32), jax.ShapeDtypeStruct(q.shape, F32)],
        compiler_params=_params())(q, k, v)


def _attn_dense_bwd(name, q, k, v, o, lse, dmix, cfg):
    T, G, Lc = q.shape[0], cfg['G'], cfg['Lc']

    def body(q_ref, k_ref, v_ref, o_ref, lse_ref, do_ref, dq_ref, dk_ref, dv_ref):
        i = pl.program_id(1)

        @pl.when(i == 0)
        def _():
            dk_ref[...] = jnp.zeros_like(dk_ref)
            dv_ref[...] = jnp.zeros_like(dv_ref)

        def attend(rows):
            kk, vv = k_ref[0:rows, :], v_ref[0:rows, :]
            for g in range(G):
                sl = slice(g * HEAD_DIM, (g + 1) * HEAD_DIM)
                qg, do = q_ref[:, sl], do_ref[:, sl]
                delta = jnp.sum(do * o_ref[:, sl], axis=1, keepdims=True)
                p = jnp.exp2(_nt(qg, kk) - lse_ref[:, g * HEAD_DIM:g * HEAD_DIM + 1])
                dob = _bf(do)
                dv_ref[0:rows, :] += _tn(_bf(p), dob)
                ds = _bf(p * (_nt(dob, vv) - delta))
                dq_ref[:, sl] = _nn(ds, kk)
                dk_ref[0:rows, :] += _tn(ds, qg)

        @pl.when(i == 0)
        def _():
            attend(Lc)

        @pl.when(i > 0)
        def _():
            attend(T)

    qs, kvs = _att_specs(T, G)
    return pl.pallas_call(
        body, name=name, grid=(cfg['NKV'], T // ROW_TILE), in_specs=[qs, kvs, kvs, qs, qs, qs], out_specs=[qs, kvs, kvs],
        out_shape=[jax.ShapeDtypeStruct(q.shape, F32), jax.ShapeDtypeStruct(k.shape, F32), jax.ShapeDtypeStruct(k.shape, F32)],
        compiler_params=_params())(q, k, v, o, lse, dmix)


def _band(i, T, Lc):
    start = pl.multiple_of(jnp.clip(WINDOW + (i - 1) * ROW_TILE, 0, T - BAND), WINDOW)
    qpos = (i - 1) * ROW_TILE + lax.broadcasted_iota(jnp.int32, (ROW_TILE, 1), 0)
    kpos = start - Lc + lax.broadcasted_iota(jnp.int32, (1, BAND), 1)
    ok = (jnp.abs(kpos - qpos) <= WINDOW) & (kpos >= 0) & (i > 0)
    return start, jnp.where(ok, 0.0, NEG_INF).astype(F32)


def _attn_win_fwd(name, q, k, v, sink, cfg):
    T, G, Lc = q.shape[0], cfg['G'], cfg['Lc']

    def body(sink_ref, q_ref, k_ref, v_ref, o_ref, lse_ref):
        h, i = pl.program_id(0), pl.program_id(1)
        start, bias = _band(i, T, Lc)
        kc, vc = k_ref[0:Lc, :], v_ref[0:Lc, :]
        kb, vb = k_ref[pl.ds(start, BAND), :], v_ref[pl.ds(start, BAND), :]
        for g in range(G):
            sl = slice(g * HEAD_DIM, (g + 1) * HEAD_DIM)
            qg = q_ref[:, sl]
            sk = sink_ref[h * G + g] * LOG2_E
            sc = _nt(qg, kc)
            sb = _nt(qg, kb) + bias
            m = jnp.maximum(jnp.maximum(jnp.max(sc, axis=1, keepdims=True), jnp.max(sb, axis=1, keepdims=True)), sk)
            pc, pb = jnp.exp2(sc - m), jnp.exp2(sb - m)
            l = jnp.sum(pc, axis=1, keepdims=True) + jnp.sum(pb, axis=1, keepdims=True) + jnp.exp2(sk - m)
            o_ref[:, sl] = (_nn(_bf(pc), vc) + _nn(_bf(pb), vb)) / l
            lse_ref[:, sl] = jnp.broadcast_to(m + jnp.log2(l), (ROW_TILE, HEAD_DIM))

    qs, kvs = _att_specs(T, G)
    return pl.pallas_call(
        body, name=name, grid=(cfg['NKV'], T // ROW_TILE),
        in_specs=[pl.BlockSpec(memory_space=pltpu.SMEM), qs, kvs, kvs], out_specs=[qs, qs],
        out_shape=[jax.ShapeDtypeStruct(q.shape, F32), jax.ShapeDtypeStruct(q.shape, F32)],
        compiler_params=_params())(sink, q, k, v)


def _attn_win_bwd(name, q, k, v, o, lse, dmix, sink, cfg):
    T, G, Lc = q.shape[0], cfg['G'], cfg['Lc']

    def body(sink_ref, q_ref, k_ref, v_ref, o_ref, lse_ref, do_ref, dq_ref, dk_ref, dv_ref, dsink_ref):
        h, i = pl.program_id(0), pl.program_id(1)
        start, bias = _band(i, T, Lc)
        kc, vc = k_ref[0:Lc, :], v_ref[0:Lc, :]
        kb, vb = k_ref[pl.ds(start, BAND), :], v_ref[pl.ds(start, BAND), :]

        @pl.when(i == 0)
        def _():
            dk_ref[...] = jnp.zeros_like(dk_ref)
            dv_ref[...] = jnp.zeros_like(dv_ref)
            dsink_ref[...] = jnp.zeros_like(dsink_ref)

        for g in range(G):
            sl = slice(g * HEAD_DIM, (g + 1) * HEAD_DIM)
            qg, do = q_ref[:, sl], do_ref[:, sl]
            lse = lse_ref[:, g * HEAD_DIM:g * HEAD_DIM + 1]
            delta = jnp.sum(do * o_ref[:, sl], axis=1, keepdims=True)
            pc = jnp.exp2(_nt(qg, kc) - lse)
            pb = jnp.exp2(_nt(qg, kb) + bias - lse)
            ps = jnp.exp2(sink_ref[h * G + g] * LOG2_E - lse)
            dob = _bf(do)
            dv_ref[0:Lc, :] += _tn(_bf(pc), dob)
            dv_ref[pl.ds(start, BAND), :] += _tn(_bf(pb), dob)
            dsc = _bf(pc * (_nt(dob, vc) - delta))
            dsb = _bf(pb * (_nt(dob, vb) - delta))
            dq_ref[:, sl] = _nn(dsc, kc) + _nn(dsb, kb)
            dk_ref[0:Lc, :] += _tn(dsc, qg)
            dk_ref[pl.ds(start, BAND), :] += _tn(dsb, qg)
            dsk = jnp.where(i > 0, -jnp.sum(ps * delta, axis=0, keepdims=True), 0.0)
            dsink_ref[:, sl] += jnp.broadcast_to(dsk, (8, HEAD_DIM))

    qs, kvs = _att_specs(T, G)
    return pl.pallas_call(
        body, name=name, grid=(cfg['NKV'], T // ROW_TILE),
        in_specs=[pl.BlockSpec(memory_space=pltpu.SMEM), qs, kvs, kvs, qs, qs, qs],
        out_specs=[qs, kvs, kvs, pl.BlockSpec((None, 8, G * HEAD_DIM), lambda h, i: (h, 0, 0))],
        out_shape=[jax.ShapeDtypeStruct(q.shape, F32), jax.ShapeDtypeStruct(k.shape, F32), jax.ShapeDtypeStruct(k.shape, F32),
                   jax.ShapeDtypeStruct((cfg['NKV'], 8, G * HEAD_DIM), F32)],
        compiler_params=_params())(sink, q, k, v, o, lse, dmix)


def _seq_pos(T, Lc):
    row = lax.broadcasted_iota(jnp.int32, (T, 1), 0)
    return jnp.where(row < Lc, row, row - Lc), jnp.where(row < Lc, Lc, T - Lc)


def _fw(x, k, pos, seglen):
    return jnp.where(pos + k < seglen, pltpu.roll(x, x.shape[0] - k, 0), 0.0)


def _bw(x, k, pos):
    return jnp.where(pos - k >= 0, pltpu.roll(x, k, 0), 0.0)


def _conv_fwd(name, P, conv_w8, cfg):
    T, Lc = P.shape[0], cfg['Lc']
    cb = (cfg['ATT'] + 2 * cfg['KVW']) // HEAD_DIM
    na = AUX_WIDTH // HEAD_DIM

    def body(gb_ref, gc_ref, u_ref, w_ref, o_ref):
        pos, seglen = _seq_pos(T, Lc)
        z = gc_ref[...] * u_ref[...]
        conv = w_ref[0:1, :] * _bw(z, 1, pos) + w_ref[1:2, :] * z + w_ref[2:3, :] * _fw(z, 1, pos, seglen)
        o_ref[...] = gb_ref[...] * conv

    col = lambda off: pl.BlockSpec((T, HEAD_DIM), lambda c: (0, cb + off + c))
    return pl.pallas_call(
        body, name=name, grid=(na,),
        in_specs=[col(0), col(na), col(2 * na), pl.BlockSpec((8, HEAD_DIM), lambda c: (0, c))],
        out_specs=pl.BlockSpec((T, HEAD_DIM), lambda c: (0, c)),
        out_shape=jax.ShapeDtypeStruct((T, AUX_WIDTH), F32), compiler_params=_params())(P, P, P, conv_w8)


def _conv_bwd(name, P, conv_w8, dmix, cfg):
    T, Lc = P.shape[0], cfg['Lc']
    cb = (cfg['ATT'] + 2 * cfg['KVW']) // HEAD_DIM
    ob = cfg['ATT'] // HEAD_DIM
    na = AUX_WIDTH // HEAD_DIM

    def body(gb_ref, gc_ref, u_ref, w_ref, do_ref, dgb_ref, dgc_ref, du_ref, dw_ref):
        pos, seglen = _seq_pos(T, Lc)
        gc, u, do = gc_ref[...], u_ref[...], do_ref[...]
        z = gc * u
        zm, zp = _bw(z, 1, pos), _fw(z, 1, pos, seglen)
        w0, w1, w2 = w_ref[0:1, :], w_ref[1:2, :], w_ref[2:3, :]
        dgb_ref[...] = (do * (w0 * zm + w1 * z + w2 * zp)).astype(dgb_ref.dtype)
        dc = do * gb_ref[...]
        dz = w0 * _fw(dc, 1, pos, seglen) + w1 * dc + w2 * _bw(dc, 1, pos)
        dgc_ref[...] = (dz * u).astype(dgc_ref.dtype)
        du_ref[...] = (dz * gc).astype(du_ref.dtype)
        dw_ref[...] = jnp.zeros_like(dw_ref)
        dw_ref[0:1, :] = jnp.sum(dc * zm, axis=0, keepdims=True)
        dw_ref[1:2, :] = jnp.sum(dc * z, axis=0, keepdims=True)
        dw_ref[2:3, :] = jnp.sum(dc * zp, axis=0, keepdims=True)

    col = lambda off: pl.BlockSpec((T, HEAD_DIM), lambda c: (0, cb + off + c))
    wspec = pl.BlockSpec((8, HEAD_DIM), lambda c: (0, c))
    ocol = lambda off: pl.BlockSpec((T, HEAD_DIM), lambda c: (0, off + c))
    return pl.pallas_call(
        body, name=name, grid=(na,),
        in_specs=[col(0), col(na), col(2 * na), wspec, ocol(ob)],
        out_specs=[ocol(0), ocol(0), ocol(0), wspec],
        out_shape=[jax.ShapeDtypeStruct((T, AUX_WIDTH), BF16)] * 3 + [jax.ShapeDtypeStruct((8, AUX_WIDTH), F32)],
        compiler_params=_params())(P, P, P, conv_w8, dmix)


def _window_sums(x, half, pos, seglen):
    fwd, bwd = x, x
    s = 1
    while s < half:
        fwd = fwd + _fw(fwd, s, pos, seglen)
        bwd = bwd + _bw(bwd, s, pos)
        s *= 2
    return fwd, bwd


def _pooled(u, half, pos, seglen):
    fwd, bwd = _window_sums(u, half, pos, seglen)
    cnt = (jnp.minimum(pos + half, seglen) - jnp.maximum(pos - half, 0)).astype(F32)
    return (fwd + _bw(bwd, 1, pos)) / cnt - u, cnt


def _pool_fwd(name, P, pool_w, pool_scale, cfg):
    T, Lc = P.shape[0], cfg['Lc']
    cb = (cfg['ATT'] + 2 * cfg['KVW']) // HEAD_DIM

    def body(u_ref, w_ref, s_ref, o_ref):
        g = pl.program_id(0)
        pos, seglen = _seq_pos(T, Lc)
        for k, half in enumerate(POOL_HALF):
            @pl.when(g == k)
            def _(half=half):
                pooled, _ = _pooled(u_ref[...], half, pos, seglen)
                o_ref[...] = _nn(_bf(pooled), _bf(w_ref[...])) * s_ref[...]

    return pl.pallas_call(
        body, name=name, grid=(AUX_GROUPS,),
        in_specs=[pl.BlockSpec((T, HEAD_DIM), lambda g: (0, cb + g)), pl.BlockSpec((None, HEAD_DIM, HEAD_DIM), lambda g: (g, 0, 0)),
                  pl.BlockSpec((1, HEAD_DIM), lambda g: (0, g))],
        out_specs=pl.BlockSpec((T, HEAD_DIM), lambda g: (0, g)),
        out_shape=jax.ShapeDtypeStruct((T, AUX_WIDTH), F32), compiler_params=_params())(P, pool_w, pool_scale)


def _pool_bwd(name, P, pool_w, pool_scale, dmix, cfg):
    T, Lc = P.shape[0], cfg['Lc']
    cb = (cfg['ATT'] + 2 * cfg['KVW']) // HEAD_DIM
    ob = cfg['ATT'] // HEAD_DIM

    def body(u_ref, w_ref, s_ref, do_ref, du_ref, dw_ref, ds_ref):
        g = pl.program_id(0)
        pos, seglen = _seq_pos(T, Lc)
        for k, half in enumerate(POOL_HALF):
            @pl.when(g == k)
            def _(half=half):
                do = do_ref[...]
                pooled, cnt = _pooled(u_ref[...], half, pos, seglen)
                wb = _bf(w_ref[...])
                mixed = _nn(_bf(pooled), wb)
                ds_ref[...] = jnp.broadcast_to(jnp.sum(do * mixed, axis=0, keepdims=True), ds_ref.shape)
                dmixed = _bf(do * s_ref[...])
                dw_ref[...] = _tn(_bf(pooled), dmixed)
                dpooled = _nt(dmixed, wb)
                e = dpooled / cnt
                fwd, bwd = _window_sums(e, half, pos, seglen)
                adj = fwd + _fw(e, half, pos, seglen) + _bw(bwd, 1, pos) - _bw(e, half, pos)
                du_ref[...] = (adj - dpooled).astype(du_ref.dtype)

    wspec = pl.BlockSpec((None, HEAD_DIM, HEAD_DIM), lambda g: (g, 0, 0))
    return pl.pallas_call(
        body, name=name, grid=(AUX_GROUPS,),
        in_specs=[pl.BlockSpec((T, HEAD_DIM), lambda g: (0, cb + g)), wspec, pl.BlockSpec((1, HEAD_DIM), lambda g: (0, g)),
                  pl.BlockSpec((T, HEAD_DIM), lambda g: (0, ob + g))],
        out_specs=[pl.BlockSpec((T, HEAD_DIM), lambda g: (0, g)), wspec, pl.BlockSpec((8, HEAD_DIM), lambda g: (0, g))],
        out_shape=[jax.ShapeDtypeStruct((T, AUX_WIDTH), BF16), jax.ShapeDtypeStruct(pool_w.shape, F32),
                   jax.ShapeDtypeStruct((8, AUX_WIDTH), F32)],
        compiler_params=_params())(P, pool_w, pool_scale, dmix)


def _ffn_up(name, hn, wg3, wu3, dep=None):
    T, D = hn.shape
    J, k, _ = wg3.shape
    tm = _row_tile(T, 1088)
    dep_ins, dep_specs = _dep(dep, 2)

    def body(x_ref, wg_ref, wu_ref, *rest):
        g_ref, u_ref, a_ref = rest[len(dep_ins):]
        x = x_ref[...]
        g, u = _nt(x, wg_ref[...]), _nt(x, wu_ref[...])
        g_ref[...] = g.astype(g_ref.dtype)
        u_ref[...] = u.astype(u_ref.dtype)
        a_ref[...] = (g * jax.nn.sigmoid(g) * u).astype(a_ref.dtype)

    wspec = pl.BlockSpec((None, k, D), lambda j, i: (j, 0, 0))
    ospec = pl.BlockSpec((None, tm, k), lambda j, i: (j, i, 0))
    return pl.pallas_call(
        body, name=name, grid=(J, T // tm), in_specs=[pl.BlockSpec((tm, D), lambda j, i: (i, 0)), wspec, wspec] + dep_specs,
        out_specs=[ospec, ospec, ospec],
        out_shape=[jax.ShapeDtypeStruct((J, T, k), BF16)] * 3,
        compiler_params=_params())(hn, wg3, wu3, *dep_ins)


def _ffn_dact(name, dF, wd3, G, U):
    T, D = dF.shape
    J, k, _ = wd3.shape
    tm = _row_tile(T, 1088)

    def body(df_ref, wd_ref, g_ref, u_ref, dg_ref, du_ref):
        da = _nt(df_ref[...], wd_ref[...])
        g = g_ref[...].astype(F32)
        sig = jax.nn.sigmoid(g)
        du_ref[...] = (da * g * sig).astype(du_ref.dtype)
        dg_ref[...] = (da * u_ref[...].astype(F32) * (sig * (1 + g * (1 - sig)))).astype(dg_ref.dtype)

    aspec = pl.BlockSpec((None, tm, k), lambda j, i: (j, i, 0))
    return pl.pallas_call(
        body, name=name, grid=(J, T // tm),
        in_specs=[pl.BlockSpec((tm, D), lambda j, i: (i, 0)), pl.BlockSpec((None, k, D), lambda j, i: (j, 0, 0)), aspec, aspec],
        out_specs=[aspec, aspec],
        out_shape=[jax.ShapeDtypeStruct((J, T, k), BF16), jax.ShapeDtypeStruct((J, T, k), BF16)],
        compiler_params=_params())(dF, wd3, G, U)


def _loss_head(name, h, g, target, cfg):
    T, D = h.shape

    def body(h_ref, g_ref, t_ref, dh_ref, loss_ref, dg_ref):
        i = pl.program_id(0)

        @pl.when(i == 0)
        def _():
            dh_ref[...] = jnp.zeros_like(dh_ref)
            loss_ref[...] = jnp.zeros_like(loss_ref)
            dg_ref[...] = jnp.zeros_like(dg_ref)

        @pl.when(i > 0)
        def _():
            x = h_ref[...]
            r = _rstd(x)
            xhat = x * r
            gg = g_ref[...]
            err = xhat * gg - t_ref[...]
            loss_ref[...] += 0.5 * jnp.sum(jnp.sum(err * err, axis=1, keepdims=True) / D, axis=0, keepdims=True)
            dy = err / D
            dg_ref[0:1, :] += jnp.sum(dy * xhat, axis=0, keepdims=True)
            dxh = dy * gg
            dh_ref[...] = r * (dxh - xhat * jnp.mean(dxh * xhat, axis=-1, keepdims=True))

    row = pl.BlockSpec((ROW_TILE, D), lambda i: (i, 0))
    return pl.pallas_call(
        body, name=name, grid=(T // ROW_TILE,),
        in_specs=[row, pl.BlockSpec((1, D), lambda i: (0, 0)), pl.BlockSpec((ROW_TILE, D), lambda i: (jnp.maximum(i - 1, 0), 0))],
        out_specs=[row, pl.BlockSpec((8, 128), lambda i: (0, 0)), pl.BlockSpec((8, D), lambda i: (0, 0))],
        out_shape=[jax.ShapeDtypeStruct((T, D), F32), jax.ShapeDtypeStruct((8, 128), F32), jax.ShapeDtypeStruct((8, D), F32)],
        compiler_params=_params())(h, g, target)


def _adamw(name, parts, w, m, v, dep=None):
    R, C = w.shape
    n_parts = parts.shape[0]
    tr = _row_tile(R, max(16, (1 << 18) // C)) if R % 16 == 0 else R
    bc1 = 1.0 - ADAM_B1 ** ADAM_STEP
    bc2 = 1.0 - ADAM_B2 ** ADAM_STEP
    dep_ins, dep_specs = _dep(dep, 1)

    def body(p_ref, w_ref, m_ref, v_ref, *rest):
        g_ref, d_ref, nm_ref, nv_ref = rest[len(dep_ins):]
        g = p_ref[0].astype(F32)
        for k in range(1, n_parts):
            g = g + p_ref[k].astype(F32)
        nm = ADAM_B1 * m_ref[...] + (1.0 - ADAM_B1) * g
        nv = ADAM_B2 * v_ref[...] + (1.0 - ADAM_B2) * (g * g)
        g_ref[...] = g
        nm_ref[...] = nm
        nv_ref[...] = nv
        d_ref[...] = -ADAM_LR * ((nm / bc1) / (jnp.sqrt(nv / bc2) + ADAM_EPS) + ADAM_WD * w_ref[...])

    blk = pl.BlockSpec((tr, C), lambda i: (i, 0))
    return pl.pallas_call(
        body, name=name, grid=(R // tr,), in_specs=[pl.BlockSpec((n_parts, tr, C), lambda i: (0, i, 0)), blk, blk, blk] + dep_specs,
        out_specs=[blk] * 4, out_shape=[jax.ShapeDtypeStruct((R, C), F32)] * 4, compiler_params=_params())(parts, w, m, v, *dep_ins)


class _WeightStream:
    def __init__(self, cast):
        self.cast, self.handles = cast, {}

    @staticmethod
    def _tag(l, group):
        return ("ffn" if group is FFN_WEIGHTS else group[0]) + str(l)

    def start(self, l, group, after=None):
        self.handles[l, group], token = _exchange_start(
            f"gather_{self._tag(l, group)}_start", [self.cast(l, n) for n in group], False, after)
        return token

    def get(self, l, group, after):
        got = dict(zip(group, _exchange_wait(f"gather_{self._tag(l, group)}_wait", self.handles[l, group], after)))
        if 'w_out' in got:
            rows, cols = got['w_out'].shape[1:]
            got['w_out'] = got['w_out'].reshape(N_DEV * rows, cols)
        return got


def _layer_fwd(l, h, p, stream, mod, rope, conv_w8, cfg):
    nm = f"l{l}_"
    xn = _norm_mod(nm + "norm1", h, p['norm1_g'], mod, 0)
    W = stream.get(l, IN_WEIGHT, xn)
    token = None
    if l == 0:
        token = stream.start(0, OUT_WEIGHT, after=W['w_in']) + stream.start(0, FFN_WEIGHTS, after=W['w_in'])
    P = _mm_cols(nm + "w_in", xn, W['w_in'], dep=token)
    qr, kr, vb = _qk_prep(nm + "qk_prep", P, p['q_norm_g'], p['k_norm_g'], rope[0], rope[1], cfg)
    if l == 0:
        o, lse = _attn_dense_fwd(nm + "attn", qr, kr, vb, cfg)
        aux = _conv_fwd(nm + "conv", P, conv_w8, cfg)
    else:
        o, lse = _attn_win_fwd(nm + "attn", qr, kr, vb, p['sink'], cfg)
        aux = _pool_fwd(nm + "pool", P, p['pool_w'], p['pool_scale'], cfg)
    mix = jnp.concatenate([o, aux], axis=1).astype(BF16)
    W.update(stream.get(l, OUT_WEIGHT, mix))
    y = _mm_plain(nm + "w_out", mix, W['w_out'], False)
    h2 = _gate_res(nm + "res1", h, y, mod, 2)
    hn = _norm_mod(nm + "norm2", h2, p['norm2_g'], mod, 1)
    W.update(stream.get(l, FFN_WEIGHTS, hn))
    token = stream.start(1, IN_WEIGHT, after=W['w_down']) if l == 0 else None
    G, U, A = _ffn_up(nm + "ffn_up", hn, W['w_gate'], W['w_up'], dep=token)
    if l == 0:
        token = stream.start(1, OUT_WEIGHT, after=A) + stream.start(1, FFN_WEIGHTS, after=A)
    F = _mm_shards_nn(nm + "w_down", A, W['w_down'], dep=token)
    h3 = _gate_res(nm + "res2", h2, F, mod, 5)
    saved = dict(h=h, xn=xn, P=P, qr=qr, kr=kr, vb=vb, o=o, lse=lse, mix=mix, y=y, h2=h2, hn=hn, G=G, U=U, A=A, F=F)
    return h3, saved, W


def _layer_bwd(l, dh3, s, p, W, mod, rope, conv_w8, cfg):
    nm = f"l{l}_bwd_"
    J = N_DEV
    dF, dmod = _gate_bwd(nm + "res2", dh3, s['F'], mod, 5)
    dG, dU = _ffn_dact(nm + "ffn_act", dF, W['w_down'], s['G'], s['U'])
    big = {'w_down': _wgrad_down(nm + "dw_down", s['A'], dF),
           'w_gate': _wgrad_down(nm + "dw_gate", dG, s['hn']),
           'w_up': _wgrad_down(nm + "dw_up", dU, s['hn'])}
    handles = {}
    handles['ffn'], token = _exchange_start(f"scatter_ffn{l}_start", [big[n] for n in FFN_WEIGHTS], True)
    mod = mod + token[0, 0]
    dhn = _mm_shards_nn2(nm + "dhn", dG, W['w_gate'], dU, W['w_up'])
    dh2, dm, dg2 = _norm_mod_bwd(nm + "norm2", dhn, s['h2'], p['norm2_g'], mod, 1, dh3)
    dmod += dm
    dY, dm = _gate_bwd(nm + "res1", dh2, s['y'], mod, 2)
    dmod += dm
    dwo = _wgrad_rows(nm + "dw_out", s['mix'], dY)
    handles['w_out'], token = _exchange_start(f"scatter_w_out{l}_start", [dwo.reshape((J, dwo.shape[0] // J, dwo.shape[1]))], True)
    dmix = _mm_plain(nm + "dmix", dY, W['w_out'], True, dep=token)
    small = {'norm2_g': dg2[0]}
    if l == 0:
        dqr, dkr, dv = _attn_dense_bwd(nm + "attn", s['qr'], s['kr'], s['vb'], s['o'], s['lse'], dmix, cfg)
        *daux, dcw = _conv_bwd(nm + "conv", s['P'], conv_w8, dmix, cfg)
        small['conv_w'] = dcw[0:3]
    else:
        dqr, dkr, dv, dsk = _attn_win_bwd(nm + "attn", s['qr'], s['kr'], s['vb'], s['o'], s['lse'], dmix, p['sink'], cfg)
        du, dpw, dps = _pool_bwd(nm + "pool", s['P'], p['pool_w'], p['pool_scale'], dmix, cfg)
        daux = [du]
        small.update(sink=dsk[:, 0, ::HEAD_DIM].reshape(-1), pool_w=dpw, pool_scale=dps[0])
    dq, dk, dqg, dkg = _qk_prep_bwd(nm + "qk_prep", dqr, dkr, s['P'], p['q_norm_g'], p['k_norm_g'], rope[0], rope[1], cfg)
    small.update(q_norm_g=dqg[0], k_norm_g=dkg[0])
    dP = jnp.concatenate([dq, dk, dv.astype(BF16), *daux], axis=1)
    dw_in = _wgrad_cols(nm + "dw_in", s['xn'], dP, J)
    if l == 0:
        handles['w_in'] = dw_in
    else:
        handles['w_in'], token = _exchange_start(f"scatter_w_in{l}_start", [dw_in], True)
        mod = mod + token[0, 0]
    dxn = _mm_cols_nt(nm + "dxn", dP, W['w_in'])
    dh, dm, dg1 = _norm_mod_bwd(nm + "norm1", dxn, s['h'], p['norm1_g'], mod, 0, dh2)
    dmod += dm
    small['norm1_g'] = dg1[0]
    return dh, dmod, small, handles, token


def _rope_tables(S, Lc):
    half = HEAD_DIM // 4
    pos = np.arange(S)
    inv = ROPE_THETA ** (-np.arange(0, 2 * half, 2, dtype=np.float32) / (2 * half))
    inv = jnp.asarray(inv, F32)
    ang_r = jnp.asarray(pos // GRID_W, F32)[:, None] * inv
    ang_c = jnp.asarray(pos % GRID_W, F32)[:, None] * inv
    cos = jnp.concatenate([jnp.cos(ang_r)] * 2 + [jnp.cos(ang_c)] * 2, axis=1)
    sin = jnp.concatenate([-jnp.sin(ang_r), jnp.sin(ang_r), -jnp.sin(ang_c), jnp.sin(ang_c)], axis=1)
    return (jnp.concatenate([jnp.ones((Lc, HEAD_DIM), F32), cos], axis=0),
            jnp.concatenate([jnp.zeros((Lc, HEAD_DIM), F32), sin], axis=0))


def _pad_rows(a, rows):
    return jnp.concatenate([a, jnp.zeros((rows - a.shape[0],) + a.shape[1:], a.dtype)], axis=0)


def _flat128(a, nlead):
    lead = a.shape[:nlead]
    f = a.reshape(lead + (-1,))
    pad = (-f.shape[-1]) % 128
    if pad:
        f = jnp.concatenate([f, jnp.zeros(lead + (pad,), f.dtype)], axis=-1)
    return f.reshape(lead + (-1, 128))


def _pack(named, nlead=0):
    rows, layout, at = [], {}, 0
    for name, a in named:
        f = _flat128(a, nlead)
        n = f.shape[-2]
        pad = (-n) % 8
        if pad:
            f = jnp.concatenate([f, jnp.zeros(f.shape[:-2] + (pad, 128), f.dtype)], axis=-2)
        layout[name] = (at, n, a.shape[nlead:])
        rows.append(f)
        at += n + pad
    return jnp.concatenate(rows, axis=-2), layout


def _unpack(arr, layout, name):
    at, n, shape = layout[name]
    return arr[..., at:at + n, :].reshape(arr.shape[:-2] + (-1,))[..., :math.prod(shape)].reshape(arr.shape[:-2] + tuple(shape))


def kernel(*args):
    A = dict(zip(INPUT_NAMES, args, strict=True))
    x, ctx = A['x'][0], A['ctx'][0]
    S, D = x.shape
    Lc = ctx.shape[0]
    T = Lc + S
    ATT = D - AUX_WIDTH
    KVW = (A['l1_w_in'].shape[1] * N_DEV - ATT - AUX_WIDTH) // 2
    cfg = dict(ATT=ATT, KVW=KVW, NKV=KVW // HEAD_DIM, G=ATT // KVW, Lc=Lc)
    assert Lc == ROW_TILE and S % ROW_TILE == 0 and T >= BAND and S % GRID_W == 0
    cw = A['l0_conv_w'].shape[1]
    me = 4 * lax.axis_index("x") + 2 * lax.axis_index("y") + lax.axis_index("c")

    def layer_params(l):
        pre = f"l{l}_"
        return {k[len(pre):]: (v.reshape(1, -1) if v.ndim == 1 and k != 'l1_sink' else v) for k, v in A.items() if k.startswith(pre)}

    params = [layer_params(0), layer_params(1)]

    def cast(l, n):
        w = A[f'l{l}_{n}']
        return (w.T if n in TRANSPOSED else w).astype(BF16)

    stream = _WeightStream(cast)

    sc_own = jax.nn.silu(A['c'])
    first, lay0 = _pack([('sc', sc_own), ('conv_w', A['l0_conv_w'])])
    first_all = _exchange("gather_cond", [first], False)[0]
    sc_all = _unpack(first_all, lay0, 'sc')[:, 0]
    conv_w = _unpack(first_all, lay0, 'conv_w').transpose(1, 0, 2).reshape(3, N_DEV * cw)
    conv_w8 = _pad_rows(conv_w, 8)
    sc_ctx = jax.nn.silu(A['c_ctx'])
    s16 = _pad_rows(jnp.concatenate([sc_all, sc_ctx[None]], axis=0), 16)

    nmod = A['l0_w_mod'].shape[1]
    modp = jnp.concatenate([_mm_plain(f"l{l}_mod", s16, A[f'l{l}_w_mod'], False) for l in range(2)], axis=1)
    modp_all = _exchange("gather_mod", [modp], False)[0]
    mods = []
    for l in range(2):
        full = modp_all[:, :, l * nmod:(l + 1) * nmod].transpose(1, 0, 2).reshape(16, N_MOD * D) + A[f'l{l}_b_mod'][None]
        both = jnp.stack([full[8], lax.dynamic_index_in_dim(full, me, 0, keepdims=False)]).reshape(2, N_MOD, D)
        mods.append(jnp.concatenate([both, jnp.zeros((2, 8 - N_MOD, D), F32)], axis=1))

    token = stream.start(0, IN_WEIGHT, after=modp_all)
    mods[0] = mods[0] + token[0, 0]

    rope = _rope_tables(S, Lc)
    h = jnp.concatenate([ctx, x], axis=0)
    saved, W = [], []
    for l in range(2):
        h, s, Wl = _layer_fwd(l, h, params[l], stream, mods[l], rope, conv_w8, cfg)
        saved.append(s)
        W.append(Wl)

    dh, loss_blk, dgf = _loss_head("loss_head", h, A['final_norm_g'].reshape(1, -1), A['loss_target'][0], cfg)
    loss = lax.psum(loss_blk[0, 0], ("x", "y", "c"))

    grads, small, dmods, scatters = {}, {'final_norm_g': dgf[0]}, [None, None], [None, None]
    token = jnp.zeros((8, 128), F32)
    for l in (1, 0):
        dh, dmods[l], sm, scatters[l], token = _layer_bwd(l, dh, saved[l], params[l], W[l], mods[l] + token[0, 0], rope, conv_w8, cfg)
        small.update({f'l{l}_{k}': v for k, v in sm.items()})
    grad_x = dh[Lc:][None]

    def landed(l, key, after):
        group = FFN_WEIGHTS if key == 'ffn' else (key,)
        for n, parts in zip(group, _exchange_wait(f"scatter_{key}{l}_wait", scatters[l][key], after)):
            shape = A[f'l{l}_{n}'].shape
            grads[f'l{l}_{n}'] = (parts.reshape((N_DEV,) + (shape[::-1] if n in TRANSPOSED else shape)), None)

    small_names = [n for n in WEIGHT_NAMES if n in small]
    pieces = [(n, small[n]) for n in small_names]
    for l in range(2):
        pieces += [(f'dmod{l}', dmods[l][1, :N_MOD]), (f'dcmod{l}', dmods[l][0, :N_MOD])]
    second, lay1 = _pack(pieces)
    second_all = _exchange("gather_small", [second], False)[0]

    dsc_part = jnp.zeros((16, D), F32)
    for l in range(2):
        dm16 = _pad_rows(jnp.concatenate([_unpack(second_all, lay1, f'dmod{l}').reshape(N_DEV, N_MOD * D),
                                          jnp.sum(_unpack(second_all, lay1, f'dcmod{l}'), axis=0).reshape(1, N_MOD * D)], axis=0), 16)
        mine = lax.dynamic_slice_in_dim(dm16, me * nmod, nmod, axis=1)
        tk = _col_tile(D, 512)
        gw = _mm_tn(f"l{l}_dw_mod", s16, mine, pl.BlockSpec((16, tk), lambda kb, t: (0, kb)), pl.BlockSpec((16, nmod), lambda kb, t: (0, 0)),
                    jax.ShapeDtypeStruct((D, nmod), F32), pl.BlockSpec((tk, nmod), lambda kb, t: (kb, 0)), (tk, nmod), (D // tk, 1))
        grads[f'l{l}_w_mod'] = (gw[None], None)
        dsc_part += _mm_plain(f"l{l}_dsc", mine, A[f'l{l}_w_mod'], True)
        dmod_dev = _unpack(second_all, lay1, f'dmod{l}') + _unpack(second_all, lay1, f'dcmod{l}')
        grads[f'l{l}_b_mod'] = (dmod_dev.reshape(N_DEV, N_MOD * D), None)
    dsig = jax.nn.sigmoid(A['c_ctx'])
    dsilu = dsig * (1 + A['c_ctx'] * (1 - dsig))
    third_all = _exchange("gather_dsc", [dsc_part[8:9]], False)[0]
    grads['c_ctx'] = (third_all[:, 0] * dsilu[None], None)
    scatters[0]['w_in'], last_token = _exchange_start("scatter_w_in0_start", [scatters[0]['w_in']], True, after=third_all)
    for n in small_names:
        g8 = _unpack(second_all, lay1, n)
        if n == 'l0_conv_w':
            g8 = lax.dynamic_slice_in_dim(g8, me * cw, cw, axis=2)
        grads[n] = (g8, None)

    out = {}
    big_names = [n for n in WEIGHT_NAMES if n[3:] in BIG_WEIGHTS + ('w_mod',)]

    def adam(n, dep=None):
        w, m, v = A[n], A['m_' + n], A['v_' + n]
        if n[3:] in TRANSPOSED:
            out[n] = tuple(r.T for r in _adamw("adamw_" + n, grads[n][0], w.T, m.T, v.T, dep))
        else:
            out[n] = _adamw("adamw_" + n, grads[n][0], w, m, v, dep)
        return out[n][1]

    last = adam('l0_w_mod', last_token)
    adam('l1_w_mod', last_token)
    for l in (1, 0):
        for key in ('ffn', 'w_out', 'w_in'):
            landed(l, key, last)
            for n in (FFN_WEIGHTS if key == 'ffn' else (key,)):
                last = adam(f'l{l}_{n}')
    rest = [n for n in WEIGHT_NAMES if n not in big_names]
    wp, layw = _pack([(n, A[n]) for n in rest])
    mp, _ = _pack([(n, A['m_' + n]) for n in rest])
    vp, _ = _pack([(n, A['v_' + n]) for n in rest])
    gp, _ = _pack([(n, grads[n][0]) for n in rest], nlead=1)
    res = _adamw("adamw_small", gp, wp, mp, vp)
    for n in rest:
        out[n] = tuple(_unpack(r, layw, n) for r in res)

    outs = [loss, grad_x]
    for k in range(4):
        outs += [out[n][k] for n in WEIGHT_NAMES]
    return tuple(outs)
```

```python
import functools
import math

import numpy as np
import jax
import jax.numpy as jnp
from jax import lax
from jax.experimental import pallas as pl
from jax.experimental.pallas import tpu as pltpu

F32 = jnp.float32
BF16 = jnp.bfloat16
HEAD_DIM = 128
AUX_WIDTH = 512
AUX_GROUPS = 4
POOL_HALF = (1, 2, 4, 8)
WINDOW = 128
GRID_W = 64
ROPE_THETA = 10000.0
EPS = 1e-6
NEG_INF = -1e30
ATT_SCALE = HEAD_DIM ** -0.5
LOG2_E = math.log2(math.e)
Q_SCALE = ATT_SCALE * LOG2_E
N_MOD = 6
N_DEV = 8
ROW_TILE = 256
BAND = ROW_TILE + 2 * WINDOW
ADAM_LR, ADAM_B1, ADAM_B2, ADAM_EPS, ADAM_WD, ADAM_STEP = 0.001, 0.9, 0.999, 1e-08, 0.01, 10
VMEM_LIMIT_MB = 56
MESH = pl.DeviceIdType.MESH

WEIGHT_NAMES = ['c_ctx', 'l0_norm1_g', 'l0_w_mod', 'l0_b_mod', 'l0_w_in', 'l0_q_norm_g', 'l0_k_norm_g', 'l0_conv_w', 'l0_w_out', 'l0_norm2_g', 'l0_w_gate', 'l0_w_up', 'l0_w_down', 'l1_norm1_g', 'l1_w_mod', 'l1_b_mod', 'l1_w_in', 'l1_q_norm_g', 'l1_k_norm_g', 'l1_sink', 'l1_pool_w', 'l1_pool_scale', 'l1_w_out', 'l1_norm2_g', 'l1_w_gate', 'l1_w_up', 'l1_w_down', 'final_norm_g']
INPUT_NAMES = (['x', 'c', 'ctx'] + WEIGHT_NAMES + ['loss_target'] + ['m_' + n for n in WEIGHT_NAMES]
               + ['v_' + n for n in WEIGHT_NAMES])
IN_WEIGHT = ('w_in',)
OUT_WEIGHT = ('w_out',)
MIXER_WEIGHTS = OUT_WEIGHT + IN_WEIGHT
FFN_WEIGHTS = ('w_down', 'w_gate', 'w_up')
TRANSPOSED = ('w_gate', 'w_up')
BIG_WEIGHTS = MIXER_WEIGHTS + FFN_WEIGHTS


def _params(vmem_mb=VMEM_LIMIT_MB):
    return pltpu.CompilerParams(vmem_limit_bytes=vmem_mb << 20)


def _row_tile(n, cap):
    best = None
    for t in range(16, min(n, cap) + 1, 16):
        if n % t == 0:
            best = t
    assert best is not None, (n, cap)
    return best


def _col_tile(n, cap):
    best = n
    for t in range(128, min(n, cap) + 1, 128):
        if n % t == 0:
            best = t
    return best if best <= cap or n % 128 else n


def _dot(a, b, ca, cb):
    return lax.dot_general(a, b, (((ca,), (cb,)), ((), ())), preferred_element_type=F32)


def _nn(a, b):
    return _dot(a, b, 1, 0)


def _nt(a, b):
    return _dot(a, b, 1, 1)


def _tn(a, b):
    return _dot(a, b, 0, 0)


def _bf(x):
    return x.astype(BF16)


def _exchange(name, arrs, scatter):
    n = len(arrs)
    if scatter:
        out_shape = [jax.ShapeDtypeStruct(a.shape, a.dtype) for a in arrs]
    else:
        out_shape = [jax.ShapeDtypeStruct((N_DEV,) + a.shape, a.dtype) for a in arrs]

    def body(*refs):
        ins, outs = refs[:n], refs[n:2 * n]
        send_sems, recv_sems, local_sems = refs[2 * n:]
        x, y, c = lax.axis_index("x"), lax.axis_index("y"), lax.axis_index("c")
        me = 4 * x + 2 * y + c
        local, remote = [], []
        for a in range(n):
            own = ins[a].at[me] if scatter else ins[a]
            cp = pltpu.make_async_copy(own, outs[a].at[me], local_sems.at[a])
            cp.start()
            local.append(cp)
            for r in range(1, N_DEV):
                px = 1 - x if r & 4 else x
                py = 1 - y if r & 2 else y
                pc = 1 - c if r & 1 else c
                src = ins[a].at[4 * px + 2 * py + pc] if scatter else ins[a]
                cp = pltpu.make_async_remote_copy(
                    src_ref=src, dst_ref=outs[a].at[me], send_sem=send_sems.at[a, r - 1],
                    recv_sem=recv_sems.at[a, r - 1], device_id=(px, py, pc), device_id_type=MESH)
                cp.start()
                remote.append(cp)
        for cp in remote:
            cp.wait()
        for cp in local:
            cp.wait()

    any_spec = pl.BlockSpec(memory_space=pl.ANY)
    return pl.pallas_call(
        body, name=name, out_shape=out_shape,
        in_specs=[any_spec] * n, out_specs=[any_spec] * n,
        scratch_shapes=[pltpu.SemaphoreType.DMA((n, N_DEV - 1)), pltpu.SemaphoreType.DMA((n, N_DEV - 1)),
                        pltpu.SemaphoreType.DMA((n,))],
    )(*arrs)


HBM_SPEC = pl.BlockSpec(memory_space=pltpu.HBM)
SEM_SPEC = pl.BlockSpec(memory_space=pltpu.SEMAPHORE)
EFFECT = pltpu.SideEffectType.DATAFLOW_SIDE_EFFECTING


def _split_copies(srcs, lands, send_sems, recv_sems, local_sems, scatter):
    x, y, c = lax.axis_index("x"), lax.axis_index("y"), lax.axis_index("c")
    me = 4 * x + 2 * y + c
    local, remote = [], []
    for a in range(len(srcs)):
        own = srcs[a].at[me] if scatter else srcs[a]
        local.append(pltpu.make_async_copy(own, lands[a].at[me], local_sems.at[a]))
        for r in range(1, N_DEV):
            px = 1 - x if r & 4 else x
            py = 1 - y if r & 2 else y
            pc = 1 - c if r & 1 else c
            src = srcs[a].at[4 * px + 2 * py + pc] if scatter else srcs[a]
            remote.append(pltpu.make_async_remote_copy(
                src_ref=src, dst_ref=lands[a].at[me], send_sem=send_sems.at[a * (N_DEV - 1) + r - 1],
                recv_sem=recv_sems.at[a * (N_DEV - 1) + r - 1], device_id=(px, py, pc), device_id_type=MESH))
    return local, remote


def _exchange_start(name, arrs, scatter, after=None):
    n = len(arrs)
    extra = [] if after is None else [after]
    shapes = [a.shape if scatter else (N_DEV,) + a.shape for a in arrs]
    lands = [pltpu.with_memory_space_constraint(lax.empty(s, a.dtype), pltpu.HBM) for s, a in zip(shapes, arrs)]
    srcs = [pltpu.with_memory_space_constraint(a, pltpu.HBM) for a in arrs]

    def body(*refs):
        src_refs, land_refs = refs[:n], refs[n:2 * n]
        send_sems, recv_sems, local_sems = refs[2 * n + len(extra):2 * n + len(extra) + 3]
        token = refs[-1]
        local, remote = _split_copies(src_refs, land_refs, send_sems, recv_sems, local_sems, scatter)
        for cp in local + remote:
            cp.start()
        token[...] = jnp.zeros_like(token)

    res = pl.pallas_call(
        body, name=name,
        out_shape=[pltpu.SemaphoreType.DMA((n * (N_DEV - 1),)), pltpu.SemaphoreType.DMA((n * (N_DEV - 1),)), pltpu.SemaphoreType.DMA((n,))]
        + [pltpu.HBM(a.shape, a.dtype) for a in arrs] + [pltpu.HBM(s, a.dtype) for s, a in zip(shapes, arrs)]
        + [jax.ShapeDtypeStruct((8, 128), F32)],
        in_specs=[HBM_SPEC] * (2 * n) + [pl.BlockSpec(memory_space=pl.ANY)] * len(extra),
        out_specs=[SEM_SPEC] * 3 + [HBM_SPEC] * (2 * n) + [pl.BlockSpec(memory_space=pltpu.VMEM)],
        input_output_aliases={i: 3 + i for i in range(2 * n)},
        compiler_params=pltpu.CompilerParams(has_side_effects=EFFECT),
    )(*srcs, *lands, *extra)
    return (scatter, res[:3], res[3:3 + n], res[3 + n:3 + 2 * n]), res[-1]


def _exchange_wait(name, handle, after):
    scatter, sems, srcs, lands = handle
    n = len(srcs)

    def body(*refs):
        src_refs, land_refs = refs[:n], refs[n:2 * n]
        send_sems, recv_sems, local_sems = refs[2 * n:2 * n + 3]
        local, remote = _split_copies(src_refs, land_refs, send_sems, recv_sems, local_sems, scatter)
        for cp in remote:
            cp.wait_send()
            cp.wait_recv()
        for cp in local:
            cp.wait()

    res = pl.pallas_call(
        body, name=name,
        out_shape=[pltpu.HBM(a.shape, a.dtype) for a in srcs] + [pltpu.HBM(a.shape, a.dtype) for a in lands],
        in_specs=[HBM_SPEC] * (2 * n) + [SEM_SPEC] * 3 + [pl.BlockSpec(memory_space=pl.ANY)], out_specs=[HBM_SPEC] * (2 * n),
        input_output_aliases={i: i for i in range(2 * n)},
        compiler_params=pltpu.CompilerParams(has_side_effects=EFFECT),
    )(*srcs, *lands, *sems, after)
    return list(res[n:])


FIRST_COPIES = 4
RELAY_COPIES = 3


def _gather_copies(srcs, lands, sems):
    send_sems, recv_sems, local_sems = sems[:3]
    x, y, c = lax.axis_index("x"), lax.axis_index("y"), lax.axis_index("c")
    me = 4 * x + 2 * y + c
    chips = [(1 - x, y), (x, 1 - y), (1 - x, 1 - y)]
    local, first, relay = [], [], []
    for a in range(len(srcs)):
        local.append(pltpu.make_async_copy(srcs[a], lands[a].at[me], local_sems.at[a]))
        targets = [(x, y, 1 - c)] + [(px, py, c) for px, py in chips]
        first.append([pltpu.make_async_remote_copy(
            src_ref=srcs[a], dst_ref=lands[a].at[me], send_sem=send_sems.at[FIRST_COPIES * a + k],
            recv_sem=recv_sems.at[FIRST_COPIES * a + k], device_id=t, device_id_type=MESH) for k, t in enumerate(targets)])
        if len(sems) > 3:
            rsend, rrecv = sems[3:]
            slots = [lands[a].at[4 * px + 2 * py + c] for px, py in chips]
            relay.append([pltpu.make_async_remote_copy(
                src_ref=slot, dst_ref=slot, send_sem=rsend.at[RELAY_COPIES * a + j], recv_sem=rrecv.at[RELAY_COPIES * a + j],
                device_id=(x, y, 1 - c), device_id_type=MESH) for j, slot in enumerate(slots)])
    return local, first, relay


def _gather_start(name, arrs, after=None):
    n = len(arrs)
    extra = [] if after is None else [after]
    lands = [pltpu.with_memory_space_constraint(lax.empty((N_DEV,) + a.shape, a.dtype), pltpu.HBM) for a in arrs]
    srcs = [pltpu.with_memory_space_constraint(a, pltpu.HBM) for a in arrs]

    def body(*refs):
        at = 2 * n + len(extra)
        local, first, _ = _gather_copies(refs[:n], refs[n:2 * n], refs[at:at + 3])
        for cp in local + [cp for cps in first for cp in cps]:
            cp.start()
        refs[-1][...] = jnp.zeros_like(refs[-1])

    res = pl.pallas_call(
        body, name=name,
        out_shape=[pltpu.SemaphoreType.DMA((FIRST_COPIES * n,)), pltpu.SemaphoreType.DMA((FIRST_COPIES * n,)), pltpu.SemaphoreType.DMA((n,))]
        + [pltpu.HBM(a.shape, a.dtype) for a in arrs] + [pltpu.HBM((N_DEV,) + a.shape, a.dtype) for a in arrs]
        + [jax.ShapeDtypeStruct((8, 128), F32)],
        in_specs=[HBM_SPEC] * (2 * n) + [pl.BlockSpec(memory_space=pl.ANY)] * len(extra),
        out_specs=[SEM_SPEC] * 3 + [HBM_SPEC] * (2 * n) + [pl.BlockSpec(memory_space=pltpu.VMEM)],
        input_output_aliases={i: 3 + i for i in range(2 * n)},
        compiler_params=pltpu.CompilerParams(has_side_effects=EFFECT),
    )(*srcs, *lands, *extra)
    return (list(res[:3]), list(res[3:3 + n]), list(res[3 + n:3 + 2 * n])), res[-1]


def _gather_relay(name, handle, after):
    sems, srcs, lands = handle
    n = len(srcs)

    def body(*refs):
        in_sems = refs[2 * n:2 * n + 3]
        out_sems = refs[2 * n + 4 + 2 * n:2 * n + 4 + 2 * n + 2]
        _, first, relay = _gather_copies(refs[:n], refs[n:2 * n], list(in_sems) + list(out_sems))
        for a in range(n):
            for j in range(RELAY_COPIES):
                first[a][1 + j].wait_recv()
                relay[a][j].start()
        refs[-1][...] = jnp.zeros_like(refs[-1])

    res = pl.pallas_call(
        body, name=name,
        out_shape=[pltpu.HBM(a.shape, a.dtype) for a in srcs] + [pltpu.HBM(a.shape, a.dtype) for a in lands]
        + [pltpu.SemaphoreType.DMA((RELAY_COPIES * n,)), pltpu.SemaphoreType.DMA((RELAY_COPIES * n,)), jax.ShapeDtypeStruct((8, 128), F32)],
        in_specs=[HBM_SPEC] * (2 * n) + [SEM_SPEC] * 3 + [pl.BlockSpec(memory_space=pl.ANY)],
        out_specs=[HBM_SPEC] * (2 * n) + [SEM_SPEC] * 2 + [pl.BlockSpec(memory_space=pltpu.VMEM)],
        input_output_aliases={i: i for i in range(2 * n)},
        compiler_params=pltpu.CompilerParams(has_side_effects=EFFECT),
    )(*srcs, *lands, *sems, after)
    return (sems + list(res[2 * n:2 * n + 2]), list(res[:n]), list(res[n:2 * n])), res[-1]


def _gather_wait(name, handle, after):
    sems, srcs, lands = handle
    n = len(srcs)

    def body(*refs):
        local, first, relay = _gather_copies(refs[:n], refs[n:2 * n], refs[2 * n:2 * n + 5])
        for a in range(n):
            for cp in first[a]:
                cp.wait_send()
            first[a][0].wait_recv()
            for cp in relay[a]:
                cp.wait_send()
                cp.wait_recv()
            local[a].wait()

    res = pl.pallas_call(
        body, name=name,
        out_shape=[pltpu.HBM(a.shape, a.dtype) for a in srcs] + [pltpu.HBM(a.shape, a.dtype) for a in lands],
        in_specs=[HBM_SPEC] * (2 * n) + [SEM_SPEC] * 5 + [pl.BlockSpec(memory_space=pl.ANY)], out_specs=[HBM_SPEC] * (2 * n),
        input_output_aliases={i: i for i in range(2 * n)},
        compiler_params=pltpu.CompilerParams(has_side_effects=EFFECT),
    )(*srcs, *lands, *sems, after)
    return list(res[n:])


def _dep(dep, grid_rank):
    if dep is None:
        return [], []
    return [dep], [pl.BlockSpec((8, 128), (lambda i, j: (0, 0)) if grid_rank == 2 else (lambda i: (0, 0)))]


def _mm_step(name, fn, ins, in_specs, out_shape, out_spec, grid, dep=None):
    n = len(ins)
    dep_ins, dep_specs = _dep(dep, len(grid))

    def body(*refs):
        o_ref = refs[n + len(dep_ins)]
        o_ref[...] = fn(*refs[:n]).astype(o_ref.dtype)

    return pl.pallas_call(body, name=name, grid=grid, in_specs=list(in_specs) + dep_specs, out_specs=out_spec,
                          out_shape=out_shape, compiler_params=_params())(*ins, *dep_ins)


def _mm_tn(name, a, b, a_spec, b_spec, out_shape, out_spec, acc_shape, grid):
    nk = grid[-1]
    kax = len(grid) - 1

    def body(a_ref, b_ref, o_ref, acc_ref):
        k = pl.program_id(kax)

        @pl.when(k == 0)
        def _():
            acc_ref[...] = jnp.zeros_like(acc_ref)

        acc_ref[...] += _tn(_bf(a_ref[...]), _bf(b_ref[...]))

        @pl.when(k == nk - 1)
        def _():
            o_ref[...] = acc_ref[...].astype(o_ref.dtype)

    return pl.pallas_call(body, name=name, grid=grid, in_specs=[a_spec, b_spec], out_specs=out_spec,
                          out_shape=out_shape, scratch_shapes=[pltpu.VMEM(acc_shape, F32)],
                          compiler_params=_params())(a, b)


def _mm_cols(name, a, w3, out_dtype=F32, dep=None):
    M, K = a.shape
    J, _, n = w3.shape
    tm = _row_tile(M, 1088)
    return _mm_step(
        name, lambda a_ref, w_ref: _nn(_bf(a_ref[...]), w_ref[...]), [a, w3],
        [pl.BlockSpec((tm, K), lambda j, i: (i, 0)), pl.BlockSpec((None, K, n), lambda j, i: (j, 0, 0))],
        jax.ShapeDtypeStruct((M, J * n), out_dtype), pl.BlockSpec((tm, n), lambda j, i: (i, j)), (J, M // tm), dep)


def _mm_plain(name, a, b, transpose_b, out_dtype=F32, tn=512, dep=None):
    M, K = a.shape
    N = b.shape[0] if transpose_b else b.shape[1]
    tm = _row_tile(M, 1088)
    tn = _col_tile(N, tn)
    if transpose_b:
        b_spec = pl.BlockSpec((tn, K), lambda j, i: (j, 0))
        fn = lambda a_ref, b_ref: _nt(_bf(a_ref[...]), _bf(b_ref[...]))
    else:
        b_spec = pl.BlockSpec((K, tn), lambda j, i: (0, j))
        fn = lambda a_ref, b_ref: _nn(_bf(a_ref[...]), _bf(b_ref[...]))
    return _mm_step(name, fn, [a, b], [pl.BlockSpec((tm, K), lambda j, i: (i, 0)), b_spec],
                    jax.ShapeDtypeStruct((M, N), out_dtype), pl.BlockSpec((tm, tn), lambda j, i: (i, j)),
                    (N // tn, M // tm), dep)


def _mm_shards_nn(name, a3, w3, tn=512, dep=None):
    J, M, k = a3.shape
    N = w3.shape[2]
    tm = _row_tile(M, 544)
    tn = _col_tile(N, tn)

    def fn(a_ref, w_ref):
        acc = _nn(a_ref[0], w_ref[0])
        for j in range(1, J):
            acc += _nn(a_ref[j], w_ref[j])
        return acc

    return _mm_step(name, fn, [a3, w3],
                    [pl.BlockSpec((J, tm, k), lambda jn, i: (0, i, 0)), pl.BlockSpec((J, k, tn), lambda jn, i: (0, 0, jn))],
                    jax.ShapeDtypeStruct((M, N), F32), pl.BlockSpec((tm, tn), lambda jn, i: (i, jn)), (N // tn, M // tm), dep)


def _mm_shards_nn2(name, a3, w3a, b3, w3b, tn=512):
    J, M, k = a3.shape
    N = w3a.shape[2]
    tm = _row_tile(M, 544)
    tn = _col_tile(N, tn)

    def fn(a_ref, wa_ref, b_ref, wb_ref):
        acc = _nn(a_ref[0], wa_ref[0]) + _nn(b_ref[0], wb_ref[0])
        for j in range(1, J):
            acc += _nn(a_ref[j], wa_ref[j]) + _nn(b_ref[j], wb_ref[j])
        return acc

    act = pl.BlockSpec((J, tm, k), lambda jn, i: (0, i, 0))
    wsp = pl.BlockSpec((J, k, tn), lambda jn, i: (0, 0, jn))
    return _mm_step(name, fn, [a3, w3a, b3, w3b], [act, wsp, act, wsp],
                    jax.ShapeDtypeStruct((M, N), F32), pl.BlockSpec((tm, tn), lambda jn, i: (i, jn)), (N // tn, M // tm))


def _mm_cols_nt(name, a, w3, tn=512):
    M = a.shape[0]
    J, N, n = w3.shape
    tm = _row_tile(M, 544)
    tn = _col_tile(N, tn)

    def fn(a_ref, w_ref):
        acc = _nt(a_ref[:, 0:n], w_ref[0])
        for j in range(1, J):
            acc += _nt(a_ref[:, j * n:(j + 1) * n], w_ref[j])
        return acc

    return _mm_step(name, fn, [a, w3],
                    [pl.BlockSpec((tm, J * n), lambda jn, i: (i, 0)), pl.BlockSpec((J, tn, n), lambda jn, i: (0, jn, 0))],
                    jax.ShapeDtypeStruct((M, N), F32), pl.BlockSpec((tm, tn), lambda jn, i: (i, jn)), (N // tn, M // tm))


def _wgrad_cols(name, a, b, J):
    T, K = a.shape
    n = b.shape[1] // J
    tt = _row_tile(T, 1088)
    return _mm_tn(name, a, b, pl.BlockSpec((tt, K), lambda j, t: (t, 0)), pl.BlockSpec((tt, n), lambda j, t: (t, j)),
                  jax.ShapeDtypeStruct((J, K, n), BF16), pl.BlockSpec((None, K, n), lambda j, t: (j, 0, 0)), (K, n), (J, T // tt))


def _wgrad_rows(name, a, b, tk=512):
    T, K = a.shape
    N = b.shape[1]
    tt = _row_tile(T, 1088)
    tk = _col_tile(K, tk)
    return _mm_tn(name, a, b, pl.BlockSpec((tt, tk), lambda kb, t: (t, kb)), pl.BlockSpec((tt, N), lambda kb, t: (t, 0)),
                  jax.ShapeDtypeStruct((K, N), BF16), pl.BlockSpec((tk, N), lambda kb, t: (kb, 0)), (tk, N), (K // tk, T // tt))


def _wgrad_up(name, a, b3):
    T, K = a.shape
    J, _, k = b3.shape
    tt = _row_tile(T, 1088)
    return _mm_tn(name, a, b3, pl.BlockSpec((tt, K), lambda j, t: (t, 0)), pl.BlockSpec((None, tt, k), lambda j, t: (j, t, 0)),
                  jax.ShapeDtypeStruct((J, K, k), BF16), pl.BlockSpec((None, K, k), lambda j, t: (j, 0, 0)), (K, k), (J, T // tt))


def _wgrad_down(name, a3, b):
    J, T, k = a3.shape
    N = b.shape[1]
    tt = _row_tile(T, 1088)
    return _mm_tn(name, a3, b, pl.BlockSpec((None, tt, k), lambda j, t: (j, t, 0)), pl.BlockSpec((tt, N), lambda j, t: (t, 0)),
                  jax.ShapeDtypeStruct((J, k, N), BF16), pl.BlockSpec((None, k, N), lambda j, t: (j, 0, 0)), (k, N), (J, T // tt))


def _seg(i):
    return jnp.minimum(i, 1)


def _rstd(x):
    return lax.rsqrt(jnp.mean(x * x, axis=-1, keepdims=True) + EPS)


def _norm_mod(name, h, g, mod, which):
    T, D = h.shape

    def body(h_ref, g_ref, mod_ref, o_ref):
        x = h_ref[...]
        n = x * _rstd(x) * g_ref[...]
        shift = mod_ref[3 * which:3 * which + 1, :]
        scale = mod_ref[3 * which + 1:3 * which + 2, :]
        o_ref[...] = (n * (1 + scale) + shift).astype(o_ref.dtype)

    row = pl.BlockSpec((ROW_TILE, D), lambda i: (i, 0))
    return pl.pallas_call(
        body, name=name, grid=(T // ROW_TILE,),
        in_specs=[row, pl.BlockSpec((1, D), lambda i: (0, 0)), pl.BlockSpec((None, 8, D), lambda i: (_seg(i), 0, 0))],
        out_specs=row, out_shape=jax.ShapeDtypeStruct((T, D), BF16), compiler_params=_params())(h, g, mod)


def _norm_mod_bwd(name, dxn, h, g, mod, which, dres):
    T, D = h.shape

    def body(dxn_ref, h_ref, g_ref, mod_ref, dres_ref, dh_ref, dmod_ref, dg_ref):
        i = pl.program_id(0)
        x = h_ref[...]
        r = _rstd(x)
        xhat = x * r
        g = g_ref[...]
        n = xhat * g
        scale = mod_ref[3 * which + 1:3 * which + 2, :]
        dxn = dxn_ref[...]
        dn = dxn * (1 + scale)
        dxh = dn * g
        dh_ref[...] = dres_ref[...] + r * (dxh - xhat * jnp.mean(dxh * xhat, axis=-1, keepdims=True))

        @pl.when(i <= 1)
        def _():
            dmod_ref[...] = jnp.zeros_like(dmod_ref)

        @pl.when(i == 0)
        def _():
            dg_ref[...] = jnp.zeros_like(dg_ref)

        dmod_ref[3 * which:3 * which + 1, :] += jnp.sum(dxn, axis=0, keepdims=True)
        dmod_ref[3 * which + 1:3 * which + 2, :] += jnp.sum(dxn * n, axis=0, keepdims=True)
        dg_ref[0:1, :] += jnp.sum(dn * xhat, axis=0, keepdims=True)

    row = pl.BlockSpec((ROW_TILE, D), lambda i: (i, 0))
    modspec = pl.BlockSpec((None, 8, D), lambda i: (_seg(i), 0, 0))
    return pl.pallas_call(
        body, name=name, grid=(T // ROW_TILE,),
        in_specs=[row, row, pl.BlockSpec((1, D), lambda i: (0, 0)), modspec, row],
        out_specs=[row, modspec, pl.BlockSpec((8, D), lambda i: (0, 0))],
        out_shape=[jax.ShapeDtypeStruct((T, D), F32), jax.ShapeDtypeStruct((2, 8, D), F32), jax.ShapeDtypeStruct((8, D), F32)],
        compiler_params=_params())(dxn, h, g, mod, dres)


def _gate_res(name, h, y, mod, row_idx):
    T, D = h.shape

    def body(h_ref, y_ref, mod_ref, o_ref):
        o_ref[...] = h_ref[...] + mod_ref[row_idx:row_idx + 1, :] * y_ref[...]

    row = pl.BlockSpec((ROW_TILE, D), lambda i: (i, 0))
    return pl.pallas_call(
        body, name=name, grid=(T // ROW_TILE,),
        in_specs=[row, row, pl.BlockSpec((None, 8, D), lambda i: (_seg(i), 0, 0))],
        out_specs=row, out_shape=jax.ShapeDtypeStruct((T, D), F32), compiler_params=_params())(h, y, mod)


def _gate_bwd(name, dh, y, mod, row_idx):
    T, D = dh.shape

    def body(dh_ref, y_ref, mod_ref, dy_ref, dmod_ref):
        i = pl.program_id(0)
        dh = dh_ref[...]
        dy_ref[...] = (dh * mod_ref[row_idx:row_idx + 1, :]).astype(dy_ref.dtype)

        @pl.when(i <= 1)
        def _():
            dmod_ref[...] = jnp.zeros_like(dmod_ref)

        dmod_ref[row_idx:row_idx + 1, :] += jnp.sum(dh * y_ref[...], axis=0, keepdims=True)

    row = pl.BlockSpec((ROW_TILE, D), lambda i: (i, 0))
    modspec = pl.BlockSpec((None, 8, D), lambda i: (_seg(i), 0, 0))
    return pl.pallas_call(
        body, name=name, grid=(T // ROW_TILE,), in_specs=[row, row, modspec], out_specs=[row, modspec],
        out_shape=[jax.ShapeDtypeStruct((T, D), BF16), jax.ShapeDtypeStruct((2, 8, D), F32)],
        compiler_params=_params())(dh, y, mod)


def _rot(y):
    lane = lax.broadcasted_iota(jnp.int32, y.shape, 1)
    return jnp.where((lane & 32) == 0, pltpu.roll(y, 96, 1), pltpu.roll(y, 32, 1))


def _qk_prep(name, P, q_g, k_g, rope_c, rope_s, cfg):
    T = P.shape[0]
    ATT, KVW = cfg['ATT'], cfg['KVW']

    def body(q_ref, k_ref, v_ref, qg_ref, kg_ref, c_ref, s_ref, qo_ref, ko_ref, vo_ref):
        cc, ss = c_ref[...], s_ref[...]

        def head(x, g):
            y = x * _rstd(x) * g
            return y * cc + _rot(y) * ss

        for hh in range(ATT // HEAD_DIM):
            sl = slice(hh * HEAD_DIM, (hh + 1) * HEAD_DIM)
            qo_ref[:, sl] = (head(q_ref[:, sl], qg_ref[...]) * Q_SCALE).astype(qo_ref.dtype)
        for hh in range(KVW // HEAD_DIM):
            sl = slice(hh * HEAD_DIM, (hh + 1) * HEAD_DIM)
            ko_ref[:, sl] = head(k_ref[:, sl], kg_ref[...]).astype(ko_ref.dtype)
        vo_ref[...] = v_ref[...].astype(vo_ref.dtype)

    kb = ATT // KVW
    gain = pl.BlockSpec((1, HEAD_DIM), lambda i: (0, 0))
    tab = pl.BlockSpec((ROW_TILE, HEAD_DIM), lambda i: (i, 0))
    qs = pl.BlockSpec((ROW_TILE, ATT), lambda i: (i, 0))
    ks = pl.BlockSpec((ROW_TILE, KVW), lambda i: (i, 0))
    return pl.pallas_call(
        body, name=name, grid=(T // ROW_TILE,),
        in_specs=[qs, pl.BlockSpec((ROW_TILE, KVW), lambda i: (i, kb)), pl.BlockSpec((ROW_TILE, KVW), lambda i: (i, kb + 1)),
                  gain, gain, tab, tab],
        out_specs=[qs, ks, ks],
        out_shape=[jax.ShapeDtypeStruct((T, ATT), BF16), jax.ShapeDtypeStruct((T, KVW), BF16), jax.ShapeDtypeStruct((T, KVW), BF16)],
        compiler_params=_params())(P, P, P, q_g, k_g, rope_c, rope_s)


def _qk_prep_bwd(name, dqr, dkr, P, q_g, k_g, rope_c, rope_s, cfg):
    T = P.shape[0]
    ATT, KVW = cfg['ATT'], cfg['KVW']

    def body(dq_ref, dk_ref, q_ref, k_ref, qg_ref, kg_ref, c_ref, s_ref, dqo_ref, dko_ref, dqg_ref, dkg_ref):
        i = pl.program_id(0)
        cc, ss = c_ref[...], s_ref[...]

        @pl.when(i == 0)
        def _():
            dqg_ref[...] = jnp.zeros_like(dqg_ref)
            dkg_ref[...] = jnp.zeros_like(dkg_ref)

        def head(x, g, dout):
            dy = dout * cc + _rot(dout * ss)
            r = _rstd(x)
            xhat = x * r
            dxh = dy * g
            dx = r * (dxh - xhat * jnp.mean(dxh * xhat, axis=-1, keepdims=True))
            return dx, jnp.sum(dy * xhat, axis=0, keepdims=True)

        dg = jnp.zeros((1, HEAD_DIM), F32)
        for hh in range(ATT // HEAD_DIM):
            sl = slice(hh * HEAD_DIM, (hh + 1) * HEAD_DIM)
            dx, d = head(q_ref[:, sl], qg_ref[...], dq_ref[:, sl] * ATT_SCALE)
            dqo_ref[:, sl] = dx.astype(dqo_ref.dtype)
            dg += d
        dqg_ref[0:1, :] += dg
        dg = jnp.zeros((1, HEAD_DIM), F32)
        for hh in range(KVW // HEAD_DIM):
            sl = slice(hh * HEAD_DIM, (hh + 1) * HEAD_DIM)
            dx, d = head(k_ref[:, sl], kg_ref[...], dk_ref[:, sl] * (1.0 / LOG2_E))
            dko_ref[:, sl] = dx.astype(dko_ref.dtype)
            dg += d
        dkg_ref[0:1, :] += dg

    kb = ATT // KVW
    gain = pl.BlockSpec((1, HEAD_DIM), lambda i: (0, 0))
    dgain = pl.BlockSpec((8, HEAD_DIM), lambda i: (0, 0))
    tab = pl.BlockSpec((ROW_TILE, HEAD_DIM), lambda i: (i, 0))
    qs = pl.BlockSpec((ROW_TILE, ATT), lambda i: (i, 0))
    ks = pl.BlockSpec((ROW_TILE, KVW), lambda i: (i, 0))
    return pl.pallas_call(
        body, name=name, grid=(T // ROW_TILE,),
        in_specs=[qs, ks, qs, pl.BlockSpec((ROW_TILE, KVW), lambda i: (i, kb)), gain, gain, tab, tab],
        out_specs=[qs, ks, dgain, dgain],
        out_shape=[jax.ShapeDtypeStruct((T, ATT), BF16), jax.ShapeDtypeStruct((T, KVW), BF16),
                   jax.ShapeDtypeStruct((8, HEAD_DIM), F32), jax.ShapeDtypeStruct((8, HEAD_DIM), F32)],
        compiler_params=_params())(dqr, dkr, P, P, q_g, k_g, rope_c, rope_s)


def _att_specs(T, G):
    qs = pl.BlockSpec((ROW_TILE, G * HEAD_DIM), lambda h, i: (i, h))
    kvs = pl.BlockSpec((T, HEAD_DIM), lambda h, i: (0, h))
    return qs, kvs


def _attn_dense_fwd(name, q, k, v, cfg):
    T, G, Lc = q.shape[0], cfg['G'], cfg['Lc']

    def body(q_ref, k_ref, v_ref, o_ref, lse_ref):
        def attend(rows):
            kk, vv = k_ref[0:rows, :], v_ref[0:rows, :]
            for g in range(G):
                sl = slice(g * HEAD_DIM, (g + 1) * HEAD_DIM)
                s = _nt(q_ref[:, sl], kk)
                m = jnp.max(s, axis=1, keepdims=True)
                p = jnp.exp2(s - m)
                l = jnp.sum(p, axis=1, keepdims=True)
                o_ref[:, sl] = _nn(_bf(p), vv) / l
                lse_ref[:, sl] = jnp.broadcast_to(m + jnp.log2(l), (ROW_TILE, HEAD_DIM))

        @pl.when(pl.program_id(1) == 0)
        def _():
            attend(Lc)

        @pl.when(pl.program_id(1) > 0)
        def _():
            attend(T)

    qs, kvs = _att_specs(T, G)
    return pl.pallas_call(
        body, name=name, grid=(cfg['NKV'], T // ROW_TILE), in_specs=[qs, kvs, kvs], out_specs=[qs, qs],
        out_shape=[jax.ShapeDtypeStruct(q.shape, F32), jax.ShapeDtypeStruct(q.shape, F32)],
        compiler_params=_params())(q, k, v)


def _attn_dense_bwd(name, q, k, v, o, lse, dmix, cfg):
    T, G, Lc = q.shape[0], cfg['G'], cfg['Lc']

    def body(q_ref, k_ref, v_ref, o_ref, lse_ref, do_ref, dq_ref, dk_ref, dv_ref):
        i = pl.program_id(1)

        @pl.when(i == 0)
        def _():
            dk_ref[...] = jnp.zeros_like(dk_ref)
            dv_ref[...] = jnp.zeros_like(dv_ref)

        def attend(rows):
            kk, vv = k_ref[0:rows, :], v_ref[0:rows, :]
            for g in range(G):
                sl = slice(g * HEAD_DIM, (g + 1) * HEAD_DIM)
                qg, do = q_ref[:, sl], do_ref[:, sl]
                delta = jnp.sum(do * o_ref[:, sl], axis=1, keepdims=True)
                p = jnp.exp2(_nt(qg, kk) - lse_ref[:, g * HEAD_DIM:g * HEAD_DIM + 1])
                dob = _bf(do)
                dv_ref[0:rows, :] += _tn(_bf(p), dob)
                ds = _bf(p * (_nt(dob, vv) - delta))
                dq_ref[:, sl] = _nn(ds, kk)
                dk_ref[0:rows, :] += _tn(ds, qg)

        @pl.when(i == 0)
        def _():
            attend(Lc)

        @pl.when(i > 0)
        def _():
            attend(T)

    qs, kvs = _att_specs(T, G)
    return pl.pallas_call(
        body, name=name, grid=(cfg['NKV'], T // ROW_TILE), in_specs=[qs, kvs, kvs, qs, qs, qs], out_specs=[qs, kvs, kvs],
        out_shape=[jax.ShapeDtypeStruct(q.shape, F32), jax.ShapeDtypeStruct(k.shape, F32), jax.ShapeDtypeStruct(k.shape, F32)],
        compiler_params=_params())(q, k, v, o, lse, dmix)


def _band(i, T, Lc):
    start = pl.multiple_of(jnp.clip(WINDOW + (i - 1) * ROW_TILE, 0, T - BAND), WINDOW)
    qpos = (i - 1) * ROW_TILE + lax.broadcasted_iota(jnp.int32, (ROW_TILE, 1), 0)
    kpos = start - Lc + lax.broadcasted_iota(jnp.int32, (1, BAND), 1)
    ok = (jnp.abs(kpos - qpos) <= WINDOW) & (kpos >= 0) & (i > 0)
    return start, jnp.where(ok, 0.0, NEG_INF).astype(F32)


def _attn_win_fwd(name, q, k, v, sink, cfg):
    T, G, Lc = q.shape[0], cfg['G'], cfg['Lc']

    def body(sink_ref, q_ref, k_ref, v_ref, o_ref, lse_ref):
        h, i = pl.program_id(0), pl.program_id(1)
        start, bias = _band(i, T, Lc)
        kc, vc = k_ref[0:Lc, :], v_ref[0:Lc, :]
        kb, vb = k_ref[pl.ds(start, BAND), :], v_ref[pl.ds(start, BAND), :]
        for g in range(G):
            sl = slice(g * HEAD_DIM, (g + 1) * HEAD_DIM)
            qg = q_ref[:, sl]
            sk = sink_ref[h * G + g] * LOG2_E
            sc = _nt(qg, kc)
            sb = _nt(qg, kb) + bias
            m = jnp.maximum(jnp.maximum(jnp.max(sc, axis=1, keepdims=True), jnp.max(sb, axis=1, keepdims=True)), sk)
            pc, pb = jnp.exp2(sc - m), jnp.exp2(sb - m)
            l = jnp.sum(pc, axis=1, keepdims=True) + jnp.sum(pb, axis=1, keepdims=True) + jnp.exp2(sk - m)
            o_ref[:, sl] = (_nn(_bf(pc), vc) + _nn(_bf(pb), vb)) / l
            lse_ref[:, sl] = jnp.broadcast_to(m + jnp.log2(l), (ROW_TILE, HEAD_DIM))

    qs, kvs = _att_specs(T, G)
    return pl.pallas_call(
        body, name=name, grid=(cfg['NKV'], T // ROW_TILE),
        in_specs=[pl.BlockSpec(memory_space=pltpu.SMEM), qs, kvs, kvs], out_specs=[qs, qs],
        out_shape=[jax.ShapeDtypeStruct(q.shape, F32), jax.ShapeDtypeStruct(q.shape, F32)],
        compiler_params=_params())(sink, q, k, v)


def _attn_win_bwd(name, q, k, v, o, lse, dmix, sink, cfg):
    T, G, Lc = q.shape[0], cfg['G'], cfg['Lc']

    def body(sink_ref, q_ref, k_ref, v_ref, o_ref, lse_ref, do_ref, dq_ref, dk_ref, dv_ref, dsink_ref):
        h, i = pl.program_id(0), pl.program_id(1)
        start, bias = _band(i, T, Lc)
        kc, vc = k_ref[0:Lc, :], v_ref[0:Lc, :]
        kb, vb = k_ref[pl.ds(start, BAND), :], v_ref[pl.ds(start, BAND), :]

        @pl.when(i == 0)
        def _():
            dk_ref[...] = jnp.zeros_like(dk_ref)
            dv_ref[...] = jnp.zeros_like(dv_ref)
            dsink_ref[...] = jnp.zeros_like(dsink_ref)

        for g in range(G):
            sl = slice(g * HEAD_DIM, (g + 1) * HEAD_DIM)
            qg, do = q_ref[:, sl], do_ref[:, sl]
            lse = lse_ref[:, g * HEAD_DIM:g * HEAD_DIM + 1]
            delta = jnp.sum(do * o_ref[:, sl], axis=1, keepdims=True)
            pc = jnp.exp2(_nt(qg, kc) - lse)
            pb = jnp.exp2(_nt(qg, kb) + bias - lse)
            ps = jnp.exp2(sink_ref[h * G + g] * LOG2_E - lse)
            dob = _bf(do)
            dv_ref[0:Lc, :] += _tn(_bf(pc), dob)
            dv_ref[pl.ds(start, BAND), :] += _tn(_bf(pb), dob)
            dsc = _bf(pc * (_nt(dob, vc) - delta))
            dsb = _bf(pb * (_nt(dob, vb) - delta))
            dq_ref[:, sl] = _nn(dsc, kc) + _nn(dsb, kb)
            dk_ref[0:Lc, :] += _tn(dsc, qg)
            dk_ref[pl.ds(start, BAND), :] += _tn(dsb, qg)
            dsk = jnp.where(i > 0, -jnp.sum(ps * delta, axis=0, keepdims=True), 0.0)
            dsink_ref[:, sl] += jnp.broadcast_to(dsk, (8, HEAD_DIM))

    qs, kvs = _att_specs(T, G)
    return pl.pallas_call(
        body, name=name, grid=(cfg['NKV'], T // ROW_TILE),
        in_specs=[pl.BlockSpec(memory_space=pltpu.SMEM), qs, kvs, kvs, qs, qs, qs],
        out_specs=[qs, kvs, kvs, pl.BlockSpec((None, 8, G * HEAD_DIM), lambda h, i: (h, 0, 0))],
        out_shape=[jax.ShapeDtypeStruct(q.shape, F32), jax.ShapeDtypeStruct(k.shape, F32), jax.ShapeDtypeStruct(k.shape, F32),
                   jax.ShapeDtypeStruct((cfg['NKV'], 8, G * HEAD_DIM), F32)],
        compiler_params=_params())(sink, q, k, v, o, lse, dmix)


def _seq_pos(T, Lc):
    row = lax.broadcasted_iota(jnp.int32, (T, 1), 0)
    return jnp.where(row < Lc, row, row - Lc), jnp.where(row < Lc, Lc, T - Lc)


def _fw(x, k, pos, seglen):
    return jnp.where(pos + k < seglen, pltpu.roll(x, x.shape[0] - k, 0), 0.0)


def _bw(x, k, pos):
    return jnp.where(pos - k >= 0, pltpu.roll(x, k, 0), 0.0)


def _conv_fwd(name, P, conv_w8, cfg):
    T, Lc = P.shape[0], cfg['Lc']
    cb = (cfg['ATT'] + 2 * cfg['KVW']) // HEAD_DIM
    na = AUX_WIDTH // HEAD_DIM

    def body(gb_ref, gc_ref, u_ref, w_ref, o_ref):
        pos, seglen = _seq_pos(T, Lc)
        z = gc_ref[...] * u_ref[...]
        conv = w_ref[0:1, :] * _bw(z, 1, pos) + w_ref[1:2, :] * z + w_ref[2:3, :] * _fw(z, 1, pos, seglen)
        o_ref[...] = gb_ref[...] * conv

    col = lambda off: pl.BlockSpec((T, HEAD_DIM), lambda c: (0, cb + off + c))
    return pl.pallas_call(
        body, name=name, grid=(na,),
        in_specs=[col(0), col(na), col(2 * na), pl.BlockSpec((8, HEAD_DIM), lambda c: (0, c))],
        out_specs=pl.BlockSpec((T, HEAD_DIM), lambda c: (0, c)),
        out_shape=jax.ShapeDtypeStruct((T, AUX_WIDTH), F32), compiler_params=_params())(P, P, P, conv_w8)


def _conv_bwd(name, P, conv_w8, dmix, cfg):
    T, Lc = P.shape[0], cfg['Lc']
    cb = (cfg['ATT'] + 2 * cfg['KVW']) // HEAD_DIM
    ob = cfg['ATT'] // HEAD_DIM
    na = AUX_WIDTH // HEAD_DIM

    def body(gb_ref, gc_ref, u_ref, w_ref, do_ref, dgb_ref, dgc_ref, du_ref, dw_ref):
        pos, seglen = _seq_pos(T, Lc)
        gc, u, do = gc_ref[...], u_ref[...], do_ref[...]
        z = gc * u
        zm, zp = _bw(z, 1, pos), _fw(z, 1, pos, seglen)
        w0, w1, w2 = w_ref[0:1, :], w_ref[1:2, :], w_ref[2:3, :]
        dgb_ref[...] = (do * (w0 * zm + w1 * z + w2 * zp)).astype(dgb_ref.dtype)
        dc = do * gb_ref[...]
        dz = w0 * _fw(dc, 1, pos, seglen) + w1 * dc + w2 * _bw(dc, 1, pos)
        dgc_ref[...] = (dz * u).astype(dgc_ref.dtype)
        du_ref[...] = (dz * gc).astype(du_ref.dtype)
        dw_ref[...] = jnp.zeros_like(dw_ref)
        dw_ref[0:1, :] = jnp.sum(dc * zm, axis=0, keepdims=True)
        dw_ref[1:2, :] = jnp.sum(dc * z, axis=0, keepdims=True)
        dw_ref[2:3, :] = jnp.sum(dc * zp, axis=0, keepdims=True)

    col = lambda off: pl.BlockSpec((T, HEAD_DIM), lambda c: (0, cb + off + c))
    wspec = pl.BlockSpec((8, HEAD_DIM), lambda c: (0, c))
    ocol = lambda off: pl.BlockSpec((T, HEAD_DIM), lambda c: (0, off + c))
    return pl.pallas_call(
        body, name=name, grid=(na,),
        in_specs=[col(0), col(na), col(2 * na), wspec, ocol(ob)],
        out_specs=[ocol(0), ocol(0), ocol(0), wspec],
        out_shape=[jax.ShapeDtypeStruct((T, AUX_WIDTH), BF16)] * 3 + [jax.ShapeDtypeStruct((8, AUX_WIDTH), F32)],
        compiler_params=_params())(P, P, P, conv_w8, dmix)


def _window_sums(x, half, pos, seglen):
    fwd, bwd = x, x
    s = 1
    while s < half:
        fwd = fwd + _fw(fwd, s, pos, seglen)
        bwd = bwd + _bw(bwd, s, pos)
        s *= 2
    return fwd, bwd


def _pooled(u, half, pos, seglen):
    fwd, bwd = _window_sums(u, half, pos, seglen)
    cnt = (jnp.minimum(pos + half, seglen) - jnp.maximum(pos - half, 0)).astype(F32)
    return (fwd + _bw(bwd, 1, pos)) / cnt - u, cnt


def _pool_fwd(name, P, pool_w, pool_scale, cfg):
    T, Lc = P.shape[0], cfg['Lc']
    cb = (cfg['ATT'] + 2 * cfg['KVW']) // HEAD_DIM

    def body(u_ref, w_ref, s_ref, o_ref):
        g = pl.program_id(0)
        pos, seglen = _seq_pos(T, Lc)
        for k, half in enumerate(POOL_HALF):
            @pl.when(g == k)
            def _(half=half):
                pooled, _ = _pooled(u_ref[...], half, pos, seglen)
                o_ref[...] = _nn(_bf(pooled), _bf(w_ref[...])) * s_ref[...]

    return pl.pallas_call(
        body, name=name, grid=(AUX_GROUPS,),
        in_specs=[pl.BlockSpec((T, HEAD_DIM), lambda g: (0, cb + g)), pl.BlockSpec((None, HEAD_DIM, HEAD_DIM), lambda g: (g, 0, 0)),
                  pl.BlockSpec((1, HEAD_DIM), lambda g: (0, g))],
        out_specs=pl.BlockSpec((T, HEAD_DIM), lambda g: (0, g)),
        out_shape=jax.ShapeDtypeStruct((T, AUX_WIDTH), F32), compiler_params=_params())(P, pool_w, pool_scale)


def _pool_bwd(name, P, pool_w, pool_scale, dmix, cfg):
    T, Lc = P.shape[0], cfg['Lc']
    cb = (cfg['ATT'] + 2 * cfg['KVW']) // HEAD_DIM
    ob = cfg['ATT'] // HEAD_DIM

    def body(u_ref, w_ref, s_ref, do_ref, du_ref, dw_ref, ds_ref):
        g = pl.program_id(0)
        pos, seglen = _seq_pos(T, Lc)
        for k, half in enumerate(POOL_HALF):
            @pl.when(g == k)
            def _(half=half):
                do = do_ref[...]
                pooled, cnt = _pooled(u_ref[...], half, pos, seglen)
                wb = _bf(w_ref[...])
                mixed = _nn(_bf(pooled), wb)
                ds_ref[...] = jnp.broadcast_to(jnp.sum(do * mixed, axis=0, keepdims=True), ds_ref.shape)
                dmixed = _bf(do * s_ref[...])
                dw_ref[...] = _tn(_bf(pooled), dmixed)
                dpooled = _nt(dmixed, wb)
                e = dpooled / cnt
                fwd, bwd = _window_sums(e, half, pos, seglen)
                adj = fwd + _fw(e, half, pos, seglen) + _bw(bwd, 1, pos) - _bw(e, half, pos)
                du_ref[...] = (adj - dpooled).astype(du_ref.dtype)

    wspec = pl.BlockSpec((None, HEAD_DIM, HEAD_DIM), lambda g: (g, 0, 0))
    return pl.pallas_call(
        body, name=name, grid=(AUX_GROUPS,),
        in_specs=[pl.BlockSpec((T, HEAD_DIM), lambda g: (0, cb + g)), wspec, pl.BlockSpec((1, HEAD_DIM), lambda g: (0, g)),
                  pl.BlockSpec((T, HEAD_DIM), lambda g: (0, ob + g))],
        out_specs=[pl.BlockSpec((T, HEAD_DIM), lambda g: (0, g)), wspec, pl.BlockSpec((8, HEAD_DIM), lambda g: (0, g))],
        out_shape=[jax.ShapeDtypeStruct((T, AUX_WIDTH), BF16), jax.ShapeDtypeStruct(pool_w.shape, F32),
                   jax.ShapeDtypeStruct((8, AUX_WIDTH), F32)],
        compiler_params=_params())(P, pool_w, pool_scale, dmix)


def _ffn_up(name, hn, wg3, wu3, dep=None):
    T, D = hn.shape
    J, k, _ = wg3.shape
    tm = _row_tile(T, 1088)
    dep_ins, dep_specs = _dep(dep, 2)

    def body(x_ref, wg_ref, wu_ref, *rest):
        g_ref, u_ref, a_ref = rest[len(dep_ins):]
        x = x_ref[...]
        g, u = _nt(x, wg_ref[...]), _nt(x, wu_ref[...])
        g_ref[...] = g.astype(g_ref.dtype)
        u_ref[...] = u.astype(u_ref.dtype)
        a_ref[...] = (g * jax.nn.sigmoid(g) * u).astype(a_ref.dtype)

    wspec = pl.BlockSpec((None, k, D), lambda j, i: (j, 0, 0))
    ospec = pl.BlockSpec((None, tm, k), lambda j, i: (j, i, 0))
    return pl.pallas_call(
        body, name=name, grid=(J, T // tm), in_specs=[pl.BlockSpec((tm, D), lambda j, i: (i, 0)), wspec, wspec] + dep_specs,
        out_specs=[ospec, ospec, ospec],
        out_shape=[jax.ShapeDtypeStruct((J, T, k), BF16)] * 3,
        compiler_params=_params())(hn, wg3, wu3, *dep_ins)


def _ffn_dact(name, dF, wd3, G, U):
    T, D = dF.shape
    J, k, _ = wd3.shape
    tm = _row_tile(T, 1088)

    def body(df_ref, wd_ref, g_ref, u_ref, dg_ref, du_ref):
        da = _nt(df_ref[...], wd_ref[...])
        g = g_ref[...].astype(F32)
        sig = jax.nn.sigmoid(g)
        du_ref[...] = (da * g * sig).astype(du_ref.dtype)
        dg_ref[...] = (da * u_ref[...].astype(F32) * (sig * (1 + g * (1 - sig)))).astype(dg_ref.dtype)

    aspec = pl.BlockSpec((None, tm, k), lambda j, i: (j, i, 0))
    return pl.pallas_call(
        body, name=name, grid=(J, T // tm),
        in_specs=[pl.BlockSpec((tm, D), lambda j, i: (i, 0)), pl.BlockSpec((None, k, D), lambda j, i: (j, 0, 0)), aspec, aspec],
        out_specs=[aspec, aspec],
        out_shape=[jax.ShapeDtypeStruct((J, T, k), BF16), jax.ShapeDtypeStruct((J, T, k), BF16)],
        compiler_params=_params())(dF, wd3, G, U)


def _loss_head(name, h, g, target, cfg):
    T, D = h.shape

    def body(h_ref, g_ref, t_ref, dh_ref, loss_ref, dg_ref):
        i = pl.program_id(0)

        @pl.when(i == 0)
        def _():
            dh_ref[...] = jnp.zeros_like(dh_ref)
            loss_ref[...] = jnp.zeros_like(loss_ref)
            dg_ref[...] = jnp.zeros_like(dg_ref)

        @pl.when(i > 0)
        def _():
            x = h_ref[...]
            r = _rstd(x)
            xhat = x * r
            gg = g_ref[...]
            err = xhat * gg - t_ref[...]
            loss_ref[...] += 0.5 * jnp.sum(jnp.sum(err * err, axis=1, keepdims=True) / D, axis=0, keepdims=True)
            dy = err / D
            dg_ref[0:1, :] += jnp.sum(dy * xhat, axis=0, keepdims=True)
            dxh = dy * gg
            dh_ref[...] = r * (dxh - xhat * jnp.mean(dxh * xhat, axis=-1, keepdims=True))

    row = pl.BlockSpec((ROW_TILE, D), lambda i: (i, 0))
    return pl.pallas_call(
        body, name=name, grid=(T // ROW_TILE,),
        in_specs=[row, pl.BlockSpec((1, D), lambda i: (0, 0)), pl.BlockSpec((ROW_TILE, D), lambda i: (jnp.maximum(i - 1, 0), 0))],
        out_specs=[row, pl.BlockSpec((8, 128), lambda i: (0, 0)), pl.BlockSpec((8, D), lambda i: (0, 0))],
        out_shape=[jax.ShapeDtypeStruct((T, D), F32), jax.ShapeDtypeStruct((8, 128), F32), jax.ShapeDtypeStruct((8, D), F32)],
        compiler_params=_params())(h, g, target)


def _adamw(name, parts, w, m, v, dep=None):
    R, C = w.shape
    n_parts = parts.shape[0]
    tr = _row_tile(R, max(16, (1 << 18) // C)) if R % 16 == 0 else R
    bc1 = 1.0 - ADAM_B1 ** ADAM_STEP
    bc2 = 1.0 - ADAM_B2 ** ADAM_STEP
    dep_ins, dep_specs = _dep(dep, 1)

    def body(p_ref, w_ref, m_ref, v_ref, *rest):
        g_ref, d_ref, nm_ref, nv_ref = rest[len(dep_ins):]
        g = p_ref[0].astype(F32)
        for k in range(1, n_parts):
            g = g + p_ref[k].astype(F32)
        nm = ADAM_B1 * m_ref[...] + (1.0 - ADAM_B1) * g
        nv = ADAM_B2 * v_ref[...] + (1.0 - ADAM_B2) * (g * g)
        g_ref[...] = g
        nm_ref[...] = nm
        nv_ref[...] = nv
        d_ref[...] = -ADAM_LR * ((nm / bc1) / (jnp.sqrt(nv / bc2) + ADAM_EPS) + ADAM_WD * w_ref[...])

    blk = pl.BlockSpec((tr, C), lambda i: (i, 0))
    return pl.pallas_call(
        body, name=name, grid=(R // tr,), in_specs=[pl.BlockSpec((n_parts, tr, C), lambda i: (0, i, 0)), blk, blk, blk] + dep_specs,
        out_specs=[blk] * 4, out_shape=[jax.ShapeDtypeStruct((R, C), F32)] * 4, compiler_params=_params())(parts, w, m, v, *dep_ins)


class _WeightStream:
    def __init__(self, cast):
        self.cast, self.handles = cast, {}

    @staticmethod
    def _tag(l, group):
        return ("ffn" if group is FFN_WEIGHTS else group[0]) + str(l)

    def start(self, l, group, after=None):
        self.handles[l, group], token = _gather_start(f"gather_{self._tag(l, group)}_start", [self.cast(l, n) for n in group], after)
        return token

    def relay(self, l, group, after):
        self.handles[l, group], token = _gather_relay(f"gather_{self._tag(l, group)}_relay", self.handles[l, group], after)
        return token

    def get(self, l, group, after):
        got = dict(zip(group, _gather_wait(f"gather_{self._tag(l, group)}_wait", self.handles[l, group], after)))
        if 'w_out' in got:
            rows, cols = got['w_out'].shape[1:]
            got['w_out'] = got['w_out'].reshape(N_DEV * rows, cols)
        return got


def _layer_fwd(l, h, p, stream, mod, rope, conv_w8, cfg):
    nm = f"l{l}_"
    mod = mod + stream.relay(l, IN_WEIGHT, h)[0, 0]
    xn = _norm_mod(nm + "norm1", h, p['norm1_g'], mod, 0)
    W = stream.get(l, IN_WEIGHT, xn)
    token = None
    if l == 0:
        token = stream.start(0, OUT_WEIGHT, after=W['w_in']) + stream.start(0, FFN_WEIGHTS, after=W['w_in'])
    P = _mm_cols(nm + "w_in", xn, W['w_in'], dep=token)
    qr, kr, vb = _qk_prep(nm + "qk_prep", P, p['q_norm_g'], p['k_norm_g'], rope[0], rope[1], cfg)
    if l == 0:
        o, lse = _attn_dense_fwd(nm + "attn", qr, kr, vb, cfg)
        aux = _conv_fwd(nm + "conv", P, conv_w8, cfg)
    else:
        o, lse = _attn_win_fwd(nm + "attn", qr, kr, vb, p['sink'], cfg)
        aux = _pool_fwd(nm + "pool", P, p['pool_w'], p['pool_scale'], cfg)
    mix = jnp.concatenate([o, aux], axis=1).astype(BF16)
    W.update(stream.get(l, OUT_WEIGHT, stream.relay(l, OUT_WEIGHT, mix)))
    y = _mm_plain(nm + "w_out", mix, W['w_out'], False, dep=stream.relay(l, FFN_WEIGHTS, o))
    h2 = _gate_res(nm + "res1", h, y, mod, 2)
    hn = _norm_mod(nm + "norm2", h2, p['norm2_g'], mod, 1)
    W.update(stream.get(l, FFN_WEIGHTS, hn))
    token = stream.start(1, IN_WEIGHT, after=W['w_down']) if l == 0 else None
    G, U, A = _ffn_up(nm + "ffn_up", hn, W['w_gate'], W['w_up'], dep=token)
    if l == 0:
        token = stream.start(1, OUT_WEIGHT, after=A) + stream.start(1, FFN_WEIGHTS, after=A)
    F = _mm_shards_nn(nm + "w_down", A, W['w_down'], dep=token)
    h3 = _gate_res(nm + "res2", h2, F, mod, 5)
    saved = dict(h=h, xn=xn, P=P, qr=qr, kr=kr, vb=vb, o=o, lse=lse, mix=mix, y=y, h2=h2, hn=hn, G=G, U=U, A=A, F=F)
    return h3, saved, W


def _layer_bwd(l, dh3, s, p, W, mod, rope, conv_w8, cfg):
    nm = f"l{l}_bwd_"
    J = N_DEV
    dF, dmod = _gate_bwd(nm + "res2", dh3, s['F'], mod, 5)
    dG, dU = _ffn_dact(nm + "ffn_act", dF, W['w_down'], s['G'], s['U'])
    big = {'w_down': _wgrad_down(nm + "dw_down", s['A'], dF),
           'w_gate': _wgrad_down(nm + "dw_gate", dG, s['hn']),
           'w_up': _wgrad_down(nm + "dw_up", dU, s['hn'])}
    handles = {}
    handles['ffn'], token = _exchange_start(f"scatter_ffn{l}_start", [big[n] for n in FFN_WEIGHTS], True)
    mod = mod + token[0, 0]
    dhn = _mm_shards_nn2(nm + "dhn", dG, W['w_gate'], dU, W['w_up'])
    dh2, dm, dg2 = _norm_mod_bwd(nm + "norm2", dhn, s['h2'], p['norm2_g'], mod, 1, dh3)
    dmod += dm
    dY, dm = _gate_bwd(nm + "res1", dh2, s['y'], mod, 2)
    dmod += dm
    dwo = _wgrad_rows(nm + "dw_out", s['mix'], dY)
    handles['w_out'], token = _exchange_start(f"scatter_w_out{l}_start", [dwo.reshape((J, dwo.shape[0] // J, dwo.shape[1]))], True)
    dmix = _mm_plain(nm + "dmix", dY, W['w_out'], True, dep=token)
    small = {'norm2_g': dg2[0]}
    if l == 0:
        dqr, dkr, dv = _attn_dense_bwd(nm + "attn", s['qr'], s['kr'], s['vb'], s['o'], s['lse'], dmix, cfg)
        *daux, dcw = _conv_bwd(nm + "conv", s['P'], conv_w8, dmix, cfg)
        small['conv_w'] = dcw[0:3]
    else:
        dqr, dkr, dv, dsk = _attn_win_bwd(nm + "attn", s['qr'], s['kr'], s['vb'], s['o'], s['lse'], dmix, p['sink'], cfg)
        du, dpw, dps = _pool_bwd(nm + "pool", s['P'], p['pool_w'], p['pool_scale'], dmix, cfg)
        daux = [du]
        small.update(sink=dsk[:, 0, ::HEAD_DIM].reshape(-1), pool_w=dpw, pool_scale=dps[0])
    dq, dk, dqg, dkg = _qk_prep_bwd(nm + "qk_prep", dqr, dkr, s['P'], p['q_norm_g'], p['k_norm_g'], rope[0], rope[1], cfg)
    small.update(q_norm_g=dqg[0], k_norm_g=dkg[0])
    dP = jnp.concatenate([dq, dk, dv.astype(BF16), *daux], axis=1)
    dw_in = _wgrad_cols(nm + "dw_in", s['xn'], dP, J)
    if l == 0:
        handles['w_in'] = dw_in
    else:
        handles['w_in'], token = _exchange_start(f"scatter_w_in{l}_start", [dw_in], True)
        mod = mod + token[0, 0]
    dxn = _mm_cols_nt(nm + "dxn", dP, W['w_in'])
    dh, dm, dg1 = _norm_mod_bwd(nm + "norm1", dxn, s['h'], p['norm1_g'], mod, 0, dh2)
    dmod += dm
    small['norm1_g'] = dg1[0]
    return dh, dmod, small, handles, token


def _rope_tables(S, Lc):
    half = HEAD_DIM // 4
    pos = np.arange(S)
    inv = ROPE_THETA ** (-np.arange(0, 2 * half, 2, dtype=np.float32) / (2 * half))
    inv = jnp.asarray(inv, F32)
    ang_r = jnp.asarray(pos // GRID_W, F32)[:, None] * inv
    ang_c = jnp.asarray(pos % GRID_W, F32)[:, None] * inv
    cos = jnp.concatenate([jnp.cos(ang_r)] * 2 + [jnp.cos(ang_c)] * 2, axis=1)
    sin = jnp.concatenate([-jnp.sin(ang_r), jnp.sin(ang_r), -jnp.sin(ang_c), jnp.sin(ang_c)], axis=1)
    return (jnp.concatenate([jnp.ones((Lc, HEAD_DIM), F32), cos], axis=0),
            jnp.concatenate([jnp.zeros((Lc, HEAD_DIM), F32), sin], axis=0))


def _pad_rows(a, rows):
    return jnp.concatenate([a, jnp.zeros((rows - a.shape[0],) + a.shape[1:], a.dtype)], axis=0)


def _flat128(a, nlead):
    lead = a.shape[:nlead]
    f = a.reshape(lead + (-1,))
    pad = (-f.shape[-1]) % 128
    if pad:
        f = jnp.concatenate([f, jnp.zeros(lead + (pad,), f.dtype)], axis=-1)
    return f.reshape(lead + (-1, 128))


def _pack(named, nlead=0):
    rows, layout, at = [], {}, 0
    for name, a in named:
        f = _flat128(a, nlead)
        n = f.shape[-2]
        pad = (-n) % 8
        if pad:
            f = jnp.concatenate([f, jnp.zeros(f.shape[:-2] + (pad, 128), f.dtype)], axis=-2)
        layout[name] = (at, n, a.shape[nlead:])
        rows.append(f)
        at += n + pad
    return jnp.concatenate(rows, axis=-2), layout


def _unpack(arr, layout, name):
    at, n, shape = layout[name]
    return arr[..., at:at + n, :].reshape(arr.shape[:-2] + (-1,))[..., :math.prod(shape)].reshape(arr.shape[:-2] + tuple(shape))


def kernel(*args):
    A = dict(zip(INPUT_NAMES, args, strict=True))
    x, ctx = A['x'][0], A['ctx'][0]
    S, D = x.shape
    Lc = ctx.shape[0]
    T = Lc + S
    ATT = D - AUX_WIDTH
    KVW = (A['l1_w_in'].shape[1] * N_DEV - ATT - AUX_WIDTH) // 2
    cfg = dict(ATT=ATT, KVW=KVW, NKV=KVW // HEAD_DIM, G=ATT // KVW, Lc=Lc)
    assert Lc == ROW_TILE and S % ROW_TILE == 0 and T >= BAND and S % GRID_W == 0
    cw = A['l0_conv_w'].shape[1]
    me = 4 * lax.axis_index("x") + 2 * lax.axis_index("y") + lax.axis_index("c")

    def layer_params(l):
        pre = f"l{l}_"
        return {k[len(pre):]: (v.reshape(1, -1) if v.ndim == 1 and k != 'l1_sink' else v) for k, v in A.items() if k.startswith(pre)}

    params = [layer_params(0), layer_params(1)]

    def cast(l, n):
        w = A[f'l{l}_{n}']
        return (w.T if n in TRANSPOSED else w).astype(BF16)

    stream = _WeightStream(cast)

    sc_own = jax.nn.silu(A['c'])
    first, lay0 = _pack([('sc', sc_own), ('conv_w', A['l0_conv_w'])])
    first_all = _exchange("gather_cond", [first], False)[0]
    sc_all = _unpack(first_all, lay0, 'sc')[:, 0]
    conv_w = _unpack(first_all, lay0, 'conv_w').transpose(1, 0, 2).reshape(3, N_DEV * cw)
    conv_w8 = _pad_rows(conv_w, 8)
    sc_ctx = jax.nn.silu(A['c_ctx'])
    s16 = _pad_rows(jnp.concatenate([sc_all, sc_ctx[None]], axis=0), 16)

    nmod = A['l0_w_mod'].shape[1]
    modp = jnp.concatenate([_mm_plain(f"l{l}_mod", s16, A[f'l{l}_w_mod'], False) for l in range(2)], axis=1)
    modp_all = _exchange("gather_mod", [modp], False)[0]
    mods = []
    for l in range(2):
        full = modp_all[:, :, l * nmod:(l + 1) * nmod].transpose(1, 0, 2).reshape(16, N_MOD * D) + A[f'l{l}_b_mod'][None]
        both = jnp.stack([full[8], lax.dynamic_index_in_dim(full, me, 0, keepdims=False)]).reshape(2, N_MOD, D)
        mods.append(jnp.concatenate([both, jnp.zeros((2, 8 - N_MOD, D), F32)], axis=1))

    token = stream.start(0, IN_WEIGHT, after=modp_all)
    mods[0] = mods[0] + token[0, 0]

    rope = _rope_tables(S, Lc)
    h = jnp.concatenate([ctx, x], axis=0)
    saved, W = [], []
    for l in range(2):
        h, s, Wl = _layer_fwd(l, h, params[l], stream, mods[l], rope, conv_w8, cfg)
        saved.append(s)
        W.append(Wl)

    dh, loss_blk, dgf = _loss_head("loss_head", h, A['final_norm_g'].reshape(1, -1), A['loss_target'][0], cfg)
    loss = lax.psum(loss_blk[0, 0], ("x", "y", "c"))

    grads, small, dmods, scatters = {}, {'final_norm_g': dgf[0]}, [None, None], [None, None]
    token = jnp.zeros((8, 128), F32)
    for l in (1, 0):
        dh, dmods[l], sm, scatters[l], token = _layer_bwd(l, dh, saved[l], params[l], W[l], mods[l] + token[0, 0], rope, conv_w8, cfg)
        small.update({f'l{l}_{k}': v for k, v in sm.items()})
    grad_x = dh[Lc:][None]

    def landed(l, key, after):
        group = FFN_WEIGHTS if key == 'ffn' else (key,)
        for n, parts in zip(group, _exchange_wait(f"scatter_{key}{l}_wait", scatters[l][key], after)):
            shape = A[f'l{l}_{n}'].shape
            grads[f'l{l}_{n}'] = (parts.reshape((N_DEV,) + (shape[::-1] if n in TRANSPOSED else shape)), None)

    small_names = [n for n in WEIGHT_NAMES if n in small]
    pieces = [(n, small[n]) for n in small_names]
    for l in range(2):
        pieces += [(f'dmod{l}', dmods[l][1, :N_MOD]), (f'dcmod{l}', dmods[l][0, :N_MOD])]
    second, lay1 = _pack(pieces)
    second_all = _exchange("gather_small", [second], False)[0]

    dsc_part = jnp.zeros((16, D), F32)
    for l in range(2):
        dm16 = _pad_rows(jnp.concatenate([_unpack(second_all, lay1, f'dmod{l}').reshape(N_DEV, N_MOD * D),
                                          jnp.sum(_unpack(second_all, lay1, f'dcmod{l}'), axis=0).reshape(1, N_MOD * D)], axis=0), 16)
        mine = lax.dynamic_slice_in_dim(dm16, me * nmod, nmod, axis=1)
        tk = _col_tile(D, 512)
        gw = _mm_tn(f"l{l}_dw_mod", s16, mine, pl.BlockSpec((16, tk), lambda kb, t: (0, kb)), pl.BlockSpec((16, nmod), lambda kb, t: (0, 0)),
                    jax.ShapeDtypeStruct((D, nmod), F32), pl.BlockSpec((tk, nmod), lambda kb, t: (kb, 0)), (tk, nmod), (D // tk, 1))
        grads[f'l{l}_w_mod'] = (gw[None], None)
        dsc_part += _mm_plain(f"l{l}_dsc", mine, A[f'l{l}_w_mod'], True)
        dmod_dev = _unpack(second_all, lay1, f'dmod{l}') + _unpack(second_all, lay1, f'dcmod{l}')
        grads[f'l{l}_b_mod'] = (dmod_dev.reshape(N_DEV, N_MOD * D), None)
    dsig = jax.nn.sigmoid(A['c_ctx'])
    dsilu = dsig * (1 + A['c_ctx'] * (1 - dsig))
    third_all = _exchange("gather_dsc", [dsc_part[8:9]], False)[0]
    grads['c_ctx'] = (third_all[:, 0] * dsilu[None], None)
    scatters[0]['w_in'], last_token = _exchange_start("scatter_w_in0_start", [scatters[0]['w_in']], True, after=third_all)
    for n in small_names:
        g8 = _unpack(second_all, lay1, n)
        if n == 'l0_conv_w':
            g8 = lax.dynamic_slice_in_dim(g8, me * cw, cw, axis=2)
        grads[n] = (g8, None)

    out = {}
    big_names = [n for n in WEIGHT_NAMES if n[3:] in BIG_WEIGHTS + ('w_mod',)]

    def adam(n, dep=None):
        w, m, v = A[n], A['m_' + n], A['v_' + n]
        if n[3:] in TRANSPOSED:
            out[n] = tuple(r.T for r in _adamw("adamw_" + n, grads[n][0], w.T, m.T, v.T, dep))
        else:
            out[n] = _adamw("adamw_" + n, grads[n][0], w, m, v, dep)
        return out[n][1]

    last = adam('l0_w_mod', last_token)
    adam('l1_w_mod', last_token)
    for l in (1, 0):
        for key in ('ffn', 'w_out', 'w_in'):
            landed(l, key, last)
            for n in (FFN_WEIGHTS if key == 'ffn' else (key,)):
                last = adam(f'l{l}_{n}')
    rest = [n for n in WEIGHT_NAMES if n not in big_names]
    wp, layw = _pack([(n, A[n]) for n in rest])
    mp, _ = _pack([(n, A['m_' + n]) for n in rest])
    vp, _ = _pack([(n, A['v_' + n]) for n in rest])
    gp, _ = _pack([(n, grads[n][0]) for n in rest], nlead=1)
    res = _adamw("adamw_small", gp, wp, mp, vp)
    for n in rest:
        out[n] = tuple(_unpack(r, layw, n) for r in res)

    outs = [loss, grad_x]
    for k in range(4):
        outs += [out[n][k] for n in WEIGHT_NAMES]
    return tuple(outs)
```

```python
import functools
import math

import numpy as np
import jax
import jax.numpy as jnp
from jax import lax
from jax.experimental import pallas as pl
from jax.experimental.pallas import tpu as pltpu

F32 = jnp.float32
BF16 = jnp.bfloat16
HEAD_DIM = 128
AUX_WIDTH = 512
AUX_GROUPS = 4
POOL_HALF = (1, 2, 4, 8)
WINDOW = 128
GRID_W = 64
ROPE_THETA = 10000.0
EPS = 1e-6
NEG_INF = -1e30
ATT_SCALE = HEAD_DIM ** -0.5
LOG2_E = math.log2(math.e)
Q_SCALE = ATT_SCALE * LOG2_E
N_MOD = 6
N_DEV = 8
ROW_TILE = 256
BAND = ROW_TILE + 2 * WINDOW
ADAM_LR, ADAM_B1, ADAM_B2, ADAM_EPS, ADAM_WD, ADAM_STEP = 0.001, 0.9, 0.999, 1e-08, 0.01, 10
VMEM_LIMIT_MB = 56
MESH = pl.DeviceIdType.MESH

WEIGHT_NAMES = ['c_ctx', 'l0_norm1_g', 'l0_w_mod', 'l0_b_mod', 'l0_w_in', 'l0_q_norm_g', 'l0_k_norm_g', 'l0_conv_w', 'l0_w_out', 'l0_norm2_g', 'l0_w_gate', 'l0_w_up', 'l0_w_down', 'l1_norm1_g', 'l1_w_mod', 'l1_b_mod', 'l1_w_in', 'l1_q_norm_g', 'l1_k_norm_g', 'l1_sink', 'l1_pool_w', 'l1_pool_scale', 'l1_w_out', 'l1_norm2_g', 'l1_w_gate', 'l1_w_up', 'l1_w_down', 'final_norm_g']
INPUT_NAMES = (['x', 'c', 'ctx'] + WEIGHT_NAMES + ['loss_target'] + ['m_' + n for n in WEIGHT_NAMES]
               + ['v_' + n for n in WEIGHT_NAMES])
IN_WEIGHT = ('w_in',)
OUT_WEIGHT = ('w_out',)
MIXER_WEIGHTS = OUT_WEIGHT + IN_WEIGHT
FFN_WEIGHTS = ('w_down', 'w_gate', 'w_up')
TRANSPOSED = ('w_gate', 'w_up')
BIG_WEIGHTS = MIXER_WEIGHTS + FFN_WEIGHTS


def _params(vmem_mb=VMEM_LIMIT_MB):
    return pltpu.CompilerParams(vmem_limit_bytes=vmem_mb << 20)


def _row_tile(n, cap):
    best = None
    for t in range(16, min(n, cap) + 1, 16):
        if n % t == 0:
            best = t
    assert best is not None, (n, cap)
    return best


def _col_tile(n, cap):
    best = n
    for t in range(128, min(n, cap) + 1, 128):
        if n % t == 0:
            best = t
    return best if best <= cap or n % 128 else n


def _dot(a, b, ca, cb):
    return lax.dot_general(a, b, (((ca,), (cb,)), ((), ())), preferred_element_type=F32)


def _nn(a, b):
    return _dot(a, b, 1, 0)


def _nt(a, b):
    return _dot(a, b, 1, 1)


def _tn(a, b):
    return _dot(a, b, 0, 0)


def _bf(x):
    return x.astype(BF16)


def _exchange(name, arrs, scatter, after=None):
    n = len(arrs)
    extra = [] if after is None else [after]
    if scatter:
        out_shape = [jax.ShapeDtypeStruct(a.shape, a.dtype) for a in arrs]
    else:
        out_shape = [jax.ShapeDtypeStruct((N_DEV,) + a.shape, a.dtype) for a in arrs]

    def body(*refs):
        ins, outs = refs[:n], refs[n + len(extra):2 * n + len(extra)]
        send_sems, recv_sems, local_sems = refs[2 * n + len(extra):]
        x, y, c = lax.axis_index("x"), lax.axis_index("y"), lax.axis_index("c")
        me = 4 * x + 2 * y + c
        local, remote = [], []
        for a in range(n):
            own = ins[a].at[me] if scatter else ins[a]
            cp = pltpu.make_async_copy(own, outs[a].at[me], local_sems.at[a])
            cp.start()
            local.append(cp)
            for r in range(1, N_DEV):
                px = 1 - x if r & 4 else x
                py = 1 - y if r & 2 else y
                pc = 1 - c if r & 1 else c
                src = ins[a].at[4 * px + 2 * py + pc] if scatter else ins[a]
                cp = pltpu.make_async_remote_copy(
                    src_ref=src, dst_ref=outs[a].at[me], send_sem=send_sems.at[a, r - 1],
                    recv_sem=recv_sems.at[a, r - 1], device_id=(px, py, pc), device_id_type=MESH)
                cp.start()
                remote.append(cp)
        for cp in remote:
            cp.wait()
        for cp in local:
            cp.wait()

    any_spec = pl.BlockSpec(memory_space=pl.ANY)
    return pl.pallas_call(
        body, name=name, out_shape=out_shape,
        in_specs=[any_spec] * (n + len(extra)), out_specs=[any_spec] * n,
        scratch_shapes=[pltpu.SemaphoreType.DMA((n, N_DEV - 1)), pltpu.SemaphoreType.DMA((n, N_DEV - 1)),
                        pltpu.SemaphoreType.DMA((n,))],
    )(*arrs, *extra)


HBM_SPEC = pl.BlockSpec(memory_space=pltpu.HBM)
SEM_SPEC = pl.BlockSpec(memory_space=pltpu.SEMAPHORE)
EFFECT = pltpu.SideEffectType.DATAFLOW_SIDE_EFFECTING


def _split_copies(srcs, lands, send_sems, recv_sems, local_sems, scatter):
    x, y, c = lax.axis_index("x"), lax.axis_index("y"), lax.axis_index("c")
    me = 4 * x + 2 * y + c
    local, remote = [], []
    for a in range(len(srcs)):
        own = srcs[a].at[me] if scatter else srcs[a]
        local.append(pltpu.make_async_copy(own, lands[a].at[me], local_sems.at[a]))
        for r in range(1, N_DEV):
            px = 1 - x if r & 4 else x
            py = 1 - y if r & 2 else y
            pc = 1 - c if r & 1 else c
            src = srcs[a].at[4 * px + 2 * py + pc] if scatter else srcs[a]
            remote.append(pltpu.make_async_remote_copy(
                src_ref=src, dst_ref=lands[a].at[me], send_sem=send_sems.at[a * (N_DEV - 1) + r - 1],
                recv_sem=recv_sems.at[a * (N_DEV - 1) + r - 1], device_id=(px, py, pc), device_id_type=MESH))
    return local, remote


def _exchange_start(name, arrs, scatter, after=None):
    n = len(arrs)
    extra = [] if after is None else [after]
    shapes = [a.shape if scatter else (N_DEV,) + a.shape for a in arrs]
    lands = [pltpu.with_memory_space_constraint(lax.empty(s, a.dtype), pltpu.HBM) for s, a in zip(shapes, arrs)]
    srcs = [pltpu.with_memory_space_constraint(a, pltpu.HBM) for a in arrs]

    def body(*refs):
        src_refs, land_refs = refs[:n], refs[n:2 * n]
        send_sems, recv_sems, local_sems = refs[2 * n + len(extra):2 * n + len(extra) + 3]
        token = refs[-1]
        local, remote = _split_copies(src_refs, land_refs, send_sems, recv_sems, local_sems, scatter)
        for cp in local + remote:
            cp.start()
        token[...] = jnp.zeros_like(token)

    res = pl.pallas_call(
        body, name=name,
        out_shape=[pltpu.SemaphoreType.DMA((n * (N_DEV - 1),)), pltpu.SemaphoreType.DMA((n * (N_DEV - 1),)), pltpu.SemaphoreType.DMA((n,))]
        + [pltpu.HBM(a.shape, a.dtype) for a in arrs] + [pltpu.HBM(s, a.dtype) for s, a in zip(shapes, arrs)]
        + [jax.ShapeDtypeStruct((8, 128), F32)],
        in_specs=[HBM_SPEC] * (2 * n) + [pl.BlockSpec(memory_space=pl.ANY)] * len(extra),
        out_specs=[SEM_SPEC] * 3 + [HBM_SPEC] * (2 * n) + [pl.BlockSpec(memory_space=pltpu.VMEM)],
        input_output_aliases={i: 3 + i for i in range(2 * n)},
        compiler_params=pltpu.CompilerParams(has_side_effects=EFFECT),
    )(*srcs, *lands, *extra)
    return (scatter, res[:3], res[3:3 + n], res[3 + n:3 + 2 * n]), res[-1]


def _exchange_wait(name, handle, after):
    scatter, sems, srcs, lands = handle
    n = len(srcs)

    def body(*refs):
        src_refs, land_refs = refs[:n], refs[n:2 * n]
        send_sems, recv_sems, local_sems = refs[2 * n:2 * n + 3]
        local, remote = _split_copies(src_refs, land_refs, send_sems, recv_sems, local_sems, scatter)
        for cp in remote:
            cp.wait_send()
            cp.wait_recv()
        for cp in local:
            cp.wait()

    res = pl.pallas_call(
        body, name=name,
        out_shape=[pltpu.HBM(a.shape, a.dtype) for a in srcs] + [pltpu.HBM(a.shape, a.dtype) for a in lands],
        in_specs=[HBM_SPEC] * (2 * n) + [SEM_SPEC] * 3 + [pl.BlockSpec(memory_space=pl.ANY)], out_specs=[HBM_SPEC] * (2 * n),
        input_output_aliases={i: i for i in range(2 * n)},
        compiler_params=pltpu.CompilerParams(has_side_effects=EFFECT),
    )(*srcs, *lands, *sems, after)
    return list(res[n:])


FIRST_COPIES = 4
RELAY_COPIES = 3


def _gather_copies(srcs, lands, sems):
    send_sems, recv_sems, local_sems = sems[:3]
    x, y, c = lax.axis_index("x"), lax.axis_index("y"), lax.axis_index("c")
    me = 4 * x + 2 * y + c
    chips = [(1 - x, y), (x, 1 - y), (1 - x, 1 - y)]
    local, first, relay = [], [], []
    for a in range(len(srcs)):
        local.append(pltpu.make_async_copy(srcs[a], lands[a].at[me], local_sems.at[a]))
        targets = [(x, y, 1 - c)] + [(px, py, c) for px, py in chips]
        first.append([pltpu.make_async_remote_copy(
            src_ref=srcs[a], dst_ref=lands[a].at[me], send_sem=send_sems.at[FIRST_COPIES * a + k],
            recv_sem=recv_sems.at[FIRST_COPIES * a + k], device_id=t, device_id_type=MESH) for k, t in enumerate(targets)])
        if len(sems) > 3:
            rsend, rrecv = sems[3:]
            slots = [lands[a].at[4 * px + 2 * py + c] for px, py in chips]
            relay.append([pltpu.make_async_remote_copy(
                src_ref=slot, dst_ref=slot, send_sem=rsend.at[RELAY_COPIES * a + j], recv_sem=rrecv.at[RELAY_COPIES * a + j],
                device_id=(x, y, 1 - c), device_id_type=MESH) for j, slot in enumerate(slots)])
    return local, first, relay


def _gather_start(name, arrs, after=None):
    n = len(arrs)
    extra = [] if after is None else [after]
    lands = [pltpu.with_memory_space_constraint(lax.empty((N_DEV,) + a.shape, a.dtype), pltpu.HBM) for a in arrs]
    srcs = [pltpu.with_memory_space_constraint(a, pltpu.HBM) for a in arrs]

    def body(*refs):
        at = 2 * n + len(extra)
        local, first, _ = _gather_copies(refs[:n], refs[n:2 * n], refs[at:at + 3])
        for cp in local + [cp for cps in first for cp in cps]:
            cp.start()
        refs[-1][...] = jnp.zeros_like(refs[-1])

    res = pl.pallas_call(
        body, name=name,
        out_shape=[pltpu.SemaphoreType.DMA((FIRST_COPIES * n,)), pltpu.SemaphoreType.DMA((FIRST_COPIES * n,)), pltpu.SemaphoreType.DMA((n,))]
        + [pltpu.HBM(a.shape, a.dtype) for a in arrs] + [pltpu.HBM((N_DEV,) + a.shape, a.dtype) for a in arrs]
        + [jax.ShapeDtypeStruct((8, 128), F32)],
        in_specs=[HBM_SPEC] * (2 * n) + [pl.BlockSpec(memory_space=pl.ANY)] * len(extra),
        out_specs=[SEM_SPEC] * 3 + [HBM_SPEC] * (2 * n) + [pl.BlockSpec(memory_space=pltpu.VMEM)],
        input_output_aliases={i: 3 + i for i in range(2 * n)},
        compiler_params=pltpu.CompilerParams(has_side_effects=EFFECT),
    )(*srcs, *lands, *extra)
    return (list(res[:3]), list(res[3:3 + n]), list(res[3 + n:3 + 2 * n])), res[-1]


def _gather_relay(name, handle, after):
    sems, srcs, lands = handle
    n = len(srcs)

    def body(*refs):
        in_sems = refs[2 * n:2 * n + 3]
        out_sems = refs[2 * n + 4 + 2 * n:2 * n + 4 + 2 * n + 2]
        _, first, relay = _gather_copies(refs[:n], refs[n:2 * n], list(in_sems) + list(out_sems))
        for a in range(n):
            for j in range(RELAY_COPIES):
                first[a][1 + j].wait_recv()
                relay[a][j].start()
        refs[-1][...] = jnp.zeros_like(refs[-1])

    res = pl.pallas_call(
        body, name=name,
        out_shape=[pltpu.HBM(a.shape, a.dtype) for a in srcs] + [pltpu.HBM(a.shape, a.dtype) for a in lands]
        + [pltpu.SemaphoreType.DMA((RELAY_COPIES * n,)), pltpu.SemaphoreType.DMA((RELAY_COPIES * n,)), jax.ShapeDtypeStruct((8, 128), F32)],
        in_specs=[HBM_SPEC] * (2 * n) + [SEM_SPEC] * 3 + [pl.BlockSpec(memory_space=pl.ANY)],
        out_specs=[HBM_SPEC] * (2 * n) + [SEM_SPEC] * 2 + [pl.BlockSpec(memory_space=pltpu.VMEM)],
        input_output_aliases={i: i for i in range(2 * n)},
        compiler_params=pltpu.CompilerParams(has_side_effects=EFFECT),
    )(*srcs, *lands, *sems, after)
    return (sems + list(res[2 * n:2 * n + 2]), list(res[:n]), list(res[n:2 * n])), res[-1]


def _gather_wait(name, handle, after):
    sems, srcs, lands = handle
    n = len(srcs)

    def body(*refs):
        local, first, relay = _gather_copies(refs[:n], refs[n:2 * n], refs[2 * n:2 * n + 5])
        for a in range(n):
            for cp in first[a]:
                cp.wait_send()
            first[a][0].wait_recv()
            for cp in relay[a]:
                cp.wait_send()
                cp.wait_recv()
            local[a].wait()

    res = pl.pallas_call(
        body, name=name,
        out_shape=[pltpu.HBM(a.shape, a.dtype) for a in srcs] + [pltpu.HBM(a.shape, a.dtype) for a in lands],
        in_specs=[HBM_SPEC] * (2 * n) + [SEM_SPEC] * 5 + [pl.BlockSpec(memory_space=pl.ANY)], out_specs=[HBM_SPEC] * (2 * n),
        input_output_aliases={i: i for i in range(2 * n)},
        compiler_params=pltpu.CompilerParams(has_side_effects=EFFECT),
    )(*srcs, *lands, *sems, after)
    return list(res[n:])


def _dep(dep, grid_rank):
    if dep is None:
        return [], []
    return [dep], [pl.BlockSpec((8, 128), (lambda i, j: (0, 0)) if grid_rank == 2 else (lambda i: (0, 0)))]


def _mm_step(name, fn, ins, in_specs, out_shape, out_spec, grid, dep=None):
    n = len(ins)
    dep_ins, dep_specs = _dep(dep, len(grid))

    def body(*refs):
        o_ref = refs[n + len(dep_ins)]
        o_ref[...] = fn(*refs[:n]).astype(o_ref.dtype)

    return pl.pallas_call(body, name=name, grid=grid, in_specs=list(in_specs) + dep_specs, out_specs=out_spec,
                          out_shape=out_shape, compiler_params=_params())(*ins, *dep_ins)


def _mm_tn(name, a, b, a_spec, b_spec, out_shape, out_spec, acc_shape, grid):
    nk = grid[-1]
    kax = len(grid) - 1

    def body(a_ref, b_ref, o_ref, acc_ref):
        k = pl.program_id(kax)

        @pl.when(k == 0)
        def _():
            acc_ref[...] = jnp.zeros_like(acc_ref)

        acc_ref[...] += _tn(_bf(a_ref[...]), _bf(b_ref[...]))

        @pl.when(k == nk - 1)
        def _():
            o_ref[...] = acc_ref[...].astype(o_ref.dtype)

    return pl.pallas_call(body, name=name, grid=grid, in_specs=[a_spec, b_spec], out_specs=out_spec,
                          out_shape=out_shape, scratch_shapes=[pltpu.VMEM(acc_shape, F32)],
                          compiler_params=_params())(a, b)


def _mm_cols(name, a, w3, out_dtype=F32, dep=None):
    M, K = a.shape
    J, _, n = w3.shape
    tm = _row_tile(M, 1088)
    return _mm_step(
        name, lambda a_ref, w_ref: _nn(_bf(a_ref[...]), w_ref[...]), [a, w3],
        [pl.BlockSpec((tm, K), lambda j, i: (i, 0)), pl.BlockSpec((None, K, n), lambda j, i: (j, 0, 0))],
        jax.ShapeDtypeStruct((M, J * n), out_dtype), pl.BlockSpec((tm, n), lambda j, i: (i, j)), (J, M // tm), dep)


def _mm_plain(name, a, b, transpose_b, out_dtype=F32, tn=512, dep=None):
    M, K = a.shape
    N = b.shape[0] if transpose_b else b.shape[1]
    tm = _row_tile(M, 1088)
    tn = _col_tile(N, tn)
    if transpose_b:
        b_spec = pl.BlockSpec((tn, K), lambda j, i: (j, 0))
        fn = lambda a_ref, b_ref: _nt(_bf(a_ref[...]), _bf(b_ref[...]))
    else:
        b_spec = pl.BlockSpec((K, tn), lambda j, i: (0, j))
        fn = lambda a_ref, b_ref: _nn(_bf(a_ref[...]), _bf(b_ref[...]))
    return _mm_step(name, fn, [a, b], [pl.BlockSpec((tm, K), lambda j, i: (i, 0)), b_spec],
                    jax.ShapeDtypeStruct((M, N), out_dtype), pl.BlockSpec((tm, tn), lambda j, i: (i, j)),
                    (N // tn, M // tm), dep)


def _mm_shards_nn(name, a3, w3, tn=512, dep=None):
    J, M, k = a3.shape
    N = w3.shape[2]
    tm = _row_tile(M, 544)
    tn = _col_tile(N, tn)

    def fn(a_ref, w_ref):
        acc = _nn(a_ref[0], w_ref[0])
        for j in range(1, J):
            acc += _nn(a_ref[j], w_ref[j])
        return acc

    return _mm_step(name, fn, [a3, w3],
                    [pl.BlockSpec((J, tm, k), lambda jn, i: (0, i, 0)), pl.BlockSpec((J, k, tn), lambda jn, i: (0, 0, jn))],
                    jax.ShapeDtypeStruct((M, N), F32), pl.BlockSpec((tm, tn), lambda jn, i: (i, jn)), (N // tn, M // tm), dep)


def _mm_shards_nn2(name, a3, w3a, b3, w3b, tn=512):
    J, M, k = a3.shape
    N = w3a.shape[2]
    tm = _row_tile(M, 544)
    tn = _col_tile(N, tn)

    def fn(a_ref, wa_ref, b_ref, wb_ref):
        acc = _nn(a_ref[0], wa_ref[0]) + _nn(b_ref[0], wb_ref[0])
        for j in range(1, J):
            acc += _nn(a_ref[j], wa_ref[j]) + _nn(b_ref[j], wb_ref[j])
        return acc

    act = pl.BlockSpec((J, tm, k), lambda jn, i: (0, i, 0))
    wsp = pl.BlockSpec((J, k, tn), lambda jn, i: (0, 0, jn))
    return _mm_step(name, fn, [a3, w3a, b3, w3b], [act, wsp, act, wsp],
                    jax.ShapeDtypeStruct((M, N), F32), pl.BlockSpec((tm, tn), lambda jn, i: (i, jn)), (N // tn, M // tm))


def _mm_cols_nt(name, a, w3, tn=512):
    M = a.shape[0]
    J, N, n = w3.shape
    tm = _row_tile(M, 544)
    tn = _col_tile(N, tn)

    def fn(a_ref, w_ref):
        acc = _nt(a_ref[:, 0:n], w_ref[0])
        for j in range(1, J):
            acc += _nt(a_ref[:, j * n:(j + 1) * n], w_ref[j])
        return acc

    return _mm_step(name, fn, [a, w3],
                    [pl.BlockSpec((tm, J * n), lambda jn, i: (i, 0)), pl.BlockSpec((J, tn, n), lambda jn, i: (0, jn, 0))],
                    jax.ShapeDtypeStruct((M, N), F32), pl.BlockSpec((tm, tn), lambda jn, i: (i, jn)), (N // tn, M // tm))


def _wgrad_cols(name, a, b, J):
    T, K = a.shape
    n = b.shape[1] // J
    tt = _row_tile(T, 1088)
    return _mm_tn(name, a, b, pl.BlockSpec((tt, K), lambda j, t: (t, 0)), pl.BlockSpec((tt, n), lambda j, t: (t, j)),
                  jax.ShapeDtypeStruct((J, K, n), BF16), pl.BlockSpec((None, K, n), lambda j, t: (j, 0, 0)), (K, n), (J, T // tt))


def _wgrad_rows(name, a, b, tk=512):
    T, K = a.shape
    N = b.shape[1]
    tt = _row_tile(T, 1088)
    tk = _col_tile(K, tk)
    return _mm_tn(name, a, b, pl.BlockSpec((tt, tk), lambda kb, t: (t, kb)), pl.BlockSpec((tt, N), lambda kb, t: (t, 0)),
                  jax.ShapeDtypeStruct((K, N), BF16), pl.BlockSpec((tk, N), lambda kb, t: (kb, 0)), (tk, N), (K // tk, T // tt))


def _wgrad_up(name, a, b3):
    T, K = a.shape
    J, _, k = b3.shape
    tt = _row_tile(T, 1088)
    return _mm_tn(name, a, b3, pl.BlockSpec((tt, K), lambda j, t: (t, 0)), pl.BlockSpec((None, tt, k), lambda j, t: (j, t, 0)),
                  jax.ShapeDtypeStruct((J, K, k), BF16), pl.BlockSpec((None, K, k), lambda j, t: (j, 0, 0)), (K, k), (J, T // tt))


def _wgrad_down(name, a3, b):
    J, T, k = a3.shape
    N = b.shape[1]
    tt = _row_tile(T, 1088)
    return _mm_tn(name, a3, b, pl.BlockSpec((None, tt, k), lambda j, t: (j, t, 0)), pl.BlockSpec((tt, N), lambda j, t: (t, 0)),
                  jax.ShapeDtypeStruct((J, k, N), BF16), pl.BlockSpec((None, k, N), lambda j, t: (j, 0, 0)), (k, N), (J, T // tt))


def _seg(i):
    return jnp.minimum(i, 1)


def _rstd(x):
    return lax.rsqrt(jnp.mean(x * x, axis=-1, keepdims=True) + EPS)


def _norm_mod(name, h, g, mod, which):
    T, D = h.shape

    def body(h_ref, g_ref, mod_ref, o_ref):
        x = h_ref[...]
        n = x * _rstd(x) * g_ref[...]
        shift = mod_ref[3 * which:3 * which + 1, :]
        scale = mod_ref[3 * which + 1:3 * which + 2, :]
        o_ref[...] = (n * (1 + scale) + shift).astype(o_ref.dtype)

    row = pl.BlockSpec((ROW_TILE, D), lambda i: (i, 0))
    return pl.pallas_call(
        body, name=name, grid=(T // ROW_TILE,),
        in_specs=[row, pl.BlockSpec((1, D), lambda i: (0, 0)), pl.BlockSpec((None, 8, D), lambda i: (_seg(i), 0, 0))],
        out_specs=row, out_shape=jax.ShapeDtypeStruct((T, D), BF16), compiler_params=_params())(h, g, mod)


def _norm_mod_bwd(name, dxn, h, g, mod, which, dres):
    T, D = h.shape

    def body(dxn_ref, h_ref, g_ref, mod_ref, dres_ref, dh_ref, dmod_ref, dg_ref):
        i = pl.program_id(0)
        x = h_ref[...]
        r = _rstd(x)
        xhat = x * r
        g = g_ref[...]
        n = xhat * g
        scale = mod_ref[3 * which + 1:3 * which + 2, :]
        dxn = dxn_ref[...]
        dn = dxn * (1 + scale)
        dxh = dn * g
        dh_ref[...] = dres_ref[...] + r * (dxh - xhat * jnp.mean(dxh * xhat, axis=-1, keepdims=True))

        @pl.when(i <= 1)
        def _():
            dmod_ref[...] = jnp.zeros_like(dmod_ref)

        @pl.when(i == 0)
        def _():
            dg_ref[...] = jnp.zeros_like(dg_ref)

        dmod_ref[3 * which:3 * which + 1, :] += jnp.sum(dxn, axis=0, keepdims=True)
        dmod_ref[3 * which + 1:3 * which + 2, :] += jnp.sum(dxn * n, axis=0, keepdims=True)
        dg_ref[0:1, :] += jnp.sum(dn * xhat, axis=0, keepdims=True)

    row = pl.BlockSpec((ROW_TILE, D), lambda i: (i, 0))
    modspec = pl.BlockSpec((None, 8, D), lambda i: (_seg(i), 0, 0))
    return pl.pallas_call(
        body, name=name, grid=(T // ROW_TILE,),
        in_specs=[row, row, pl.BlockSpec((1, D), lambda i: (0, 0)), modspec, row],
        out_specs=[row, modspec, pl.BlockSpec((8, D), lambda i: (0, 0))],
        out_shape=[jax.ShapeDtypeStruct((T, D), F32), jax.ShapeDtypeStruct((2, 8, D), F32), jax.ShapeDtypeStruct((8, D), F32)],
        compiler_params=_params())(dxn, h, g, mod, dres)


def _gate_res(name, h, y, mod, row_idx):
    T, D = h.shape

    def body(h_ref, y_ref, mod_ref, o_ref):
        o_ref[...] = h_ref[...] + mod_ref[row_idx:row_idx + 1, :] * y_ref[...]

    row = pl.BlockSpec((ROW_TILE, D), lambda i: (i, 0))
    return pl.pallas_call(
        body, name=name, grid=(T // ROW_TILE,),
        in_specs=[row, row, pl.BlockSpec((None, 8, D), lambda i: (_seg(i), 0, 0))],
        out_specs=row, out_shape=jax.ShapeDtypeStruct((T, D), F32), compiler_params=_params())(h, y, mod)


def _gate_bwd(name, dh, y, mod, row_idx):
    T, D = dh.shape

    def body(dh_ref, y_ref, mod_ref, dy_ref, dmod_ref):
        i = pl.program_id(0)
        dh = dh_ref[...]
        dy_ref[...] = (dh * mod_ref[row_idx:row_idx + 1, :]).astype(dy_ref.dtype)

        @pl.when(i <= 1)
        def _():
            dmod_ref[...] = jnp.zeros_like(dmod_ref)

        dmod_ref[row_idx:row_idx + 1, :] += jnp.sum(dh * y_ref[...], axis=0, keepdims=True)

    row = pl.BlockSpec((ROW_TILE, D), lambda i: (i, 0))
    modspec = pl.BlockSpec((None, 8, D), lambda i: (_seg(i), 0, 0))
    return pl.pallas_call(
        body, name=name, grid=(T // ROW_TILE,), in_specs=[row, row, modspec], out_specs=[row, modspec],
        out_shape=[jax.ShapeDtypeStruct((T, D), BF16), jax.ShapeDtypeStruct((2, 8, D), F32)],
        compiler_params=_params())(dh, y, mod)


def _rot(y):
    lane = lax.broadcasted_iota(jnp.int32, y.shape, 1)
    return jnp.where((lane & 32) == 0, pltpu.roll(y, 96, 1), pltpu.roll(y, 32, 1))


def _qk_prep(name, P, q_g, k_g, rope_c, rope_s, cfg):
    T = P.shape[0]
    ATT, KVW = cfg['ATT'], cfg['KVW']

    def body(q_ref, k_ref, v_ref, qg_ref, kg_ref, c_ref, s_ref, qo_ref, ko_ref, vo_ref):
        cc, ss = c_ref[...], s_ref[...]

        def head(x, g):
            y = x * _rstd(x) * g
            return y * cc + _rot(y) * ss

        for hh in range(ATT // HEAD_DIM):
            sl = slice(hh * HEAD_DIM, (hh + 1) * HEAD_DIM)
            qo_ref[:, sl] = (head(q_ref[:, sl], qg_ref[...]) * Q_SCALE).astype(qo_ref.dtype)
        for hh in range(KVW // HEAD_DIM):
            sl = slice(hh * HEAD_DIM, (hh + 1) * HEAD_DIM)
            ko_ref[:, sl] = head(k_ref[:, sl], kg_ref[...]).astype(ko_ref.dtype)
        vo_ref[...] = v_ref[...].astype(vo_ref.dtype)

    kb = ATT // KVW
    gain = pl.BlockSpec((1, HEAD_DIM), lambda i: (0, 0))
    tab = pl.BlockSpec((ROW_TILE, HEAD_DIM), lambda i: (i, 0))
    qs = pl.BlockSpec((ROW_TILE, ATT), lambda i: (i, 0))
    ks = pl.BlockSpec((ROW_TILE, KVW), lambda i: (i, 0))
    return pl.pallas_call(
        body, name=name, grid=(T // ROW_TILE,),
        in_specs=[qs, pl.BlockSpec((ROW_TILE, KVW), lambda i: (i, kb)), pl.BlockSpec((ROW_TILE, KVW), lambda i: (i, kb + 1)),
                  gain, gain, tab, tab],
        out_specs=[qs, ks, ks],
        out_shape=[jax.ShapeDtypeStruct((T, ATT), BF16), jax.ShapeDtypeStruct((T, KVW), BF16), jax.ShapeDtypeStruct((T, KVW), BF16)],
        compiler_params=_params())(P, P, P, q_g, k_g, rope_c, rope_s)


def _qk_prep_bwd(name, dqr, dkr, P, q_g, k_g, rope_c, rope_s, cfg):
    T = P.shape[0]
    ATT, KVW = cfg['ATT'], cfg['KVW']

    def body(dq_ref, dk_ref, q_ref, k_ref, qg_ref, kg_ref, c_ref, s_ref, dqo_ref, dko_ref, dqg_ref, dkg_ref):
        i = pl.program_id(0)
        cc, ss = c_ref[...], s_ref[...]

        @pl.when(i == 0)
        def _():
            dqg_ref[...] = jnp.zeros_like(dqg_ref)
            dkg_ref[...] = jnp.zeros_like(dkg_ref)

        def head(x, g, dout):
            dy = dout * cc + _rot(dout * ss)
            r = _rstd(x)
            xhat = x * r
            dxh = dy * g
            dx = r * (dxh - xhat * jnp.mean(dxh * xhat, axis=-1, keepdims=True))
            return dx, jnp.sum(dy * xhat, axis=0, keepdims=True)

        dg = jnp.zeros((1, HEAD_DIM), F32)
        for hh in range(ATT // HEAD_DIM):
            sl = slice(hh * HEAD_DIM, (hh + 1) * HEAD_DIM)
            dx, d = head(q_ref[:, sl], qg_ref[...], dq_ref[:, sl] * ATT_SCALE)
            dqo_ref[:, sl] = dx.astype(dqo_ref.dtype)
            dg += d
        dqg_ref[0:1, :] += dg
        dg = jnp.zeros((1, HEAD_DIM), F32)
        for hh in range(KVW // HEAD_DIM):
            sl = slice(hh * HEAD_DIM, (hh + 1) * HEAD_DIM)
            dx, d = head(k_ref[:, sl], kg_ref[...], dk_ref[:, sl] * (1.0 / LOG2_E))
            dko_ref[:, sl] = dx.astype(dko_ref.dtype)
            dg += d
        dkg_ref[0:1, :] += dg

    kb = ATT // KVW
    gain = pl.BlockSpec((1, HEAD_DIM), lambda i: (0, 0))
    dgain = pl.BlockSpec((8, HEAD_DIM), lambda i: (0, 0))
    tab = pl.BlockSpec((ROW_TILE, HEAD_DIM), lambda i: (i, 0))
    qs = pl.BlockSpec((ROW_TILE, ATT), lambda i: (i, 0))
    ks = pl.BlockSpec((ROW_TILE, KVW), lambda i: (i, 0))
    return pl.pallas_call(
        body, name=name, grid=(T // ROW_TILE,),
        in_specs=[qs, ks, qs, pl.BlockSpec((ROW_TILE, KVW), lambda i: (i, kb)), gain, gain, tab, tab],
        out_specs=[qs, ks, dgain, dgain],
        out_shape=[jax.ShapeDtypeStruct((T, ATT), BF16), jax.ShapeDtypeStruct((T, KVW), BF16),
                   jax.ShapeDtypeStruct((8, HEAD_DIM), F32), jax.ShapeDtypeStruct((8, HEAD_DIM), F32)],
        compiler_params=_params())(dqr, dkr, P, P, q_g, k_g, rope_c, rope_s)


def _att_specs(T, G):
    qs = pl.BlockSpec((ROW_TILE, G * HEAD_DIM), lambda h, i: (i, h))
    kvs = pl.BlockSpec((T, HEAD_DIM), lambda h, i: (0, h))
    return qs, kvs


def _attn_dense_fwd(name, q, k, v, cfg):
    T, G, Lc = q.shape[0], cfg['G'], cfg['Lc']

    def body(q_ref, k_ref, v_ref, o_ref, lse_ref):
        def attend(rows):
            kk, vv = k_ref[0:rows, :], v_ref[0:rows, :]
            for g in range(G):
                sl = slice(g * HEAD_DIM, (g + 1) * HEAD_DIM)
                s = _nt(q_ref[:, sl], kk)
                m = jnp.max(s, axis=1, keepdims=True)
                p = jnp.exp2(s - m)
                l = jnp.sum(p, axis=1, keepdims=True)
                o_ref[:, sl] = _nn(_bf(p), vv) / l
                lse_ref[:, sl] = jnp.broadcast_to(m + jnp.log2(l), (ROW_TILE, HEAD_DIM))

        @pl.when(pl.program_id(1) == 0)
        def _():
            attend(Lc)

        @pl.when(pl.program_id(1) > 0)
        def _():
            attend(T)

    qs, kvs = _att_specs(T, G)
    return pl.pallas_call(
        body, name=name, grid=(cfg['NKV'], T // ROW_TILE), in_specs=[qs, kvs, kvs], out_specs=[qs, qs],
        out_shape=[jax.ShapeDtypeStruct(q.shape, F32), jax.ShapeDtypeStruct(q.shape, F32)],
        compiler_params=_params())(q, k, v)


def _attn_dense_bwd(name, q, k, v, o, lse, dmix, cfg):
    T, G, Lc = q.shape[0], cfg['G'], cfg['Lc']

    def body(q_ref, k_ref, v_ref, o_ref, lse_ref, do_ref, dq_ref, dk_ref, dv_ref):
        i = pl.program_id(1)

        @pl.when(i == 0)
        def _():
            dk_ref[...] = jnp.zeros_like(dk_ref)
            dv_ref[...] = jnp.zeros_like(dv_ref)

        def attend(rows):
            kk, vv = k_ref[0:rows, :], v_ref[0:rows, :]
            for g in range(G):
                sl = slice(g * HEAD_DIM, (g + 1) * HEAD_DIM)
                qg, do = q_ref[:, sl], do_ref[:, sl]
                delta = jnp.sum(do * o_ref[:, sl], axis=1, keepdims=True)
                p = jnp.exp2(_nt(qg, kk) - lse_ref[:, g * HEAD_DIM:g * HEAD_DIM + 1])
                dob = _bf(do)
                dv_ref[0:rows, :] += _tn(_bf(p), dob)
                ds = _bf(p * (_nt(dob, vv) - delta))
                dq_ref[:, sl] = _nn(ds, kk)
                dk_ref[0:rows, :] += _tn(ds, qg)

        @pl.when(i == 0)
        def _():
            attend(Lc)

        @pl.when(i > 0)
        def _():
            attend(T)

    qs, kvs = _att_specs(T, G)
    return pl.pallas_call(
        body, name=name, grid=(cfg['NKV'], T // ROW_TILE), in_specs=[qs, kvs, kvs, qs, qs, qs], out_specs=[qs, kvs, kvs],
        out_shape=[jax.ShapeDtypeStruct(q.shape, F32), jax.ShapeDtypeStruct(k.shape, F32), jax.ShapeDtypeStruct(k.shape, F32)],
        compiler_params=_params())(q, k, v, o, lse, dmix)


def _band(i, T, Lc):
    start = pl.multiple_of(jnp.clip(WINDOW + (i - 1) * ROW_TILE, 0, T - BAND), WINDOW)
    qpos = (i - 1) * ROW_TILE + lax.broadcasted_iota(jnp.int32, (ROW_TILE, 1), 0)
    kpos = start - Lc + lax.broadcasted_iota(jnp.int32, (1, BAND), 1)
    ok = (jnp.abs(kpos - qpos) <= WINDOW) & (kpos >= 0) & (i > 0)
    return start, jnp.where(ok, 0.0, NEG_INF).astype(F32)


def _attn_win_fwd(name, q, k, v, sink, cfg):
    T, G, Lc = q.shape[0], cfg['G'], cfg['Lc']

    def body(sink_ref, q_ref, k_ref, v_ref, o_ref, lse_ref):
        h, i = pl.program_id(0), pl.program_id(1)
        start, bias = _band(i, T, Lc)
        kc, vc = k_ref[0:Lc, :], v_ref[0:Lc, :]
        kb, vb = k_ref[pl.ds(start, BAND), :], v_ref[pl.ds(start, BAND), :]
        for g in range(G):
            sl = slice(g * HEAD_DIM, (g + 1) * HEAD_DIM)
            qg = q_ref[:, sl]
            sk = sink_ref[h * G + g] * LOG2_E
            sc = _nt(qg, kc)
            sb = _nt(qg, kb) + bias
            m = jnp.maximum(jnp.maximum(jnp.max(sc, axis=1, keepdims=True), jnp.max(sb, axis=1, keepdims=True)), sk)
            pc, pb = jnp.exp2(sc - m), jnp.exp2(sb - m)
            l = jnp.sum(pc, axis=1, keepdims=True) + jnp.sum(pb, axis=1, keepdims=True) + jnp.exp2(sk - m)
            o_ref[:, sl] = (_nn(_bf(pc), vc) + _nn(_bf(pb), vb)) / l
            lse_ref[:, sl] = jnp.broadcast_to(m + jnp.log2(l), (ROW_TILE, HEAD_DIM))

    qs, kvs = _att_specs(T, G)
    return pl.pallas_call(
        body, name=name, grid=(cfg['NKV'], T // ROW_TILE),
        in_specs=[pl.BlockSpec(memory_space=pltpu.SMEM), qs, kvs, kvs], out_specs=[qs, qs],
        out_shape=[jax.ShapeDtypeStruct(q.shape, F32), jax.ShapeDtypeStruct(q.shape, F32)],
        compiler_params=_params())(sink, q, k, v)


def _attn_win_bwd(name, q, k, v, o, lse, dmix, sink, cfg):
    T, G, Lc = q.shape[0], cfg['G'], cfg['Lc']

    def body(sink_ref, q_ref, k_ref, v_ref, o_ref, lse_ref, do_ref, dq_ref, dk_ref, dv_ref, dsink_ref):
        h, i = pl.program_id(0), pl.program_id(1)
        start, bias = _band(i, T, Lc)
        kc, vc = k_ref[0:Lc, :], v_ref[0:Lc, :]
        kb, vb = k_ref[pl.ds(start, BAND), :], v_ref[pl.ds(start, BAND), :]

        @pl.when(i == 0)
        def _():
            dk_ref[...] = jnp.zeros_like(dk_ref)
            dv_ref[...] = jnp.zeros_like(dv_ref)
            dsink_ref[...] = jnp.zeros_like(dsink_ref)

        for g in range(G):
            sl = slice(g * HEAD_DIM, (g + 1) * HEAD_DIM)
            qg, do = q_ref[:, sl], do_ref[:, sl]
            lse = lse_ref[:, g * HEAD_DIM:g * HEAD_DIM + 1]
            delta = jnp.sum(do * o_ref[:, sl], axis=1, keepdims=True)
            pc = jnp.exp2(_nt(qg, kc) - lse)
            pb = jnp.exp2(_nt(qg, kb) + bias - lse)
            ps = jnp.exp2(sink_ref[h * G + g] * LOG2_E - lse)
            dob = _bf(do)
            dv_ref[0:Lc, :] += _tn(_bf(pc), dob)
            dv_ref[pl.ds(start, BAND), :] += _tn(_bf(pb), dob)
            dsc = _bf(pc * (_nt(dob, vc) - delta))
            dsb = _bf(pb * (_nt(dob, vb) - delta))
            dq_ref[:, sl] = _nn(dsc, kc) + _nn(dsb, kb)
            dk_ref[0:Lc, :] += _tn(dsc, qg)
            dk_ref[pl.ds(start, BAND), :] += _tn(dsb, qg)
            dsk = jnp.where(i > 0, -jnp.sum(ps * delta, axis=0, keepdims=True), 0.0)
            dsink_ref[:, sl] += jnp.broadcast_to(dsk, (8, HEAD_DIM))

    qs, kvs = _att_specs(T, G)
    return pl.pallas_call(
        body, name=name, grid=(cfg['NKV'], T // ROW_TILE),
        in_specs=[pl.BlockSpec(memory_space=pltpu.SMEM), qs, kvs, kvs, qs, qs, qs],
        out_specs=[qs, kvs, kvs, pl.BlockSpec((None, 8, G * HEAD_DIM), lambda h, i: (h, 0, 0))],
        out_shape=[jax.ShapeDtypeStruct(q.shape, F32), jax.ShapeDtypeStruct(k.shape, F32), jax.ShapeDtypeStruct(k.shape, F32),
                   jax.ShapeDtypeStruct((cfg['NKV'], 8, G * HEAD_DIM), F32)],
        compiler_params=_params())(sink, q, k, v, o, lse, dmix)


def _seq_pos(T, Lc):
    row = lax.broadcasted_iota(jnp.int32, (T, 1), 0)
    return jnp.where(row < Lc, row, row - Lc), jnp.where(row < Lc, Lc, T - Lc)


def _fw(x, k, pos, seglen):
    return jnp.where(pos + k < seglen, pltpu.roll(x, x.shape[0] - k, 0), 0.0)


def _bw(x, k, pos):
    return jnp.where(pos - k >= 0, pltpu.roll(x, k, 0), 0.0)


def _conv_fwd(name, P, conv_w8, cfg):
    T, Lc = P.shape[0], cfg['Lc']
    cb = (cfg['ATT'] + 2 * cfg['KVW']) // HEAD_DIM
    na = AUX_WIDTH // HEAD_DIM

    def body(gb_ref, gc_ref, u_ref, w_ref, o_ref):
        pos, seglen = _seq_pos(T, Lc)
        z = gc_ref[...] * u_ref[...]
        conv = w_ref[0:1, :] * _bw(z, 1, pos) + w_ref[1:2, :] * z + w_ref[2:3, :] * _fw(z, 1, pos, seglen)
        o_ref[...] = gb_ref[...] * conv

    col = lambda off: pl.BlockSpec((T, HEAD_DIM), lambda c: (0, cb + off + c))
    return pl.pallas_call(
        body, name=name, grid=(na,),
        in_specs=[col(0), col(na), col(2 * na), pl.BlockSpec((8, HEAD_DIM), lambda c: (0, c))],
        out_specs=pl.BlockSpec((T, HEAD_DIM), lambda c: (0, c)),
        out_shape=jax.ShapeDtypeStruct((T, AUX_WIDTH), F32), compiler_params=_params())(P, P, P, conv_w8)


def _conv_bwd(name, P, conv_w8, dmix, cfg):
    T, Lc = P.shape[0], cfg['Lc']
    cb = (cfg['ATT'] + 2 * cfg['KVW']) // HEAD_DIM
    ob = cfg['ATT'] // HEAD_DIM
    na = AUX_WIDTH // HEAD_DIM

    def body(gb_ref, gc_ref, u_ref, w_ref, do_ref, dgb_ref, dgc_ref, du_ref, dw_ref):
        pos, seglen = _seq_pos(T, Lc)
        gc, u, do = gc_ref[...], u_ref[...], do_ref[...]
        z = gc * u
        zm, zp = _bw(z, 1, pos), _fw(z, 1, pos, seglen)
        w0, w1, w2 = w_ref[0:1, :], w_ref[1:2, :], w_ref[2:3, :]
        dgb_ref[...] = (do * (w0 * zm + w1 * z + w2 * zp)).astype(dgb_ref.dtype)
        dc = do * gb_ref[...]
        dz = w0 * _fw(dc, 1, pos, seglen) + w1 * dc + w2 * _bw(dc, 1, pos)
        dgc_ref[...] = (dz * u).astype(dgc_ref.dtype)
        du_ref[...] = (dz * gc).astype(du_ref.dtype)
        dw_ref[...] = jnp.zeros_like(dw_ref)
        dw_ref[0:1, :] = jnp.sum(dc * zm, axis=0, keepdims=True)
        dw_ref[1:2, :] = jnp.sum(dc * z, axis=0, keepdims=True)
        dw_ref[2:3, :] = jnp.sum(dc * zp, axis=0, keepdims=True)

    col = lambda off: pl.BlockSpec((T, HEAD_DIM), lambda c: (0, cb + off + c))
    wspec = pl.BlockSpec((8, HEAD_DIM), lambda c: (0, c))
    ocol = lambda off: pl.BlockSpec((T, HEAD_DIM), lambda c: (0, off + c))
    return pl.pallas_call(
        body, name=name, grid=(na,),
        in_specs=[col(0), col(na), col(2 * na), wspec, ocol(ob)],
        out_specs=[ocol(0), ocol(0), ocol(0), wspec],
        out_shape=[jax.ShapeDtypeStruct((T, AUX_WIDTH), BF16)] * 3 + [jax.ShapeDtypeStruct((8, AUX_WIDTH), F32)],
        compiler_params=_params())(P, P, P, conv_w8, dmix)


def _window_sums(x, half, pos, seglen):
    fwd, bwd = x, x
    s = 1
    while s < half:
        fwd = fwd + _fw(fwd, s, pos, seglen)
        bwd = bwd + _bw(bwd, s, pos)
        s *= 2
    return fwd, bwd


def _pooled(u, half, pos, seglen):
    fwd, bwd = _window_sums(u, half, pos, seglen)
    cnt = (jnp.minimum(pos + half, seglen) - jnp.maximum(pos - half, 0)).astype(F32)
    return (fwd + _bw(bwd, 1, pos)) / cnt - u, cnt


def _pool_fwd(name, P, pool_w, pool_scale, cfg):
    T, Lc = P.shape[0], cfg['Lc']
    cb = (cfg['ATT'] + 2 * cfg['KVW']) // HEAD_DIM

    def body(u_ref, w_ref, s_ref, o_ref):
        g = pl.program_id(0)
        pos, seglen = _seq_pos(T, Lc)
        for k, half in enumerate(POOL_HALF):
            @pl.when(g == k)
            def _(half=half):
                pooled, _ = _pooled(u_ref[...], half, pos, seglen)
                o_ref[...] = _nn(_bf(pooled), _bf(w_ref[...])) * s_ref[...]

    return pl.pallas_call(
        body, name=name, grid=(AUX_GROUPS,),
        in_specs=[pl.BlockSpec((T, HEAD_DIM), lambda g: (0, cb + g)), pl.BlockSpec((None, HEAD_DIM, HEAD_DIM), lambda g: (g, 0, 0)),
                  pl.BlockSpec((1, HEAD_DIM), lambda g: (0, g))],
        out_specs=pl.BlockSpec((T, HEAD_DIM), lambda g: (0, g)),
        out_shape=jax.ShapeDtypeStruct((T, AUX_WIDTH), F32), compiler_params=_params())(P, pool_w, pool_scale)


def _pool_bwd(name, P, pool_w, pool_scale, dmix, cfg):
    T, Lc = P.shape[0], cfg['Lc']
    cb = (cfg['ATT'] + 2 * cfg['KVW']) // HEAD_DIM
    ob = cfg['ATT'] // HEAD_DIM

    def body(u_ref, w_ref, s_ref, do_ref, du_ref, dw_ref, ds_ref):
        g = pl.program_id(0)
        pos, seglen = _seq_pos(T, Lc)
        for k, half in enumerate(POOL_HALF):
            @pl.when(g == k)
            def _(half=half):
                do = do_ref[...]
                pooled, cnt = _pooled(u_ref[...], half, pos, seglen)
                wb = _bf(w_ref[...])
                mixed = _nn(_bf(pooled), wb)
                ds_ref[...] = jnp.broadcast_to(jnp.sum(do * mixed, axis=0, keepdims=True), ds_ref.shape)
                dmixed = _bf(do * s_ref[...])
                dw_ref[...] = _tn(_bf(pooled), dmixed)
                dpooled = _nt(dmixed, wb)
                e = dpooled / cnt
                fwd, bwd = _window_sums(e, half, pos, seglen)
                adj = fwd + _fw(e, half, pos, seglen) + _bw(bwd, 1, pos) - _bw(e, half, pos)
                du_ref[...] = (adj - dpooled).astype(du_ref.dtype)

    wspec = pl.BlockSpec((None, HEAD_DIM, HEAD_DIM), lambda g: (g, 0, 0))
    return pl.pallas_call(
        body, name=name, grid=(AUX_GROUPS,),
        in_specs=[pl.BlockSpec((T, HEAD_DIM), lambda g: (0, cb + g)), wspec, pl.BlockSpec((1, HEAD_DIM), lambda g: (0, g)),
                  pl.BlockSpec((T, HEAD_DIM), lambda g: (0, ob + g))],
        out_specs=[pl.BlockSpec((T, HEAD_DIM), lambda g: (0, g)), wspec, pl.BlockSpec((8, HEAD_DIM), lambda g: (0, g))],
        out_shape=[jax.ShapeDtypeStruct((T, AUX_WIDTH), BF16), jax.ShapeDtypeStruct(pool_w.shape, F32),
                   jax.ShapeDtypeStruct((8, AUX_WIDTH), F32)],
        compiler_params=_params())(P, pool_w, pool_scale, dmix)


def _ffn_up(name, hn, wg3, wu3, dep=None):
    T, D = hn.shape
    J, k, _ = wg3.shape
    tm = _row_tile(T, 1088)
    dep_ins, dep_specs = _dep(dep, 2)

    def body(x_ref, wg_ref, wu_ref, *rest):
        g_ref, u_ref, a_ref = rest[len(dep_ins):]
        x = x_ref[...]
        g, u = _nt(x, wg_ref[...]), _nt(x, wu_ref[...])
        g_ref[...] = g.astype(g_ref.dtype)
        u_ref[...] = u.astype(u_ref.dtype)
        a_ref[...] = (g * jax.nn.sigmoid(g) * u).astype(a_ref.dtype)

    wspec = pl.BlockSpec((None, k, D), lambda j, i: (j, 0, 0))
    ospec = pl.BlockSpec((None, tm, k), lambda j, i: (j, i, 0))
    return pl.pallas_call(
        body, name=name, grid=(J, T // tm), in_specs=[pl.BlockSpec((tm, D), lambda j, i: (i, 0)), wspec, wspec] + dep_specs,
        out_specs=[ospec, ospec, ospec],
        out_shape=[jax.ShapeDtypeStruct((J, T, k), BF16)] * 3,
        compiler_params=_params())(hn, wg3, wu3, *dep_ins)


def _ffn_dact(name, dF, wd3, G, U):
    T, D = dF.shape
    J, k, _ = wd3.shape
    tm = _row_tile(T, 1088)

    def body(df_ref, wd_ref, g_ref, u_ref, dg_ref, du_ref):
        da = _nt(df_ref[...], wd_ref[...])
        g = g_ref[...].astype(F32)
        sig = jax.nn.sigmoid(g)
        du_ref[...] = (da * g * sig).astype(du_ref.dtype)
        dg_ref[...] = (da * u_ref[...].astype(F32) * (sig * (1 + g * (1 - sig)))).astype(dg_ref.dtype)

    aspec = pl.BlockSpec((None, tm, k), lambda j, i: (j, i, 0))
    return pl.pallas_call(
        body, name=name, grid=(J, T // tm),
        in_specs=[pl.BlockSpec((tm, D), lambda j, i: (i, 0)), pl.BlockSpec((None, k, D), lambda j, i: (j, 0, 0)), aspec, aspec],
        out_specs=[aspec, aspec],
        out_shape=[jax.ShapeDtypeStruct((J, T, k), BF16), jax.ShapeDtypeStruct((J, T, k), BF16)],
        compiler_params=_params())(dF, wd3, G, U)


def _loss_head(name, h, g, target, cfg):
    T, D = h.shape

    def body(h_ref, g_ref, t_ref, dh_ref, loss_ref, dg_ref):
        i = pl.program_id(0)

        @pl.when(i == 0)
        def _():
            dh_ref[...] = jnp.zeros_like(dh_ref)
            loss_ref[...] = jnp.zeros_like(loss_ref)
            dg_ref[...] = jnp.zeros_like(dg_ref)

        @pl.when(i > 0)
        def _():
            x = h_ref[...]
            r = _rstd(x)
            xhat = x * r
            gg = g_ref[...]
            err = xhat * gg - t_ref[...]
            loss_ref[...] += 0.5 * jnp.sum(jnp.sum(err * err, axis=1, keepdims=True) / D, axis=0, keepdims=True)
            dy = err / D
            dg_ref[0:1, :] += jnp.sum(dy * xhat, axis=0, keepdims=True)
            dxh = dy * gg
            dh_ref[...] = r * (dxh - xhat * jnp.mean(dxh * xhat, axis=-1, keepdims=True))

    row = pl.BlockSpec((ROW_TILE, D), lambda i: (i, 0))
    return pl.pallas_call(
        body, name=name, grid=(T // ROW_TILE,),
        in_specs=[row, pl.BlockSpec((1, D), lambda i: (0, 0)), pl.BlockSpec((ROW_TILE, D), lambda i: (jnp.maximum(i - 1, 0), 0))],
        out_specs=[row, pl.BlockSpec((8, 128), lambda i: (0, 0)), pl.BlockSpec((8, D), lambda i: (0, 0))],
        out_shape=[jax.ShapeDtypeStruct((T, D), F32), jax.ShapeDtypeStruct((8, 128), F32), jax.ShapeDtypeStruct((8, D), F32)],
        compiler_params=_params())(h, g, target)


def _adamw(name, parts, w, m, v, dep=None):
    R, C = w.shape
    n_parts = parts.shape[0]
    tr = _row_tile(R, max(16, (1 << 18) // C)) if R % 16 == 0 else R
    bc1 = 1.0 - ADAM_B1 ** ADAM_STEP
    bc2 = 1.0 - ADAM_B2 ** ADAM_STEP
    dep_ins, dep_specs = _dep(dep, 1)

    def body(p_ref, w_ref, m_ref, v_ref, *rest):
        g_ref, d_ref, nm_ref, nv_ref = rest[len(dep_ins):]
        g = p_ref[0].astype(F32)
        for k in range(1, n_parts):
            g = g + p_ref[k].astype(F32)
        nm = ADAM_B1 * m_ref[...] + (1.0 - ADAM_B1) * g
        nv = ADAM_B2 * v_ref[...] + (1.0 - ADAM_B2) * (g * g)
        g_ref[...] = g
        nm_ref[...] = nm
        nv_ref[...] = nv
        d_ref[...] = -ADAM_LR * ((nm / bc1) / (jnp.sqrt(nv / bc2) + ADAM_EPS) + ADAM_WD * w_ref[...])

    blk = pl.BlockSpec((tr, C), lambda i: (i, 0))
    return pl.pallas_call(
        body, name=name, grid=(R // tr,), in_specs=[pl.BlockSpec((n_parts, tr, C), lambda i: (0, i, 0)), blk, blk, blk] + dep_specs,
        out_specs=[blk] * 4, out_shape=[jax.ShapeDtypeStruct((R, C), F32)] * 4, compiler_params=_params())(parts, w, m, v, *dep_ins)


class _WeightStream:
    def __init__(self, cast):
        self.cast, self.handles = cast, {}

    @staticmethod
    def _tag(l, group):
        return ("ffn" if group is FFN_WEIGHTS else group[0]) + str(l)

    def start(self, l, group, after=None):
        self.handles[l, group], token = _gather_start(f"gather_{self._tag(l, group)}_start", [self.cast(l, n) for n in group], after)
        return token

    def relay(self, l, group, after):
        self.handles[l, group], token = _gather_relay(f"gather_{self._tag(l, group)}_relay", self.handles[l, group], after)
        return token

    def get(self, l, group, after):
        got = dict(zip(group, _gather_wait(f"gather_{self._tag(l, group)}_wait", self.handles[l, group], after)))
        if 'w_out' in got:
            rows, cols = got['w_out'].shape[1:]
            got['w_out'] = got['w_out'].reshape(N_DEV * rows, cols)
        return got


def _layer_fwd(l, h, p, stream, mod, rope, conv_w8, cfg):
    nm = f"l{l}_"
    mod = mod + stream.relay(l, IN_WEIGHT, h)[0, 0]
    xn = _norm_mod(nm + "norm1", h, p['norm1_g'], mod, 0)
    W = stream.get(l, IN_WEIGHT, xn)
    token = None
    if l == 0:
        token = stream.start(0, OUT_WEIGHT, after=W['w_in']) + stream.start(0, FFN_WEIGHTS, after=W['w_in'])
    P = _mm_cols(nm + "w_in", xn, W['w_in'], dep=token)
    qr, kr, vb = _qk_prep(nm + "qk_prep", P, p['q_norm_g'], p['k_norm_g'], rope[0], rope[1], cfg)
    if l == 0:
        o, lse = _attn_dense_fwd(nm + "attn", qr, kr, vb, cfg)
        aux = _conv_fwd(nm + "conv", P, conv_w8, cfg)
    else:
        o, lse = _attn_win_fwd(nm + "attn", qr, kr, vb, p['sink'], cfg)
        aux = _pool_fwd(nm + "pool", P, p['pool_w'], p['pool_scale'], cfg)
    mix = jnp.concatenate([o, aux], axis=1).astype(BF16)
    W.update(stream.get(l, OUT_WEIGHT, stream.relay(l, OUT_WEIGHT, mix)))
    y = _mm_plain(nm + "w_out", mix, W['w_out'], False, dep=stream.relay(l, FFN_WEIGHTS, o))
    h2 = _gate_res(nm + "res1", h, y, mod, 2)
    hn = _norm_mod(nm + "norm2", h2, p['norm2_g'], mod, 1)
    W.update(stream.get(l, FFN_WEIGHTS, hn))
    token = stream.start(1, IN_WEIGHT, after=W['w_down']) if l == 0 else None
    G, U, A = _ffn_up(nm + "ffn_up", hn, W['w_gate'], W['w_up'], dep=token)
    if l == 0:
        token = stream.start(1, OUT_WEIGHT, after=A) + stream.start(1, FFN_WEIGHTS, after=A)
    F = _mm_shards_nn(nm + "w_down", A, W['w_down'], dep=token)
    h3 = _gate_res(nm + "res2", h2, F, mod, 5)
    saved = dict(h=h, xn=xn, P=P, qr=qr, kr=kr, vb=vb, o=o, lse=lse, mix=mix, y=y, h2=h2, hn=hn, G=G, U=U, A=A, F=F)
    return h3, saved, W


def _layer_bwd(l, dh3, s, p, W, mod, rope, conv_w8, cfg):
    nm = f"l{l}_bwd_"
    J = N_DEV
    dF, dmod = _gate_bwd(nm + "res2", dh3, s['F'], mod, 5)
    dG, dU = _ffn_dact(nm + "ffn_act", dF, W['w_down'], s['G'], s['U'])
    big = {'w_down': _wgrad_down(nm + "dw_down", s['A'], dF),
           'w_gate': _wgrad_down(nm + "dw_gate", dG, s['hn']),
           'w_up': _wgrad_down(nm + "dw_up", dU, s['hn'])}
    handles = {}
    handles['ffn'], token = _exchange_start(f"scatter_ffn{l}_start", [big[n] for n in FFN_WEIGHTS], True)
    mod = mod + token[0, 0]
    dhn = _mm_shards_nn2(nm + "dhn", dG, W['w_gate'], dU, W['w_up'])
    dh2, dm, dg2 = _norm_mod_bwd(nm + "norm2", dhn, s['h2'], p['norm2_g'], mod, 1, dh3)
    dmod += dm
    dY, dm = _gate_bwd(nm + "res1", dh2, s['y'], mod, 2)
    dmod += dm
    dwo = _wgrad_rows(nm + "dw_out", s['mix'], dY)
    handles['w_out'], token = _exchange_start(f"scatter_w_out{l}_start", [dwo.reshape((J, dwo.shape[0] // J, dwo.shape[1]))], True)
    dmix = _mm_plain(nm + "dmix", dY, W['w_out'], True, dep=token)
    small = {'norm2_g': dg2[0]}
    if l == 0:
        dqr, dkr, dv = _attn_dense_bwd(nm + "attn", s['qr'], s['kr'], s['vb'], s['o'], s['lse'], dmix, cfg)
        *daux, dcw = _conv_bwd(nm + "conv", s['P'], conv_w8, dmix, cfg)
        small['conv_w'] = dcw[0:3]
    else:
        dqr, dkr, dv, dsk = _attn_win_bwd(nm + "attn", s['qr'], s['kr'], s['vb'], s['o'], s['lse'], dmix, p['sink'], cfg)
        du, dpw, dps = _pool_bwd(nm + "pool", s['P'], p['pool_w'], p['pool_scale'], dmix, cfg)
        daux = [du]
        small.update(sink=dsk[:, 0, ::HEAD_DIM].reshape(-1), pool_w=dpw, pool_scale=dps[0])
    dq, dk, dqg, dkg = _qk_prep_bwd(nm + "qk_prep", dqr, dkr, s['P'], p['q_norm_g'], p['k_norm_g'], rope[0], rope[1], cfg)
    small.update(q_norm_g=dqg[0], k_norm_g=dkg[0])
    dP = jnp.concatenate([dq, dk, dv.astype(BF16), *daux], axis=1)
    handles['w_in'], token = _exchange_start(f"scatter_w_in{l}_start", [_wgrad_cols(nm + "dw_in", s['xn'], dP, J)], True)
    mod = mod + token[0, 0]
    dxn = _mm_cols_nt(nm + "dxn", dP, W['w_in'])
    dh, dm, dg1 = _norm_mod_bwd(nm + "norm1", dxn, s['h'], p['norm1_g'], mod, 0, dh2)
    dmod += dm
    small['norm1_g'] = dg1[0]
    return dh, dmod, small, handles, token


def _rope_tables(S, Lc):
    half = HEAD_DIM // 4
    pos = np.arange(S)
    inv = ROPE_THETA ** (-np.arange(0, 2 * half, 2, dtype=np.float32) / (2 * half))
    inv = jnp.asarray(inv, F32)
    ang_r = jnp.asarray(pos // GRID_W, F32)[:, None] * inv
    ang_c = jnp.asarray(pos % GRID_W, F32)[:, None] * inv
    cos = jnp.concatenate([jnp.cos(ang_r)] * 2 + [jnp.cos(ang_c)] * 2, axis=1)
    sin = jnp.concatenate([-jnp.sin(ang_r), jnp.sin(ang_r), -jnp.sin(ang_c), jnp.sin(ang_c)], axis=1)
    return (jnp.concatenate([jnp.ones((Lc, HEAD_DIM), F32), cos], axis=0),
            jnp.concatenate([jnp.zeros((Lc, HEAD_DIM), F32), sin], axis=0))


def _pad_rows(a, rows):
    return jnp.concatenate([a, jnp.zeros((rows - a.shape[0],) + a.shape[1:], a.dtype)], axis=0)


def _flat128(a, nlead):
    lead = a.shape[:nlead]
    f = a.reshape(lead + (-1,))
    pad = (-f.shape[-1]) % 128
    if pad:
        f = jnp.concatenate([f, jnp.zeros(lead + (pad,), f.dtype)], axis=-1)
    return f.reshape(lead + (-1, 128))


def _pack(named, nlead=0):
    rows, layout, at = [], {}, 0
    for name, a in named:
        f = _flat128(a, nlead)
        n = f.shape[-2]
        pad = (-n) % 8
        if pad:
            f = jnp.concatenate([f, jnp.zeros(f.shape[:-2] + (pad, 128), f.dtype)], axis=-2)
        layout[name] = (at, n, a.shape[nlead:])
        rows.append(f)
        at += n + pad
    return jnp.concatenate(rows, axis=-2), layout


def _unpack(arr, layout, name):
    at, n, shape = layout[name]
    return arr[..., at:at + n, :].reshape(arr.shape[:-2] + (-1,))[..., :math.prod(shape)].reshape(arr.shape[:-2] + tuple(shape))


def kernel(*args):
    A = dict(zip(INPUT_NAMES, args, strict=True))
    x, ctx = A['x'][0], A['ctx'][0]
    S, D = x.shape
    Lc = ctx.shape[0]
    T = Lc + S
    ATT = D - AUX_WIDTH
    KVW = (A['l1_w_in'].shape[1] * N_DEV - ATT - AUX_WIDTH) // 2
    cfg = dict(ATT=ATT, KVW=KVW, NKV=KVW // HEAD_DIM, G=ATT // KVW, Lc=Lc)
    assert Lc == ROW_TILE and S % ROW_TILE == 0 and T >= BAND and S % GRID_W == 0
    cw = A['l0_conv_w'].shape[1]
    me = 4 * lax.axis_index("x") + 2 * lax.axis_index("y") + lax.axis_index("c")

    def layer_params(l):
        pre = f"l{l}_"
        return {k[len(pre):]: (v.reshape(1, -1) if v.ndim == 1 and k != 'l1_sink' else v) for k, v in A.items() if k.startswith(pre)}

    params = [layer_params(0), layer_params(1)]

    def cast(l, n):
        w = A[f'l{l}_{n}']
        return (w.T if n in TRANSPOSED else w).astype(BF16)

    stream = _WeightStream(cast)
    token = stream.start(0, IN_WEIGHT)

    big_names = [n for n in WEIGHT_NAMES if n[3:] in BIG_WEIGHTS + ('w_mod',)]
    rest = [n for n in WEIGHT_NAMES if n not in big_names]
    wp, layw = _pack([(n, A[n]) for n in rest])
    mp, _ = _pack([(n, A['m_' + n]) for n in rest])
    vp, _ = _pack([(n, A['v_' + n]) for n in rest])

    sc_own = jax.nn.silu(A['c']) + token[0, 0]
    first, lay0 = _pack([('sc', sc_own), ('conv_w', A['l0_conv_w'])])
    first_all = _exchange("gather_cond", [first], False)[0]
    sc_all = _unpack(first_all, lay0, 'sc')[:, 0]
    conv_w = _unpack(first_all, lay0, 'conv_w').transpose(1, 0, 2).reshape(3, N_DEV * cw)
    conv_w8 = _pad_rows(conv_w, 8)
    sc_ctx = jax.nn.silu(A['c_ctx'])
    s16 = _pad_rows(jnp.concatenate([sc_all, sc_ctx[None]], axis=0), 16)

    nmod = A['l0_w_mod'].shape[1]
    modp = jnp.concatenate([_mm_plain(f"l{l}_mod", s16, A[f'l{l}_w_mod'], False) for l in range(2)], axis=1)
    modp_all = _exchange("gather_mod", [modp], False)[0]
    mods = []
    for l in range(2):
        full = modp_all[:, :, l * nmod:(l + 1) * nmod].transpose(1, 0, 2).reshape(16, N_MOD * D) + A[f'l{l}_b_mod'][None]
        both = jnp.stack([full[8], lax.dynamic_index_in_dim(full, me, 0, keepdims=False)]).reshape(2, N_MOD, D)
        mods.append(jnp.concatenate([both, jnp.zeros((2, 8 - N_MOD, D), F32)], axis=1))

    rope = _rope_tables(S, Lc)
    h = jnp.concatenate([ctx, x], axis=0)
    h, wp, mp, vp = lax.optimization_barrier((h, wp, mp, vp))
    saved, W = [], []
    for l in range(2):
        h, s, Wl = _layer_fwd(l, h, params[l], stream, mods[l], rope, conv_w8, cfg)
        saved.append(s)
        W.append(Wl)

    dh, loss_blk, dgf = _loss_head("loss_head", h, A['final_norm_g'].reshape(1, -1), A['loss_target'][0], cfg)
    loss = lax.psum(loss_blk[0, 0], ("x", "y", "c"))

    grads, small, dmods, scatters = {}, {'final_norm_g': dgf[0]}, [None, None], [None, None]
    token = jnp.zeros((8, 128), F32)
    for l in (1, 0):
        dh, dmods[l], sm, scatters[l], token = _layer_bwd(l, dh, saved[l], params[l], W[l], mods[l] + token[0, 0], rope, conv_w8, cfg)
        small.update({f'l{l}_{k}': v for k, v in sm.items()})
    grad_x = dh[Lc:][None]

    def landed(l, key, after):
        group = FFN_WEIGHTS if key == 'ffn' else (key,)
        for n, parts in zip(group, _exchange_wait(f"scatter_{key}{l}_wait", scatters[l][key], after)):
            shape = A[f'l{l}_{n}'].shape
            grads[f'l{l}_{n}'] = (parts.reshape((N_DEV,) + (shape[::-1] if n in TRANSPOSED else shape)), None)

    out = {}

    def adam(n, dep=None):
        w, m, v = A[n], A['m_' + n], A['v_' + n]
        if n[3:] in TRANSPOSED:
            out[n] = tuple(r.T for r in _adamw("adamw_" + n, grads[n][0], w.T, m.T, v.T, dep))
        else:
            out[n] = _adamw("adamw_" + n, grads[n][0], w, m, v, dep)
        return out[n][1]

    last = dh
    for l in (1, 0):
        for key in ('ffn', 'w_out') + (('w_in',) if l == 1 else ()):
            landed(l, key, last)
            for n in (FFN_WEIGHTS if key == 'ffn' else (key,)):
                last = adam(f'l{l}_{n}', token)

    small_names = [n for n in WEIGHT_NAMES if n in small]
    pieces = [(n, small[n]) for n in small_names]
    for l in range(2):
        pieces += [(f'dmod{l}', dmods[l][1, :N_MOD]), (f'dcmod{l}', dmods[l][0, :N_MOD])]
    second, lay1 = _pack(pieces)
    second_all = _exchange("gather_small", [second], False, after=last)[0]

    dsc_part = jnp.zeros((16, D), F32)
    for l in range(2):
        dm16 = _pad_rows(jnp.concatenate([_unpack(second_all, lay1, f'dmod{l}').reshape(N_DEV, N_MOD * D),
                                          jnp.sum(_unpack(second_all, lay1, f'dcmod{l}'), axis=0).reshape(1, N_MOD * D)], axis=0), 16)
        mine = lax.dynamic_slice_in_dim(dm16, me * nmod, nmod, axis=1)
        tk = _col_tile(D, 512)
        gw = _mm_tn(f"l{l}_dw_mod", s16, mine, pl.BlockSpec((16, tk), lambda kb, t: (0, kb)), pl.BlockSpec((16, nmod), lambda kb, t: (0, 0)),
                    jax.ShapeDtypeStruct((D, nmod), F32), pl.BlockSpec((tk, nmod), lambda kb, t: (kb, 0)), (tk, nmod), (D // tk, 1))
        grads[f'l{l}_w_mod'] = (gw[None], None)
        dsc_part += _mm_plain(f"l{l}_dsc", mine, A[f'l{l}_w_mod'], True)
        dmod_dev = _unpack(second_all, lay1, f'dmod{l}') + _unpack(second_all, lay1, f'dcmod{l}')
        grads[f'l{l}_b_mod'] = (dmod_dev.reshape(N_DEV, N_MOD * D), None)
    dsig = jax.nn.sigmoid(A['c_ctx'])
    dsilu = dsig * (1 + A['c_ctx'] * (1 - dsig))
    third_all = _exchange("gather_dsc", [dsc_part[8:9]], False)[0]
    grads['c_ctx'] = (third_all[:, 0] * dsilu[None], None)
    for n in small_names:
        g8 = _unpack(second_all, lay1, n)
        if n == 'l0_conv_w':
            g8 = lax.dynamic_slice_in_dim(g8, me * cw, cw, axis=2)
        grads[n] = (g8, None)

    adam('l0_w_mod')
    last = adam('l1_w_mod')
    gp, _ = _pack([(n, grads[n][0]) for n in rest], nlead=1)
    res = _adamw("adamw_small", gp, wp, mp, vp)
    for n in rest:
        out[n] = tuple(_unpack(r, layw, n) for r in res)
    landed(0, 'w_in', last)
    adam('l0_w_in')

    outs = [loss, grad_x]
    for k in range(4):
        outs += [out[n][k] for n in WEIGHT_NAMES]
    return tuple(outs)
```

```python
import functools
import math

import numpy as np
import jax
import jax.numpy as jnp
from jax import lax
from jax.experimental import pallas as pl
from jax.experimental.pallas import tpu as pltpu

F32 = jnp.float32
BF16 = jnp.bfloat16
HEAD_DIM = 128
AUX_WIDTH = 512
AUX_GROUPS = 4
POOL_HALF = (1, 2, 4, 8)
WINDOW = 128
GRID_W = 64
ROPE_THETA = 10000.0
EPS = 1e-6
NEG_INF = -1e30
ATT_SCALE = HEAD_DIM ** -0.5
LOG2_E = math.log2(math.e)
Q_SCALE = ATT_SCALE * LOG2_E
N_MOD = 6
N_DEV = 8
ROW_TILE = 256
BAND = ROW_TILE + 2 * WINDOW
ADAM_LR, ADAM_B1, ADAM_B2, ADAM_EPS, ADAM_WD, ADAM_STEP = 0.001, 0.9, 0.999, 1e-08, 0.01, 10
VMEM_LIMIT_MB = 56
MESH = pl.DeviceIdType.MESH

WEIGHT_NAMES = ['c_ctx', 'l0_norm1_g', 'l0_w_mod', 'l0_b_mod', 'l0_w_in', 'l0_q_norm_g', 'l0_k_norm_g', 'l0_conv_w', 'l0_w_out', 'l0_norm2_g', 'l0_w_gate', 'l0_w_up', 'l0_w_down', 'l1_norm1_g', 'l1_w_mod', 'l1_b_mod', 'l1_w_in', 'l1_q_norm_g', 'l1_k_norm_g', 'l1_sink', 'l1_pool_w', 'l1_pool_scale', 'l1_w_out', 'l1_norm2_g', 'l1_w_gate', 'l1_w_up', 'l1_w_down', 'final_norm_g']
INPUT_NAMES = (['x', 'c', 'ctx'] + WEIGHT_NAMES + ['loss_target'] + ['m_' + n for n in WEIGHT_NAMES]
               + ['v_' + n for n in WEIGHT_NAMES])
IN_WEIGHT = ('w_in',)
OUT_WEIGHT = ('w_out',)
MIXER_WEIGHTS = OUT_WEIGHT + IN_WEIGHT
FFN_WEIGHTS = ('w_down', 'w_gate', 'w_up')
TRANSPOSED = ('w_gate', 'w_up')
BIG_WEIGHTS = MIXER_WEIGHTS + FFN_WEIGHTS


def _params(vmem_mb=VMEM_LIMIT_MB):
    return pltpu.CompilerParams(vmem_limit_bytes=vmem_mb << 20)


def _row_tile(n, cap):
    best = None
    for t in range(16, min(n, cap) + 1, 16):
        if n % t == 0:
            best = t
    assert best is not None, (n, cap)
    return best


def _col_tile(n, cap):
    best = n
    for t in range(128, min(n, cap) + 1, 128):
        if n % t == 0:
            best = t
    return best if best <= cap or n % 128 else n


def _dot(a, b, ca, cb):
    return lax.dot_general(a, b, (((ca,), (cb,)), ((), ())), preferred_element_type=F32)


def _nn(a, b):
    return _dot(a, b, 1, 0)


def _nt(a, b):
    return _dot(a, b, 1, 1)


def _tn(a, b):
    return _dot(a, b, 0, 0)


def _bf(x):
    return x.astype(BF16)


def _exchange(name, arrs, scatter, after=None):
    n = len(arrs)
    extra = [] if after is None else [after]
    if scatter:
        out_shape = [jax.ShapeDtypeStruct(a.shape, a.dtype) for a in arrs]
    else:
        out_shape = [jax.ShapeDtypeStruct((N_DEV,) + a.shape, a.dtype) for a in arrs]

    def body(*refs):
        ins, outs = refs[:n], refs[n + len(extra):2 * n + len(extra)]
        send_sems, recv_sems, local_sems = refs[2 * n + len(extra):]
        x, y, c = lax.axis_index("x"), lax.axis_index("y"), lax.axis_index("c")
        me = 4 * x + 2 * y + c
        local, remote = [], []
        for a in range(n):
            own = ins[a].at[me] if scatter else ins[a]
            cp = pltpu.make_async_copy(own, outs[a].at[me], local_sems.at[a])
            cp.start()
            local.append(cp)
            for r in range(1, N_DEV):
                px = 1 - x if r & 4 else x
                py = 1 - y if r & 2 else y
                pc = 1 - c if r & 1 else c
                src = ins[a].at[4 * px + 2 * py + pc] if scatter else ins[a]
                cp = pltpu.make_async_remote_copy(
                    src_ref=src, dst_ref=outs[a].at[me], send_sem=send_sems.at[a, r - 1],
                    recv_sem=recv_sems.at[a, r - 1], device_id=(px, py, pc), device_id_type=MESH)
                cp.start()
                remote.append(cp)
        for cp in remote:
            cp.wait()
        for cp in local:
            cp.wait()

    any_spec = pl.BlockSpec(memory_space=pl.ANY)
    return pl.pallas_call(
        body, name=name, out_shape=out_shape,
        in_specs=[any_spec] * (n + len(extra)), out_specs=[any_spec] * n,
        scratch_shapes=[pltpu.SemaphoreType.DMA((n, N_DEV - 1)), pltpu.SemaphoreType.DMA((n, N_DEV - 1)),
                        pltpu.SemaphoreType.DMA((n,))],
    )(*arrs, *extra)


HBM_SPEC = pl.BlockSpec(memory_space=pltpu.HBM)
SEM_SPEC = pl.BlockSpec(memory_space=pltpu.SEMAPHORE)
EFFECT = pltpu.SideEffectType.DATAFLOW_SIDE_EFFECTING


def _split_copies(srcs, lands, send_sems, recv_sems, local_sems, scatter):
    x, y, c = lax.axis_index("x"), lax.axis_index("y"), lax.axis_index("c")
    me = 4 * x + 2 * y + c
    local, remote = [], []
    for a in range(len(srcs)):
        own = srcs[a].at[me] if scatter else srcs[a]
        local.append(pltpu.make_async_copy(own, lands[a].at[me], local_sems.at[a]))
        for r in range(1, N_DEV):
            px = 1 - x if r & 4 else x
            py = 1 - y if r & 2 else y
            pc = 1 - c if r & 1 else c
            src = srcs[a].at[4 * px + 2 * py + pc] if scatter else srcs[a]
            remote.append(pltpu.make_async_remote_copy(
                src_ref=src, dst_ref=lands[a].at[me], send_sem=send_sems.at[a * (N_DEV - 1) + r - 1],
                recv_sem=recv_sems.at[a * (N_DEV - 1) + r - 1], device_id=(px, py, pc), device_id_type=MESH))
    return local, remote


def _exchange_start(name, arrs, scatter, after=None):
    n = len(arrs)
    extra = [] if after is None else [after]
    shapes = [a.shape if scatter else (N_DEV,) + a.shape for a in arrs]
    lands = [pltpu.with_memory_space_constraint(lax.empty(s, a.dtype), pltpu.HBM) for s, a in zip(shapes, arrs)]
    srcs = [pltpu.with_memory_space_constraint(a, pltpu.HBM) for a in arrs]

    def body(*refs):
        src_refs, land_refs = refs[:n], refs[n:2 * n]
        send_sems, recv_sems, local_sems = refs[2 * n + len(extra):2 * n + len(extra) + 3]
        token = refs[-1]
        local, remote = _split_copies(src_refs, land_refs, send_sems, recv_sems, local_sems, scatter)
        for cp in local + remote:
            cp.start()
        token[...] = jnp.zeros_like(token)

    res = pl.pallas_call(
        body, name=name,
        out_shape=[pltpu.SemaphoreType.DMA((n * (N_DEV - 1),)), pltpu.SemaphoreType.DMA((n * (N_DEV - 1),)), pltpu.SemaphoreType.DMA((n,))]
        + [pltpu.HBM(a.shape, a.dtype) for a in arrs] + [pltpu.HBM(s, a.dtype) for s, a in zip(shapes, arrs)]
        + [jax.ShapeDtypeStruct((8, 128), F32)],
        in_specs=[HBM_SPEC] * (2 * n) + [pl.BlockSpec(memory_space=pl.ANY)] * len(extra),
        out_specs=[SEM_SPEC] * 3 + [HBM_SPEC] * (2 * n) + [pl.BlockSpec(memory_space=pltpu.VMEM)],
        input_output_aliases={i: 3 + i for i in range(2 * n)},
        compiler_params=pltpu.CompilerParams(has_side_effects=EFFECT),
    )(*srcs, *lands, *extra)
    return (scatter, res[:3], res[3:3 + n], res[3 + n:3 + 2 * n]), res[-1]


def _exchange_wait(name, handle, after):
    scatter, sems, srcs, lands = handle
    n = len(srcs)

    def body(*refs):
        src_refs, land_refs = refs[:n], refs[n:2 * n]
        send_sems, recv_sems, local_sems = refs[2 * n:2 * n + 3]
        local, remote = _split_copies(src_refs, land_refs, send_sems, recv_sems, local_sems, scatter)
        for cp in remote:
            cp.wait_send()
            cp.wait_recv()
        for cp in local:
            cp.wait()

    res = pl.pallas_call(
        body, name=name,
        out_shape=[pltpu.HBM(a.shape, a.dtype) for a in srcs] + [pltpu.HBM(a.shape, a.dtype) for a in lands],
        in_specs=[HBM_SPEC] * (2 * n) + [SEM_SPEC] * 3 + [pl.BlockSpec(memory_space=pl.ANY)], out_specs=[HBM_SPEC] * (2 * n),
        input_output_aliases={i: i for i in range(2 * n)},
        compiler_params=pltpu.CompilerParams(has_side_effects=EFFECT),
    )(*srcs, *lands, *sems, after)
    return list(res[n:])


FIRST_COPIES = 4
RELAY_COPIES = 3


def _gather_copies(srcs, lands, sems):
    send_sems, recv_sems, local_sems = sems[:3]
    x, y, c = lax.axis_index("x"), lax.axis_index("y"), lax.axis_index("c")
    me = 4 * x + 2 * y + c
    chips = [(1 - x, y), (x, 1 - y), (1 - x, 1 - y)]
    local, first, relay = [], [], []
    for a in range(len(srcs)):
        local.append(pltpu.make_async_copy(srcs[a], lands[a].at[me], local_sems.at[a]))
        targets = [(x, y, 1 - c)] + [(px, py, c) for px, py in chips]
        first.append([pltpu.make_async_remote_copy(
            src_ref=srcs[a], dst_ref=lands[a].at[me], send_sem=send_sems.at[FIRST_COPIES * a + k],
            recv_sem=recv_sems.at[FIRST_COPIES * a + k], device_id=t, device_id_type=MESH) for k, t in enumerate(targets)])
        if len(sems) > 3:
            rsend, rrecv = sems[3:]
            slots = [lands[a].at[4 * px + 2 * py + c] for px, py in chips]
            relay.append([pltpu.make_async_remote_copy(
                src_ref=slot, dst_ref=slot, send_sem=rsend.at[RELAY_COPIES * a + j], recv_sem=rrecv.at[RELAY_COPIES * a + j],
                device_id=(x, y, 1 - c), device_id_type=MESH) for j, slot in enumerate(slots)])
    return local, first, relay


def _gather_start(name, arrs, after=None):
    n = len(arrs)
    extra = [] if after is None else [after]
    lands = [pltpu.with_memory_space_constraint(lax.empty((N_DEV,) + a.shape, a.dtype), pltpu.HBM) for a in arrs]
    srcs = [pltpu.with_memory_space_constraint(a, pltpu.HBM) for a in arrs]

    def body(*refs):
        at = 2 * n + len(extra)
        local, first, _ = _gather_copies(refs[:n], refs[n:2 * n], refs[at:at + 3])
        for cp in local + [cp for cps in first for cp in cps]:
            cp.start()
        refs[-1][...] = jnp.zeros_like(refs[-1])

    res = pl.pallas_call(
        body, name=name,
        out_shape=[pltpu.SemaphoreType.DMA((FIRST_COPIES * n,)), pltpu.SemaphoreType.DMA((FIRST_COPIES * n,)), pltpu.SemaphoreType.DMA((n,))]
        + [pltpu.HBM(a.shape, a.dtype) for a in arrs] + [pltpu.HBM((N_DEV,) + a.shape, a.dtype) for a in arrs]
        + [jax.ShapeDtypeStruct((8, 128), F32)],
        in_specs=[HBM_SPEC] * (2 * n) + [pl.BlockSpec(memory_space=pl.ANY)] * len(extra),
        out_specs=[SEM_SPEC] * 3 + [HBM_SPEC] * (2 * n) + [pl.BlockSpec(memory_space=pltpu.VMEM)],
        input_output_aliases={i: 3 + i for i in range(2 * n)},
        compiler_params=pltpu.CompilerParams(has_side_effects=EFFECT),
    )(*srcs, *lands, *extra)
    return (list(res[:3]), list(res[3:3 + n]), list(res[3 + n:3 + 2 * n])), res[-1]


def _gather_relay(name, handle, after):
    sems, srcs, lands = handle
    n = len(srcs)

    def body(*refs):
        in_sems = refs[2 * n:2 * n + 3]
        out_sems = refs[2 * n + 4 + 2 * n:2 * n + 4 + 2 * n + 2]
        _, first, relay = _gather_copies(refs[:n], refs[n:2 * n], list(in_sems) + list(out_sems))
        for a in range(n):
            for j in range(RELAY_COPIES):
                first[a][1 + j].wait_recv()
                relay[a][j].start()
        refs[-1][...] = jnp.zeros_like(refs[-1])

    res = pl.pallas_call(
        body, name=name,
        out_shape=[pltpu.HBM(a.shape, a.dtype) for a in srcs] + [pltpu.HBM(a.shape, a.dtype) for a in lands]
        + [pltpu.SemaphoreType.DMA((RELAY_COPIES * n,)), pltpu.SemaphoreType.DMA((RELAY_COPIES * n,)), jax.ShapeDtypeStruct((8, 128), F32)],
        in_specs=[HBM_SPEC] * (2 * n) + [SEM_SPEC] * 3 + [pl.BlockSpec(memory_space=pl.ANY)],
        out_specs=[HBM_SPEC] * (2 * n) + [SEM_SPEC] * 2 + [pl.BlockSpec(memory_space=pltpu.VMEM)],
        input_output_aliases={i: i for i in range(2 * n)},
        compiler_params=pltpu.CompilerParams(has_side_effects=EFFECT),
    )(*srcs, *lands, *sems, after)
    return (sems + list(res[2 * n:2 * n + 2]), list(res[:n]), list(res[n:2 * n])), res[-1]


def _gather_wait(name, handle, after):
    sems, srcs, lands = handle
    n = len(srcs)

    def body(*refs):
        local, first, relay = _gather_copies(refs[:n], refs[n:2 * n], refs[2 * n:2 * n + 5])
        for a in range(n):
            for cp in first[a]:
                cp.wait_send()
            first[a][0].wait_recv()
            for cp in relay[a]:
                cp.wait_send()
                cp.wait_recv()
            local[a].wait()

    res = pl.pallas_call(
        body, name=name,
        out_shape=[pltpu.HBM(a.shape, a.dtype) for a in srcs] + [pltpu.HBM(a.shape, a.dtype) for a in lands],
        in_specs=[HBM_SPEC] * (2 * n) + [SEM_SPEC] * 5 + [pl.BlockSpec(memory_space=pl.ANY)], out_specs=[HBM_SPEC] * (2 * n),
        input_output_aliases={i: i for i in range(2 * n)},
        compiler_params=pltpu.CompilerParams(has_side_effects=EFFECT),
    )(*srcs, *lands, *sems, after)
    return list(res[n:])


def _dep(dep, grid_rank):
    if dep is None:
        return [], []
    return [dep], [pl.BlockSpec((8, 128), (lambda i, j: (0, 0)) if grid_rank == 2 else (lambda i: (0, 0)))]


def _mm_step(name, fn, ins, in_specs, out_shape, out_spec, grid, dep=None, res=None):
    n = len(ins)
    dep_ins, dep_specs = _dep(dep, len(grid))
    res_ins, res_specs, out_shapes, out_specs = [], [], out_shape, out_spec
    if res is not None:
        h, mod, row_idx, lc = res
        tm, tn = out_spec.block_shape
        res_ins = [h, mod]
        res_specs = [pl.BlockSpec((tm, tn), lambda j, i: (i, j)), pl.BlockSpec((2, 8, tn), lambda j, i: (0, 0, j))]
        out_shapes, out_specs = [out_shape, jax.ShapeDtypeStruct(h.shape, h.dtype)], [out_spec, res_specs[0]]

    def body(*refs):
        outs = refs[n + len(res_ins) + len(dep_ins):]
        acc = fn(*refs[:n])
        outs[0][...] = acc.astype(outs[0].dtype)
        if res is not None:
            h_ref, mod_ref = refs[n:n + 2]
            row = pl.program_id(1) * tm + lax.broadcasted_iota(jnp.int32, (tm, 1), 0)
            gate = jnp.where(row < lc, mod_ref[0, row_idx:row_idx + 1, :], mod_ref[1, row_idx:row_idx + 1, :])
            outs[1][...] = h_ref[...] + gate * acc

    return pl.pallas_call(body, name=name, grid=grid, in_specs=list(in_specs) + res_specs + dep_specs, out_specs=out_specs,
                          out_shape=out_shapes, compiler_params=_params())(*ins, *res_ins, *dep_ins)


def _mm_tn(name, a, b, a_spec, b_spec, out_shape, out_spec, acc_shape, grid):
    nk = grid[-1]
    kax = len(grid) - 1

    def body(a_ref, b_ref, o_ref, acc_ref):
        k = pl.program_id(kax)

        @pl.when(k == 0)
        def _():
            acc_ref[...] = jnp.zeros_like(acc_ref)

        acc_ref[...] += _tn(_bf(a_ref[...]), _bf(b_ref[...]))

        @pl.when(k == nk - 1)
        def _():
            o_ref[...] = acc_ref[...].astype(o_ref.dtype)

    return pl.pallas_call(body, name=name, grid=grid, in_specs=[a_spec, b_spec], out_specs=out_spec,
                          out_shape=out_shape, scratch_shapes=[pltpu.VMEM(acc_shape, F32)],
                          compiler_params=_params())(a, b)


def _mm_cols(name, a, w3, out_dtype=F32, dep=None):
    M, K = a.shape
    J, _, n = w3.shape
    tm = _row_tile(M, 1088)
    return _mm_step(
        name, lambda a_ref, w_ref: _nn(_bf(a_ref[...]), w_ref[...]), [a, w3],
        [pl.BlockSpec((tm, K), lambda j, i: (i, 0)), pl.BlockSpec((None, K, n), lambda j, i: (j, 0, 0))],
        jax.ShapeDtypeStruct((M, J * n), out_dtype), pl.BlockSpec((tm, n), lambda j, i: (i, j)), (J, M // tm), dep)


def _mm_plain(name, a, b, transpose_b, out_dtype=F32, tn=512, dep=None, res=None):
    M, K = a.shape
    N = b.shape[0] if transpose_b else b.shape[1]
    tm = _row_tile(M, 1088)
    tn = _col_tile(N, tn)
    if transpose_b:
        b_spec = pl.BlockSpec((tn, K), lambda j, i: (j, 0))
        fn = lambda a_ref, b_ref: _nt(_bf(a_ref[...]), _bf(b_ref[...]))
    else:
        b_spec = pl.BlockSpec((K, tn), lambda j, i: (0, j))
        fn = lambda a_ref, b_ref: _nn(_bf(a_ref[...]), _bf(b_ref[...]))
    return _mm_step(name, fn, [a, b], [pl.BlockSpec((tm, K), lambda j, i: (i, 0)), b_spec],
                    jax.ShapeDtypeStruct((M, N), out_dtype), pl.BlockSpec((tm, tn), lambda j, i: (i, j)),
                    (N // tn, M // tm), dep, res)


def _mm_shards_nn(name, a3, w3, tn=512, dep=None, res=None):
    J, M, k = a3.shape
    N = w3.shape[2]
    tm = _row_tile(M, 544)
    tn = _col_tile(N, tn)

    def fn(a_ref, w_ref):
        acc = _nn(a_ref[0], w_ref[0])
        for j in range(1, J):
            acc += _nn(a_ref[j], w_ref[j])
        return acc

    return _mm_step(name, fn, [a3, w3],
                    [pl.BlockSpec((J, tm, k), lambda jn, i: (0, i, 0)), pl.BlockSpec((J, k, tn), lambda jn, i: (0, 0, jn))],
                    jax.ShapeDtypeStruct((M, N), F32), pl.BlockSpec((tm, tn), lambda jn, i: (i, jn)), (N // tn, M // tm), dep, res)


def _mm_shards_nn2(name, a3, w3a, b3, w3b, tn=512):
    J, M, k = a3.shape
    N = w3a.shape[2]
    tm = _row_tile(M, 544)
    tn = _col_tile(N, tn)

    def fn(a_ref, wa_ref, b_ref, wb_ref):
        acc = _nn(a_ref[0], wa_ref[0]) + _nn(b_ref[0], wb_ref[0])
        for j in range(1, J):
            acc += _nn(a_ref[j], wa_ref[j]) + _nn(b_ref[j], wb_ref[j])
        return acc

    act = pl.BlockSpec((J, tm, k), lambda jn, i: (0, i, 0))
    wsp = pl.BlockSpec((J, k, tn), lambda jn, i: (0, 0, jn))
    return _mm_step(name, fn, [a3, w3a, b3, w3b], [act, wsp, act, wsp],
                    jax.ShapeDtypeStruct((M, N), F32), pl.BlockSpec((tm, tn), lambda jn, i: (i, jn)), (N // tn, M // tm))


def _mm_cols_nt(name, a, w3, tn=512):
    M = a.shape[0]
    J, N, n = w3.shape
    tm = _row_tile(M, 544)
    tn = _col_tile(N, tn)

    def fn(a_ref, w_ref):
        acc = _nt(a_ref[:, 0:n], w_ref[0])
        for j in range(1, J):
            acc += _nt(a_ref[:, j * n:(j + 1) * n], w_ref[j])
        return acc

    return _mm_step(name, fn, [a, w3],
                    [pl.BlockSpec((tm, J * n), lambda jn, i: (i, 0)), pl.BlockSpec((J, tn, n), lambda jn, i: (0, jn, 0))],
                    jax.ShapeDtypeStruct((M, N), F32), pl.BlockSpec((tm, tn), lambda jn, i: (i, jn)), (N // tn, M // tm))


def _wgrad_cols(name, a, b, J):
    T, K = a.shape
    n = b.shape[1] // J
    tt = _row_tile(T, 1088)
    return _mm_tn(name, a, b, pl.BlockSpec((tt, K), lambda j, t: (t, 0)), pl.BlockSpec((tt, n), lambda j, t: (t, j)),
                  jax.ShapeDtypeStruct((J, K, n), BF16), pl.BlockSpec((None, K, n), lambda j, t: (j, 0, 0)), (K, n), (J, T // tt))


def _wgrad_rows(name, a, b, tk=512):
    T, K = a.shape
    N = b.shape[1]
    tt = _row_tile(T, 1088)
    tk = _col_tile(K, tk)
    return _mm_tn(name, a, b, pl.BlockSpec((tt, tk), lambda kb, t: (t, kb)), pl.BlockSpec((tt, N), lambda kb, t: (t, 0)),
                  jax.ShapeDtypeStruct((K, N), BF16), pl.BlockSpec((tk, N), lambda kb, t: (kb, 0)), (tk, N), (K // tk, T // tt))


def _wgrad_up(name, a, b3):
    T, K = a.shape
    J, _, k = b3.shape
    tt = _row_tile(T, 1088)
    return _mm_tn(name, a, b3, pl.BlockSpec((tt, K), lambda j, t: (t, 0)), pl.BlockSpec((None, tt, k), lambda j, t: (j, t, 0)),
                  jax.ShapeDtypeStruct((J, K, k), BF16), pl.BlockSpec((None, K, k), lambda j, t: (j, 0, 0)), (K, k), (J, T // tt))


def _wgrad_down(name, a3, b):
    J, T, k = a3.shape
    N = b.shape[1]
    tt = _row_tile(T, 1088)
    return _mm_tn(name, a3, b, pl.BlockSpec((None, tt, k), lambda j, t: (j, t, 0)), pl.BlockSpec((tt, N), lambda j, t: (t, 0)),
                  jax.ShapeDtypeStruct((J, k, N), BF16), pl.BlockSpec((None, k, N), lambda j, t: (j, 0, 0)), (k, N), (J, T // tt))


def _seg(i):
    return jnp.minimum(i, 1)


def _rstd(x):
    return lax.rsqrt(jnp.mean(x * x, axis=-1, keepdims=True) + EPS)


def _norm_mod(name, h, g, mod, which):
    T, D = h.shape

    def body(h_ref, g_ref, mod_ref, o_ref):
        x = h_ref[...]
        n = x * _rstd(x) * g_ref[...]
        shift = mod_ref[3 * which:3 * which + 1, :]
        scale = mod_ref[3 * which + 1:3 * which + 2, :]
        o_ref[...] = (n * (1 + scale) + shift).astype(o_ref.dtype)

    row = pl.BlockSpec((ROW_TILE, D), lambda i: (i, 0))
    return pl.pallas_call(
        body, name=name, grid=(T // ROW_TILE,),
        in_specs=[row, pl.BlockSpec((1, D), lambda i: (0, 0)), pl.BlockSpec((None, 8, D), lambda i: (_seg(i), 0, 0))],
        out_specs=row, out_shape=jax.ShapeDtypeStruct((T, D), BF16), compiler_params=_params())(h, g, mod)


def _norm_mod_bwd(name, dxn, h, g, mod, which, dres):
    T, D = h.shape

    def body(dxn_ref, h_ref, g_ref, mod_ref, dres_ref, dh_ref, dmod_ref, dg_ref):
        i = pl.program_id(0)
        x = h_ref[...]
        r = _rstd(x)
        xhat = x * r
        g = g_ref[...]
        n = xhat * g
        scale = mod_ref[3 * which + 1:3 * which + 2, :]
        dxn = dxn_ref[...]
        dn = dxn * (1 + scale)
        dxh = dn * g
        dh_ref[...] = dres_ref[...] + r * (dxh - xhat * jnp.mean(dxh * xhat, axis=-1, keepdims=True))

        @pl.when(i <= 1)
        def _():
            dmod_ref[...] = jnp.zeros_like(dmod_ref)

        @pl.when(i == 0)
        def _():
            dg_ref[...] = jnp.zeros_like(dg_ref)

        dmod_ref[3 * which:3 * which + 1, :] += jnp.sum(dxn, axis=0, keepdims=True)
        dmod_ref[3 * which + 1:3 * which + 2, :] += jnp.sum(dxn * n, axis=0, keepdims=True)
        dg_ref[0:1, :] += jnp.sum(dn * xhat, axis=0, keepdims=True)

    row = pl.BlockSpec((ROW_TILE, D), lambda i: (i, 0))
    modspec = pl.BlockSpec((None, 8, D), lambda i: (_seg(i), 0, 0))
    return pl.pallas_call(
        body, name=name, grid=(T // ROW_TILE,),
        in_specs=[row, row, pl.BlockSpec((1, D), lambda i: (0, 0)), modspec, row],
        out_specs=[row, modspec, pl.BlockSpec((8, D), lambda i: (0, 0))],
        out_shape=[jax.ShapeDtypeStruct((T, D), F32), jax.ShapeDtypeStruct((2, 8, D), F32), jax.ShapeDtypeStruct((8, D), F32)],
        compiler_params=_params())(dxn, h, g, mod, dres)


def _gate_bwd(name, dh, y, mod, row_idx):
    T, D = dh.shape

    def body(dh_ref, y_ref, mod_ref, dy_ref, dmod_ref):
        i = pl.program_id(0)
        dh = dh_ref[...]
        dy_ref[...] = (dh * mod_ref[row_idx:row_idx + 1, :]).astype(dy_ref.dtype)

        @pl.when(i <= 1)
        def _():
            dmod_ref[...] = jnp.zeros_like(dmod_ref)

        dmod_ref[row_idx:row_idx + 1, :] += jnp.sum(dh * y_ref[...], axis=0, keepdims=True)

    row = pl.BlockSpec((ROW_TILE, D), lambda i: (i, 0))
    modspec = pl.BlockSpec((None, 8, D), lambda i: (_seg(i), 0, 0))
    return pl.pallas_call(
        body, name=name, grid=(T // ROW_TILE,), in_specs=[row, row, modspec], out_specs=[row, modspec],
        out_shape=[jax.ShapeDtypeStruct((T, D), BF16), jax.ShapeDtypeStruct((2, 8, D), F32)],
        compiler_params=_params())(dh, y, mod)


def _rot(y):
    lane = lax.broadcasted_iota(jnp.int32, y.shape, 1)
    return jnp.where((lane & 32) == 0, pltpu.roll(y, 96, 1), pltpu.roll(y, 32, 1))


def _qk_prep(name, P, q_g, k_g, rope_c, rope_s, cfg):
    T = P.shape[0]
    ATT, KVW = cfg['ATT'], cfg['KVW']

    def body(q_ref, k_ref, v_ref, qg_ref, kg_ref, c_ref, s_ref, qo_ref, ko_ref, vo_ref):
        cc, ss = c_ref[...], s_ref[...]

        def head(x, g):
            y = x * _rstd(x) * g
            return y * cc + _rot(y) * ss

        for hh in range(ATT // HEAD_DIM):
            sl = slice(hh * HEAD_DIM, (hh + 1) * HEAD_DIM)
            qo_ref[:, sl] = (head(q_ref[:, sl], qg_ref[...]) * Q_SCALE).astype(qo_ref.dtype)
        for hh in range(KVW // HEAD_DIM):
            sl = slice(hh * HEAD_DIM, (hh + 1) * HEAD_DIM)
            ko_ref[:, sl] = head(k_ref[:, sl], kg_ref[...]).astype(ko_ref.dtype)
        vo_ref[...] = v_ref[...].astype(vo_ref.dtype)

    kb = ATT // KVW
    gain = pl.BlockSpec((1, HEAD_DIM), lambda i: (0, 0))
    tab = pl.BlockSpec((ROW_TILE, HEAD_DIM), lambda i: (i, 0))
    qs = pl.BlockSpec((ROW_TILE, ATT), lambda i: (i, 0))
    ks = pl.BlockSpec((ROW_TILE, KVW), lambda i: (i, 0))
    return pl.pallas_call(
        body, name=name, grid=(T // ROW_TILE,),
        in_specs=[qs, pl.BlockSpec((ROW_TILE, KVW), lambda i: (i, kb)), pl.BlockSpec((ROW_TILE, KVW), lambda i: (i, kb + 1)),
                  gain, gain, tab, tab],
        out_specs=[qs, ks, ks],
        out_shape=[jax.ShapeDtypeStruct((T, ATT), BF16), jax.ShapeDtypeStruct((T, KVW), BF16), jax.ShapeDtypeStruct((T, KVW), BF16)],
        compiler_params=_params())(P, P, P, q_g, k_g, rope_c, rope_s)


def _qk_prep_bwd(name, dqr, dkr, P, q_g, k_g, rope_c, rope_s, cfg):
    T = P.shape[0]
    ATT, KVW = cfg['ATT'], cfg['KVW']

    def body(dq_ref, dk_ref, q_ref, k_ref, qg_ref, kg_ref, c_ref, s_ref, dqo_ref, dko_ref, dqg_ref, dkg_ref):
        i = pl.program_id(0)
        cc, ss = c_ref[...], s_ref[...]

        @pl.when(i == 0)
        def _():
            dqg_ref[...] = jnp.zeros_like(dqg_ref)
            dkg_ref[...] = jnp.zeros_like(dkg_ref)

        def head(x, g, dout):
            dy = dout * cc + _rot(dout * ss)
            r = _rstd(x)
            xhat = x * r
            dxh = dy * g
            dx = r * (dxh - xhat * jnp.mean(dxh * xhat, axis=-1, keepdims=True))
            return dx, jnp.sum(dy * xhat, axis=0, keepdims=True)

        dg = jnp.zeros((1, HEAD_DIM), F32)
        for hh in range(ATT // HEAD_DIM):
            sl = slice(hh * HEAD_DIM, (hh + 1) * HEAD_DIM)
            dx, d = head(q_ref[:, sl], qg_ref[...], dq_ref[:, sl] * ATT_SCALE)
            dqo_ref[:, sl] = dx.astype(dqo_ref.dtype)
            dg += d
        dqg_ref[0:1, :] += dg
        dg = jnp.zeros((1, HEAD_DIM), F32)
        for hh in range(KVW // HEAD_DIM):
            sl = slice(hh * HEAD_DIM, (hh + 1) * HEAD_DIM)
            dx, d = head(k_ref[:, sl], kg_ref[...], dk_ref[:, sl] * (1.0 / LOG2_E))
            dko_ref[:, sl] = dx.astype(dko_ref.dtype)
            dg += d
        dkg_ref[0:1, :] += dg

    kb = ATT // KVW
    gain = pl.BlockSpec((1, HEAD_DIM), lambda i: (0, 0))
    dgain = pl.BlockSpec((8, HEAD_DIM), lambda i: (0, 0))
    tab = pl.BlockSpec((ROW_TILE, HEAD_DIM), lambda i: (i, 0))
    qs = pl.BlockSpec((ROW_TILE, ATT), lambda i: (i, 0))
    ks = pl.BlockSpec((ROW_TILE, KVW), lambda i: (i, 0))
    return pl.pallas_call(
        body, name=name, grid=(T // ROW_TILE,),
        in_specs=[qs, ks, qs, pl.BlockSpec((ROW_TILE, KVW), lambda i: (i, kb)), gain, gain, tab, tab],
        out_specs=[qs, ks, dgain, dgain],
        out_shape=[jax.ShapeDtypeStruct((T, ATT), BF16), jax.ShapeDtypeStruct((T, KVW), BF16),
                   jax.ShapeDtypeStruct((8, HEAD_DIM), F32), jax.ShapeDtypeStruct((8, HEAD_DIM), F32)],
        compiler_params=_params())(dqr, dkr, P, P, q_g, k_g, rope_c, rope_s)


def _att_specs(T, G):
    qs = pl.BlockSpec((ROW_TILE, G * HEAD_DIM), lambda h, i: (i, h))
    kvs = pl.BlockSpec((T, HEAD_DIM), lambda h, i: (0, h))
    return qs, kvs


def _attn_dense_fwd(name, q, k, v, cfg):
    T, G, Lc = q.shape[0], cfg['G'], cfg['Lc']

    def body(q_ref, k_ref, v_ref, o_ref, lse_ref):
        def attend(rows):
            kk, vv = k_ref[0:rows, :], v_ref[0:rows, :]
            for g in range(G):
                sl = slice(g * HEAD_DIM, (g + 1) * HEAD_DIM)
                s = _nt(q_ref[:, sl], kk)
                m = jnp.max(s, axis=1, keepdims=True)
                p = jnp.exp2(s - m)
                l = jnp.sum(p, axis=1, keepdims=True)
                o_ref[:, sl] = _nn(_bf(p), vv) / l
                lse_ref[:, sl] = jnp.broadcast_to(m + jnp.log2(l), (ROW_TILE, HEAD_DIM))

        @pl.when(pl.program_id(1) == 0)
        def _():
            attend(Lc)

        @pl.when(pl.program_id(1) > 0)
        def _():
            attend(T)

    qs, kvs = _att_specs(T, G)
    return pl.pallas_call(
        body, name=name, grid=(cfg['NKV'], T // ROW_TILE), in_specs=[qs, kvs, kvs], out_specs=[qs, qs],
        out_shape=[jax.ShapeDtypeStruct(q.shape, F32), jax.ShapeDtypeStruct(q.shape, F32)],
        compiler_params=_params())(q, k, v)


def _attn_dense_bwd(name, q, k, v, o, lse, dmix, cfg):
    T, G, Lc = q.shape[0], cfg['G'], cfg['Lc']

    def body(q_ref, k_ref, v_ref, o_ref, lse_ref, do_ref, dq_ref, dk_ref, dv_ref):
        i = pl.program_id(1)

        @pl.when(i == 0)
        def _():
            dk_ref[...] = jnp.zeros_like(dk_ref)
            dv_ref[...] = jnp.zeros_like(dv_ref)

        def attend(rows):
            kk, vv = k_ref[0:rows, :], v_ref[0:rows, :]
            for g in range(G):
                sl = slice(g * HEAD_DIM, (g + 1) * HEAD_DIM)
                qg, do = q_ref[:, sl], do_ref[:, sl]
                delta = jnp.sum(do * o_ref[:, sl], axis=1, keepdims=True)
                p = jnp.exp2(_nt(qg, kk) - lse_ref[:, g * HEAD_DIM:g * HEAD_DIM + 1])
                dob = _bf(do)
                dv_ref[0:rows, :] += _tn(_bf(p), dob)
                ds = _bf(p * (_nt(dob, vv) - delta))
                dq_ref[:, sl] = _nn(ds, kk)
                dk_ref[0:rows, :] += _tn(ds, qg)

        @pl.when(i == 0)
        def _():
            attend(Lc)

        @pl.when(i > 0)
        def _():
            attend(T)

    qs, kvs = _att_specs(T, G)
    return pl.pallas_call(
        body, name=name, grid=(cfg['NKV'], T // ROW_TILE), in_specs=[qs, kvs, kvs, qs, qs, qs], out_specs=[qs, kvs, kvs],
        out_shape=[jax.ShapeDtypeStruct(q.shape, F32), jax.ShapeDtypeStruct(k.shape, F32), jax.ShapeDtypeStruct(k.shape, F32)],
        compiler_params=_params())(q, k, v, o, lse, dmix)


def _band(i, T, Lc):
    start = pl.multiple_of(jnp.clip(WINDOW + (i - 1) * ROW_TILE, 0, T - BAND), WINDOW)
    qpos = (i - 1) * ROW_TILE + lax.broadcasted_iota(jnp.int32, (ROW_TILE, 1), 0)
    kpos = start - Lc + lax.broadcasted_iota(jnp.int32, (1, BAND), 1)
    ok = (jnp.abs(kpos - qpos) <= WINDOW) & (kpos >= 0) & (i > 0)
    return start, jnp.where(ok, 0.0, NEG_INF).astype(F32)


def _attn_win_fwd(name, q, k, v, sink, cfg):
    T, G, Lc = q.shape[0], cfg['G'], cfg['Lc']

    def body(sink_ref, q_ref, k_ref, v_ref, o_ref, lse_ref):
        h, i = pl.program_id(0), pl.program_id(1)
        start, bias = _band(i, T, Lc)
        kc, vc = k_ref[0:Lc, :], v_ref[0:Lc, :]
        kb, vb = k_ref[pl.ds(start, BAND), :], v_ref[pl.ds(start, BAND), :]
        for g in range(G):
            sl = slice(g * HEAD_DIM, (g + 1) * HEAD_DIM)
            qg = q_ref[:, sl]
            sk = sink_ref[h * G + g] * LOG2_E
            sc = _nt(qg, kc)
            sb = _nt(qg, kb) + bias
            m = jnp.maximum(jnp.maximum(jnp.max(sc, axis=1, keepdims=True), jnp.max(sb, axis=1, keepdims=True)), sk)
            pc, pb = jnp.exp2(sc - m), jnp.exp2(sb - m)
            l = jnp.sum(pc, axis=1, keepdims=True) + jnp.sum(pb, axis=1, keepdims=True) + jnp.exp2(sk - m)
            o_ref[:, sl] = (_nn(_bf(pc), vc) + _nn(_bf(pb), vb)) / l
            lse_ref[:, sl] = jnp.broadcast_to(m + jnp.log2(l), (ROW_TILE, HEAD_DIM))

    qs, kvs = _att_specs(T, G)
    return pl.pallas_call(
        body, name=name, grid=(cfg['NKV'], T // ROW_TILE),
        in_specs=[pl.BlockSpec(memory_space=pltpu.SMEM), qs, kvs, kvs], out_specs=[qs, qs],
        out_shape=[jax.ShapeDtypeStruct(q.shape, F32), jax.ShapeDtypeStruct(q.shape, F32)],
        compiler_params=_params())(sink, q, k, v)


def _attn_win_bwd(name, q, k, v, o, lse, dmix, sink, cfg):
    T, G, Lc = q.shape[0], cfg['G'], cfg['Lc']

    def body(sink_ref, q_ref, k_ref, v_ref, o_ref, lse_ref, do_ref, dq_ref, dk_ref, dv_ref, dsink_ref):
        h, i = pl.program_id(0), pl.program_id(1)
        start, bias = _band(i, T, Lc)
        kc, vc = k_ref[0:Lc, :], v_ref[0:Lc, :]
        kb, vb = k_ref[pl.ds(start, BAND), :], v_ref[pl.ds(start, BAND), :]

        @pl.when(i == 0)
        def _():
            dk_ref[...] = jnp.zeros_like(dk_ref)
            dv_ref[...] = jnp.zeros_like(dv_ref)
            dsink_ref[...] = jnp.zeros_like(dsink_ref)

        for g in range(G):
            sl = slice(g * HEAD_DIM, (g + 1) * HEAD_DIM)
            qg, do = q_ref[:, sl], do_ref[:, sl]
            lse = lse_ref[:, g * HEAD_DIM:g * HEAD_DIM + 1]
            delta = jnp.sum(do * o_ref[:, sl], axis=1, keepdims=True)
            pc = jnp.exp2(_nt(qg, kc) - lse)
            pb = jnp.exp2(_nt(qg, kb) + bias - lse)
            ps = jnp.exp2(sink_ref[h * G + g] * LOG2_E - lse)
            dob = _bf(do)
            dv_ref[0:Lc, :] += _tn(_bf(pc), dob)
            dv_ref[pl.ds(start, BAND), :] += _tn(_bf(pb), dob)
            dsc = _bf(pc * (_nt(dob, vc) - delta))
            dsb = _bf(pb * (_nt(dob, vb) - delta))
            dq_ref[:, sl] = _nn(dsc, kc) + _nn(dsb, kb)
            dk_ref[0:Lc, :] += _tn(dsc, qg)
            dk_ref[pl.ds(start, BAND), :] += _tn(dsb, qg)
            dsk = jnp.where(i > 0, -jnp.sum(ps * delta, axis=0, keepdims=True), 0.0)
            dsink_ref[:, sl] += jnp.broadcast_to(dsk, (8, HEAD_DIM))

    qs, kvs = _att_specs(T, G)
    return pl.pallas_call(
        body, name=name, grid=(cfg['NKV'], T // ROW_TILE),
        in_specs=[pl.BlockSpec(memory_space=pltpu.SMEM), qs, kvs, kvs, qs, qs, qs],
        out_specs=[qs, kvs, kvs, pl.BlockSpec((None, 8, G * HEAD_DIM), lambda h, i: (h, 0, 0))],
        out_shape=[jax.ShapeDtypeStruct(q.shape, F32), jax.ShapeDtypeStruct(k.shape, F32), jax.ShapeDtypeStruct(k.shape, F32),
                   jax.ShapeDtypeStruct((cfg['NKV'], 8, G * HEAD_DIM), F32)],
        compiler_params=_params())(sink, q, k, v, o, lse, dmix)


def _seq_pos(T, Lc):
    row = lax.broadcasted_iota(jnp.int32, (T, 1), 0)
    return jnp.where(row < Lc, row, row - Lc), jnp.where(row < Lc, Lc, T - Lc)


def _fw(x, k, pos, seglen):
    return jnp.where(pos + k < seglen, pltpu.roll(x, x.shape[0] - k, 0), 0.0)


def _bw(x, k, pos):
    return jnp.where(pos - k >= 0, pltpu.roll(x, k, 0), 0.0)


def _conv_fwd(name, P, conv_w8, cfg):
    T, Lc = P.shape[0], cfg['Lc']
    cb = (cfg['ATT'] + 2 * cfg['KVW']) // HEAD_DIM
    na = AUX_WIDTH // HEAD_DIM

    def body(gb_ref, gc_ref, u_ref, w_ref, o_ref):
        pos, seglen = _seq_pos(T, Lc)
        z = gc_ref[...] * u_ref[...]
        conv = w_ref[0:1, :] * _bw(z, 1, pos) + w_ref[1:2, :] * z + w_ref[2:3, :] * _fw(z, 1, pos, seglen)
        o_ref[...] = gb_ref[...] * conv

    col = lambda off: pl.BlockSpec((T, HEAD_DIM), lambda c: (0, cb + off + c))
    return pl.pallas_call(
        body, name=name, grid=(na,),
        in_specs=[col(0), col(na), col(2 * na), pl.BlockSpec((8, HEAD_DIM), lambda c: (0, c))],
        out_specs=pl.BlockSpec((T, HEAD_DIM), lambda c: (0, c)),
        out_shape=jax.ShapeDtypeStruct((T, AUX_WIDTH), F32), compiler_params=_params())(P, P, P, conv_w8)


def _conv_bwd(name, P, conv_w8, dmix, cfg):
    T, Lc = P.shape[0], cfg['Lc']
    cb = (cfg['ATT'] + 2 * cfg['KVW']) // HEAD_DIM
    ob = cfg['ATT'] // HEAD_DIM
    na = AUX_WIDTH // HEAD_DIM

    def body(gb_ref, gc_ref, u_ref, w_ref, do_ref, dgb_ref, dgc_ref, du_ref, dw_ref):
        pos, seglen = _seq_pos(T, Lc)
        gc, u, do = gc_ref[...], u_ref[...], do_ref[...]
        z = gc * u
        zm, zp = _bw(z, 1, pos), _fw(z, 1, pos, seglen)
        w0, w1, w2 = w_ref[0:1, :], w_ref[1:2, :], w_ref[2:3, :]
        dgb_ref[...] = (do * (w0 * zm + w1 * z + w2 * zp)).astype(dgb_ref.dtype)
        dc = do * gb_ref[...]
        dz = w0 * _fw(dc, 1, pos, seglen) + w1 * dc + w2 * _bw(dc, 1, pos)
        dgc_ref[...] = (dz * u).astype(dgc_ref.dtype)
        du_ref[...] = (dz * gc).astype(du_ref.dtype)
        dw_ref[...] = jnp.zeros_like(dw_ref)
        dw_ref[0:1, :] = jnp.sum(dc * zm, axis=0, keepdims=True)
        dw_ref[1:2, :] = jnp.sum(dc * z, axis=0, keepdims=True)
        dw_ref[2:3, :] = jnp.sum(dc * zp, axis=0, keepdims=True)

    col = lambda off: pl.BlockSpec((T, HEAD_DIM), lambda c: (0, cb + off + c))
    wspec = pl.BlockSpec((8, HEAD_DIM), lambda c: (0, c))
    ocol = lambda off: pl.BlockSpec((T, HEAD_DIM), lambda c: (0, off + c))
    return pl.pallas_call(
        body, name=name, grid=(na,),
        in_specs=[col(0), col(na), col(2 * na), wspec, ocol(ob)],
        out_specs=[ocol(0), ocol(0), ocol(0), wspec],
        out_shape=[jax.ShapeDtypeStruct((T, AUX_WIDTH), BF16)] * 3 + [jax.ShapeDtypeStruct((8, AUX_WIDTH), F32)],
        compiler_params=_params())(P, P, P, conv_w8, dmix)


def _window_sums(x, half, pos, seglen):
    fwd, bwd = x, x
    s = 1
    while s < half:
        fwd = fwd + _fw(fwd, s, pos, seglen)
        bwd = bwd + _bw(bwd, s, pos)
        s *= 2
    return fwd, bwd


def _pooled(u, half, pos, seglen):
    fwd, bwd = _window_sums(u, half, pos, seglen)
    cnt = (jnp.minimum(pos + half, seglen) - jnp.maximum(pos - half, 0)).astype(F32)
    return (fwd + _bw(bwd, 1, pos)) / cnt - u, cnt


def _pool_fwd(name, P, pool_w, pool_scale, cfg):
    T, Lc = P.shape[0], cfg['Lc']
    cb = (cfg['ATT'] + 2 * cfg['KVW']) // HEAD_DIM

    def body(u_ref, w_ref, s_ref, o_ref):
        g = pl.program_id(0)
        pos, seglen = _seq_pos(T, Lc)
        for k, half in enumerate(POOL_HALF):
            @pl.when(g == k)
            def _(half=half):
                pooled, _ = _pooled(u_ref[...], half, pos, seglen)
                o_ref[...] = _nn(_bf(pooled), _bf(w_ref[...])) * s_ref[...]

    return pl.pallas_call(
        body, name=name, grid=(AUX_GROUPS,),
        in_specs=[pl.BlockSpec((T, HEAD_DIM), lambda g: (0, cb + g)), pl.BlockSpec((None, HEAD_DIM, HEAD_DIM), lambda g: (g, 0, 0)),
                  pl.BlockSpec((1, HEAD_DIM), lambda g: (0, g))],
        out_specs=pl.BlockSpec((T, HEAD_DIM), lambda g: (0, g)),
        out_shape=jax.ShapeDtypeStruct((T, AUX_WIDTH), F32), compiler_params=_params())(P, pool_w, pool_scale)


def _pool_bwd(name, P, pool_w, pool_scale, dmix, cfg):
    T, Lc = P.shape[0], cfg['Lc']
    cb = (cfg['ATT'] + 2 * cfg['KVW']) // HEAD_DIM
    ob = cfg['ATT'] // HEAD_DIM

    def body(u_ref, w_ref, s_ref, do_ref, du_ref, dw_ref, ds_ref):
        g = pl.program_id(0)
        pos, seglen = _seq_pos(T, Lc)
        for k, half in enumerate(POOL_HALF):
            @pl.when(g == k)
            def _(half=half):
                do = do_ref[...]
                pooled, cnt = _pooled(u_ref[...], half, pos, seglen)
                wb = _bf(w_ref[...])
                mixed = _nn(_bf(pooled), wb)
                ds_ref[...] = jnp.broadcast_to(jnp.sum(do * mixed, axis=0, keepdims=True), ds_ref.shape)
                dmixed = _bf(do * s_ref[...])
                dw_ref[...] = _tn(_bf(pooled), dmixed)
                dpooled = _nt(dmixed, wb)
                e = dpooled / cnt
                fwd, bwd = _window_sums(e, half, pos, seglen)
                adj = fwd + _fw(e, half, pos, seglen) + _bw(bwd, 1, pos) - _bw(e, half, pos)
                du_ref[...] = (adj - dpooled).astype(du_ref.dtype)

    wspec = pl.BlockSpec((None, HEAD_DIM, HEAD_DIM), lambda g: (g, 0, 0))
    return pl.pallas_call(
        body, name=name, grid=(AUX_GROUPS,),
        in_specs=[pl.BlockSpec((T, HEAD_DIM), lambda g: (0, cb + g)), wspec, pl.BlockSpec((1, HEAD_DIM), lambda g: (0, g)),
                  pl.BlockSpec((T, HEAD_DIM), lambda g: (0, ob + g))],
        out_specs=[pl.BlockSpec((T, HEAD_DIM), lambda g: (0, g)), wspec, pl.BlockSpec((8, HEAD_DIM), lambda g: (0, g))],
        out_shape=[jax.ShapeDtypeStruct((T, AUX_WIDTH), BF16), jax.ShapeDtypeStruct(pool_w.shape, F32),
                   jax.ShapeDtypeStruct((8, AUX_WIDTH), F32)],
        compiler_params=_params())(P, pool_w, pool_scale, dmix)


def _ffn_up(name, hn, wg3, wu3, dep=None):
    T, D = hn.shape
    J, k, _ = wg3.shape
    tm = _row_tile(T, 1088)
    dep_ins, dep_specs = _dep(dep, 2)

    def body(x_ref, wg_ref, wu_ref, *rest):
        g_ref, u_ref, a_ref = rest[len(dep_ins):]
        x = x_ref[...]
        g, u = _nt(x, wg_ref[...]), _nt(x, wu_ref[...])
        g_ref[...] = g.astype(g_ref.dtype)
        u_ref[...] = u.astype(u_ref.dtype)
        a_ref[...] = (g * jax.nn.sigmoid(g) * u).astype(a_ref.dtype)

    wspec = pl.BlockSpec((None, k, D), lambda j, i: (j, 0, 0))
    ospec = pl.BlockSpec((None, tm, k), lambda j, i: (j, i, 0))
    return pl.pallas_call(
        body, name=name, grid=(J, T // tm), in_specs=[pl.BlockSpec((tm, D), lambda j, i: (i, 0)), wspec, wspec] + dep_specs,
        out_specs=[ospec, ospec, ospec],
        out_shape=[jax.ShapeDtypeStruct((J, T, k), BF16)] * 3,
        compiler_params=_params())(hn, wg3, wu3, *dep_ins)


def _ffn_dact(name, dF, wd3, G, U):
    T, D = dF.shape
    J, k, _ = wd3.shape
    tm = _row_tile(T, 1088)

    def body(df_ref, wd_ref, g_ref, u_ref, dg_ref, du_ref):
        da = _nt(df_ref[...], wd_ref[...])
        g = g_ref[...].astype(F32)
        sig = jax.nn.sigmoid(g)
        du_ref[...] = (da * g * sig).astype(du_ref.dtype)
        dg_ref[...] = (da * u_ref[...].astype(F32) * (sig * (1 + g * (1 - sig)))).astype(dg_ref.dtype)

    aspec = pl.BlockSpec((None, tm, k), lambda j, i: (j, i, 0))
    return pl.pallas_call(
        body, name=name, grid=(J, T // tm),
        in_specs=[pl.BlockSpec((tm, D), lambda j, i: (i, 0)), pl.BlockSpec((None, k, D), lambda j, i: (j, 0, 0)), aspec, aspec],
        out_specs=[aspec, aspec],
        out_shape=[jax.ShapeDtypeStruct((J, T, k), BF16), jax.ShapeDtypeStruct((J, T, k), BF16)],
        compiler_params=_params())(dF, wd3, G, U)


def _loss_head(name, h, g, target, cfg):
    T, D = h.shape

    def body(h_ref, g_ref, t_ref, dh_ref, loss_ref, dg_ref):
        i = pl.program_id(0)

        @pl.when(i == 0)
        def _():
            dh_ref[...] = jnp.zeros_like(dh_ref)
            loss_ref[...] = jnp.zeros_like(loss_ref)
            dg_ref[...] = jnp.zeros_like(dg_ref)

        @pl.when(i > 0)
        def _():
            x = h_ref[...]
            r = _rstd(x)
            xhat = x * r
            gg = g_ref[...]
            err = xhat * gg - t_ref[...]
            loss_ref[...] += 0.5 * jnp.sum(jnp.sum(err * err, axis=1, keepdims=True) / D, axis=0, keepdims=True)
            dy = err / D
            dg_ref[0:1, :] += jnp.sum(dy * xhat, axis=0, keepdims=True)
            dxh = dy * gg
            dh_ref[...] = r * (dxh - xhat * jnp.mean(dxh * xhat, axis=-1, keepdims=True))

    row = pl.BlockSpec((ROW_TILE, D), lambda i: (i, 0))
    return pl.pallas_call(
        body, name=name, grid=(T // ROW_TILE,),
        in_specs=[row, pl.BlockSpec((1, D), lambda i: (0, 0)), pl.BlockSpec((ROW_TILE, D), lambda i: (jnp.maximum(i - 1, 0), 0))],
        out_specs=[row, pl.BlockSpec((8, 128), lambda i: (0, 0)), pl.BlockSpec((8, D), lambda i: (0, 0))],
        out_shape=[jax.ShapeDtypeStruct((T, D), F32), jax.ShapeDtypeStruct((8, 128), F32), jax.ShapeDtypeStruct((8, D), F32)],
        compiler_params=_params())(h, g, target)


def _adamw(name, parts, w, m, v, dep=None):
    R, C = w.shape
    n_parts = parts.shape[0]
    tr = _row_tile(R, max(16, (1 << 18) // C)) if R % 16 == 0 else R
    bc1 = 1.0 - ADAM_B1 ** ADAM_STEP
    bc2 = 1.0 - ADAM_B2 ** ADAM_STEP
    dep_ins, dep_specs = _dep(dep, 1)

    def body(p_ref, w_ref, m_ref, v_ref, *rest):
        g_ref, d_ref, nm_ref, nv_ref = rest[len(dep_ins):]
        g = p_ref[0].astype(F32)
        for k in range(1, n_parts):
            g = g + p_ref[k].astype(F32)
        nm = ADAM_B1 * m_ref[...] + (1.0 - ADAM_B1) * g
        nv = ADAM_B2 * v_ref[...] + (1.0 - ADAM_B2) * (g * g)
        g_ref[...] = g
        nm_ref[...] = nm
        nv_ref[...] = nv
        d_ref[...] = -ADAM_LR * ((nm / bc1) / (jnp.sqrt(nv / bc2) + ADAM_EPS) + ADAM_WD * w_ref[...])

    blk = pl.BlockSpec((tr, C), lambda i: (i, 0))
    return pl.pallas_call(
        body, name=name, grid=(R // tr,), in_specs=[pl.BlockSpec((n_parts, tr, C), lambda i: (0, i, 0)), blk, blk, blk] + dep_specs,
        out_specs=[blk] * 4, out_shape=[jax.ShapeDtypeStruct((R, C), F32)] * 4, compiler_params=_params())(parts, w, m, v, *dep_ins)


class _WeightStream:
    def __init__(self, cast):
        self.cast, self.handles = cast, {}

    @staticmethod
    def _tag(l, group):
        return ("ffn" if group is FFN_WEIGHTS else group[0]) + str(l)

    def start(self, l, group, after=None):
        self.handles[l, group], token = _gather_start(f"gather_{self._tag(l, group)}_start", [self.cast(l, n) for n in group], after)
        return token

    def relay(self, l, group, after):
        self.handles[l, group], token = _gather_relay(f"gather_{self._tag(l, group)}_relay", self.handles[l, group], after)
        return token

    def get(self, l, group, after):
        got = dict(zip(group, _gather_wait(f"gather_{self._tag(l, group)}_wait", self.handles[l, group], after)))
        if 'w_out' in got:
            rows, cols = got['w_out'].shape[1:]
            got['w_out'] = got['w_out'].reshape(N_DEV * rows, cols)
        return got


def _layer_fwd(l, h, p, stream, mod, rope, conv_w8, cfg):
    nm = f"l{l}_"
    mod = mod + stream.relay(l, IN_WEIGHT, h)[0, 0]
    xn = _norm_mod(nm + "norm1", h, p['norm1_g'], mod, 0)
    W = stream.get(l, IN_WEIGHT, xn)
    token = None
    if l == 0:
        token = stream.start(0, OUT_WEIGHT, after=W['w_in']) + stream.start(0, FFN_WEIGHTS, after=W['w_in'])
    P = _mm_cols(nm + "w_in", xn, W['w_in'], dep=token)
    qr, kr, vb = _qk_prep(nm + "qk_prep", P, p['q_norm_g'], p['k_norm_g'], rope[0], rope[1], cfg)
    if l == 0:
        o, lse = _attn_dense_fwd(nm + "attn", qr, kr, vb, cfg)
        aux = _conv_fwd(nm + "conv", P, conv_w8, cfg)
    else:
        o, lse = _attn_win_fwd(nm + "attn", qr, kr, vb, p['sink'], cfg)
        aux = _pool_fwd(nm + "pool", P, p['pool_w'], p['pool_scale'], cfg)
    mix = jnp.concatenate([o, aux], axis=1).astype(BF16)
    W.update(stream.get(l, OUT_WEIGHT, stream.relay(l, OUT_WEIGHT, mix)))
    y, h2 = _mm_plain(nm + "w_out", mix, W['w_out'], False, dep=stream.relay(l, FFN_WEIGHTS, o), res=(h, mod, 2, cfg['Lc']))
    hn = _norm_mod(nm + "norm2", h2, p['norm2_g'], mod, 1)
    W.update(stream.get(l, FFN_WEIGHTS, hn))
    token = stream.start(1, IN_WEIGHT, after=W['w_down']) if l == 0 else None
    G, U, A = _ffn_up(nm + "ffn_up", hn, W['w_gate'], W['w_up'], dep=token)
    if l == 0:
        token = stream.start(1, OUT_WEIGHT, after=A) + stream.start(1, FFN_WEIGHTS, after=A)
    F, h3 = _mm_shards_nn(nm + "w_down", A, W['w_down'], dep=token, res=(h2, mod, 5, cfg['Lc']))
    saved = dict(h=h, xn=xn, P=P, qr=qr, kr=kr, vb=vb, o=o, lse=lse, mix=mix, y=y, h2=h2, hn=hn, G=G, U=U, A=A, F=F)
    return h3, saved, W


def _layer_bwd(l, dh3, s, p, W, mod, rope, conv_w8, cfg):
    nm = f"l{l}_bwd_"
    J = N_DEV
    dF, dmod = _gate_bwd(nm + "res2", dh3, s['F'], mod, 5)
    dG, dU = _ffn_dact(nm + "ffn_act", dF, W['w_down'], s['G'], s['U'])
    big = {'w_down': _wgrad_down(nm + "dw_down", s['A'], dF),
           'w_gate': _wgrad_down(nm + "dw_gate", dG, s['hn']),
           'w_up': _wgrad_down(nm + "dw_up", dU, s['hn'])}
    handles = {}
    handles['ffn'], token = _exchange_start(f"scatter_ffn{l}_start", [big[n] for n in FFN_WEIGHTS], True)
    mod = mod + token[0, 0]
    dhn = _mm_shards_nn2(nm + "dhn", dG, W['w_gate'], dU, W['w_up'])
    dh2, dm, dg2 = _norm_mod_bwd(nm + "norm2", dhn, s['h2'], p['norm2_g'], mod, 1, dh3)
    dmod += dm
    dY, dm = _gate_bwd(nm + "res1", dh2, s['y'], mod, 2)
    dmod += dm
    dwo = _wgrad_rows(nm + "dw_out", s['mix'], dY)
    handles['w_out'], token = _exchange_start(f"scatter_w_out{l}_start", [dwo.reshape((J, dwo.shape[0] // J, dwo.shape[1]))], True)
    dmix = _mm_plain(nm + "dmix", dY, W['w_out'], True, dep=token)
    small = {'norm2_g': dg2[0]}
    if l == 0:
        dqr, dkr, dv = _attn_dense_bwd(nm + "attn", s['qr'], s['kr'], s['vb'], s['o'], s['lse'], dmix, cfg)
        *daux, dcw = _conv_bwd(nm + "conv", s['P'], conv_w8, dmix, cfg)
        small['conv_w'] = dcw[0:3]
    else:
        dqr, dkr, dv, dsk = _attn_win_bwd(nm + "attn", s['qr'], s['kr'], s['vb'], s['o'], s['lse'], dmix, p['sink'], cfg)
        du, dpw, dps = _pool_bwd(nm + "pool", s['P'], p['pool_w'], p['pool_scale'], dmix, cfg)
        daux = [du]
        small.update(sink=dsk[:, 0, ::HEAD_DIM].reshape(-1), pool_w=dpw, pool_scale=dps[0])
    dq, dk, dqg, dkg = _qk_prep_bwd(nm + "qk_prep", dqr, dkr, s['P'], p['q_norm_g'], p['k_norm_g'], rope[0], rope[1], cfg)
    small.update(q_norm_g=dqg[0], k_norm_g=dkg[0])
    dP = jnp.concatenate([dq, dk, dv.astype(BF16), *daux], axis=1)
    handles['w_in'], token = _exchange_start(f"scatter_w_in{l}_start", [_wgrad_cols(nm + "dw_in", s['xn'], dP, J)], True)
    mod = mod + token[0, 0]
    dxn = _mm_cols_nt(nm + "dxn", dP, W['w_in'])
    dh, dm, dg1 = _norm_mod_bwd(nm + "norm1", dxn, s['h'], p['norm1_g'], mod, 0, dh2)
    dmod += dm
    small['norm1_g'] = dg1[0]
    return dh, dmod, small, handles, token


def _rope_tables(S, Lc):
    half = HEAD_DIM // 4
    pos = np.arange(S)
    inv = ROPE_THETA ** (-np.arange(0, 2 * half, 2, dtype=np.float32) / (2 * half))
    inv = jnp.asarray(inv, F32)
    ang_r = jnp.asarray(pos // GRID_W, F32)[:, None] * inv
    ang_c = jnp.asarray(pos % GRID_W, F32)[:, None] * inv
    cos = jnp.concatenate([jnp.cos(ang_r)] * 2 + [jnp.cos(ang_c)] * 2, axis=1)
    sin = jnp.concatenate([-jnp.sin(ang_r), jnp.sin(ang_r), -jnp.sin(ang_c), jnp.sin(ang_c)], axis=1)
    return (jnp.concatenate([jnp.ones((Lc, HEAD_DIM), F32), cos], axis=0),
            jnp.concatenate([jnp.zeros((Lc, HEAD_DIM), F32), sin], axis=0))


def _pad_rows(a, rows):
    return jnp.concatenate([a, jnp.zeros((rows - a.shape[0],) + a.shape[1:], a.dtype)], axis=0)


def _flat128(a, nlead):
    lead = a.shape[:nlead]
    f = a.reshape(lead + (-1,))
    pad = (-f.shape[-1]) % 128
    if pad:
        f = jnp.concatenate([f, jnp.zeros(lead + (pad,), f.dtype)], axis=-1)
    return f.reshape(lead + (-1, 128))


def _pack(named, nlead=0):
    rows, layout, at = [], {}, 0
    for name, a in named:
        f = _flat128(a, nlead)
        n = f.shape[-2]
        pad = (-n) % 8
        if pad:
            f = jnp.concatenate([f, jnp.zeros(f.shape[:-2] + (pad, 128), f.dtype)], axis=-2)
        layout[name] = (at, n, a.shape[nlead:])
        rows.append(f)
        at += n + pad
    return jnp.concatenate(rows, axis=-2), layout


def _unpack(arr, layout, name):
    at, n, shape = layout[name]
    return arr[..., at:at + n, :].reshape(arr.shape[:-2] + (-1,))[..., :math.prod(shape)].reshape(arr.shape[:-2] + tuple(shape))


def kernel(*args):
    A = dict(zip(INPUT_NAMES, args, strict=True))
    x, ctx = A['x'][0], A['ctx'][0]
    S, D = x.shape
    Lc = ctx.shape[0]
    T = Lc + S
    ATT = D - AUX_WIDTH
    KVW = (A['l1_w_in'].shape[1] * N_DEV - ATT - AUX_WIDTH) // 2
    cfg = dict(ATT=ATT, KVW=KVW, NKV=KVW // HEAD_DIM, G=ATT // KVW, Lc=Lc)
    assert Lc == ROW_TILE and S % ROW_TILE == 0 and T >= BAND and S % GRID_W == 0
    cw = A['l0_conv_w'].shape[1]
    me = 4 * lax.axis_index("x") + 2 * lax.axis_index("y") + lax.axis_index("c")

    def layer_params(l):
        pre = f"l{l}_"
        return {k[len(pre):]: (v.reshape(1, -1) if v.ndim == 1 and k != 'l1_sink' else v) for k, v in A.items() if k.startswith(pre)}

    params = [layer_params(0), layer_params(1)]

    def cast(l, n):
        w = A[f'l{l}_{n}']
        return (w.T if n in TRANSPOSED else w).astype(BF16)

    stream = _WeightStream(cast)
    token = stream.start(0, IN_WEIGHT)

    big_names = [n for n in WEIGHT_NAMES if n[3:] in BIG_WEIGHTS + ('w_mod',)]
    rest = [n for n in WEIGHT_NAMES if n not in big_names]
    early = ['x', 'ctx'] + rest + ['m_' + n for n in rest] + ['v_' + n for n in rest]
    token, held = lax.optimization_barrier((token, [A[n] for n in early]))
    A.update(zip(early, held))
    x, ctx = A['x'][0], A['ctx'][0]
    wp, layw = _pack([(n, A[n]) for n in rest])
    mp, _ = _pack([(n, A['m_' + n]) for n in rest])
    vp, _ = _pack([(n, A['v_' + n]) for n in rest])
    rope = _rope_tables(S, Lc)
    h = jnp.concatenate([ctx, x], axis=0)

    sc_own = jax.nn.silu(A['c']) + token[0, 0]
    first, lay0 = _pack([('sc', sc_own), ('conv_w', A['l0_conv_w'])])
    first, h, wp, mp, vp = lax.optimization_barrier((first, h, wp, mp, vp))
    first_all = _exchange("gather_cond", [first], False)[0]
    sc_all = _unpack(first_all, lay0, 'sc')[:, 0]
    conv_w = _unpack(first_all, lay0, 'conv_w').transpose(1, 0, 2).reshape(3, N_DEV * cw)
    conv_w8 = _pad_rows(conv_w, 8)
    sc_ctx = jax.nn.silu(A['c_ctx'])
    s16 = _pad_rows(jnp.concatenate([sc_all, sc_ctx[None]], axis=0), 16)

    nmod = A['l0_w_mod'].shape[1]
    modp = jnp.concatenate([_mm_plain(f"l{l}_mod", s16, A[f'l{l}_w_mod'], False) for l in range(2)], axis=1)
    modp_all = _exchange("gather_mod", [modp], False)[0]
    mods = []
    for l in range(2):
        full = modp_all[:, :, l * nmod:(l + 1) * nmod].transpose(1, 0, 2).reshape(16, N_MOD * D) + A[f'l{l}_b_mod'][None]
        both = jnp.stack([full[8], lax.dynamic_index_in_dim(full, me, 0, keepdims=False)]).reshape(2, N_MOD, D)
        mods.append(jnp.concatenate([both, jnp.zeros((2, 8 - N_MOD, D), F32)], axis=1))

    saved, W = [], []
    for l in range(2):
        h, s, Wl = _layer_fwd(l, h, params[l], stream, mods[l], rope, conv_w8, cfg)
        saved.append(s)
        W.append(Wl)

    dh, loss_blk, dgf = _loss_head("loss_head", h, A['final_norm_g'].reshape(1, -1), A['loss_target'][0], cfg)
    loss = lax.psum(loss_blk[0, 0], ("x", "y", "c"))

    grads, small, dmods, scatters = {}, {'final_norm_g': dgf[0]}, [None, None], [None, None]
    token = jnp.zeros((8, 128), F32)
    for l in (1, 0):
        dh, dmods[l], sm, scatters[l], token = _layer_bwd(l, dh, saved[l], params[l], W[l], mods[l] + token[0, 0], rope, conv_w8, cfg)
        small.update({f'l{l}_{k}': v for k, v in sm.items()})
    grad_x = dh[Lc:][None]

    def landed(l, key, after):
        group = FFN_WEIGHTS if key == 'ffn' else (key,)
        for n, parts in zip(group, _exchange_wait(f"scatter_{key}{l}_wait", scatters[l][key], after)):
            shape = A[f'l{l}_{n}'].shape
            grads[f'l{l}_{n}'] = (parts.reshape((N_DEV,) + (shape[::-1] if n in TRANSPOSED else shape)), None)

    out = {}

    def adam(n, dep=None):
        w, m, v = A[n], A['m_' + n], A['v_' + n]
        if n[3:] in TRANSPOSED:
            out[n] = tuple(r.T for r in _adamw("adamw_" + n, grads[n][0], w.T, m.T, v.T, dep))
        else:
            out[n] = _adamw("adamw_" + n, grads[n][0], w, m, v, dep)
        return out[n][1]

    last = dh
    for l in (1, 0):
        for key in ('ffn', 'w_out') + (('w_in',) if l == 1 else ()):
            landed(l, key, last)
            for n in (FFN_WEIGHTS if key == 'ffn' else (key,)):
                last = adam(f'l{l}_{n}', token)

    small_names = [n for n in WEIGHT_NAMES if n in small]
    pieces = [(n, small[n]) for n in small_names]
    for l in range(2):
        pieces += [(f'dmod{l}', dmods[l][1, :N_MOD]), (f'dcmod{l}', dmods[l][0, :N_MOD])]
    second, lay1 = _pack(pieces)
    second_all = _exchange("gather_small", [second], False, after=last)[0]

    dsc_part = jnp.zeros((16, D), F32)
    for l in range(2):
        dm16 = _pad_rows(jnp.concatenate([_unpack(second_all, lay1, f'dmod{l}').reshape(N_DEV, N_MOD * D),
                                          jnp.sum(_unpack(second_all, lay1, f'dcmod{l}'), axis=0).reshape(1, N_MOD * D)], axis=0), 16)
        mine = lax.dynamic_slice_in_dim(dm16, me * nmod, nmod, axis=1)
        tk = _col_tile(D, 512)
        gw = _mm_tn(f"l{l}_dw_mod", s16, mine, pl.BlockSpec((16, tk), lambda kb, t: (0, kb)), pl.BlockSpec((16, nmod), lambda kb, t: (0, 0)),
                    jax.ShapeDtypeStruct((D, nmod), F32), pl.BlockSpec((tk, nmod), lambda kb, t: (kb, 0)), (tk, nmod), (D // tk, 1))
        grads[f'l{l}_w_mod'] = (gw[None], None)
        dsc_part += _mm_plain(f"l{l}_dsc", mine, A[f'l{l}_w_mod'], True)
        dmod_dev = _unpack(second_all, lay1, f'dmod{l}') + _unpack(second_all, lay1, f'dcmod{l}')
        grads[f'l{l}_b_mod'] = (dmod_dev.reshape(N_DEV, N_MOD * D), None)
    dsig = jax.nn.sigmoid(A['c_ctx'])
    dsilu = dsig * (1 + A['c_ctx'] * (1 - dsig))
    third_all = _exchange("gather_dsc", [dsc_part[8:9]], False)[0]
    grads['c_ctx'] = (third_all[:, 0] * dsilu[None], None)
    for n in small_names:
        g8 = _unpack(second_all, lay1, n)
        if n == 'l0_conv_w':
            g8 = lax.dynamic_slice_in_dim(g8, me * cw, cw, axis=2)
        grads[n] = (g8, None)

    adam('l0_w_mod')
    last = adam('l1_w_mod')
    gp, _ = _pack([(n, grads[n][0]) for n in rest], nlead=1)
    res = _adamw("adamw_small", gp, wp, mp, vp)
    for n in rest:
        out[n] = tuple(_unpack(r, layw, n) for r in res)
    landed(0, 'w_in', last)
    adam('l0_w_in')

    outs = [loss, grad_x]
    for k in range(4):
        outs += [out[n][k] for n in WEIGHT_NAMES]
    return tuple(outs)
```

```python
import functools
import math

import numpy as np
import jax
import jax.numpy as jnp
from jax import lax
from jax.experimental import pallas as pl
from jax.experimental.pallas import tpu as pltpu

F32 = jnp.float32
BF16 = jnp.bfloat16
HEAD_DIM = 128
AUX_WIDTH = 512
AUX_GROUPS = 4
POOL_HALF = (1, 2, 4, 8)
WINDOW = 128
GRID_W = 64
ROPE_THETA = 10000.0
EPS = 1e-6
NEG_INF = -1e30
ATT_SCALE = HEAD_DIM ** -0.5
LOG2_E = math.log2(math.e)
Q_SCALE = ATT_SCALE * LOG2_E
N_MOD = 6
N_DEV = 8
ROW_TILE = 256
BAND = ROW_TILE + 2 * WINDOW
ADAM_LR, ADAM_B1, ADAM_B2, ADAM_EPS, ADAM_WD, ADAM_STEP = 0.001, 0.9, 0.999, 1e-08, 0.01, 10
VMEM_LIMIT_MB = 56
MESH = pl.DeviceIdType.MESH

WEIGHT_NAMES = ['c_ctx', 'l0_norm1_g', 'l0_w_mod', 'l0_b_mod', 'l0_w_in', 'l0_q_norm_g', 'l0_k_norm_g', 'l0_conv_w', 'l0_w_out', 'l0_norm2_g', 'l0_w_gate', 'l0_w_up', 'l0_w_down', 'l1_norm1_g', 'l1_w_mod', 'l1_b_mod', 'l1_w_in', 'l1_q_norm_g', 'l1_k_norm_g', 'l1_sink', 'l1_pool_w', 'l1_pool_scale', 'l1_w_out', 'l1_norm2_g', 'l1_w_gate', 'l1_w_up', 'l1_w_down', 'final_norm_g']
INPUT_NAMES = (['x', 'c', 'ctx'] + WEIGHT_NAMES + ['loss_target'] + ['m_' + n for n in WEIGHT_NAMES]
               + ['v_' + n for n in WEIGHT_NAMES])
IN_WEIGHT = ('w_in',)
OUT_WEIGHT = ('w_out',)
MIXER_WEIGHTS = OUT_WEIGHT + IN_WEIGHT
FFN_WEIGHTS = ('w_down', 'w_gate', 'w_up')
TRANSPOSED = ('w_gate', 'w_up')
BIG_WEIGHTS = MIXER_WEIGHTS + FFN_WEIGHTS


def _params(vmem_mb=VMEM_LIMIT_MB):
    return pltpu.CompilerParams(vmem_limit_bytes=vmem_mb << 20)


def _row_tile(n, cap):
    best = None
    for t in range(16, min(n, cap) + 1, 16):
        if n % t == 0:
            best = t
    assert best is not None, (n, cap)
    return best


def _col_tile(n, cap):
    best = n
    for t in range(128, min(n, cap) + 1, 128):
        if n % t == 0:
            best = t
    return best if best <= cap or n % 128 else n


def _dot(a, b, ca, cb):
    return lax.dot_general(a, b, (((ca,), (cb,)), ((), ())), preferred_element_type=F32)


def _nn(a, b):
    return _dot(a, b, 1, 0)


def _nt(a, b):
    return _dot(a, b, 1, 1)


def _tn(a, b):
    return _dot(a, b, 0, 0)


def _bf(x):
    return x.astype(BF16)


def _exchange(name, arrs, scatter, after=None):
    n = len(arrs)
    extra = [] if after is None else [after]
    if scatter:
        out_shape = [jax.ShapeDtypeStruct(a.shape, a.dtype) for a in arrs]
    else:
        out_shape = [jax.ShapeDtypeStruct((N_DEV,) + a.shape, a.dtype) for a in arrs]

    def body(*refs):
        ins, outs = refs[:n], refs[n + len(extra):2 * n + len(extra)]
        send_sems, recv_sems, local_sems = refs[2 * n + len(extra):]
        x, y, c = lax.axis_index("x"), lax.axis_index("y"), lax.axis_index("c")
        me = 4 * x + 2 * y + c
        local, remote = [], []
        for a in range(n):
            own = ins[a].at[me] if scatter else ins[a]
            cp = pltpu.make_async_copy(own, outs[a].at[me], local_sems.at[a])
            cp.start()
            local.append(cp)
            for r in range(1, N_DEV):
                px = 1 - x if r & 4 else x
                py = 1 - y if r & 2 else y
                pc = 1 - c if r & 1 else c
                src = ins[a].at[4 * px + 2 * py + pc] if scatter else ins[a]
                cp = pltpu.make_async_remote_copy(
                    src_ref=src, dst_ref=outs[a].at[me], send_sem=send_sems.at[a, r - 1],
                    recv_sem=recv_sems.at[a, r - 1], device_id=(px, py, pc), device_id_type=MESH)
                cp.start()
                remote.append(cp)
        for cp in remote:
            cp.wait()
        for cp in local:
            cp.wait()

    any_spec = pl.BlockSpec(memory_space=pl.ANY)
    return pl.pallas_call(
        body, name=name, out_shape=out_shape,
        in_specs=[any_spec] * (n + len(extra)), out_specs=[any_spec] * n,
        scratch_shapes=[pltpu.SemaphoreType.DMA((n, N_DEV - 1)), pltpu.SemaphoreType.DMA((n, N_DEV - 1)),
                        pltpu.SemaphoreType.DMA((n,))],
    )(*arrs, *extra)


HBM_SPEC = pl.BlockSpec(memory_space=pltpu.HBM)
SEM_SPEC = pl.BlockSpec(memory_space=pltpu.SEMAPHORE)
EFFECT = pltpu.SideEffectType.DATAFLOW_SIDE_EFFECTING


def _split_copies(srcs, lands, send_sems, recv_sems, local_sems, scatter):
    x, y, c = lax.axis_index("x"), lax.axis_index("y"), lax.axis_index("c")
    me = 4 * x + 2 * y + c
    local, remote = [], []
    for a in range(len(srcs)):
        own = srcs[a].at[me] if scatter else srcs[a]
        local.append(pltpu.make_async_copy(own, lands[a].at[me], local_sems.at[a]))
        for r in range(1, N_DEV):
            px = 1 - x if r & 4 else x
            py = 1 - y if r & 2 else y
            pc = 1 - c if r & 1 else c
            src = srcs[a].at[4 * px + 2 * py + pc] if scatter else srcs[a]
            remote.append(pltpu.make_async_remote_copy(
                src_ref=src, dst_ref=lands[a].at[me], send_sem=send_sems.at[a * (N_DEV - 1) + r - 1],
                recv_sem=recv_sems.at[a * (N_DEV - 1) + r - 1], device_id=(px, py, pc), device_id_type=MESH))
    return local, remote


def _exchange_start(name, arrs, scatter, after=None):
    n = len(arrs)
    extra = [] if after is None else [after]
    shapes = [a.shape if scatter else (N_DEV,) + a.shape for a in arrs]
    lands = [pltpu.with_memory_space_constraint(lax.empty(s, a.dtype), pltpu.HBM) for s, a in zip(shapes, arrs)]
    srcs = [pltpu.with_memory_space_constraint(a, pltpu.HBM) for a in arrs]

    def body(*refs):
        src_refs, land_refs = refs[:n], refs[n:2 * n]
        send_sems, recv_sems, local_sems = refs[2 * n + len(extra):2 * n + len(extra) + 3]
        token = refs[-1]
        local, remote = _split_copies(src_refs, land_refs, send_sems, recv_sems, local_sems, scatter)
        for cp in local + remote:
            cp.start()
        token[...] = jnp.zeros_like(token)

    res = pl.pallas_call(
        body, name=name,
        out_shape=[pltpu.SemaphoreType.DMA((n * (N_DEV - 1),)), pltpu.SemaphoreType.DMA((n * (N_DEV - 1),)), pltpu.SemaphoreType.DMA((n,))]
        + [pltpu.HBM(a.shape, a.dtype) for a in arrs] + [pltpu.HBM(s, a.dtype) for s, a in zip(shapes, arrs)]
        + [jax.ShapeDtypeStruct((8, 128), F32)],
        in_specs=[HBM_SPEC] * (2 * n) + [pl.BlockSpec(memory_space=pl.ANY)] * len(extra),
        out_specs=[SEM_SPEC] * 3 + [HBM_SPEC] * (2 * n) + [pl.BlockSpec(memory_space=pltpu.VMEM)],
        input_output_aliases={i: 3 + i for i in range(2 * n)},
        compiler_params=pltpu.CompilerParams(has_side_effects=EFFECT),
    )(*srcs, *lands, *extra)
    return (scatter, res[:3], res[3:3 + n], res[3 + n:3 + 2 * n]), res[-1]


def _exchange_wait(name, handle, after):
    scatter, sems, srcs, lands = handle
    n = len(srcs)

    def body(*refs):
        src_refs, land_refs = refs[:n], refs[n:2 * n]
        send_sems, recv_sems, local_sems = refs[2 * n:2 * n + 3]
        local, remote = _split_copies(src_refs, land_refs, send_sems, recv_sems, local_sems, scatter)
        for cp in remote:
            cp.wait_send()
            cp.wait_recv()
        for cp in local:
            cp.wait()

    res = pl.pallas_call(
        body, name=name,
        out_shape=[pltpu.HBM(a.shape, a.dtype) for a in srcs] + [pltpu.HBM(a.shape, a.dtype) for a in lands],
        in_specs=[HBM_SPEC] * (2 * n) + [SEM_SPEC] * 3 + [pl.BlockSpec(memory_space=pl.ANY)], out_specs=[HBM_SPEC] * (2 * n),
        input_output_aliases={i: i for i in range(2 * n)},
        compiler_params=pltpu.CompilerParams(has_side_effects=EFFECT),
    )(*srcs, *lands, *sems, after)
    return list(res[n:])


FIRST_COPIES = 4
RELAY_COPIES = 3


def _gather_copies(srcs, lands, sems):
    send_sems, recv_sems, local_sems = sems[:3]
    x, y, c = lax.axis_index("x"), lax.axis_index("y"), lax.axis_index("c")
    me = 4 * x + 2 * y + c
    chips = [(1 - x, y), (x, 1 - y), (1 - x, 1 - y)]
    local, first, relay = [], [], []
    for a in range(len(srcs)):
        local.append(pltpu.make_async_copy(srcs[a], lands[a].at[me], local_sems.at[a]))
        targets = [(x, y, 1 - c)] + [(px, py, c) for px, py in chips]
        first.append([pltpu.make_async_remote_copy(
            src_ref=srcs[a], dst_ref=lands[a].at[me], send_sem=send_sems.at[FIRST_COPIES * a + k],
            recv_sem=recv_sems.at[FIRST_COPIES * a + k], device_id=t, device_id_type=MESH) for k, t in enumerate(targets)])
        if len(sems) > 3:
            rsend, rrecv = sems[3:]
            slots = [lands[a].at[4 * px + 2 * py + c] for px, py in chips]
            relay.append([pltpu.make_async_remote_copy(
                src_ref=slot, dst_ref=slot, send_sem=rsend.at[RELAY_COPIES * a + j], recv_sem=rrecv.at[RELAY_COPIES * a + j],
                device_id=(x, y, 1 - c), device_id_type=MESH) for j, slot in enumerate(slots)])
    return local, first, relay


def _gather_start(name, arrs, after=None):
    n = len(arrs)
    extra = [] if after is None else [after]
    lands = [pltpu.with_memory_space_constraint(lax.empty((N_DEV,) + a.shape, a.dtype), pltpu.HBM) for a in arrs]
    srcs = [pltpu.with_memory_space_constraint(a, pltpu.HBM) for a in arrs]

    def body(*refs):
        at = 2 * n + len(extra)
        local, first, _ = _gather_copies(refs[:n], refs[n:2 * n], refs[at:at + 3])
        for cp in local + [cp for cps in first for cp in cps]:
            cp.start()
        refs[-1][...] = jnp.zeros_like(refs[-1])

    res = pl.pallas_call(
        body, name=name,
        out_shape=[pltpu.SemaphoreType.DMA((FIRST_COPIES * n,)), pltpu.SemaphoreType.DMA((FIRST_COPIES * n,)), pltpu.SemaphoreType.DMA((n,))]
        + [pltpu.HBM(a.shape, a.dtype) for a in arrs] + [pltpu.HBM((N_DEV,) + a.shape, a.dtype) for a in arrs]
        + [jax.ShapeDtypeStruct((8, 128), F32)],
        in_specs=[HBM_SPEC] * (2 * n) + [pl.BlockSpec(memory_space=pl.ANY)] * len(extra),
        out_specs=[SEM_SPEC] * 3 + [HBM_SPEC] * (2 * n) + [pl.BlockSpec(memory_space=pltpu.VMEM)],
        input_output_aliases={i: 3 + i for i in range(2 * n)},
        compiler_params=pltpu.CompilerParams(has_side_effects=EFFECT),
    )(*srcs, *lands, *extra)
    return (list(res[:3]), list(res[3:3 + n]), list(res[3 + n:3 + 2 * n])), res[-1]


def _gather_relay(name, handle, after):
    sems, srcs, lands = handle
    n = len(srcs)

    def body(*refs):
        in_sems = refs[2 * n:2 * n + 3]
        out_sems = refs[2 * n + 4 + 2 * n:2 * n + 4 + 2 * n + 2]
        _, first, relay = _gather_copies(refs[:n], refs[n:2 * n], list(in_sems) + list(out_sems))
        for a in range(n):
            for j in range(RELAY_COPIES):
                first[a][1 + j].wait_recv()
                relay[a][j].start()
        refs[-1][...] = jnp.zeros_like(refs[-1])

    res = pl.pallas_call(
        body, name=name,
        out_shape=[pltpu.HBM(a.shape, a.dtype) for a in srcs] + [pltpu.HBM(a.shape, a.dtype) for a in lands]
        + [pltpu.SemaphoreType.DMA((RELAY_COPIES * n,)), pltpu.SemaphoreType.DMA((RELAY_COPIES * n,)), jax.ShapeDtypeStruct((8, 128), F32)],
        in_specs=[HBM_SPEC] * (2 * n) + [SEM_SPEC] * 3 + [pl.BlockSpec(memory_space=pl.ANY)],
        out_specs=[HBM_SPEC] * (2 * n) + [SEM_SPEC] * 2 + [pl.BlockSpec(memory_space=pltpu.VMEM)],
        input_output_aliases={i: i for i in range(2 * n)},
        compiler_params=pltpu.CompilerParams(has_side_effects=EFFECT),
    )(*srcs, *lands, *sems, after)
    return (sems + list(res[2 * n:2 * n + 2]), list(res[:n]), list(res[n:2 * n])), res[-1]


def _gather_wait(name, handle, after):
    sems, srcs, lands = handle
    n = len(srcs)

    def body(*refs):
        local, first, relay = _gather_copies(refs[:n], refs[n:2 * n], refs[2 * n:2 * n + 5])
        for a in range(n):
            for cp in first[a]:
                cp.wait_send()
            first[a][0].wait_recv()
            for cp in relay[a]:
                cp.wait_send()
                cp.wait_recv()
            local[a].wait()

    res = pl.pallas_call(
        body, name=name,
        out_shape=[pltpu.HBM(a.shape, a.dtype) for a in srcs] + [pltpu.HBM(a.shape, a.dtype) for a in lands],
        in_specs=[HBM_SPEC] * (2 * n) + [SEM_SPEC] * 5 + [pl.BlockSpec(memory_space=pl.ANY)], out_specs=[HBM_SPEC] * (2 * n),
        input_output_aliases={i: i for i in range(2 * n)},
        compiler_params=pltpu.CompilerParams(has_side_effects=EFFECT),
    )(*srcs, *lands, *sems, after)
    return list(res[n:])


def _dep(dep, grid_rank):
    if dep is None:
        return [], []
    return [dep], [pl.BlockSpec((8, 128), (lambda i, j: (0, 0)) if grid_rank == 2 else (lambda i: (0, 0)))]


def _mm_step(name, fn, ins, in_specs, out_shape, out_spec, grid, dep=None, res=None):
    n = len(ins)
    dep_ins, dep_specs = _dep(dep, len(grid))
    res_ins, res_specs, out_shapes, out_specs = [], [], out_shape, out_spec
    if res is not None:
        h, mod, row_idx, lc = res
        tm, tn = out_spec.block_shape
        res_ins = [h, mod]
        res_specs = [pl.BlockSpec((tm, tn), lambda j, i: (i, j)), pl.BlockSpec((2, 8, tn), lambda j, i: (0, 0, j))]
        out_shapes, out_specs = [out_shape, jax.ShapeDtypeStruct(h.shape, h.dtype)], [out_spec, res_specs[0]]

    def body(*refs):
        outs = refs[n + len(res_ins) + len(dep_ins):]
        acc = fn(*refs[:n])
        outs[0][...] = acc.astype(outs[0].dtype)
        if res is not None:
            h_ref, mod_ref = refs[n:n + 2]
            row = pl.program_id(1) * tm + lax.broadcasted_iota(jnp.int32, (tm, 1), 0)
            gate = jnp.where(row < lc, mod_ref[0, row_idx:row_idx + 1, :], mod_ref[1, row_idx:row_idx + 1, :])
            outs[1][...] = h_ref[...] + gate * acc

    return pl.pallas_call(body, name=name, grid=grid, in_specs=list(in_specs) + res_specs + dep_specs, out_specs=out_specs,
                          out_shape=out_shapes, compiler_params=_params())(*ins, *res_ins, *dep_ins)


def _mm_tn(name, a, b, a_spec, b_spec, out_shape, out_spec, acc_shape, grid):
    nk = grid[-1]
    kax = len(grid) - 1
    if nk == 1:
        def whole(a_ref, b_ref, o_ref):
            o_ref[...] = _tn(_bf(a_ref[...]), _bf(b_ref[...])).astype(o_ref.dtype)

        return pl.pallas_call(whole, name=name, grid=grid, in_specs=[a_spec, b_spec], out_specs=out_spec,
                              out_shape=out_shape, compiler_params=_params())(a, b)

    def body(a_ref, b_ref, o_ref, acc_ref):
        k = pl.program_id(kax)

        @pl.when(k == 0)
        def _():
            acc_ref[...] = jnp.zeros_like(acc_ref)

        acc_ref[...] += _tn(_bf(a_ref[...]), _bf(b_ref[...]))

        @pl.when(k == nk - 1)
        def _():
            o_ref[...] = acc_ref[...].astype(o_ref.dtype)

    return pl.pallas_call(body, name=name, grid=grid, in_specs=[a_spec, b_spec], out_specs=out_spec,
                          out_shape=out_shape, scratch_shapes=[pltpu.VMEM(acc_shape, F32)],
                          compiler_params=_params())(a, b)


def _mm_cols(name, a, w3, out_dtype=F32, dep=None):
    M, K = a.shape
    J, _, n = w3.shape
    tm = _row_tile(M, 1088)
    return _mm_step(
        name, lambda a_ref, w_ref: _nn(_bf(a_ref[...]), w_ref[...]), [a, w3],
        [pl.BlockSpec((tm, K), lambda j, i: (i, 0)), pl.BlockSpec((None, K, n), lambda j, i: (j, 0, 0))],
        jax.ShapeDtypeStruct((M, J * n), out_dtype), pl.BlockSpec((tm, n), lambda j, i: (i, j)), (J, M // tm), dep)


def _mm_plain(name, a, b, transpose_b, out_dtype=F32, tn=512, dep=None, res=None):
    M, K = a.shape
    N = b.shape[0] if transpose_b else b.shape[1]
    tm = _row_tile(M, 1088)
    tn = _col_tile(N, tn)
    if transpose_b:
        b_spec = pl.BlockSpec((tn, K), lambda j, i: (j, 0))
        fn = lambda a_ref, b_ref: _nt(_bf(a_ref[...]), _bf(b_ref[...]))
    else:
        b_spec = pl.BlockSpec((K, tn), lambda j, i: (0, j))
        fn = lambda a_ref, b_ref: _nn(_bf(a_ref[...]), _bf(b_ref[...]))
    return _mm_step(name, fn, [a, b], [pl.BlockSpec((tm, K), lambda j, i: (i, 0)), b_spec],
                    jax.ShapeDtypeStruct((M, N), out_dtype), pl.BlockSpec((tm, tn), lambda j, i: (i, j)),
                    (N // tn, M // tm), dep, res)


def _mm_shards_nn(name, a3, w3, tn=512, dep=None, res=None):
    J, M, k = a3.shape
    N = w3.shape[2]
    tm = _row_tile(M, 544)
    tn = _col_tile(N, tn)

    def fn(a_ref, w_ref):
        acc = _nn(a_ref[0], w_ref[0])
        for j in range(1, J):
            acc += _nn(a_ref[j], w_ref[j])
        return acc

    return _mm_step(name, fn, [a3, w3],
                    [pl.BlockSpec((J, tm, k), lambda jn, i: (0, i, 0)), pl.BlockSpec((J, k, tn), lambda jn, i: (0, 0, jn))],
                    jax.ShapeDtypeStruct((M, N), F32), pl.BlockSpec((tm, tn), lambda jn, i: (i, jn)), (N // tn, M // tm), dep, res)


def _mm_shards_nn2(name, a3, w3a, b3, w3b, tn=512):
    J, M, k = a3.shape
    N = w3a.shape[2]
    tm = _row_tile(M, 544)
    tn = _col_tile(N, tn)

    def fn(a_ref, wa_ref, b_ref, wb_ref):
        acc = _nn(a_ref[0], wa_ref[0]) + _nn(b_ref[0], wb_ref[0])
        for j in range(1, J):
            acc += _nn(a_ref[j], wa_ref[j]) + _nn(b_ref[j], wb_ref[j])
        return acc

    act = pl.BlockSpec((J, tm, k), lambda jn, i: (0, i, 0))
    wsp = pl.BlockSpec((J, k, tn), lambda jn, i: (0, 0, jn))
    return _mm_step(name, fn, [a3, w3a, b3, w3b], [act, wsp, act, wsp],
                    jax.ShapeDtypeStruct((M, N), F32), pl.BlockSpec((tm, tn), lambda jn, i: (i, jn)), (N // tn, M // tm))


def _mm_cols_nt(name, a, w3, tn=512):
    M = a.shape[0]
    J, N, n = w3.shape
    tm = _row_tile(M, 544)
    tn = _col_tile(N, tn)

    def fn(a_ref, w_ref):
        acc = _nt(a_ref[:, 0:n], w_ref[0])
        for j in range(1, J):
            acc += _nt(a_ref[:, j * n:(j + 1) * n], w_ref[j])
        return acc

    return _mm_step(name, fn, [a, w3],
                    [pl.BlockSpec((tm, J * n), lambda jn, i: (i, 0)), pl.BlockSpec((J, tn, n), lambda jn, i: (0, jn, 0))],
                    jax.ShapeDtypeStruct((M, N), F32), pl.BlockSpec((tm, tn), lambda jn, i: (i, jn)), (N // tn, M // tm))


def _wgrad_cols(name, a, b, J):
    T, K = a.shape
    n = b.shape[1] // J
    tt = _row_tile(T, 1088)
    return _mm_tn(name, a, b, pl.BlockSpec((tt, K), lambda j, t: (t, 0)), pl.BlockSpec((tt, n), lambda j, t: (t, j)),
                  jax.ShapeDtypeStruct((J, K, n), BF16), pl.BlockSpec((None, K, n), lambda j, t: (j, 0, 0)), (K, n), (J, T // tt))


def _wgrad_rows(name, a, b, tk=512):
    T, K = a.shape
    N = b.shape[1]
    tt = _row_tile(T, 1088)
    tk = _col_tile(K, tk)
    return _mm_tn(name, a, b, pl.BlockSpec((tt, tk), lambda kb, t: (t, kb)), pl.BlockSpec((tt, N), lambda kb, t: (t, 0)),
                  jax.ShapeDtypeStruct((K, N), BF16), pl.BlockSpec((tk, N), lambda kb, t: (kb, 0)), (tk, N), (K // tk, T // tt))


def _wgrad_up(name, a, b3):
    T, K = a.shape
    J, _, k = b3.shape
    tt = _row_tile(T, 1088)
    return _mm_tn(name, a, b3, pl.BlockSpec((tt, K), lambda j, t: (t, 0)), pl.BlockSpec((None, tt, k), lambda j, t: (j, t, 0)),
                  jax.ShapeDtypeStruct((J, K, k), BF16), pl.BlockSpec((None, K, k), lambda j, t: (j, 0, 0)), (K, k), (J, T // tt))


def _wgrad_down(name, a3, b):
    J, T, k = a3.shape
    N = b.shape[1]
    return _mm_tn(name, a3, b, pl.BlockSpec((None, T, k), lambda j, t: (j, 0, 0)),
                  pl.BlockSpec((T, N), lambda j, t: (0, 0), pipeline_mode=pl.Buffered(buffer_count=1)),
                  jax.ShapeDtypeStruct((J, k, N), BF16), pl.BlockSpec((None, k, N), lambda j, t: (j, 0, 0)), (k, N), (J, 1))


def _seg(i):
    return jnp.minimum(i, 1)


def _rstd(x):
    return lax.rsqrt(jnp.mean(x * x, axis=-1, keepdims=True) + EPS)


def _norm_mod(name, h, g, mod, which):
    T, D = h.shape

    def body(h_ref, g_ref, mod_ref, o_ref):
        x = h_ref[...]
        n = x * _rstd(x) * g_ref[...]
        shift = mod_ref[3 * which:3 * which + 1, :]
        scale = mod_ref[3 * which + 1:3 * which + 2, :]
        o_ref[...] = (n * (1 + scale) + shift).astype(o_ref.dtype)

    row = pl.BlockSpec((ROW_TILE, D), lambda i: (i, 0))
    return pl.pallas_call(
        body, name=name, grid=(T // ROW_TILE,),
        in_specs=[row, pl.BlockSpec((1, D), lambda i: (0, 0)), pl.BlockSpec((None, 8, D), lambda i: (_seg(i), 0, 0))],
        out_specs=row, out_shape=jax.ShapeDtypeStruct((T, D), BF16), compiler_params=_params())(h, g, mod)


def _norm_mod_bwd(name, dxn, h, g, mod, which, dres, latent_only=False):
    T, D = h.shape

    def body(dxn_ref, h_ref, g_ref, mod_ref, dres_ref, dh_ref, dmod_ref, dg_ref):
        i = pl.program_id(0)
        x = h_ref[...]
        r = _rstd(x)
        xhat = x * r
        g = g_ref[...]
        n = xhat * g
        scale = mod_ref[3 * which + 1:3 * which + 2, :]
        dxn = dxn_ref[...]
        dn = dxn * (1 + scale)
        dxh = dn * g
        dh = dres_ref[...] + r * (dxh - xhat * jnp.mean(dxh * xhat, axis=-1, keepdims=True))
        if latent_only:
            @pl.when(i > 0)
            def _():
                dh_ref[...] = dh
        else:
            dh_ref[...] = dh

        @pl.when(i <= 1)
        def _():
            dmod_ref[...] = jnp.zeros_like(dmod_ref)

        @pl.when(i == 0)
        def _():
            dg_ref[...] = jnp.zeros_like(dg_ref)

        dmod_ref[3 * which:3 * which + 1, :] += jnp.sum(dxn, axis=0, keepdims=True)
        dmod_ref[3 * which + 1:3 * which + 2, :] += jnp.sum(dxn * n, axis=0, keepdims=True)
        dg_ref[0:1, :] += jnp.sum(dn * xhat, axis=0, keepdims=True)

    row = pl.BlockSpec((ROW_TILE, D), lambda i: (i, 0))
    modspec = pl.BlockSpec((None, 8, D), lambda i: (_seg(i), 0, 0))
    dh_rows = T - ROW_TILE if latent_only else T
    dh_spec = pl.BlockSpec((ROW_TILE, D), lambda i: (jnp.maximum(i - 1, 0), 0)) if latent_only else row
    return pl.pallas_call(
        body, name=name, grid=(T // ROW_TILE,),
        in_specs=[row, row, pl.BlockSpec((1, D), lambda i: (0, 0)), modspec, row],
        out_specs=[dh_spec, modspec, pl.BlockSpec((8, D), lambda i: (0, 0))],
        out_shape=[jax.ShapeDtypeStruct((dh_rows, D), F32), jax.ShapeDtypeStruct((2, 8, D), F32), jax.ShapeDtypeStruct((8, D), F32)],
        compiler_params=_params())(dxn, h, g, mod, dres)


def _gate_bwd(name, dh, y, mod, row_idx):
    T, D = dh.shape

    def body(dh_ref, y_ref, mod_ref, dy_ref, dmod_ref):
        i = pl.program_id(0)
        dh = dh_ref[...]
        dy_ref[...] = (dh * mod_ref[row_idx:row_idx + 1, :]).astype(dy_ref.dtype)

        @pl.when(i <= 1)
        def _():
            dmod_ref[...] = jnp.zeros_like(dmod_ref)

        dmod_ref[row_idx:row_idx + 1, :] += jnp.sum(dh * y_ref[...], axis=0, keepdims=True)

    row = pl.BlockSpec((ROW_TILE, D), lambda i: (i, 0))
    modspec = pl.BlockSpec((None, 8, D), lambda i: (_seg(i), 0, 0))
    return pl.pallas_call(
        body, name=name, grid=(T // ROW_TILE,), in_specs=[row, row, modspec], out_specs=[row, modspec],
        out_shape=[jax.ShapeDtypeStruct((T, D), BF16), jax.ShapeDtypeStruct((2, 8, D), F32)],
        compiler_params=_params())(dh, y, mod)


def _rot(y):
    lane = lax.broadcasted_iota(jnp.int32, y.shape, 1)
    return jnp.where((lane & 32) == 0, pltpu.roll(y, 96, 1), pltpu.roll(y, 32, 1))


def _qk_prep(name, P, q_g, k_g, rope_c, rope_s, cfg):
    T = P.shape[0]
    ATT, KVW = cfg['ATT'], cfg['KVW']

    def body(q_ref, k_ref, v_ref, qg_ref, kg_ref, c_ref, s_ref, qo_ref, ko_ref, vo_ref):
        cc, ss = c_ref[...], s_ref[...]

        def head(x, g):
            y = x * _rstd(x) * g
            return y * cc + _rot(y) * ss

        for hh in range(ATT // HEAD_DIM):
            sl = slice(hh * HEAD_DIM, (hh + 1) * HEAD_DIM)
            qo_ref[:, sl] = (head(q_ref[:, sl], qg_ref[...]) * Q_SCALE).astype(qo_ref.dtype)
        for hh in range(KVW // HEAD_DIM):
            sl = slice(hh * HEAD_DIM, (hh + 1) * HEAD_DIM)
            ko_ref[:, sl] = head(k_ref[:, sl], kg_ref[...]).astype(ko_ref.dtype)
        vo_ref[...] = v_ref[...].astype(vo_ref.dtype)

    kb = ATT // KVW
    gain = pl.BlockSpec((1, HEAD_DIM), lambda i: (0, 0))
    tab = pl.BlockSpec((ROW_TILE, HEAD_DIM), lambda i: (i, 0))
    qs = pl.BlockSpec((ROW_TILE, ATT), lambda i: (i, 0))
    ks = pl.BlockSpec((ROW_TILE, KVW), lambda i: (i, 0))
    return pl.pallas_call(
        body, name=name, grid=(T // ROW_TILE,),
        in_specs=[qs, pl.BlockSpec((ROW_TILE, KVW), lambda i: (i, kb)), pl.BlockSpec((ROW_TILE, KVW), lambda i: (i, kb + 1)),
                  gain, gain, tab, tab],
        out_specs=[qs, ks, ks],
        out_shape=[jax.ShapeDtypeStruct((T, ATT), BF16), jax.ShapeDtypeStruct((T, KVW), BF16), jax.ShapeDtypeStruct((T, KVW), BF16)],
        compiler_params=_params())(P, P, P, q_g, k_g, rope_c, rope_s)


def _qk_prep_bwd(name, dqr, dkr, P, q_g, k_g, rope_c, rope_s, cfg):
    T = P.shape[0]
    ATT, KVW = cfg['ATT'], cfg['KVW']

    def body(dq_ref, dk_ref, q_ref, k_ref, qg_ref, kg_ref, c_ref, s_ref, dqo_ref, dko_ref, dqg_ref, dkg_ref):
        i = pl.program_id(0)
        cc, ss = c_ref[...], s_ref[...]

        @pl.when(i == 0)
        def _():
            dqg_ref[...] = jnp.zeros_like(dqg_ref)
            dkg_ref[...] = jnp.zeros_like(dkg_ref)

        def head(x, g, dout):
            dy = dout * cc + _rot(dout * ss)
            r = _rstd(x)
            xhat = x * r
            dxh = dy * g
            dx = r * (dxh - xhat * jnp.mean(dxh * xhat, axis=-1, keepdims=True))
            return dx, jnp.sum(dy * xhat, axis=0, keepdims=True)

        dg = jnp.zeros((1, HEAD_DIM), F32)
        for hh in range(ATT // HEAD_DIM):
            sl = slice(hh * HEAD_DIM, (hh + 1) * HEAD_DIM)
            dx, d = head(q_ref[:, sl], qg_ref[...], dq_ref[:, sl] * ATT_SCALE)
            dqo_ref[:, sl] = dx.astype(dqo_ref.dtype)
            dg += d
        dqg_ref[0:1, :] += dg
        dg = jnp.zeros((1, HEAD_DIM), F32)
        for hh in range(KVW // HEAD_DIM):
            sl = slice(hh * HEAD_DIM, (hh + 1) * HEAD_DIM)
            dx, d = head(k_ref[:, sl], kg_ref[...], dk_ref[:, sl] * (1.0 / LOG2_E))
            dko_ref[:, sl] = dx.astype(dko_ref.dtype)
            dg += d
        dkg_ref[0:1, :] += dg

    kb = ATT // KVW
    gain = pl.BlockSpec((1, HEAD_DIM), lambda i: (0, 0))
    dgain = pl.BlockSpec((8, HEAD_DIM), lambda i: (0, 0))
    tab = pl.BlockSpec((ROW_TILE, HEAD_DIM), lambda i: (i, 0))
    qs = pl.BlockSpec((ROW_TILE, ATT), lambda i: (i, 0))
    ks = pl.BlockSpec((ROW_TILE, KVW), lambda i: (i, 0))
    return pl.pallas_call(
        body, name=name, grid=(T // ROW_TILE,),
        in_specs=[qs, ks, qs, pl.BlockSpec((ROW_TILE, KVW), lambda i: (i, kb)), gain, gain, tab, tab],
        out_specs=[qs, ks, dgain, dgain],
        out_shape=[jax.ShapeDtypeStruct((T, ATT), BF16), jax.ShapeDtypeStruct((T, KVW), BF16),
                   jax.ShapeDtypeStruct((8, HEAD_DIM), F32), jax.ShapeDtypeStruct((8, HEAD_DIM), F32)],
        compiler_params=_params())(dqr, dkr, P, P, q_g, k_g, rope_c, rope_s)


def _att_specs(T, G):
    qs = pl.BlockSpec((ROW_TILE, G * HEAD_DIM), lambda h, i: (i, h))
    kvs = pl.BlockSpec((T, HEAD_DIM), lambda h, i: (0, h))
    return qs, kvs


def _attn_dense_fwd(name, q, k, v, cfg):
    T, G, Lc = q.shape[0], cfg['G'], cfg['Lc']

    def body(q_ref, k_ref, v_ref, o_ref, lse_ref):
        def attend(rows):
            kk, vv = k_ref[0:rows, :], v_ref[0:rows, :]
            for g in range(G):
                sl = slice(g * HEAD_DIM, (g + 1) * HEAD_DIM)
                s = _nt(q_ref[:, sl], kk)
                m = jnp.max(s, axis=1, keepdims=True)
                p = jnp.exp2(s - m)
                l = jnp.sum(p, axis=1, keepdims=True)
                o_ref[:, sl] = _nn(_bf(p), vv) / l
                lse_ref[:, sl] = jnp.broadcast_to(m + jnp.log2(l), (ROW_TILE, HEAD_DIM))

        @pl.when(pl.program_id(1) == 0)
        def _():
            attend(Lc)

        @pl.when(pl.program_id(1) > 0)
        def _():
            attend(T)

    qs, kvs = _att_specs(T, G)
    return pl.pallas_call(
        body, name=name, grid=(cfg['NKV'], T // ROW_TILE), in_specs=[qs, kvs, kvs], out_specs=[qs, qs],
        out_shape=[jax.ShapeDtypeStruct(q.shape, F32), jax.ShapeDtypeStruct(q.shape, F32)],
        compiler_params=_params())(q, k, v)


def _attn_dense_bwd(name, q, k, v, o, lse, dmix, cfg):
    T, G, Lc = q.shape[0], cfg['G'], cfg['Lc']

    def body(q_ref, k_ref, v_ref, o_ref, lse_ref, do_ref, dq_ref, dk_ref, dv_ref):
        i = pl.program_id(1)

        @pl.when(i == 0)
        def _():
            dk_ref[...] = jnp.zeros_like(dk_ref)
            dv_ref[...] = jnp.zeros_like(dv_ref)

        def attend(rows):
            kk, vv = k_ref[0:rows, :], v_ref[0:rows, :]
            for g in range(G):
                sl = slice(g * HEAD_DIM, (g + 1) * HEAD_DIM)
                qg, do = q_ref[:, sl], do_ref[:, sl]
                delta = jnp.sum(do * o_ref[:, sl], axis=1, keepdims=True)
                p = jnp.exp2(_nt(qg, kk) - lse_ref[:, g * HEAD_DIM:g * HEAD_DIM + 1])
                dob = _bf(do)
                dv_ref[0:rows, :] += _tn(_bf(p), dob)
                ds = _bf(p * (_nt(dob, vv) - delta))
                dq_ref[:, sl] = _nn(ds, kk)
                dk_ref[0:rows, :] += _tn(ds, qg)

        @pl.when(i == 0)
        def _():
            attend(Lc)

        @pl.when(i > 0)
        def _():
            attend(T)

    qs, kvs = _att_specs(T, G)
    return pl.pallas_call(
        body, name=name, grid=(cfg['NKV'], T // ROW_TILE), in_specs=[qs, kvs, kvs, qs, qs, qs], out_specs=[qs, kvs, kvs],
        out_shape=[jax.ShapeDtypeStruct(q.shape, F32), jax.ShapeDtypeStruct(k.shape, F32), jax.ShapeDtypeStruct(k.shape, F32)],
        compiler_params=_params())(q, k, v, o, lse, dmix)


def _band(i, T, Lc):
    start = pl.multiple_of(jnp.clip(WINDOW + (i - 1) * ROW_TILE, 0, T - BAND), WINDOW)
    qpos = (i - 1) * ROW_TILE + lax.broadcasted_iota(jnp.int32, (ROW_TILE, 1), 0)
    kpos = start - Lc + lax.broadcasted_iota(jnp.int32, (1, BAND), 1)
    ok = (jnp.abs(kpos - qpos) <= WINDOW) & (kpos >= 0) & (i > 0)
    return start, jnp.where(ok, 0.0, NEG_INF).astype(F32)


def _attn_win_fwd(name, q, k, v, sink, cfg):
    T, G, Lc = q.shape[0], cfg['G'], cfg['Lc']

    def body(sink_ref, q_ref, k_ref, v_ref, o_ref, lse_ref):
        h, i = pl.program_id(0), pl.program_id(1)
        start, bias = _band(i, T, Lc)
        kc, vc = k_ref[0:Lc, :], v_ref[0:Lc, :]
        kb, vb = k_ref[pl.ds(start, BAND), :], v_ref[pl.ds(start, BAND), :]
        for g in range(G):
            sl = slice(g * HEAD_DIM, (g + 1) * HEAD_DIM)
            qg = q_ref[:, sl]
            sk = sink_ref[h * G + g] * LOG2_E
            sc = _nt(qg, kc)
            sb = _nt(qg, kb) + bias
            m = jnp.maximum(jnp.maximum(jnp.max(sc, axis=1, keepdims=True), jnp.max(sb, axis=1, keepdims=True)), sk)
            pc, pb = jnp.exp2(sc - m), jnp.exp2(sb - m)
            l = jnp.sum(pc, axis=1, keepdims=True) + jnp.sum(pb, axis=1, keepdims=True) + jnp.exp2(sk - m)
            o_ref[:, sl] = (_nn(_bf(pc), vc) + _nn(_bf(pb), vb)) / l
            lse_ref[:, sl] = jnp.broadcast_to(m + jnp.log2(l), (ROW_TILE, HEAD_DIM))

    qs, kvs = _att_specs(T, G)
    return pl.pallas_call(
        body, name=name, grid=(cfg['NKV'], T // ROW_TILE),
        in_specs=[pl.BlockSpec(memory_space=pltpu.SMEM), qs, kvs, kvs], out_specs=[qs, qs],
        out_shape=[jax.ShapeDtypeStruct(q.shape, F32), jax.ShapeDtypeStruct(q.shape, F32)],
        compiler_params=_params())(sink, q, k, v)


def _attn_win_bwd(name, q, k, v, o, lse, dmix, sink, cfg):
    T, G, Lc = q.shape[0], cfg['G'], cfg['Lc']

    def body(sink_ref, q_ref, k_ref, v_ref, o_ref, lse_ref, do_ref, dq_ref, dk_ref, dv_ref, dsink_ref):
        h, i = pl.program_id(0), pl.program_id(1)
        start, bias = _band(i, T, Lc)
        kc, vc = k_ref[0:Lc, :], v_ref[0:Lc, :]
        kb, vb = k_ref[pl.ds(start, BAND), :], v_ref[pl.ds(start, BAND), :]

        @pl.when(i == 0)
        def _():
            dk_ref[...] = jnp.zeros_like(dk_ref)
            dv_ref[...] = jnp.zeros_like(dv_ref)
            dsink_ref[...] = jnp.zeros_like(dsink_ref)

        for g in range(G):
            sl = slice(g * HEAD_DIM, (g + 1) * HEAD_DIM)
            qg, do = q_ref[:, sl], do_ref[:, sl]
            lse = lse_ref[:, g * HEAD_DIM:g * HEAD_DIM + 1]
            delta = jnp.sum(do * o_ref[:, sl], axis=1, keepdims=True)
            pc = jnp.exp2(_nt(qg, kc) - lse)
            pb = jnp.exp2(_nt(qg, kb) + bias - lse)
            ps = jnp.exp2(sink_ref[h * G + g] * LOG2_E - lse)
            dob = _bf(do)
            dv_ref[0:Lc, :] += _tn(_bf(pc), dob)
            dv_ref[pl.ds(start, BAND), :] += _tn(_bf(pb), dob)
            dsc = _bf(pc * (_nt(dob, vc) - delta))
            dsb = _bf(pb * (_nt(dob, vb) - delta))
            dq_ref[:, sl] = _nn(dsc, kc) + _nn(dsb, kb)
            dk_ref[0:Lc, :] += _tn(dsc, qg)
            dk_ref[pl.ds(start, BAND), :] += _tn(dsb, qg)
            dsk = jnp.where(i > 0, -jnp.sum(ps * delta, axis=0, keepdims=True), 0.0)
            dsink_ref[:, sl] += jnp.broadcast_to(dsk, (8, HEAD_DIM))

    qs, kvs = _att_specs(T, G)
    return pl.pallas_call(
        body, name=name, grid=(cfg['NKV'], T // ROW_TILE),
        in_specs=[pl.BlockSpec(memory_space=pltpu.SMEM), qs, kvs, kvs, qs, qs, qs],
        out_specs=[qs, kvs, kvs, pl.BlockSpec((None, 8, G * HEAD_DIM), lambda h, i: (h, 0, 0))],
        out_shape=[jax.ShapeDtypeStruct(q.shape, F32), jax.ShapeDtypeStruct(k.shape, F32), jax.ShapeDtypeStruct(k.shape, F32),
                   jax.ShapeDtypeStruct((cfg['NKV'], 8, G * HEAD_DIM), F32)],
        compiler_params=_params())(sink, q, k, v, o, lse, dmix)


def _seq_pos(T, Lc):
    row = lax.broadcasted_iota(jnp.int32, (T, 1), 0)
    return jnp.where(row < Lc, row, row - Lc), jnp.where(row < Lc, Lc, T - Lc)


def _fw(x, k, pos, seglen):
    return jnp.where(pos + k < seglen, pltpu.roll(x, x.shape[0] - k, 0), 0.0)


def _bw(x, k, pos):
    return jnp.where(pos - k >= 0, pltpu.roll(x, k, 0), 0.0)


def _conv_fwd(name, P, conv_w8, cfg):
    T, Lc = P.shape[0], cfg['Lc']
    cb = (cfg['ATT'] + 2 * cfg['KVW']) // HEAD_DIM
    na = AUX_WIDTH // HEAD_DIM

    def body(gb_ref, gc_ref, u_ref, w_ref, o_ref):
        pos, seglen = _seq_pos(T, Lc)
        z = gc_ref[...] * u_ref[...]
        conv = w_ref[0:1, :] * _bw(z, 1, pos) + w_ref[1:2, :] * z + w_ref[2:3, :] * _fw(z, 1, pos, seglen)
        o_ref[...] = gb_ref[...] * conv

    col = lambda off: pl.BlockSpec((T, HEAD_DIM), lambda c: (0, cb + off + c))
    return pl.pallas_call(
        body, name=name, grid=(na,),
        in_specs=[col(0), col(na), col(2 * na), pl.BlockSpec((8, HEAD_DIM), lambda c: (0, c))],
        out_specs=pl.BlockSpec((T, HEAD_DIM), lambda c: (0, c)),
        out_shape=jax.ShapeDtypeStruct((T, AUX_WIDTH), F32), compiler_params=_params())(P, P, P, conv_w8)


def _conv_bwd(name, P, conv_w8, dmix, cfg):
    T, Lc = P.shape[0], cfg['Lc']
    cb = (cfg['ATT'] + 2 * cfg['KVW']) // HEAD_DIM
    ob = cfg['ATT'] // HEAD_DIM
    na = AUX_WIDTH // HEAD_DIM

    def body(gb_ref, gc_ref, u_ref, w_ref, do_ref, dgb_ref, dgc_ref, du_ref, dw_ref):
        pos, seglen = _seq_pos(T, Lc)
        gc, u, do = gc_ref[...], u_ref[...], do_ref[...]
        z = gc * u
        zm, zp = _bw(z, 1, pos), _fw(z, 1, pos, seglen)
        w0, w1, w2 = w_ref[0:1, :], w_ref[1:2, :], w_ref[2:3, :]
        dgb_ref[...] = (do * (w0 * zm + w1 * z + w2 * zp)).astype(dgb_ref.dtype)
        dc = do * gb_ref[...]
        dz = w0 * _fw(dc, 1, pos, seglen) + w1 * dc + w2 * _bw(dc, 1, pos)
        dgc_ref[...] = (dz * u).astype(dgc_ref.dtype)
        du_ref[...] = (dz * gc).astype(du_ref.dtype)
        dw_ref[...] = jnp.zeros_like(dw_ref)
        dw_ref[0:1, :] = jnp.sum(dc * zm, axis=0, keepdims=True)
        dw_ref[1:2, :] = jnp.sum(dc * z, axis=0, keepdims=True)
        dw_ref[2:3, :] = jnp.sum(dc * zp, axis=0, keepdims=True)

    col = lambda off: pl.BlockSpec((T, HEAD_DIM), lambda c: (0, cb + off + c))
    wspec = pl.BlockSpec((8, HEAD_DIM), lambda c: (0, c))
    ocol = lambda off: pl.BlockSpec((T, HEAD_DIM), lambda c: (0, off + c))
    return pl.pallas_call(
        body, name=name, grid=(na,),
        in_specs=[col(0), col(na), col(2 * na), wspec, ocol(ob)],
        out_specs=[ocol(0), ocol(0), ocol(0), wspec],
        out_shape=[jax.ShapeDtypeStruct((T, AUX_WIDTH), BF16)] * 3 + [jax.ShapeDtypeStruct((8, AUX_WIDTH), F32)],
        compiler_params=_params())(P, P, P, conv_w8, dmix)


def _window_sums(x, half, pos, seglen):
    fwd, bwd = x, x
    s = 1
    while s < half:
        fwd = fwd + _fw(fwd, s, pos, seglen)
        bwd = bwd + _bw(bwd, s, pos)
        s *= 2
    return fwd, bwd


def _pooled(u, half, pos, seglen):
    fwd, bwd = _window_sums(u, half, pos, seglen)
    cnt = (jnp.minimum(pos + half, seglen) - jnp.maximum(pos - half, 0)).astype(F32)
    return (fwd + _bw(bwd, 1, pos)) / cnt - u, cnt


def _pool_fwd(name, P, pool_w, pool_scale, cfg):
    T, Lc = P.shape[0], cfg['Lc']
    cb = (cfg['ATT'] + 2 * cfg['KVW']) // HEAD_DIM

    def body(u_ref, w_ref, s_ref, o_ref):
        g = pl.program_id(0)
        pos, seglen = _seq_pos(T, Lc)
        for k, half in enumerate(POOL_HALF):
            @pl.when(g == k)
            def _(half=half):
                pooled, _ = _pooled(u_ref[...], half, pos, seglen)
                o_ref[...] = _nn(_bf(pooled), _bf(w_ref[...])) * s_ref[...]

    return pl.pallas_call(
        body, name=name, grid=(AUX_GROUPS,),
        in_specs=[pl.BlockSpec((T, HEAD_DIM), lambda g: (0, cb + g)), pl.BlockSpec((None, HEAD_DIM, HEAD_DIM), lambda g: (g, 0, 0)),
                  pl.BlockSpec((1, HEAD_DIM), lambda g: (0, g))],
        out_specs=pl.BlockSpec((T, HEAD_DIM), lambda g: (0, g)),
        out_shape=jax.ShapeDtypeStruct((T, AUX_WIDTH), F32), compiler_params=_params())(P, pool_w, pool_scale)


def _pool_bwd(name, P, pool_w, pool_scale, dmix, cfg):
    T, Lc = P.shape[0], cfg['Lc']
    cb = (cfg['ATT'] + 2 * cfg['KVW']) // HEAD_DIM
    ob = cfg['ATT'] // HEAD_DIM

    def body(u_ref, w_ref, s_ref, do_ref, du_ref, dw_ref, ds_ref):
        g = pl.program_id(0)
        pos, seglen = _seq_pos(T, Lc)
        for k, half in enumerate(POOL_HALF):
            @pl.when(g == k)
            def _(half=half):
                do = do_ref[...]
                pooled, cnt = _pooled(u_ref[...], half, pos, seglen)
                wb = _bf(w_ref[...])
                mixed = _nn(_bf(pooled), wb)
                ds_ref[...] = jnp.broadcast_to(jnp.sum(do * mixed, axis=0, keepdims=True), ds_ref.shape)
                dmixed = _bf(do * s_ref[...])
                dw_ref[...] = _tn(_bf(pooled), dmixed)
                dpooled = _nt(dmixed, wb)
                e = dpooled / cnt
                fwd, bwd = _window_sums(e, half, pos, seglen)
                adj = fwd + _fw(e, half, pos, seglen) + _bw(bwd, 1, pos) - _bw(e, half, pos)
                du_ref[...] = (adj - dpooled).astype(du_ref.dtype)

    wspec = pl.BlockSpec((None, HEAD_DIM, HEAD_DIM), lambda g: (g, 0, 0))
    return pl.pallas_call(
        body, name=name, grid=(AUX_GROUPS,),
        in_specs=[pl.BlockSpec((T, HEAD_DIM), lambda g: (0, cb + g)), wspec, pl.BlockSpec((1, HEAD_DIM), lambda g: (0, g)),
                  pl.BlockSpec((T, HEAD_DIM), lambda g: (0, ob + g))],
        out_specs=[pl.BlockSpec((T, HEAD_DIM), lambda g: (0, g)), wspec, pl.BlockSpec((8, HEAD_DIM), lambda g: (0, g))],
        out_shape=[jax.ShapeDtypeStruct((T, AUX_WIDTH), BF16), jax.ShapeDtypeStruct(pool_w.shape, F32),
                   jax.ShapeDtypeStruct((8, AUX_WIDTH), F32)],
        compiler_params=_params())(P, pool_w, pool_scale, dmix)


def _ffn_up(name, hn, wg3, wu3, dep=None):
    T, D = hn.shape
    J, k, _ = wg3.shape
    tm = _row_tile(T, 1088)
    dep_ins, dep_specs = _dep(dep, 2)

    def body(x_ref, wg_ref, wu_ref, *rest):
        s_ref, ud_ref, a_ref = rest[len(dep_ins):]
        x = x_ref[...]
        g, u = _nt(x, wg_ref[...]), _nt(x, wu_ref[...])
        sig = jax.nn.sigmoid(g)
        silu = g * sig
        s_ref[...] = silu.astype(s_ref.dtype)
        ud_ref[...] = (u * (sig * (1 + g * (1 - sig)))).astype(ud_ref.dtype)
        a_ref[...] = (silu * u).astype(a_ref.dtype)

    wspec = pl.BlockSpec((None, k, D), lambda j, i: (j, 0, 0))
    ospec = pl.BlockSpec((None, tm, k), lambda j, i: (j, i, 0))
    return pl.pallas_call(
        body, name=name, grid=(J, T // tm), in_specs=[pl.BlockSpec((tm, D), lambda j, i: (i, 0)), wspec, wspec] + dep_specs,
        out_specs=[ospec, ospec, ospec],
        out_shape=[jax.ShapeDtypeStruct((J, T, k), BF16)] * 3,
        compiler_params=_params())(hn, wg3, wu3, *dep_ins)


def _ffn_dact(name, dF, wd3, silu_g, u_dsilu):
    T, D = dF.shape
    J, k, _ = wd3.shape
    tm = _row_tile(T, 1088)

    def body(df_ref, wd_ref, s_ref, ud_ref, dg_ref, du_ref):
        da = _nt(df_ref[...], wd_ref[...])
        du_ref[...] = (da * s_ref[...].astype(F32)).astype(du_ref.dtype)
        dg_ref[...] = (da * ud_ref[...].astype(F32)).astype(dg_ref.dtype)

    aspec = pl.BlockSpec((None, tm, k), lambda j, i: (j, i, 0))
    return pl.pallas_call(
        body, name=name, grid=(J, T // tm),
        in_specs=[pl.BlockSpec((tm, D), lambda j, i: (i, 0)), pl.BlockSpec((None, k, D), lambda j, i: (j, 0, 0)), aspec, aspec],
        out_specs=[aspec, aspec],
        out_shape=[jax.ShapeDtypeStruct((J, T, k), BF16), jax.ShapeDtypeStruct((J, T, k), BF16)],
        compiler_params=_params())(dF, wd3, silu_g, u_dsilu)


def _loss_head(name, h, g, target, cfg):
    T, D = h.shape

    def body(h_ref, g_ref, t_ref, dh_ref, loss_ref, dg_ref):
        i = pl.program_id(0)

        @pl.when(i == 0)
        def _():
            dh_ref[...] = jnp.zeros_like(dh_ref)
            loss_ref[...] = jnp.zeros_like(loss_ref)
            dg_ref[...] = jnp.zeros_like(dg_ref)

        @pl.when(i > 0)
        def _():
            x = h_ref[...]
            r = _rstd(x)
            xhat = x * r
            gg = g_ref[...]
            err = xhat * gg - t_ref[...]
            loss_ref[...] += 0.5 * jnp.sum(jnp.sum(err * err, axis=1, keepdims=True) / D, axis=0, keepdims=True)
            dy = err / D
            dg_ref[0:1, :] += jnp.sum(dy * xhat, axis=0, keepdims=True)
            dxh = dy * gg
            dh_ref[...] = r * (dxh - xhat * jnp.mean(dxh * xhat, axis=-1, keepdims=True))

    row = pl.BlockSpec((ROW_TILE, D), lambda i: (i, 0))
    return pl.pallas_call(
        body, name=name, grid=(T // ROW_TILE,),
        in_specs=[row, pl.BlockSpec((1, D), lambda i: (0, 0)), pl.BlockSpec((ROW_TILE, D), lambda i: (jnp.maximum(i - 1, 0), 0))],
        out_specs=[row, pl.BlockSpec((8, 128), lambda i: (0, 0)), pl.BlockSpec((8, D), lambda i: (0, 0))],
        out_shape=[jax.ShapeDtypeStruct((T, D), F32), jax.ShapeDtypeStruct((8, 128), F32), jax.ShapeDtypeStruct((8, D), F32)],
        compiler_params=_params())(h, g, target)


def _adamw(name, parts, w, m, v, dep=None):
    R, C = w.shape
    n_parts = parts.shape[0]
    tr = _row_tile(R, max(16, (1 << 18) // C)) if R % 16 == 0 else R
    bc1 = 1.0 - ADAM_B1 ** ADAM_STEP
    bc2 = 1.0 - ADAM_B2 ** ADAM_STEP
    dep_ins, dep_specs = _dep(dep, 1)

    def body(p_ref, w_ref, m_ref, v_ref, *rest):
        g_ref, d_ref, nm_ref, nv_ref = rest[len(dep_ins):]
        g = p_ref[0].astype(F32)
        for k in range(1, n_parts):
            g = g + p_ref[k].astype(F32)
        nm = ADAM_B1 * m_ref[...] + (1.0 - ADAM_B1) * g
        nv = ADAM_B2 * v_ref[...] + (1.0 - ADAM_B2) * (g * g)
        g_ref[...] = g
        nm_ref[...] = nm
        nv_ref[...] = nv
        d_ref[...] = -ADAM_LR * ((nm / bc1) / (jnp.sqrt(nv / bc2) + ADAM_EPS) + ADAM_WD * w_ref[...])

    blk = pl.BlockSpec((tr, C), lambda i: (i, 0))
    return pl.pallas_call(
        body, name=name, grid=(R // tr,), in_specs=[pl.BlockSpec((n_parts, tr, C), lambda i: (0, i, 0)), blk, blk, blk] + dep_specs,
        out_specs=[blk] * 4, out_shape=[jax.ShapeDtypeStruct((R, C), F32)] * 4, compiler_params=_params())(parts, w, m, v, *dep_ins)


class _WeightStream:
    def __init__(self, cast):
        self.cast, self.handles = cast, {}

    @staticmethod
    def _tag(l, group):
        return ("ffn" if group is FFN_WEIGHTS else group[0]) + str(l)

    def start(self, l, group, after=None):
        self.handles[l, group], token = _gather_start(f"gather_{self._tag(l, group)}_start", [self.cast(l, n) for n in group], after)
        return token

    def relay(self, l, group, after):
        self.handles[l, group], token = _gather_relay(f"gather_{self._tag(l, group)}_relay", self.handles[l, group], after)
        return token

    def get(self, l, group, after):
        got = dict(zip(group, _gather_wait(f"gather_{self._tag(l, group)}_wait", self.handles[l, group], after)))
        if 'w_out' in got:
            rows, cols = got['w_out'].shape[1:]
            got['w_out'] = got['w_out'].reshape(N_DEV * rows, cols)
        return got


def _layer_fwd(l, h, p, stream, mod, rope, conv_w8, cfg):
    nm = f"l{l}_"
    mod = mod + stream.relay(l, IN_WEIGHT, h)[0, 0]
    xn = _norm_mod(nm + "norm1", h, p['norm1_g'], mod, 0)
    W = stream.get(l, IN_WEIGHT, xn)
    token = None
    if l == 0:
        token = stream.start(0, OUT_WEIGHT, after=W['w_in']) + stream.start(0, FFN_WEIGHTS, after=W['w_in'])
    P = _mm_cols(nm + "w_in", xn, W['w_in'], dep=token)
    qr, kr, vb = _qk_prep(nm + "qk_prep", P, p['q_norm_g'], p['k_norm_g'], rope[0], rope[1], cfg)
    if l == 0:
        o, lse = _attn_dense_fwd(nm + "attn", qr, kr, vb, cfg)
        aux = _conv_fwd(nm + "conv", P, conv_w8, cfg)
    else:
        o, lse = _attn_win_fwd(nm + "attn", qr, kr, vb, p['sink'], cfg)
        aux = _pool_fwd(nm + "pool", P, p['pool_w'], p['pool_scale'], cfg)
    mix = jnp.concatenate([o, aux], axis=1).astype(BF16)
    W.update(stream.get(l, OUT_WEIGHT, stream.relay(l, OUT_WEIGHT, mix)))
    y, h2 = _mm_plain(nm + "w_out", mix, W['w_out'], False, dep=stream.relay(l, FFN_WEIGHTS, o), res=(h, mod, 2, cfg['Lc']))
    hn = _norm_mod(nm + "norm2", h2, p['norm2_g'], mod, 1)
    W.update(stream.get(l, FFN_WEIGHTS, hn))
    token = stream.start(1, IN_WEIGHT, after=W['w_down']) if l == 0 else None
    silu_g, u_dsilu, A = _ffn_up(nm + "ffn_up", hn, W['w_gate'], W['w_up'], dep=token)
    if l == 0:
        token = stream.start(1, OUT_WEIGHT, after=A) + stream.start(1, FFN_WEIGHTS, after=A)
    F, h3 = _mm_shards_nn(nm + "w_down", A, W['w_down'], dep=token, res=(h2, mod, 5, cfg['Lc']))
    saved = dict(h=h, xn=xn, P=P, qr=qr, kr=kr, vb=vb, o=o, lse=lse, mix=mix, y=y, h2=h2, hn=hn, silu_g=silu_g, u_dsilu=u_dsilu, A=A, F=F)
    return h3, saved, W


def _layer_bwd(l, dh3, s, p, W, mod, rope, conv_w8, cfg):
    nm = f"l{l}_bwd_"
    J = N_DEV
    dF, dmod = _gate_bwd(nm + "res2", dh3, s['F'], mod, 5)
    dG, dU = _ffn_dact(nm + "ffn_act", dF, W['w_down'], s['silu_g'], s['u_dsilu'])
    big = {'w_down': _wgrad_down(nm + "dw_down", s['A'], dF),
           'w_gate': _wgrad_down(nm + "dw_gate", dG, s['hn']),
           'w_up': _wgrad_down(nm + "dw_up", dU, s['hn'])}
    handles = {}
    handles['ffn'], token = _exchange_start(f"scatter_ffn{l}_start", [big[n] for n in FFN_WEIGHTS], True)
    mod = mod + token[0, 0]
    dhn = _mm_shards_nn2(nm + "dhn", dG, W['w_gate'], dU, W['w_up'])
    dh2, dm, dg2 = _norm_mod_bwd(nm + "norm2", dhn, s['h2'], p['norm2_g'], mod, 1, dh3)
    dmod += dm
    dY, dm = _gate_bwd(nm + "res1", dh2, s['y'], mod, 2)
    dmod += dm
    dwo = _wgrad_rows(nm + "dw_out", s['mix'], dY)
    handles['w_out'], token = _exchange_start(f"scatter_w_out{l}_start", [dwo.reshape((J, dwo.shape[0] // J, dwo.shape[1]))], True)
    dmix = _mm_plain(nm + "dmix", dY, W['w_out'], True, dep=token)
    small = {'norm2_g': dg2[0]}
    if l == 0:
        dqr, dkr, dv = _attn_dense_bwd(nm + "attn", s['qr'], s['kr'], s['vb'], s['o'], s['lse'], dmix, cfg)
        *daux, dcw = _conv_bwd(nm + "conv", s['P'], conv_w8, dmix, cfg)
        small['conv_w'] = dcw[0:3]
    else:
        dqr, dkr, dv, dsk = _attn_win_bwd(nm + "attn", s['qr'], s['kr'], s['vb'], s['o'], s['lse'], dmix, p['sink'], cfg)
        du, dpw, dps = _pool_bwd(nm + "pool", s['P'], p['pool_w'], p['pool_scale'], dmix, cfg)
        daux = [du]
        small.update(sink=dsk[:, 0, ::HEAD_DIM].reshape(-1), pool_w=dpw, pool_scale=dps[0])
    dq, dk, dqg, dkg = _qk_prep_bwd(nm + "qk_prep", dqr, dkr, s['P'], p['q_norm_g'], p['k_norm_g'], rope[0], rope[1], cfg)
    small.update(q_norm_g=dqg[0], k_norm_g=dkg[0])
    dP = jnp.concatenate([dq, dk, dv.astype(BF16), *daux], axis=1)
    handles['w_in'], token = _exchange_start(f"scatter_w_in{l}_start", [_wgrad_cols(nm + "dw_in", s['xn'], dP, J)], True)
    mod = mod + token[0, 0]
    dxn = _mm_cols_nt(nm + "dxn", dP, W['w_in'])
    dh, dm, dg1 = _norm_mod_bwd(nm + "norm1", dxn, s['h'], p['norm1_g'], mod, 0, dh2, latent_only=(l == 0))
    dmod += dm
    small['norm1_g'] = dg1[0]
    return dh, dmod, small, handles, token


def _rope_tables(S, Lc):
    half = HEAD_DIM // 4
    pos = np.arange(S)
    inv = ROPE_THETA ** (-np.arange(0, 2 * half, 2, dtype=np.float32) / (2 * half))
    inv = jnp.asarray(inv, F32)
    ang_r = jnp.asarray(pos // GRID_W, F32)[:, None] * inv
    ang_c = jnp.asarray(pos % GRID_W, F32)[:, None] * inv
    cos = jnp.concatenate([jnp.cos(ang_r)] * 2 + [jnp.cos(ang_c)] * 2, axis=1)
    sin = jnp.concatenate([-jnp.sin(ang_r), jnp.sin(ang_r), -jnp.sin(ang_c), jnp.sin(ang_c)], axis=1)
    return (jnp.concatenate([jnp.ones((Lc, HEAD_DIM), F32), cos], axis=0),
            jnp.concatenate([jnp.zeros((Lc, HEAD_DIM), F32), sin], axis=0))


def _pad_rows(a, rows):
    return jnp.concatenate([a, jnp.zeros((rows - a.shape[0],) + a.shape[1:], a.dtype)], axis=0)


def _flat128(a, nlead):
    lead = a.shape[:nlead]
    f = a.reshape(lead + (-1,))
    pad = (-f.shape[-1]) % 128
    if pad:
        f = jnp.concatenate([f, jnp.zeros(lead + (pad,), f.dtype)], axis=-1)
    return f.reshape(lead + (-1, 128))


def _pack(named, nlead=0):
    rows, layout, at = [], {}, 0
    for name, a in named:
        f = _flat128(a, nlead)
        n = f.shape[-2]
        pad = (-n) % 8
        if pad:
            f = jnp.concatenate([f, jnp.zeros(f.shape[:-2] + (pad, 128), f.dtype)], axis=-2)
        layout[name] = (at, n, a.shape[nlead:])
        rows.append(f)
        at += n + pad
    return jnp.concatenate(rows, axis=-2), layout


def _unpack(arr, layout, name):
    at, n, shape = layout[name]
    return arr[..., at:at + n, :].reshape(arr.shape[:-2] + (-1,))[..., :math.prod(shape)].reshape(arr.shape[:-2] + tuple(shape))


def kernel(*args):
    A = dict(zip(INPUT_NAMES, args, strict=True))
    x, ctx = A['x'][0], A['ctx'][0]
    S, D = x.shape
    Lc = ctx.shape[0]
    T = Lc + S
    ATT = D - AUX_WIDTH
    KVW = (A['l1_w_in'].shape[1] * N_DEV - ATT - AUX_WIDTH) // 2
    cfg = dict(ATT=ATT, KVW=KVW, NKV=KVW // HEAD_DIM, G=ATT // KVW, Lc=Lc)
    assert Lc == ROW_TILE and S % ROW_TILE == 0 and T >= BAND and S % GRID_W == 0
    cw = A['l0_conv_w'].shape[1]
    me = 4 * lax.axis_index("x") + 2 * lax.axis_index("y") + lax.axis_index("c")

    def layer_params(l):
        pre = f"l{l}_"
        return {k[len(pre):]: (v.reshape(1, -1) if v.ndim == 1 and k != 'l1_sink' else v) for k, v in A.items() if k.startswith(pre)}

    params = [layer_params(0), layer_params(1)]

    def cast(l, n):
        w = A[f'l{l}_{n}']
        return (w.T if n in TRANSPOSED else w).astype(BF16)

    stream = _WeightStream(cast)
    token = stream.start(0, IN_WEIGHT)

    big_names = [n for n in WEIGHT_NAMES if n[3:] in BIG_WEIGHTS + ('w_mod',)]
    rest = [n for n in WEIGHT_NAMES if n not in big_names]
    early = ['x', 'ctx'] + rest + ['m_' + n for n in rest] + ['v_' + n for n in rest]
    token, held = lax.optimization_barrier((token, [A[n] for n in early]))
    A.update(zip(early, held))
    x, ctx = A['x'][0], A['ctx'][0]
    wp, layw = _pack([(n, A[n]) for n in rest])
    mp, _ = _pack([(n, A['m_' + n]) for n in rest])
    vp, _ = _pack([(n, A['v_' + n]) for n in rest])
    rope = _rope_tables(S, Lc)
    h = jnp.concatenate([ctx, x], axis=0)

    sc_own = jax.nn.silu(A['c']) + token[0, 0]
    first, lay0 = _pack([('sc', sc_own), ('conv_w', A['l0_conv_w'])])
    first, h, wp, mp, vp = lax.optimization_barrier((first, h, wp, mp, vp))
    first_all = _exchange("gather_cond", [first], False)[0]
    sc_all = _unpack(first_all, lay0, 'sc')[:, 0]
    conv_w = _unpack(first_all, lay0, 'conv_w').transpose(1, 0, 2).reshape(3, N_DEV * cw)
    conv_w8 = _pad_rows(conv_w, 8)
    sc_ctx = jax.nn.silu(A['c_ctx'])
    s16 = _pad_rows(jnp.concatenate([sc_all, sc_ctx[None]], axis=0), 16)

    nmod = A['l0_w_mod'].shape[1]
    modp = jnp.concatenate([_mm_plain(f"l{l}_mod", s16, A[f'l{l}_w_mod'], False) for l in range(2)], axis=1)
    modp_all = _exchange("gather_mod", [modp], False)[0]
    mods = []
    for l in range(2):
        full = modp_all[:, :, l * nmod:(l + 1) * nmod].transpose(1, 0, 2).reshape(16, N_MOD * D) + A[f'l{l}_b_mod'][None]
        both = jnp.stack([full[8], lax.dynamic_index_in_dim(full, me, 0, keepdims=False)]).reshape(2, N_MOD, D)
        mods.append(jnp.concatenate([both, jnp.zeros((2, 8 - N_MOD, D), F32)], axis=1))

    saved, W = [], []
    for l in range(2):
        h, s, Wl = _layer_fwd(l, h, params[l], stream, mods[l], rope, conv_w8, cfg)
        saved.append(s)
        W.append(Wl)

    dh, loss_blk, dgf = _loss_head("loss_head", h, A['final_norm_g'].reshape(1, -1), A['loss_target'][0], cfg)
    loss = lax.psum(loss_blk[0, 0], ("x", "y", "c"))

    grads, small, dmods, scatters = {}, {'final_norm_g': dgf[0]}, [None, None], [None, None]
    token = jnp.zeros((8, 128), F32)
    for l in (1, 0):
        dh, dmods[l], sm, scatters[l], token = _layer_bwd(l, dh, saved[l], params[l], W[l], mods[l] + token[0, 0], rope, conv_w8, cfg)
        small.update({f'l{l}_{k}': v for k, v in sm.items()})
    grad_x = dh[None]

    def landed(l, key, after):
        group = FFN_WEIGHTS if key == 'ffn' else (key,)
        for n, parts in zip(group, _exchange_wait(f"scatter_{key}{l}_wait", scatters[l][key], after)):
            shape = A[f'l{l}_{n}'].shape
            grads[f'l{l}_{n}'] = (parts.reshape((N_DEV,) + (shape[::-1] if n in TRANSPOSED else shape)), None)

    out = {}

    def adam(n, dep=None):
        w, m, v = A[n], A['m_' + n], A['v_' + n]
        if n[3:] in TRANSPOSED:
            out[n] = tuple(r.T for r in _adamw("adamw_" + n, grads[n][0], w.T, m.T, v.T, dep))
        else:
            out[n] = _adamw("adamw_" + n, grads[n][0], w, m, v, dep)
        return out[n][1]

    last = dh
    for l in (1, 0):
        for key in ('ffn', 'w_out') + (('w_in',) if l == 1 else ()):
            landed(l, key, last)
            for n in (FFN_WEIGHTS if key == 'ffn' else (key,)):
                last = adam(f'l{l}_{n}', token)

    small_names = [n for n in WEIGHT_NAMES if n in small]
    pieces = [(n, small[n]) for n in small_names]
    for l in range(2):
        pieces += [(f'dmod{l}', dmods[l][1, :N_MOD]), (f'dcmod{l}', dmods[l][0, :N_MOD])]
    second, lay1 = _pack(pieces)
    second_all = _exchange("gather_small", [second], False, after=last)[0]

    dsc_part = jnp.zeros((16, D), F32)
    for l in range(2):
        dm16 = _pad_rows(jnp.concatenate([_unpack(second_all, lay1, f'dmod{l}').reshape(N_DEV, N_MOD * D),
                                          jnp.sum(_unpack(second_all, lay1, f'dcmod{l}'), axis=0).reshape(1, N_MOD * D)], axis=0), 16)
        mine = lax.dynamic_slice_in_dim(dm16, me * nmod, nmod, axis=1)
        tk = _col_tile(D, 512)
        gw = _mm_tn(f"l{l}_dw_mod", s16, mine, pl.BlockSpec((16, tk), lambda kb, t: (0, kb)), pl.BlockSpec((16, nmod), lambda kb, t: (0, 0)),
                    jax.ShapeDtypeStruct((D, nmod), F32), pl.BlockSpec((tk, nmod), lambda kb, t: (kb, 0)), (tk, nmod), (D // tk, 1))
        grads[f'l{l}_w_mod'] = (gw[None], None)
        dsc_part += _mm_plain(f"l{l}_dsc", mine, A[f'l{l}_w_mod'], True)
        dmod_dev = _unpack(second_all, lay1, f'dmod{l}') + _unpack(second_all, lay1, f'dcmod{l}')
        grads[f'l{l}_b_mod'] = (dmod_dev.reshape(N_DEV, N_MOD * D), None)
    dsig = jax.nn.sigmoid(A['c_ctx'])
    dsilu = dsig * (1 + A['c_ctx'] * (1 - dsig))
    third_all = _exchange("gather_dsc", [dsc_part[8:9]], False)[0]
    grads['c_ctx'] = (third_all[:, 0] * dsilu[None], None)
    for n in small_names:
        g8 = _unpack(second_all, lay1, n)
        if n == 'l0_conv_w':
            g8 = lax.dynamic_slice_in_dim(g8, me * cw, cw, axis=2)
        grads[n] = (g8, None)

    adam('l0_w_mod')
    last = adam('l1_w_mod')
    gp, _ = _pack([(n, grads[n][0]) for n in rest], nlead=1)
    res = _adamw("adamw_small", gp, wp, mp, vp)
    for n in rest:
        out[n] = tuple(_unpack(r, layw, n) for r in res)
    landed(0, 'w_in', last)
    adam('l0_w_in')

    outs = [loss, grad_x]
    for k in range(4):
        outs += [out[n][k] for n in WEIGHT_NAMES]
    return tuple(outs)
```

```python
import functools
import math

import numpy as np
import jax
import jax.numpy as jnp
from jax import lax
from jax.experimental import pallas as pl
from jax.experimental.pallas import tpu as pltpu

F32 = jnp.float32
BF16 = jnp.bfloat16
HEAD_DIM = 128
AUX_WIDTH = 512
AUX_GROUPS = 4
POOL_HALF = (1, 2, 4, 8)
WINDOW = 128
GRID_W = 64
ROPE_THETA = 10000.0
EPS = 1e-6
NEG_INF = -1e30
ATT_SCALE = HEAD_DIM ** -0.5
LOG2_E = math.log2(math.e)
Q_SCALE = ATT_SCALE * LOG2_E
N_MOD = 6
N_DEV = 8
ROW_TILE = 256
BAND = ROW_TILE + 2 * WINDOW
ADAM_LR, ADAM_B1, ADAM_B2, ADAM_EPS, ADAM_WD, ADAM_STEP = 0.001, 0.9, 0.999, 1e-08, 0.01, 10
VMEM_LIMIT_MB = 56
MESH = pl.DeviceIdType.MESH
RESIDENT = pl.Buffered(buffer_count=1)

WEIGHT_NAMES = ['c_ctx', 'l0_norm1_g', 'l0_w_mod', 'l0_b_mod', 'l0_w_in', 'l0_q_norm_g', 'l0_k_norm_g', 'l0_conv_w', 'l0_w_out', 'l0_norm2_g', 'l0_w_gate', 'l0_w_up', 'l0_w_down', 'l1_norm1_g', 'l1_w_mod', 'l1_b_mod', 'l1_w_in', 'l1_q_norm_g', 'l1_k_norm_g', 'l1_sink', 'l1_pool_w', 'l1_pool_scale', 'l1_w_out', 'l1_norm2_g', 'l1_w_gate', 'l1_w_up', 'l1_w_down', 'final_norm_g']
INPUT_NAMES = (['x', 'c', 'ctx'] + WEIGHT_NAMES + ['loss_target'] + ['m_' + n for n in WEIGHT_NAMES]
               + ['v_' + n for n in WEIGHT_NAMES])
IN_WEIGHT = ('w_in',)
OUT_WEIGHT = ('w_out',)
MIXER_WEIGHTS = OUT_WEIGHT + IN_WEIGHT
FFN_WEIGHTS = ('w_down', 'w_gate', 'w_up')
TRANSPOSED = ('w_gate', 'w_up')
BIG_WEIGHTS = MIXER_WEIGHTS + FFN_WEIGHTS


def _params(vmem_mb=VMEM_LIMIT_MB):
    return pltpu.CompilerParams(vmem_limit_bytes=vmem_mb << 20)


def _row_tile(n, cap):
    best = None
    for t in range(16, min(n, cap) + 1, 16):
        if n % t == 0:
            best = t
    assert best is not None, (n, cap)
    return best


def _col_tile(n, cap):
    best = n
    for t in range(128, min(n, cap) + 1, 128):
        if n % t == 0:
            best = t
    return best if best <= cap or n % 128 else n


def _dot(a, b, ca, cb):
    return lax.dot_general(a, b, (((ca,), (cb,)), ((), ())), preferred_element_type=F32)


def _nn(a, b):
    return _dot(a, b, 1, 0)


def _nt(a, b):
    return _dot(a, b, 1, 1)


def _tn(a, b):
    return _dot(a, b, 0, 0)


def _bf(x):
    return x.astype(BF16)


def _exchange(name, arrs, scatter, after=None):
    n = len(arrs)
    extra = [] if after is None else [after]
    if scatter:
        out_shape = [jax.ShapeDtypeStruct(a.shape, a.dtype) for a in arrs]
    else:
        out_shape = [jax.ShapeDtypeStruct((N_DEV,) + a.shape, a.dtype) for a in arrs]

    def body(*refs):
        ins, outs = refs[:n], refs[n + len(extra):2 * n + len(extra)]
        send_sems, recv_sems, local_sems = refs[2 * n + len(extra):]
        x, y, c = lax.axis_index("x"), lax.axis_index("y"), lax.axis_index("c")
        me = 4 * x + 2 * y + c
        local, remote = [], []
        for a in range(n):
            own = ins[a].at[me] if scatter else ins[a]
            cp = pltpu.make_async_copy(own, outs[a].at[me], local_sems.at[a])
            cp.start()
            local.append(cp)
            for r in range(1, N_DEV):
                px = 1 - x if r & 4 else x
                py = 1 - y if r & 2 else y
                pc = 1 - c if r & 1 else c
                src = ins[a].at[4 * px + 2 * py + pc] if scatter else ins[a]
                cp = pltpu.make_async_remote_copy(
                    src_ref=src, dst_ref=outs[a].at[me], send_sem=send_sems.at[a, r - 1],
                    recv_sem=recv_sems.at[a, r - 1], device_id=(px, py, pc), device_id_type=MESH)
                cp.start()
                remote.append(cp)
        for cp in remote:
            cp.wait()
        for cp in local:
            cp.wait()

    any_spec = pl.BlockSpec(memory_space=pl.ANY)
    return pl.pallas_call(
        body, name=name, out_shape=out_shape,
        in_specs=[any_spec] * (n + len(extra)), out_specs=[any_spec] * n,
        scratch_shapes=[pltpu.SemaphoreType.DMA((n, N_DEV - 1)), pltpu.SemaphoreType.DMA((n, N_DEV - 1)),
                        pltpu.SemaphoreType.DMA((n,))],
    )(*arrs, *extra)


HBM_SPEC = pl.BlockSpec(memory_space=pltpu.HBM)
SEM_SPEC = pl.BlockSpec(memory_space=pltpu.SEMAPHORE)
EFFECT = pltpu.SideEffectType.DATAFLOW_SIDE_EFFECTING


def _split_copies(srcs, lands, send_sems, recv_sems, local_sems, scatter):
    x, y, c = lax.axis_index("x"), lax.axis_index("y"), lax.axis_index("c")
    me = 4 * x + 2 * y + c
    local, remote = [], []
    for a in range(len(srcs)):
        own = srcs[a].at[me] if scatter else srcs[a]
        local.append(pltpu.make_async_copy(own, lands[a].at[me], local_sems.at[a]))
        for r in range(1, N_DEV):
            px = 1 - x if r & 4 else x
            py = 1 - y if r & 2 else y
            pc = 1 - c if r & 1 else c
            src = srcs[a].at[4 * px + 2 * py + pc] if scatter else srcs[a]
            remote.append(pltpu.make_async_remote_copy(
                src_ref=src, dst_ref=lands[a].at[me], send_sem=send_sems.at[a * (N_DEV - 1) + r - 1],
                recv_sem=recv_sems.at[a * (N_DEV - 1) + r - 1], device_id=(px, py, pc), device_id_type=MESH))
    return local, remote


def _exchange_start(name, arrs, scatter, after=None):
    n = len(arrs)
    extra = [] if after is None else [after]
    shapes = [a.shape if scatter else (N_DEV,) + a.shape for a in arrs]
    lands = [pltpu.with_memory_space_constraint(lax.empty(s, a.dtype), pltpu.HBM) for s, a in zip(shapes, arrs)]
    srcs = [pltpu.with_memory_space_constraint(a, pltpu.HBM) for a in arrs]

    def body(*refs):
        src_refs, land_refs = refs[:n], refs[n:2 * n]
        send_sems, recv_sems, local_sems = refs[2 * n + len(extra):2 * n + len(extra) + 3]
        token = refs[-1]
        local, remote = _split_copies(src_refs, land_refs, send_sems, recv_sems, local_sems, scatter)
        for cp in local + remote:
            cp.start()
        token[...] = jnp.zeros_like(token)

    res = pl.pallas_call(
        body, name=name,
        out_shape=[pltpu.SemaphoreType.DMA((n * (N_DEV - 1),)), pltpu.SemaphoreType.DMA((n * (N_DEV - 1),)), pltpu.SemaphoreType.DMA((n,))]
        + [pltpu.HBM(a.shape, a.dtype) for a in arrs] + [pltpu.HBM(s, a.dtype) for s, a in zip(shapes, arrs)]
        + [jax.ShapeDtypeStruct((8, 128), F32)],
        in_specs=[HBM_SPEC] * (2 * n) + [pl.BlockSpec(memory_space=pl.ANY)] * len(extra),
        out_specs=[SEM_SPEC] * 3 + [HBM_SPEC] * (2 * n) + [pl.BlockSpec(memory_space=pltpu.VMEM)],
        input_output_aliases={i: 3 + i for i in range(2 * n)},
        compiler_params=pltpu.CompilerParams(has_side_effects=EFFECT),
    )(*srcs, *lands, *extra)
    return (scatter, res[:3], res[3:3 + n], res[3 + n:3 + 2 * n]), res[-1]


def _exchange_wait(name, handle, after):
    scatter, sems, srcs, lands = handle
    n = len(srcs)

    def body(*refs):
        src_refs, land_refs = refs[:n], refs[n:2 * n]
        send_sems, recv_sems, local_sems = refs[2 * n:2 * n + 3]
        local, remote = _split_copies(src_refs, land_refs, send_sems, recv_sems, local_sems, scatter)
        for cp in remote:
            cp.wait_send()
            cp.wait_recv()
        for cp in local:
            cp.wait()

    res = pl.pallas_call(
        body, name=name,
        out_shape=[pltpu.HBM(a.shape, a.dtype) for a in srcs] + [pltpu.HBM(a.shape, a.dtype) for a in lands],
        in_specs=[HBM_SPEC] * (2 * n) + [SEM_SPEC] * 3 + [pl.BlockSpec(memory_space=pl.ANY)], out_specs=[HBM_SPEC] * (2 * n),
        input_output_aliases={i: i for i in range(2 * n)},
        compiler_params=pltpu.CompilerParams(has_side_effects=EFFECT),
    )(*srcs, *lands, *sems, after)
    return list(res[n:])


FIRST_COPIES = 4
RELAY_COPIES = 3


def _gather_copies(srcs, lands, sems):
    send_sems, recv_sems, local_sems = sems[:3]
    x, y, c = lax.axis_index("x"), lax.axis_index("y"), lax.axis_index("c")
    me = 4 * x + 2 * y + c
    chips = [(1 - x, y), (x, 1 - y), (1 - x, 1 - y)]
    local, first, relay = [], [], []
    for a in range(len(srcs)):
        local.append(pltpu.make_async_copy(srcs[a], lands[a].at[me], local_sems.at[a]))
        targets = [(x, y, 1 - c)] + [(px, py, c) for px, py in chips]
        first.append([pltpu.make_async_remote_copy(
            src_ref=srcs[a], dst_ref=lands[a].at[me], send_sem=send_sems.at[FIRST_COPIES * a + k],
            recv_sem=recv_sems.at[FIRST_COPIES * a + k], device_id=t, device_id_type=MESH) for k, t in enumerate(targets)])
        if len(sems) > 3:
            rsend, rrecv = sems[3:]
            slots = [lands[a].at[4 * px + 2 * py + c] for px, py in chips]
            relay.append([pltpu.make_async_remote_copy(
                src_ref=slot, dst_ref=slot, send_sem=rsend.at[RELAY_COPIES * a + j], recv_sem=rrecv.at[RELAY_COPIES * a + j],
                device_id=(x, y, 1 - c), device_id_type=MESH) for j, slot in enumerate(slots)])
    return local, first, relay


def _gather_start(name, arrs, after=None):
    n = len(arrs)
    extra = [] if after is None else [after]
    lands = [pltpu.with_memory_space_constraint(lax.empty((N_DEV,) + a.shape, a.dtype), pltpu.HBM) for a in arrs]
    srcs = [pltpu.with_memory_space_constraint(a, pltpu.HBM) for a in arrs]

    def body(*refs):
        at = 2 * n + len(extra)
        local, first, _ = _gather_copies(refs[:n], refs[n:2 * n], refs[at:at + 3])
        for cp in local + [cp for cps in first for cp in cps]:
            cp.start()
        refs[-1][...] = jnp.zeros_like(refs[-1])

    res = pl.pallas_call(
        body, name=name,
        out_shape=[pltpu.SemaphoreType.DMA((FIRST_COPIES * n,)), pltpu.SemaphoreType.DMA((FIRST_COPIES * n,)), pltpu.SemaphoreType.DMA((n,))]
        + [pltpu.HBM(a.shape, a.dtype) for a in arrs] + [pltpu.HBM((N_DEV,) + a.shape, a.dtype) for a in arrs]
        + [jax.ShapeDtypeStruct((8, 128), F32)],
        in_specs=[HBM_SPEC] * (2 * n) + [pl.BlockSpec(memory_space=pl.ANY)] * len(extra),
        out_specs=[SEM_SPEC] * 3 + [HBM_SPEC] * (2 * n) + [pl.BlockSpec(memory_space=pltpu.VMEM)],
        input_output_aliases={i: 3 + i for i in range(2 * n)},
        compiler_params=pltpu.CompilerParams(has_side_effects=EFFECT),
    )(*srcs, *lands, *extra)
    return (list(res[:3]), list(res[3:3 + n]), list(res[3 + n:3 + 2 * n])), res[-1]


def _gather_relay(name, handle, after):
    sems, srcs, lands = handle
    n = len(srcs)

    def body(*refs):
        in_sems = refs[2 * n:2 * n + 3]
        out_sems = refs[2 * n + 4 + 2 * n:2 * n + 4 + 2 * n + 2]
        _, first, relay = _gather_copies(refs[:n], refs[n:2 * n], list(in_sems) + list(out_sems))
        for a in range(n):
            for j in range(RELAY_COPIES):
                first[a][1 + j].wait_recv()
                relay[a][j].start()
        refs[-1][...] = jnp.zeros_like(refs[-1])

    res = pl.pallas_call(
        body, name=name,
        out_shape=[pltpu.HBM(a.shape, a.dtype) for a in srcs] + [pltpu.HBM(a.shape, a.dtype) for a in lands]
        + [pltpu.SemaphoreType.DMA((RELAY_COPIES * n,)), pltpu.SemaphoreType.DMA((RELAY_COPIES * n,)), jax.ShapeDtypeStruct((8, 128), F32)],
        in_specs=[HBM_SPEC] * (2 * n) + [SEM_SPEC] * 3 + [pl.BlockSpec(memory_space=pl.ANY)],
        out_specs=[HBM_SPEC] * (2 * n) + [SEM_SPEC] * 2 + [pl.BlockSpec(memory_space=pltpu.VMEM)],
        input_output_aliases={i: i for i in range(2 * n)},
        compiler_params=pltpu.CompilerParams(has_side_effects=EFFECT),
    )(*srcs, *lands, *sems, after)
    return (sems + list(res[2 * n:2 * n + 2]), list(res[:n]), list(res[n:2 * n])), res[-1]


def _gather_wait(name, handle, after):
    sems, srcs, lands = handle
    n = len(srcs)

    def body(*refs):
        local, first, relay = _gather_copies(refs[:n], refs[n:2 * n], refs[2 * n:2 * n + 5])
        for a in range(n):
            for cp in first[a]:
                cp.wait_send()
            first[a][0].wait_recv()
            for cp in relay[a]:
                cp.wait_send()
                cp.wait_recv()
            local[a].wait()

    res = pl.pallas_call(
        body, name=name,
        out_shape=[pltpu.HBM(a.shape, a.dtype) for a in srcs] + [pltpu.HBM(a.shape, a.dtype) for a in lands],
        in_specs=[HBM_SPEC] * (2 * n) + [SEM_SPEC] * 5 + [pl.BlockSpec(memory_space=pl.ANY)], out_specs=[HBM_SPEC] * (2 * n),
        input_output_aliases={i: i for i in range(2 * n)},
        compiler_params=pltpu.CompilerParams(has_side_effects=EFFECT),
    )(*srcs, *lands, *sems, after)
    return list(res[n:])


def _dep(dep, grid_rank):
    if dep is None:
        return [], []
    return [dep], [pl.BlockSpec((8, 128), (lambda i, j: (0, 0)) if grid_rank == 2 else (lambda i: (0, 0)))]


def _mm_step(name, fn, ins, in_specs, out_shape, out_spec, grid, dep=None, res=None):
    n = len(ins)
    dep_ins, dep_specs = _dep(dep, len(grid))
    res_ins, res_specs, out_shapes, out_specs = [], [], out_shape, out_spec
    if res is not None:
        h, mod, row_idx, lc = res
        tm, tn = out_spec.block_shape
        res_ins = [h, mod]
        res_specs = [pl.BlockSpec((tm, tn), lambda j, i: (i, j)), pl.BlockSpec((2, 8, tn), lambda j, i: (0, 0, j))]
        out_shapes, out_specs = [out_shape, jax.ShapeDtypeStruct(h.shape, h.dtype)], [out_spec, res_specs[0]]

    def body(*refs):
        outs = refs[n + len(res_ins) + len(dep_ins):]
        acc = fn(*refs[:n])
        outs[0][...] = acc.astype(outs[0].dtype)
        if res is not None:
            h_ref, mod_ref = refs[n:n + 2]
            row = pl.program_id(1) * tm + lax.broadcasted_iota(jnp.int32, (tm, 1), 0)
            gate = jnp.where(row < lc, mod_ref[0, row_idx:row_idx + 1, :], mod_ref[1, row_idx:row_idx + 1, :])
            outs[1][...] = h_ref[...] + gate * acc

    return pl.pallas_call(body, name=name, grid=grid, in_specs=list(in_specs) + res_specs + dep_specs, out_specs=out_specs,
                          out_shape=out_shapes, compiler_params=_params())(*ins, *res_ins, *dep_ins)


def _mm_tn(name, a, b, a_spec, b_spec, out_shape, out_spec, acc_shape, grid):
    nk = grid[-1]
    kax = len(grid) - 1
    if nk == 1:
        def whole(a_ref, b_ref, o_ref):
            o_ref[...] = _tn(_bf(a_ref[...]), _bf(b_ref[...])).astype(o_ref.dtype)

        return pl.pallas_call(whole, name=name, grid=grid, in_specs=[a_spec, b_spec], out_specs=out_spec,
                              out_shape=out_shape, compiler_params=_params())(a, b)

    def body(a_ref, b_ref, o_ref, acc_ref):
        k = pl.program_id(kax)

        @pl.when(k == 0)
        def _():
            acc_ref[...] = jnp.zeros_like(acc_ref)

        acc_ref[...] += _tn(_bf(a_ref[...]), _bf(b_ref[...]))

        @pl.when(k == nk - 1)
        def _():
            o_ref[...] = acc_ref[...].astype(o_ref.dtype)

    return pl.pallas_call(body, name=name, grid=grid, in_specs=[a_spec, b_spec], out_specs=out_spec,
                          out_shape=out_shape, scratch_shapes=[pltpu.VMEM(acc_shape, F32)],
                          compiler_params=_params())(a, b)


def _mm_cols(name, a, w3, out_dtype=F32, dep=None):
    M, K = a.shape
    J, _, n = w3.shape
    return _mm_step(
        name, lambda a_ref, w_ref: _nn(_bf(a_ref[...]), w_ref[...]), [a, w3],
        [pl.BlockSpec((M, K), lambda j, i: (0, 0), pipeline_mode=RESIDENT), pl.BlockSpec((None, K, n), lambda j, i: (j, 0, 0))],
        jax.ShapeDtypeStruct((M, J * n), out_dtype), pl.BlockSpec((M, n), lambda j, i: (0, j)), (J, 1), dep)


def _mm_plain(name, a, b, transpose_b, out_dtype=F32, tn=512, dep=None, res=None):
    M, K = a.shape
    N = b.shape[0] if transpose_b else b.shape[1]
    tn = _col_tile(N, tn)
    if res is None:
        tm, a_spec = M, pl.BlockSpec((M, K), lambda j, i: (0, 0), pipeline_mode=RESIDENT)
    else:
        tm = _row_tile(M, 1088)
        a_spec = pl.BlockSpec((tm, K), lambda j, i: (i, 0))
    if transpose_b:
        b_spec = pl.BlockSpec((tn, K), lambda j, i: (j, 0))
        fn = lambda a_ref, b_ref: _nt(_bf(a_ref[...]), _bf(b_ref[...]))
    else:
        b_spec = pl.BlockSpec((K, tn), lambda j, i: (0, j))
        fn = lambda a_ref, b_ref: _nn(_bf(a_ref[...]), _bf(b_ref[...]))
    return _mm_step(name, fn, [a, b], [a_spec, b_spec],
                    jax.ShapeDtypeStruct((M, N), out_dtype), pl.BlockSpec((tm, tn), lambda j, i: (i, j)),
                    (N // tn, M // tm), dep, res)


def _mm_shards_nn(name, a3, w3, tn=512, dep=None, res=None):
    J, M, k = a3.shape
    N = w3.shape[2]
    tm = _row_tile(M, 544)
    tn = _col_tile(N, tn)

    def fn(a_ref, w_ref):
        acc = _nn(a_ref[0], w_ref[0])
        for j in range(1, J):
            acc += _nn(a_ref[j], w_ref[j])
        return acc

    return _mm_step(name, fn, [a3, w3],
                    [pl.BlockSpec((J, tm, k), lambda jn, i: (0, i, 0)), pl.BlockSpec((J, k, tn), lambda jn, i: (0, 0, jn))],
                    jax.ShapeDtypeStruct((M, N), F32), pl.BlockSpec((tm, tn), lambda jn, i: (i, jn)), (N // tn, M // tm), dep, res)


def _mm_shards_nn2(name, a3, w3a, b3, w3b, tn=512):
    J, M, k = a3.shape
    N = w3a.shape[2]
    tm = _row_tile(M, 544)
    tn = _col_tile(N, tn)

    def fn(a_ref, wa_ref, b_ref, wb_ref):
        acc = _nn(a_ref[0], wa_ref[0]) + _nn(b_ref[0], wb_ref[0])
        for j in range(1, J):
            acc += _nn(a_ref[j], wa_ref[j]) + _nn(b_ref[j], wb_ref[j])
        return acc

    act = pl.BlockSpec((J, tm, k), lambda jn, i: (0, i, 0))
    wsp = pl.BlockSpec((J, k, tn), lambda jn, i: (0, 0, jn))
    return _mm_step(name, fn, [a3, w3a, b3, w3b], [act, wsp, act, wsp],
                    jax.ShapeDtypeStruct((M, N), F32), pl.BlockSpec((tm, tn), lambda jn, i: (i, jn)), (N // tn, M // tm))


def _mm_cols_nt(name, a, w3, tn=512):
    M = a.shape[0]
    J, N, n = w3.shape
    tm = _row_tile(M, 544)
    tn = _col_tile(N, tn)

    def fn(a_ref, w_ref):
        acc = _nt(a_ref[:, 0:n], w_ref[0])
        for j in range(1, J):
            acc += _nt(a_ref[:, j * n:(j + 1) * n], w_ref[j])
        return acc

    return _mm_step(name, fn, [a, w3],
                    [pl.BlockSpec((tm, J * n), lambda jn, i: (i, 0)), pl.BlockSpec((J, tn, n), lambda jn, i: (0, jn, 0))],
                    jax.ShapeDtypeStruct((M, N), F32), pl.BlockSpec((tm, tn), lambda jn, i: (i, jn)), (N // tn, M // tm))


def _wgrad_cols(name, a, b, J):
    T, K = a.shape
    n = b.shape[1] // J
    return _mm_tn(name, a, b, pl.BlockSpec((T, K), lambda j, t: (0, 0), pipeline_mode=RESIDENT), pl.BlockSpec((T, n), lambda j, t: (0, j)),
                  jax.ShapeDtypeStruct((J, K, n), BF16), pl.BlockSpec((None, K, n), lambda j, t: (j, 0, 0)), (K, n), (J, 1))


def _wgrad_rows(name, a, b, tk=512):
    T, K = a.shape
    N = b.shape[1]
    tk = _col_tile(K, tk)
    return _mm_tn(name, a, b, pl.BlockSpec((T, tk), lambda kb, t: (0, kb)), pl.BlockSpec((T, N), lambda kb, t: (0, 0), pipeline_mode=RESIDENT),
                  jax.ShapeDtypeStruct((K, N), BF16), pl.BlockSpec((tk, N), lambda kb, t: (kb, 0)), (tk, N), (K // tk, 1))


def _wgrad_up(name, a, b3):
    T, K = a.shape
    J, _, k = b3.shape
    tt = _row_tile(T, 1088)
    return _mm_tn(name, a, b3, pl.BlockSpec((tt, K), lambda j, t: (t, 0)), pl.BlockSpec((None, tt, k), lambda j, t: (j, t, 0)),
                  jax.ShapeDtypeStruct((J, K, k), BF16), pl.BlockSpec((None, K, k), lambda j, t: (j, 0, 0)), (K, k), (J, T // tt))


def _wgrad_down(name, a3, b):
    J, T, k = a3.shape
    N = b.shape[1]
    return _mm_tn(name, a3, b, pl.BlockSpec((None, T, k), lambda j, t: (j, 0, 0)),
                  pl.BlockSpec((T, N), lambda j, t: (0, 0), pipeline_mode=RESIDENT),
                  jax.ShapeDtypeStruct((J, k, N), BF16), pl.BlockSpec((None, k, N), lambda j, t: (j, 0, 0)), (k, N), (J, 1))


def _seg(i):
    return jnp.minimum(i, 1)


def _rstd(x):
    return lax.rsqrt(jnp.mean(x * x, axis=-1, keepdims=True) + EPS)


def _norm_mod(name, h, g, mod, which):
    T, D = h.shape

    def body(h_ref, g_ref, mod_ref, o_ref):
        x = h_ref[...]
        n = x * _rstd(x) * g_ref[...]
        shift = mod_ref[3 * which:3 * which + 1, :]
        scale = mod_ref[3 * which + 1:3 * which + 2, :]
        o_ref[...] = (n * (1 + scale) + shift).astype(o_ref.dtype)

    row = pl.BlockSpec((ROW_TILE, D), lambda i: (i, 0))
    return pl.pallas_call(
        body, name=name, grid=(T // ROW_TILE,),
        in_specs=[row, pl.BlockSpec((1, D), lambda i: (0, 0)), pl.BlockSpec((None, 8, D), lambda i: (_seg(i), 0, 0))],
        out_specs=row, out_shape=jax.ShapeDtypeStruct((T, D), BF16), compiler_params=_params())(h, g, mod)


def _norm_mod_bwd(name, dxn, h, g, mod, which, dres, latent_only=False):
    T, D = h.shape

    def body(dxn_ref, h_ref, g_ref, mod_ref, dres_ref, dh_ref, dmod_ref, dg_ref):
        i = pl.program_id(0)
        x = h_ref[...]
        r = _rstd(x)
        xhat = x * r
        g = g_ref[...]
        n = xhat * g
        scale = mod_ref[3 * which + 1:3 * which + 2, :]
        dxn = dxn_ref[...]
        dn = dxn * (1 + scale)
        dxh = dn * g
        dh = dres_ref[...] + r * (dxh - xhat * jnp.mean(dxh * xhat, axis=-1, keepdims=True))
        if latent_only:
            @pl.when(i > 0)
            def _():
                dh_ref[...] = dh
        else:
            dh_ref[...] = dh

        @pl.when(i <= 1)
        def _():
            dmod_ref[...] = jnp.zeros_like(dmod_ref)

        @pl.when(i == 0)
        def _():
            dg_ref[...] = jnp.zeros_like(dg_ref)

        dmod_ref[3 * which:3 * which + 1, :] += jnp.sum(dxn, axis=0, keepdims=True)
        dmod_ref[3 * which + 1:3 * which + 2, :] += jnp.sum(dxn * n, axis=0, keepdims=True)
        dg_ref[0:1, :] += jnp.sum(dn * xhat, axis=0, keepdims=True)

    row = pl.BlockSpec((ROW_TILE, D), lambda i: (i, 0))
    modspec = pl.BlockSpec((None, 8, D), lambda i: (_seg(i), 0, 0))
    dh_rows = T - ROW_TILE if latent_only else T
    dh_spec = pl.BlockSpec((ROW_TILE, D), lambda i: (jnp.maximum(i - 1, 0), 0)) if latent_only else row
    return pl.pallas_call(
        body, name=name, grid=(T // ROW_TILE,),
        in_specs=[row, row, pl.BlockSpec((1, D), lambda i: (0, 0)), modspec, row],
        out_specs=[dh_spec, modspec, pl.BlockSpec((8, D), lambda i: (0, 0))],
        out_shape=[jax.ShapeDtypeStruct((dh_rows, D), F32), jax.ShapeDtypeStruct((2, 8, D), F32), jax.ShapeDtypeStruct((8, D), F32)],
        compiler_params=_params())(dxn, h, g, mod, dres)


def _gate_bwd(name, dh, y, mod, row_idx):
    T, D = dh.shape

    def body(dh_ref, y_ref, mod_ref, dy_ref, dmod_ref):
        i = pl.program_id(0)
        dh = dh_ref[...]
        dy_ref[...] = (dh * mod_ref[row_idx:row_idx + 1, :]).astype(dy_ref.dtype)

        @pl.when(i <= 1)
        def _():
            dmod_ref[...] = jnp.zeros_like(dmod_ref)

        dmod_ref[row_idx:row_idx + 1, :] += jnp.sum(dh * y_ref[...], axis=0, keepdims=True)

    row = pl.BlockSpec((ROW_TILE, D), lambda i: (i, 0))
    modspec = pl.BlockSpec((None, 8, D), lambda i: (_seg(i), 0, 0))
    return pl.pallas_call(
        body, name=name, grid=(T // ROW_TILE,), in_specs=[row, row, modspec], out_specs=[row, modspec],
        out_shape=[jax.ShapeDtypeStruct((T, D), BF16), jax.ShapeDtypeStruct((2, 8, D), F32)],
        compiler_params=_params())(dh, y, mod)


def _rot(y):
    lane = lax.broadcasted_iota(jnp.int32, y.shape, 1)
    return jnp.where((lane & 32) == 0, pltpu.roll(y, 96, 1), pltpu.roll(y, 32, 1))


def _qk_prep(name, P, q_g, k_g, rope_c, rope_s, cfg):
    T = P.shape[0]
    ATT, KVW = cfg['ATT'], cfg['KVW']

    def body(q_ref, k_ref, v_ref, qg_ref, kg_ref, c_ref, s_ref, qo_ref, ko_ref, vo_ref):
        cc, ss = c_ref[...], s_ref[...]

        def head(x, g):
            y = x * _rstd(x) * g
            return y * cc + _rot(y) * ss

        for hh in range(ATT // HEAD_DIM):
            sl = slice(hh * HEAD_DIM, (hh + 1) * HEAD_DIM)
            qo_ref[:, sl] = (head(q_ref[:, sl], qg_ref[...]) * Q_SCALE).astype(qo_ref.dtype)
        for hh in range(KVW // HEAD_DIM):
            sl = slice(hh * HEAD_DIM, (hh + 1) * HEAD_DIM)
            ko_ref[:, sl] = head(k_ref[:, sl], kg_ref[...]).astype(ko_ref.dtype)
        vo_ref[...] = v_ref[...].astype(vo_ref.dtype)

    kb = ATT // KVW
    gain = pl.BlockSpec((1, HEAD_DIM), lambda i: (0, 0))
    tab = pl.BlockSpec((ROW_TILE, HEAD_DIM), lambda i: (i, 0))
    qs = pl.BlockSpec((ROW_TILE, ATT), lambda i: (i, 0))
    ks = pl.BlockSpec((ROW_TILE, KVW), lambda i: (i, 0))
    return pl.pallas_call(
        body, name=name, grid=(T // ROW_TILE,),
        in_specs=[qs, pl.BlockSpec((ROW_TILE, KVW), lambda i: (i, kb)), pl.BlockSpec((ROW_TILE, KVW), lambda i: (i, kb + 1)),
                  gain, gain, tab, tab],
        out_specs=[qs, ks, ks],
        out_shape=[jax.ShapeDtypeStruct((T, ATT), BF16), jax.ShapeDtypeStruct((T, KVW), BF16), jax.ShapeDtypeStruct((T, KVW), BF16)],
        compiler_params=_params())(P, P, P, q_g, k_g, rope_c, rope_s)


def _qk_prep_bwd(name, dqr, dkr, P, q_g, k_g, rope_c, rope_s, cfg):
    T = P.shape[0]
    ATT, KVW = cfg['ATT'], cfg['KVW']

    def body(dq_ref, dk_ref, q_ref, k_ref, qg_ref, kg_ref, c_ref, s_ref, dqo_ref, dko_ref, dqg_ref, dkg_ref):
        i = pl.program_id(0)
        cc, ss = c_ref[...], s_ref[...]

        @pl.when(i == 0)
        def _():
            dqg_ref[...] = jnp.zeros_like(dqg_ref)
            dkg_ref[...] = jnp.zeros_like(dkg_ref)

        def head(x, g, dout):
            dy = dout * cc + _rot(dout * ss)
            r = _rstd(x)
            xhat = x * r
            dxh = dy * g
            dx = r * (dxh - xhat * jnp.mean(dxh * xhat, axis=-1, keepdims=True))
            return dx, jnp.sum(dy * xhat, axis=0, keepdims=True)

        dg = jnp.zeros((1, HEAD_DIM), F32)
        for hh in range(ATT // HEAD_DIM):
            sl = slice(hh * HEAD_DIM, (hh + 1) * HEAD_DIM)
            dx, d = head(q_ref[:, sl], qg_ref[...], dq_ref[:, sl] * ATT_SCALE)
            dqo_ref[:, sl] = dx.astype(dqo_ref.dtype)
            dg += d
        dqg_ref[0:1, :] += dg
        dg = jnp.zeros((1, HEAD_DIM), F32)
        for hh in range(KVW // HEAD_DIM):
            sl = slice(hh * HEAD_DIM, (hh + 1) * HEAD_DIM)
            dx, d = head(k_ref[:, sl], kg_ref[...], dk_ref[:, sl] * (1.0 / LOG2_E))
            dko_ref[:, sl] = dx.astype(dko_ref.dtype)
            dg += d
        dkg_ref[0:1, :] += dg

    kb = ATT // KVW
    gain = pl.BlockSpec((1, HEAD_DIM), lambda i: (0, 0))
    dgain = pl.BlockSpec((8, HEAD_DIM), lambda i: (0, 0))
    tab = pl.BlockSpec((ROW_TILE, HEAD_DIM), lambda i: (i, 0))
    qs = pl.BlockSpec((ROW_TILE, ATT), lambda i: (i, 0))
    ks = pl.BlockSpec((ROW_TILE, KVW), lambda i: (i, 0))
    return pl.pallas_call(
        body, name=name, grid=(T // ROW_TILE,),
        in_specs=[qs, ks, qs, pl.BlockSpec((ROW_TILE, KVW), lambda i: (i, kb)), gain, gain, tab, tab],
        out_specs=[qs, ks, dgain, dgain],
        out_shape=[jax.ShapeDtypeStruct((T, ATT), BF16), jax.ShapeDtypeStruct((T, KVW), BF16),
                   jax.ShapeDtypeStruct((8, HEAD_DIM), F32), jax.ShapeDtypeStruct((8, HEAD_DIM), F32)],
        compiler_params=_params())(dqr, dkr, P, P, q_g, k_g, rope_c, rope_s)


def _att_specs(T, G):
    qs = pl.BlockSpec((ROW_TILE, G * HEAD_DIM), lambda h, i: (i, h))
    kvs = pl.BlockSpec((T, HEAD_DIM), lambda h, i: (0, h))
    return qs, kvs


def _attn_dense_fwd(name, q, k, v, cfg):
    T, G, Lc = q.shape[0], cfg['G'], cfg['Lc']

    def body(q_ref, k_ref, v_ref, o_ref, lse_ref):
        def attend(rows):
            kk, vv = k_ref[0:rows, :], v_ref[0:rows, :]
            for g in range(G):
                sl = slice(g * HEAD_DIM, (g + 1) * HEAD_DIM)
                s = _nt(q_ref[:, sl], kk)
                m = jnp.max(s, axis=1, keepdims=True)
                p = jnp.exp2(s - m)
                l = jnp.sum(p, axis=1, keepdims=True)
                o_ref[:, sl] = _nn(_bf(p), vv) / l
                lse_ref[:, sl] = jnp.broadcast_to(m + jnp.log2(l), (ROW_TILE, HEAD_DIM))

        @pl.when(pl.program_id(1) == 0)
        def _():
            attend(Lc)

        @pl.when(pl.program_id(1) > 0)
        def _():
            attend(T)

    qs, kvs = _att_specs(T, G)
    return pl.pallas_call(
        body, name=name, grid=(cfg['NKV'], T // ROW_TILE), in_specs=[qs, kvs, kvs], out_specs=[qs, qs],
        out_shape=[jax.ShapeDtypeStruct(q.shape, F32), jax.ShapeDtypeStruct(q.shape, F32)],
        compiler_params=_params())(q, k, v)


def _attn_dense_bwd(name, q, k, v, o, lse, dmix, cfg):
    T, G, Lc = q.shape[0], cfg['G'], cfg['Lc']

    def body(q_ref, k_ref, v_ref, o_ref, lse_ref, do_ref, dq_ref, dk_ref, dv_ref):
        i = pl.program_id(1)

        @pl.when(i == 0)
        def _():
            dk_ref[...] = jnp.zeros_like(dk_ref)
            dv_ref[...] = jnp.zeros_like(dv_ref)

        def attend(rows):
            kk, vv = k_ref[0:rows, :], v_ref[0:rows, :]
            for g in range(G):
                sl = slice(g * HEAD_DIM, (g + 1) * HEAD_DIM)
                qg, do = q_ref[:, sl], do_ref[:, sl]
                delta = jnp.sum(do * o_ref[:, sl], axis=1, keepdims=True)
                p = jnp.exp2(_nt(qg, kk) - lse_ref[:, g * HEAD_DIM:g * HEAD_DIM + 1])
                dob = _bf(do)
                dv_ref[0:rows, :] += _tn(_bf(p), dob)
                ds = _bf(p * (_nt(dob, vv) - delta))
                dq_ref[:, sl] = _nn(ds, kk)
                dk_ref[0:rows, :] += _tn(ds, qg)

        @pl.when(i == 0)
        def _():
            attend(Lc)

        @pl.when(i > 0)
        def _():
            attend(T)

    qs, kvs = _att_specs(T, G)
    return pl.pallas_call(
        body, name=name, grid=(cfg['NKV'], T // ROW_TILE), in_specs=[qs, kvs, kvs, qs, qs, qs], out_specs=[qs, kvs, kvs],
        out_shape=[jax.ShapeDtypeStruct(q.shape, F32), jax.ShapeDtypeStruct(k.shape, F32), jax.ShapeDtypeStruct(k.shape, F32)],
        compiler_params=_params())(q, k, v, o, lse, dmix)


def _band(i, T, Lc):
    start = pl.multiple_of(jnp.clip(WINDOW + (i - 1) * ROW_TILE, 0, T - BAND), WINDOW)
    qpos = (i - 1) * ROW_TILE + lax.broadcasted_iota(jnp.int32, (ROW_TILE, 1), 0)
    kpos = start - Lc + lax.broadcasted_iota(jnp.int32, (1, BAND), 1)
    ok = (jnp.abs(kpos - qpos) <= WINDOW) & (kpos >= 0) & (i > 0)
    return start, jnp.where(ok, 0.0, NEG_INF).astype(F32)


def _attn_win_fwd(name, q, k, v, sink, cfg):
    T, G, Lc = q.shape[0], cfg['G'], cfg['Lc']

    def body(sink_ref, q_ref, k_ref, v_ref, o_ref, lse_ref):
        h, i = pl.program_id(0), pl.program_id(1)
        start, bias = _band(i, T, Lc)
        kc, vc = k_ref[0:Lc, :], v_ref[0:Lc, :]
        kb, vb = k_ref[pl.ds(start, BAND), :], v_ref[pl.ds(start, BAND), :]
        for g in range(G):
            sl = slice(g * HEAD_DIM, (g + 1) * HEAD_DIM)
            qg = q_ref[:, sl]
            sk = sink_ref[h * G + g] * LOG2_E
            sc = _nt(qg, kc)
            sb = _nt(qg, kb) + bias
            m = jnp.maximum(jnp.maximum(jnp.max(sc, axis=1, keepdims=True), jnp.max(sb, axis=1, keepdims=True)), sk)
            pc, pb = jnp.exp2(sc - m), jnp.exp2(sb - m)
            l = jnp.sum(pc, axis=1, keepdims=True) + jnp.sum(pb, axis=1, keepdims=True) + jnp.exp2(sk - m)
            o_ref[:, sl] = (_nn(_bf(pc), vc) + _nn(_bf(pb), vb)) / l
            lse_ref[:, sl] = jnp.broadcast_to(m + jnp.log2(l), (ROW_TILE, HEAD_DIM))

    qs, kvs = _att_specs(T, G)
    return pl.pallas_call(
        body, name=name, grid=(cfg['NKV'], T // ROW_TILE),
        in_specs=[pl.BlockSpec(memory_space=pltpu.SMEM), qs, kvs, kvs], out_specs=[qs, qs],
        out_shape=[jax.ShapeDtypeStruct(q.shape, F32), jax.ShapeDtypeStruct(q.shape, F32)],
        compiler_params=_params())(sink, q, k, v)


def _attn_win_bwd(name, q, k, v, o, lse, dmix, sink, cfg):
    T, G, Lc = q.shape[0], cfg['G'], cfg['Lc']

    def body(sink_ref, q_ref, k_ref, v_ref, o_ref, lse_ref, do_ref, dq_ref, dk_ref, dv_ref, dsink_ref):
        h, i = pl.program_id(0), pl.program_id(1)
        start, bias = _band(i, T, Lc)
        kc, vc = k_ref[0:Lc, :], v_ref[0:Lc, :]
        kb, vb = k_ref[pl.ds(start, BAND), :], v_ref[pl.ds(start, BAND), :]

        @pl.when(i == 0)
        def _():
            dk_ref[...] = jnp.zeros_like(dk_ref)
            dv_ref[...] = jnp.zeros_like(dv_ref)
            dsink_ref[...] = jnp.zeros_like(dsink_ref)

        for g in range(G):
            sl = slice(g * HEAD_DIM, (g + 1) * HEAD_DIM)
            qg, do = q_ref[:, sl], do_ref[:, sl]
            lse = lse_ref[:, g * HEAD_DIM:g * HEAD_DIM + 1]
            delta = jnp.sum(do * o_ref[:, sl], axis=1, keepdims=True)
            pc = jnp.exp2(_nt(qg, kc) - lse)
            pb = jnp.exp2(_nt(qg, kb) + bias - lse)
            ps = jnp.exp2(sink_ref[h * G + g] * LOG2_E - lse)
            dob = _bf(do)
            dv_ref[0:Lc, :] += _tn(_bf(pc), dob)
            dv_ref[pl.ds(start, BAND), :] += _tn(_bf(pb), dob)
            dsc = _bf(pc * (_nt(dob, vc) - delta))
            dsb = _bf(pb * (_nt(dob, vb) - delta))
            dq_ref[:, sl] = _nn(dsc, kc) + _nn(dsb, kb)
            dk_ref[0:Lc, :] += _tn(dsc, qg)
            dk_ref[pl.ds(start, BAND), :] += _tn(dsb, qg)
            dsk = jnp.where(i > 0, -jnp.sum(ps * delta, axis=0, keepdims=True), 0.0)
            dsink_ref[:, sl] += jnp.broadcast_to(dsk, (8, HEAD_DIM))

    qs, kvs = _att_specs(T, G)
    return pl.pallas_call(
        body, name=name, grid=(cfg['NKV'], T // ROW_TILE),
        in_specs=[pl.BlockSpec(memory_space=pltpu.SMEM), qs, kvs, kvs, qs, qs, qs],
        out_specs=[qs, kvs, kvs, pl.BlockSpec((None, 8, G * HEAD_DIM), lambda h, i: (h, 0, 0))],
        out_shape=[jax.ShapeDtypeStruct(q.shape, F32), jax.ShapeDtypeStruct(k.shape, F32), jax.ShapeDtypeStruct(k.shape, F32),
                   jax.ShapeDtypeStruct((cfg['NKV'], 8, G * HEAD_DIM), F32)],
        compiler_params=_params())(sink, q, k, v, o, lse, dmix)


def _seq_pos(T, Lc):
    row = lax.broadcasted_iota(jnp.int32, (T, 1), 0)
    return jnp.where(row < Lc, row, row - Lc), jnp.where(row < Lc, Lc, T - Lc)


def _fw(x, k, pos, seglen):
    return jnp.where(pos + k < seglen, pltpu.roll(x, x.shape[0] - k, 0), 0.0)


def _bw(x, k, pos):
    return jnp.where(pos - k >= 0, pltpu.roll(x, k, 0), 0.0)


def _conv_fwd(name, P, conv_w8, cfg):
    T, Lc = P.shape[0], cfg['Lc']
    cb = (cfg['ATT'] + 2 * cfg['KVW']) // HEAD_DIM
    na = AUX_WIDTH // HEAD_DIM

    def body(gb_ref, gc_ref, u_ref, w_ref, o_ref):
        pos, seglen = _seq_pos(T, Lc)
        z = gc_ref[...] * u_ref[...]
        conv = w_ref[0:1, :] * _bw(z, 1, pos) + w_ref[1:2, :] * z + w_ref[2:3, :] * _fw(z, 1, pos, seglen)
        o_ref[...] = gb_ref[...] * conv

    col = lambda off: pl.BlockSpec((T, HEAD_DIM), lambda c: (0, cb + off + c))
    return pl.pallas_call(
        body, name=name, grid=(na,),
        in_specs=[col(0), col(na), col(2 * na), pl.BlockSpec((8, HEAD_DIM), lambda c: (0, c))],
        out_specs=pl.BlockSpec((T, HEAD_DIM), lambda c: (0, c)),
        out_shape=jax.ShapeDtypeStruct((T, AUX_WIDTH), F32), compiler_params=_params())(P, P, P, conv_w8)


def _conv_bwd(name, P, conv_w8, dmix, cfg):
    T, Lc = P.shape[0], cfg['Lc']
    cb = (cfg['ATT'] + 2 * cfg['KVW']) // HEAD_DIM
    ob = cfg['ATT'] // HEAD_DIM
    na = AUX_WIDTH // HEAD_DIM

    def body(gb_ref, gc_ref, u_ref, w_ref, do_ref, dgb_ref, dgc_ref, du_ref, dw_ref):
        pos, seglen = _seq_pos(T, Lc)
        gc, u, do = gc_ref[...], u_ref[...], do_ref[...]
        z = gc * u
        zm, zp = _bw(z, 1, pos), _fw(z, 1, pos, seglen)
        w0, w1, w2 = w_ref[0:1, :], w_ref[1:2, :], w_ref[2:3, :]
        dgb_ref[...] = (do * (w0 * zm + w1 * z + w2 * zp)).astype(dgb_ref.dtype)
        dc = do * gb_ref[...]
        dz = w0 * _fw(dc, 1, pos, seglen) + w1 * dc + w2 * _bw(dc, 1, pos)
        dgc_ref[...] = (dz * u).astype(dgc_ref.dtype)
        du_ref[...] = (dz * gc).astype(du_ref.dtype)
        dw_ref[...] = jnp.zeros_like(dw_ref)
        dw_ref[0:1, :] = jnp.sum(dc * zm, axis=0, keepdims=True)
        dw_ref[1:2, :] = jnp.sum(dc * z, axis=0, keepdims=True)
        dw_ref[2:3, :] = jnp.sum(dc * zp, axis=0, keepdims=True)

    col = lambda off: pl.BlockSpec((T, HEAD_DIM), lambda c: (0, cb + off + c))
    wspec = pl.BlockSpec((8, HEAD_DIM), lambda c: (0, c))
    ocol = lambda off: pl.BlockSpec((T, HEAD_DIM), lambda c: (0, off + c))
    return pl.pallas_call(
        body, name=name, grid=(na,),
        in_specs=[col(0), col(na), col(2 * na), wspec, ocol(ob)],
        out_specs=[ocol(0), ocol(0), ocol(0), wspec],
        out_shape=[jax.ShapeDtypeStruct((T, AUX_WIDTH), BF16)] * 3 + [jax.ShapeDtypeStruct((8, AUX_WIDTH), F32)],
        compiler_params=_params())(P, P, P, conv_w8, dmix)


def _window_sums(x, half, pos, seglen):
    fwd, bwd = x, x
    s = 1
    while s < half:
        fwd = fwd + _fw(fwd, s, pos, seglen)
        bwd = bwd + _bw(bwd, s, pos)
        s *= 2
    return fwd, bwd


def _pooled(u, half, pos, seglen):
    fwd, bwd = _window_sums(u, half, pos, seglen)
    cnt = (jnp.minimum(pos + half, seglen) - jnp.maximum(pos - half, 0)).astype(F32)
    return (fwd + _bw(bwd, 1, pos)) / cnt - u, cnt


def _pool_fwd(name, P, pool_w, pool_scale, cfg):
    T, Lc = P.shape[0], cfg['Lc']
    cb = (cfg['ATT'] + 2 * cfg['KVW']) // HEAD_DIM

    def body(u_ref, w_ref, s_ref, o_ref):
        g = pl.program_id(0)
        pos, seglen = _seq_pos(T, Lc)
        for k, half in enumerate(POOL_HALF):
            @pl.when(g == k)
            def _(half=half):
                pooled, _ = _pooled(u_ref[...], half, pos, seglen)
                o_ref[...] = _nn(_bf(pooled), _bf(w_ref[...])) * s_ref[...]

    return pl.pallas_call(
        body, name=name, grid=(AUX_GROUPS,),
        in_specs=[pl.BlockSpec((T, HEAD_DIM), lambda g: (0, cb + g)), pl.BlockSpec((None, HEAD_DIM, HEAD_DIM), lambda g: (g, 0, 0)),
                  pl.BlockSpec((1, HEAD_DIM), lambda g: (0, g))],
        out_specs=pl.BlockSpec((T, HEAD_DIM), lambda g: (0, g)),
        out_shape=jax.ShapeDtypeStruct((T, AUX_WIDTH), F32), compiler_params=_params())(P, pool_w, pool_scale)


def _pool_bwd(name, P, pool_w, pool_scale, dmix, cfg):
    T, Lc = P.shape[0], cfg['Lc']
    cb = (cfg['ATT'] + 2 * cfg['KVW']) // HEAD_DIM
    ob = cfg['ATT'] // HEAD_DIM

    def body(u_ref, w_ref, s_ref, do_ref, du_ref, dw_ref, ds_ref):
        g = pl.program_id(0)
        pos, seglen = _seq_pos(T, Lc)
        for k, half in enumerate(POOL_HALF):
            @pl.when(g == k)
            def _(half=half):
                do = do_ref[...]
                pooled, cnt = _pooled(u_ref[...], half, pos, seglen)
                wb = _bf(w_ref[...])
                mixed = _nn(_bf(pooled), wb)
                ds_ref[...] = jnp.broadcast_to(jnp.sum(do * mixed, axis=0, keepdims=True), ds_ref.shape)
                dmixed = _bf(do * s_ref[...])
                dw_ref[...] = _tn(_bf(pooled), dmixed)
                dpooled = _nt(dmixed, wb)
                e = dpooled / cnt
                fwd, bwd = _window_sums(e, half, pos, seglen)
                adj = fwd + _fw(e, half, pos, seglen) + _bw(bwd, 1, pos) - _bw(e, half, pos)
                du_ref[...] = (adj - dpooled).astype(du_ref.dtype)

    wspec = pl.BlockSpec((None, HEAD_DIM, HEAD_DIM), lambda g: (g, 0, 0))
    return pl.pallas_call(
        body, name=name, grid=(AUX_GROUPS,),
        in_specs=[pl.BlockSpec((T, HEAD_DIM), lambda g: (0, cb + g)), wspec, pl.BlockSpec((1, HEAD_DIM), lambda g: (0, g)),
                  pl.BlockSpec((T, HEAD_DIM), lambda g: (0, ob + g))],
        out_specs=[pl.BlockSpec((T, HEAD_DIM), lambda g: (0, g)), wspec, pl.BlockSpec((8, HEAD_DIM), lambda g: (0, g))],
        out_shape=[jax.ShapeDtypeStruct((T, AUX_WIDTH), BF16), jax.ShapeDtypeStruct(pool_w.shape, F32),
                   jax.ShapeDtypeStruct((8, AUX_WIDTH), F32)],
        compiler_params=_params())(P, pool_w, pool_scale, dmix)


def _ffn_up(name, hn, wg3, wu3, dep=None):
    T, D = hn.shape
    J, k, _ = wg3.shape
    tm = _row_tile(T, 1088)
    dep_ins, dep_specs = _dep(dep, 2)

    def body(x_ref, wg_ref, wu_ref, *rest):
        s_ref, ud_ref, a_ref = rest[len(dep_ins):]
        x = x_ref[...]
        g, u = _nt(x, wg_ref[...]), _nt(x, wu_ref[...])
        sig = jax.nn.sigmoid(g)
        silu = g * sig
        s_ref[...] = silu.astype(s_ref.dtype)
        ud_ref[...] = (u * (sig * (1 + g * (1 - sig)))).astype(ud_ref.dtype)
        a_ref[...] = (silu * u).astype(a_ref.dtype)

    wspec = pl.BlockSpec((None, k, D), lambda j, i: (j, 0, 0))
    ospec = pl.BlockSpec((None, tm, k), lambda j, i: (j, i, 0))
    return pl.pallas_call(
        body, name=name, grid=(J, T // tm), in_specs=[pl.BlockSpec((tm, D), lambda j, i: (i, 0)), wspec, wspec] + dep_specs,
        out_specs=[ospec, ospec, ospec],
        out_shape=[jax.ShapeDtypeStruct((J, T, k), BF16)] * 3,
        compiler_params=_params())(hn, wg3, wu3, *dep_ins)


def _ffn_dact(name, dF, wd3, silu_g, u_dsilu):
    T, D = dF.shape
    J, k, _ = wd3.shape
    tm = _row_tile(T, 1088)

    def body(df_ref, wd_ref, s_ref, ud_ref, dg_ref, du_ref):
        da = _nt(df_ref[...], wd_ref[...])
        du_ref[...] = (da * s_ref[...].astype(F32)).astype(du_ref.dtype)
        dg_ref[...] = (da * ud_ref[...].astype(F32)).astype(dg_ref.dtype)

    aspec = pl.BlockSpec((None, tm, k), lambda j, i: (j, i, 0))
    return pl.pallas_call(
        body, name=name, grid=(J, T // tm),
        in_specs=[pl.BlockSpec((tm, D), lambda j, i: (i, 0)), pl.BlockSpec((None, k, D), lambda j, i: (j, 0, 0)), aspec, aspec],
        out_specs=[aspec, aspec],
        out_shape=[jax.ShapeDtypeStruct((J, T, k), BF16), jax.ShapeDtypeStruct((J, T, k), BF16)],
        compiler_params=_params())(dF, wd3, silu_g, u_dsilu)


def _loss_head(name, h, g, target, cfg):
    T, D = h.shape

    def body(h_ref, g_ref, t_ref, dh_ref, loss_ref, dg_ref):
        i = pl.program_id(0)

        @pl.when(i == 0)
        def _():
            dh_ref[...] = jnp.zeros_like(dh_ref)
            loss_ref[...] = jnp.zeros_like(loss_ref)
            dg_ref[...] = jnp.zeros_like(dg_ref)

        @pl.when(i > 0)
        def _():
            x = h_ref[...]
            r = _rstd(x)
            xhat = x * r
            gg = g_ref[...]
            err = xhat * gg - t_ref[...]
            loss_ref[...] += 0.5 * jnp.sum(jnp.sum(err * err, axis=1, keepdims=True) / D, axis=0, keepdims=True)
            dy = err / D
            dg_ref[0:1, :] += jnp.sum(dy * xhat, axis=0, keepdims=True)
            dxh = dy * gg
            dh_ref[...] = r * (dxh - xhat * jnp.mean(dxh * xhat, axis=-1, keepdims=True))

    row = pl.BlockSpec((ROW_TILE, D), lambda i: (i, 0))
    return pl.pallas_call(
        body, name=name, grid=(T // ROW_TILE,),
        in_specs=[row, pl.BlockSpec((1, D), lambda i: (0, 0)), pl.BlockSpec((ROW_TILE, D), lambda i: (jnp.maximum(i - 1, 0), 0))],
        out_specs=[row, pl.BlockSpec((8, 128), lambda i: (0, 0)), pl.BlockSpec((8, D), lambda i: (0, 0))],
        out_shape=[jax.ShapeDtypeStruct((T, D), F32), jax.ShapeDtypeStruct((8, 128), F32), jax.ShapeDtypeStruct((8, D), F32)],
        compiler_params=_params())(h, g, target)


def _adamw(name, parts, w, m, v, dep=None):
    R, C = w.shape
    n_parts = parts.shape[0]
    tr = _row_tile(R, max(16, (1 << 18) // C)) if R % 16 == 0 else R
    bc1 = 1.0 - ADAM_B1 ** ADAM_STEP
    bc2 = 1.0 - ADAM_B2 ** ADAM_STEP
    dep_ins, dep_specs = _dep(dep, 1)

    def body(p_ref, w_ref, m_ref, v_ref, *rest):
        g_ref, d_ref, nm_ref, nv_ref = rest[len(dep_ins):]
        g = p_ref[0].astype(F32)
        for k in range(1, n_parts):
            g = g + p_ref[k].astype(F32)
        nm = ADAM_B1 * m_ref[...] + (1.0 - ADAM_B1) * g
        nv = ADAM_B2 * v_ref[...] + (1.0 - ADAM_B2) * (g * g)
        g_ref[...] = g
        nm_ref[...] = nm
        nv_ref[...] = nv
        d_ref[...] = -ADAM_LR * ((nm / bc1) / (jnp.sqrt(nv / bc2) + ADAM_EPS) + ADAM_WD * w_ref[...])

    blk = pl.BlockSpec((tr, C), lambda i: (i, 0))
    return pl.pallas_call(
        body, name=name, grid=(R // tr,), in_specs=[pl.BlockSpec((n_parts, tr, C), lambda i: (0, i, 0)), blk, blk, blk] + dep_specs,
        out_specs=[blk] * 4, out_shape=[jax.ShapeDtypeStruct((R, C), F32)] * 4, compiler_params=_params())(parts, w, m, v, *dep_ins)


class _WeightStream:
    def __init__(self, cast):
        self.cast, self.handles = cast, {}

    @staticmethod
    def _tag(l, group):
        return ("ffn" if group is FFN_WEIGHTS else group[0]) + str(l)

    def start(self, l, group, after=None):
        self.handles[l, group], token = _gather_start(f"gather_{self._tag(l, group)}_start", [self.cast(l, n) for n in group], after)
        return token

    def relay(self, l, group, after):
        self.handles[l, group], token = _gather_relay(f"gather_{self._tag(l, group)}_relay", self.handles[l, group], after)
        return token

    def get(self, l, group, after):
        got = dict(zip(group, _gather_wait(f"gather_{self._tag(l, group)}_wait", self.handles[l, group], after)))
        if 'w_out' in got:
            rows, cols = got['w_out'].shape[1:]
            got['w_out'] = got['w_out'].reshape(N_DEV * rows, cols)
        return got


def _layer_fwd(l, h, p, stream, mod, rope, conv_w8, cfg):
    nm = f"l{l}_"
    mod = mod + stream.relay(l, IN_WEIGHT, h)[0, 0]
    xn = _norm_mod(nm + "norm1", h, p['norm1_g'], mod, 0)
    W = stream.get(l, IN_WEIGHT, xn)
    token = None
    if l == 0:
        token = stream.start(0, OUT_WEIGHT, after=W['w_in']) + stream.start(0, FFN_WEIGHTS, after=W['w_in'])
    P = _mm_cols(nm + "w_in", xn, W['w_in'], dep=token)
    qr, kr, vb = _qk_prep(nm + "qk_prep", P, p['q_norm_g'], p['k_norm_g'], rope[0], rope[1], cfg)
    if l == 0:
        o, lse = _attn_dense_fwd(nm + "attn", qr, kr, vb, cfg)
        aux = _conv_fwd(nm + "conv", P, conv_w8, cfg)
    else:
        o, lse = _attn_win_fwd(nm + "attn", qr, kr, vb, p['sink'], cfg)
        aux = _pool_fwd(nm + "pool", P, p['pool_w'], p['pool_scale'], cfg)
    mix = jnp.concatenate([o, aux], axis=1).astype(BF16)
    W.update(stream.get(l, OUT_WEIGHT, stream.relay(l, OUT_WEIGHT, mix)))
    y, h2 = _mm_plain(nm + "w_out", mix, W['w_out'], False, dep=stream.relay(l, FFN_WEIGHTS, o), res=(h, mod, 2, cfg['Lc']))
    hn = _norm_mod(nm + "norm2", h2, p['norm2_g'], mod, 1)
    W.update(stream.get(l, FFN_WEIGHTS, hn))
    token = stream.start(1, IN_WEIGHT, after=W['w_down']) if l == 0 else None
    silu_g, u_dsilu, A = _ffn_up(nm + "ffn_up", hn, W['w_gate'], W['w_up'], dep=token)
    if l == 0:
        token = stream.start(1, OUT_WEIGHT, after=A) + stream.start(1, FFN_WEIGHTS, after=A)
    F, h3 = _mm_shards_nn(nm + "w_down", A, W['w_down'], dep=token, res=(h2, mod, 5, cfg['Lc']))
    saved = dict(h=h, xn=xn, P=P, qr=qr, kr=kr, vb=vb, o=o, lse=lse, mix=mix, y=y, h2=h2, hn=hn, silu_g=silu_g, u_dsilu=u_dsilu, A=A, F=F)
    return h3, saved, W


def _layer_bwd(l, dh3, s, p, W, mod, rope, conv_w8, cfg):
    nm = f"l{l}_bwd_"
    J = N_DEV
    dF, dmod = _gate_bwd(nm + "res2", dh3, s['F'], mod, 5)
    dG, dU = _ffn_dact(nm + "ffn_act", dF, W['w_down'], s['silu_g'], s['u_dsilu'])
    big = {'w_down': _wgrad_down(nm + "dw_down", s['A'], dF),
           'w_gate': _wgrad_down(nm + "dw_gate", dG, s['hn']),
           'w_up': _wgrad_down(nm + "dw_up", dU, s['hn'])}
    handles = {}
    handles['ffn'], token = _exchange_start(f"scatter_ffn{l}_start", [big[n] for n in FFN_WEIGHTS], True)
    mod = mod + token[0, 0]
    dhn = _mm_shards_nn2(nm + "dhn", dG, W['w_gate'], dU, W['w_up'])
    dh2, dm, dg2 = _norm_mod_bwd(nm + "norm2", dhn, s['h2'], p['norm2_g'], mod, 1, dh3)
    dmod += dm
    dY, dm = _gate_bwd(nm + "res1", dh2, s['y'], mod, 2)
    dmod += dm
    dwo = _wgrad_rows(nm + "dw_out", s['mix'], dY)
    handles['w_out'], token = _exchange_start(f"scatter_w_out{l}_start", [dwo.reshape((J, dwo.shape[0] // J, dwo.shape[1]))], True)
    dmix = _mm_plain(nm + "dmix", dY, W['w_out'], True, dep=token)
    small = {'norm2_g': dg2[0]}
    if l == 0:
        dqr, dkr, dv = _attn_dense_bwd(nm + "attn", s['qr'], s['kr'], s['vb'], s['o'], s['lse'], dmix, cfg)
        *daux, dcw = _conv_bwd(nm + "conv", s['P'], conv_w8, dmix, cfg)
        small['conv_w'] = dcw[0:3]
    else:
        dqr, dkr, dv, dsk = _attn_win_bwd(nm + "attn", s['qr'], s['kr'], s['vb'], s['o'], s['lse'], dmix, p['sink'], cfg)
        du, dpw, dps = _pool_bwd(nm + "pool", s['P'], p['pool_w'], p['pool_scale'], dmix, cfg)
        daux = [du]
        small.update(sink=dsk[:, 0, ::HEAD_DIM].reshape(-1), pool_w=dpw, pool_scale=dps[0])
    dq, dk, dqg, dkg = _qk_prep_bwd(nm + "qk_prep", dqr, dkr, s['P'], p['q_norm_g'], p['k_norm_g'], rope[0], rope[1], cfg)
    small.update(q_norm_g=dqg[0], k_norm_g=dkg[0])
    dP = jnp.concatenate([dq, dk, dv.astype(BF16), *daux], axis=1)
    handles['w_in'], token = _exchange_start(f"scatter_w_in{l}_start", [_wgrad_cols(nm + "dw_in", s['xn'], dP, J)], True)
    mod = mod + token[0, 0]
    dxn = _mm_cols_nt(nm + "dxn", dP, W['w_in'])
    dh, dm, dg1 = _norm_mod_bwd(nm + "norm1", dxn, s['h'], p['norm1_g'], mod, 0, dh2, latent_only=(l == 0))
    dmod += dm
    small['norm1_g'] = dg1[0]
    return dh, dmod, small, handles, token


def _rope_tables(S, Lc):
    half = HEAD_DIM // 4
    pos = np.arange(S)
    inv = ROPE_THETA ** (-np.arange(0, 2 * half, 2, dtype=np.float32) / (2 * half))
    inv = jnp.asarray(inv, F32)
    ang_r = jnp.asarray(pos // GRID_W, F32)[:, None] * inv
    ang_c = jnp.asarray(pos % GRID_W, F32)[:, None] * inv
    cos = jnp.concatenate([jnp.cos(ang_r)] * 2 + [jnp.cos(ang_c)] * 2, axis=1)
    sin = jnp.concatenate([-jnp.sin(ang_r), jnp.sin(ang_r), -jnp.sin(ang_c), jnp.sin(ang_c)], axis=1)
    return (jnp.concatenate([jnp.ones((Lc, HEAD_DIM), F32), cos], axis=0),
            jnp.concatenate([jnp.zeros((Lc, HEAD_DIM), F32), sin], axis=0))


def _pad_rows(a, rows):
    return jnp.concatenate([a, jnp.zeros((rows - a.shape[0],) + a.shape[1:], a.dtype)], axis=0)


def _flat128(a, nlead):
    lead = a.shape[:nlead]
    f = a.reshape(lead + (-1,))
    pad = (-f.shape[-1]) % 128
    if pad:
        f = jnp.concatenate([f, jnp.zeros(lead + (pad,), f.dtype)], axis=-1)
    return f.reshape(lead + (-1, 128))


def _pack(named, nlead=0):
    rows, layout, at = [], {}, 0
    for name, a in named:
        f = _flat128(a, nlead)
        n = f.shape[-2]
        pad = (-n) % 8
        if pad:
            f = jnp.concatenate([f, jnp.zeros(f.shape[:-2] + (pad, 128), f.dtype)], axis=-2)
        layout[name] = (at, n, a.shape[nlead:])
        rows.append(f)
        at += n + pad
    return jnp.concatenate(rows, axis=-2), layout


def _unpack(arr, layout, name):
    at, n, shape = layout[name]
    return arr[..., at:at + n, :].reshape(arr.shape[:-2] + (-1,))[..., :math.prod(shape)].reshape(arr.shape[:-2] + tuple(shape))


def kernel(*args):
    A = dict(zip(INPUT_NAMES, args, strict=True))
    x, ctx = A['x'][0], A['ctx'][0]
    S, D = x.shape
    Lc = ctx.shape[0]
    T = Lc + S
    ATT = D - AUX_WIDTH
    KVW = (A['l1_w_in'].shape[1] * N_DEV - ATT - AUX_WIDTH) // 2
    cfg = dict(ATT=ATT, KVW=KVW, NKV=KVW // HEAD_DIM, G=ATT // KVW, Lc=Lc)
    assert Lc == ROW_TILE and S % ROW_TILE == 0 and T >= BAND and S % GRID_W == 0
    cw = A['l0_conv_w'].shape[1]
    me = 4 * lax.axis_index("x") + 2 * lax.axis_index("y") + lax.axis_index("c")

    def layer_params(l):
        pre = f"l{l}_"
        return {k[len(pre):]: (v.reshape(1, -1) if v.ndim == 1 and k != 'l1_sink' else v) for k, v in A.items() if k.startswith(pre)}

    params = [layer_params(0), layer_params(1)]

    def cast(l, n):
        w = A[f'l{l}_{n}']
        return (w.T if n in TRANSPOSED else w).astype(BF16)

    stream = _WeightStream(cast)
    token = stream.start(0, IN_WEIGHT)

    big_names = [n for n in WEIGHT_NAMES if n[3:] in BIG_WEIGHTS + ('w_mod',)]
    rest = [n for n in WEIGHT_NAMES if n not in big_names]
    early = ['x', 'ctx'] + rest + ['m_' + n for n in rest] + ['v_' + n for n in rest]
    token, held = lax.optimization_barrier((token, [A[n] for n in early]))
    A.update(zip(early, held))
    x, ctx = A['x'][0], A['ctx'][0]
    wp, layw = _pack([(n, A[n]) for n in rest])
    mp, _ = _pack([(n, A['m_' + n]) for n in rest])
    vp, _ = _pack([(n, A['v_' + n]) for n in rest])
    rope = _rope_tables(S, Lc)
    h = jnp.concatenate([ctx, x], axis=0)

    sc_own = jax.nn.silu(A['c']) + token[0, 0]
    first, lay0 = _pack([('sc', sc_own), ('conv_w', A['l0_conv_w'])])
    first, h, wp, mp, vp = lax.optimization_barrier((first, h, wp, mp, vp))
    first_all = _exchange("gather_cond", [first], False)[0]
    sc_all = _unpack(first_all, lay0, 'sc')[:, 0]
    conv_w = _unpack(first_all, lay0, 'conv_w').transpose(1, 0, 2).reshape(3, N_DEV * cw)
    conv_w8 = _pad_rows(conv_w, 8)
    sc_ctx = jax.nn.silu(A['c_ctx'])
    s16 = _pad_rows(jnp.concatenate([sc_all, sc_ctx[None]], axis=0), 16)

    nmod = A['l0_w_mod'].shape[1]
    modp = jnp.concatenate([_mm_plain(f"l{l}_mod", s16, A[f'l{l}_w_mod'], False) for l in range(2)], axis=1)
    modp_all = _exchange("gather_mod", [modp], False)[0]
    mods = []
    for l in range(2):
        full = modp_all[:, :, l * nmod:(l + 1) * nmod].transpose(1, 0, 2).reshape(16, N_MOD * D) + A[f'l{l}_b_mod'][None]
        both = jnp.stack([full[8], lax.dynamic_index_in_dim(full, me, 0, keepdims=False)]).reshape(2, N_MOD, D)
        mods.append(jnp.concatenate([both, jnp.zeros((2, 8 - N_MOD, D), F32)], axis=1))

    saved, W = [], []
    for l in range(2):
        h, s, Wl = _layer_fwd(l, h, params[l], stream, mods[l], rope, conv_w8, cfg)
        saved.append(s)
        W.append(Wl)

    dh, loss_blk, dgf = _loss_head("loss_head", h, A['final_norm_g'].reshape(1, -1), A['loss_target'][0], cfg)
    loss = lax.psum(loss_blk[0, 0], ("x", "y", "c"))

    grads, small, dmods, scatters = {}, {'final_norm_g': dgf[0]}, [None, None], [None, None]
    token = jnp.zeros((8, 128), F32)
    for l in (1, 0):
        dh, dmods[l], sm, scatters[l], token = _layer_bwd(l, dh, saved[l], params[l], W[l], mods[l] + token[0, 0], rope, conv_w8, cfg)
        small.update({f'l{l}_{k}': v for k, v in sm.items()})
    grad_x = dh[None]

    def landed(l, key, after):
        group = FFN_WEIGHTS if key == 'ffn' else (key,)
        for n, parts in zip(group, _exchange_wait(f"scatter_{key}{l}_wait", scatters[l][key], after)):
            shape = A[f'l{l}_{n}'].shape
            grads[f'l{l}_{n}'] = (parts.reshape((N_DEV,) + (shape[::-1] if n in TRANSPOSED else shape)), None)

    out = {}

    def adam(n, dep=None):
        w, m, v = A[n], A['m_' + n], A['v_' + n]
        if n[3:] in TRANSPOSED:
            out[n] = tuple(r.T for r in _adamw("adamw_" + n, grads[n][0], w.T, m.T, v.T, dep))
        else:
            out[n] = _adamw("adamw_" + n, grads[n][0], w, m, v, dep)
        return out[n][1]

    last = dh
    for l in (1, 0):
        for key in ('ffn', 'w_out') + (('w_in',) if l == 1 else ()):
            landed(l, key, last)
            for n in (FFN_WEIGHTS if key == 'ffn' else (key,)):
                last = adam(f'l{l}_{n}', token)

    small_names = [n for n in WEIGHT_NAMES if n in small]
    pieces = [(n, small[n]) for n in small_names]
    for l in range(2):
        pieces += [(f'dmod{l}', dmods[l][1, :N_MOD]), (f'dcmod{l}', dmods[l][0, :N_MOD])]
    second, lay1 = _pack(pieces)
    second_all = _exchange("gather_small", [second], False, after=last)[0]

    dsc_part = jnp.zeros((16, D), F32)
    for l in range(2):
        dm16 = _pad_rows(jnp.concatenate([_unpack(second_all, lay1, f'dmod{l}').reshape(N_DEV, N_MOD * D),
                                          jnp.sum(_unpack(second_all, lay1, f'dcmod{l}'), axis=0).reshape(1, N_MOD * D)], axis=0), 16)
        mine = lax.dynamic_slice_in_dim(dm16, me * nmod, nmod, axis=1)
        tk = _col_tile(D, 512)
        gw = _mm_tn(f"l{l}_dw_mod", s16, mine, pl.BlockSpec((16, tk), lambda kb, t: (0, kb)), pl.BlockSpec((16, nmod), lambda kb, t: (0, 0)),
                    jax.ShapeDtypeStruct((D, nmod), F32), pl.BlockSpec((tk, nmod), lambda kb, t: (kb, 0)), (tk, nmod), (D // tk, 1))
        grads[f'l{l}_w_mod'] = (gw[None], None)
        dsc_part += _mm_plain(f"l{l}_dsc", mine, A[f'l{l}_w_mod'], True)
        dmod_dev = _unpack(second_all, lay1, f'dmod{l}') + _unpack(second_all, lay1, f'dcmod{l}')
        grads[f'l{l}_b_mod'] = (dmod_dev.reshape(N_DEV, N_MOD * D), None)
    dsig = jax.nn.sigmoid(A['c_ctx'])
    dsilu = dsig * (1 + A['c_ctx'] * (1 - dsig))
    third_all = _exchange("gather_dsc", [dsc_part[8:9]], False)[0]
    grads['c_ctx'] = (third_all[:, 0] * dsilu[None], None)
    for n in small_names:
        g8 = _unpack(second_all, lay1, n)
        if n == 'l0_conv_w':
            g8 = lax.dynamic_slice_in_dim(g8, me * cw, cw, axis=2)
        grads[n] = (g8, None)

    adam('l0_w_mod')
    last = adam('l1_w_mod')
    gp, _ = _pack([(n, grads[n][0]) for n in rest], nlead=1)
    res = _adamw("adamw_small", gp, wp, mp, vp)
    for n in rest:
        out[n] = tuple(_unpack(r, layw, n) for r in res)
    landed(0, 'w_in', last)
    adam('l0_w_in')

    outs = [loss, grad_x]
    for k in range(4):
        outs += [out[n][k] for n in WEIGHT_NAMES]
    return tuple(outs)
```

```python
import functools
import math

import numpy as np
import jax
import jax.numpy as jnp
from jax import lax
from jax.experimental import pallas as pl
from jax.experimental.pallas import tpu as pltpu

F32 = jnp.float32
BF16 = jnp.bfloat16
HEAD_DIM = 128
AUX_WIDTH = 512
AUX_GROUPS = 4
POOL_HALF = (1, 2, 4, 8)
WINDOW = 128
GRID_W = 64
ROPE_THETA = 10000.0
EPS = 1e-6
NEG_INF = -1e30
ATT_SCALE = HEAD_DIM ** -0.5
LOG2_E = math.log2(math.e)
Q_SCALE = ATT_SCALE * LOG2_E
N_MOD = 6
N_DEV = 8
ROW_TILE = 256
BAND = ROW_TILE + 2 * WINDOW
ADAM_LR, ADAM_B1, ADAM_B2, ADAM_EPS, ADAM_WD, ADAM_STEP = 0.001, 0.9, 0.999, 1e-08, 0.01, 10
VMEM_LIMIT_MB = 56
MESH = pl.DeviceIdType.MESH
RESIDENT = pl.Buffered(buffer_count=1)

WEIGHT_NAMES = ['c_ctx', 'l0_norm1_g', 'l0_w_mod', 'l0_b_mod', 'l0_w_in', 'l0_q_norm_g', 'l0_k_norm_g', 'l0_conv_w', 'l0_w_out', 'l0_norm2_g', 'l0_w_gate', 'l0_w_up', 'l0_w_down', 'l1_norm1_g', 'l1_w_mod', 'l1_b_mod', 'l1_w_in', 'l1_q_norm_g', 'l1_k_norm_g', 'l1_sink', 'l1_pool_w', 'l1_pool_scale', 'l1_w_out', 'l1_norm2_g', 'l1_w_gate', 'l1_w_up', 'l1_w_down', 'final_norm_g']
INPUT_NAMES = (['x', 'c', 'ctx'] + WEIGHT_NAMES + ['loss_target'] + ['m_' + n for n in WEIGHT_NAMES]
               + ['v_' + n for n in WEIGHT_NAMES])
IN_WEIGHT = ('w_in',)
OUT_WEIGHT = ('w_out',)
MIXER_WEIGHTS = OUT_WEIGHT + IN_WEIGHT
FFN_WEIGHTS = ('w_down', 'w_gate', 'w_up')
TRANSPOSED = ('w_gate', 'w_up')
BIG_WEIGHTS = MIXER_WEIGHTS + FFN_WEIGHTS


def _params(vmem_mb=VMEM_LIMIT_MB):
    return pltpu.CompilerParams(vmem_limit_bytes=vmem_mb << 20)


def _row_tile(n, cap):
    best = None
    for t in range(16, min(n, cap) + 1, 16):
        if n % t == 0:
            best = t
    assert best is not None, (n, cap)
    return best


def _col_tile(n, cap):
    best = n
    for t in range(128, min(n, cap) + 1, 128):
        if n % t == 0:
            best = t
    return best if best <= cap or n % 128 else n


def _dot(a, b, ca, cb):
    return lax.dot_general(a, b, (((ca,), (cb,)), ((), ())), preferred_element_type=F32)


def _nn(a, b):
    return _dot(a, b, 1, 0)


def _nt(a, b):
    return _dot(a, b, 1, 1)


def _tn(a, b):
    return _dot(a, b, 0, 0)


def _bf(x):
    return x.astype(BF16)


def _exchange(name, arrs, scatter, after=None):
    n = len(arrs)
    extra = [] if after is None else [after]
    if scatter:
        out_shape = [jax.ShapeDtypeStruct(a.shape, a.dtype) for a in arrs]
    else:
        out_shape = [jax.ShapeDtypeStruct((N_DEV,) + a.shape, a.dtype) for a in arrs]

    def body(*refs):
        ins, outs = refs[:n], refs[n + len(extra):2 * n + len(extra)]
        send_sems, recv_sems, local_sems = refs[2 * n + len(extra):]
        x, y, c = lax.axis_index("x"), lax.axis_index("y"), lax.axis_index("c")
        me = 4 * x + 2 * y + c
        local, remote = [], []
        for a in range(n):
            own = ins[a].at[me] if scatter else ins[a]
            cp = pltpu.make_async_copy(own, outs[a].at[me], local_sems.at[a])
            cp.start()
            local.append(cp)
            for r in range(1, N_DEV):
                px = 1 - x if r & 4 else x
                py = 1 - y if r & 2 else y
                pc = 1 - c if r & 1 else c
                src = ins[a].at[4 * px + 2 * py + pc] if scatter else ins[a]
                cp = pltpu.make_async_remote_copy(
                    src_ref=src, dst_ref=outs[a].at[me], send_sem=send_sems.at[a, r - 1],
                    recv_sem=recv_sems.at[a, r - 1], device_id=(px, py, pc), device_id_type=MESH)
                cp.start()
                remote.append(cp)
        for cp in remote:
            cp.wait()
        for cp in local:
            cp.wait()

    any_spec = pl.BlockSpec(memory_space=pl.ANY)
    return pl.pallas_call(
        body, name=name, out_shape=out_shape,
        in_specs=[any_spec] * (n + len(extra)), out_specs=[any_spec] * n,
        scratch_shapes=[pltpu.SemaphoreType.DMA((n, N_DEV - 1)), pltpu.SemaphoreType.DMA((n, N_DEV - 1)),
                        pltpu.SemaphoreType.DMA((n,))],
    )(*arrs, *extra)


HBM_SPEC = pl.BlockSpec(memory_space=pltpu.HBM)
SEM_SPEC = pl.BlockSpec(memory_space=pltpu.SEMAPHORE)
EFFECT = pltpu.SideEffectType.DATAFLOW_SIDE_EFFECTING


def _split_copies(srcs, lands, send_sems, recv_sems, local_sems, scatter):
    x, y, c = lax.axis_index("x"), lax.axis_index("y"), lax.axis_index("c")
    me = 4 * x + 2 * y + c
    local, remote = [], []
    for a in range(len(srcs)):
        own = srcs[a].at[me] if scatter else srcs[a]
        local.append(pltpu.make_async_copy(own, lands[a].at[me], local_sems.at[a]))
        for r in range(1, N_DEV):
            px = 1 - x if r & 4 else x
            py = 1 - y if r & 2 else y
            pc = 1 - c if r & 1 else c
            src = srcs[a].at[4 * px + 2 * py + pc] if scatter else srcs[a]
            remote.append(pltpu.make_async_remote_copy(
                src_ref=src, dst_ref=lands[a].at[me], send_sem=send_sems.at[a * (N_DEV - 1) + r - 1],
                recv_sem=recv_sems.at[a * (N_DEV - 1) + r - 1], device_id=(px, py, pc), device_id_type=MESH))
    return local, remote


def _exchange_start(name, arrs, scatter, after=None):
    n = len(arrs)
    extra = [] if after is None else [after]
    shapes = [a.shape if scatter else (N_DEV,) + a.shape for a in arrs]
    lands = [pltpu.with_memory_space_constraint(lax.empty(s, a.dtype), pltpu.HBM) for s, a in zip(shapes, arrs)]
    srcs = [pltpu.with_memory_space_constraint(a, pltpu.HBM) for a in arrs]

    def body(*refs):
        src_refs, land_refs = refs[:n], refs[n:2 * n]
        send_sems, recv_sems, local_sems = refs[2 * n + len(extra):2 * n + len(extra) + 3]
        token = refs[-1]
        local, remote = _split_copies(src_refs, land_refs, send_sems, recv_sems, local_sems, scatter)
        for cp in local + remote:
            cp.start()
        token[...] = jnp.zeros_like(token)

    res = pl.pallas_call(
        body, name=name,
        out_shape=[pltpu.SemaphoreType.DMA((n * (N_DEV - 1),)), pltpu.SemaphoreType.DMA((n * (N_DEV - 1),)), pltpu.SemaphoreType.DMA((n,))]
        + [pltpu.HBM(a.shape, a.dtype) for a in arrs] + [pltpu.HBM(s, a.dtype) for s, a in zip(shapes, arrs)]
        + [jax.ShapeDtypeStruct((8, 128), F32)],
        in_specs=[HBM_SPEC] * (2 * n) + [pl.BlockSpec(memory_space=pl.ANY)] * len(extra),
        out_specs=[SEM_SPEC] * 3 + [HBM_SPEC] * (2 * n) + [pl.BlockSpec(memory_space=pltpu.VMEM)],
        input_output_aliases={i: 3 + i for i in range(2 * n)},
        compiler_params=pltpu.CompilerParams(has_side_effects=EFFECT),
    )(*srcs, *lands, *extra)
    return (scatter, res[:3], res[3:3 + n], res[3 + n:3 + 2 * n]), res[-1]


def _exchange_wait(name, handle, after):
    scatter, sems, srcs, lands = handle
    n = len(srcs)

    def body(*refs):
        src_refs, land_refs = refs[:n], refs[n:2 * n]
        send_sems, recv_sems, local_sems = refs[2 * n:2 * n + 3]
        local, remote = _split_copies(src_refs, land_refs, send_sems, recv_sems, local_sems, scatter)
        for cp in remote:
            cp.wait_send()
            cp.wait_recv()
        for cp in local:
            cp.wait()

    res = pl.pallas_call(
        body, name=name,
        out_shape=[pltpu.HBM(a.shape, a.dtype) for a in srcs] + [pltpu.HBM(a.shape, a.dtype) for a in lands],
        in_specs=[HBM_SPEC] * (2 * n) + [SEM_SPEC] * 3 + [pl.BlockSpec(memory_space=pl.ANY)], out_specs=[HBM_SPEC] * (2 * n),
        input_output_aliases={i: i for i in range(2 * n)},
        compiler_params=pltpu.CompilerParams(has_side_effects=EFFECT),
    )(*srcs, *lands, *sems, after)
    return list(res[n:])


FIRST_COPIES = 4
RELAY_COPIES = 3


def _gather_copies(srcs, lands, sems):
    send_sems, recv_sems, local_sems = sems[:3]
    x, y, c = lax.axis_index("x"), lax.axis_index("y"), lax.axis_index("c")
    me = 4 * x + 2 * y + c
    chips = [(1 - x, y), (x, 1 - y), (1 - x, 1 - y)]
    local, first, relay = [], [], []
    for a in range(len(srcs)):
        local.append(pltpu.make_async_copy(srcs[a], lands[a].at[me], local_sems.at[a]))
        targets = [(x, y, 1 - c)] + [(px, py, c) for px, py in chips]
        first.append([pltpu.make_async_remote_copy(
            src_ref=srcs[a], dst_ref=lands[a].at[me], send_sem=send_sems.at[FIRST_COPIES * a + k],
            recv_sem=recv_sems.at[FIRST_COPIES * a + k], device_id=t, device_id_type=MESH) for k, t in enumerate(targets)])
        if len(sems) > 3:
            rsend, rrecv = sems[3:]
            slots = [lands[a].at[4 * px + 2 * py + c] for px, py in chips]
            relay.append([pltpu.make_async_remote_copy(
                src_ref=slot, dst_ref=slot, send_sem=rsend.at[RELAY_COPIES * a + j], recv_sem=rrecv.at[RELAY_COPIES * a + j],
                device_id=(x, y, 1 - c), device_id_type=MESH) for j, slot in enumerate(slots)])
    return local, first, relay


def _gather_start(name, arrs, after=None):
    n = len(arrs)
    extra = [] if after is None else [after]
    lands = [pltpu.with_memory_space_constraint(lax.empty((N_DEV,) + a.shape, a.dtype), pltpu.HBM) for a in arrs]
    srcs = [pltpu.with_memory_space_constraint(a, pltpu.HBM) for a in arrs]

    def body(*refs):
        at = 2 * n + len(extra)
        local, first, _ = _gather_copies(refs[:n], refs[n:2 * n], refs[at:at + 3])
        for cp in local + [cp for cps in first for cp in cps]:
            cp.start()
        refs[-1][...] = jnp.zeros_like(refs[-1])

    res = pl.pallas_call(
        body, name=name,
        out_shape=[pltpu.SemaphoreType.DMA((FIRST_COPIES * n,)), pltpu.SemaphoreType.DMA((FIRST_COPIES * n,)), pltpu.SemaphoreType.DMA((n,))]
        + [pltpu.HBM(a.shape, a.dtype) for a in arrs] + [pltpu.HBM((N_DEV,) + a.shape, a.dtype) for a in arrs]
        + [jax.ShapeDtypeStruct((8, 128), F32)],
        in_specs=[HBM_SPEC] * (2 * n) + [pl.BlockSpec(memory_space=pl.ANY)] * len(extra),
        out_specs=[SEM_SPEC] * 3 + [HBM_SPEC] * (2 * n) + [pl.BlockSpec(memory_space=pltpu.VMEM)],
        input_output_aliases={i: 3 + i for i in range(2 * n)},
        compiler_params=pltpu.CompilerParams(has_side_effects=EFFECT),
    )(*srcs, *lands, *extra)
    return (list(res[:3]), list(res[3:3 + n]), list(res[3 + n:3 + 2 * n])), res[-1]


def _gather_relay(name, handle, after):
    sems, srcs, lands = handle
    n = len(srcs)

    def body(*refs):
        in_sems = refs[2 * n:2 * n + 3]
        out_sems = refs[2 * n + 4 + 2 * n:2 * n + 4 + 2 * n + 2]
        _, first, relay = _gather_copies(refs[:n], refs[n:2 * n], list(in_sems) + list(out_sems))
        for a in range(n):
            for j in range(RELAY_COPIES):
                first[a][1 + j].wait_recv()
                relay[a][j].start()
        refs[-1][...] = jnp.zeros_like(refs[-1])

    res = pl.pallas_call(
        body, name=name,
        out_shape=[pltpu.HBM(a.shape, a.dtype) for a in srcs] + [pltpu.HBM(a.shape, a.dtype) for a in lands]
        + [pltpu.SemaphoreType.DMA((RELAY_COPIES * n,)), pltpu.SemaphoreType.DMA((RELAY_COPIES * n,)), jax.ShapeDtypeStruct((8, 128), F32)],
        in_specs=[HBM_SPEC] * (2 * n) + [SEM_SPEC] * 3 + [pl.BlockSpec(memory_space=pl.ANY)],
        out_specs=[HBM_SPEC] * (2 * n) + [SEM_SPEC] * 2 + [pl.BlockSpec(memory_space=pltpu.VMEM)],
        input_output_aliases={i: i for i in range(2 * n)},
        compiler_params=pltpu.CompilerParams(has_side_effects=EFFECT),
    )(*srcs, *lands, *sems, after)
    return (sems + list(res[2 * n:2 * n + 2]), list(res[:n]), list(res[n:2 * n])), res[-1]


def _gather_wait(name, handle, after):
    sems, srcs, lands = handle
    n = len(srcs)

    def body(*refs):
        local, first, relay = _gather_copies(refs[:n], refs[n:2 * n], refs[2 * n:2 * n + 5])
        for a in range(n):
            for cp in first[a]:
                cp.wait_send()
            first[a][0].wait_recv()
            for cp in relay[a]:
                cp.wait_send()
                cp.wait_recv()
            local[a].wait()

    res = pl.pallas_call(
        body, name=name,
        out_shape=[pltpu.HBM(a.shape, a.dtype) for a in srcs] + [pltpu.HBM(a.shape, a.dtype) for a in lands],
        in_specs=[HBM_SPEC] * (2 * n) + [SEM_SPEC] * 5 + [pl.BlockSpec(memory_space=pl.ANY)], out_specs=[HBM_SPEC] * (2 * n),
        input_output_aliases={i: i for i in range(2 * n)},
        compiler_params=pltpu.CompilerParams(has_side_effects=EFFECT),
    )(*srcs, *lands, *sems, after)
    return list(res[n:])


def _dep(dep, grid_rank):
    if dep is None:
        return [], []
    return [dep], [pl.BlockSpec((8, 128), (lambda i, j: (0, 0)) if grid_rank == 2 else (lambda i: (0, 0)))]


def _mm_step(name, fn, ins, in_specs, out_shape, out_spec, grid, dep=None, res=None):
    n = len(ins)
    dep_ins, dep_specs = _dep(dep, len(grid))
    res_ins, res_specs, out_shapes, out_specs = [], [], out_shape, out_spec
    if res is not None:
        h, mod, row_idx, lc = res
        tm, tn = out_spec.block_shape
        res_ins = [h, mod]
        res_specs = [pl.BlockSpec((tm, tn), lambda j, i: (i, j)), pl.BlockSpec((2, 8, tn), lambda j, i: (0, 0, j))]
        out_shapes, out_specs = [out_shape, jax.ShapeDtypeStruct(h.shape, h.dtype)], [out_spec, res_specs[0]]

    def body(*refs):
        outs = refs[n + len(res_ins) + len(dep_ins):]
        acc = fn(*refs[:n])
        outs[0][...] = acc.astype(outs[0].dtype)
        if res is not None:
            h_ref, mod_ref = refs[n:n + 2]
            row = pl.program_id(1) * tm + lax.broadcasted_iota(jnp.int32, (tm, 1), 0)
            gate = jnp.where(row < lc, mod_ref[0, row_idx:row_idx + 1, :], mod_ref[1, row_idx:row_idx + 1, :])
            outs[1][...] = h_ref[...] + gate * acc

    return pl.pallas_call(body, name=name, grid=grid, in_specs=list(in_specs) + res_specs + dep_specs, out_specs=out_specs,
                          out_shape=out_shapes, compiler_params=_params())(*ins, *res_ins, *dep_ins)


def _mm_tn(name, a, b, a_spec, b_spec, out_shape, out_spec, acc_shape, grid):
    nk = grid[-1]
    kax = len(grid) - 1
    if nk == 1:
        def whole(a_ref, b_ref, o_ref):
            o_ref[...] = _tn(_bf(a_ref[...]), _bf(b_ref[...])).astype(o_ref.dtype)

        return pl.pallas_call(whole, name=name, grid=grid, in_specs=[a_spec, b_spec], out_specs=out_spec,
                              out_shape=out_shape, compiler_params=_params())(a, b)

    def body(a_ref, b_ref, o_ref, acc_ref):
        k = pl.program_id(kax)

        @pl.when(k == 0)
        def _():
            acc_ref[...] = jnp.zeros_like(acc_ref)

        acc_ref[...] += _tn(_bf(a_ref[...]), _bf(b_ref[...]))

        @pl.when(k == nk - 1)
        def _():
            o_ref[...] = acc_ref[...].astype(o_ref.dtype)

    return pl.pallas_call(body, name=name, grid=grid, in_specs=[a_spec, b_spec], out_specs=out_spec,
                          out_shape=out_shape, scratch_shapes=[pltpu.VMEM(acc_shape, F32)],
                          compiler_params=_params())(a, b)


def _mm_cols(name, a, w3, out_dtype=F32, dep=None):
    M, K = a.shape
    J, _, n = w3.shape
    return _mm_step(
        name, lambda a_ref, w_ref: _nn(_bf(a_ref[...]), w_ref[...]), [a, w3],
        [pl.BlockSpec((M, K), lambda j, i: (0, 0), pipeline_mode=RESIDENT), pl.BlockSpec((None, K, n), lambda j, i: (j, 0, 0))],
        jax.ShapeDtypeStruct((M, J * n), out_dtype), pl.BlockSpec((M, n), lambda j, i: (0, j)), (J, 1), dep)


def _mm_plain(name, a, b, transpose_b, out_dtype=F32, tn=512, dep=None, res=None):
    M, K = a.shape
    N = b.shape[0] if transpose_b else b.shape[1]
    tn = _col_tile(N, tn)
    if res is None:
        tm, a_spec = M, pl.BlockSpec((M, K), lambda j, i: (0, 0), pipeline_mode=RESIDENT)
    else:
        tm = _row_tile(M, 1088)
        a_spec = pl.BlockSpec((tm, K), lambda j, i: (i, 0))
    if transpose_b:
        b_spec = pl.BlockSpec((tn, K), lambda j, i: (j, 0))
        fn = lambda a_ref, b_ref: _nt(_bf(a_ref[...]), _bf(b_ref[...]))
    else:
        b_spec = pl.BlockSpec((K, tn), lambda j, i: (0, j))
        fn = lambda a_ref, b_ref: _nn(_bf(a_ref[...]), _bf(b_ref[...]))
    return _mm_step(name, fn, [a, b], [a_spec, b_spec],
                    jax.ShapeDtypeStruct((M, N), out_dtype), pl.BlockSpec((tm, tn), lambda j, i: (i, j)),
                    (N // tn, M // tm), dep, res)


def _mm_shards_nn(name, a3, w3, tn=512, dep=None, res=None):
    J, M, k = a3.shape
    N = w3.shape[2]
    tm = _row_tile(M, 544)
    tn = _col_tile(N, tn)

    def fn(a_ref, w_ref):
        acc = _nn(a_ref[0], w_ref[0])
        for j in range(1, J):
            acc += _nn(a_ref[j], w_ref[j])
        return acc

    return _mm_step(name, fn, [a3, w3],
                    [pl.BlockSpec((J, tm, k), lambda jn, i: (0, i, 0)), pl.BlockSpec((J, k, tn), lambda jn, i: (0, 0, jn))],
                    jax.ShapeDtypeStruct((M, N), F32), pl.BlockSpec((tm, tn), lambda jn, i: (i, jn)), (N // tn, M // tm), dep, res)


def _mm_shards_nn2(name, a3, w3a, b3, w3b, tn=512):
    J, M, k = a3.shape
    N = w3a.shape[2]
    tm = _row_tile(M, 544)
    tn = _col_tile(N, tn)

    def fn(a_ref, wa_ref, b_ref, wb_ref):
        acc = _nn(a_ref[0], wa_ref[0]) + _nn(b_ref[0], wb_ref[0])
        for j in range(1, J):
            acc += _nn(a_ref[j], wa_ref[j]) + _nn(b_ref[j], wb_ref[j])
        return acc

    act = pl.BlockSpec((J, tm, k), lambda jn, i: (0, i, 0))
    wsp = pl.BlockSpec((J, k, tn), lambda jn, i: (0, 0, jn))
    return _mm_step(name, fn, [a3, w3a, b3, w3b], [act, wsp, act, wsp],
                    jax.ShapeDtypeStruct((M, N), F32), pl.BlockSpec((tm, tn), lambda jn, i: (i, jn)), (N // tn, M // tm))


def _mm_cols_nt(name, a, w3, tn=512):
    M = a.shape[0]
    J, N, n = w3.shape
    tm = _row_tile(M, 544)
    tn = _col_tile(N, tn)

    def fn(a_ref, w_ref):
        acc = _nt(a_ref[:, 0:n], w_ref[0])
        for j in range(1, J):
            acc += _nt(a_ref[:, j * n:(j + 1) * n], w_ref[j])
        return acc

    return _mm_step(name, fn, [a, w3],
                    [pl.BlockSpec((tm, J * n), lambda jn, i: (i, 0)), pl.BlockSpec((J, tn, n), lambda jn, i: (0, jn, 0))],
                    jax.ShapeDtypeStruct((M, N), F32), pl.BlockSpec((tm, tn), lambda jn, i: (i, jn)), (N // tn, M // tm))


def _wgrad_cols(name, a, b, J):
    T, K = a.shape
    n = b.shape[1] // J
    return _mm_tn(name, a, b, pl.BlockSpec((T, K), lambda j, t: (0, 0), pipeline_mode=RESIDENT), pl.BlockSpec((T, n), lambda j, t: (0, j)),
                  jax.ShapeDtypeStruct((J, K, n), BF16), pl.BlockSpec((None, K, n), lambda j, t: (j, 0, 0)), (K, n), (J, 1))


def _wgrad_rows(name, a, b, tk=512):
    T, K = a.shape
    N = b.shape[1]
    tk = _col_tile(K, tk)
    return _mm_tn(name, a, b, pl.BlockSpec((T, tk), lambda kb, t: (0, kb)), pl.BlockSpec((T, N), lambda kb, t: (0, 0), pipeline_mode=RESIDENT),
                  jax.ShapeDtypeStruct((K, N), BF16), pl.BlockSpec((tk, N), lambda kb, t: (kb, 0)), (tk, N), (K // tk, 1))


def _wgrad_up(name, a, b3):
    T, K = a.shape
    J, _, k = b3.shape
    tt = _row_tile(T, 1088)
    return _mm_tn(name, a, b3, pl.BlockSpec((tt, K), lambda j, t: (t, 0)), pl.BlockSpec((None, tt, k), lambda j, t: (j, t, 0)),
                  jax.ShapeDtypeStruct((J, K, k), BF16), pl.BlockSpec((None, K, k), lambda j, t: (j, 0, 0)), (K, k), (J, T // tt))


def _wgrad_down(name, a3, b):
    J, T, k = a3.shape
    N = b.shape[1]
    return _mm_tn(name, a3, b, pl.BlockSpec((None, T, k), lambda j, t: (j, 0, 0)),
                  pl.BlockSpec((T, N), lambda j, t: (0, 0), pipeline_mode=RESIDENT),
                  jax.ShapeDtypeStruct((J, k, N), BF16), pl.BlockSpec((None, k, N), lambda j, t: (j, 0, 0)), (k, N), (J, 1))


def _seg(i):
    return jnp.minimum(i, 1)


def _rstd(x):
    return lax.rsqrt(jnp.mean(x * x, axis=-1, keepdims=True) + EPS)


def _norm_mod(name, h, g, mod, which):
    T, D = h.shape

    def body(h_ref, g_ref, mod_ref, o_ref):
        x = h_ref[...]
        n = x * _rstd(x) * g_ref[...]
        shift = mod_ref[3 * which:3 * which + 1, :]
        scale = mod_ref[3 * which + 1:3 * which + 2, :]
        o_ref[...] = (n * (1 + scale) + shift).astype(o_ref.dtype)

    row = pl.BlockSpec((ROW_TILE, D), lambda i: (i, 0))
    return pl.pallas_call(
        body, name=name, grid=(T // ROW_TILE,),
        in_specs=[row, pl.BlockSpec((1, D), lambda i: (0, 0)), pl.BlockSpec((None, 8, D), lambda i: (_seg(i), 0, 0))],
        out_specs=row, out_shape=jax.ShapeDtypeStruct((T, D), BF16), compiler_params=_params())(h, g, mod)


def _norm_mod_bwd(name, dxn, h, g, mod, which, dres, latent_only=False):
    T, D = h.shape

    def body(dxn_ref, h_ref, g_ref, mod_ref, dres_ref, dh_ref, dmod_ref, dg_ref):
        i = pl.program_id(0)
        x = h_ref[...]
        r = _rstd(x)
        xhat = x * r
        g = g_ref[...]
        n = xhat * g
        scale = mod_ref[3 * which + 1:3 * which + 2, :]
        dxn = dxn_ref[...]
        dn = dxn * (1 + scale)
        dxh = dn * g
        dh = dres_ref[...] + r * (dxh - xhat * jnp.mean(dxh * xhat, axis=-1, keepdims=True))
        if latent_only:
            @pl.when(i > 0)
            def _():
                dh_ref[...] = dh
        else:
            dh_ref[...] = dh

        @pl.when(i <= 1)
        def _():
            dmod_ref[...] = jnp.zeros_like(dmod_ref)

        @pl.when(i == 0)
        def _():
            dg_ref[...] = jnp.zeros_like(dg_ref)

        dmod_ref[3 * which:3 * which + 1, :] += jnp.sum(dxn, axis=0, keepdims=True)
        dmod_ref[3 * which + 1:3 * which + 2, :] += jnp.sum(dxn * n, axis=0, keepdims=True)
        dg_ref[0:1, :] += jnp.sum(dn * xhat, axis=0, keepdims=True)

    row = pl.BlockSpec((ROW_TILE, D), lambda i: (i, 0))
    modspec = pl.BlockSpec((None, 8, D), lambda i: (_seg(i), 0, 0))
    dh_rows = T - ROW_TILE if latent_only else T
    dh_spec = pl.BlockSpec((ROW_TILE, D), lambda i: (jnp.maximum(i - 1, 0), 0)) if latent_only else row
    return pl.pallas_call(
        body, name=name, grid=(T // ROW_TILE,),
        in_specs=[row, row, pl.BlockSpec((1, D), lambda i: (0, 0)), modspec, row],
        out_specs=[dh_spec, modspec, pl.BlockSpec((8, D), lambda i: (0, 0))],
        out_shape=[jax.ShapeDtypeStruct((dh_rows, D), F32), jax.ShapeDtypeStruct((2, 8, D), F32), jax.ShapeDtypeStruct((8, D), F32)],
        compiler_params=_params())(dxn, h, g, mod, dres)


def _gate_bwd(name, dh, y, mod, row_idx):
    T, D = dh.shape

    def body(dh_ref, y_ref, mod_ref, dy_ref, dmod_ref):
        i = pl.program_id(0)
        dh = dh_ref[...]
        dy_ref[...] = (dh * mod_ref[row_idx:row_idx + 1, :]).astype(dy_ref.dtype)

        @pl.when(i <= 1)
        def _():
            dmod_ref[...] = jnp.zeros_like(dmod_ref)

        dmod_ref[row_idx:row_idx + 1, :] += jnp.sum(dh * y_ref[...], axis=0, keepdims=True)

    row = pl.BlockSpec((ROW_TILE, D), lambda i: (i, 0))
    modspec = pl.BlockSpec((None, 8, D), lambda i: (_seg(i), 0, 0))
    return pl.pallas_call(
        body, name=name, grid=(T // ROW_TILE,), in_specs=[row, row, modspec], out_specs=[row, modspec],
        out_shape=[jax.ShapeDtypeStruct((T, D), BF16), jax.ShapeDtypeStruct((2, 8, D), F32)],
        compiler_params=_params())(dh, y, mod)


def _rot(y):
    lane = lax.broadcasted_iota(jnp.int32, y.shape, 1)
    return jnp.where((lane & 32) == 0, pltpu.roll(y, 96, 1), pltpu.roll(y, 32, 1))


def _qk_prep(name, P, q_g, k_g, rope_c, rope_s, cfg):
    T = P.shape[0]
    ATT, KVW = cfg['ATT'], cfg['KVW']

    def body(q_ref, k_ref, v_ref, qg_ref, kg_ref, c_ref, s_ref, qo_ref, ko_ref, vo_ref):
        cc, ss = c_ref[...], s_ref[...]

        def head(x, g):
            y = x * _rstd(x) * g
            return y * cc + _rot(y) * ss

        for hh in range(ATT // HEAD_DIM):
            sl = slice(hh * HEAD_DIM, (hh + 1) * HEAD_DIM)
            qo_ref[:, sl] = (head(q_ref[:, sl], qg_ref[...]) * Q_SCALE).astype(qo_ref.dtype)
        for hh in range(KVW // HEAD_DIM):
            sl = slice(hh * HEAD_DIM, (hh + 1) * HEAD_DIM)
            ko_ref[:, sl] = head(k_ref[:, sl], kg_ref[...]).astype(ko_ref.dtype)
        vo_ref[...] = v_ref[...].astype(vo_ref.dtype)

    kb = ATT // KVW
    gain = pl.BlockSpec((1, HEAD_DIM), lambda i: (0, 0))
    tab = pl.BlockSpec((ROW_TILE, HEAD_DIM), lambda i: (i, 0))
    qs = pl.BlockSpec((ROW_TILE, ATT), lambda i: (i, 0))
    ks = pl.BlockSpec((ROW_TILE, KVW), lambda i: (i, 0))
    return pl.pallas_call(
        body, name=name, grid=(T // ROW_TILE,),
        in_specs=[qs, pl.BlockSpec((ROW_TILE, KVW), lambda i: (i, kb)), pl.BlockSpec((ROW_TILE, KVW), lambda i: (i, kb + 1)),
                  gain, gain, tab, tab],
        out_specs=[qs, ks, ks],
        out_shape=[jax.ShapeDtypeStruct((T, ATT), BF16), jax.ShapeDtypeStruct((T, KVW), BF16), jax.ShapeDtypeStruct((T, KVW), BF16)],
        compiler_params=_params())(P, P, P, q_g, k_g, rope_c, rope_s)


def _qk_prep_bwd(name, dqr, dkr, P, q_g, k_g, rope_c, rope_s, cfg):
    T = P.shape[0]
    ATT, KVW = cfg['ATT'], cfg['KVW']

    def body(dq_ref, dk_ref, q_ref, k_ref, qg_ref, kg_ref, c_ref, s_ref, dqo_ref, dko_ref, dqg_ref, dkg_ref):
        i = pl.program_id(0)
        cc, ss = c_ref[...], s_ref[...]

        @pl.when(i == 0)
        def _():
            dqg_ref[...] = jnp.zeros_like(dqg_ref)
            dkg_ref[...] = jnp.zeros_like(dkg_ref)

        def head(x, g, dout):
            dy = dout * cc + _rot(dout * ss)
            r = _rstd(x)
            xhat = x * r
            dxh = dy * g
            dx = r * (dxh - xhat * jnp.mean(dxh * xhat, axis=-1, keepdims=True))
            return dx, jnp.sum(dy * xhat, axis=0, keepdims=True)

        dg = jnp.zeros((1, HEAD_DIM), F32)
        for hh in range(ATT // HEAD_DIM):
            sl = slice(hh * HEAD_DIM, (hh + 1) * HEAD_DIM)
            dx, d = head(q_ref[:, sl], qg_ref[...], dq_ref[:, sl] * ATT_SCALE)
            dqo_ref[:, sl] = dx.astype(dqo_ref.dtype)
            dg += d
        dqg_ref[0:1, :] += dg
        dg = jnp.zeros((1, HEAD_DIM), F32)
        for hh in range(KVW // HEAD_DIM):
            sl = slice(hh * HEAD_DIM, (hh + 1) * HEAD_DIM)
            dx, d = head(k_ref[:, sl], kg_ref[...], dk_ref[:, sl] * (1.0 / LOG2_E))
            dko_ref[:, sl] = dx.astype(dko_ref.dtype)
            dg += d
        dkg_ref[0:1, :] += dg

    kb = ATT // KVW
    gain = pl.BlockSpec((1, HEAD_DIM), lambda i: (0, 0))
    dgain = pl.BlockSpec((8, HEAD_DIM), lambda i: (0, 0))
    tab = pl.BlockSpec((ROW_TILE, HEAD_DIM), lambda i: (i, 0))
    qs = pl.BlockSpec((ROW_TILE, ATT), lambda i: (i, 0))
    ks = pl.BlockSpec((ROW_TILE, KVW), lambda i: (i, 0))
    return pl.pallas_call(
        body, name=name, grid=(T // ROW_TILE,),
        in_specs=[qs, ks, qs, pl.BlockSpec((ROW_TILE, KVW), lambda i: (i, kb)), gain, gain, tab, tab],
        out_specs=[qs, ks, dgain, dgain],
        out_shape=[jax.ShapeDtypeStruct((T, ATT), BF16), jax.ShapeDtypeStruct((T, KVW), BF16),
                   jax.ShapeDtypeStruct((8, HEAD_DIM), F32), jax.ShapeDtypeStruct((8, HEAD_DIM), F32)],
        compiler_params=_params())(dqr, dkr, P, P, q_g, k_g, rope_c, rope_s)


def _att_specs(T, G):
    qs = pl.BlockSpec((ROW_TILE, G * HEAD_DIM), lambda h, i: (i, h))
    kvs = pl.BlockSpec((T, HEAD_DIM), lambda h, i: (0, h))
    return qs, kvs


def _attn_dense_fwd(name, q, k, v, cfg):
    T, G, Lc = q.shape[0], cfg['G'], cfg['Lc']

    def body(q_ref, k_ref, v_ref, o_ref, lse_ref):
        def attend(rows):
            kk, vv = k_ref[0:rows, :], v_ref[0:rows, :]
            for g in range(G):
                sl = slice(g * HEAD_DIM, (g + 1) * HEAD_DIM)
                s = _nt(q_ref[:, sl], kk)
                m = jnp.max(s, axis=1, keepdims=True)
                p = jnp.exp2(s - m)
                l = jnp.sum(p, axis=1, keepdims=True)
                o_ref[:, sl] = _nn(_bf(p), vv) / l
                lse_ref[:, sl] = jnp.broadcast_to(m + jnp.log2(l), (ROW_TILE, HEAD_DIM))

        @pl.when(pl.program_id(1) == 0)
        def _():
            attend(Lc)

        @pl.when(pl.program_id(1) > 0)
        def _():
            attend(T)

    qs, kvs = _att_specs(T, G)
    return pl.pallas_call(
        body, name=name, grid=(cfg['NKV'], T // ROW_TILE), in_specs=[qs, kvs, kvs], out_specs=[qs, qs],
        out_shape=[jax.ShapeDtypeStruct(q.shape, F32), jax.ShapeDtypeStruct(q.shape, F32)],
        compiler_params=_params())(q, k, v)


def _attn_dense_bwd(name, q, k, v, o, lse, dmix, cfg):
    T, G, Lc = q.shape[0], cfg['G'], cfg['Lc']

    def body(q_ref, k_ref, v_ref, o_ref, lse_ref, do_ref, dq_ref, dk_ref, dv_ref):
        i = pl.program_id(1)

        @pl.when(i == 0)
        def _():
            dk_ref[...] = jnp.zeros_like(dk_ref)
            dv_ref[...] = jnp.zeros_like(dv_ref)

        def attend(rows):
            kk, vv = k_ref[0:rows, :], v_ref[0:rows, :]
            for g in range(G):
                sl = slice(g * HEAD_DIM, (g + 1) * HEAD_DIM)
                qg, do = q_ref[:, sl], do_ref[:, sl]
                delta = jnp.sum(do * o_ref[:, sl], axis=1, keepdims=True)
                p = jnp.exp2(_nt(qg, kk) - lse_ref[:, g * HEAD_DIM:g * HEAD_DIM + 1])
                dob = _bf(do)
                dv_ref[0:rows, :] += _tn(_bf(p), dob)
                ds = _bf(p * (_nt(dob, vv) - delta))
                dq_ref[:, sl] = _nn(ds, kk)
                dk_ref[0:rows, :] += _tn(ds, qg)

        @pl.when(i == 0)
        def _():
            attend(Lc)

        @pl.when(i > 0)
        def _():
            attend(T)

    qs, kvs = _att_specs(T, G)
    return pl.pallas_call(
        body, name=name, grid=(cfg['NKV'], T // ROW_TILE), in_specs=[qs, kvs, kvs, qs, qs, qs], out_specs=[qs, kvs, kvs],
        out_shape=[jax.ShapeDtypeStruct(q.shape, F32), jax.ShapeDtypeStruct(k.shape, F32), jax.ShapeDtypeStruct(k.shape, F32)],
        compiler_params=_params())(q, k, v, o, lse, dmix)


def _band(i, T, Lc):
    start = pl.multiple_of(jnp.clip(WINDOW + (i - 1) * ROW_TILE, 0, T - BAND), WINDOW)
    qpos = (i - 1) * ROW_TILE + lax.broadcasted_iota(jnp.int32, (ROW_TILE, 1), 0)
    kpos = start - Lc + lax.broadcasted_iota(jnp.int32, (1, BAND), 1)
    ok = (jnp.abs(kpos - qpos) <= WINDOW) & (kpos >= 0) & (i > 0)
    return start, jnp.where(ok, 0.0, NEG_INF).astype(F32)


def _attn_win_fwd(name, q, k, v, sink, cfg):
    T, G, Lc = q.shape[0], cfg['G'], cfg['Lc']

    def body(sink_ref, q_ref, k_ref, v_ref, o_ref, lse_ref):
        h, i = pl.program_id(0), pl.program_id(1)
        start, bias = _band(i, T, Lc)
        kc, vc = k_ref[0:Lc, :], v_ref[0:Lc, :]
        kb, vb = k_ref[pl.ds(start, BAND), :], v_ref[pl.ds(start, BAND), :]
        for g in range(G):
            sl = slice(g * HEAD_DIM, (g + 1) * HEAD_DIM)
            qg = q_ref[:, sl]
            sk = sink_ref[h * G + g] * LOG2_E
            sc = _nt(qg, kc)
            sb = _nt(qg, kb) + bias
            m = jnp.maximum(jnp.maximum(jnp.max(sc, axis=1, keepdims=True), jnp.max(sb, axis=1, keepdims=True)), sk)
            pc, pb = jnp.exp2(sc - m), jnp.exp2(sb - m)
            l = jnp.sum(pc, axis=1, keepdims=True) + jnp.sum(pb, axis=1, keepdims=True) + jnp.exp2(sk - m)
            o_ref[:, sl] = (_nn(_bf(pc), vc) + _nn(_bf(pb), vb)) / l
            lse_ref[:, sl] = jnp.broadcast_to(m + jnp.log2(l), (ROW_TILE, HEAD_DIM))

    qs, kvs = _att_specs(T, G)
    return pl.pallas_call(
        body, name=name, grid=(cfg['NKV'], T // ROW_TILE),
        in_specs=[pl.BlockSpec(memory_space=pltpu.SMEM), qs, kvs, kvs], out_specs=[qs, qs],
        out_shape=[jax.ShapeDtypeStruct(q.shape, F32), jax.ShapeDtypeStruct(q.shape, F32)],
        compiler_params=_params())(sink, q, k, v)


def _attn_win_bwd(name, q, k, v, o, lse, dmix, sink, cfg):
    T, G, Lc = q.shape[0], cfg['G'], cfg['Lc']

    def body(sink_ref, q_ref, k_ref, v_ref, o_ref, lse_ref, do_ref, dq_ref, dk_ref, dv_ref, dsink_ref):
        h, i = pl.program_id(0), pl.program_id(1)
        start, bias = _band(i, T, Lc)
        kc, vc = k_ref[0:Lc, :], v_ref[0:Lc, :]
        kb, vb = k_ref[pl.ds(start, BAND), :], v_ref[pl.ds(start, BAND), :]

        @pl.when(i == 0)
        def _():
            dk_ref[...] = jnp.zeros_like(dk_ref)
            dv_ref[...] = jnp.zeros_like(dv_ref)
            dsink_ref[...] = jnp.zeros_like(dsink_ref)

        for g in range(G):
            sl = slice(g * HEAD_DIM, (g + 1) * HEAD_DIM)
            qg, do = q_ref[:, sl], do_ref[:, sl]
            lse = lse_ref[:, g * HEAD_DIM:g * HEAD_DIM + 1]
            delta = jnp.sum(do * o_ref[:, sl], axis=1, keepdims=True)
            pc = jnp.exp2(_nt(qg, kc) - lse)
            pb = jnp.exp2(_nt(qg, kb) + bias - lse)
            ps = jnp.exp2(sink_ref[h * G + g] * LOG2_E - lse)
            dob = _bf(do)
            dv_ref[0:Lc, :] += _tn(_bf(pc), dob)
            dv_ref[pl.ds(start, BAND), :] += _tn(_bf(pb), dob)
            dsc = _bf(pc * (_nt(dob, vc) - delta))
            dsb = _bf(pb * (_nt(dob, vb) - delta))
            dq_ref[:, sl] = _nn(dsc, kc) + _nn(dsb, kb)
            dk_ref[0:Lc, :] += _tn(dsc, qg)
            dk_ref[pl.ds(start, BAND), :] += _tn(dsb, qg)
            dsk = jnp.where(i > 0, -jnp.sum(ps * delta, axis=0, keepdims=True), 0.0)
            dsink_ref[:, sl] += jnp.broadcast_to(dsk, (8, HEAD_DIM))

    qs, kvs = _att_specs(T, G)
    return pl.pallas_call(
        body, name=name, grid=(cfg['NKV'], T // ROW_TILE),
        in_specs=[pl.BlockSpec(memory_space=pltpu.SMEM), qs, kvs, kvs, qs, qs, qs],
        out_specs=[qs, kvs, kvs, pl.BlockSpec((None, 8, G * HEAD_DIM), lambda h, i: (h, 0, 0))],
        out_shape=[jax.ShapeDtypeStruct(q.shape, F32), jax.ShapeDtypeStruct(k.shape, F32), jax.ShapeDtypeStruct(k.shape, F32),
                   jax.ShapeDtypeStruct((cfg['NKV'], 8, G * HEAD_DIM), F32)],
        compiler_params=_params())(sink, q, k, v, o, lse, dmix)


def _seq_pos(T, Lc):
    row = lax.broadcasted_iota(jnp.int32, (T, 1), 0)
    return jnp.where(row < Lc, row, row - Lc), jnp.where(row < Lc, Lc, T - Lc)


def _fw(x, k, pos, seglen):
    return jnp.where(pos + k < seglen, pltpu.roll(x, x.shape[0] - k, 0), 0.0)


def _bw(x, k, pos):
    return jnp.where(pos - k >= 0, pltpu.roll(x, k, 0), 0.0)


def _conv_fwd(name, P, conv_w8, cfg):
    T, Lc = P.shape[0], cfg['Lc']
    cb = (cfg['ATT'] + 2 * cfg['KVW']) // HEAD_DIM
    na = AUX_WIDTH // HEAD_DIM

    def body(gb_ref, gc_ref, u_ref, w_ref, o_ref):
        pos, seglen = _seq_pos(T, Lc)
        z = gc_ref[...] * u_ref[...]
        conv = w_ref[0:1, :] * _bw(z, 1, pos) + w_ref[1:2, :] * z + w_ref[2:3, :] * _fw(z, 1, pos, seglen)
        o_ref[...] = gb_ref[...] * conv

    col = lambda off: pl.BlockSpec((T, HEAD_DIM), lambda c: (0, cb + off + c))
    return pl.pallas_call(
        body, name=name, grid=(na,),
        in_specs=[col(0), col(na), col(2 * na), pl.BlockSpec((8, HEAD_DIM), lambda c: (0, c))],
        out_specs=pl.BlockSpec((T, HEAD_DIM), lambda c: (0, c)),
        out_shape=jax.ShapeDtypeStruct((T, AUX_WIDTH), F32), compiler_params=_params())(P, P, P, conv_w8)


def _conv_bwd(name, P, conv_w8, dmix, cfg):
    T, Lc = P.shape[0], cfg['Lc']
    cb = (cfg['ATT'] + 2 * cfg['KVW']) // HEAD_DIM
    ob = cfg['ATT'] // HEAD_DIM
    na = AUX_WIDTH // HEAD_DIM

    def body(gb_ref, gc_ref, u_ref, w_ref, do_ref, dgb_ref, dgc_ref, du_ref, dw_ref):
        pos, seglen = _seq_pos(T, Lc)
        gc, u, do = gc_ref[...], u_ref[...], do_ref[...]
        z = gc * u
        zm, zp = _bw(z, 1, pos), _fw(z, 1, pos, seglen)
        w0, w1, w2 = w_ref[0:1, :], w_ref[1:2, :], w_ref[2:3, :]
        dgb_ref[...] = (do * (w0 * zm + w1 * z + w2 * zp)).astype(dgb_ref.dtype)
        dc = do * gb_ref[...]
        dz = w0 * _fw(dc, 1, pos, seglen) + w1 * dc + w2 * _bw(dc, 1, pos)
        dgc_ref[...] = (dz * u).astype(dgc_ref.dtype)
        du_ref[...] = (dz * gc).astype(du_ref.dtype)
        dw_ref[...] = jnp.zeros_like(dw_ref)
        dw_ref[0:1, :] = jnp.sum(dc * zm, axis=0, keepdims=True)
        dw_ref[1:2, :] = jnp.sum(dc * z, axis=0, keepdims=True)
        dw_ref[2:3, :] = jnp.sum(dc * zp, axis=0, keepdims=True)

    col = lambda off: pl.BlockSpec((T, HEAD_DIM), lambda c: (0, cb + off + c))
    wspec = pl.BlockSpec((8, HEAD_DIM), lambda c: (0, c))
    ocol = lambda off: pl.BlockSpec((T, HEAD_DIM), lambda c: (0, off + c))
    return pl.pallas_call(
        body, name=name, grid=(na,),
        in_specs=[col(0), col(na), col(2 * na), wspec, ocol(ob)],
        out_specs=[ocol(0), ocol(0), ocol(0), wspec],
        out_shape=[jax.ShapeDtypeStruct((T, AUX_WIDTH), BF16)] * 3 + [jax.ShapeDtypeStruct((8, AUX_WIDTH), F32)],
        compiler_params=_params())(P, P, P, conv_w8, dmix)


def _window_sums(x, half, pos, seglen):
    fwd, bwd = x, x
    s = 1
    while s < half:
        fwd = fwd + _fw(fwd, s, pos, seglen)
        bwd = bwd + _bw(bwd, s, pos)
        s *= 2
    return fwd, bwd


def _pooled(u, half, pos, seglen):
    fwd, bwd = _window_sums(u, half, pos, seglen)
    cnt = (jnp.minimum(pos + half, seglen) - jnp.maximum(pos - half, 0)).astype(F32)
    return (fwd + _bw(bwd, 1, pos)) / cnt - u, cnt


def _pool_fwd(name, P, pool_w, pool_scale, cfg):
    T, Lc = P.shape[0], cfg['Lc']
    cb = (cfg['ATT'] + 2 * cfg['KVW']) // HEAD_DIM

    def body(u_ref, w_ref, s_ref, o_ref):
        g = pl.program_id(0)
        pos, seglen = _seq_pos(T, Lc)
        for k, half in enumerate(POOL_HALF):
            @pl.when(g == k)
            def _(half=half):
                pooled, _ = _pooled(u_ref[...], half, pos, seglen)
                o_ref[...] = _nn(_bf(pooled), _bf(w_ref[...])) * s_ref[...]

    return pl.pallas_call(
        body, name=name, grid=(AUX_GROUPS,),
        in_specs=[pl.BlockSpec((T, HEAD_DIM), lambda g: (0, cb + g)), pl.BlockSpec((None, HEAD_DIM, HEAD_DIM), lambda g: (g, 0, 0)),
                  pl.BlockSpec((1, HEAD_DIM), lambda g: (0, g))],
        out_specs=pl.BlockSpec((T, HEAD_DIM), lambda g: (0, g)),
        out_shape=jax.ShapeDtypeStruct((T, AUX_WIDTH), F32), compiler_params=_params())(P, pool_w, pool_scale)


def _pool_bwd(name, P, pool_w, pool_scale, dmix, cfg):
    T, Lc = P.shape[0], cfg['Lc']
    cb = (cfg['ATT'] + 2 * cfg['KVW']) // HEAD_DIM
    ob = cfg['ATT'] // HEAD_DIM

    def body(u_ref, w_ref, s_ref, do_ref, du_ref, dw_ref, ds_ref):
        g = pl.program_id(0)
        pos, seglen = _seq_pos(T, Lc)
        for k, half in enumerate(POOL_HALF):
            @pl.when(g == k)
            def _(half=half):
                do = do_ref[...]
                pooled, cnt = _pooled(u_ref[...], half, pos, seglen)
                wb = _bf(w_ref[...])
                mixed = _nn(_bf(pooled), wb)
                ds_ref[...] = jnp.broadcast_to(jnp.sum(do * mixed, axis=0, keepdims=True), ds_ref.shape)
                dmixed = _bf(do * s_ref[...])
                dw_ref[...] = _tn(_bf(pooled), dmixed)
                dpooled = _nt(dmixed, wb)
                e = dpooled / cnt
                fwd, bwd = _window_sums(e, half, pos, seglen)
                adj = fwd + _fw(e, half, pos, seglen) + _bw(bwd, 1, pos) - _bw(e, half, pos)
                du_ref[...] = (adj - dpooled).astype(du_ref.dtype)

    wspec = pl.BlockSpec((None, HEAD_DIM, HEAD_DIM), lambda g: (g, 0, 0))
    return pl.pallas_call(
        body, name=name, grid=(AUX_GROUPS,),
        in_specs=[pl.BlockSpec((T, HEAD_DIM), lambda g: (0, cb + g)), wspec, pl.BlockSpec((1, HEAD_DIM), lambda g: (0, g)),
                  pl.BlockSpec((T, HEAD_DIM), lambda g: (0, ob + g))],
        out_specs=[pl.BlockSpec((T, HEAD_DIM), lambda g: (0, g)), wspec, pl.BlockSpec((8, HEAD_DIM), lambda g: (0, g))],
        out_shape=[jax.ShapeDtypeStruct((T, AUX_WIDTH), BF16), jax.ShapeDtypeStruct(pool_w.shape, F32),
                   jax.ShapeDtypeStruct((8, AUX_WIDTH), F32)],
        compiler_params=_params())(P, pool_w, pool_scale, dmix)


def _ffn_up(name, hn, wg3, wu3, dep=None):
    T, D = hn.shape
    J, k, _ = wg3.shape
    tm = _row_tile(T, 1088)
    dep_ins, dep_specs = _dep(dep, 2)

    def body(x_ref, wg_ref, wu_ref, *rest):
        s_ref, ud_ref, a_ref = rest[len(dep_ins):]
        x = x_ref[pl.ds(pl.multiple_of(pl.program_id(1) * tm, 16), tm), :]
        g, u = _nt(x, wg_ref[...]), _nt(x, wu_ref[...])
        sig = jax.nn.sigmoid(g)
        silu = g * sig
        s_ref[...] = silu.astype(s_ref.dtype)
        ud_ref[...] = (u * (sig * (1 + g * (1 - sig)))).astype(ud_ref.dtype)
        a_ref[...] = (silu * u).astype(a_ref.dtype)

    wspec = pl.BlockSpec((None, k, D), lambda j, i: (j, 0, 0))
    ospec = pl.BlockSpec((None, tm, k), lambda j, i: (j, i, 0))
    return pl.pallas_call(
        body, name=name, grid=(J, T // tm),
        in_specs=[pl.BlockSpec((T, D), lambda j, i: (0, 0), pipeline_mode=RESIDENT), wspec, wspec] + dep_specs,
        out_specs=[ospec, ospec, ospec],
        out_shape=[jax.ShapeDtypeStruct((J, T, k), BF16)] * 3,
        compiler_params=_params())(hn, wg3, wu3, *dep_ins)


def _ffn_dact(name, dF, wd3, silu_g, u_dsilu):
    T, D = dF.shape
    J, k, _ = wd3.shape
    tm = _row_tile(T, 1088)

    def body(df_ref, wd_ref, s_ref, ud_ref, dg_ref, du_ref):
        rows = pl.ds(pl.multiple_of(pl.program_id(1) * tm, 16), tm)
        da = _nt(df_ref[rows, :], wd_ref[...])
        du_ref[...] = (da * s_ref[...].astype(F32)).astype(du_ref.dtype)
        dg_ref[...] = (da * ud_ref[...].astype(F32)).astype(dg_ref.dtype)

    aspec = pl.BlockSpec((None, tm, k), lambda j, i: (j, i, 0))
    return pl.pallas_call(
        body, name=name, grid=(J, T // tm),
        in_specs=[pl.BlockSpec((T, D), lambda j, i: (0, 0), pipeline_mode=RESIDENT), pl.BlockSpec((None, k, D), lambda j, i: (j, 0, 0)), aspec, aspec],
        out_specs=[aspec, aspec],
        out_shape=[jax.ShapeDtypeStruct((J, T, k), BF16), jax.ShapeDtypeStruct((J, T, k), BF16)],
        compiler_params=_params())(dF, wd3, silu_g, u_dsilu)


def _loss_head(name, h, g, target, cfg):
    T, D = h.shape

    def body(h_ref, g_ref, t_ref, dh_ref, loss_ref, dg_ref):
        i = pl.program_id(0)

        @pl.when(i == 0)
        def _():
            dh_ref[...] = jnp.zeros_like(dh_ref)
            loss_ref[...] = jnp.zeros_like(loss_ref)
            dg_ref[...] = jnp.zeros_like(dg_ref)

        @pl.when(i > 0)
        def _():
            x = h_ref[...]
            r = _rstd(x)
            xhat = x * r
            gg = g_ref[...]
            err = xhat * gg - t_ref[...]
            loss_ref[...] += 0.5 * jnp.sum(jnp.sum(err * err, axis=1, keepdims=True) / D, axis=0, keepdims=True)
            dy = err / D
            dg_ref[0:1, :] += jnp.sum(dy * xhat, axis=0, keepdims=True)
            dxh = dy * gg
            dh_ref[...] = r * (dxh - xhat * jnp.mean(dxh * xhat, axis=-1, keepdims=True))

    row = pl.BlockSpec((ROW_TILE, D), lambda i: (i, 0))
    return pl.pallas_call(
        body, name=name, grid=(T // ROW_TILE,),
        in_specs=[row, pl.BlockSpec((1, D), lambda i: (0, 0)), pl.BlockSpec((ROW_TILE, D), lambda i: (jnp.maximum(i - 1, 0), 0))],
        out_specs=[row, pl.BlockSpec((8, 128), lambda i: (0, 0)), pl.BlockSpec((8, D), lambda i: (0, 0))],
        out_shape=[jax.ShapeDtypeStruct((T, D), F32), jax.ShapeDtypeStruct((8, 128), F32), jax.ShapeDtypeStruct((8, D), F32)],
        compiler_params=_params())(h, g, target)


def _adamw(name, parts, w, m, v, dep=None):
    R, C = w.shape
    n_parts = parts.shape[0]
    tr = _row_tile(R, max(16, (1 << 18) // C)) if R % 16 == 0 else R
    bc1 = 1.0 - ADAM_B1 ** ADAM_STEP
    bc2 = 1.0 - ADAM_B2 ** ADAM_STEP
    dep_ins, dep_specs = _dep(dep, 1)

    def body(p_ref, w_ref, m_ref, v_ref, *rest):
        g_ref, d_ref, nm_ref, nv_ref = rest[len(dep_ins):]
        g = p_ref[0].astype(F32)
        for k in range(1, n_parts):
            g = g + p_ref[k].astype(F32)
        nm = ADAM_B1 * m_ref[...] + (1.0 - ADAM_B1) * g
        nv = ADAM_B2 * v_ref[...] + (1.0 - ADAM_B2) * (g * g)
        g_ref[...] = g
        nm_ref[...] = nm
        nv_ref[...] = nv
        d_ref[...] = -ADAM_LR * ((nm / bc1) / (jnp.sqrt(nv / bc2) + ADAM_EPS) + ADAM_WD * w_ref[...])

    blk = pl.BlockSpec((tr, C), lambda i: (i, 0))
    return pl.pallas_call(
        body, name=name, grid=(R // tr,), in_specs=[pl.BlockSpec((n_parts, tr, C), lambda i: (0, i, 0)), blk, blk, blk] + dep_specs,
        out_specs=[blk] * 4, out_shape=[jax.ShapeDtypeStruct((R, C), F32)] * 4, compiler_params=_params())(parts, w, m, v, *dep_ins)


class _WeightStream:
    def __init__(self, cast):
        self.cast, self.handles = cast, {}

    @staticmethod
    def _tag(l, group):
        return ("ffn" if group is FFN_WEIGHTS else group[0]) + str(l)

    def start(self, l, group, after=None):
        self.handles[l, group], token = _gather_start(f"gather_{self._tag(l, group)}_start", [self.cast(l, n) for n in group], after)
        return token

    def relay(self, l, group, after):
        self.handles[l, group], token = _gather_relay(f"gather_{self._tag(l, group)}_relay", self.handles[l, group], after)
        return token

    def get(self, l, group, after):
        got = dict(zip(group, _gather_wait(f"gather_{self._tag(l, group)}_wait", self.handles[l, group], after)))
        if 'w_out' in got:
            rows, cols = got['w_out'].shape[1:]
            got['w_out'] = got['w_out'].reshape(N_DEV * rows, cols)
        return got


def _layer_fwd(l, h, p, stream, mod, rope, conv_w8, cfg):
    nm = f"l{l}_"
    mod = mod + stream.relay(l, IN_WEIGHT, h)[0, 0]
    xn = _norm_mod(nm + "norm1", h, p['norm1_g'], mod, 0)
    W = stream.get(l, IN_WEIGHT, xn)
    token = None
    if l == 0:
        token = stream.start(0, OUT_WEIGHT, after=W['w_in']) + stream.start(0, FFN_WEIGHTS, after=W['w_in'])
    P = _mm_cols(nm + "w_in", xn, W['w_in'], dep=token)
    qr, kr, vb = _qk_prep(nm + "qk_prep", P, p['q_norm_g'], p['k_norm_g'], rope[0], rope[1], cfg)
    if l == 0:
        o, lse = _attn_dense_fwd(nm + "attn", qr, kr, vb, cfg)
        aux = _conv_fwd(nm + "conv", P, conv_w8, cfg)
    else:
        o, lse = _attn_win_fwd(nm + "attn", qr, kr, vb, p['sink'], cfg)
        aux = _pool_fwd(nm + "pool", P, p['pool_w'], p['pool_scale'], cfg)
    mix = jnp.concatenate([o, aux], axis=1).astype(BF16)
    W.update(stream.get(l, OUT_WEIGHT, stream.relay(l, OUT_WEIGHT, mix)))
    y, h2 = _mm_plain(nm + "w_out", mix, W['w_out'], False, dep=stream.relay(l, FFN_WEIGHTS, o), res=(h, mod, 2, cfg['Lc']))
    hn = _norm_mod(nm + "norm2", h2, p['norm2_g'], mod, 1)
    W.update(stream.get(l, FFN_WEIGHTS, hn))
    token = stream.start(1, IN_WEIGHT, after=W['w_down']) if l == 0 else None
    silu_g, u_dsilu, A = _ffn_up(nm + "ffn_up", hn, W['w_gate'], W['w_up'], dep=token)
    if l == 0:
        token = stream.start(1, OUT_WEIGHT, after=A) + stream.start(1, FFN_WEIGHTS, after=A)
    F, h3 = _mm_shards_nn(nm + "w_down", A, W['w_down'], dep=token, res=(h2, mod, 5, cfg['Lc']))
    saved = dict(h=h, xn=xn, P=P, qr=qr, kr=kr, vb=vb, o=o, lse=lse, mix=mix, y=y, h2=h2, hn=hn, silu_g=silu_g, u_dsilu=u_dsilu, A=A, F=F)
    return h3, saved, W


def _layer_bwd(l, dh3, s, p, W, mod, rope, conv_w8, cfg):
    nm = f"l{l}_bwd_"
    J = N_DEV
    dF, dmod = _gate_bwd(nm + "res2", dh3, s['F'], mod, 5)
    dG, dU = _ffn_dact(nm + "ffn_act", dF, W['w_down'], s['silu_g'], s['u_dsilu'])
    big = {'w_down': _wgrad_down(nm + "dw_down", s['A'], dF),
           'w_gate': _wgrad_down(nm + "dw_gate", dG, s['hn']),
           'w_up': _wgrad_down(nm + "dw_up", dU, s['hn'])}
    handles = {}
    handles['ffn'], token = _exchange_start(f"scatter_ffn{l}_start", [big[n] for n in FFN_WEIGHTS], True)
    mod = mod + token[0, 0]
    dhn = _mm_shards_nn2(nm + "dhn", dG, W['w_gate'], dU, W['w_up'])
    dh2, dm, dg2 = _norm_mod_bwd(nm + "norm2", dhn, s['h2'], p['norm2_g'], mod, 1, dh3)
    dmod += dm
    dY, dm = _gate_bwd(nm + "res1", dh2, s['y'], mod, 2)
    dmod += dm
    dwo = _wgrad_rows(nm + "dw_out", s['mix'], dY)
    handles['w_out'], token = _exchange_start(f"scatter_w_out{l}_start", [dwo.reshape((J, dwo.shape[0] // J, dwo.shape[1]))], True)
    dmix = _mm_plain(nm + "dmix", dY, W['w_out'], True, dep=token)
    small = {'norm2_g': dg2[0]}
    if l == 0:
        dqr, dkr, dv = _attn_dense_bwd(nm + "attn", s['qr'], s['kr'], s['vb'], s['o'], s['lse'], dmix, cfg)
        *daux, dcw = _conv_bwd(nm + "conv", s['P'], conv_w8, dmix, cfg)
        small['conv_w'] = dcw[0:3]
    else:
        dqr, dkr, dv, dsk = _attn_win_bwd(nm + "attn", s['qr'], s['kr'], s['vb'], s['o'], s['lse'], dmix, p['sink'], cfg)
        du, dpw, dps = _pool_bwd(nm + "pool", s['P'], p['pool_w'], p['pool_scale'], dmix, cfg)
        daux = [du]
        small.update(sink=dsk[:, 0, ::HEAD_DIM].reshape(-1), pool_w=dpw, pool_scale=dps[0])
    dq, dk, dqg, dkg = _qk_prep_bwd(nm + "qk_prep", dqr, dkr, s['P'], p['q_norm_g'], p['k_norm_g'], rope[0], rope[1], cfg)
    small.update(q_norm_g=dqg[0], k_norm_g=dkg[0])
    dP = jnp.concatenate([dq, dk, dv.astype(BF16), *daux], axis=1)
    handles['w_in'], token = _exchange_start(f"scatter_w_in{l}_start", [_wgrad_cols(nm + "dw_in", s['xn'], dP, J)], True)
    mod = mod + token[0, 0]
    dxn = _mm_cols_nt(nm + "dxn", dP, W['w_in'])
    dh, dm, dg1 = _norm_mod_bwd(nm + "norm1", dxn, s['h'], p['norm1_g'], mod, 0, dh2, latent_only=(l == 0))
    dmod += dm
    small['norm1_g'] = dg1[0]
    return dh, dmod, small, handles, token


def _rope_tables(S, Lc):
    half = HEAD_DIM // 4
    pos = np.arange(S)
    inv = ROPE_THETA ** (-np.arange(0, 2 * half, 2, dtype=np.float32) / (2 * half))
    inv = jnp.asarray(inv, F32)
    ang_r = jnp.asarray(pos // GRID_W, F32)[:, None] * inv
    ang_c = jnp.asarray(pos % GRID_W, F32)[:, None] * inv
    cos = jnp.concatenate([jnp.cos(ang_r)] * 2 + [jnp.cos(ang_c)] * 2, axis=1)
    sin = jnp.concatenate([-jnp.sin(ang_r), jnp.sin(ang_r), -jnp.sin(ang_c), jnp.sin(ang_c)], axis=1)
    return (jnp.concatenate([jnp.ones((Lc, HEAD_DIM), F32), cos], axis=0),
            jnp.concatenate([jnp.zeros((Lc, HEAD_DIM), F32), sin], axis=0))


def _pad_rows(a, rows):
    return jnp.concatenate([a, jnp.zeros((rows - a.shape[0],) + a.shape[1:], a.dtype)], axis=0)


def _flat128(a, nlead):
    lead = a.shape[:nlead]
    f = a.reshape(lead + (-1,))
    pad = (-f.shape[-1]) % 128
    if pad:
        f = jnp.concatenate([f, jnp.zeros(lead + (pad,), f.dtype)], axis=-1)
    return f.reshape(lead + (-1, 128))


def _pack(named, nlead=0):
    rows, layout, at = [], {}, 0
    for name, a in named:
        f = _flat128(a, nlead)
        n = f.shape[-2]
        pad = (-n) % 8
        if pad:
            f = jnp.concatenate([f, jnp.zeros(f.shape[:-2] + (pad, 128), f.dtype)], axis=-2)
        layout[name] = (at, n, a.shape[nlead:])
        rows.append(f)
        at += n + pad
    return jnp.concatenate(rows, axis=-2), layout


def _unpack(arr, layout, name):
    at, n, shape = layout[name]
    return arr[..., at:at + n, :].reshape(arr.shape[:-2] + (-1,))[..., :math.prod(shape)].reshape(arr.shape[:-2] + tuple(shape))


def kernel(*args):
    A = dict(zip(INPUT_NAMES, args, strict=True))
    x, ctx = A['x'][0], A['ctx'][0]
    S, D = x.shape
    Lc = ctx.shape[0]
    T = Lc + S
    ATT = D - AUX_WIDTH
    KVW = (A['l1_w_in'].shape[1] * N_DEV - ATT - AUX_WIDTH) // 2
    cfg = dict(ATT=ATT, KVW=KVW, NKV=KVW // HEAD_DIM, G=ATT // KVW, Lc=Lc)
    assert Lc == ROW_TILE and S % ROW_TILE == 0 and T >= BAND and S % GRID_W == 0
    cw = A['l0_conv_w'].shape[1]
    me = 4 * lax.axis_index("x") + 2 * lax.axis_index("y") + lax.axis_index("c")

    def layer_params(l):
        pre = f"l{l}_"
        return {k[len(pre):]: (v.reshape(1, -1) if v.ndim == 1 and k != 'l1_sink' else v) for k, v in A.items() if k.startswith(pre)}

    params = [layer_params(0), layer_params(1)]

    def cast(l, n):
        w = A[f'l{l}_{n}']
        return (w.T if n in TRANSPOSED else w).astype(BF16)

    stream = _WeightStream(cast)
    token = stream.start(0, IN_WEIGHT)

    big_names = [n for n in WEIGHT_NAMES if n[3:] in BIG_WEIGHTS + ('w_mod',)]
    rest = [n for n in WEIGHT_NAMES if n not in big_names]
    early = ['x', 'ctx'] + rest + ['m_' + n for n in rest] + ['v_' + n for n in rest]
    token, held = lax.optimization_barrier((token, [A[n] for n in early]))
    A.update(zip(early, held))
    x, ctx = A['x'][0], A['ctx'][0]
    wp, layw = _pack([(n, A[n]) for n in rest])
    mp, _ = _pack([(n, A['m_' + n]) for n in rest])
    vp, _ = _pack([(n, A['v_' + n]) for n in rest])
    rope = _rope_tables(S, Lc)
    h = jnp.concatenate([ctx, x], axis=0)

    sc_own = jax.nn.silu(A['c']) + token[0, 0]
    first, lay0 = _pack([('sc', sc_own), ('conv_w', A['l0_conv_w'])])
    first, h, wp, mp, vp = lax.optimization_barrier((first, h, wp, mp, vp))
    first_all = _exchange("gather_cond", [first], False)[0]
    sc_all = _unpack(first_all, lay0, 'sc')[:, 0]
    conv_w = _unpack(first_all, lay0, 'conv_w').transpose(1, 0, 2).reshape(3, N_DEV * cw)
    conv_w8 = _pad_rows(conv_w, 8)
    sc_ctx = jax.nn.silu(A['c_ctx'])
    s16 = _pad_rows(jnp.concatenate([sc_all, sc_ctx[None]], axis=0), 16)

    nmod = A['l0_w_mod'].shape[1]
    modp = jnp.concatenate([_mm_plain(f"l{l}_mod", s16, A[f'l{l}_w_mod'], False) for l in range(2)], axis=1)
    modp_all = _exchange("gather_mod", [modp], False)[0]
    mods = []
    for l in range(2):
        full = modp_all[:, :, l * nmod:(l + 1) * nmod].transpose(1, 0, 2).reshape(16, N_MOD * D) + A[f'l{l}_b_mod'][None]
        both = jnp.stack([full[8], lax.dynamic_index_in_dim(full, me, 0, keepdims=False)]).reshape(2, N_MOD, D)
        mods.append(jnp.concatenate([both, jnp.zeros((2, 8 - N_MOD, D), F32)], axis=1))

    saved, W = [], []
    for l in range(2):
        h, s, Wl = _layer_fwd(l, h, params[l], stream, mods[l], rope, conv_w8, cfg)
        saved.append(s)
        W.append(Wl)

    dh, loss_blk, dgf = _loss_head("loss_head", h, A['final_norm_g'].reshape(1, -1), A['loss_target'][0], cfg)
    loss = lax.psum(loss_blk[0, 0], ("x", "y", "c"))

    grads, small, dmods, scatters = {}, {'final_norm_g': dgf[0]}, [None, None], [None, None]
    token = jnp.zeros((8, 128), F32)
    for l in (1, 0):
        dh, dmods[l], sm, scatters[l], token = _layer_bwd(l, dh, saved[l], params[l], W[l], mods[l] + token[0, 0], rope, conv_w8, cfg)
        small.update({f'l{l}_{k}': v for k, v in sm.items()})
    grad_x = dh[None]

    def landed(l, key, after):
        group = FFN_WEIGHTS if key == 'ffn' else (key,)
        for n, parts in zip(group, _exchange_wait(f"scatter_{key}{l}_wait", scatters[l][key], after)):
            shape = A[f'l{l}_{n}'].shape
            grads[f'l{l}_{n}'] = (parts.reshape((N_DEV,) + (shape[::-1] if n in TRANSPOSED else shape)), None)

    out = {}

    def adam(n, dep=None):
        w, m, v = A[n], A['m_' + n], A['v_' + n]
        if n[3:] in TRANSPOSED:
            out[n] = tuple(r.T for r in _adamw("adamw_" + n, grads[n][0], w.T, m.T, v.T, dep))
        else:
            out[n] = _adamw("adamw_" + n, grads[n][0], w, m, v, dep)
        return out[n][1]

    small_names = [n for n in WEIGHT_NAMES if n in small]
    pieces = [(n, small[n]) for n in small_names]
    for l in range(2):
        pieces += [(f'dmod{l}', dmods[l][1, :N_MOD]), (f'dcmod{l}', dmods[l][0, :N_MOD])]
    second, lay1 = _pack(pieces)
    small_handle, small_token = _exchange_start("gather_small_start", [second], False, after=token)

    last = dh
    for l in (1, 0):
        for key in ('ffn', 'w_out') + (('w_in',) if l == 1 else ()):
            landed(l, key, last)
            for n in (FFN_WEIGHTS if key == 'ffn' else (key,)):
                last = adam(f'l{l}_{n}', small_token)
    second_all = _exchange_wait("gather_small_wait", small_handle, last)[0]

    dsc_part = jnp.zeros((16, D), F32)
    for l in range(2):
        dm16 = _pad_rows(jnp.concatenate([_unpack(second_all, lay1, f'dmod{l}').reshape(N_DEV, N_MOD * D),
                                          jnp.sum(_unpack(second_all, lay1, f'dcmod{l}'), axis=0).reshape(1, N_MOD * D)], axis=0), 16)
        mine = lax.dynamic_slice_in_dim(dm16, me * nmod, nmod, axis=1)
        tk = _col_tile(D, 512)
        gw = _mm_tn(f"l{l}_dw_mod", s16, mine, pl.BlockSpec((16, tk), lambda kb, t: (0, kb)), pl.BlockSpec((16, nmod), lambda kb, t: (0, 0)),
                    jax.ShapeDtypeStruct((D, nmod), F32), pl.BlockSpec((tk, nmod), lambda kb, t: (kb, 0)), (tk, nmod), (D // tk, 1))
        grads[f'l{l}_w_mod'] = (gw[None], None)
        dsc_part += _mm_plain(f"l{l}_dsc", mine, A[f'l{l}_w_mod'], True)
        dmod_dev = _unpack(second_all, lay1, f'dmod{l}') + _unpack(second_all, lay1, f'dcmod{l}')
        grads[f'l{l}_b_mod'] = (dmod_dev.reshape(N_DEV, N_MOD * D), None)
    dsig = jax.nn.sigmoid(A['c_ctx'])
    dsilu = dsig * (1 + A['c_ctx'] * (1 - dsig))
    third_all = _exchange("gather_dsc", [dsc_part[8:9]], False)[0]
    grads['c_ctx'] = (third_all[:, 0] * dsilu[None], None)
    for n in small_names:
        g8 = _unpack(second_all, lay1, n)
        if n == 'l0_conv_w':
            g8 = lax.dynamic_slice_in_dim(g8, me * cw, cw, axis=2)
        grads[n] = (g8, None)

    adam('l0_w_mod')
    last = adam('l1_w_mod')
    gp, _ = _pack([(n, grads[n][0]) for n in rest], nlead=1)
    res = _adamw("adamw_small", gp, wp, mp, vp)
    for n in rest:
        out[n] = tuple(_unpack(r, layw, n) for r in res)
    landed(0, 'w_in', last)
    adam('l0_w_in')

    outs = [loss, grad_x]
    for k in range(4):
        outs += [out[n][k] for n in WEIGHT_NAMES]
    return tuple(outs)
```

```python
import functools
import math

import numpy as np
import jax
import jax.numpy as jnp
from jax import lax
from jax.experimental import pallas as pl
from jax.experimental.pallas import tpu as pltpu

F32 = jnp.float32
BF16 = jnp.bfloat16
HEAD_DIM = 128
AUX_WIDTH = 512
AUX_GROUPS = 4
POOL_HALF = (1, 2, 4, 8)
WINDOW = 128
GRID_W = 64
ROPE_THETA = 10000.0
EPS = 1e-6
NEG_INF = -1e30
ATT_SCALE = HEAD_DIM ** -0.5
LOG2_E = math.log2(math.e)
Q_SCALE = ATT_SCALE * LOG2_E
N_MOD = 6
N_DEV = 8
ROW_TILE = 256
BAND = ROW_TILE + 2 * WINDOW
ADAM_LR, ADAM_B1, ADAM_B2, ADAM_EPS, ADAM_WD, ADAM_STEP = 0.001, 0.9, 0.999, 1e-08, 0.01, 10
VMEM_LIMIT_MB = 56
MESH = pl.DeviceIdType.MESH
RESIDENT = pl.Buffered(buffer_count=1)

WEIGHT_NAMES = ['c_ctx', 'l0_norm1_g', 'l0_w_mod', 'l0_b_mod', 'l0_w_in', 'l0_q_norm_g', 'l0_k_norm_g', 'l0_conv_w', 'l0_w_out', 'l0_norm2_g', 'l0_w_gate', 'l0_w_up', 'l0_w_down', 'l1_norm1_g', 'l1_w_mod', 'l1_b_mod', 'l1_w_in', 'l1_q_norm_g', 'l1_k_norm_g', 'l1_sink', 'l1_pool_w', 'l1_pool_scale', 'l1_w_out', 'l1_norm2_g', 'l1_w_gate', 'l1_w_up', 'l1_w_down', 'final_norm_g']
INPUT_NAMES = (['x', 'c', 'ctx'] + WEIGHT_NAMES + ['loss_target'] + ['m_' + n for n in WEIGHT_NAMES]
               + ['v_' + n for n in WEIGHT_NAMES])
IN_WEIGHT = ('w_in',)
OUT_WEIGHT = ('w_out',)
MIXER_WEIGHTS = OUT_WEIGHT + IN_WEIGHT
FFN_WEIGHTS = ('w_down', 'w_gate', 'w_up')
TRANSPOSED = ('w_gate', 'w_up')
BIG_WEIGHTS = MIXER_WEIGHTS + FFN_WEIGHTS


def _params(vmem_mb=VMEM_LIMIT_MB):
    return pltpu.CompilerParams(vmem_limit_bytes=vmem_mb << 20)


def _row_tile(n, cap):
    best = None
    for t in range(16, min(n, cap) + 1, 16):
        if n % t == 0:
            best = t
    assert best is not None, (n, cap)
    return best


def _col_tile(n, cap):
    best = n
    for t in range(128, min(n, cap) + 1, 128):
        if n % t == 0:
            best = t
    return best if best <= cap or n % 128 else n


def _dot(a, b, ca, cb):
    return lax.dot_general(a, b, (((ca,), (cb,)), ((), ())), preferred_element_type=F32)


def _nn(a, b):
    return _dot(a, b, 1, 0)


def _nt(a, b):
    return _dot(a, b, 1, 1)


def _tn(a, b):
    return _dot(a, b, 0, 0)


def _bf(x):
    return x.astype(BF16)


def _exchange(name, arrs, scatter, after=None):
    n = len(arrs)
    extra = [] if after is None else [after]
    if scatter:
        out_shape = [jax.ShapeDtypeStruct(a.shape, a.dtype) for a in arrs]
    else:
        out_shape = [jax.ShapeDtypeStruct((N_DEV,) + a.shape, a.dtype) for a in arrs]

    def body(*refs):
        ins, outs = refs[:n], refs[n + len(extra):2 * n + len(extra)]
        send_sems, recv_sems, local_sems = refs[2 * n + len(extra):]
        x, y, c = lax.axis_index("x"), lax.axis_index("y"), lax.axis_index("c")
        me = 4 * x + 2 * y + c
        local, remote = [], []
        for a in range(n):
            own = ins[a].at[me] if scatter else ins[a]
            cp = pltpu.make_async_copy(own, outs[a].at[me], local_sems.at[a])
            cp.start()
            local.append(cp)
            for r in range(1, N_DEV):
                px = 1 - x if r & 4 else x
                py = 1 - y if r & 2 else y
                pc = 1 - c if r & 1 else c
                src = ins[a].at[4 * px + 2 * py + pc] if scatter else ins[a]
                cp = pltpu.make_async_remote_copy(
                    src_ref=src, dst_ref=outs[a].at[me], send_sem=send_sems.at[a, r - 1],
                    recv_sem=recv_sems.at[a, r - 1], device_id=(px, py, pc), device_id_type=MESH)
                cp.start()
                remote.append(cp)
        for cp in remote:
            cp.wait()
        for cp in local:
            cp.wait()

    any_spec = pl.BlockSpec(memory_space=pl.ANY)
    return pl.pallas_call(
        body, name=name, out_shape=out_shape,
        in_specs=[any_spec] * (n + len(extra)), out_specs=[any_spec] * n,
        scratch_shapes=[pltpu.SemaphoreType.DMA((n, N_DEV - 1)), pltpu.SemaphoreType.DMA((n, N_DEV - 1)),
                        pltpu.SemaphoreType.DMA((n,))],
    )(*arrs, *extra)


HBM_SPEC = pl.BlockSpec(memory_space=pltpu.HBM)
SEM_SPEC = pl.BlockSpec(memory_space=pltpu.SEMAPHORE)
EFFECT = pltpu.SideEffectType.DATAFLOW_SIDE_EFFECTING


def _split_copies(srcs, lands, send_sems, recv_sems, local_sems, scatter):
    x, y, c = lax.axis_index("x"), lax.axis_index("y"), lax.axis_index("c")
    me = 4 * x + 2 * y + c
    local, remote = [], []
    for a in range(len(srcs)):
        own = srcs[a].at[me] if scatter else srcs[a]
        local.append(pltpu.make_async_copy(own, lands[a].at[me], local_sems.at[a]))
        for r in range(1, N_DEV):
            px = 1 - x if r & 4 else x
            py = 1 - y if r & 2 else y
            pc = 1 - c if r & 1 else c
            src = srcs[a].at[4 * px + 2 * py + pc] if scatter else srcs[a]
            remote.append(pltpu.make_async_remote_copy(
                src_ref=src, dst_ref=lands[a].at[me], send_sem=send_sems.at[a * (N_DEV - 1) + r - 1],
                recv_sem=recv_sems.at[a * (N_DEV - 1) + r - 1], device_id=(px, py, pc), device_id_type=MESH))
    return local, remote


def _exchange_start(name, arrs, scatter, after=None):
    n = len(arrs)
    extra = [] if after is None else [after]
    shapes = [a.shape if scatter else (N_DEV,) + a.shape for a in arrs]
    lands = [pltpu.with_memory_space_constraint(lax.empty(s, a.dtype), pltpu.HBM) for s, a in zip(shapes, arrs)]
    srcs = [pltpu.with_memory_space_constraint(a, pltpu.HBM) for a in arrs]

    def body(*refs):
        src_refs, land_refs = refs[:n], refs[n:2 * n]
        send_sems, recv_sems, local_sems = refs[2 * n + len(extra):2 * n + len(extra) + 3]
        token = refs[-1]
        local, remote = _split_copies(src_refs, land_refs, send_sems, recv_sems, local_sems, scatter)
        for cp in local + remote:
            cp.start()
        token[...] = jnp.zeros_like(token)

    res = pl.pallas_call(
        body, name=name,
        out_shape=[pltpu.SemaphoreType.DMA((n * (N_DEV - 1),)), pltpu.SemaphoreType.DMA((n * (N_DEV - 1),)), pltpu.SemaphoreType.DMA((n,))]
        + [pltpu.HBM(a.shape, a.dtype) for a in arrs] + [pltpu.HBM(s, a.dtype) for s, a in zip(shapes, arrs)]
        + [jax.ShapeDtypeStruct((8, 128), F32)],
        in_specs=[HBM_SPEC] * (2 * n) + [pl.BlockSpec(memory_space=pl.ANY)] * len(extra),
        out_specs=[SEM_SPEC] * 3 + [HBM_SPEC] * (2 * n) + [pl.BlockSpec(memory_space=pltpu.VMEM)],
        input_output_aliases={i: 3 + i for i in range(2 * n)},
        compiler_params=pltpu.CompilerParams(has_side_effects=EFFECT),
    )(*srcs, *lands, *extra)
    return (scatter, res[:3], res[3:3 + n], res[3 + n:3 + 2 * n]), res[-1]


def _exchange_wait(name, handle, after):
    scatter, sems, srcs, lands = handle
    n = len(srcs)

    def body(*refs):
        src_refs, land_refs = refs[:n], refs[n:2 * n]
        send_sems, recv_sems, local_sems = refs[2 * n:2 * n + 3]
        local, remote = _split_copies(src_refs, land_refs, send_sems, recv_sems, local_sems, scatter)
        for cp in remote:
            cp.wait_send()
            cp.wait_recv()
        for cp in local:
            cp.wait()

    res = pl.pallas_call(
        body, name=name,
        out_shape=[pltpu.HBM(a.shape, a.dtype) for a in srcs] + [pltpu.HBM(a.shape, a.dtype) for a in lands],
        in_specs=[HBM_SPEC] * (2 * n) + [SEM_SPEC] * 3 + [pl.BlockSpec(memory_space=pl.ANY)], out_specs=[HBM_SPEC] * (2 * n),
        input_output_aliases={i: i for i in range(2 * n)},
        compiler_params=pltpu.CompilerParams(has_side_effects=EFFECT),
    )(*srcs, *lands, *sems, after)
    return list(res[n:])


FIRST_COPIES = 4
RELAY_COPIES = 3


def _gather_copies(srcs, lands, sems):
    send_sems, recv_sems, local_sems = sems[:3]
    x, y, c = lax.axis_index("x"), lax.axis_index("y"), lax.axis_index("c")
    me = 4 * x + 2 * y + c
    chips = [(1 - x, y), (x, 1 - y), (1 - x, 1 - y)]
    local, first, relay = [], [], []
    for a in range(len(srcs)):
        local.append(pltpu.make_async_copy(srcs[a], lands[a].at[me], local_sems.at[a]))
        targets = [(x, y, 1 - c)] + [(px, py, c) for px, py in chips]
        first.append([pltpu.make_async_remote_copy(
            src_ref=srcs[a], dst_ref=lands[a].at[me], send_sem=send_sems.at[FIRST_COPIES * a + k],
            recv_sem=recv_sems.at[FIRST_COPIES * a + k], device_id=t, device_id_type=MESH) for k, t in enumerate(targets)])
        if len(sems) > 3:
            rsend, rrecv = sems[3:]
            slots = [lands[a].at[4 * px + 2 * py + c] for px, py in chips]
            relay.append([pltpu.make_async_remote_copy(
                src_ref=slot, dst_ref=slot, send_sem=rsend.at[RELAY_COPIES * a + j], recv_sem=rrecv.at[RELAY_COPIES * a + j],
                device_id=(x, y, 1 - c), device_id_type=MESH) for j, slot in enumerate(slots)])
    return local, first, relay


def _gather_start(name, arrs, after=None):
    n = len(arrs)
    extra = [] if after is None else [after]
    lands = [pltpu.with_memory_space_constraint(lax.empty((N_DEV,) + a.shape, a.dtype), pltpu.HBM) for a in arrs]
    srcs = [pltpu.with_memory_space_constraint(a, pltpu.HBM) for a in arrs]

    def body(*refs):
        at = 2 * n + len(extra)
        local, first, _ = _gather_copies(refs[:n], refs[n:2 * n], refs[at:at + 3])
        for cp in local + [cp for cps in first for cp in cps]:
            cp.start()
        refs[-1][...] = jnp.zeros_like(refs[-1])

    res = pl.pallas_call(
        body, name=name,
        out_shape=[pltpu.SemaphoreType.DMA((FIRST_COPIES * n,)), pltpu.SemaphoreType.DMA((FIRST_COPIES * n,)), pltpu.SemaphoreType.DMA((n,))]
        + [pltpu.HBM(a.shape, a.dtype) for a in arrs] + [pltpu.HBM((N_DEV,) + a.shape, a.dtype) for a in arrs]
        + [jax.ShapeDtypeStruct((8, 128), F32)],
        in_specs=[HBM_SPEC] * (2 * n) + [pl.BlockSpec(memory_space=pl.ANY)] * len(extra),
        out_specs=[SEM_SPEC] * 3 + [HBM_SPEC] * (2 * n) + [pl.BlockSpec(memory_space=pltpu.VMEM)],
        input_output_aliases={i: 3 + i for i in range(2 * n)},
        compiler_params=pltpu.CompilerParams(has_side_effects=EFFECT),
    )(*srcs, *lands, *extra)
    return (list(res[:3]), list(res[3:3 + n]), list(res[3 + n:3 + 2 * n])), res[-1]


def _gather_relay(name, handle, after):
    sems, srcs, lands = handle
    n = len(srcs)

    def body(*refs):
        in_sems = refs[2 * n:2 * n + 3]
        out_sems = refs[2 * n + 4 + 2 * n:2 * n + 4 + 2 * n + 2]
        _, first, relay = _gather_copies(refs[:n], refs[n:2 * n], list(in_sems) + list(out_sems))
        for a in range(n):
            for j in range(RELAY_COPIES):
                first[a][1 + j].wait_recv()
                relay[a][j].start()
        refs[-1][...] = jnp.zeros_like(refs[-1])

    res = pl.pallas_call(
        body, name=name,
        out_shape=[pltpu.HBM(a.shape, a.dtype) for a in srcs] + [pltpu.HBM(a.shape, a.dtype) for a in lands]
        + [pltpu.SemaphoreType.DMA((RELAY_COPIES * n,)), pltpu.SemaphoreType.DMA((RELAY_COPIES * n,)), jax.ShapeDtypeStruct((8, 128), F32)],
        in_specs=[HBM_SPEC] * (2 * n) + [SEM_SPEC] * 3 + [pl.BlockSpec(memory_space=pl.ANY)],
        out_specs=[HBM_SPEC] * (2 * n) + [SEM_SPEC] * 2 + [pl.BlockSpec(memory_space=pltpu.VMEM)],
        input_output_aliases={i: i for i in range(2 * n)},
        compiler_params=pltpu.CompilerParams(has_side_effects=EFFECT),
    )(*srcs, *lands, *sems, after)
    return (sems + list(res[2 * n:2 * n + 2]), list(res[:n]), list(res[n:2 * n])), res[-1]


def _gather_wait(name, handle, after):
    sems, srcs, lands = handle
    n = len(srcs)

    def body(*refs):
        local, first, relay = _gather_copies(refs[:n], refs[n:2 * n], refs[2 * n:2 * n + 5])
        for a in range(n):
            for cp in first[a]:
                cp.wait_send()
            first[a][0].wait_recv()
            for cp in relay[a]:
                cp.wait_send()
                cp.wait_recv()
            local[a].wait()

    res = pl.pallas_call(
        body, name=name,
        out_shape=[pltpu.HBM(a.shape, a.dtype) for a in srcs] + [pltpu.HBM(a.shape, a.dtype) for a in lands],
        in_specs=[HBM_SPEC] * (2 * n) + [SEM_SPEC] * 5 + [pl.BlockSpec(memory_space=pl.ANY)], out_specs=[HBM_SPEC] * (2 * n),
        input_output_aliases={i: i for i in range(2 * n)},
        compiler_params=pltpu.CompilerParams(has_side_effects=EFFECT),
    )(*srcs, *lands, *sems, after)
    return list(res[n:])


def _dep(dep, grid_rank):
    if dep is None:
        return [], []
    return [dep], [pl.BlockSpec((8, 128), (lambda i, j: (0, 0)) if grid_rank == 2 else (lambda i: (0, 0)))]


def _mm_step(name, fn, ins, in_specs, out_shape, out_spec, grid, dep=None, res=None):
    n = len(ins)
    dep_ins, dep_specs = _dep(dep, len(grid))
    res_ins, res_specs, out_shapes, out_specs = [], [], out_shape, out_spec
    if res is not None:
        h, mod, row_idx, lc = res
        tm, tn = out_spec.block_shape
        res_ins = [h, mod]
        res_specs = [pl.BlockSpec((tm, tn), lambda j, i: (i, j)), pl.BlockSpec((2, 8, tn), lambda j, i: (0, 0, j))]
        out_shapes, out_specs = [out_shape, jax.ShapeDtypeStruct(h.shape, h.dtype)], [out_spec, res_specs[0]]

    def body(*refs):
        outs = refs[n + len(res_ins) + len(dep_ins):]
        acc = fn(*refs[:n])
        outs[0][...] = acc.astype(outs[0].dtype)
        if res is not None:
            h_ref, mod_ref = refs[n:n + 2]
            row = pl.program_id(1) * tm + lax.broadcasted_iota(jnp.int32, (tm, 1), 0)
            gate = jnp.where(row < lc, mod_ref[0, row_idx:row_idx + 1, :], mod_ref[1, row_idx:row_idx + 1, :])
            outs[1][...] = h_ref[...] + gate * acc

    return pl.pallas_call(body, name=name, grid=grid, in_specs=list(in_specs) + res_specs + dep_specs, out_specs=out_specs,
                          out_shape=out_shapes, compiler_params=_params())(*ins, *res_ins, *dep_ins)


def _mm_tn(name, a, b, a_spec, b_spec, out_shape, out_spec, acc_shape, grid):
    nk = grid[-1]
    kax = len(grid) - 1
    if nk == 1:
        def whole(a_ref, b_ref, o_ref):
            o_ref[...] = _tn(_bf(a_ref[...]), _bf(b_ref[...])).astype(o_ref.dtype)

        return pl.pallas_call(whole, name=name, grid=grid, in_specs=[a_spec, b_spec], out_specs=out_spec,
                              out_shape=out_shape, compiler_params=_params())(a, b)

    def body(a_ref, b_ref, o_ref, acc_ref):
        k = pl.program_id(kax)

        @pl.when(k == 0)
        def _():
            acc_ref[...] = jnp.zeros_like(acc_ref)

        acc_ref[...] += _tn(_bf(a_ref[...]), _bf(b_ref[...]))

        @pl.when(k == nk - 1)
        def _():
            o_ref[...] = acc_ref[...].astype(o_ref.dtype)

    return pl.pallas_call(body, name=name, grid=grid, in_specs=[a_spec, b_spec], out_specs=out_spec,
                          out_shape=out_shape, scratch_shapes=[pltpu.VMEM(acc_shape, F32)],
                          compiler_params=_params())(a, b)


def _mm_cols(name, a, w3, out_dtype=F32, dep=None):
    M, K = a.shape
    J, _, n = w3.shape
    return _mm_step(
        name, lambda a_ref, w_ref: _nn(_bf(a_ref[...]), w_ref[...]), [a, w3],
        [pl.BlockSpec((M, K), lambda j, i: (0, 0), pipeline_mode=RESIDENT), pl.BlockSpec((None, K, n), lambda j, i: (j, 0, 0))],
        jax.ShapeDtypeStruct((M, J * n), out_dtype), pl.BlockSpec((M, n), lambda j, i: (0, j)), (J, 1), dep)


def _mm_plain(name, a, b, transpose_b, out_dtype=F32, tn=512, dep=None, res=None):
    M, K = a.shape
    N = b.shape[0] if transpose_b else b.shape[1]
    tn = _col_tile(N, tn)
    if res is None:
        tm, a_spec = M, pl.BlockSpec((M, K), lambda j, i: (0, 0), pipeline_mode=RESIDENT)
    else:
        tm = _row_tile(M, 1088)
        a_spec = pl.BlockSpec((tm, K), lambda j, i: (i, 0))
    if transpose_b:
        b_spec = pl.BlockSpec((tn, K), lambda j, i: (j, 0))
        fn = lambda a_ref, b_ref: _nt(_bf(a_ref[...]), _bf(b_ref[...]))
    else:
        b_spec = pl.BlockSpec((K, tn), lambda j, i: (0, j))
        fn = lambda a_ref, b_ref: _nn(_bf(a_ref[...]), _bf(b_ref[...]))
    return _mm_step(name, fn, [a, b], [a_spec, b_spec],
                    jax.ShapeDtypeStruct((M, N), out_dtype), pl.BlockSpec((tm, tn), lambda j, i: (i, j)),
                    (N // tn, M // tm), dep, res)


def _mm_shards_nn(name, a3, w3, tn=512, dep=None, res=None):
    J, M, k = a3.shape
    N = w3.shape[2]
    tm = _row_tile(M, 544)
    tn = _col_tile(N, tn)

    def fn(a_ref, w_ref):
        acc = _nn(a_ref[0], w_ref[0])
        for j in range(1, J):
            acc += _nn(a_ref[j], w_ref[j])
        return acc

    return _mm_step(name, fn, [a3, w3],
                    [pl.BlockSpec((J, tm, k), lambda jn, i: (0, i, 0)), pl.BlockSpec((J, k, tn), lambda jn, i: (0, 0, jn))],
                    jax.ShapeDtypeStruct((M, N), F32), pl.BlockSpec((tm, tn), lambda jn, i: (i, jn)), (N // tn, M // tm), dep, res)


def _mm_shards_nn2(name, a3, w3a, b3, w3b, tn=512):
    J, M, k = a3.shape
    N = w3a.shape[2]
    tm = _row_tile(M, 544)
    tn = _col_tile(N, tn)

    def fn(a_ref, wa_ref, b_ref, wb_ref):
        acc = _nn(a_ref[0], wa_ref[0]) + _nn(b_ref[0], wb_ref[0])
        for j in range(1, J):
            acc += _nn(a_ref[j], wa_ref[j]) + _nn(b_ref[j], wb_ref[j])
        return acc

    act = pl.BlockSpec((J, tm, k), lambda jn, i: (0, i, 0))
    wsp = pl.BlockSpec((J, k, tn), lambda jn, i: (0, 0, jn))
    return _mm_step(name, fn, [a3, w3a, b3, w3b], [act, wsp, act, wsp],
                    jax.ShapeDtypeStruct((M, N), F32), pl.BlockSpec((tm, tn), lambda jn, i: (i, jn)), (N // tn, M // tm))


def _mm_cols_nt(name, a, w3, tn=512):
    M = a.shape[0]
    J, N, n = w3.shape
    tm = _row_tile(M, 544)
    tn = _col_tile(N, tn)

    def fn(a_ref, w_ref):
        acc = _nt(a_ref[:, 0:n], w_ref[0])
        for j in range(1, J):
            acc += _nt(a_ref[:, j * n:(j + 1) * n], w_ref[j])
        return acc

    return _mm_step(name, fn, [a, w3],
                    [pl.BlockSpec((tm, J * n), lambda jn, i: (i, 0)), pl.BlockSpec((J, tn, n), lambda jn, i: (0, jn, 0))],
                    jax.ShapeDtypeStruct((M, N), F32), pl.BlockSpec((tm, tn), lambda jn, i: (i, jn)), (N // tn, M // tm))


def _wgrad_cols(name, a, b, J):
    T, K = a.shape
    n = b.shape[1] // J
    return _mm_tn(name, a, b, pl.BlockSpec((T, K), lambda j, t: (0, 0), pipeline_mode=RESIDENT), pl.BlockSpec((T, n), lambda j, t: (0, j)),
                  jax.ShapeDtypeStruct((J, K, n), BF16), pl.BlockSpec((None, K, n), lambda j, t: (j, 0, 0)), (K, n), (J, 1))


def _wgrad_rows(name, a, b, tk=512):
    T, K = a.shape
    N = b.shape[1]
    tk = _col_tile(K, tk)
    return _mm_tn(name, a, b, pl.BlockSpec((T, tk), lambda kb, t: (0, kb)), pl.BlockSpec((T, N), lambda kb, t: (0, 0), pipeline_mode=RESIDENT),
                  jax.ShapeDtypeStruct((K, N), BF16), pl.BlockSpec((tk, N), lambda kb, t: (kb, 0)), (tk, N), (K // tk, 1))


def _wgrad_up(name, a, b3):
    T, K = a.shape
    J, _, k = b3.shape
    tt = _row_tile(T, 1088)
    return _mm_tn(name, a, b3, pl.BlockSpec((tt, K), lambda j, t: (t, 0)), pl.BlockSpec((None, tt, k), lambda j, t: (j, t, 0)),
                  jax.ShapeDtypeStruct((J, K, k), BF16), pl.BlockSpec((None, K, k), lambda j, t: (j, 0, 0)), (K, k), (J, T // tt))


def _wgrad_down(name, a3, b):
    J, T, k = a3.shape
    N = b.shape[1]
    return _mm_tn(name, a3, b, pl.BlockSpec((None, T, k), lambda j, t: (j, 0, 0)),
                  pl.BlockSpec((T, N), lambda j, t: (0, 0), pipeline_mode=RESIDENT),
                  jax.ShapeDtypeStruct((J, k, N), BF16), pl.BlockSpec((None, k, N), lambda j, t: (j, 0, 0)), (k, N), (J, 1))


def _seg(i):
    return jnp.minimum(i, 1)


def _rstd(x):
    return lax.rsqrt(jnp.mean(x * x, axis=-1, keepdims=True) + EPS)


def _norm_mod(name, h, g, mod, which):
    T, D = h.shape

    def body(h_ref, g_ref, mod_ref, o_ref):
        x = h_ref[...]
        n = x * _rstd(x) * g_ref[...]
        shift = mod_ref[3 * which:3 * which + 1, :]
        scale = mod_ref[3 * which + 1:3 * which + 2, :]
        o_ref[...] = (n * (1 + scale) + shift).astype(o_ref.dtype)

    row = pl.BlockSpec((ROW_TILE, D), lambda i: (i, 0))
    return pl.pallas_call(
        body, name=name, grid=(T // ROW_TILE,),
        in_specs=[row, pl.BlockSpec((1, D), lambda i: (0, 0)), pl.BlockSpec((None, 8, D), lambda i: (_seg(i), 0, 0))],
        out_specs=row, out_shape=jax.ShapeDtypeStruct((T, D), BF16), compiler_params=_params())(h, g, mod)


def _norm_mod_bwd(name, dxn, h, g, mod, which, dres, latent_only=False, gate=None):
    T, D = h.shape
    n_gate = 0 if gate is None else 2

    def body(dxn_ref, h_ref, g_ref, mod_ref, dres_ref, *rest):
        dh_ref, dmod_ref, dg_ref = rest[n_gate:n_gate + 3]
        i = pl.program_id(0)
        x = h_ref[...]
        r = _rstd(x)
        xhat = x * r
        g = g_ref[...]
        n = xhat * g
        scale = mod_ref[3 * which + 1:3 * which + 2, :]
        dxn = dxn_ref[...]
        dn = dxn * (1 + scale)
        dxh = dn * g
        dh = dres_ref[...] + r * (dxh - xhat * jnp.mean(dxh * xhat, axis=-1, keepdims=True))
        if latent_only:
            @pl.when(i > 0)
            def _():
                dh_ref[...] = dh
        else:
            dh_ref[...] = dh

        @pl.when(i <= 1)
        def _():
            dmod_ref[...] = jnp.zeros_like(dmod_ref)

        @pl.when(i == 0)
        def _():
            dg_ref[...] = jnp.zeros_like(dg_ref)

        dmod_ref[3 * which:3 * which + 1, :] += jnp.sum(dxn, axis=0, keepdims=True)
        dmod_ref[3 * which + 1:3 * which + 2, :] += jnp.sum(dxn * n, axis=0, keepdims=True)
        dg_ref[0:1, :] += jnp.sum(dn * xhat, axis=0, keepdims=True)

        if gate is not None:
            y_ref, gmod_ref = rest[:2]
            dy_ref, dgmod_ref = rest[5:7]
            dy_ref[...] = (dh * gmod_ref[gate[2]:gate[2] + 1, :]).astype(dy_ref.dtype)

            @pl.when(i <= 1)
            def _():
                dgmod_ref[...] = jnp.zeros_like(dgmod_ref)

            dgmod_ref[gate[2]:gate[2] + 1, :] += jnp.sum(dh * y_ref[...], axis=0, keepdims=True)

    row = pl.BlockSpec((ROW_TILE, D), lambda i: (i, 0))
    modspec = pl.BlockSpec((None, 8, D), lambda i: (_seg(i), 0, 0))
    dh_rows = T - ROW_TILE if latent_only else T
    dh_spec = pl.BlockSpec((ROW_TILE, D), lambda i: (jnp.maximum(i - 1, 0), 0)) if latent_only else row
    gate_ins, gate_specs, gate_outs, gate_shapes = [], [], [], []
    if gate is not None:
        gate_ins, gate_specs = [gate[0], gate[1]], [row, modspec]
        gate_outs, gate_shapes = [row, modspec], [jax.ShapeDtypeStruct((T, D), BF16), jax.ShapeDtypeStruct((2, 8, D), F32)]
    return pl.pallas_call(
        body, name=name, grid=(T // ROW_TILE,),
        in_specs=[row, row, pl.BlockSpec((1, D), lambda i: (0, 0)), modspec, row] + gate_specs,
        out_specs=[dh_spec, modspec, pl.BlockSpec((8, D), lambda i: (0, 0))] + gate_outs,
        out_shape=[jax.ShapeDtypeStruct((dh_rows, D), F32), jax.ShapeDtypeStruct((2, 8, D), F32), jax.ShapeDtypeStruct((8, D), F32)] + gate_shapes,
        compiler_params=_params())(dxn, h, g, mod, dres, *gate_ins)


def _gate_bwd(name, dh, y, mod, row_idx):
    T, D = dh.shape

    def body(dh_ref, y_ref, mod_ref, dy_ref, dmod_ref):
        i = pl.program_id(0)
        dh = dh_ref[...]
        dy_ref[...] = (dh * mod_ref[row_idx:row_idx + 1, :]).astype(dy_ref.dtype)

        @pl.when(i <= 1)
        def _():
            dmod_ref[...] = jnp.zeros_like(dmod_ref)

        dmod_ref[row_idx:row_idx + 1, :] += jnp.sum(dh * y_ref[...], axis=0, keepdims=True)

    row = pl.BlockSpec((ROW_TILE, D), lambda i: (i, 0))
    modspec = pl.BlockSpec((None, 8, D), lambda i: (_seg(i), 0, 0))
    return pl.pallas_call(
        body, name=name, grid=(T // ROW_TILE,), in_specs=[row, row, modspec], out_specs=[row, modspec],
        out_shape=[jax.ShapeDtypeStruct((T, D), BF16), jax.ShapeDtypeStruct((2, 8, D), F32)],
        compiler_params=_params())(dh, y, mod)


def _rot(y):
    lane = lax.broadcasted_iota(jnp.int32, y.shape, 1)
    return jnp.where((lane & 32) == 0, pltpu.roll(y, 96, 1), pltpu.roll(y, 32, 1))


def _qk_prep(name, P, q_g, k_g, rope_c, rope_s, cfg):
    T = P.shape[0]
    ATT, KVW = cfg['ATT'], cfg['KVW']

    def body(q_ref, k_ref, v_ref, qg_ref, kg_ref, c_ref, s_ref, qo_ref, ko_ref, vo_ref):
        cc, ss = c_ref[...], s_ref[...]

        def head(x, g):
            y = x * _rstd(x) * g
            return y * cc + _rot(y) * ss

        for hh in range(ATT // HEAD_DIM):
            sl = slice(hh * HEAD_DIM, (hh + 1) * HEAD_DIM)
            qo_ref[:, sl] = (head(q_ref[:, sl], qg_ref[...]) * Q_SCALE).astype(qo_ref.dtype)
        for hh in range(KVW // HEAD_DIM):
            sl = slice(hh * HEAD_DIM, (hh + 1) * HEAD_DIM)
            ko_ref[:, sl] = head(k_ref[:, sl], kg_ref[...]).astype(ko_ref.dtype)
        vo_ref[...] = v_ref[...].astype(vo_ref.dtype)

    kb = ATT // KVW
    gain = pl.BlockSpec((1, HEAD_DIM), lambda i: (0, 0))
    tab = pl.BlockSpec((ROW_TILE, HEAD_DIM), lambda i: (i, 0))
    qs = pl.BlockSpec((ROW_TILE, ATT), lambda i: (i, 0))
    ks = pl.BlockSpec((ROW_TILE, KVW), lambda i: (i, 0))
    return pl.pallas_call(
        body, name=name, grid=(T // ROW_TILE,),
        in_specs=[qs, pl.BlockSpec((ROW_TILE, KVW), lambda i: (i, kb)), pl.BlockSpec((ROW_TILE, KVW), lambda i: (i, kb + 1)),
                  gain, gain, tab, tab],
        out_specs=[qs, ks, ks],
        out_shape=[jax.ShapeDtypeStruct((T, ATT), BF16), jax.ShapeDtypeStruct((T, KVW), BF16), jax.ShapeDtypeStruct((T, KVW), BF16)],
        compiler_params=_params())(P, P, P, q_g, k_g, rope_c, rope_s)


def _qk_prep_bwd(name, dqr, dkr, P, q_g, k_g, rope_c, rope_s, cfg):
    T = P.shape[0]
    ATT, KVW = cfg['ATT'], cfg['KVW']

    def body(dq_ref, dk_ref, q_ref, k_ref, qg_ref, kg_ref, c_ref, s_ref, dqo_ref, dko_ref, dqg_ref, dkg_ref):
        i = pl.program_id(0)
        cc, ss = c_ref[...], s_ref[...]

        @pl.when(i == 0)
        def _():
            dqg_ref[...] = jnp.zeros_like(dqg_ref)
            dkg_ref[...] = jnp.zeros_like(dkg_ref)

        def head(x, g, dout):
            dy = dout * cc + _rot(dout * ss)
            r = _rstd(x)
            xhat = x * r
            dxh = dy * g
            dx = r * (dxh - xhat * jnp.mean(dxh * xhat, axis=-1, keepdims=True))
            return dx, jnp.sum(dy * xhat, axis=0, keepdims=True)

        dg = jnp.zeros((1, HEAD_DIM), F32)
        for hh in range(ATT // HEAD_DIM):
            sl = slice(hh * HEAD_DIM, (hh + 1) * HEAD_DIM)
            dx, d = head(q_ref[:, sl], qg_ref[...], dq_ref[:, sl] * ATT_SCALE)
            dqo_ref[:, sl] = dx.astype(dqo_ref.dtype)
            dg += d
        dqg_ref[0:1, :] += dg
        dg = jnp.zeros((1, HEAD_DIM), F32)
        for hh in range(KVW // HEAD_DIM):
            sl = slice(hh * HEAD_DIM, (hh + 1) * HEAD_DIM)
            dx, d = head(k_ref[:, sl], kg_ref[...], dk_ref[:, sl] * (1.0 / LOG2_E))
            dko_ref[:, sl] = dx.astype(dko_ref.dtype)
            dg += d
        dkg_ref[0:1, :] += dg

    kb = ATT // KVW
    gain = pl.BlockSpec((1, HEAD_DIM), lambda i: (0, 0))
    dgain = pl.BlockSpec((8, HEAD_DIM), lambda i: (0, 0))
    tab = pl.BlockSpec((ROW_TILE, HEAD_DIM), lambda i: (i, 0))
    qs = pl.BlockSpec((ROW_TILE, ATT), lambda i: (i, 0))
    ks = pl.BlockSpec((ROW_TILE, KVW), lambda i: (i, 0))
    return pl.pallas_call(
        body, name=name, grid=(T // ROW_TILE,),
        in_specs=[qs, ks, qs, pl.BlockSpec((ROW_TILE, KVW), lambda i: (i, kb)), gain, gain, tab, tab],
        out_specs=[qs, ks, dgain, dgain],
        out_shape=[jax.ShapeDtypeStruct((T, ATT), BF16), jax.ShapeDtypeStruct((T, KVW), BF16),
                   jax.ShapeDtypeStruct((8, HEAD_DIM), F32), jax.ShapeDtypeStruct((8, HEAD_DIM), F32)],
        compiler_params=_params())(dqr, dkr, P, P, q_g, k_g, rope_c, rope_s)


def _att_specs(T, G):
    qs = pl.BlockSpec((ROW_TILE, G * HEAD_DIM), lambda h, i: (i, h))
    kvs = pl.BlockSpec((T, HEAD_DIM), lambda h, i: (0, h))
    return qs, kvs


def _attn_dense_fwd(name, q, k, v, cfg):
    T, G, Lc = q.shape[0], cfg['G'], cfg['Lc']

    def body(q_ref, k_ref, v_ref, o_ref, lse_ref):
        def attend(rows):
            kk, vv = k_ref[0:rows, :], v_ref[0:rows, :]
            for g in range(G):
                sl = slice(g * HEAD_DIM, (g + 1) * HEAD_DIM)
                s = _nt(q_ref[:, sl], kk)
                m = jnp.max(s, axis=1, keepdims=True)
                p = jnp.exp2(s - m)
                l = jnp.sum(p, axis=1, keepdims=True)
                o_ref[:, sl] = _nn(_bf(p), vv) / l
                lse_ref[:, sl] = jnp.broadcast_to(m + jnp.log2(l), (ROW_TILE, HEAD_DIM))

        @pl.when(pl.program_id(1) == 0)
        def _():
            attend(Lc)

        @pl.when(pl.program_id(1) > 0)
        def _():
            attend(T)

    qs, kvs = _att_specs(T, G)
    return pl.pallas_call(
        body, name=name, grid=(cfg['NKV'], T // ROW_TILE), in_specs=[qs, kvs, kvs], out_specs=[qs, qs],
        out_shape=[jax.ShapeDtypeStruct(q.shape, F32), jax.ShapeDtypeStruct(q.shape, F32)],
        compiler_params=_params())(q, k, v)


def _attn_dense_bwd(name, q, k, v, o, lse, dmix, cfg):
    T, G, Lc = q.shape[0], cfg['G'], cfg['Lc']

    def body(q_ref, k_ref, v_ref, o_ref, lse_ref, do_ref, dq_ref, dk_ref, dv_ref):
        i = pl.program_id(1)

        @pl.when(i == 0)
        def _():
            dk_ref[...] = jnp.zeros_like(dk_ref)
            dv_ref[...] = jnp.zeros_like(dv_ref)

        def attend(rows):
            kk, vv = k_ref[0:rows, :], v_ref[0:rows, :]
            for g in range(G):
                sl = slice(g * HEAD_DIM, (g + 1) * HEAD_DIM)
                qg, do = q_ref[:, sl], do_ref[:, sl]
                delta = jnp.sum(do * o_ref[:, sl], axis=1, keepdims=True)
                p = jnp.exp2(_nt(qg, kk) - lse_ref[:, g * HEAD_DIM:g * HEAD_DIM + 1])
                dob = _bf(do)
                dv_ref[0:rows, :] += _tn(_bf(p), dob)
                ds = _bf(p * (_nt(dob, vv) - delta))
                dq_ref[:, sl] = _nn(ds, kk)
                dk_ref[0:rows, :] += _tn(ds, qg)

        @pl.when(i == 0)
        def _():
            attend(Lc)

        @pl.when(i > 0)
        def _():
            attend(T)

    qs, kvs = _att_specs(T, G)
    return pl.pallas_call(
        body, name=name, grid=(cfg['NKV'], T // ROW_TILE), in_specs=[qs, kvs, kvs, qs, qs, qs], out_specs=[qs, kvs, kvs],
        out_shape=[jax.ShapeDtypeStruct(q.shape, F32), jax.ShapeDtypeStruct(k.shape, F32), jax.ShapeDtypeStruct(k.shape, F32)],
        compiler_params=_params())(q, k, v, o, lse, dmix)


def _band(i, T, Lc):
    start = pl.multiple_of(jnp.clip(WINDOW + (i - 1) * ROW_TILE, 0, T - BAND), WINDOW)
    qpos = (i - 1) * ROW_TILE + lax.broadcasted_iota(jnp.int32, (ROW_TILE, 1), 0)
    kpos = start - Lc + lax.broadcasted_iota(jnp.int32, (1, BAND), 1)
    ok = (jnp.abs(kpos - qpos) <= WINDOW) & (kpos >= 0) & (i > 0)
    return start, jnp.where(ok, 0.0, NEG_INF).astype(F32)


def _attn_win_fwd(name, q, k, v, sink, cfg):
    T, G, Lc = q.shape[0], cfg['G'], cfg['Lc']

    def body(sink_ref, q_ref, k_ref, v_ref, o_ref, lse_ref):
        h, i = pl.program_id(0), pl.program_id(1)
        start, bias = _band(i, T, Lc)
        kc, vc = k_ref[0:Lc, :], v_ref[0:Lc, :]
        kb, vb = k_ref[pl.ds(start, BAND), :], v_ref[pl.ds(start, BAND), :]
        for g in range(G):
            sl = slice(g * HEAD_DIM, (g + 1) * HEAD_DIM)
            qg = q_ref[:, sl]
            sk = sink_ref[h * G + g] * LOG2_E
            sc = _nt(qg, kc)
            sb = _nt(qg, kb) + bias
            m = jnp.maximum(jnp.maximum(jnp.max(sc, axis=1, keepdims=True), jnp.max(sb, axis=1, keepdims=True)), sk)
            pc, pb = jnp.exp2(sc - m), jnp.exp2(sb - m)
            l = jnp.sum(pc, axis=1, keepdims=True) + jnp.sum(pb, axis=1, keepdims=True) + jnp.exp2(sk - m)
            o_ref[:, sl] = (_nn(_bf(pc), vc) + _nn(_bf(pb), vb)) / l
            lse_ref[:, sl] = jnp.broadcast_to(m + jnp.log2(l), (ROW_TILE, HEAD_DIM))

    qs, kvs = _att_specs(T, G)
    return pl.pallas_call(
        body, name=name, grid=(cfg['NKV'], T // ROW_TILE),
        in_specs=[pl.BlockSpec(memory_space=pltpu.SMEM), qs, kvs, kvs], out_specs=[qs, qs],
        out_shape=[jax.ShapeDtypeStruct(q.shape, F32), jax.ShapeDtypeStruct(q.shape, F32)],
        compiler_params=_params())(sink, q, k, v)


def _attn_win_bwd(name, q, k, v, o, lse, dmix, sink, cfg):
    T, G, Lc = q.shape[0], cfg['G'], cfg['Lc']

    def body(sink_ref, q_ref, k_ref, v_ref, o_ref, lse_ref, do_ref, dq_ref, dk_ref, dv_ref, dsink_ref):
        h, i = pl.program_id(0), pl.program_id(1)
        start, bias = _band(i, T, Lc)
        kc, vc = k_ref[0:Lc, :], v_ref[0:Lc, :]
        kb, vb = k_ref[pl.ds(start, BAND), :], v_ref[pl.ds(start, BAND), :]

        @pl.when(i == 0)
        def _():
            dk_ref[...] = jnp.zeros_like(dk_ref)
            dv_ref[...] = jnp.zeros_like(dv_ref)
            dsink_ref[...] = jnp.zeros_like(dsink_ref)

        for g in range(G):
            sl = slice(g * HEAD_DIM, (g + 1) * HEAD_DIM)
            qg, do = q_ref[:, sl], do_ref[:, sl]
            lse = lse_ref[:, g * HEAD_DIM:g * HEAD_DIM + 1]
            delta = jnp.sum(do * o_ref[:, sl], axis=1, keepdims=True)
            pc = jnp.exp2(_nt(qg, kc) - lse)
            pb = jnp.exp2(_nt(qg, kb) + bias - lse)
            ps = jnp.exp2(sink_ref[h * G + g] * LOG2_E - lse)
            dob = _bf(do)
            dv_ref[0:Lc, :] += _tn(_bf(pc), dob)
            dv_ref[pl.ds(start, BAND), :] += _tn(_bf(pb), dob)
            dsc = _bf(pc * (_nt(dob, vc) - delta))
            dsb = _bf(pb * (_nt(dob, vb) - delta))
            dq_ref[:, sl] = _nn(dsc, kc) + _nn(dsb, kb)
            dk_ref[0:Lc, :] += _tn(dsc, qg)
            dk_ref[pl.ds(start, BAND), :] += _tn(dsb, qg)
            dsk = jnp.where(i > 0, -jnp.sum(ps * delta, axis=0, keepdims=True), 0.0)
            dsink_ref[:, sl] += jnp.broadcast_to(dsk, (8, HEAD_DIM))

    qs, kvs = _att_specs(T, G)
    return pl.pallas_call(
        body, name=name, grid=(cfg['NKV'], T // ROW_TILE),
        in_specs=[pl.BlockSpec(memory_space=pltpu.SMEM), qs, kvs, kvs, qs, qs, qs],
        out_specs=[qs, kvs, kvs, pl.BlockSpec((None, 8, G * HEAD_DIM), lambda h, i: (h, 0, 0))],
        out_shape=[jax.ShapeDtypeStruct(q.shape, F32), jax.ShapeDtypeStruct(k.shape, F32), jax.ShapeDtypeStruct(k.shape, F32),
                   jax.ShapeDtypeStruct((cfg['NKV'], 8, G * HEAD_DIM), F32)],
        compiler_params=_params())(sink, q, k, v, o, lse, dmix)


def _seq_pos(T, Lc):
    row = lax.broadcasted_iota(jnp.int32, (T, 1), 0)
    return jnp.where(row < Lc, row, row - Lc), jnp.where(row < Lc, Lc, T - Lc)


def _fw(x, k, pos, seglen):
    return jnp.where(pos + k < seglen, pltpu.roll(x, x.shape[0] - k, 0), 0.0)


def _bw(x, k, pos):
    return jnp.where(pos - k >= 0, pltpu.roll(x, k, 0), 0.0)


def _conv_fwd(name, P, conv_w8, cfg):
    T, Lc = P.shape[0], cfg['Lc']
    cb = (cfg['ATT'] + 2 * cfg['KVW']) // HEAD_DIM
    na = AUX_WIDTH // HEAD_DIM

    def body(gb_ref, gc_ref, u_ref, w_ref, o_ref):
        pos, seglen = _seq_pos(T, Lc)
        z = gc_ref[...] * u_ref[...]
        conv = w_ref[0:1, :] * _bw(z, 1, pos) + w_ref[1:2, :] * z + w_ref[2:3, :] * _fw(z, 1, pos, seglen)
        o_ref[...] = gb_ref[...] * conv

    col = lambda off: pl.BlockSpec((T, HEAD_DIM), lambda c: (0, cb + off + c))
    return pl.pallas_call(
        body, name=name, grid=(na,),
        in_specs=[col(0), col(na), col(2 * na), pl.BlockSpec((8, HEAD_DIM), lambda c: (0, c))],
        out_specs=pl.BlockSpec((T, HEAD_DIM), lambda c: (0, c)),
        out_shape=jax.ShapeDtypeStruct((T, AUX_WIDTH), F32), compiler_params=_params())(P, P, P, conv_w8)


def _conv_bwd(name, P, conv_w8, dmix, cfg):
    T, Lc = P.shape[0], cfg['Lc']
    cb = (cfg['ATT'] + 2 * cfg['KVW']) // HEAD_DIM
    ob = cfg['ATT'] // HEAD_DIM
    na = AUX_WIDTH // HEAD_DIM

    def body(gb_ref, gc_ref, u_ref, w_ref, do_ref, dgb_ref, dgc_ref, du_ref, dw_ref):
        pos, seglen = _seq_pos(T, Lc)
        gc, u, do = gc_ref[...], u_ref[...], do_ref[...]
        z = gc * u
        zm, zp = _bw(z, 1, pos), _fw(z, 1, pos, seglen)
        w0, w1, w2 = w_ref[0:1, :], w_ref[1:2, :], w_ref[2:3, :]
        dgb_ref[...] = (do * (w0 * zm + w1 * z + w2 * zp)).astype(dgb_ref.dtype)
        dc = do * gb_ref[...]
        dz = w0 * _fw(dc, 1, pos, seglen) + w1 * dc + w2 * _bw(dc, 1, pos)
        dgc_ref[...] = (dz * u).astype(dgc_ref.dtype)
        du_ref[...] = (dz * gc).astype(du_ref.dtype)
        dw_ref[...] = jnp.zeros_like(dw_ref)
        dw_ref[0:1, :] = jnp.sum(dc * zm, axis=0, keepdims=True)
        dw_ref[1:2, :] = jnp.sum(dc * z, axis=0, keepdims=True)
        dw_ref[2:3, :] = jnp.sum(dc * zp, axis=0, keepdims=True)

    col = lambda off: pl.BlockSpec((T, HEAD_DIM), lambda c: (0, cb + off + c))
    wspec = pl.BlockSpec((8, HEAD_DIM), lambda c: (0, c))
    ocol = lambda off: pl.BlockSpec((T, HEAD_DIM), lambda c: (0, off + c))
    return pl.pallas_call(
        body, name=name, grid=(na,),
        in_specs=[col(0), col(na), col(2 * na), wspec, ocol(ob)],
        out_specs=[ocol(0), ocol(0), ocol(0), wspec],
        out_shape=[jax.ShapeDtypeStruct((T, AUX_WIDTH), BF16)] * 3 + [jax.ShapeDtypeStruct((8, AUX_WIDTH), F32)],
        compiler_params=_params())(P, P, P, conv_w8, dmix)


def _window_sums(x, half, pos, seglen):
    fwd, bwd = x, x
    s = 1
    while s < half:
        fwd = fwd + _fw(fwd, s, pos, seglen)
        bwd = bwd + _bw(bwd, s, pos)
        s *= 2
    return fwd, bwd


def _pooled(u, half, pos, seglen):
    fwd, bwd = _window_sums(u, half, pos, seglen)
    cnt = (jnp.minimum(pos + half, seglen) - jnp.maximum(pos - half, 0)).astype(F32)
    return (fwd + _bw(bwd, 1, pos)) / cnt - u, cnt


def _pool_fwd(name, P, pool_w, pool_scale, cfg):
    T, Lc = P.shape[0], cfg['Lc']
    cb = (cfg['ATT'] + 2 * cfg['KVW']) // HEAD_DIM

    def body(u_ref, w_ref, s_ref, o_ref):
        g = pl.program_id(0)
        pos, seglen = _seq_pos(T, Lc)
        for k, half in enumerate(POOL_HALF):
            @pl.when(g == k)
            def _(half=half):
                pooled, _ = _pooled(u_ref[...], half, pos, seglen)
                o_ref[...] = _nn(_bf(pooled), _bf(w_ref[...])) * s_ref[...]

    return pl.pallas_call(
        body, name=name, grid=(AUX_GROUPS,),
        in_specs=[pl.BlockSpec((T, HEAD_DIM), lambda g: (0, cb + g)), pl.BlockSpec((None, HEAD_DIM, HEAD_DIM), lambda g: (g, 0, 0)),
                  pl.BlockSpec((1, HEAD_DIM), lambda g: (0, g))],
        out_specs=pl.BlockSpec((T, HEAD_DIM), lambda g: (0, g)),
        out_shape=jax.ShapeDtypeStruct((T, AUX_WIDTH), F32), compiler_params=_params())(P, pool_w, pool_scale)


def _pool_bwd(name, P, pool_w, pool_scale, dmix, cfg):
    T, Lc = P.shape[0], cfg['Lc']
    cb = (cfg['ATT'] + 2 * cfg['KVW']) // HEAD_DIM
    ob = cfg['ATT'] // HEAD_DIM

    def body(u_ref, w_ref, s_ref, do_ref, du_ref, dw_ref, ds_ref):
        g = pl.program_id(0)
        pos, seglen = _seq_pos(T, Lc)
        for k, half in enumerate(POOL_HALF):
            @pl.when(g == k)
            def _(half=half):
                do = do_ref[...]
                pooled, cnt = _pooled(u_ref[...], half, pos, seglen)
                wb = _bf(w_ref[...])
                mixed = _nn(_bf(pooled), wb)
                ds_ref[...] = jnp.broadcast_to(jnp.sum(do * mixed, axis=0, keepdims=True), ds_ref.shape)
                dmixed = _bf(do * s_ref[...])
                dw_ref[...] = _tn(_bf(pooled), dmixed)
                dpooled = _nt(dmixed, wb)
                e = dpooled / cnt
                fwd, bwd = _window_sums(e, half, pos, seglen)
                adj = fwd + _fw(e, half, pos, seglen) + _bw(bwd, 1, pos) - _bw(e, half, pos)
                du_ref[...] = (adj - dpooled).astype(du_ref.dtype)

    wspec = pl.BlockSpec((None, HEAD_DIM, HEAD_DIM), lambda g: (g, 0, 0))
    return pl.pallas_call(
        body, name=name, grid=(AUX_GROUPS,),
        in_specs=[pl.BlockSpec((T, HEAD_DIM), lambda g: (0, cb + g)), wspec, pl.BlockSpec((1, HEAD_DIM), lambda g: (0, g)),
                  pl.BlockSpec((T, HEAD_DIM), lambda g: (0, ob + g))],
        out_specs=[pl.BlockSpec((T, HEAD_DIM), lambda g: (0, g)), wspec, pl.BlockSpec((8, HEAD_DIM), lambda g: (0, g))],
        out_shape=[jax.ShapeDtypeStruct((T, AUX_WIDTH), BF16), jax.ShapeDtypeStruct(pool_w.shape, F32),
                   jax.ShapeDtypeStruct((8, AUX_WIDTH), F32)],
        compiler_params=_params())(P, pool_w, pool_scale, dmix)


def _ffn_up(name, hn, wg3, wu3, dep=None):
    T, D = hn.shape
    J, k, _ = wg3.shape
    tm = _row_tile(T, 1088)
    dep_ins, dep_specs = _dep(dep, 2)

    def body(x_ref, wg_ref, wu_ref, *rest):
        s_ref, ud_ref, a_ref = rest[len(dep_ins):]
        x = x_ref[pl.ds(pl.multiple_of(pl.program_id(1) * tm, 16), tm), :]
        g, u = _nt(x, wg_ref[...]), _nt(x, wu_ref[...])
        sig = jax.nn.sigmoid(g)
        silu = g * sig
        s_ref[...] = silu.astype(s_ref.dtype)
        ud_ref[...] = (u * (sig * (1 + g * (1 - sig)))).astype(ud_ref.dtype)
        a_ref[...] = (silu * u).astype(a_ref.dtype)

    wspec = pl.BlockSpec((None, k, D), lambda j, i: (j, 0, 0))
    ospec = pl.BlockSpec((None, tm, k), lambda j, i: (j, i, 0))
    return pl.pallas_call(
        body, name=name, grid=(J, T // tm),
        in_specs=[pl.BlockSpec((T, D), lambda j, i: (0, 0), pipeline_mode=RESIDENT), wspec, wspec] + dep_specs,
        out_specs=[ospec, ospec, ospec],
        out_shape=[jax.ShapeDtypeStruct((J, T, k), BF16)] * 3,
        compiler_params=_params())(hn, wg3, wu3, *dep_ins)


def _ffn_dact(name, dF, wd3, silu_g, u_dsilu):
    T, D = dF.shape
    J, k, _ = wd3.shape
    tm = _row_tile(T, 1088)

    def body(df_ref, wd_ref, s_ref, ud_ref, dg_ref, du_ref):
        rows = pl.ds(pl.multiple_of(pl.program_id(1) * tm, 16), tm)
        da = _nt(df_ref[rows, :], wd_ref[...])
        du_ref[...] = (da * s_ref[...].astype(F32)).astype(du_ref.dtype)
        dg_ref[...] = (da * ud_ref[...].astype(F32)).astype(dg_ref.dtype)

    aspec = pl.BlockSpec((None, tm, k), lambda j, i: (j, i, 0))
    return pl.pallas_call(
        body, name=name, grid=(J, T // tm),
        in_specs=[pl.BlockSpec((T, D), lambda j, i: (0, 0), pipeline_mode=RESIDENT), pl.BlockSpec((None, k, D), lambda j, i: (j, 0, 0)), aspec, aspec],
        out_specs=[aspec, aspec],
        out_shape=[jax.ShapeDtypeStruct((J, T, k), BF16), jax.ShapeDtypeStruct((J, T, k), BF16)],
        compiler_params=_params())(dF, wd3, silu_g, u_dsilu)


def _loss_head(name, h, g, target, cfg):
    T, D = h.shape

    def body(h_ref, g_ref, t_ref, dh_ref, loss_ref, dg_ref):
        i = pl.program_id(0)

        @pl.when(i == 0)
        def _():
            dh_ref[...] = jnp.zeros_like(dh_ref)
            loss_ref[...] = jnp.zeros_like(loss_ref)
            dg_ref[...] = jnp.zeros_like(dg_ref)

        @pl.when(i > 0)
        def _():
            x = h_ref[...]
            r = _rstd(x)
            xhat = x * r
            gg = g_ref[...]
            err = xhat * gg - t_ref[...]
            loss_ref[...] += 0.5 * jnp.sum(jnp.sum(err * err, axis=1, keepdims=True) / D, axis=0, keepdims=True)
            dy = err / D
            dg_ref[0:1, :] += jnp.sum(dy * xhat, axis=0, keepdims=True)
            dxh = dy * gg
            dh_ref[...] = r * (dxh - xhat * jnp.mean(dxh * xhat, axis=-1, keepdims=True))

    row = pl.BlockSpec((ROW_TILE, D), lambda i: (i, 0))
    return pl.pallas_call(
        body, name=name, grid=(T // ROW_TILE,),
        in_specs=[row, pl.BlockSpec((1, D), lambda i: (0, 0)), pl.BlockSpec((ROW_TILE, D), lambda i: (jnp.maximum(i - 1, 0), 0))],
        out_specs=[row, pl.BlockSpec((8, 128), lambda i: (0, 0)), pl.BlockSpec((8, D), lambda i: (0, 0))],
        out_shape=[jax.ShapeDtypeStruct((T, D), F32), jax.ShapeDtypeStruct((8, 128), F32), jax.ShapeDtypeStruct((8, D), F32)],
        compiler_params=_params())(h, g, target)


def _adamw(name, parts, w, m, v, dep=None):
    R, C = w.shape
    n_parts = parts.shape[0]
    tr = _row_tile(R, max(16, (1 << 18) // C)) if R % 16 == 0 else R
    bc1 = 1.0 - ADAM_B1 ** ADAM_STEP
    bc2 = 1.0 - ADAM_B2 ** ADAM_STEP
    dep_ins, dep_specs = _dep(dep, 1)

    def body(p_ref, w_ref, m_ref, v_ref, *rest):
        g_ref, d_ref, nm_ref, nv_ref = rest[len(dep_ins):]
        g = p_ref[0].astype(F32)
        for k in range(1, n_parts):
            g = g + p_ref[k].astype(F32)
        nm = ADAM_B1 * m_ref[...] + (1.0 - ADAM_B1) * g
        nv = ADAM_B2 * v_ref[...] + (1.0 - ADAM_B2) * (g * g)
        g_ref[...] = g
        nm_ref[...] = nm
        nv_ref[...] = nv
        d_ref[...] = -ADAM_LR * ((nm / bc1) / (jnp.sqrt(nv / bc2) + ADAM_EPS) + ADAM_WD * w_ref[...])

    blk = pl.BlockSpec((tr, C), lambda i: (i, 0))
    return pl.pallas_call(
        body, name=name, grid=(R // tr,), in_specs=[pl.BlockSpec((n_parts, tr, C), lambda i: (0, i, 0)), blk, blk, blk] + dep_specs,
        out_specs=[blk] * 4, out_shape=[jax.ShapeDtypeStruct((R, C), F32)] * 4, compiler_params=_params())(parts, w, m, v, *dep_ins)


class _WeightStream:
    def __init__(self, cast):
        self.cast, self.handles = cast, {}

    @staticmethod
    def _tag(l, group):
        return ("ffn" if group is FFN_WEIGHTS else group[0]) + str(l)

    def start(self, l, group, after=None):
        self.handles[l, group], token = _gather_start(f"gather_{self._tag(l, group)}_start", [self.cast(l, n) for n in group], after)
        return token

    def relay(self, l, group, after):
        self.handles[l, group], token = _gather_relay(f"gather_{self._tag(l, group)}_relay", self.handles[l, group], after)
        return token

    def get(self, l, group, after):
        got = dict(zip(group, _gather_wait(f"gather_{self._tag(l, group)}_wait", self.handles[l, group], after)))
        if 'w_out' in got:
            rows, cols = got['w_out'].shape[1:]
            got['w_out'] = got['w_out'].reshape(N_DEV * rows, cols)
        return got


def _layer_fwd(l, h, p, stream, mod, rope, conv_w8, cfg):
    nm = f"l{l}_"
    mod = mod + stream.relay(l, IN_WEIGHT, h)[0, 0]
    xn = _norm_mod(nm + "norm1", h, p['norm1_g'], mod, 0)
    W = stream.get(l, IN_WEIGHT, xn)
    token = None
    if l == 0:
        token = stream.start(0, OUT_WEIGHT, after=W['w_in']) + stream.start(0, FFN_WEIGHTS, after=W['w_in'])
    P = _mm_cols(nm + "w_in", xn, W['w_in'], dep=token)
    qr, kr, vb = _qk_prep(nm + "qk_prep", P, p['q_norm_g'], p['k_norm_g'], rope[0], rope[1], cfg)
    if l == 0:
        o, lse = _attn_dense_fwd(nm + "attn", qr, kr, vb, cfg)
        aux = _conv_fwd(nm + "conv", P, conv_w8, cfg)
    else:
        o, lse = _attn_win_fwd(nm + "attn", qr, kr, vb, p['sink'], cfg)
        aux = _pool_fwd(nm + "pool", P, p['pool_w'], p['pool_scale'], cfg)
    mix = jnp.concatenate([o, aux], axis=1).astype(BF16)
    W.update(stream.get(l, OUT_WEIGHT, stream.relay(l, OUT_WEIGHT, mix)))
    y, h2 = _mm_plain(nm + "w_out", mix, W['w_out'], False, dep=stream.relay(l, FFN_WEIGHTS, o), res=(h, mod, 2, cfg['Lc']))
    hn = _norm_mod(nm + "norm2", h2, p['norm2_g'], mod, 1)
    W.update(stream.get(l, FFN_WEIGHTS, hn))
    token = stream.start(1, IN_WEIGHT, after=W['w_down']) if l == 0 else None
    silu_g, u_dsilu, A = _ffn_up(nm + "ffn_up", hn, W['w_gate'], W['w_up'], dep=token)
    if l == 0:
        token = stream.start(1, OUT_WEIGHT, after=A) + stream.start(1, FFN_WEIGHTS, after=A)
    F, h3 = _mm_shards_nn(nm + "w_down", A, W['w_down'], dep=token, res=(h2, mod, 5, cfg['Lc']))
    saved = dict(h=h, xn=xn, P=P, qr=qr, kr=kr, vb=vb, o=o, lse=lse, mix=mix, y=y, h2=h2, hn=hn, silu_g=silu_g, u_dsilu=u_dsilu, A=A, F=F)
    return h3, saved, W


def _layer_bwd(l, dh3, s, p, W, mod, rope, conv_w8, cfg, res2_bwd=None, below=None):
    nm = f"l{l}_bwd_"
    J = N_DEV
    dF, dmod = _gate_bwd(nm + "res2", dh3, s['F'], mod, 5) if res2_bwd is None else res2_bwd
    dG, dU = _ffn_dact(nm + "ffn_act", dF, W['w_down'], s['silu_g'], s['u_dsilu'])
    big = {'w_down': _wgrad_down(nm + "dw_down", s['A'], dF),
           'w_gate': _wgrad_down(nm + "dw_gate", dG, s['hn']),
           'w_up': _wgrad_down(nm + "dw_up", dU, s['hn'])}
    handles = {}
    handles['ffn'], token = _exchange_start(f"scatter_ffn{l}_start", [big[n] for n in FFN_WEIGHTS], True)
    mod = mod + token[0, 0]
    dhn = _mm_shards_nn2(nm + "dhn", dG, W['w_gate'], dU, W['w_up'])
    dh2, dm, dg2, dY, dm_gate = _norm_mod_bwd(nm + "norm2", dhn, s['h2'], p['norm2_g'], mod, 1, dh3, gate=(s['y'], mod, 2))
    dmod += dm + dm_gate
    dwo = _wgrad_rows(nm + "dw_out", s['mix'], dY)
    handles['w_out'], token = _exchange_start(f"scatter_w_out{l}_start", [dwo.reshape((J, dwo.shape[0] // J, dwo.shape[1]))], True)
    dmix = _mm_plain(nm + "dmix", dY, W['w_out'], True, dep=token)
    small = {'norm2_g': dg2[0]}
    if l == 0:
        dqr, dkr, dv = _attn_dense_bwd(nm + "attn", s['qr'], s['kr'], s['vb'], s['o'], s['lse'], dmix, cfg)
        *daux, dcw = _conv_bwd(nm + "conv", s['P'], conv_w8, dmix, cfg)
        small['conv_w'] = dcw[0:3]
    else:
        dqr, dkr, dv, dsk = _attn_win_bwd(nm + "attn", s['qr'], s['kr'], s['vb'], s['o'], s['lse'], dmix, p['sink'], cfg)
        du, dpw, dps = _pool_bwd(nm + "pool", s['P'], p['pool_w'], p['pool_scale'], dmix, cfg)
        daux = [du]
        small.update(sink=dsk[:, 0, ::HEAD_DIM].reshape(-1), pool_w=dpw, pool_scale=dps[0])
    dq, dk, dqg, dkg = _qk_prep_bwd(nm + "qk_prep", dqr, dkr, s['P'], p['q_norm_g'], p['k_norm_g'], rope[0], rope[1], cfg)
    small.update(q_norm_g=dqg[0], k_norm_g=dkg[0])
    dP = jnp.concatenate([dq, dk, dv.astype(BF16), *daux], axis=1)
    handles['w_in'], token = _exchange_start(f"scatter_w_in{l}_start", [_wgrad_cols(nm + "dw_in", s['xn'], dP, J)], True)
    mod = mod + token[0, 0]
    dxn = _mm_cols_nt(nm + "dxn", dP, W['w_in'])
    gate = None if below is None else (below[0], below[1], 5)
    dh, dm, dg1, *res2_below = _norm_mod_bwd(nm + "norm1", dxn, s['h'], p['norm1_g'], mod, 0, dh2, latent_only=(l == 0), gate=gate)
    dmod += dm
    small['norm1_g'] = dg1[0]
    return dh, dmod, small, handles, token, (tuple(res2_below) or None)


def _rope_tables(S, Lc):
    half = HEAD_DIM // 4
    pos = np.arange(S)
    inv = ROPE_THETA ** (-np.arange(0, 2 * half, 2, dtype=np.float32) / (2 * half))
    inv = jnp.asarray(inv, F32)
    ang_r = jnp.asarray(pos // GRID_W, F32)[:, None] * inv
    ang_c = jnp.asarray(pos % GRID_W, F32)[:, None] * inv
    cos = jnp.concatenate([jnp.cos(ang_r)] * 2 + [jnp.cos(ang_c)] * 2, axis=1)
    sin = jnp.concatenate([-jnp.sin(ang_r), jnp.sin(ang_r), -jnp.sin(ang_c), jnp.sin(ang_c)], axis=1)
    return (jnp.concatenate([jnp.ones((Lc, HEAD_DIM), F32), cos], axis=0),
            jnp.concatenate([jnp.zeros((Lc, HEAD_DIM), F32), sin], axis=0))


def _pad_rows(a, rows):
    return jnp.concatenate([a, jnp.zeros((rows - a.shape[0],) + a.shape[1:], a.dtype)], axis=0)


def _flat128(a, nlead):
    lead = a.shape[:nlead]
    f = a.reshape(lead + (-1,))
    pad = (-f.shape[-1]) % 128
    if pad:
        f = jnp.concatenate([f, jnp.zeros(lead + (pad,), f.dtype)], axis=-1)
    return f.reshape(lead + (-1, 128))


def _pack(named, nlead=0):
    rows, layout, at = [], {}, 0
    for name, a in named:
        f = _flat128(a, nlead)
        n = f.shape[-2]
        pad = (-n) % 8
        if pad:
            f = jnp.concatenate([f, jnp.zeros(f.shape[:-2] + (pad, 128), f.dtype)], axis=-2)
        layout[name] = (at, n, a.shape[nlead:])
        rows.append(f)
        at += n + pad
    return jnp.concatenate(rows, axis=-2), layout


def _unpack(arr, layout, name):
    at, n, shape = layout[name]
    return arr[..., at:at + n, :].reshape(arr.shape[:-2] + (-1,))[..., :math.prod(shape)].reshape(arr.shape[:-2] + tuple(shape))


def kernel(*args):
    A = dict(zip(INPUT_NAMES, args, strict=True))
    x, ctx = A['x'][0], A['ctx'][0]
    S, D = x.shape
    Lc = ctx.shape[0]
    T = Lc + S
    ATT = D - AUX_WIDTH
    KVW = (A['l1_w_in'].shape[1] * N_DEV - ATT - AUX_WIDTH) // 2
    cfg = dict(ATT=ATT, KVW=KVW, NKV=KVW // HEAD_DIM, G=ATT // KVW, Lc=Lc)
    assert Lc == ROW_TILE and S % ROW_TILE == 0 and T >= BAND and S % GRID_W == 0
    cw = A['l0_conv_w'].shape[1]
    me = 4 * lax.axis_index("x") + 2 * lax.axis_index("y") + lax.axis_index("c")

    def layer_params(l):
        pre = f"l{l}_"
        return {k[len(pre):]: (v.reshape(1, -1) if v.ndim == 1 and k != 'l1_sink' else v) for k, v in A.items() if k.startswith(pre)}

    params = [layer_params(0), layer_params(1)]

    def cast(l, n):
        w = A[f'l{l}_{n}']
        return (w.T if n in TRANSPOSED else w).astype(BF16)

    stream = _WeightStream(cast)
    token = stream.start(0, IN_WEIGHT)

    big_names = [n for n in WEIGHT_NAMES if n[3:] in BIG_WEIGHTS + ('w_mod',)]
    rest = [n for n in WEIGHT_NAMES if n not in big_names]
    early = ['x', 'ctx'] + rest + ['m_' + n for n in rest] + ['v_' + n for n in rest]
    token, held = lax.optimization_barrier((token, [A[n] for n in early]))
    A.update(zip(early, held))
    x, ctx = A['x'][0], A['ctx'][0]
    wp, layw = _pack([(n, A[n]) for n in rest])
    mp, _ = _pack([(n, A['m_' + n]) for n in rest])
    vp, _ = _pack([(n, A['v_' + n]) for n in rest])
    rope = _rope_tables(S, Lc)
    h = jnp.concatenate([ctx, x], axis=0)

    sc_own = jax.nn.silu(A['c']) + token[0, 0]
    first, lay0 = _pack([('sc', sc_own), ('conv_w', A['l0_conv_w'])])
    first, h, wp, mp, vp = lax.optimization_barrier((first, h, wp, mp, vp))
    first_all = _exchange("gather_cond", [first], False)[0]
    sc_all = _unpack(first_all, lay0, 'sc')[:, 0]
    conv_w = _unpack(first_all, lay0, 'conv_w').transpose(1, 0, 2).reshape(3, N_DEV * cw)
    conv_w8 = _pad_rows(conv_w, 8)
    sc_ctx = jax.nn.silu(A['c_ctx'])
    s16 = _pad_rows(jnp.concatenate([sc_all, sc_ctx[None]], axis=0), 16)

    nmod = A['l0_w_mod'].shape[1]
    modp = jnp.concatenate([_mm_plain(f"l{l}_mod", s16, A[f'l{l}_w_mod'], False) for l in range(2)], axis=1)
    modp_all = _exchange("gather_mod", [modp], False)[0]
    mods = []
    for l in range(2):
        full = modp_all[:, :, l * nmod:(l + 1) * nmod].transpose(1, 0, 2).reshape(16, N_MOD * D) + A[f'l{l}_b_mod'][None]
        both = jnp.stack([full[8], lax.dynamic_index_in_dim(full, me, 0, keepdims=False)]).reshape(2, N_MOD, D)
        mods.append(jnp.concatenate([both, jnp.zeros((2, 8 - N_MOD, D), F32)], axis=1))

    saved, W = [], []
    for l in range(2):
        h, s, Wl = _layer_fwd(l, h, params[l], stream, mods[l], rope, conv_w8, cfg)
        saved.append(s)
        W.append(Wl)

    dh, loss_blk, dgf = _loss_head("loss_head", h, A['final_norm_g'].reshape(1, -1), A['loss_target'][0], cfg)
    loss = lax.psum(loss_blk[0, 0], ("x", "y", "c"))

    grads, small, dmods, scatters = {}, {'final_norm_g': dgf[0]}, [None, None], [None, None]
    token, res2_bwd = jnp.zeros((8, 128), F32), None
    for l in (1, 0):
        below = (saved[0]['F'], mods[0]) if l == 1 else None
        dh, dmods[l], sm, scatters[l], token, res2_bwd = _layer_bwd(
            l, dh, saved[l], params[l], W[l], mods[l] + token[0, 0], rope, conv_w8, cfg, res2_bwd, below)
        small.update({f'l{l}_{k}': v for k, v in sm.items()})
    grad_x = dh[None]

    def landed(l, key, after):
        group = FFN_WEIGHTS if key == 'ffn' else (key,)
        for n, parts in zip(group, _exchange_wait(f"scatter_{key}{l}_wait", scatters[l][key], after)):
            shape = A[f'l{l}_{n}'].shape
            grads[f'l{l}_{n}'] = (parts.reshape((N_DEV,) + (shape[::-1] if n in TRANSPOSED else shape)), None)

    out = {}

    def adam(n, dep=None):
        w, m, v = A[n], A['m_' + n], A['v_' + n]
        if n[3:] in TRANSPOSED:
            out[n] = tuple(r.T for r in _adamw("adamw_" + n, grads[n][0], w.T, m.T, v.T, dep))
        else:
            out[n] = _adamw("adamw_" + n, grads[n][0], w, m, v, dep)
        return out[n][1]

    small_names = [n for n in WEIGHT_NAMES if n in small]
    pieces = [(n, small[n]) for n in small_names]
    for l in range(2):
        pieces += [(f'dmod{l}', dmods[l][1, :N_MOD]), (f'dcmod{l}', dmods[l][0, :N_MOD])]
    second, lay1 = _pack(pieces)
    small_handle, small_token = _exchange_start("gather_small_start", [second], False, after=token)

    last, chain = dh, small_token
    for l in (1, 0):
        for key in ('ffn', 'w_out') + (('w_in',) if l == 1 else ()):
            landed(l, key, last)
            for n in (FFN_WEIGHTS if key == 'ffn' else (key,)):
                last = adam(f'l{l}_{n}', chain)
                chain = last[:8, :128]
    second_all = _exchange_wait("gather_small_wait", small_handle, last)[0]

    dsc_part = jnp.zeros((16, D), F32)
    for l in range(2):
        dm16 = _pad_rows(jnp.concatenate([_unpack(second_all, lay1, f'dmod{l}').reshape(N_DEV, N_MOD * D),
                                          jnp.sum(_unpack(second_all, lay1, f'dcmod{l}'), axis=0).reshape(1, N_MOD * D)], axis=0), 16)
        mine = lax.dynamic_slice_in_dim(dm16, me * nmod, nmod, axis=1)
        tk = _col_tile(D, 512)
        gw = _mm_tn(f"l{l}_dw_mod", s16, mine, pl.BlockSpec((16, tk), lambda kb, t: (0, kb)), pl.BlockSpec((16, nmod), lambda kb, t: (0, 0)),
                    jax.ShapeDtypeStruct((D, nmod), F32), pl.BlockSpec((tk, nmod), lambda kb, t: (kb, 0)), (tk, nmod), (D // tk, 1))
        grads[f'l{l}_w_mod'] = (gw[None], None)
        dsc_part += _mm_plain(f"l{l}_dsc", mine, A[f'l{l}_w_mod'], True)
        dmod_dev = _unpack(second_all, lay1, f'dmod{l}') + _unpack(second_all, lay1, f'dcmod{l}')
        grads[f'l{l}_b_mod'] = (dmod_dev.reshape(N_DEV, N_MOD * D), None)
    dsig = jax.nn.sigmoid(A['c_ctx'])
    dsilu = dsig * (1 + A['c_ctx'] * (1 - dsig))
    third_all = _exchange("gather_dsc", [dsc_part[8:9]], False)[0]
    grads['c_ctx'] = (third_all[:, 0] * dsilu[None], None)
    for n in small_names:
        g8 = _unpack(second_all, lay1, n)
        if n == 'l0_conv_w':
            g8 = lax.dynamic_slice_in_dim(g8, me * cw, cw, axis=2)
        grads[n] = (g8, None)

    adam('l0_w_mod')
    last = adam('l1_w_mod')
    gp, _ = _pack([(n, grads[n][0]) for n in rest], nlead=1)
    res = _adamw("adamw_small", gp, wp, mp, vp)
    for n in rest:
        out[n] = tuple(_unpack(r, layw, n) for r in res)
    landed(0, 'w_in', last)
    adam('l0_w_in')

    outs = [loss, grad_x]
    for k in range(4):
        outs += [out[n][k] for n in WEIGHT_NAMES]
    return tuple(outs)
```

```python
import math

import numpy as np
import jax
import jax.numpy as jnp
from jax import lax
from jax.experimental import pallas as pl
from jax.experimental.pallas import tpu as pltpu

F32 = jnp.float32
BF16 = jnp.bfloat16
HEAD_DIM = 128
AUX_WIDTH = 512
AUX_GROUPS = 4
POOL_HALF = (1, 2, 4, 8)
WINDOW = 128
GRID_W = 64
ROPE_THETA = 10000.0
EPS = 1e-6
NEG_INF = -1e30
ATT_SCALE = HEAD_DIM ** -0.5
LOG2_E = math.log2(math.e)
Q_SCALE = ATT_SCALE * LOG2_E
N_MOD = 6
N_DEV = 8
ROW_TILE = 256
BAND = ROW_TILE + 2 * WINDOW
ADAM_LR, ADAM_B1, ADAM_B2, ADAM_EPS, ADAM_WD, ADAM_STEP = 0.001, 0.9, 0.999, 1e-08, 0.01, 10
VMEM_LIMIT_MB = 56
ADAM_BLOCK_ELEMS = 3 << 17
MESH = pl.DeviceIdType.MESH
RESIDENT = pl.Buffered(buffer_count=1)

WEIGHT_NAMES = ['c_ctx', 'l0_norm1_g', 'l0_w_mod', 'l0_b_mod', 'l0_w_in', 'l0_q_norm_g', 'l0_k_norm_g', 'l0_conv_w', 'l0_w_out', 'l0_norm2_g', 'l0_w_gate', 'l0_w_up', 'l0_w_down', 'l1_norm1_g', 'l1_w_mod', 'l1_b_mod', 'l1_w_in', 'l1_q_norm_g', 'l1_k_norm_g', 'l1_sink', 'l1_pool_w', 'l1_pool_scale', 'l1_w_out', 'l1_norm2_g', 'l1_w_gate', 'l1_w_up', 'l1_w_down', 'final_norm_g']
INPUT_NAMES = (['x', 'c', 'ctx'] + WEIGHT_NAMES + ['loss_target'] + ['m_' + n for n in WEIGHT_NAMES]
               + ['v_' + n for n in WEIGHT_NAMES])
IN_WEIGHT = ('w_in',)
OUT_WEIGHT = ('w_out',)
MIXER_WEIGHTS = OUT_WEIGHT + IN_WEIGHT
FFN_WEIGHTS = ('w_down', 'w_gate', 'w_up')
TRANSPOSED = ('w_gate', 'w_up')
BIG_WEIGHTS = MIXER_WEIGHTS + FFN_WEIGHTS


def _params(vmem_mb=VMEM_LIMIT_MB):
    return pltpu.CompilerParams(vmem_limit_bytes=vmem_mb << 20)


def _row_tile(n, cap):
    best = None
    for t in range(16, min(n, cap) + 1, 16):
        if n % t == 0:
            best = t
    assert best is not None, (n, cap)
    return best


def _col_tile(n, cap):
    best = n
    for t in range(128, min(n, cap) + 1, 128):
        if n % t == 0:
            best = t
    return best


def _dot(a, b, ca, cb):
    return lax.dot_general(a, b, (((ca,), (cb,)), ((), ())), preferred_element_type=F32)


def _nn(a, b):
    return _dot(a, b, 1, 0)


def _nt(a, b):
    return _dot(a, b, 1, 1)


def _tn(a, b):
    return _dot(a, b, 0, 0)


def _bf(x):
    return x.astype(BF16)


def _exchange(name, arrs, scatter, after=None):
    n = len(arrs)
    extra = [] if after is None else [after]
    if scatter:
        out_shape = [jax.ShapeDtypeStruct(a.shape, a.dtype) for a in arrs]
    else:
        out_shape = [jax.ShapeDtypeStruct((N_DEV,) + a.shape, a.dtype) for a in arrs]

    def body(*refs):
        ins, outs = refs[:n], refs[n + len(extra):2 * n + len(extra)]
        send_sems, recv_sems, local_sems = refs[2 * n + len(extra):]
        x, y, c = lax.axis_index("x"), lax.axis_index("y"), lax.axis_index("c")
        me = 4 * x + 2 * y + c
        local, remote = [], []
        for a in range(n):
            own = ins[a].at[me] if scatter else ins[a]
            cp = pltpu.make_async_copy(own, outs[a].at[me], local_sems.at[a])
            cp.start()
            local.append(cp)
            for r in range(1, N_DEV):
                px = 1 - x if r & 4 else x
                py = 1 - y if r & 2 else y
                pc = 1 - c if r & 1 else c
                src = ins[a].at[4 * px + 2 * py + pc] if scatter else ins[a]
                cp = pltpu.make_async_remote_copy(
                    src_ref=src, dst_ref=outs[a].at[me], send_sem=send_sems.at[a, r - 1],
                    recv_sem=recv_sems.at[a, r - 1], device_id=(px, py, pc), device_id_type=MESH)
                cp.start()
                remote.append(cp)
        for cp in remote:
            cp.wait()
        for cp in local:
            cp.wait()

    any_spec = pl.BlockSpec(memory_space=pl.ANY)
    return pl.pallas_call(
        body, name=name, out_shape=out_shape,
        in_specs=[any_spec] * (n + len(extra)), out_specs=[any_spec] * n,
        scratch_shapes=[pltpu.SemaphoreType.DMA((n, N_DEV - 1)), pltpu.SemaphoreType.DMA((n, N_DEV - 1)),
                        pltpu.SemaphoreType.DMA((n,))],
    )(*arrs, *extra)


HBM_SPEC = pl.BlockSpec(memory_space=pltpu.HBM)
SEM_SPEC = pl.BlockSpec(memory_space=pltpu.SEMAPHORE)
EFFECT = pltpu.SideEffectType.DATAFLOW_SIDE_EFFECTING


def _split_copies(srcs, lands, send_sems, recv_sems, local_sems, scatter):
    x, y, c = lax.axis_index("x"), lax.axis_index("y"), lax.axis_index("c")
    me = 4 * x + 2 * y + c
    local, remote = [], []
    for a in range(len(srcs)):
        own = srcs[a].at[me] if scatter else srcs[a]
        local.append(pltpu.make_async_copy(own, lands[a].at[me], local_sems.at[a]))
        for r in range(1, N_DEV):
            px = 1 - x if r & 4 else x
            py = 1 - y if r & 2 else y
            pc = 1 - c if r & 1 else c
            src = srcs[a].at[4 * px + 2 * py + pc] if scatter else srcs[a]
            remote.append(pltpu.make_async_remote_copy(
                src_ref=src, dst_ref=lands[a].at[me], send_sem=send_sems.at[a * (N_DEV - 1) + r - 1],
                recv_sem=recv_sems.at[a * (N_DEV - 1) + r - 1], device_id=(px, py, pc), device_id_type=MESH))
    return local, remote


def _exchange_start(name, arrs, scatter, after=None):
    n = len(arrs)
    extra = [] if after is None else [after]
    shapes = [a.shape if scatter else (N_DEV,) + a.shape for a in arrs]
    lands = [pltpu.with_memory_space_constraint(lax.empty(s, a.dtype), pltpu.HBM) for s, a in zip(shapes, arrs)]
    srcs = [pltpu.with_memory_space_constraint(a, pltpu.HBM) for a in arrs]

    def body(*refs):
        src_refs, land_refs = refs[:n], refs[n:2 * n]
        send_sems, recv_sems, local_sems = refs[2 * n + len(extra):2 * n + len(extra) + 3]
        token = refs[-1]
        local, remote = _split_copies(src_refs, land_refs, send_sems, recv_sems, local_sems, scatter)
        for cp in local + remote:
            cp.start()
        token[...] = jnp.zeros_like(token)

    res = pl.pallas_call(
        body, name=name,
        out_shape=[pltpu.SemaphoreType.DMA((n * (N_DEV - 1),)), pltpu.SemaphoreType.DMA((n * (N_DEV - 1),)), pltpu.SemaphoreType.DMA((n,))]
        + [pltpu.HBM(a.shape, a.dtype) for a in arrs] + [pltpu.HBM(s, a.dtype) for s, a in zip(shapes, arrs)]
        + [jax.ShapeDtypeStruct((8, 128), F32)],
        in_specs=[HBM_SPEC] * (2 * n) + [pl.BlockSpec(memory_space=pl.ANY)] * len(extra),
        out_specs=[SEM_SPEC] * 3 + [HBM_SPEC] * (2 * n) + [pl.BlockSpec(memory_space=pltpu.VMEM)],
        input_output_aliases={i: 3 + i for i in range(2 * n)},
        compiler_params=pltpu.CompilerParams(has_side_effects=EFFECT),
    )(*srcs, *lands, *extra)
    return (scatter, res[:3], res[3:3 + n], res[3 + n:3 + 2 * n]), res[-1]


def _exchange_wait(name, handle, after):
    scatter, sems, srcs, lands = handle
    n = len(srcs)

    def body(*refs):
        src_refs, land_refs = refs[:n], refs[n:2 * n]
        send_sems, recv_sems, local_sems = refs[2 * n:2 * n + 3]
        local, remote = _split_copies(src_refs, land_refs, send_sems, recv_sems, local_sems, scatter)
        for cp in remote:
            cp.wait_send()
            cp.wait_recv()
        for cp in local:
            cp.wait()

    res = pl.pallas_call(
        body, name=name,
        out_shape=[pltpu.HBM(a.shape, a.dtype) for a in srcs] + [pltpu.HBM(a.shape, a.dtype) for a in lands],
        in_specs=[HBM_SPEC] * (2 * n) + [SEM_SPEC] * 3 + [pl.BlockSpec(memory_space=pl.ANY)], out_specs=[HBM_SPEC] * (2 * n),
        input_output_aliases={i: i for i in range(2 * n)},
        compiler_params=pltpu.CompilerParams(has_side_effects=EFFECT),
    )(*srcs, *lands, *sems, after)
    return list(res[n:])


FIRST_COPIES = 4
RELAY_COPIES = 3


def _gather_copies(srcs, lands, sems):
    send_sems, recv_sems, local_sems = sems[:3]
    x, y, c = lax.axis_index("x"), lax.axis_index("y"), lax.axis_index("c")
    me = 4 * x + 2 * y + c
    chips = [(1 - x, y), (x, 1 - y), (1 - x, 1 - y)]
    local, first, relay = [], [], []
    for a in range(len(srcs)):
        local.append(pltpu.make_async_copy(srcs[a], lands[a].at[me], local_sems.at[a]))
        targets = [(x, y, 1 - c)] + [(px, py, c) for px, py in chips]
        first.append([pltpu.make_async_remote_copy(
            src_ref=srcs[a], dst_ref=lands[a].at[me], send_sem=send_sems.at[FIRST_COPIES * a + k],
            recv_sem=recv_sems.at[FIRST_COPIES * a + k], device_id=t, device_id_type=MESH) for k, t in enumerate(targets)])
        if len(sems) > 3:
            rsend, rrecv = sems[3:]
            slots = [lands[a].at[4 * px + 2 * py + c] for px, py in chips]
            relay.append([pltpu.make_async_remote_copy(
                src_ref=slot, dst_ref=slot, send_sem=rsend.at[RELAY_COPIES * a + j], recv_sem=rrecv.at[RELAY_COPIES * a + j],
                device_id=(x, y, 1 - c), device_id_type=MESH) for j, slot in enumerate(slots)])
    return local, first, relay


def _gather_start(name, arrs, after=None):
    n = len(arrs)
    extra = [] if after is None else [after]
    lands = [pltpu.with_memory_space_constraint(lax.empty((N_DEV,) + a.shape, a.dtype), pltpu.HBM) for a in arrs]
    srcs = [pltpu.with_memory_space_constraint(a, pltpu.HBM) for a in arrs]

    def body(*refs):
        at = 2 * n + len(extra)
        local, first, _ = _gather_copies(refs[:n], refs[n:2 * n], refs[at:at + 3])
        for cp in local + [cp for cps in first for cp in cps]:
            cp.start()
        refs[-1][...] = jnp.zeros_like(refs[-1])

    res = pl.pallas_call(
        body, name=name,
        out_shape=[pltpu.SemaphoreType.DMA((FIRST_COPIES * n,)), pltpu.SemaphoreType.DMA((FIRST_COPIES * n,)), pltpu.SemaphoreType.DMA((n,))]
        + [pltpu.HBM(a.shape, a.dtype) for a in arrs] + [pltpu.HBM((N_DEV,) + a.shape, a.dtype) for a in arrs]
        + [jax.ShapeDtypeStruct((8, 128), F32)],
        in_specs=[HBM_SPEC] * (2 * n) + [pl.BlockSpec(memory_space=pl.ANY)] * len(extra),
        out_specs=[SEM_SPEC] * 3 + [HBM_SPEC] * (2 * n) + [pl.BlockSpec(memory_space=pltpu.VMEM)],
        input_output_aliases={i: 3 + i for i in range(2 * n)},
        compiler_params=pltpu.CompilerParams(has_side_effects=EFFECT),
    )(*srcs, *lands, *extra)
    return (list(res[:3]), list(res[3:3 + n]), list(res[3 + n:3 + 2 * n])), res[-1]


def _gather_relay(name, handle, after):
    sems, srcs, lands = handle
    n = len(srcs)

    def body(*refs):
        in_sems = refs[2 * n:2 * n + 3]
        out_sems = refs[2 * n + 4 + 2 * n:2 * n + 4 + 2 * n + 2]
        _, first, relay = _gather_copies(refs[:n], refs[n:2 * n], list(in_sems) + list(out_sems))
        for a in range(n):
            for j in range(RELAY_COPIES):
                first[a][1 + j].wait_recv()
                relay[a][j].start()
        refs[-1][...] = jnp.zeros_like(refs[-1])

    res = pl.pallas_call(
        body, name=name,
        out_shape=[pltpu.HBM(a.shape, a.dtype) for a in srcs] + [pltpu.HBM(a.shape, a.dtype) for a in lands]
        + [pltpu.SemaphoreType.DMA((RELAY_COPIES * n,)), pltpu.SemaphoreType.DMA((RELAY_COPIES * n,)), jax.ShapeDtypeStruct((8, 128), F32)],
        in_specs=[HBM_SPEC] * (2 * n) + [SEM_SPEC] * 3 + [pl.BlockSpec(memory_space=pl.ANY)],
        out_specs=[HBM_SPEC] * (2 * n) + [SEM_SPEC] * 2 + [pl.BlockSpec(memory_space=pltpu.VMEM)],
        input_output_aliases={i: i for i in range(2 * n)},
        compiler_params=pltpu.CompilerParams(has_side_effects=EFFECT),
    )(*srcs, *lands, *sems, after)
    return (sems + list(res[2 * n:2 * n + 2]), list(res[:n]), list(res[n:2 * n])), res[-1]


def _gather_wait(name, handle, after):
    sems, srcs, lands = handle
    n = len(srcs)

    def body(*refs):
        local, first, relay = _gather_copies(refs[:n], refs[n:2 * n], refs[2 * n:2 * n + 5])
        for a in range(n):
            for cp in first[a]:
                cp.wait_send()
            first[a][0].wait_recv()
            for cp in relay[a]:
                cp.wait_send()
                cp.wait_recv()
            local[a].wait()

    res = pl.pallas_call(
        body, name=name,
        out_shape=[pltpu.HBM(a.shape, a.dtype) for a in srcs] + [pltpu.HBM(a.shape, a.dtype) for a in lands],
        in_specs=[HBM_SPEC] * (2 * n) + [SEM_SPEC] * 5 + [pl.BlockSpec(memory_space=pl.ANY)], out_specs=[HBM_SPEC] * (2 * n),
        input_output_aliases={i: i for i in range(2 * n)},
        compiler_params=pltpu.CompilerParams(has_side_effects=EFFECT),
    )(*srcs, *lands, *sems, after)
    return list(res[n:])


def _dep(dep, grid_rank):
    if dep is None:
        return [], []
    return [dep], [pl.BlockSpec((8, 128), (lambda i, j: (0, 0)) if grid_rank == 2 else (lambda i: (0, 0)))]


def _mm_step(name, fn, ins, in_specs, out_shape, out_spec, grid, dep=None, res=None):
    n = len(ins)
    dep_ins, dep_specs = _dep(dep, len(grid))
    res_ins, res_specs, out_shapes, out_specs = [], [], out_shape, out_spec
    if res is not None:
        h, mod, row_idx, lc = res
        tm, tn = out_spec.block_shape
        res_ins = [h, mod]
        res_specs = [pl.BlockSpec((tm, tn), lambda j, i: (i, j)), pl.BlockSpec((2, 8, tn), lambda j, i: (0, 0, j))]
        out_shapes, out_specs = [out_shape, jax.ShapeDtypeStruct(h.shape, h.dtype)], [out_spec, res_specs[0]]

    def body(*refs):
        outs = refs[n + len(res_ins) + len(dep_ins):]
        acc = fn(*refs[:n])
        outs[0][...] = acc.astype(outs[0].dtype)
        if res is not None:
            h_ref, mod_ref = refs[n:n + 2]
            row = pl.program_id(1) * tm + lax.broadcasted_iota(jnp.int32, (tm, 1), 0)
            gate = jnp.where(row < lc, mod_ref[0, row_idx:row_idx + 1, :], mod_ref[1, row_idx:row_idx + 1, :])
            outs[1][...] = h_ref[...] + gate * acc

    return pl.pallas_call(body, name=name, grid=grid, in_specs=list(in_specs) + res_specs + dep_specs, out_specs=out_specs,
                          out_shape=out_shapes, compiler_params=_params())(*ins, *res_ins, *dep_ins)


def _mm_tn(name, a, b, a_spec, b_spec, out_shape, out_spec, acc_shape, grid):
    nk = grid[-1]
    kax = len(grid) - 1
    if nk == 1:
        def whole(a_ref, b_ref, o_ref):
            o_ref[...] = _tn(_bf(a_ref[...]), _bf(b_ref[...])).astype(o_ref.dtype)

        return pl.pallas_call(whole, name=name, grid=grid, in_specs=[a_spec, b_spec], out_specs=out_spec,
                              out_shape=out_shape, compiler_params=_params())(a, b)

    def body(a_ref, b_ref, o_ref, acc_ref):
        k = pl.program_id(kax)

        @pl.when(k == 0)
        def _():
            acc_ref[...] = jnp.zeros_like(acc_ref)

        acc_ref[...] += _tn(_bf(a_ref[...]), _bf(b_ref[...]))

        @pl.when(k == nk - 1)
        def _():
            o_ref[...] = acc_ref[...].astype(o_ref.dtype)

    return pl.pallas_call(body, name=name, grid=grid, in_specs=[a_spec, b_spec], out_specs=out_spec,
                          out_shape=out_shape, scratch_shapes=[pltpu.VMEM(acc_shape, F32)],
                          compiler_params=_params())(a, b)


def _mm_cols(name, a, w3, out_dtype=F32, dep=None):
    M, K = a.shape
    J, _, n = w3.shape
    return _mm_step(
        name, lambda a_ref, w_ref: _nn(_bf(a_ref[...]), w_ref[...]), [a, w3],
        [pl.BlockSpec((M, K), lambda j, i: (0, 0), pipeline_mode=RESIDENT), pl.BlockSpec((None, K, n), lambda j, i: (j, 0, 0))],
        jax.ShapeDtypeStruct((M, J * n), out_dtype), pl.BlockSpec((M, n), lambda j, i: (0, j)), (J, 1), dep)


def _mm_plain(name, a, b, transpose_b, out_dtype=F32, tn=512, dep=None, res=None):
    M, K = a.shape
    N = b.shape[0] if transpose_b else b.shape[1]
    tn = _col_tile(N, tn)
    if res is None:
        tm, a_spec = M, pl.BlockSpec((M, K), lambda j, i: (0, 0), pipeline_mode=RESIDENT)
    else:
        tm = _row_tile(M, 1088)
        a_spec = pl.BlockSpec((tm, K), lambda j, i: (i, 0))
    if transpose_b:
        b_spec = pl.BlockSpec((tn, K), lambda j, i: (j, 0))
        fn = lambda a_ref, b_ref: _nt(_bf(a_ref[...]), _bf(b_ref[...]))
    else:
        b_spec = pl.BlockSpec((K, tn), lambda j, i: (0, j))
        fn = lambda a_ref, b_ref: _nn(_bf(a_ref[...]), _bf(b_ref[...]))
    return _mm_step(name, fn, [a, b], [a_spec, b_spec],
                    jax.ShapeDtypeStruct((M, N), out_dtype), pl.BlockSpec((tm, tn), lambda j, i: (i, j)),
                    (N // tn, M // tm), dep, res)


def _mm_shards_nn(name, a3, w3, tn=512, dep=None, res=None):
    J, M, k = a3.shape
    N = w3.shape[2]
    tm = _row_tile(M, 544)
    tn = _col_tile(N, tn)

    def fn(a_ref, w_ref):
        acc = _nn(a_ref[0], w_ref[0])
        for j in range(1, J):
            acc += _nn(a_ref[j], w_ref[j])
        return acc

    return _mm_step(name, fn, [a3, w3],
                    [pl.BlockSpec((J, tm, k), lambda jn, i: (0, i, 0)), pl.BlockSpec((J, k, tn), lambda jn, i: (0, 0, jn))],
                    jax.ShapeDtypeStruct((M, N), F32), pl.BlockSpec((tm, tn), lambda jn, i: (i, jn)), (N // tn, M // tm), dep, res)


def _mm_shards_nn2(name, a3, w3a, b3, w3b, tn=512):
    J, M, k = a3.shape
    N = w3a.shape[2]
    tm = _row_tile(M, 544)
    tn = _col_tile(N, tn)

    def fn(a_ref, wa_ref, b_ref, wb_ref):
        acc = _nn(a_ref[0], wa_ref[0]) + _nn(b_ref[0], wb_ref[0])
        for j in range(1, J):
            acc += _nn(a_ref[j], wa_ref[j]) + _nn(b_ref[j], wb_ref[j])
        return acc

    act = pl.BlockSpec((J, tm, k), lambda jn, i: (0, i, 0))
    wsp = pl.BlockSpec((J, k, tn), lambda jn, i: (0, 0, jn))
    return _mm_step(name, fn, [a3, w3a, b3, w3b], [act, wsp, act, wsp],
                    jax.ShapeDtypeStruct((M, N), F32), pl.BlockSpec((tm, tn), lambda jn, i: (i, jn)), (N // tn, M // tm))


def _mm_cols_nt(name, a, w3, tn=512):
    M = a.shape[0]
    J, N, n = w3.shape
    tm = _row_tile(M, 544)
    tn = _col_tile(N, tn)

    def fn(a_ref, w_ref):
        acc = _nt(a_ref[:, 0:n], w_ref[0])
        for j in range(1, J):
            acc += _nt(a_ref[:, j * n:(j + 1) * n], w_ref[j])
        return acc

    return _mm_step(name, fn, [a, w3],
                    [pl.BlockSpec((tm, J * n), lambda jn, i: (i, 0)), pl.BlockSpec((J, tn, n), lambda jn, i: (0, jn, 0))],
                    jax.ShapeDtypeStruct((M, N), F32), pl.BlockSpec((tm, tn), lambda jn, i: (i, jn)), (N // tn, M // tm))


def _wgrad_cols(name, a, b, J):
    T, K = a.shape
    n = b.shape[1] // J
    return _mm_tn(name, a, b, pl.BlockSpec((T, K), lambda j, t: (0, 0), pipeline_mode=RESIDENT), pl.BlockSpec((T, n), lambda j, t: (0, j)),
                  jax.ShapeDtypeStruct((J, K, n), BF16), pl.BlockSpec((None, K, n), lambda j, t: (j, 0, 0)), (K, n), (J, 1))


def _wgrad_rows(name, a, b, tk=512):
    T, K = a.shape
    N = b.shape[1]
    tk = _col_tile(K, tk)
    return _mm_tn(name, a, b, pl.BlockSpec((T, tk), lambda kb, t: (0, kb)), pl.BlockSpec((T, N), lambda kb, t: (0, 0), pipeline_mode=RESIDENT),
                  jax.ShapeDtypeStruct((K, N), BF16), pl.BlockSpec((tk, N), lambda kb, t: (kb, 0)), (tk, N), (K // tk, 1))


def _wgrad_down(name, a3, b):
    J, T, k = a3.shape
    N = b.shape[1]
    return _mm_tn(name, a3, b, pl.BlockSpec((None, T, k), lambda j, t: (j, 0, 0)),
                  pl.BlockSpec((T, N), lambda j, t: (0, 0), pipeline_mode=RESIDENT),
                  jax.ShapeDtypeStruct((J, k, N), BF16), pl.BlockSpec((None, k, N), lambda j, t: (j, 0, 0)), (k, N), (J, 1))


def _seg(i):
    return jnp.minimum(i, 1)


def _rstd(x):
    return lax.rsqrt(jnp.mean(x * x, axis=-1, keepdims=True) + EPS)


def _norm_mod(name, h, g, mod, which):
    T, D = h.shape

    def body(h_ref, g_ref, mod_ref, o_ref):
        x = h_ref[...]
        n = x * _rstd(x) * g_ref[...]
        shift = mod_ref[3 * which:3 * which + 1, :]
        scale = mod_ref[3 * which + 1:3 * which + 2, :]
        o_ref[...] = (n * (1 + scale) + shift).astype(o_ref.dtype)

    row = pl.BlockSpec((ROW_TILE, D), lambda i: (i, 0))
    return pl.pallas_call(
        body, name=name, grid=(T // ROW_TILE,),
        in_specs=[row, pl.BlockSpec((1, D), lambda i: (0, 0)), pl.BlockSpec((None, 8, D), lambda i: (_seg(i), 0, 0))],
        out_specs=row, out_shape=jax.ShapeDtypeStruct((T, D), BF16), compiler_params=_params())(h, g, mod)


def _norm_mod_bwd(name, dxn, h, g, mod, which, dres, latent_only=False, gate=None):
    T, D = h.shape
    n_gate = 0 if gate is None else 2

    def body(dxn_ref, h_ref, g_ref, mod_ref, dres_ref, *rest):
        dh_ref, dmod_ref, dg_ref = rest[n_gate:n_gate + 3]
        i = pl.program_id(0)
        x = h_ref[...]
        r = _rstd(x)
        xhat = x * r
        g = g_ref[...]
        n = xhat * g
        scale = mod_ref[3 * which + 1:3 * which + 2, :]
        dxn = dxn_ref[...]
        dn = dxn * (1 + scale)
        dxh = dn * g
        dh = dres_ref[...] + r * (dxh - xhat * jnp.mean(dxh * xhat, axis=-1, keepdims=True))
        if latent_only:
            @pl.when(i > 0)
            def _():
                dh_ref[...] = dh
        else:
            dh_ref[...] = dh

        @pl.when(i <= 1)
        def _():
            dmod_ref[...] = jnp.zeros_like(dmod_ref)

        @pl.when(i == 0)
        def _():
            dg_ref[...] = jnp.zeros_like(dg_ref)

        dmod_ref[3 * which:3 * which + 1, :] += jnp.sum(dxn, axis=0, keepdims=True)
        dmod_ref[3 * which + 1:3 * which + 2, :] += jnp.sum(dxn * n, axis=0, keepdims=True)
        dg_ref[0:1, :] += jnp.sum(dn * xhat, axis=0, keepdims=True)

        if gate is not None:
            y_ref, gmod_ref = rest[:2]
            dy_ref, dgmod_ref = rest[5:7]
            dy_ref[...] = (dh * gmod_ref[gate[2]:gate[2] + 1, :]).astype(dy_ref.dtype)

            @pl.when(i <= 1)
            def _():
                dgmod_ref[...] = jnp.zeros_like(dgmod_ref)

            dgmod_ref[gate[2]:gate[2] + 1, :] += jnp.sum(dh * y_ref[...], axis=0, keepdims=True)

    row = pl.BlockSpec((ROW_TILE, D), lambda i: (i, 0))
    modspec = pl.BlockSpec((None, 8, D), lambda i: (_seg(i), 0, 0))
    dh_rows = T - ROW_TILE if latent_only else T
    dh_spec = pl.BlockSpec((ROW_TILE, D), lambda i: (jnp.maximum(i - 1, 0), 0)) if latent_only else row
    gate_ins, gate_specs, gate_outs, gate_shapes = [], [], [], []
    if gate is not None:
        gate_ins, gate_specs = [gate[0], gate[1]], [row, modspec]
        gate_outs, gate_shapes = [row, modspec], [jax.ShapeDtypeStruct((T, D), BF16), jax.ShapeDtypeStruct((2, 8, D), F32)]
    return pl.pallas_call(
        body, name=name, grid=(T // ROW_TILE,),
        in_specs=[row, row, pl.BlockSpec((1, D), lambda i: (0, 0)), modspec, row] + gate_specs,
        out_specs=[dh_spec, modspec, pl.BlockSpec((8, D), lambda i: (0, 0))] + gate_outs,
        out_shape=[jax.ShapeDtypeStruct((dh_rows, D), F32), jax.ShapeDtypeStruct((2, 8, D), F32), jax.ShapeDtypeStruct((8, D), F32)] + gate_shapes,
        compiler_params=_params())(dxn, h, g, mod, dres, *gate_ins)


def _gate_bwd(name, dh, y, mod, row_idx):
    T, D = dh.shape

    def body(dh_ref, y_ref, mod_ref, dy_ref, dmod_ref):
        i = pl.program_id(0)
        dh = dh_ref[...]
        dy_ref[...] = (dh * mod_ref[row_idx:row_idx + 1, :]).astype(dy_ref.dtype)

        @pl.when(i <= 1)
        def _():
            dmod_ref[...] = jnp.zeros_like(dmod_ref)

        dmod_ref[row_idx:row_idx + 1, :] += jnp.sum(dh * y_ref[...], axis=0, keepdims=True)

    row = pl.BlockSpec((ROW_TILE, D), lambda i: (i, 0))
    modspec = pl.BlockSpec((None, 8, D), lambda i: (_seg(i), 0, 0))
    return pl.pallas_call(
        body, name=name, grid=(T // ROW_TILE,), in_specs=[row, row, modspec], out_specs=[row, modspec],
        out_shape=[jax.ShapeDtypeStruct((T, D), BF16), jax.ShapeDtypeStruct((2, 8, D), F32)],
        compiler_params=_params())(dh, y, mod)


def _rot(y):
    lane = lax.broadcasted_iota(jnp.int32, y.shape, 1)
    return jnp.where((lane & 32) == 0, pltpu.roll(y, 96, 1), pltpu.roll(y, 32, 1))


def _qk_prep(name, P, q_g, k_g, rope_c, rope_s, cfg):
    T = P.shape[0]
    ATT, KVW = cfg['ATT'], cfg['KVW']

    def body(q_ref, k_ref, v_ref, qg_ref, kg_ref, c_ref, s_ref, qo_ref, ko_ref, vo_ref):
        cc, ss = c_ref[...], s_ref[...]

        def head(x, g):
            y = x * _rstd(x) * g
            return y * cc + _rot(y) * ss

        for hh in range(ATT // HEAD_DIM):
            sl = slice(hh * HEAD_DIM, (hh + 1) * HEAD_DIM)
            qo_ref[:, sl] = (head(q_ref[:, sl], qg_ref[...]) * Q_SCALE).astype(qo_ref.dtype)
        for hh in range(KVW // HEAD_DIM):
            sl = slice(hh * HEAD_DIM, (hh + 1) * HEAD_DIM)
            ko_ref[:, sl] = head(k_ref[:, sl], kg_ref[...]).astype(ko_ref.dtype)
        vo_ref[...] = v_ref[...].astype(vo_ref.dtype)

    kb = ATT // KVW
    gain = pl.BlockSpec((1, HEAD_DIM), lambda i: (0, 0))
    tab = pl.BlockSpec((ROW_TILE, HEAD_DIM), lambda i: (i, 0))
    qs = pl.BlockSpec((ROW_TILE, ATT), lambda i: (i, 0))
    ks = pl.BlockSpec((ROW_TILE, KVW), lambda i: (i, 0))
    return pl.pallas_call(
        body, name=name, grid=(T // ROW_TILE,),
        in_specs=[qs, pl.BlockSpec((ROW_TILE, KVW), lambda i: (i, kb)), pl.BlockSpec((ROW_TILE, KVW), lambda i: (i, kb + 1)),
                  gain, gain, tab, tab],
        out_specs=[qs, ks, ks],
        out_shape=[jax.ShapeDtypeStruct((T, ATT), BF16), jax.ShapeDtypeStruct((T, KVW), BF16), jax.ShapeDtypeStruct((T, KVW), BF16)],
        compiler_params=_params())(P, P, P, q_g, k_g, rope_c, rope_s)


def _qk_prep_bwd(name, dqr, dkr, P, q_g, k_g, rope_c, rope_s, cfg):
    T = P.shape[0]
    ATT, KVW = cfg['ATT'], cfg['KVW']

    def body(dq_ref, dk_ref, q_ref, k_ref, qg_ref, kg_ref, c_ref, s_ref, dqo_ref, dko_ref, dqg_ref, dkg_ref):
        i = pl.program_id(0)
        cc, ss = c_ref[...], s_ref[...]

        @pl.when(i == 0)
        def _():
            dqg_ref[...] = jnp.zeros_like(dqg_ref)
            dkg_ref[...] = jnp.zeros_like(dkg_ref)

        def head(x, g, dout):
            dy = dout * cc + _rot(dout * ss)
            r = _rstd(x)
            xhat = x * r
            dxh = dy * g
            dx = r * (dxh - xhat * jnp.mean(dxh * xhat, axis=-1, keepdims=True))
            return dx, jnp.sum(dy * xhat, axis=0, keepdims=True)

        dg = jnp.zeros((1, HEAD_DIM), F32)
        for hh in range(ATT // HEAD_DIM):
            sl = slice(hh * HEAD_DIM, (hh + 1) * HEAD_DIM)
            dx, d = head(q_ref[:, sl], qg_ref[...], dq_ref[:, sl] * ATT_SCALE)
            dqo_ref[:, sl] = dx.astype(dqo_ref.dtype)
            dg += d
        dqg_ref[0:1, :] += dg
        dg = jnp.zeros((1, HEAD_DIM), F32)
        for hh in range(KVW // HEAD_DIM):
            sl = slice(hh * HEAD_DIM, (hh + 1) * HEAD_DIM)
            dx, d = head(k_ref[:, sl], kg_ref[...], dk_ref[:, sl] * (1.0 / LOG2_E))
            dko_ref[:, sl] = dx.astype(dko_ref.dtype)
            dg += d
        dkg_ref[0:1, :] += dg

    kb = ATT // KVW
    gain = pl.BlockSpec((1, HEAD_DIM), lambda i: (0, 0))
    dgain = pl.BlockSpec((8, HEAD_DIM), lambda i: (0, 0))
    tab = pl.BlockSpec((ROW_TILE, HEAD_DIM), lambda i: (i, 0))
    qs = pl.BlockSpec((ROW_TILE, ATT), lambda i: (i, 0))
    ks = pl.BlockSpec((ROW_TILE, KVW), lambda i: (i, 0))
    return pl.pallas_call(
        body, name=name, grid=(T // ROW_TILE,),
        in_specs=[qs, ks, qs, pl.BlockSpec((ROW_TILE, KVW), lambda i: (i, kb)), gain, gain, tab, tab],
        out_specs=[qs, ks, dgain, dgain],
        out_shape=[jax.ShapeDtypeStruct((T, ATT), BF16), jax.ShapeDtypeStruct((T, KVW), BF16),
                   jax.ShapeDtypeStruct((8, HEAD_DIM), F32), jax.ShapeDtypeStruct((8, HEAD_DIM), F32)],
        compiler_params=_params())(dqr, dkr, P, P, q_g, k_g, rope_c, rope_s)


def _att_specs(T, G):
    qs = pl.BlockSpec((ROW_TILE, G * HEAD_DIM), lambda h, i: (i, h))
    kvs = pl.BlockSpec((T, HEAD_DIM), lambda h, i: (0, h))
    return qs, kvs


def _attn_dense_fwd(name, q, k, v, cfg):
    T, G, Lc = q.shape[0], cfg['G'], cfg['Lc']

    def body(q_ref, k_ref, v_ref, o_ref, lse_ref):
        def attend(rows):
            kk, vv = k_ref[0:rows, :], v_ref[0:rows, :]
            for g in range(G):
                sl = slice(g * HEAD_DIM, (g + 1) * HEAD_DIM)
                s = _nt(q_ref[:, sl], kk)
                m = jnp.max(s, axis=1, keepdims=True)
                p = jnp.exp2(s - m)
                l = jnp.sum(p, axis=1, keepdims=True)
                o_ref[:, sl] = _nn(_bf(p), vv) / l
                lse_ref[:, sl] = jnp.broadcast_to(m + jnp.log2(l), (ROW_TILE, HEAD_DIM))

        @pl.when(pl.program_id(1) == 0)
        def _():
            attend(Lc)

        @pl.when(pl.program_id(1) > 0)
        def _():
            attend(T)

    qs, kvs = _att_specs(T, G)
    return pl.pallas_call(
        body, name=name, grid=(cfg['NKV'], T // ROW_TILE), in_specs=[qs, kvs, kvs], out_specs=[qs, qs],
        out_shape=[jax.ShapeDtypeStruct(q.shape, F32), jax.ShapeDtypeStruct(q.shape, F32)],
        compiler_params=_params())(q, k, v)


def _attn_dense_bwd(name, q, k, v, o, lse, dmix, cfg):
    T, G, Lc = q.shape[0], cfg['G'], cfg['Lc']

    def body(q_ref, k_ref, v_ref, o_ref, lse_ref, do_ref, dq_ref, dk_ref, dv_ref):
        i = pl.program_id(1)

        @pl.when(i == 0)
        def _():
            dk_ref[...] = jnp.zeros_like(dk_ref)
            dv_ref[...] = jnp.zeros_like(dv_ref)

        def attend(rows):
            kk, vv = k_ref[0:rows, :], v_ref[0:rows, :]
            for g in range(G):
                sl = slice(g * HEAD_DIM, (g + 1) * HEAD_DIM)
                qg, do = q_ref[:, sl], do_ref[:, sl]
                delta = jnp.sum(do * o_ref[:, sl], axis=1, keepdims=True)
                p = jnp.exp2(_nt(qg, kk) - lse_ref[:, g * HEAD_DIM:g * HEAD_DIM + 1])
                dob = _bf(do)
                dv_ref[0:rows, :] += _tn(_bf(p), dob)
                ds = _bf(p * (_nt(dob, vv) - delta))
                dq_ref[:, sl] = _nn(ds, kk)
                dk_ref[0:rows, :] += _tn(ds, qg)

        @pl.when(i == 0)
        def _():
            attend(Lc)

        @pl.when(i > 0)
        def _():
            attend(T)

    qs, kvs = _att_specs(T, G)
    return pl.pallas_call(
        body, name=name, grid=(cfg['NKV'], T // ROW_TILE), in_specs=[qs, kvs, kvs, qs, qs, qs], out_specs=[qs, kvs, kvs],
        out_shape=[jax.ShapeDtypeStruct(q.shape, F32), jax.ShapeDtypeStruct(k.shape, F32), jax.ShapeDtypeStruct(k.shape, F32)],
        compiler_params=_params())(q, k, v, o, lse, dmix)


def _band(i, T, Lc):
    start = pl.multiple_of(jnp.clip(WINDOW + (i - 1) * ROW_TILE, 0, T - BAND), WINDOW)
    qpos = (i - 1) * ROW_TILE + lax.broadcasted_iota(jnp.int32, (ROW_TILE, 1), 0)
    kpos = start - Lc + lax.broadcasted_iota(jnp.int32, (1, BAND), 1)
    ok = (jnp.abs(kpos - qpos) <= WINDOW) & (kpos >= 0) & (i > 0)
    return start, jnp.where(ok, 0.0, NEG_INF).astype(F32)


def _attn_win_fwd(name, q, k, v, sink, cfg):
    T, G, Lc = q.shape[0], cfg['G'], cfg['Lc']

    def body(sink_ref, q_ref, k_ref, v_ref, o_ref, lse_ref):
        h, i = pl.program_id(0), pl.program_id(1)
        start, bias = _band(i, T, Lc)
        kc, vc = k_ref[0:Lc, :], v_ref[0:Lc, :]
        kb, vb = k_ref[pl.ds(start, BAND), :], v_ref[pl.ds(start, BAND), :]
        for g in range(G):
            sl = slice(g * HEAD_DIM, (g + 1) * HEAD_DIM)
            qg = q_ref[:, sl]
            sk = sink_ref[h * G + g] * LOG2_E
            sc = _nt(qg, kc)
            sb = _nt(qg, kb) + bias
            m = jnp.maximum(jnp.maximum(jnp.max(sc, axis=1, keepdims=True), jnp.max(sb, axis=1, keepdims=True)), sk)
            pc, pb = jnp.exp2(sc - m), jnp.exp2(sb - m)
            l = jnp.sum(pc, axis=1, keepdims=True) + jnp.sum(pb, axis=1, keepdims=True) + jnp.exp2(sk - m)
            o_ref[:, sl] = (_nn(_bf(pc), vc) + _nn(_bf(pb), vb)) / l
            lse_ref[:, sl] = jnp.broadcast_to(m + jnp.log2(l), (ROW_TILE, HEAD_DIM))

    qs, kvs = _att_specs(T, G)
    return pl.pallas_call(
        body, name=name, grid=(cfg['NKV'], T // ROW_TILE),
        in_specs=[pl.BlockSpec(memory_space=pltpu.SMEM), qs, kvs, kvs], out_specs=[qs, qs],
        out_shape=[jax.ShapeDtypeStruct(q.shape, F32), jax.ShapeDtypeStruct(q.shape, F32)],
        compiler_params=_params())(sink, q, k, v)


def _attn_win_bwd(name, q, k, v, o, lse, dmix, sink, cfg):
    T, G, Lc = q.shape[0], cfg['G'], cfg['Lc']

    def body(sink_ref, q_ref, k_ref, v_ref, o_ref, lse_ref, do_ref, dq_ref, dk_ref, dv_ref, dsink_ref):
        h, i = pl.program_id(0), pl.program_id(1)
        start, bias = _band(i, T, Lc)
        kc, vc = k_ref[0:Lc, :], v_ref[0:Lc, :]
        kb, vb = k_ref[pl.ds(start, BAND), :], v_ref[pl.ds(start, BAND), :]

        @pl.when(i == 0)
        def _():
            dk_ref[...] = jnp.zeros_like(dk_ref)
            dv_ref[...] = jnp.zeros_like(dv_ref)
            dsink_ref[...] = jnp.zeros_like(dsink_ref)

        for g in range(G):
            sl = slice(g * HEAD_DIM, (g + 1) * HEAD_DIM)
            qg, do = q_ref[:, sl], do_ref[:, sl]
            lse = lse_ref[:, g * HEAD_DIM:g * HEAD_DIM + 1]
            delta = jnp.sum(do * o_ref[:, sl], axis=1, keepdims=True)
            pc = jnp.exp2(_nt(qg, kc) - lse)
            pb = jnp.exp2(_nt(qg, kb) + bias - lse)
            ps = jnp.exp2(sink_ref[h * G + g] * LOG2_E - lse)
            dob = _bf(do)
            dv_ref[0:Lc, :] += _tn(_bf(pc), dob)
            dv_ref[pl.ds(start, BAND), :] += _tn(_bf(pb), dob)
            dsc = _bf(pc * (_nt(dob, vc) - delta))
            dsb = _bf(pb * (_nt(dob, vb) - delta))
            dq_ref[:, sl] = _nn(dsc, kc) + _nn(dsb, kb)
            dk_ref[0:Lc, :] += _tn(dsc, qg)
            dk_ref[pl.ds(start, BAND), :] += _tn(dsb, qg)
            dsk = jnp.where(i > 0, -jnp.sum(ps * delta, axis=0, keepdims=True), 0.0)
            dsink_ref[:, sl] += jnp.broadcast_to(dsk, (8, HEAD_DIM))

    qs, kvs = _att_specs(T, G)
    return pl.pallas_call(
        body, name=name, grid=(cfg['NKV'], T // ROW_TILE),
        in_specs=[pl.BlockSpec(memory_space=pltpu.SMEM), qs, kvs, kvs, qs, qs, qs],
        out_specs=[qs, kvs, kvs, pl.BlockSpec((None, 8, G * HEAD_DIM), lambda h, i: (h, 0, 0))],
        out_shape=[jax.ShapeDtypeStruct(q.shape, F32), jax.ShapeDtypeStruct(k.shape, F32), jax.ShapeDtypeStruct(k.shape, F32),
                   jax.ShapeDtypeStruct((cfg['NKV'], 8, G * HEAD_DIM), F32)],
        compiler_params=_params())(sink, q, k, v, o, lse, dmix)


def _seq_pos(T, Lc):
    row = lax.broadcasted_iota(jnp.int32, (T, 1), 0)
    return jnp.where(row < Lc, row, row - Lc), jnp.where(row < Lc, Lc, T - Lc)


def _fw(x, k, pos, seglen):
    return jnp.where(pos + k < seglen, pltpu.roll(x, x.shape[0] - k, 0), 0.0)


def _bw(x, k, pos):
    return jnp.where(pos - k >= 0, pltpu.roll(x, k, 0), 0.0)


def _conv_fwd(name, P, conv_w8, cfg):
    T, Lc = P.shape[0], cfg['Lc']
    cb = (cfg['ATT'] + 2 * cfg['KVW']) // HEAD_DIM
    na = AUX_WIDTH // HEAD_DIM

    def body(gb_ref, gc_ref, u_ref, w_ref, o_ref):
        pos, seglen = _seq_pos(T, Lc)
        z = gc_ref[...] * u_ref[...]
        conv = w_ref[0:1, :] * _bw(z, 1, pos) + w_ref[1:2, :] * z + w_ref[2:3, :] * _fw(z, 1, pos, seglen)
        o_ref[...] = gb_ref[...] * conv

    col = lambda off: pl.BlockSpec((T, HEAD_DIM), lambda c: (0, cb + off + c))
    return pl.pallas_call(
        body, name=name, grid=(na,),
        in_specs=[col(0), col(na), col(2 * na), pl.BlockSpec((8, HEAD_DIM), lambda c: (0, c))],
        out_specs=pl.BlockSpec((T, HEAD_DIM), lambda c: (0, c)),
        out_shape=jax.ShapeDtypeStruct((T, AUX_WIDTH), F32), compiler_params=_params())(P, P, P, conv_w8)


def _conv_bwd(name, P, conv_w8, dmix, cfg):
    T, Lc = P.shape[0], cfg['Lc']
    cb = (cfg['ATT'] + 2 * cfg['KVW']) // HEAD_DIM
    ob = cfg['ATT'] // HEAD_DIM
    na = AUX_WIDTH // HEAD_DIM

    def body(gb_ref, gc_ref, u_ref, w_ref, do_ref, dgb_ref, dgc_ref, du_ref, dw_ref):
        pos, seglen = _seq_pos(T, Lc)
        gc, u, do = gc_ref[...], u_ref[...], do_ref[...]
        z = gc * u
        zm, zp = _bw(z, 1, pos), _fw(z, 1, pos, seglen)
        w0, w1, w2 = w_ref[0:1, :], w_ref[1:2, :], w_ref[2:3, :]
        dgb_ref[...] = (do * (w0 * zm + w1 * z + w2 * zp)).astype(dgb_ref.dtype)
        dc = do * gb_ref[...]
        dz = w0 * _fw(dc, 1, pos, seglen) + w1 * dc + w2 * _bw(dc, 1, pos)
        dgc_ref[...] = (dz * u).astype(dgc_ref.dtype)
        du_ref[...] = (dz * gc).astype(du_ref.dtype)
        dw_ref[...] = jnp.zeros_like(dw_ref)
        dw_ref[0:1, :] = jnp.sum(dc * zm, axis=0, keepdims=True)
        dw_ref[1:2, :] = jnp.sum(dc * z, axis=0, keepdims=True)
        dw_ref[2:3, :] = jnp.sum(dc * zp, axis=0, keepdims=True)

    col = lambda off: pl.BlockSpec((T, HEAD_DIM), lambda c: (0, cb + off + c))
    wspec = pl.BlockSpec((8, HEAD_DIM), lambda c: (0, c))
    ocol = lambda off: pl.BlockSpec((T, HEAD_DIM), lambda c: (0, off + c))
    return pl.pallas_call(
        body, name=name, grid=(na,),
        in_specs=[col(0), col(na), col(2 * na), wspec, ocol(ob)],
        out_specs=[ocol(0), ocol(0), ocol(0), wspec],
        out_shape=[jax.ShapeDtypeStruct((T, AUX_WIDTH), BF16)] * 3 + [jax.ShapeDtypeStruct((8, AUX_WIDTH), F32)],
        compiler_params=_params())(P, P, P, conv_w8, dmix)


def _window_sums(x, half, pos, seglen):
    fwd, bwd = x, x
    s = 1
    while s < half:
        fwd = fwd + _fw(fwd, s, pos, seglen)
        bwd = bwd + _bw(bwd, s, pos)
        s *= 2
    return fwd, bwd


def _pooled(u, half, pos, seglen):
    fwd, bwd = _window_sums(u, half, pos, seglen)
    cnt = (jnp.minimum(pos + half, seglen) - jnp.maximum(pos - half, 0)).astype(F32)
    return (fwd + _bw(bwd, 1, pos)) / cnt - u, cnt


def _pool_fwd(name, P, pool_w, pool_scale, cfg):
    T, Lc = P.shape[0], cfg['Lc']
    cb = (cfg['ATT'] + 2 * cfg['KVW']) // HEAD_DIM

    def body(u_ref, w_ref, s_ref, o_ref):
        g = pl.program_id(0)
        pos, seglen = _seq_pos(T, Lc)
        for k, half in enumerate(POOL_HALF):
            @pl.when(g == k)
            def _(half=half):
                pooled, _ = _pooled(u_ref[...], half, pos, seglen)
                o_ref[...] = _nn(_bf(pooled), _bf(w_ref[...])) * s_ref[...]

    return pl.pallas_call(
        body, name=name, grid=(AUX_GROUPS,),
        in_specs=[pl.BlockSpec((T, HEAD_DIM), lambda g: (0, cb + g)), pl.BlockSpec((None, HEAD_DIM, HEAD_DIM), lambda g: (g, 0, 0)),
                  pl.BlockSpec((1, HEAD_DIM), lambda g: (0, g))],
        out_specs=pl.BlockSpec((T, HEAD_DIM), lambda g: (0, g)),
        out_shape=jax.ShapeDtypeStruct((T, AUX_WIDTH), F32), compiler_params=_params())(P, pool_w, pool_scale)


def _pool_bwd(name, P, pool_w, pool_scale, dmix, cfg):
    T, Lc = P.shape[0], cfg['Lc']
    cb = (cfg['ATT'] + 2 * cfg['KVW']) // HEAD_DIM
    ob = cfg['ATT'] // HEAD_DIM

    def body(u_ref, w_ref, s_ref, do_ref, du_ref, dw_ref, ds_ref):
        g = pl.program_id(0)
        pos, seglen = _seq_pos(T, Lc)
        for k, half in enumerate(POOL_HALF):
            @pl.when(g == k)
            def _(half=half):
                do = do_ref[...]
                pooled, cnt = _pooled(u_ref[...], half, pos, seglen)
                wb = _bf(w_ref[...])
                mixed = _nn(_bf(pooled), wb)
                ds_ref[...] = jnp.broadcast_to(jnp.sum(do * mixed, axis=0, keepdims=True), ds_ref.shape)
                dmixed = _bf(do * s_ref[...])
                dw_ref[...] = _tn(_bf(pooled), dmixed)
                dpooled = _nt(dmixed, wb)
                e = dpooled / cnt
                fwd, bwd = _window_sums(e, half, pos, seglen)
                adj = fwd + _fw(e, half, pos, seglen) + _bw(bwd, 1, pos) - _bw(e, half, pos)
                du_ref[...] = (adj - dpooled).astype(du_ref.dtype)

    wspec = pl.BlockSpec((None, HEAD_DIM, HEAD_DIM), lambda g: (g, 0, 0))
    return pl.pallas_call(
        body, name=name, grid=(AUX_GROUPS,),
        in_specs=[pl.BlockSpec((T, HEAD_DIM), lambda g: (0, cb + g)), wspec, pl.BlockSpec((1, HEAD_DIM), lambda g: (0, g)),
                  pl.BlockSpec((T, HEAD_DIM), lambda g: (0, ob + g))],
        out_specs=[pl.BlockSpec((T, HEAD_DIM), lambda g: (0, g)), wspec, pl.BlockSpec((8, HEAD_DIM), lambda g: (0, g))],
        out_shape=[jax.ShapeDtypeStruct((T, AUX_WIDTH), BF16), jax.ShapeDtypeStruct(pool_w.shape, F32),
                   jax.ShapeDtypeStruct((8, AUX_WIDTH), F32)],
        compiler_params=_params())(P, pool_w, pool_scale, dmix)


def _ffn_up(name, hn, wg3, wu3, dep=None):
    T, D = hn.shape
    J, k, _ = wg3.shape
    tm = _row_tile(T, 1088)
    dep_ins, dep_specs = _dep(dep, 2)

    def body(x_ref, wg_ref, wu_ref, *rest):
        s_ref, ud_ref, a_ref = rest[len(dep_ins):]
        x = x_ref[pl.ds(pl.multiple_of(pl.program_id(1) * tm, 16), tm), :]
        g, u = _nt(x, wg_ref[...]), _nt(x, wu_ref[...])
        sig = jax.nn.sigmoid(g)
        silu = g * sig
        s_ref[...] = silu.astype(s_ref.dtype)
        ud_ref[...] = (u * (sig * (1 + g * (1 - sig)))).astype(ud_ref.dtype)
        a_ref[...] = (silu * u).astype(a_ref.dtype)

    wspec = pl.BlockSpec((None, k, D), lambda j, i: (j, 0, 0))
    ospec = pl.BlockSpec((None, tm, k), lambda j, i: (j, i, 0))
    return pl.pallas_call(
        body, name=name, grid=(J, T // tm),
        in_specs=[pl.BlockSpec((T, D), lambda j, i: (0, 0), pipeline_mode=RESIDENT), wspec, wspec] + dep_specs,
        out_specs=[ospec, ospec, ospec],
        out_shape=[jax.ShapeDtypeStruct((J, T, k), BF16)] * 3,
        compiler_params=_params())(hn, wg3, wu3, *dep_ins)


def _ffn_dact(name, dF, wd3, silu_g, u_dsilu):
    T, D = dF.shape
    J, k, _ = wd3.shape
    tm = _row_tile(T, 1088)

    def body(df_ref, wd_ref, s_ref, ud_ref, dg_ref, du_ref):
        rows = pl.ds(pl.multiple_of(pl.program_id(1) * tm, 16), tm)
        da = _nt(df_ref[rows, :], wd_ref[...])
        du_ref[...] = (da * s_ref[...].astype(F32)).astype(du_ref.dtype)
        dg_ref[...] = (da * ud_ref[...].astype(F32)).astype(dg_ref.dtype)

    aspec = pl.BlockSpec((None, tm, k), lambda j, i: (j, i, 0))
    return pl.pallas_call(
        body, name=name, grid=(J, T // tm),
        in_specs=[pl.BlockSpec((T, D), lambda j, i: (0, 0), pipeline_mode=RESIDENT), pl.BlockSpec((None, k, D), lambda j, i: (j, 0, 0)), aspec, aspec],
        out_specs=[aspec, aspec],
        out_shape=[jax.ShapeDtypeStruct((J, T, k), BF16), jax.ShapeDtypeStruct((J, T, k), BF16)],
        compiler_params=_params())(dF, wd3, silu_g, u_dsilu)


def _loss_head(name, h, g, target, cfg):
    T, D = h.shape

    def body(h_ref, g_ref, t_ref, dh_ref, loss_ref, dg_ref):
        i = pl.program_id(0)

        @pl.when(i == 0)
        def _():
            dh_ref[...] = jnp.zeros_like(dh_ref)
            loss_ref[...] = jnp.zeros_like(loss_ref)
            dg_ref[...] = jnp.zeros_like(dg_ref)

        @pl.when(i > 0)
        def _():
            x = h_ref[...]
            r = _rstd(x)
            xhat = x * r
            gg = g_ref[...]
            err = xhat * gg - t_ref[...]
            loss_ref[...] += 0.5 * jnp.sum(jnp.sum(err * err, axis=1, keepdims=True) / D, axis=0, keepdims=True)
            dy = err / D
            dg_ref[0:1, :] += jnp.sum(dy * xhat, axis=0, keepdims=True)
            dxh = dy * gg
            dh_ref[...] = r * (dxh - xhat * jnp.mean(dxh * xhat, axis=-1, keepdims=True))

    row = pl.BlockSpec((ROW_TILE, D), lambda i: (i, 0))
    return pl.pallas_call(
        body, name=name, grid=(T // ROW_TILE,),
        in_specs=[row, pl.BlockSpec((1, D), lambda i: (0, 0)), pl.BlockSpec((ROW_TILE, D), lambda i: (jnp.maximum(i - 1, 0), 0))],
        out_specs=[row, pl.BlockSpec((8, 128), lambda i: (0, 0)), pl.BlockSpec((8, D), lambda i: (0, 0))],
        out_shape=[jax.ShapeDtypeStruct((T, D), F32), jax.ShapeDtypeStruct((8, 128), F32), jax.ShapeDtypeStruct((8, D), F32)],
        compiler_params=_params())(h, g, target)


def _adamw(name, parts, w, m, v, dep=None):
    R, C = w.shape
    n_parts = parts.shape[0]
    tr = _row_tile(R, max(16, ADAM_BLOCK_ELEMS // C)) if R % 16 == 0 else R
    bc1 = 1.0 - ADAM_B1 ** ADAM_STEP
    bc2 = 1.0 - ADAM_B2 ** ADAM_STEP
    dep_ins, dep_specs = _dep(dep, 1)

    def body(p_ref, w_ref, m_ref, v_ref, *rest):
        g_ref, d_ref, nm_ref, nv_ref = rest[len(dep_ins):]
        g = p_ref[0].astype(F32)
        for k in range(1, n_parts):
            g = g + p_ref[k].astype(F32)
        nm = ADAM_B1 * m_ref[...] + (1.0 - ADAM_B1) * g
        nv = ADAM_B2 * v_ref[...] + (1.0 - ADAM_B2) * (g * g)
        g_ref[...] = g
        nm_ref[...] = nm
        nv_ref[...] = nv
        d_ref[...] = -ADAM_LR * ((nm / bc1) / (jnp.sqrt(nv / bc2) + ADAM_EPS) + ADAM_WD * w_ref[...])

    blk = pl.BlockSpec((tr, C), lambda i: (i, 0))
    return pl.pallas_call(
        body, name=name, grid=(R // tr,), in_specs=[pl.BlockSpec((n_parts, tr, C), lambda i: (0, i, 0)), blk, blk, blk] + dep_specs,
        out_specs=[blk] * 4, out_shape=[jax.ShapeDtypeStruct((R, C), F32)] * 4, compiler_params=_params())(parts, w, m, v, *dep_ins)


class _WeightStream:
    def __init__(self, cast):
        self.cast, self.handles = cast, {}

    @staticmethod
    def _tag(l, group):
        return ("ffn" if group is FFN_WEIGHTS else group[0]) + str(l)

    def start(self, l, group, after=None):
        self.handles[l, group], token = _gather_start(f"gather_{self._tag(l, group)}_start", [self.cast(l, n) for n in group], after)
        return token

    def relay(self, l, group, after):
        self.handles[l, group], token = _gather_relay(f"gather_{self._tag(l, group)}_relay", self.handles[l, group], after)
        return token

    def get(self, l, group, after):
        got = dict(zip(group, _gather_wait(f"gather_{self._tag(l, group)}_wait", self.handles[l, group], after)))
        if 'w_out' in got:
            rows, cols = got['w_out'].shape[1:]
            got['w_out'] = got['w_out'].reshape(N_DEV * rows, cols)
        return got


def _layer_fwd(l, h, p, stream, mod, rope, conv_w8, cfg):
    nm = f"l{l}_"
    mod = mod + stream.relay(l, IN_WEIGHT, h)[0, 0]
    xn = _norm_mod(nm + "norm1", h, p['norm1_g'], mod, 0)
    W = stream.get(l, IN_WEIGHT, xn)
    token = None
    if l == 0:
        token = stream.start(0, OUT_WEIGHT, after=W['w_in']) + stream.start(0, FFN_WEIGHTS, after=W['w_in'])
    P = _mm_cols(nm + "w_in", xn, W['w_in'], dep=token)
    qr, kr, vb = _qk_prep(nm + "qk_prep", P, p['q_norm_g'], p['k_norm_g'], rope[0], rope[1], cfg)
    if l == 0:
        o, lse = _attn_dense_fwd(nm + "attn", qr, kr, vb, cfg)
        aux = _conv_fwd(nm + "conv", P, conv_w8, cfg)
    else:
        o, lse = _attn_win_fwd(nm + "attn", qr, kr, vb, p['sink'], cfg)
        aux = _pool_fwd(nm + "pool", P, p['pool_w'], p['pool_scale'], cfg)
    mix = jnp.concatenate([o, aux], axis=1).astype(BF16)
    W.update(stream.get(l, OUT_WEIGHT, stream.relay(l, OUT_WEIGHT, mix)))
    y, h2 = _mm_plain(nm + "w_out", mix, W['w_out'], False, dep=stream.relay(l, FFN_WEIGHTS, o), res=(h, mod, 2, cfg['Lc']))
    hn = _norm_mod(nm + "norm2", h2, p['norm2_g'], mod, 1)
    W.update(stream.get(l, FFN_WEIGHTS, hn))
    token = stream.start(1, IN_WEIGHT, after=W['w_down']) if l == 0 else None
    silu_g, u_dsilu, A = _ffn_up(nm + "ffn_up", hn, W['w_gate'], W['w_up'], dep=token)
    if l == 0:
        token = stream.start(1, OUT_WEIGHT, after=A) + stream.start(1, FFN_WEIGHTS, after=A)
    F, h3 = _mm_shards_nn(nm + "w_down", A, W['w_down'], dep=token, res=(h2, mod, 5, cfg['Lc']))
    saved = dict(h=h, xn=xn, P=P, qr=qr, kr=kr, vb=vb, o=o, lse=lse, mix=mix, y=y, h2=h2, hn=hn, silu_g=silu_g, u_dsilu=u_dsilu, A=A, F=F)
    return h3, saved, W


def _layer_bwd(l, dh3, s, p, W, mod, rope, conv_w8, cfg, res2_bwd=None, below=None):
    nm = f"l{l}_bwd_"
    J = N_DEV
    dF, dmod = _gate_bwd(nm + "res2", dh3, s['F'], mod, 5) if res2_bwd is None else res2_bwd
    dG, dU = _ffn_dact(nm + "ffn_act", dF, W['w_down'], s['silu_g'], s['u_dsilu'])
    big = {'w_down': _wgrad_down(nm + "dw_down", s['A'], dF),
           'w_gate': _wgrad_down(nm + "dw_gate", dG, s['hn']),
           'w_up': _wgrad_down(nm + "dw_up", dU, s['hn'])}
    handles = {}
    handles['ffn'], token = _exchange_start(f"scatter_ffn{l}_start", [big[n] for n in FFN_WEIGHTS], True)
    mod = mod + token[0, 0]
    dhn = _mm_shards_nn2(nm + "dhn", dG, W['w_gate'], dU, W['w_up'])
    dh2, dm, dg2, dY, dm_gate = _norm_mod_bwd(nm + "norm2", dhn, s['h2'], p['norm2_g'], mod, 1, dh3, gate=(s['y'], mod, 2))
    dmod += dm + dm_gate
    dwo = _wgrad_rows(nm + "dw_out", s['mix'], dY)
    handles['w_out'], token = _exchange_start(f"scatter_w_out{l}_start", [dwo.reshape((J, dwo.shape[0] // J, dwo.shape[1]))], True)
    dmix = _mm_plain(nm + "dmix", dY, W['w_out'], True, dep=token)
    small = {'norm2_g': dg2[0]}
    if l == 0:
        dqr, dkr, dv = _attn_dense_bwd(nm + "attn", s['qr'], s['kr'], s['vb'], s['o'], s['lse'], dmix, cfg)
        *daux, dcw = _conv_bwd(nm + "conv", s['P'], conv_w8, dmix, cfg)
        small['conv_w'] = dcw[0:3]
    else:
        dqr, dkr, dv, dsk = _attn_win_bwd(nm + "attn", s['qr'], s['kr'], s['vb'], s['o'], s['lse'], dmix, p['sink'], cfg)
        du, dpw, dps = _pool_bwd(nm + "pool", s['P'], p['pool_w'], p['pool_scale'], dmix, cfg)
        daux = [du]
        small.update(sink=dsk[:, 0, ::HEAD_DIM].reshape(-1), pool_w=dpw, pool_scale=dps[0])
    dq, dk, dqg, dkg = _qk_prep_bwd(nm + "qk_prep", dqr, dkr, s['P'], p['q_norm_g'], p['k_norm_g'], rope[0], rope[1], cfg)
    small.update(q_norm_g=dqg[0], k_norm_g=dkg[0])
    dP = jnp.concatenate([dq, dk, dv.astype(BF16), *daux], axis=1)
    handles['w_in'], token = _exchange_start(f"scatter_w_in{l}_start", [_wgrad_cols(nm + "dw_in", s['xn'], dP, J)], True)
    mod = mod + token[0, 0]
    dxn = _mm_cols_nt(nm + "dxn", dP, W['w_in'])
    gate = None if below is None else (below[0], below[1], 5)
    dh, dm, dg1, *res2_below = _norm_mod_bwd(nm + "norm1", dxn, s['h'], p['norm1_g'], mod, 0, dh2, latent_only=(l == 0), gate=gate)
    dmod += dm
    small['norm1_g'] = dg1[0]
    return dh, dmod, small, handles, token, (tuple(res2_below) or None)


def _rope_tables(S, Lc):
    half = HEAD_DIM // 4
    pos = np.arange(S)
    inv = ROPE_THETA ** (-np.arange(0, 2 * half, 2, dtype=np.float32) / (2 * half))
    inv = jnp.asarray(inv, F32)
    ang_r = jnp.asarray(pos // GRID_W, F32)[:, None] * inv
    ang_c = jnp.asarray(pos % GRID_W, F32)[:, None] * inv
    cos = jnp.concatenate([jnp.cos(ang_r)] * 2 + [jnp.cos(ang_c)] * 2, axis=1)
    sin = jnp.concatenate([-jnp.sin(ang_r), jnp.sin(ang_r), -jnp.sin(ang_c), jnp.sin(ang_c)], axis=1)
    return (jnp.concatenate([jnp.ones((Lc, HEAD_DIM), F32), cos], axis=0),
            jnp.concatenate([jnp.zeros((Lc, HEAD_DIM), F32), sin], axis=0))


def _pad_rows(a, rows):
    return jnp.concatenate([a, jnp.zeros((rows - a.shape[0],) + a.shape[1:], a.dtype)], axis=0)


def _flat128(a, nlead):
    lead = a.shape[:nlead]
    f = a.reshape(lead + (-1,))
    pad = (-f.shape[-1]) % 128
    if pad:
        f = jnp.concatenate([f, jnp.zeros(lead + (pad,), f.dtype)], axis=-1)
    return f.reshape(lead + (-1, 128))


def _pack(named, nlead=0):
    rows, layout, at = [], {}, 0
    for name, a in named:
        f = _flat128(a, nlead)
        n = f.shape[-2]
        pad = (-n) % 8
        if pad:
            f = jnp.concatenate([f, jnp.zeros(f.shape[:-2] + (pad, 128), f.dtype)], axis=-2)
        layout[name] = (at, n, a.shape[nlead:])
        rows.append(f)
        at += n + pad
    return jnp.concatenate(rows, axis=-2), layout


def _unpack(arr, layout, name):
    at, n, shape = layout[name]
    return arr[..., at:at + n, :].reshape(arr.shape[:-2] + (-1,))[..., :math.prod(shape)].reshape(arr.shape[:-2] + tuple(shape))


def kernel(*args):
    A = dict(zip(INPUT_NAMES, args, strict=True))
    x, ctx = A['x'][0], A['ctx'][0]
    S, D = x.shape
    Lc = ctx.shape[0]
    T = Lc + S
    ATT = D - AUX_WIDTH
    KVW = (A['l1_w_in'].shape[1] * N_DEV - ATT - AUX_WIDTH) // 2
    cfg = dict(ATT=ATT, KVW=KVW, NKV=KVW // HEAD_DIM, G=ATT // KVW, Lc=Lc)
    assert Lc == ROW_TILE and S % ROW_TILE == 0 and T >= BAND and S % GRID_W == 0
    cw = A['l0_conv_w'].shape[1]
    me = 4 * lax.axis_index("x") + 2 * lax.axis_index("y") + lax.axis_index("c")

    def layer_params(l):
        pre = f"l{l}_"
        return {k[len(pre):]: (v.reshape(1, -1) if v.ndim == 1 and k != 'l1_sink' else v) for k, v in A.items() if k.startswith(pre)}

    params = [layer_params(0), layer_params(1)]

    def cast(l, n):
        w = A[f'l{l}_{n}']
        return (w.T if n in TRANSPOSED else w).astype(BF16)

    stream = _WeightStream(cast)
    token = stream.start(0, IN_WEIGHT)

    big_names = [n for n in WEIGHT_NAMES if n[3:] in BIG_WEIGHTS + ('w_mod',)]
    rest = [n for n in WEIGHT_NAMES if n not in big_names]
    early = ['x', 'ctx'] + rest + ['m_' + n for n in rest] + ['v_' + n for n in rest]
    token, held = lax.optimization_barrier((token, [A[n] for n in early]))
    A.update(zip(early, held))
    x, ctx = A['x'][0], A['ctx'][0]
    wp, layw = _pack([(n, A[n]) for n in rest])
    mp, _ = _pack([(n, A['m_' + n]) for n in rest])
    vp, _ = _pack([(n, A['v_' + n]) for n in rest])
    rope = _rope_tables(S, Lc)
    h = jnp.concatenate([ctx, x], axis=0)

    sc_own = jax.nn.silu(A['c']) + token[0, 0]
    first, lay0 = _pack([('sc', sc_own), ('conv_w', A['l0_conv_w'])])
    first, h, wp, mp, vp = lax.optimization_barrier((first, h, wp, mp, vp))
    first_all = _exchange("gather_cond", [first], False)[0]
    sc_all = _unpack(first_all, lay0, 'sc')[:, 0]
    conv_w = _unpack(first_all, lay0, 'conv_w').transpose(1, 0, 2).reshape(3, N_DEV * cw)
    conv_w8 = _pad_rows(conv_w, 8)
    sc_ctx = jax.nn.silu(A['c_ctx'])
    s16 = _pad_rows(jnp.concatenate([sc_all, sc_ctx[None]], axis=0), 16)

    nmod = A['l0_w_mod'].shape[1]
    modp = jnp.concatenate([_mm_plain(f"l{l}_mod", s16, A[f'l{l}_w_mod'], False) for l in range(2)], axis=1)
    modp_all = _exchange("gather_mod", [modp], False)[0]
    mods = []
    for l in range(2):
        full = modp_all[:, :, l * nmod:(l + 1) * nmod].transpose(1, 0, 2).reshape(16, N_MOD * D) + A[f'l{l}_b_mod'][None]
        both = jnp.stack([full[8], lax.dynamic_index_in_dim(full, me, 0, keepdims=False)]).reshape(2, N_MOD, D)
        mods.append(jnp.concatenate([both, jnp.zeros((2, 8 - N_MOD, D), F32)], axis=1))

    saved, W = [], []
    for l in range(2):
        h, s, Wl = _layer_fwd(l, h, params[l], stream, mods[l], rope, conv_w8, cfg)
        saved.append(s)
        W.append(Wl)

    dh, loss_blk, dgf = _loss_head("loss_head", h, A['final_norm_g'].reshape(1, -1), A['loss_target'][0], cfg)
    loss = lax.psum(loss_blk[0, 0], ("x", "y", "c"))

    grads, small, dmods, scatters = {}, {'final_norm_g': dgf[0]}, [None, None], [None, None]
    token, res2_bwd = jnp.zeros((8, 128), F32), None
    for l in (1, 0):
        below = (saved[0]['F'], mods[0]) if l == 1 else None
        dh, dmods[l], sm, scatters[l], token, res2_bwd = _layer_bwd(
            l, dh, saved[l], params[l], W[l], mods[l] + token[0, 0], rope, conv_w8, cfg, res2_bwd, below)
        small.update({f'l{l}_{k}': v for k, v in sm.items()})
    grad_x = dh[None]

    def landed(l, key, after):
        group = FFN_WEIGHTS if key == 'ffn' else (key,)
        for n, parts in zip(group, _exchange_wait(f"scatter_{key}{l}_wait", scatters[l][key], after)):
            shape = A[f'l{l}_{n}'].shape
            grads[f'l{l}_{n}'] = (parts.reshape((N_DEV,) + (shape[::-1] if n in TRANSPOSED else shape)), None)

    out = {}

    def adam(n, dep=None):
        w, m, v = A[n], A['m_' + n], A['v_' + n]
        if n[3:] in TRANSPOSED:
            res = _adamw("adamw_" + n, grads[n][0], w.T, m.T, v.T, dep)
            out[n] = tuple(r.T for r in res)
        else:
            res = out[n] = _adamw("adamw_" + n, grads[n][0], w, m, v, dep)
        return res[1]

    small_names = [n for n in WEIGHT_NAMES if n in small]
    pieces = [(n, small[n]) for n in small_names]
    for l in range(2):
        pieces += [(f'dmod{l}', dmods[l][1, :N_MOD]), (f'dcmod{l}', dmods[l][0, :N_MOD])]
    second, lay1 = _pack(pieces)
    small_handle, small_token = _exchange_start("gather_small_start", [second], False, after=token)

    last, chain = dh, small_token
    for l in (1, 0):
        for key in ('ffn', 'w_out') + (('w_in',) if l == 1 else ()):
            landed(l, key, last)
            for n in (FFN_WEIGHTS if key == 'ffn' else (key,)):
                last = adam(f'l{l}_{n}', chain)
                chain = last[:8, :128]
    second_all = _exchange_wait("gather_small_wait", small_handle, last)[0]

    dsc_part = jnp.zeros((16, D), F32)
    for l in range(2):
        dm16 = _pad_rows(jnp.concatenate([_unpack(second_all, lay1, f'dmod{l}').reshape(N_DEV, N_MOD * D),
                                          jnp.sum(_unpack(second_all, lay1, f'dcmod{l}'), axis=0).reshape(1, N_MOD * D)], axis=0), 16)
        mine = lax.dynamic_slice_in_dim(dm16, me * nmod, nmod, axis=1)
        tk = _col_tile(D, 512)
        gw = _mm_tn(f"l{l}_dw_mod", s16, mine, pl.BlockSpec((16, tk), lambda kb, t: (0, kb)), pl.BlockSpec((16, nmod), lambda kb, t: (0, 0)),
                    jax.ShapeDtypeStruct((D, nmod), F32), pl.BlockSpec((tk, nmod), lambda kb, t: (kb, 0)), (tk, nmod), (D // tk, 1))
        grads[f'l{l}_w_mod'] = (gw[None], None)
        dsc_part += _mm_plain(f"l{l}_dsc", mine, A[f'l{l}_w_mod'], True)
        dmod_dev = _unpack(second_all, lay1, f'dmod{l}') + _unpack(second_all, lay1, f'dcmod{l}')
        grads[f'l{l}_b_mod'] = (dmod_dev.reshape(N_DEV, N_MOD * D), None)
    dsig = jax.nn.sigmoid(A['c_ctx'])
    dsilu = dsig * (1 + A['c_ctx'] * (1 - dsig))
    third_all = _exchange("gather_dsc", [dsc_part[8:9]], False)[0]
    grads['c_ctx'] = (third_all[:, 0] * dsilu[None], None)
    for n in small_names:
        g8 = _unpack(second_all, lay1, n)
        if n == 'l0_conv_w':
            g8 = lax.dynamic_slice_in_dim(g8, me * cw, cw, axis=2)
        grads[n] = (g8, None)

    adam('l0_w_mod')
    last = adam('l1_w_mod')
    gp, _ = _pack([(n, grads[n][0]) for n in rest], nlead=1)
    res = _adamw("adamw_small", gp, wp, mp, vp)
    for n in rest:
        out[n] = tuple(_unpack(r, layw, n) for r in res)
    landed(0, 'w_in', last)
    adam('l0_w_in')

    outs = [loss, grad_x]
    for k in range(4):
        outs += [out[n][k] for n in WEIGHT_NAMES]
    return tuple(outs)
```

```python
import math

import numpy as np
import jax
import jax.numpy as jnp
from jax import lax
from jax.experimental import pallas as pl
from jax.experimental.pallas import tpu as pltpu

F32 = jnp.float32
BF16 = jnp.bfloat16
HEAD_DIM = 128
AUX_WIDTH = 512
AUX_GROUPS = 4
POOL_HALF = (1, 2, 4, 8)
WINDOW = 128
GRID_W = 64
ROPE_THETA = 10000.0
EPS = 1e-6
NEG_INF = -1e30
ATT_SCALE = HEAD_DIM ** -0.5
LOG2_E = math.log2(math.e)
Q_SCALE = ATT_SCALE * LOG2_E
N_MOD = 6
N_DEV = 8
ROW_TILE = 256
BAND = ROW_TILE + 2 * WINDOW
ADAM_LR, ADAM_B1, ADAM_B2, ADAM_EPS, ADAM_WD, ADAM_STEP = 0.001, 0.9, 0.999, 1e-08, 0.01, 10
VMEM_LIMIT_MB = 56
ADAM_BLOCK_ELEMS = 3 << 17
MESH = pl.DeviceIdType.MESH
RESIDENT = pl.Buffered(buffer_count=1)

WEIGHT_NAMES = ['c_ctx', 'l0_norm1_g', 'l0_w_mod', 'l0_b_mod', 'l0_w_in', 'l0_q_norm_g', 'l0_k_norm_g', 'l0_conv_w', 'l0_w_out', 'l0_norm2_g', 'l0_w_gate', 'l0_w_up', 'l0_w_down', 'l1_norm1_g', 'l1_w_mod', 'l1_b_mod', 'l1_w_in', 'l1_q_norm_g', 'l1_k_norm_g', 'l1_sink', 'l1_pool_w', 'l1_pool_scale', 'l1_w_out', 'l1_norm2_g', 'l1_w_gate', 'l1_w_up', 'l1_w_down', 'final_norm_g']
INPUT_NAMES = (['x', 'c', 'ctx'] + WEIGHT_NAMES + ['loss_target'] + ['m_' + n for n in WEIGHT_NAMES]
               + ['v_' + n for n in WEIGHT_NAMES])
IN_WEIGHT = ('w_in',)
OUT_WEIGHT = ('w_out',)
MIXER_WEIGHTS = OUT_WEIGHT + IN_WEIGHT
FFN_WEIGHTS = ('w_down', 'w_gate', 'w_up')
TRANSPOSED = ('w_gate', 'w_up')
BIG_WEIGHTS = MIXER_WEIGHTS + FFN_WEIGHTS


def _params(vmem_mb=VMEM_LIMIT_MB):
    return pltpu.CompilerParams(vmem_limit_bytes=vmem_mb << 20)


def _row_tile(n, cap):
    best = None
    for t in range(16, min(n, cap) + 1, 16):
        if n % t == 0:
            best = t
    assert best is not None, (n, cap)
    return best


def _col_tile(n, cap):
    best = n
    for t in range(128, min(n, cap) + 1, 128):
        if n % t == 0:
            best = t
    return best


def _dot(a, b, ca, cb):
    return lax.dot_general(a, b, (((ca,), (cb,)), ((), ())), preferred_element_type=F32)


def _nn(a, b):
    return _dot(a, b, 1, 0)


def _nt(a, b):
    return _dot(a, b, 1, 1)


def _tn(a, b):
    return _dot(a, b, 0, 0)


def _bf(x):
    return x.astype(BF16)


def _exchange(name, arrs, scatter, after=None):
    n = len(arrs)
    extra = [] if after is None else [after]
    if scatter:
        out_shape = [jax.ShapeDtypeStruct(a.shape, a.dtype) for a in arrs]
    else:
        out_shape = [jax.ShapeDtypeStruct((N_DEV,) + a.shape, a.dtype) for a in arrs]

    def body(*refs):
        ins, outs = refs[:n], refs[n + len(extra):2 * n + len(extra)]
        send_sems, recv_sems, local_sems = refs[2 * n + len(extra):]
        x, y, c = lax.axis_index("x"), lax.axis_index("y"), lax.axis_index("c")
        me = 4 * x + 2 * y + c
        local, remote = [], []
        for a in range(n):
            own = ins[a].at[me] if scatter else ins[a]
            cp = pltpu.make_async_copy(own, outs[a].at[me], local_sems.at[a])
            cp.start()
            local.append(cp)
            for r in range(1, N_DEV):
                px = 1 - x if r & 4 else x
                py = 1 - y if r & 2 else y
                pc = 1 - c if r & 1 else c
                src = ins[a].at[4 * px + 2 * py + pc] if scatter else ins[a]
                cp = pltpu.make_async_remote_copy(
                    src_ref=src, dst_ref=outs[a].at[me], send_sem=send_sems.at[a, r - 1],
                    recv_sem=recv_sems.at[a, r - 1], device_id=(px, py, pc), device_id_type=MESH)
                cp.start()
                remote.append(cp)
        for cp in remote:
            cp.wait()
        for cp in local:
            cp.wait()

    any_spec = pl.BlockSpec(memory_space=pl.ANY)
    return pl.pallas_call(
        body, name=name, out_shape=out_shape,
        in_specs=[any_spec] * (n + len(extra)), out_specs=[any_spec] * n,
        scratch_shapes=[pltpu.SemaphoreType.DMA((n, N_DEV - 1)), pltpu.SemaphoreType.DMA((n, N_DEV - 1)),
                        pltpu.SemaphoreType.DMA((n,))],
    )(*arrs, *extra)


HBM_SPEC = pl.BlockSpec(memory_space=pltpu.HBM)
SEM_SPEC = pl.BlockSpec(memory_space=pltpu.SEMAPHORE)
EFFECT = pltpu.SideEffectType.DATAFLOW_SIDE_EFFECTING


def _split_copies(srcs, lands, send_sems, recv_sems, local_sems, scatter):
    x, y, c = lax.axis_index("x"), lax.axis_index("y"), lax.axis_index("c")
    me = 4 * x + 2 * y + c
    local, remote = [], []
    for a in range(len(srcs)):
        own = srcs[a].at[me] if scatter else srcs[a]
        local.append(pltpu.make_async_copy(own, lands[a].at[me], local_sems.at[a]))
        for r in range(1, N_DEV):
            px = 1 - x if r & 4 else x
            py = 1 - y if r & 2 else y
            pc = 1 - c if r & 1 else c
            src = srcs[a].at[4 * px + 2 * py + pc] if scatter else srcs[a]
            remote.append(pltpu.make_async_remote_copy(
                src_ref=src, dst_ref=lands[a].at[me], send_sem=send_sems.at[a * (N_DEV - 1) + r - 1],
                recv_sem=recv_sems.at[a * (N_DEV - 1) + r - 1], device_id=(px, py, pc), device_id_type=MESH))
    return local, remote


def _exchange_start(name, arrs, scatter, after=None):
    n = len(arrs)
    extra = [] if after is None else [after]
    shapes = [a.shape if scatter else (N_DEV,) + a.shape for a in arrs]
    lands = [pltpu.with_memory_space_constraint(lax.empty(s, a.dtype), pltpu.HBM) for s, a in zip(shapes, arrs)]
    srcs = [pltpu.with_memory_space_constraint(a, pltpu.HBM) for a in arrs]

    def body(*refs):
        src_refs, land_refs = refs[:n], refs[n:2 * n]
        send_sems, recv_sems, local_sems = refs[2 * n + len(extra):2 * n + len(extra) + 3]
        token = refs[-1]
        local, remote = _split_copies(src_refs, land_refs, send_sems, recv_sems, local_sems, scatter)
        for cp in local + remote:
            cp.start()
        token[...] = jnp.zeros_like(token)

    res = pl.pallas_call(
        body, name=name,
        out_shape=[pltpu.SemaphoreType.DMA((n * (N_DEV - 1),)), pltpu.SemaphoreType.DMA((n * (N_DEV - 1),)), pltpu.SemaphoreType.DMA((n,))]
        + [pltpu.HBM(a.shape, a.dtype) for a in arrs] + [pltpu.HBM(s, a.dtype) for s, a in zip(shapes, arrs)]
        + [jax.ShapeDtypeStruct((8, 128), F32)],
        in_specs=[HBM_SPEC] * (2 * n) + [pl.BlockSpec(memory_space=pl.ANY)] * len(extra),
        out_specs=[SEM_SPEC] * 3 + [HBM_SPEC] * (2 * n) + [pl.BlockSpec(memory_space=pltpu.VMEM)],
        input_output_aliases={i: 3 + i for i in range(2 * n)},
        compiler_params=pltpu.CompilerParams(has_side_effects=EFFECT),
    )(*srcs, *lands, *extra)
    return (scatter, res[:3], res[3:3 + n], res[3 + n:3 + 2 * n]), res[-1]


def _exchange_wait(name, handle, after):
    scatter, sems, srcs, lands = handle
    n = len(srcs)

    def body(*refs):
        src_refs, land_refs = refs[:n], refs[n:2 * n]
        send_sems, recv_sems, local_sems = refs[2 * n:2 * n + 3]
        local, remote = _split_copies(src_refs, land_refs, send_sems, recv_sems, local_sems, scatter)
        for cp in remote:
            cp.wait_send()
            cp.wait_recv()
        for cp in local:
            cp.wait()

    res = pl.pallas_call(
        body, name=name,
        out_shape=[pltpu.HBM(a.shape, a.dtype) for a in srcs] + [pltpu.HBM(a.shape, a.dtype) for a in lands],
        in_specs=[HBM_SPEC] * (2 * n) + [SEM_SPEC] * 3 + [pl.BlockSpec(memory_space=pl.ANY)], out_specs=[HBM_SPEC] * (2 * n),
        input_output_aliases={i: i for i in range(2 * n)},
        compiler_params=pltpu.CompilerParams(has_side_effects=EFFECT),
    )(*srcs, *lands, *sems, after)
    return list(res[n:])


FIRST_COPIES = 4
RELAY_COPIES = 3


def _gather_copies(srcs, lands, sems):
    send_sems, recv_sems, local_sems = sems[:3]
    x, y, c = lax.axis_index("x"), lax.axis_index("y"), lax.axis_index("c")
    me = 4 * x + 2 * y + c
    chips = [(1 - x, y), (x, 1 - y), (1 - x, 1 - y)]
    local, first, relay = [], [], []
    for a in range(len(srcs)):
        local.append(pltpu.make_async_copy(srcs[a], lands[a].at[me], local_sems.at[a]))
        targets = [(x, y, 1 - c)] + [(px, py, c) for px, py in chips]
        first.append([pltpu.make_async_remote_copy(
            src_ref=srcs[a], dst_ref=lands[a].at[me], send_sem=send_sems.at[FIRST_COPIES * a + k],
            recv_sem=recv_sems.at[FIRST_COPIES * a + k], device_id=t, device_id_type=MESH) for k, t in enumerate(targets)])
        if len(sems) > 3:
            rsend, rrecv = sems[3:]
            slots = [lands[a].at[4 * px + 2 * py + c] for px, py in chips]
            relay.append([pltpu.make_async_remote_copy(
                src_ref=slot, dst_ref=slot, send_sem=rsend.at[RELAY_COPIES * a + j], recv_sem=rrecv.at[RELAY_COPIES * a + j],
                device_id=(x, y, 1 - c), device_id_type=MESH) for j, slot in enumerate(slots)])
    return local, first, relay


def _gather_start(name, arrs, after=None):
    n = len(arrs)
    extra = [] if after is None else [after]
    lands = [pltpu.with_memory_space_constraint(lax.empty((N_DEV,) + a.shape, a.dtype), pltpu.HBM) for a in arrs]
    srcs = [pltpu.with_memory_space_constraint(a, pltpu.HBM) for a in arrs]

    def body(*refs):
        at = 2 * n + len(extra)
        local, first, _ = _gather_copies(refs[:n], refs[n:2 * n], refs[at:at + 3])
        for cp in local + [cp for cps in first for cp in cps]:
            cp.start()
        refs[-1][...] = jnp.zeros_like(refs[-1])

    res = pl.pallas_call(
        body, name=name,
        out_shape=[pltpu.SemaphoreType.DMA((FIRST_COPIES * n,)), pltpu.SemaphoreType.DMA((FIRST_COPIES * n,)), pltpu.SemaphoreType.DMA((n,))]
        + [pltpu.HBM(a.shape, a.dtype) for a in arrs] + [pltpu.HBM((N_DEV,) + a.shape, a.dtype) for a in arrs]
        + [jax.ShapeDtypeStruct((8, 128), F32)],
        in_specs=[HBM_SPEC] * (2 * n) + [pl.BlockSpec(memory_space=pl.ANY)] * len(extra),
        out_specs=[SEM_SPEC] * 3 + [HBM_SPEC] * (2 * n) + [pl.BlockSpec(memory_space=pltpu.VMEM)],
        input_output_aliases={i: 3 + i for i in range(2 * n)},
        compiler_params=pltpu.CompilerParams(has_side_effects=EFFECT),
    )(*srcs, *lands, *extra)
    return (list(res[:3]), list(res[3:3 + n]), list(res[3 + n:3 + 2 * n])), res[-1]


def _gather_relay(name, handle, after):
    sems, srcs, lands = handle
    n = len(srcs)

    def body(*refs):
        in_sems = refs[2 * n:2 * n + 3]
        out_sems = refs[2 * n + 4 + 2 * n:2 * n + 4 + 2 * n + 2]
        _, first, relay = _gather_copies(refs[:n], refs[n:2 * n], list(in_sems) + list(out_sems))
        for a in range(n):
            for j in range(RELAY_COPIES):
                first[a][1 + j].wait_recv()
                relay[a][j].start()
        refs[-1][...] = jnp.zeros_like(refs[-1])

    res = pl.pallas_call(
        body, name=name,
        out_shape=[pltpu.HBM(a.shape, a.dtype) for a in srcs] + [pltpu.HBM(a.shape, a.dtype) for a in lands]
        + [pltpu.SemaphoreType.DMA((RELAY_COPIES * n,)), pltpu.SemaphoreType.DMA((RELAY_COPIES * n,)), jax.ShapeDtypeStruct((8, 128), F32)],
        in_specs=[HBM_SPEC] * (2 * n) + [SEM_SPEC] * 3 + [pl.BlockSpec(memory_space=pl.ANY)],
        out_specs=[HBM_SPEC] * (2 * n) + [SEM_SPEC] * 2 + [pl.BlockSpec(memory_space=pltpu.VMEM)],
        input_output_aliases={i: i for i in range(2 * n)},
        compiler_params=pltpu.CompilerParams(has_side_effects=EFFECT),
    )(*srcs, *lands, *sems, after)
    return (sems + list(res[2 * n:2 * n + 2]), list(res[:n]), list(res[n:2 * n])), res[-1]


def _gather_wait(name, handle, after):
    sems, srcs, lands = handle
    n = len(srcs)

    def body(*refs):
        local, first, relay = _gather_copies(refs[:n], refs[n:2 * n], refs[2 * n:2 * n + 5])
        for a in range(n):
            for cp in first[a]:
                cp.wait_send()
            first[a][0].wait_recv()
            for cp in relay[a]:
                cp.wait_send()
                cp.wait_recv()
            local[a].wait()

    res = pl.pallas_call(
        body, name=name,
        out_shape=[pltpu.HBM(a.shape, a.dtype) for a in srcs] + [pltpu.HBM(a.shape, a.dtype) for a in lands],
        in_specs=[HBM_SPEC] * (2 * n) + [SEM_SPEC] * 5 + [pl.BlockSpec(memory_space=pl.ANY)], out_specs=[HBM_SPEC] * (2 * n),
        input_output_aliases={i: i for i in range(2 * n)},
        compiler_params=pltpu.CompilerParams(has_side_effects=EFFECT),
    )(*srcs, *lands, *sems, after)
    return list(res[n:])


def _dep(dep, grid_rank):
    if dep is None:
        return [], []
    return [dep], [pl.BlockSpec((8, 128), (lambda i, j: (0, 0)) if grid_rank == 2 else (lambda i: (0, 0)))]


def _mm_step(name, fn, ins, in_specs, out_shape, out_spec, grid, dep=None, res=None):
    n = len(ins)
    dep_ins, dep_specs = _dep(dep, len(grid))
    res_ins, res_specs, out_shapes, out_specs = [], [], out_shape, out_spec
    if res is not None:
        h, mod, row_idx, lc = res
        tm, tn = out_spec.block_shape
        res_ins = [h, mod]
        res_specs = [pl.BlockSpec((tm, tn), lambda j, i: (i, j)), pl.BlockSpec((2, 8, tn), lambda j, i: (0, 0, j))]
        out_shapes, out_specs = [out_shape, jax.ShapeDtypeStruct(h.shape, h.dtype)], [out_spec, res_specs[0]]

    def body(*refs):
        outs = refs[n + len(res_ins) + len(dep_ins):]
        acc = fn(*refs[:n])
        outs[0][...] = acc.astype(outs[0].dtype)
        if res is not None:
            h_ref, mod_ref = refs[n:n + 2]
            row = pl.program_id(1) * tm + lax.broadcasted_iota(jnp.int32, (tm, 1), 0)
            gate = jnp.where(row < lc, mod_ref[0, row_idx:row_idx + 1, :], mod_ref[1, row_idx:row_idx + 1, :])
            outs[1][...] = h_ref[...] + gate * acc

    return pl.pallas_call(body, name=name, grid=grid, in_specs=list(in_specs) + res_specs + dep_specs, out_specs=out_specs,
                          out_shape=out_shapes, compiler_params=_params())(*ins, *res_ins, *dep_ins)


def _mm_tn(name, a, b, a_spec, b_spec, out_shape, out_spec, acc_shape, grid):
    nk = grid[-1]
    kax = len(grid) - 1
    if nk == 1:
        def whole(a_ref, b_ref, o_ref):
            o_ref[...] = _tn(_bf(a_ref[...]), _bf(b_ref[...])).astype(o_ref.dtype)

        return pl.pallas_call(whole, name=name, grid=grid, in_specs=[a_spec, b_spec], out_specs=out_spec,
                              out_shape=out_shape, compiler_params=_params())(a, b)

    def body(a_ref, b_ref, o_ref, acc_ref):
        k = pl.program_id(kax)

        @pl.when(k == 0)
        def _():
            acc_ref[...] = jnp.zeros_like(acc_ref)

        acc_ref[...] += _tn(_bf(a_ref[...]), _bf(b_ref[...]))

        @pl.when(k == nk - 1)
        def _():
            o_ref[...] = acc_ref[...].astype(o_ref.dtype)

    return pl.pallas_call(body, name=name, grid=grid, in_specs=[a_spec, b_spec], out_specs=out_spec,
                          out_shape=out_shape, scratch_shapes=[pltpu.VMEM(acc_shape, F32)],
                          compiler_params=_params())(a, b)


def _mm_cols(name, a, w3, out_dtype=F32, dep=None):
    M, K = a.shape
    J, _, n = w3.shape
    return _mm_step(
        name, lambda a_ref, w_ref: _nn(_bf(a_ref[...]), w_ref[...]), [a, w3],
        [pl.BlockSpec((M, K), lambda j, i: (0, 0), pipeline_mode=RESIDENT), pl.BlockSpec((None, K, n), lambda j, i: (j, 0, 0))],
        jax.ShapeDtypeStruct((M, J * n), out_dtype), pl.BlockSpec((M, n), lambda j, i: (0, j)), (J, 1), dep)


def _mm_plain(name, a, b, transpose_b, out_dtype=F32, tn=512, dep=None, res=None):
    M, K = a.shape
    N = b.shape[0] if transpose_b else b.shape[1]
    tn = _col_tile(N, tn)
    tm = M if res is None else _row_tile(M, 1088)
    a_spec = pl.BlockSpec((M, K), lambda j, i: (0, 0), pipeline_mode=RESIDENT)

    def rows(a_ref):
        return a_ref[...] if tm == M else a_ref[pl.ds(pl.multiple_of(pl.program_id(1) * tm, 16), tm), :]

    if transpose_b:
        b_spec = pl.BlockSpec((tn, K), lambda j, i: (j, 0))
        fn = lambda a_ref, b_ref: _nt(_bf(rows(a_ref)), _bf(b_ref[...]))
    else:
        b_spec = pl.BlockSpec((K, tn), lambda j, i: (0, j))
        fn = lambda a_ref, b_ref: _nn(_bf(rows(a_ref)), _bf(b_ref[...]))
    return _mm_step(name, fn, [a, b], [a_spec, b_spec],
                    jax.ShapeDtypeStruct((M, N), out_dtype), pl.BlockSpec((tm, tn), lambda j, i: (i, j)),
                    (N // tn, M // tm), dep, res)


def _mm_shards_nn(name, a3, w3, tn=512, dep=None, res=None):
    J, M, k = a3.shape
    N = w3.shape[2]
    tm = _row_tile(M, 544)
    tn = _col_tile(N, tn)

    def fn(a_ref, w_ref):
        acc = _nn(a_ref[0], w_ref[0])
        for j in range(1, J):
            acc += _nn(a_ref[j], w_ref[j])
        return acc

    return _mm_step(name, fn, [a3, w3],
                    [pl.BlockSpec((J, tm, k), lambda jn, i: (0, i, 0)), pl.BlockSpec((J, k, tn), lambda jn, i: (0, 0, jn))],
                    jax.ShapeDtypeStruct((M, N), F32), pl.BlockSpec((tm, tn), lambda jn, i: (i, jn)), (N // tn, M // tm), dep, res)


def _mm_shards_nn2(name, a3, w3a, b3, w3b, tn=512):
    J, M, k = a3.shape
    N = w3a.shape[2]
    tm = _row_tile(M, 544)
    tn = _col_tile(N, tn)

    def fn(a_ref, wa_ref, b_ref, wb_ref):
        acc = _nn(a_ref[0], wa_ref[0]) + _nn(b_ref[0], wb_ref[0])
        for j in range(1, J):
            acc += _nn(a_ref[j], wa_ref[j]) + _nn(b_ref[j], wb_ref[j])
        return acc

    act = pl.BlockSpec((J, tm, k), lambda jn, i: (0, i, 0))
    wsp = pl.BlockSpec((J, k, tn), lambda jn, i: (0, 0, jn))
    return _mm_step(name, fn, [a3, w3a, b3, w3b], [act, wsp, act, wsp],
                    jax.ShapeDtypeStruct((M, N), F32), pl.BlockSpec((tm, tn), lambda jn, i: (i, jn)), (N // tn, M // tm))


def _mm_cols_nt(name, a, w3, tn=512):
    M = a.shape[0]
    J, N, n = w3.shape
    tm = _row_tile(M, 544)
    tn = _col_tile(N, tn)

    def fn(a_ref, w_ref):
        acc = _nt(a_ref[:, 0:n], w_ref[0])
        for j in range(1, J):
            acc += _nt(a_ref[:, j * n:(j + 1) * n], w_ref[j])
        return acc

    return _mm_step(name, fn, [a, w3],
                    [pl.BlockSpec((tm, J * n), lambda jn, i: (i, 0)), pl.BlockSpec((J, tn, n), lambda jn, i: (0, jn, 0))],
                    jax.ShapeDtypeStruct((M, N), F32), pl.BlockSpec((tm, tn), lambda jn, i: (i, jn)), (N // tn, M // tm))


def _wgrad_cols(name, a, b, J):
    T, K = a.shape
    n = b.shape[1] // J
    return _mm_tn(name, a, b, pl.BlockSpec((T, K), lambda j, t: (0, 0), pipeline_mode=RESIDENT), pl.BlockSpec((T, n), lambda j, t: (0, j)),
                  jax.ShapeDtypeStruct((J, K, n), BF16), pl.BlockSpec((None, K, n), lambda j, t: (j, 0, 0)), (K, n), (J, 1))


def _wgrad_rows(name, a, b, tk=512):
    T, K = a.shape
    N = b.shape[1]
    tk = _col_tile(K, tk)
    return _mm_tn(name, a, b, pl.BlockSpec((T, tk), lambda kb, t: (0, kb)), pl.BlockSpec((T, N), lambda kb, t: (0, 0), pipeline_mode=RESIDENT),
                  jax.ShapeDtypeStruct((K, N), BF16), pl.BlockSpec((tk, N), lambda kb, t: (kb, 0)), (tk, N), (K // tk, 1))


def _wgrad_down(name, a3, b):
    J, T, k = a3.shape
    N = b.shape[1]
    return _mm_tn(name, a3, b, pl.BlockSpec((None, T, k), lambda j, t: (j, 0, 0)),
                  pl.BlockSpec((T, N), lambda j, t: (0, 0), pipeline_mode=RESIDENT),
                  jax.ShapeDtypeStruct((J, k, N), BF16), pl.BlockSpec((None, k, N), lambda j, t: (j, 0, 0)), (k, N), (J, 1))


def _seg(i):
    return jnp.minimum(i, 1)


def _rstd(x):
    return lax.rsqrt(jnp.mean(x * x, axis=-1, keepdims=True) + EPS)


def _norm_mod(name, h, g, mod, which):
    T, D = h.shape

    def body(h_ref, g_ref, mod_ref, o_ref):
        x = h_ref[...]
        n = x * _rstd(x) * g_ref[...]
        shift = mod_ref[3 * which:3 * which + 1, :]
        scale = mod_ref[3 * which + 1:3 * which + 2, :]
        o_ref[...] = (n * (1 + scale) + shift).astype(o_ref.dtype)

    row = pl.BlockSpec((ROW_TILE, D), lambda i: (i, 0))
    return pl.pallas_call(
        body, name=name, grid=(T // ROW_TILE,),
        in_specs=[row, pl.BlockSpec((1, D), lambda i: (0, 0)), pl.BlockSpec((None, 8, D), lambda i: (_seg(i), 0, 0))],
        out_specs=row, out_shape=jax.ShapeDtypeStruct((T, D), BF16), compiler_params=_params())(h, g, mod)


def _norm_mod_bwd(name, dxn, h, g, mod, which, dres, latent_only=False, gate=None):
    T, D = h.shape
    n_gate = 0 if gate is None else 2

    def body(dxn_ref, h_ref, g_ref, mod_ref, dres_ref, *rest):
        dh_ref, dmod_ref, dg_ref = rest[n_gate:n_gate + 3]
        i = pl.program_id(0)
        x = h_ref[...]
        r = _rstd(x)
        xhat = x * r
        g = g_ref[...]
        n = xhat * g
        scale = mod_ref[3 * which + 1:3 * which + 2, :]
        dxn = dxn_ref[...]
        dn = dxn * (1 + scale)
        dxh = dn * g
        dh = dres_ref[...] + r * (dxh - xhat * jnp.mean(dxh * xhat, axis=-1, keepdims=True))
        if latent_only:
            @pl.when(i > 0)
            def _():
                dh_ref[...] = dh
        else:
            dh_ref[...] = dh

        @pl.when(i <= 1)
        def _():
            dmod_ref[...] = jnp.zeros_like(dmod_ref)

        @pl.when(i == 0)
        def _():
            dg_ref[...] = jnp.zeros_like(dg_ref)

        dmod_ref[3 * which:3 * which + 1, :] += jnp.sum(dxn, axis=0, keepdims=True)
        dmod_ref[3 * which + 1:3 * which + 2, :] += jnp.sum(dxn * n, axis=0, keepdims=True)
        dg_ref[0:1, :] += jnp.sum(dn * xhat, axis=0, keepdims=True)

        if gate is not None:
            y_ref, gmod_ref = rest[:2]
            dy_ref, dgmod_ref = rest[5:7]
            dy_ref[...] = (dh * gmod_ref[gate[2]:gate[2] + 1, :]).astype(dy_ref.dtype)

            @pl.when(i <= 1)
            def _():
                dgmod_ref[...] = jnp.zeros_like(dgmod_ref)

            dgmod_ref[gate[2]:gate[2] + 1, :] += jnp.sum(dh * y_ref[...], axis=0, keepdims=True)

    row = pl.BlockSpec((ROW_TILE, D), lambda i: (i, 0))
    modspec = pl.BlockSpec((None, 8, D), lambda i: (_seg(i), 0, 0))
    dh_rows = T - ROW_TILE if latent_only else T
    dh_spec = pl.BlockSpec((ROW_TILE, D), lambda i: (jnp.maximum(i - 1, 0), 0)) if latent_only else row
    gate_ins, gate_specs, gate_outs, gate_shapes = [], [], [], []
    if gate is not None:
        gate_ins, gate_specs = [gate[0], gate[1]], [row, modspec]
        gate_outs, gate_shapes = [row, modspec], [jax.ShapeDtypeStruct((T, D), BF16), jax.ShapeDtypeStruct((2, 8, D), F32)]
    return pl.pallas_call(
        body, name=name, grid=(T // ROW_TILE,),
        in_specs=[row, row, pl.BlockSpec((1, D), lambda i: (0, 0)), modspec, row] + gate_specs,
        out_specs=[dh_spec, modspec, pl.BlockSpec((8, D), lambda i: (0, 0))] + gate_outs,
        out_shape=[jax.ShapeDtypeStruct((dh_rows, D), F32), jax.ShapeDtypeStruct((2, 8, D), F32), jax.ShapeDtypeStruct((8, D), F32)] + gate_shapes,
        compiler_params=_params())(dxn, h, g, mod, dres, *gate_ins)


def _gate_bwd(name, dh, y, mod, row_idx):
    T, D = dh.shape

    def body(dh_ref, y_ref, mod_ref, dy_ref, dmod_ref):
        i = pl.program_id(0)
        dh = dh_ref[...]
        dy_ref[...] = (dh * mod_ref[row_idx:row_idx + 1, :]).astype(dy_ref.dtype)

        @pl.when(i <= 1)
        def _():
            dmod_ref[...] = jnp.zeros_like(dmod_ref)

        dmod_ref[row_idx:row_idx + 1, :] += jnp.sum(dh * y_ref[...], axis=0, keepdims=True)

    row = pl.BlockSpec((ROW_TILE, D), lambda i: (i, 0))
    modspec = pl.BlockSpec((None, 8, D), lambda i: (_seg(i), 0, 0))
    return pl.pallas_call(
        body, name=name, grid=(T // ROW_TILE,), in_specs=[row, row, modspec], out_specs=[row, modspec],
        out_shape=[jax.ShapeDtypeStruct((T, D), BF16), jax.ShapeDtypeStruct((2, 8, D), F32)],
        compiler_params=_params())(dh, y, mod)


def _rot(y):
    lane = lax.broadcasted_iota(jnp.int32, y.shape, 1)
    return jnp.where((lane & 32) == 0, pltpu.roll(y, 96, 1), pltpu.roll(y, 32, 1))


def _qk_prep(name, P, q_g, k_g, rope_c, rope_s, cfg):
    T = P.shape[0]
    ATT, KVW = cfg['ATT'], cfg['KVW']

    def body(q_ref, k_ref, v_ref, qg_ref, kg_ref, c_ref, s_ref, qo_ref, ko_ref, vo_ref):
        cc, ss = c_ref[...], s_ref[...]

        def head(x, g):
            y = x * _rstd(x) * g
            return y * cc + _rot(y) * ss

        for hh in range(ATT // HEAD_DIM):
            sl = slice(hh * HEAD_DIM, (hh + 1) * HEAD_DIM)
            qo_ref[:, sl] = (head(q_ref[:, sl], qg_ref[...]) * Q_SCALE).astype(qo_ref.dtype)
        for hh in range(KVW // HEAD_DIM):
            sl = slice(hh * HEAD_DIM, (hh + 1) * HEAD_DIM)
            ko_ref[:, sl] = head(k_ref[:, sl], kg_ref[...]).astype(ko_ref.dtype)
        vo_ref[...] = v_ref[...].astype(vo_ref.dtype)

    kb = ATT // KVW
    gain = pl.BlockSpec((1, HEAD_DIM), lambda i: (0, 0))
    tab = pl.BlockSpec((ROW_TILE, HEAD_DIM), lambda i: (i, 0))
    qs = pl.BlockSpec((ROW_TILE, ATT), lambda i: (i, 0))
    ks = pl.BlockSpec((ROW_TILE, KVW), lambda i: (i, 0))
    return pl.pallas_call(
        body, name=name, grid=(T // ROW_TILE,),
        in_specs=[qs, pl.BlockSpec((ROW_TILE, KVW), lambda i: (i, kb)), pl.BlockSpec((ROW_TILE, KVW), lambda i: (i, kb + 1)),
                  gain, gain, tab, tab],
        out_specs=[qs, ks, ks],
        out_shape=[jax.ShapeDtypeStruct((T, ATT), BF16), jax.ShapeDtypeStruct((T, KVW), BF16), jax.ShapeDtypeStruct((T, KVW), BF16)],
        compiler_params=_params())(P, P, P, q_g, k_g, rope_c, rope_s)


def _qk_prep_bwd(name, dqr, dkr, P, q_g, k_g, rope_c, rope_s, cfg):
    T = P.shape[0]
    ATT, KVW = cfg['ATT'], cfg['KVW']

    def body(dq_ref, dk_ref, q_ref, k_ref, qg_ref, kg_ref, c_ref, s_ref, dqo_ref, dko_ref, dqg_ref, dkg_ref):
        i = pl.program_id(0)
        cc, ss = c_ref[...], s_ref[...]

        @pl.when(i == 0)
        def _():
            dqg_ref[...] = jnp.zeros_like(dqg_ref)
            dkg_ref[...] = jnp.zeros_like(dkg_ref)

        def head(x, g, dout):
            dy = dout * cc + _rot(dout * ss)
            r = _rstd(x)
            xhat = x * r
            dxh = dy * g
            dx = r * (dxh - xhat * jnp.mean(dxh * xhat, axis=-1, keepdims=True))
            return dx, jnp.sum(dy * xhat, axis=0, keepdims=True)

        dg = jnp.zeros((1, HEAD_DIM), F32)
        for hh in range(ATT // HEAD_DIM):
            sl = slice(hh * HEAD_DIM, (hh + 1) * HEAD_DIM)
            dx, d = head(q_ref[:, sl], qg_ref[...], dq_ref[:, sl] * ATT_SCALE)
            dqo_ref[:, sl] = dx.astype(dqo_ref.dtype)
            dg += d
        dqg_ref[0:1, :] += dg
        dg = jnp.zeros((1, HEAD_DIM), F32)
        for hh in range(KVW // HEAD_DIM):
            sl = slice(hh * HEAD_DIM, (hh + 1) * HEAD_DIM)
            dx, d = head(k_ref[:, sl], kg_ref[...], dk_ref[:, sl] * (1.0 / LOG2_E))
            dko_ref[:, sl] = dx.astype(dko_ref.dtype)
            dg += d
        dkg_ref[0:1, :] += dg

    kb = ATT // KVW
    gain = pl.BlockSpec((1, HEAD_DIM), lambda i: (0, 0))
    dgain = pl.BlockSpec((8, HEAD_DIM), lambda i: (0, 0))
    tab = pl.BlockSpec((ROW_TILE, HEAD_DIM), lambda i: (i, 0))
    qs = pl.BlockSpec((ROW_TILE, ATT), lambda i: (i, 0))
    ks = pl.BlockSpec((ROW_TILE, KVW), lambda i: (i, 0))
    return pl.pallas_call(
        body, name=name, grid=(T // ROW_TILE,),
        in_specs=[qs, ks, qs, pl.BlockSpec((ROW_TILE, KVW), lambda i: (i, kb)), gain, gain, tab, tab],
        out_specs=[qs, ks, dgain, dgain],
        out_shape=[jax.ShapeDtypeStruct((T, ATT), BF16), jax.ShapeDtypeStruct((T, KVW), BF16),
                   jax.ShapeDtypeStruct((8, HEAD_DIM), F32), jax.ShapeDtypeStruct((8, HEAD_DIM), F32)],
        compiler_params=_params())(dqr, dkr, P, P, q_g, k_g, rope_c, rope_s)


def _att_specs(T, G):
    qs = pl.BlockSpec((ROW_TILE, G * HEAD_DIM), lambda h, i: (i, h))
    kvs = pl.BlockSpec((T, HEAD_DIM), lambda h, i: (0, h))
    return qs, kvs


def _attn_dense_fwd(name, q, k, v, cfg):
    T, G, Lc = q.shape[0], cfg['G'], cfg['Lc']

    def body(q_ref, k_ref, v_ref, o_ref, lse_ref):
        def attend(rows):
            kk, vv = k_ref[0:rows, :], v_ref[0:rows, :]
            for g in range(G):
                sl = slice(g * HEAD_DIM, (g + 1) * HEAD_DIM)
                s = _nt(q_ref[:, sl], kk)
                m = jnp.max(s, axis=1, keepdims=True)
                p = jnp.exp2(s - m)
                l = jnp.sum(p, axis=1, keepdims=True)
                o_ref[:, sl] = _nn(_bf(p), vv) / l
                lse_ref[:, sl] = jnp.broadcast_to(m + jnp.log2(l), (ROW_TILE, HEAD_DIM))

        @pl.when(pl.program_id(1) == 0)
        def _():
            attend(Lc)

        @pl.when(pl.program_id(1) > 0)
        def _():
            attend(T)

    qs, kvs = _att_specs(T, G)
    return pl.pallas_call(
        body, name=name, grid=(cfg['NKV'], T // ROW_TILE), in_specs=[qs, kvs, kvs], out_specs=[qs, qs],
        out_shape=[jax.ShapeDtypeStruct(q.shape, F32), jax.ShapeDtypeStruct(q.shape, F32)],
        compiler_params=_params())(q, k, v)


def _attn_dense_bwd(name, q, k, v, o, lse, dmix, cfg):
    T, G, Lc = q.shape[0], cfg['G'], cfg['Lc']

    def body(q_ref, k_ref, v_ref, o_ref, lse_ref, do_ref, dq_ref, dk_ref, dv_ref):
        i = pl.program_id(1)

        @pl.when(i == 0)
        def _():
            dk_ref[...] = jnp.zeros_like(dk_ref)
            dv_ref[...] = jnp.zeros_like(dv_ref)

        def attend(rows):
            kk, vv = k_ref[0:rows, :], v_ref[0:rows, :]
            for g in range(G):
                sl = slice(g * HEAD_DIM, (g + 1) * HEAD_DIM)
                qg, do = q_ref[:, sl], do_ref[:, sl]
                delta = jnp.sum(do * o_ref[:, sl], axis=1, keepdims=True)
                p = jnp.exp2(_nt(qg, kk) - lse_ref[:, g * HEAD_DIM:g * HEAD_DIM + 1])
                dob = _bf(do)
                dv_ref[0:rows, :] += _tn(_bf(p), dob)
                ds = _bf(p * (_nt(dob, vv) - delta))
                dq_ref[:, sl] = _nn(ds, kk)
                dk_ref[0:rows, :] += _tn(ds, qg)

        @pl.when(i == 0)
        def _():
            attend(Lc)

        @pl.when(i > 0)
        def _():
            attend(T)

    qs, kvs = _att_specs(T, G)
    return pl.pallas_call(
        body, name=name, grid=(cfg['NKV'], T // ROW_TILE), in_specs=[qs, kvs, kvs, qs, qs, qs], out_specs=[qs, kvs, kvs],
        out_shape=[jax.ShapeDtypeStruct(q.shape, F32), jax.ShapeDtypeStruct(k.shape, F32), jax.ShapeDtypeStruct(k.shape, F32)],
        compiler_params=_params())(q, k, v, o, lse, dmix)


def _band(i, T, Lc):
    start = pl.multiple_of(jnp.clip(WINDOW + (i - 1) * ROW_TILE, 0, T - BAND), WINDOW)
    qpos = (i - 1) * ROW_TILE + lax.broadcasted_iota(jnp.int32, (ROW_TILE, 1), 0)
    kpos = start - Lc + lax.broadcasted_iota(jnp.int32, (1, BAND), 1)
    ok = (jnp.abs(kpos - qpos) <= WINDOW) & (kpos >= 0) & (i > 0)
    return start, jnp.where(ok, 0.0, NEG_INF).astype(F32)


def _attn_win_fwd(name, q, k, v, sink, cfg):
    T, G, Lc = q.shape[0], cfg['G'], cfg['Lc']

    def body(sink_ref, q_ref, k_ref, v_ref, o_ref, lse_ref):
        h, i = pl.program_id(0), pl.program_id(1)
        start, bias = _band(i, T, Lc)
        kc, vc = k_ref[0:Lc, :], v_ref[0:Lc, :]
        kb, vb = k_ref[pl.ds(start, BAND), :], v_ref[pl.ds(start, BAND), :]
        for g in range(G):
            sl = slice(g * HEAD_DIM, (g + 1) * HEAD_DIM)
            qg = q_ref[:, sl]
            sk = sink_ref[h * G + g] * LOG2_E
            sc = _nt(qg, kc)
            sb = _nt(qg, kb) + bias
            m = jnp.maximum(jnp.maximum(jnp.max(sc, axis=1, keepdims=True), jnp.max(sb, axis=1, keepdims=True)), sk)
            pc, pb = jnp.exp2(sc - m), jnp.exp2(sb - m)
            l = jnp.sum(pc, axis=1, keepdims=True) + jnp.sum(pb, axis=1, keepdims=True) + jnp.exp2(sk - m)
            o_ref[:, sl] = (_nn(_bf(pc), vc) + _nn(_bf(pb), vb)) / l
            lse_ref[:, sl] = jnp.broadcast_to(m + jnp.log2(l), (ROW_TILE, HEAD_DIM))

    qs, kvs = _att_specs(T, G)
    return pl.pallas_call(
        body, name=name, grid=(cfg['NKV'], T // ROW_TILE),
        in_specs=[pl.BlockSpec(memory_space=pltpu.SMEM), qs, kvs, kvs], out_specs=[qs, qs],
        out_shape=[jax.ShapeDtypeStruct(q.shape, F32), jax.ShapeDtypeStruct(q.shape, F32)],
        compiler_params=_params())(sink, q, k, v)


def _attn_win_bwd(name, q, k, v, o, lse, dmix, sink, cfg):
    T, G, Lc = q.shape[0], cfg['G'], cfg['Lc']

    def body(sink_ref, q_ref, k_ref, v_ref, o_ref, lse_ref, do_ref, dq_ref, dk_ref, dv_ref, dsink_ref):
        h, i = pl.program_id(0), pl.program_id(1)
        start, bias = _band(i, T, Lc)
        kc, vc = k_ref[0:Lc, :], v_ref[0:Lc, :]
        kb, vb = k_ref[pl.ds(start, BAND), :], v_ref[pl.ds(start, BAND), :]

        @pl.when(i == 0)
        def _():
            dk_ref[...] = jnp.zeros_like(dk_ref)
            dv_ref[...] = jnp.zeros_like(dv_ref)
            dsink_ref[...] = jnp.zeros_like(dsink_ref)

        for g in range(G):
            sl = slice(g * HEAD_DIM, (g + 1) * HEAD_DIM)
            qg, do = q_ref[:, sl], do_ref[:, sl]
            lse = lse_ref[:, g * HEAD_DIM:g * HEAD_DIM + 1]
            delta = jnp.sum(do * o_ref[:, sl], axis=1, keepdims=True)
            pc = jnp.exp2(_nt(qg, kc) - lse)
            pb = jnp.exp2(_nt(qg, kb) + bias - lse)
            ps = jnp.exp2(sink_ref[h * G + g] * LOG2_E - lse)
            dob = _bf(do)
            dv_ref[0:Lc, :] += _tn(_bf(pc), dob)
            dv_ref[pl.ds(start, BAND), :] += _tn(_bf(pb), dob)
            dsc = _bf(pc * (_nt(dob, vc) - delta))
            dsb = _bf(pb * (_nt(dob, vb) - delta))
            dq_ref[:, sl] = _nn(dsc, kc) + _nn(dsb, kb)
            dk_ref[0:Lc, :] += _tn(dsc, qg)
            dk_ref[pl.ds(start, BAND), :] += _tn(dsb, qg)
            dsk = jnp.where(i > 0, -jnp.sum(ps * delta, axis=0, keepdims=True), 0.0)
            dsink_ref[:, sl] += jnp.broadcast_to(dsk, (8, HEAD_DIM))

    qs, kvs = _att_specs(T, G)
    return pl.pallas_call(
        body, name=name, grid=(cfg['NKV'], T // ROW_TILE),
        in_specs=[pl.BlockSpec(memory_space=pltpu.SMEM), qs, kvs, kvs, qs, qs, qs],
        out_specs=[qs, kvs, kvs, pl.BlockSpec((None, 8, G * HEAD_DIM), lambda h, i: (h, 0, 0))],
        out_shape=[jax.ShapeDtypeStruct(q.shape, F32), jax.ShapeDtypeStruct(k.shape, F32), jax.ShapeDtypeStruct(k.shape, F32),
                   jax.ShapeDtypeStruct((cfg['NKV'], 8, G * HEAD_DIM), F32)],
        compiler_params=_params())(sink, q, k, v, o, lse, dmix)


def _seq_pos(T, Lc):
    row = lax.broadcasted_iota(jnp.int32, (T, 1), 0)
    return jnp.where(row < Lc, row, row - Lc), jnp.where(row < Lc, Lc, T - Lc)


def _fw(x, k, pos, seglen):
    return jnp.where(pos + k < seglen, pltpu.roll(x, x.shape[0] - k, 0), 0.0)


def _bw(x, k, pos):
    return jnp.where(pos - k >= 0, pltpu.roll(x, k, 0), 0.0)


def _conv_fwd(name, P, conv_w8, cfg):
    T, Lc = P.shape[0], cfg['Lc']
    cb = (cfg['ATT'] + 2 * cfg['KVW']) // HEAD_DIM
    na = AUX_WIDTH // HEAD_DIM

    def body(gb_ref, gc_ref, u_ref, w_ref, o_ref):
        pos, seglen = _seq_pos(T, Lc)
        z = gc_ref[...] * u_ref[...]
        conv = w_ref[0:1, :] * _bw(z, 1, pos) + w_ref[1:2, :] * z + w_ref[2:3, :] * _fw(z, 1, pos, seglen)
        o_ref[...] = gb_ref[...] * conv

    col = lambda off: pl.BlockSpec((T, HEAD_DIM), lambda c: (0, cb + off + c))
    return pl.pallas_call(
        body, name=name, grid=(na,),
        in_specs=[col(0), col(na), col(2 * na), pl.BlockSpec((8, HEAD_DIM), lambda c: (0, c))],
        out_specs=pl.BlockSpec((T, HEAD_DIM), lambda c: (0, c)),
        out_shape=jax.ShapeDtypeStruct((T, AUX_WIDTH), F32), compiler_params=_params())(P, P, P, conv_w8)


def _conv_bwd(name, P, conv_w8, dmix, cfg):
    T, Lc = P.shape[0], cfg['Lc']
    cb = (cfg['ATT'] + 2 * cfg['KVW']) // HEAD_DIM
    ob = cfg['ATT'] // HEAD_DIM
    na = AUX_WIDTH // HEAD_DIM

    def body(gb_ref, gc_ref, u_ref, w_ref, do_ref, dgb_ref, dgc_ref, du_ref, dw_ref):
        pos, seglen = _seq_pos(T, Lc)
        gc, u, do = gc_ref[...], u_ref[...], do_ref[...]
        z = gc * u
        zm, zp = _bw(z, 1, pos), _fw(z, 1, pos, seglen)
        w0, w1, w2 = w_ref[0:1, :], w_ref[1:2, :], w_ref[2:3, :]
        dgb_ref[...] = (do * (w0 * zm + w1 * z + w2 * zp)).astype(dgb_ref.dtype)
        dc = do * gb_ref[...]
        dz = w0 * _fw(dc, 1, pos, seglen) + w1 * dc + w2 * _bw(dc, 1, pos)
        dgc_ref[...] = (dz * u).astype(dgc_ref.dtype)
        du_ref[...] = (dz * gc).astype(du_ref.dtype)
        dw_ref[...] = jnp.zeros_like(dw_ref)
        dw_ref[0:1, :] = jnp.sum(dc * zm, axis=0, keepdims=True)
        dw_ref[1:2, :] = jnp.sum(dc * z, axis=0, keepdims=True)
        dw_ref[2:3, :] = jnp.sum(dc * zp, axis=0, keepdims=True)

    col = lambda off: pl.BlockSpec((T, HEAD_DIM), lambda c: (0, cb + off + c))
    wspec = pl.BlockSpec((8, HEAD_DIM), lambda c: (0, c))
    ocol = lambda off: pl.BlockSpec((T, HEAD_DIM), lambda c: (0, off + c))
    return pl.pallas_call(
        body, name=name, grid=(na,),
        in_specs=[col(0), col(na), col(2 * na), wspec, ocol(ob)],
        out_specs=[ocol(0), ocol(0), ocol(0), wspec],
        out_shape=[jax.ShapeDtypeStruct((T, AUX_WIDTH), BF16)] * 3 + [jax.ShapeDtypeStruct((8, AUX_WIDTH), F32)],
        compiler_params=_params())(P, P, P, conv_w8, dmix)


def _window_sums(x, half, pos, seglen):
    fwd, bwd = x, x
    s = 1
    while s < half:
        fwd = fwd + _fw(fwd, s, pos, seglen)
        bwd = bwd + _bw(bwd, s, pos)
        s *= 2
    return fwd, bwd


def _pooled(u, half, pos, seglen):
    fwd, bwd = _window_sums(u, half, pos, seglen)
    cnt = (jnp.minimum(pos + half, seglen) - jnp.maximum(pos - half, 0)).astype(F32)
    return (fwd + _bw(bwd, 1, pos)) / cnt - u, cnt


def _pool_fwd(name, P, pool_w, pool_scale, cfg):
    T, Lc = P.shape[0], cfg['Lc']
    cb = (cfg['ATT'] + 2 * cfg['KVW']) // HEAD_DIM

    def body(u_ref, w_ref, s_ref, o_ref):
        g = pl.program_id(0)
        pos, seglen = _seq_pos(T, Lc)
        for k, half in enumerate(POOL_HALF):
            @pl.when(g == k)
            def _(half=half):
                pooled, _ = _pooled(u_ref[...], half, pos, seglen)
                o_ref[...] = _nn(_bf(pooled), _bf(w_ref[...])) * s_ref[...]

    return pl.pallas_call(
        body, name=name, grid=(AUX_GROUPS,),
        in_specs=[pl.BlockSpec((T, HEAD_DIM), lambda g: (0, cb + g)), pl.BlockSpec((None, HEAD_DIM, HEAD_DIM), lambda g: (g, 0, 0)),
                  pl.BlockSpec((1, HEAD_DIM), lambda g: (0, g))],
        out_specs=pl.BlockSpec((T, HEAD_DIM), lambda g: (0, g)),
        out_shape=jax.ShapeDtypeStruct((T, AUX_WIDTH), F32), compiler_params=_params())(P, pool_w, pool_scale)


def _pool_bwd(name, P, pool_w, pool_scale, dmix, cfg):
    T, Lc = P.shape[0], cfg['Lc']
    cb = (cfg['ATT'] + 2 * cfg['KVW']) // HEAD_DIM
    ob = cfg['ATT'] // HEAD_DIM

    def body(u_ref, w_ref, s_ref, do_ref, du_ref, dw_ref, ds_ref):
        g = pl.program_id(0)
        pos, seglen = _seq_pos(T, Lc)
        for k, half in enumerate(POOL_HALF):
            @pl.when(g == k)
            def _(half=half):
                do = do_ref[...]
                pooled, cnt = _pooled(u_ref[...], half, pos, seglen)
                wb = _bf(w_ref[...])
                mixed = _nn(_bf(pooled), wb)
                ds_ref[...] = jnp.broadcast_to(jnp.sum(do * mixed, axis=0, keepdims=True), ds_ref.shape)
                dmixed = _bf(do * s_ref[...])
                dw_ref[...] = _tn(_bf(pooled), dmixed)
                dpooled = _nt(dmixed, wb)
                e = dpooled / cnt
                fwd, bwd = _window_sums(e, half, pos, seglen)
                adj = fwd + _fw(e, half, pos, seglen) + _bw(bwd, 1, pos) - _bw(e, half, pos)
                du_ref[...] = (adj - dpooled).astype(du_ref.dtype)

    wspec = pl.BlockSpec((None, HEAD_DIM, HEAD_DIM), lambda g: (g, 0, 0))
    return pl.pallas_call(
        body, name=name, grid=(AUX_GROUPS,),
        in_specs=[pl.BlockSpec((T, HEAD_DIM), lambda g: (0, cb + g)), wspec, pl.BlockSpec((1, HEAD_DIM), lambda g: (0, g)),
                  pl.BlockSpec((T, HEAD_DIM), lambda g: (0, ob + g))],
        out_specs=[pl.BlockSpec((T, HEAD_DIM), lambda g: (0, g)), wspec, pl.BlockSpec((8, HEAD_DIM), lambda g: (0, g))],
        out_shape=[jax.ShapeDtypeStruct((T, AUX_WIDTH), BF16), jax.ShapeDtypeStruct(pool_w.shape, F32),
                   jax.ShapeDtypeStruct((8, AUX_WIDTH), F32)],
        compiler_params=_params())(P, pool_w, pool_scale, dmix)


def _ffn_up(name, hn, wg3, wu3, dep=None):
    T, D = hn.shape
    J, k, _ = wg3.shape
    tm = _row_tile(T, 1088)
    dep_ins, dep_specs = _dep(dep, 2)

    def body(x_ref, wg_ref, wu_ref, *rest):
        s_ref, ud_ref, a_ref = rest[len(dep_ins):]
        x = x_ref[pl.ds(pl.multiple_of(pl.program_id(1) * tm, 16), tm), :]
        g, u = _nt(x, wg_ref[...]), _nt(x, wu_ref[...])
        sig = jax.nn.sigmoid(g)
        silu = g * sig
        s_ref[...] = silu.astype(s_ref.dtype)
        ud_ref[...] = (u * (sig * (1 + g * (1 - sig)))).astype(ud_ref.dtype)
        a_ref[...] = (silu * u).astype(a_ref.dtype)

    wspec = pl.BlockSpec((None, k, D), lambda j, i: (j, 0, 0))
    ospec = pl.BlockSpec((None, tm, k), lambda j, i: (j, i, 0))
    return pl.pallas_call(
        body, name=name, grid=(J, T // tm),
        in_specs=[pl.BlockSpec((T, D), lambda j, i: (0, 0), pipeline_mode=RESIDENT), wspec, wspec] + dep_specs,
        out_specs=[ospec, ospec, ospec],
        out_shape=[jax.ShapeDtypeStruct((J, T, k), BF16)] * 3,
        compiler_params=_params())(hn, wg3, wu3, *dep_ins)


def _ffn_dact(name, dF, wd3, silu_g, u_dsilu):
    T, D = dF.shape
    J, k, _ = wd3.shape
    tm = _row_tile(T, 1088)

    def body(df_ref, wd_ref, s_ref, ud_ref, dg_ref, du_ref):
        rows = pl.ds(pl.multiple_of(pl.program_id(1) * tm, 16), tm)
        da = _nt(df_ref[rows, :], wd_ref[...])
        du_ref[...] = (da * s_ref[...].astype(F32)).astype(du_ref.dtype)
        dg_ref[...] = (da * ud_ref[...].astype(F32)).astype(dg_ref.dtype)

    aspec = pl.BlockSpec((None, tm, k), lambda j, i: (j, i, 0))
    return pl.pallas_call(
        body, name=name, grid=(J, T // tm),
        in_specs=[pl.BlockSpec((T, D), lambda j, i: (0, 0), pipeline_mode=RESIDENT), pl.BlockSpec((None, k, D), lambda j, i: (j, 0, 0)), aspec, aspec],
        out_specs=[aspec, aspec],
        out_shape=[jax.ShapeDtypeStruct((J, T, k), BF16), jax.ShapeDtypeStruct((J, T, k), BF16)],
        compiler_params=_params())(dF, wd3, silu_g, u_dsilu)


def _loss_head(name, h, g, target, cfg):
    T, D = h.shape

    def body(h_ref, g_ref, t_ref, dh_ref, loss_ref, dg_ref):
        i = pl.program_id(0)

        @pl.when(i == 0)
        def _():
            dh_ref[...] = jnp.zeros_like(dh_ref)
            loss_ref[...] = jnp.zeros_like(loss_ref)
            dg_ref[...] = jnp.zeros_like(dg_ref)

        @pl.when(i > 0)
        def _():
            x = h_ref[...]
            r = _rstd(x)
            xhat = x * r
            gg = g_ref[...]
            err = xhat * gg - t_ref[...]
            loss_ref[...] += 0.5 * jnp.sum(jnp.sum(err * err, axis=1, keepdims=True) / D, axis=0, keepdims=True)
            dy = err / D
            dg_ref[0:1, :] += jnp.sum(dy * xhat, axis=0, keepdims=True)
            dxh = dy * gg
            dh_ref[...] = r * (dxh - xhat * jnp.mean(dxh * xhat, axis=-1, keepdims=True))

    row = pl.BlockSpec((ROW_TILE, D), lambda i: (i, 0))
    return pl.pallas_call(
        body, name=name, grid=(T // ROW_TILE,),
        in_specs=[row, pl.BlockSpec((1, D), lambda i: (0, 0)), pl.BlockSpec((ROW_TILE, D), lambda i: (jnp.maximum(i - 1, 0), 0))],
        out_specs=[row, pl.BlockSpec((8, 128), lambda i: (0, 0)), pl.BlockSpec((8, D), lambda i: (0, 0))],
        out_shape=[jax.ShapeDtypeStruct((T, D), F32), jax.ShapeDtypeStruct((8, 128), F32), jax.ShapeDtypeStruct((8, D), F32)],
        compiler_params=_params())(h, g, target)


def _adamw(name, parts, w, m, v, dep=None):
    R, C = w.shape
    n_parts = parts.shape[0]
    tr = _row_tile(R, max(16, ADAM_BLOCK_ELEMS // C)) if R % 16 == 0 else R
    bc1 = 1.0 - ADAM_B1 ** ADAM_STEP
    bc2 = 1.0 - ADAM_B2 ** ADAM_STEP
    dep_ins, dep_specs = _dep(dep, 1)

    def body(p_ref, w_ref, m_ref, v_ref, *rest):
        g_ref, d_ref, nm_ref, nv_ref = rest[len(dep_ins):]
        g = p_ref[0].astype(F32)
        for k in range(1, n_parts):
            g = g + p_ref[k].astype(F32)
        nm = ADAM_B1 * m_ref[...] + (1.0 - ADAM_B1) * g
        nv = ADAM_B2 * v_ref[...] + (1.0 - ADAM_B2) * (g * g)
        g_ref[...] = g
        nm_ref[...] = nm
        nv_ref[...] = nv
        d_ref[...] = -ADAM_LR * ((nm / bc1) / (jnp.sqrt(nv / bc2) + ADAM_EPS) + ADAM_WD * w_ref[...])

    blk = pl.BlockSpec((tr, C), lambda i: (i, 0))
    return pl.pallas_call(
        body, name=name, grid=(R // tr,), in_specs=[pl.BlockSpec((n_parts, tr, C), lambda i: (0, i, 0)), blk, blk, blk] + dep_specs,
        out_specs=[blk] * 4, out_shape=[jax.ShapeDtypeStruct((R, C), F32)] * 4, compiler_params=_params())(parts, w, m, v, *dep_ins)


class _WeightStream:
    def __init__(self, cast):
        self.cast, self.handles = cast, {}

    @staticmethod
    def _tag(l, group):
        return ("ffn" if group is FFN_WEIGHTS else group[0]) + str(l)

    def start(self, l, group, after=None):
        self.handles[l, group], token = _gather_start(f"gather_{self._tag(l, group)}_start", [self.cast(l, n) for n in group], after)
        return token

    def relay(self, l, group, after):
        self.handles[l, group], token = _gather_relay(f"gather_{self._tag(l, group)}_relay", self.handles[l, group], after)
        return token

    def get(self, l, group, after):
        got = dict(zip(group, _gather_wait(f"gather_{self._tag(l, group)}_wait", self.handles[l, group], after)))
        if 'w_out' in got:
            rows, cols = got['w_out'].shape[1:]
            got['w_out'] = got['w_out'].reshape(N_DEV * rows, cols)
        return got


def _layer_fwd(l, h, p, stream, mod, rope, conv_w8, cfg):
    nm = f"l{l}_"
    mod = mod + stream.relay(l, IN_WEIGHT, h)[0, 0]
    xn = _norm_mod(nm + "norm1", h, p['norm1_g'], mod, 0)
    W = stream.get(l, IN_WEIGHT, xn)
    token = None
    if l == 0:
        token = stream.start(0, OUT_WEIGHT, after=W['w_in']) + stream.start(0, FFN_WEIGHTS, after=W['w_in'])
    P = _mm_cols(nm + "w_in", xn, W['w_in'], dep=token)
    qr, kr, vb = _qk_prep(nm + "qk_prep", P, p['q_norm_g'], p['k_norm_g'], rope[0], rope[1], cfg)
    if l == 0:
        o, lse = _attn_dense_fwd(nm + "attn", qr, kr, vb, cfg)
        aux = _conv_fwd(nm + "conv", P, conv_w8, cfg)
    else:
        o, lse = _attn_win_fwd(nm + "attn", qr, kr, vb, p['sink'], cfg)
        aux = _pool_fwd(nm + "pool", P, p['pool_w'], p['pool_scale'], cfg)
    mix = jnp.concatenate([o, aux], axis=1).astype(BF16)
    W.update(stream.get(l, OUT_WEIGHT, stream.relay(l, OUT_WEIGHT, mix)))
    y, h2 = _mm_plain(nm + "w_out", mix, W['w_out'], False, dep=stream.relay(l, FFN_WEIGHTS, o), res=(h, mod, 2, cfg['Lc']))
    hn = _norm_mod(nm + "norm2", h2, p['norm2_g'], mod, 1)
    W.update(stream.get(l, FFN_WEIGHTS, hn))
    token = stream.start(1, IN_WEIGHT, after=W['w_down']) if l == 0 else None
    silu_g, u_dsilu, A = _ffn_up(nm + "ffn_up", hn, W['w_gate'], W['w_up'], dep=token)
    if l == 0:
        token = stream.start(1, OUT_WEIGHT, after=A) + stream.start(1, FFN_WEIGHTS, after=A)
    F, h3 = _mm_shards_nn(nm + "w_down", A, W['w_down'], dep=token, res=(h2, mod, 5, cfg['Lc']))
    saved = dict(h=h, xn=xn, P=P, qr=qr, kr=kr, vb=vb, o=o, lse=lse, mix=mix, y=y, h2=h2, hn=hn, silu_g=silu_g, u_dsilu=u_dsilu, A=A, F=F)
    return h3, saved, W


def _layer_bwd(l, dh3, s, p, W, mod, rope, conv_w8, cfg, res2_bwd=None, below=None):
    nm = f"l{l}_bwd_"
    J = N_DEV
    dF, dmod = _gate_bwd(nm + "res2", dh3, s['F'], mod, 5) if res2_bwd is None else res2_bwd
    dG, dU = _ffn_dact(nm + "ffn_act", dF, W['w_down'], s['silu_g'], s['u_dsilu'])
    big = {'w_down': _wgrad_down(nm + "dw_down", s['A'], dF),
           'w_gate': _wgrad_down(nm + "dw_gate", dG, s['hn']),
           'w_up': _wgrad_down(nm + "dw_up", dU, s['hn'])}
    handles = {}
    handles['ffn'], token = _exchange_start(f"scatter_ffn{l}_start", [big[n] for n in FFN_WEIGHTS], True)
    mod = mod + token[0, 0]
    dhn = _mm_shards_nn2(nm + "dhn", dG, W['w_gate'], dU, W['w_up'])
    dh2, dm, dg2, dY, dm_gate = _norm_mod_bwd(nm + "norm2", dhn, s['h2'], p['norm2_g'], mod, 1, dh3, gate=(s['y'], mod, 2))
    dmod += dm + dm_gate
    dwo = _wgrad_rows(nm + "dw_out", s['mix'], dY)
    handles['w_out'], token = _exchange_start(f"scatter_w_out{l}_start", [dwo.reshape((J, dwo.shape[0] // J, dwo.shape[1]))], True)
    dmix = _mm_plain(nm + "dmix", dY, W['w_out'], True, dep=token)
    small = {'norm2_g': dg2[0]}
    if l == 0:
        dqr, dkr, dv = _attn_dense_bwd(nm + "attn", s['qr'], s['kr'], s['vb'], s['o'], s['lse'], dmix, cfg)
        *daux, dcw = _conv_bwd(nm + "conv", s['P'], conv_w8, dmix, cfg)
        small['conv_w'] = dcw[0:3]
    else:
        dqr, dkr, dv, dsk = _attn_win_bwd(nm + "attn", s['qr'], s['kr'], s['vb'], s['o'], s['lse'], dmix, p['sink'], cfg)
        du, dpw, dps = _pool_bwd(nm + "pool", s['P'], p['pool_w'], p['pool_scale'], dmix, cfg)
        daux = [du]
        small.update(sink=dsk[:, 0, ::HEAD_DIM].reshape(-1), pool_w=dpw, pool_scale=dps[0])
    dq, dk, dqg, dkg = _qk_prep_bwd(nm + "qk_prep", dqr, dkr, s['P'], p['q_norm_g'], p['k_norm_g'], rope[0], rope[1], cfg)
    small.update(q_norm_g=dqg[0], k_norm_g=dkg[0])
    dP = jnp.concatenate([dq, dk, dv.astype(BF16), *daux], axis=1)
    handles['w_in'], token = _exchange_start(f"scatter_w_in{l}_start", [_wgrad_cols(nm + "dw_in", s['xn'], dP, J)], True)
    mod = mod + token[0, 0]
    dxn = _mm_cols_nt(nm + "dxn", dP, W['w_in'])
    gate = None if below is None else (below[0], below[1], 5)
    dh, dm, dg1, *res2_below = _norm_mod_bwd(nm + "norm1", dxn, s['h'], p['norm1_g'], mod, 0, dh2, latent_only=(l == 0), gate=gate)
    dmod += dm
    small['norm1_g'] = dg1[0]
    return dh, dmod, small, handles, token, (tuple(res2_below) or None)


def _rope_tables(S, Lc):
    half = HEAD_DIM // 4
    pos = np.arange(S)
    inv = ROPE_THETA ** (-np.arange(0, 2 * half, 2, dtype=np.float32) / (2 * half))
    inv = jnp.asarray(inv, F32)
    ang_r = jnp.asarray(pos // GRID_W, F32)[:, None] * inv
    ang_c = jnp.asarray(pos % GRID_W, F32)[:, None] * inv
    cos = jnp.concatenate([jnp.cos(ang_r)] * 2 + [jnp.cos(ang_c)] * 2, axis=1)
    sin = jnp.concatenate([-jnp.sin(ang_r), jnp.sin(ang_r), -jnp.sin(ang_c), jnp.sin(ang_c)], axis=1)
    return (jnp.concatenate([jnp.ones((Lc, HEAD_DIM), F32), cos], axis=0),
            jnp.concatenate([jnp.zeros((Lc, HEAD_DIM), F32), sin], axis=0))


def _pad_rows(a, rows):
    return jnp.concatenate([a, jnp.zeros((rows - a.shape[0],) + a.shape[1:], a.dtype)], axis=0)


def _flat128(a, nlead):
    lead = a.shape[:nlead]
    f = a.reshape(lead + (-1,))
    pad = (-f.shape[-1]) % 128
    if pad:
        f = jnp.concatenate([f, jnp.zeros(lead + (pad,), f.dtype)], axis=-1)
    return f.reshape(lead + (-1, 128))


def _pack(named, nlead=0):
    rows, layout, at = [], {}, 0
    for name, a in named:
        f = _flat128(a, nlead)
        n = f.shape[-2]
        pad = (-n) % 8
        if pad:
            f = jnp.concatenate([f, jnp.zeros(f.shape[:-2] + (pad, 128), f.dtype)], axis=-2)
        layout[name] = (at, n, a.shape[nlead:])
        rows.append(f)
        at += n + pad
    return jnp.concatenate(rows, axis=-2), layout


def _unpack(arr, layout, name):
    at, n, shape = layout[name]
    return arr[..., at:at + n, :].reshape(arr.shape[:-2] + (-1,))[..., :math.prod(shape)].reshape(arr.shape[:-2] + tuple(shape))


def kernel(*args):
    A = dict(zip(INPUT_NAMES, args, strict=True))
    x, ctx = A['x'][0], A['ctx'][0]
    S, D = x.shape
    Lc = ctx.shape[0]
    T = Lc + S
    ATT = D - AUX_WIDTH
    KVW = (A['l1_w_in'].shape[1] * N_DEV - ATT - AUX_WIDTH) // 2
    cfg = dict(ATT=ATT, KVW=KVW, NKV=KVW // HEAD_DIM, G=ATT // KVW, Lc=Lc)
    assert Lc == ROW_TILE and S % ROW_TILE == 0 and T >= BAND and S % GRID_W == 0
    cw = A['l0_conv_w'].shape[1]
    me = 4 * lax.axis_index("x") + 2 * lax.axis_index("y") + lax.axis_index("c")

    def layer_params(l):
        pre = f"l{l}_"
        return {k[len(pre):]: (v.reshape(1, -1) if v.ndim == 1 and k != 'l1_sink' else v) for k, v in A.items() if k.startswith(pre)}

    params = [layer_params(0), layer_params(1)]

    def cast(l, n):
        w = A[f'l{l}_{n}']
        return (w.T if n in TRANSPOSED else w).astype(BF16)

    stream = _WeightStream(cast)
    token = stream.start(0, IN_WEIGHT)

    big_names = [n for n in WEIGHT_NAMES if n[3:] in BIG_WEIGHTS + ('w_mod',)]
    rest = [n for n in WEIGHT_NAMES if n not in big_names]
    early = ['x', 'ctx'] + rest + ['m_' + n for n in rest] + ['v_' + n for n in rest]
    token, held = lax.optimization_barrier((token, [A[n] for n in early]))
    A.update(zip(early, held))
    x, ctx = A['x'][0], A['ctx'][0]
    wp, layw = _pack([(n, A[n]) for n in rest])
    mp, _ = _pack([(n, A['m_' + n]) for n in rest])
    vp, _ = _pack([(n, A['v_' + n]) for n in rest])
    rope = _rope_tables(S, Lc)
    h = jnp.concatenate([ctx, x], axis=0)

    sc_own = jax.nn.silu(A['c']) + token[0, 0]
    first, lay0 = _pack([('sc', sc_own), ('conv_w', A['l0_conv_w'])])
    first, h, wp, mp, vp = lax.optimization_barrier((first, h, wp, mp, vp))
    first_all = _exchange("gather_cond", [first], False)[0]
    sc_all = _unpack(first_all, lay0, 'sc')[:, 0]
    conv_w = _unpack(first_all, lay0, 'conv_w').transpose(1, 0, 2).reshape(3, N_DEV * cw)
    conv_w8 = _pad_rows(conv_w, 8)
    sc_ctx = jax.nn.silu(A['c_ctx'])
    s16 = _pad_rows(jnp.concatenate([sc_all, sc_ctx[None]], axis=0), 16)

    nmod = A['l0_w_mod'].shape[1]
    modp = jnp.concatenate([_mm_plain(f"l{l}_mod", s16, A[f'l{l}_w_mod'], False) for l in range(2)], axis=1)
    modp_all = _exchange("gather_mod", [modp], False)[0]
    mods = []
    for l in range(2):
        full = modp_all[:, :, l * nmod:(l + 1) * nmod].transpose(1, 0, 2).reshape(16, N_MOD * D) + A[f'l{l}_b_mod'][None]
        both = jnp.stack([full[8], lax.dynamic_index_in_dim(full, me, 0, keepdims=False)]).reshape(2, N_MOD, D)
        mods.append(jnp.concatenate([both, jnp.zeros((2, 8 - N_MOD, D), F32)], axis=1))

    saved, W = [], []
    for l in range(2):
        h, s, Wl = _layer_fwd(l, h, params[l], stream, mods[l], rope, conv_w8, cfg)
        saved.append(s)
        W.append(Wl)

    dh, loss_blk, dgf = _loss_head("loss_head", h, A['final_norm_g'].reshape(1, -1), A['loss_target'][0], cfg)
    loss = lax.psum(loss_blk[0, 0], ("x", "y", "c"))

    grads, small, dmods, scatters = {}, {'final_norm_g': dgf[0]}, [None, None], [None, None]
    token, res2_bwd = jnp.zeros((8, 128), F32), None
    for l in (1, 0):
        below = (saved[0]['F'], mods[0]) if l == 1 else None
        dh, dmods[l], sm, scatters[l], token, res2_bwd = _layer_bwd(
            l, dh, saved[l], params[l], W[l], mods[l] + token[0, 0], rope, conv_w8, cfg, res2_bwd, below)
        small.update({f'l{l}_{k}': v for k, v in sm.items()})
    grad_x = dh[None]

    def landed(l, key, after):
        group = FFN_WEIGHTS if key == 'ffn' else (key,)
        for n, parts in zip(group, _exchange_wait(f"scatter_{key}{l}_wait", scatters[l][key], after)):
            shape = A[f'l{l}_{n}'].shape
            grads[f'l{l}_{n}'] = (parts.reshape((N_DEV,) + (shape[::-1] if n in TRANSPOSED else shape)), None)

    out = {}

    def adam(n, dep=None):
        w, m, v = A[n], A['m_' + n], A['v_' + n]
        if n[3:] in TRANSPOSED:
            res = _adamw("adamw_" + n, grads[n][0], w.T, m.T, v.T, dep)
            out[n] = tuple(r.T for r in res)
        else:
            res = out[n] = _adamw("adamw_" + n, grads[n][0], w, m, v, dep)
        return res[1]

    small_names = [n for n in WEIGHT_NAMES if n in small]
    pieces = [(n, small[n]) for n in small_names]
    for l in range(2):
        pieces += [(f'dmod{l}', dmods[l][1, :N_MOD]), (f'dcmod{l}', dmods[l][0, :N_MOD])]
    second, lay1 = _pack(pieces)
    small_handle, small_token = _exchange_start("gather_small_start", [second], False, after=token)

    last, chain = dh, small_token
    for l in (1, 0):
        for key in ('ffn', 'w_out') + (('w_in',) if l == 1 else ()):
            landed(l, key, last)
            for n in (FFN_WEIGHTS if key == 'ffn' else (key,)):
                last = adam(f'l{l}_{n}', chain)
                chain = last[:8, :128]
    second_all = _exchange_wait("gather_small_wait", small_handle, last)[0]

    dsc_part = jnp.zeros((16, D), F32)
    for l in range(2):
        dm16 = _pad_rows(jnp.concatenate([_unpack(second_all, lay1, f'dmod{l}').reshape(N_DEV, N_MOD * D),
                                          jnp.sum(_unpack(second_all, lay1, f'dcmod{l}'), axis=0).reshape(1, N_MOD * D)], axis=0), 16)
        mine = lax.dynamic_slice_in_dim(dm16, me * nmod, nmod, axis=1)
        tk = _col_tile(D, 512)
        gw = _mm_tn(f"l{l}_dw_mod", s16, mine, pl.BlockSpec((16, tk), lambda kb, t: (0, kb)), pl.BlockSpec((16, nmod), lambda kb, t: (0, 0)),
                    jax.ShapeDtypeStruct((D, nmod), F32), pl.BlockSpec((tk, nmod), lambda kb, t: (kb, 0)), (tk, nmod), (D // tk, 1))
        grads[f'l{l}_w_mod'] = (gw[None], None)
        dsc_part += _mm_plain(f"l{l}_dsc", mine, A[f'l{l}_w_mod'], True)
        dmod_dev = _unpack(second_all, lay1, f'dmod{l}') + _unpack(second_all, lay1, f'dcmod{l}')
        grads[f'l{l}_b_mod'] = (dmod_dev.reshape(N_DEV, N_MOD * D), None)
    dsig = jax.nn.sigmoid(A['c_ctx'])
    dsilu = dsig * (1 + A['c_ctx'] * (1 - dsig))
    third_all = _exchange("gather_dsc", [dsc_part[8:9]], False)[0]
    grads['c_ctx'] = (third_all[:, 0] * dsilu[None], None)
    for n in small_names:
        g8 = _unpack(second_all, lay1, n)
        if n == 'l0_conv_w':
            g8 = lax.dynamic_slice_in_dim(g8, me * cw, cw, axis=2)
        grads[n] = (g8, None)

    adam('l0_w_mod')
    last = adam('l1_w_mod')
    gp, _ = _pack([(n, grads[n][0]) for n in rest], nlead=1)
    res = _adamw("adamw_small", gp, wp, mp, vp)
    for n in rest:
        out[n] = tuple(_unpack(r, layw, n) for r in res)
    landed(0, 'w_in', last)
    adam('l0_w_in')

    outs = [loss, grad_x]
    for k in range(4):
        outs += [out[n][k] for n in WEIGHT_NAMES]
    return tuple(outs)
```

```python
import math

import numpy as np
import jax
import jax.numpy as jnp
from jax import lax
from jax.experimental import pallas as pl
from jax.experimental.pallas import tpu as pltpu

F32 = jnp.float32
BF16 = jnp.bfloat16
HEAD_DIM = 128
AUX_WIDTH = 512
AUX_GROUPS = 4
POOL_HALF = (1, 2, 4, 8)
WINDOW = 128
GRID_W = 64
ROPE_THETA = 10000.0
EPS = 1e-6
NEG_INF = -1e30
ATT_SCALE = HEAD_DIM ** -0.5
LOG2_E = math.log2(math.e)
Q_SCALE = ATT_SCALE * LOG2_E
N_MOD = 6
N_DEV = 8
ROW_TILE = 256
BAND = ROW_TILE + 2 * WINDOW
ADAM_LR, ADAM_B1, ADAM_B2, ADAM_EPS, ADAM_WD, ADAM_STEP = 0.001, 0.9, 0.999, 1e-08, 0.01, 10
VMEM_LIMIT_MB = 56
ADAM_BLOCK_ELEMS = 3 << 17
MESH = pl.DeviceIdType.MESH
RESIDENT = pl.Buffered(buffer_count=1)

WEIGHT_NAMES = ['c_ctx', 'l0_norm1_g', 'l0_w_mod', 'l0_b_mod', 'l0_w_in', 'l0_q_norm_g', 'l0_k_norm_g', 'l0_conv_w', 'l0_w_out', 'l0_norm2_g', 'l0_w_gate', 'l0_w_up', 'l0_w_down', 'l1_norm1_g', 'l1_w_mod', 'l1_b_mod', 'l1_w_in', 'l1_q_norm_g', 'l1_k_norm_g', 'l1_sink', 'l1_pool_w', 'l1_pool_scale', 'l1_w_out', 'l1_norm2_g', 'l1_w_gate', 'l1_w_up', 'l1_w_down', 'final_norm_g']
INPUT_NAMES = (['x', 'c', 'ctx'] + WEIGHT_NAMES + ['loss_target'] + ['m_' + n for n in WEIGHT_NAMES]
               + ['v_' + n for n in WEIGHT_NAMES])
IN_WEIGHT = ('w_in',)
OUT_WEIGHT = ('w_out',)
MIXER_WEIGHTS = OUT_WEIGHT + IN_WEIGHT
FFN_WEIGHTS = ('w_down', 'w_gate', 'w_up')
TRANSPOSED = ('w_gate', 'w_up')
BIG_WEIGHTS = MIXER_WEIGHTS + FFN_WEIGHTS


def _params(vmem_mb=VMEM_LIMIT_MB):
    return pltpu.CompilerParams(vmem_limit_bytes=vmem_mb << 20)


def _row_tile(n, cap):
    best = None
    for t in range(16, min(n, cap) + 1, 16):
        if n % t == 0:
            best = t
    assert best is not None, (n, cap)
    return best


def _col_tile(n, cap):
    best = n
    for t in range(128, min(n, cap) + 1, 128):
        if n % t == 0:
            best = t
    return best


def _dot(a, b, ca, cb):
    return lax.dot_general(a, b, (((ca,), (cb,)), ((), ())), preferred_element_type=F32)


def _nn(a, b):
    return _dot(a, b, 1, 0)


def _nt(a, b):
    return _dot(a, b, 1, 1)


def _tn(a, b):
    return _dot(a, b, 0, 0)


def _bf(x):
    return x.astype(BF16)


def _exchange(name, arrs, scatter, after=None):
    n = len(arrs)
    extra = [] if after is None else [after]
    if scatter:
        out_shape = [jax.ShapeDtypeStruct(a.shape, a.dtype) for a in arrs]
    else:
        out_shape = [jax.ShapeDtypeStruct((N_DEV,) + a.shape, a.dtype) for a in arrs]

    def body(*refs):
        ins, outs = refs[:n], refs[n + len(extra):2 * n + len(extra)]
        send_sems, recv_sems, local_sems = refs[2 * n + len(extra):]
        x, y, c = lax.axis_index("x"), lax.axis_index("y"), lax.axis_index("c")
        me = 4 * x + 2 * y + c
        local, remote = [], []
        for a in range(n):
            own = ins[a].at[me] if scatter else ins[a]
            cp = pltpu.make_async_copy(own, outs[a].at[me], local_sems.at[a])
            cp.start()
            local.append(cp)
            for r in range(1, N_DEV):
                px = 1 - x if r & 4 else x
                py = 1 - y if r & 2 else y
                pc = 1 - c if r & 1 else c
                src = ins[a].at[4 * px + 2 * py + pc] if scatter else ins[a]
                cp = pltpu.make_async_remote_copy(
                    src_ref=src, dst_ref=outs[a].at[me], send_sem=send_sems.at[a, r - 1],
                    recv_sem=recv_sems.at[a, r - 1], device_id=(px, py, pc), device_id_type=MESH)
                cp.start()
                remote.append(cp)
        for cp in remote:
            cp.wait()
        for cp in local:
            cp.wait()

    any_spec = pl.BlockSpec(memory_space=pl.ANY)
    return pl.pallas_call(
        body, name=name, out_shape=out_shape,
        in_specs=[any_spec] * (n + len(extra)), out_specs=[any_spec] * n,
        scratch_shapes=[pltpu.SemaphoreType.DMA((n, N_DEV - 1)), pltpu.SemaphoreType.DMA((n, N_DEV - 1)),
                        pltpu.SemaphoreType.DMA((n,))],
    )(*arrs, *extra)


HBM_SPEC = pl.BlockSpec(memory_space=pltpu.HBM)
SEM_SPEC = pl.BlockSpec(memory_space=pltpu.SEMAPHORE)
EFFECT = pltpu.SideEffectType.DATAFLOW_SIDE_EFFECTING


def _split_copies(srcs, lands, send_sems, recv_sems, local_sems, scatter):
    x, y, c = lax.axis_index("x"), lax.axis_index("y"), lax.axis_index("c")
    me = 4 * x + 2 * y + c
    local, remote = [], []
    for a in range(len(srcs)):
        own = srcs[a].at[me] if scatter else srcs[a]
        local.append(pltpu.make_async_copy(own, lands[a].at[me], local_sems.at[a]))
        for r in range(1, N_DEV):
            px = 1 - x if r & 4 else x
            py = 1 - y if r & 2 else y
            pc = 1 - c if r & 1 else c
            src = srcs[a].at[4 * px + 2 * py + pc] if scatter else srcs[a]
            remote.append(pltpu.make_async_remote_copy(
                src_ref=src, dst_ref=lands[a].at[me], send_sem=send_sems.at[a * (N_DEV - 1) + r - 1],
                recv_sem=recv_sems.at[a * (N_DEV - 1) + r - 1], device_id=(px, py, pc), device_id_type=MESH))
    return local, remote


def _exchange_start(name, arrs, scatter, after=None):
    n = len(arrs)
    extra = [] if after is None else [after]
    shapes = [a.shape if scatter else (N_DEV,) + a.shape for a in arrs]
    lands = [pltpu.with_memory_space_constraint(lax.empty(s, a.dtype), pltpu.HBM) for s, a in zip(shapes, arrs)]
    srcs = [pltpu.with_memory_space_constraint(a, pltpu.HBM) for a in arrs]

    def body(*refs):
        src_refs, land_refs = refs[:n], refs[n:2 * n]
        send_sems, recv_sems, local_sems = refs[2 * n + len(extra):2 * n + len(extra) + 3]
        token = refs[-1]
        local, remote = _split_copies(src_refs, land_refs, send_sems, recv_sems, local_sems, scatter)
        for cp in local + remote:
            cp.start()
        token[...] = jnp.zeros_like(token)

    res = pl.pallas_call(
        body, name=name,
        out_shape=[pltpu.SemaphoreType.DMA((n * (N_DEV - 1),)), pltpu.SemaphoreType.DMA((n * (N_DEV - 1),)), pltpu.SemaphoreType.DMA((n,))]
        + [pltpu.HBM(a.shape, a.dtype) for a in arrs] + [pltpu.HBM(s, a.dtype) for s, a in zip(shapes, arrs)]
        + [jax.ShapeDtypeStruct((8, 128), F32)],
        in_specs=[HBM_SPEC] * (2 * n) + [pl.BlockSpec(memory_space=pl.ANY)] * len(extra),
        out_specs=[SEM_SPEC] * 3 + [HBM_SPEC] * (2 * n) + [pl.BlockSpec(memory_space=pltpu.VMEM)],
        input_output_aliases={i: 3 + i for i in range(2 * n)},
        compiler_params=pltpu.CompilerParams(has_side_effects=EFFECT),
    )(*srcs, *lands, *extra)
    return (scatter, res[:3], res[3:3 + n], res[3 + n:3 + 2 * n]), res[-1]


def _exchange_wait(name, handle, after):
    scatter, sems, srcs, lands = handle
    n = len(srcs)

    def body(*refs):
        src_refs, land_refs = refs[:n], refs[n:2 * n]
        send_sems, recv_sems, local_sems = refs[2 * n:2 * n + 3]
        local, remote = _split_copies(src_refs, land_refs, send_sems, recv_sems, local_sems, scatter)
        for cp in remote:
            cp.wait_send()
            cp.wait_recv()
        for cp in local:
            cp.wait()

    res = pl.pallas_call(
        body, name=name,
        out_shape=[pltpu.HBM(a.shape, a.dtype) for a in srcs] + [pltpu.HBM(a.shape, a.dtype) for a in lands],
        in_specs=[HBM_SPEC] * (2 * n) + [SEM_SPEC] * 3 + [pl.BlockSpec(memory_space=pl.ANY)], out_specs=[HBM_SPEC] * (2 * n),
        input_output_aliases={i: i for i in range(2 * n)},
        compiler_params=pltpu.CompilerParams(has_side_effects=EFFECT),
    )(*srcs, *lands, *sems, after)
    return list(res[n:])


FIRST_COPIES = 4
RELAY_COPIES = 3


def _gather_copies(srcs, lands, sems):
    send_sems, recv_sems, local_sems = sems[:3]
    x, y, c = lax.axis_index("x"), lax.axis_index("y"), lax.axis_index("c")
    me = 4 * x + 2 * y + c
    chips = [(1 - x, y), (x, 1 - y), (1 - x, 1 - y)]
    local, first, relay = [], [], []
    for a in range(len(srcs)):
        local.append(pltpu.make_async_copy(srcs[a], lands[a].at[me], local_sems.at[a]))
        targets = [(x, y, 1 - c)] + [(px, py, c) for px, py in chips]
        first.append([pltpu.make_async_remote_copy(
            src_ref=srcs[a], dst_ref=lands[a].at[me], send_sem=send_sems.at[FIRST_COPIES * a + k],
            recv_sem=recv_sems.at[FIRST_COPIES * a + k], device_id=t, device_id_type=MESH) for k, t in enumerate(targets)])
        if len(sems) > 3:
            rsend, rrecv = sems[3:]
            slots = [lands[a].at[4 * px + 2 * py + c] for px, py in chips]
            relay.append([pltpu.make_async_remote_copy(
                src_ref=slot, dst_ref=slot, send_sem=rsend.at[RELAY_COPIES * a + j], recv_sem=rrecv.at[RELAY_COPIES * a + j],
                device_id=(x, y, 1 - c), device_id_type=MESH) for j, slot in enumerate(slots)])
    return local, first, relay


def _gather_start(name, arrs, after=None):
    n = len(arrs)
    extra = [] if after is None else [after]
    lands = [pltpu.with_memory_space_constraint(lax.empty((N_DEV,) + a.shape, a.dtype), pltpu.HBM) for a in arrs]
    srcs = [pltpu.with_memory_space_constraint(a, pltpu.HBM) for a in arrs]

    def body(*refs):
        at = 2 * n + len(extra)
        local, first, _ = _gather_copies(refs[:n], refs[n:2 * n], refs[at:at + 3])
        for cp in local + [cp for cps in first for cp in cps]:
            cp.start()
        refs[-1][...] = jnp.zeros_like(refs[-1])

    res = pl.pallas_call(
        body, name=name,
        out_shape=[pltpu.SemaphoreType.DMA((FIRST_COPIES * n,)), pltpu.SemaphoreType.DMA((FIRST_COPIES * n,)), pltpu.SemaphoreType.DMA((n,))]
        + [pltpu.HBM(a.shape, a.dtype) for a in arrs] + [pltpu.HBM((N_DEV,) + a.shape, a.dtype) for a in arrs]
        + [jax.ShapeDtypeStruct((8, 128), F32)],
        in_specs=[HBM_SPEC] * (2 * n) + [pl.BlockSpec(memory_space=pl.ANY)] * len(extra),
        out_specs=[SEM_SPEC] * 3 + [HBM_SPEC] * (2 * n) + [pl.BlockSpec(memory_space=pltpu.VMEM)],
        input_output_aliases={i: 3 + i for i in range(2 * n)},
        compiler_params=pltpu.CompilerParams(has_side_effects=EFFECT),
    )(*srcs, *lands, *extra)
    return (list(res[:3]), list(res[3:3 + n]), list(res[3 + n:3 + 2 * n])), res[-1]


def _gather_relay(name, handle, after):
    sems, srcs, lands = handle
    n = len(srcs)

    def body(*refs):
        in_sems = refs[2 * n:2 * n + 3]
        out_sems = refs[2 * n + 4 + 2 * n:2 * n + 4 + 2 * n + 2]
        _, first, relay = _gather_copies(refs[:n], refs[n:2 * n], list(in_sems) + list(out_sems))
        for a in range(n):
            for j in range(RELAY_COPIES):
                first[a][1 + j].wait_recv()
                relay[a][j].start()
        refs[-1][...] = jnp.zeros_like(refs[-1])

    res = pl.pallas_call(
        body, name=name,
        out_shape=[pltpu.HBM(a.shape, a.dtype) for a in srcs] + [pltpu.HBM(a.shape, a.dtype) for a in lands]
        + [pltpu.SemaphoreType.DMA((RELAY_COPIES * n,)), pltpu.SemaphoreType.DMA((RELAY_COPIES * n,)), jax.ShapeDtypeStruct((8, 128), F32)],
        in_specs=[HBM_SPEC] * (2 * n) + [SEM_SPEC] * 3 + [pl.BlockSpec(memory_space=pl.ANY)],
        out_specs=[HBM_SPEC] * (2 * n) + [SEM_SPEC] * 2 + [pl.BlockSpec(memory_space=pltpu.VMEM)],
        input_output_aliases={i: i for i in range(2 * n)},
        compiler_params=pltpu.CompilerParams(has_side_effects=EFFECT),
    )(*srcs, *lands, *sems, after)
    return (sems + list(res[2 * n:2 * n + 2]), list(res[:n]), list(res[n:2 * n])), res[-1]


def _gather_wait(name, handle, after):
    sems, srcs, lands = handle
    n = len(srcs)

    def body(*refs):
        local, first, relay = _gather_copies(refs[:n], refs[n:2 * n], refs[2 * n:2 * n + 5])
        for a in range(n):
            for cp in first[a]:
                cp.wait_send()
            first[a][0].wait_recv()
            for cp in relay[a]:
                cp.wait_send()
                cp.wait_recv()
            local[a].wait()

    res = pl.pallas_call(
        body, name=name,
        out_shape=[pltpu.HBM(a.shape, a.dtype) for a in srcs] + [pltpu.HBM(a.shape, a.dtype) for a in lands],
        in_specs=[HBM_SPEC] * (2 * n) + [SEM_SPEC] * 5 + [pl.BlockSpec(memory_space=pl.ANY)], out_specs=[HBM_SPEC] * (2 * n),
        input_output_aliases={i: i for i in range(2 * n)},
        compiler_params=pltpu.CompilerParams(has_side_effects=EFFECT),
    )(*srcs, *lands, *sems, after)
    return list(res[n:])


def _dep(dep, grid_rank):
    if dep is None:
        return [], []
    return [dep], [pl.BlockSpec((8, 128), (lambda i, j: (0, 0)) if grid_rank == 2 else (lambda i: (0, 0)))]


def _mm_step(name, fn, ins, in_specs, out_shape, out_spec, grid, dep=None, res=None):
    n = len(ins)
    dep_ins, dep_specs = _dep(dep, len(grid))
    res_ins, res_specs, out_shapes, out_specs = [], [], out_shape, out_spec
    if res is not None:
        h, mod, row_idx, lc = res
        tm, tn = out_spec.block_shape
        res_ins = [h, mod]
        res_specs = [pl.BlockSpec((tm, tn), lambda j, i: (i, j)), pl.BlockSpec((2, 8, tn), lambda j, i: (0, 0, j))]
        out_shapes, out_specs = [out_shape, jax.ShapeDtypeStruct(h.shape, h.dtype)], [out_spec, res_specs[0]]

    def body(*refs):
        outs = refs[n + len(res_ins) + len(dep_ins):]
        acc = fn(*refs[:n])
        outs[0][...] = acc.astype(outs[0].dtype)
        if res is not None:
            h_ref, mod_ref = refs[n:n + 2]
            row = pl.program_id(1) * tm + lax.broadcasted_iota(jnp.int32, (tm, 1), 0)
            gate = jnp.where(row < lc, mod_ref[0, row_idx:row_idx + 1, :], mod_ref[1, row_idx:row_idx + 1, :])
            outs[1][...] = h_ref[...] + gate * acc

    return pl.pallas_call(body, name=name, grid=grid, in_specs=list(in_specs) + res_specs + dep_specs, out_specs=out_specs,
                          out_shape=out_shapes, compiler_params=_params())(*ins, *res_ins, *dep_ins)


def _mm_tn(name, a, b, a_spec, b_spec, out_shape, out_spec, acc_shape, grid):
    nk = grid[-1]
    kax = len(grid) - 1
    if nk == 1:
        def whole(a_ref, b_ref, o_ref):
            o_ref[...] = _tn(_bf(a_ref[...]), _bf(b_ref[...])).astype(o_ref.dtype)

        return pl.pallas_call(whole, name=name, grid=grid, in_specs=[a_spec, b_spec], out_specs=out_spec,
                              out_shape=out_shape, compiler_params=_params())(a, b)

    def body(a_ref, b_ref, o_ref, acc_ref):
        k = pl.program_id(kax)

        @pl.when(k == 0)
        def _():
            acc_ref[...] = jnp.zeros_like(acc_ref)

        acc_ref[...] += _tn(_bf(a_ref[...]), _bf(b_ref[...]))

        @pl.when(k == nk - 1)
        def _():
            o_ref[...] = acc_ref[...].astype(o_ref.dtype)

    return pl.pallas_call(body, name=name, grid=grid, in_specs=[a_spec, b_spec], out_specs=out_spec,
                          out_shape=out_shape, scratch_shapes=[pltpu.VMEM(acc_shape, F32)],
                          compiler_params=_params())(a, b)


def _shards_per_dot(J, n):
    return 2 if n % 256 and J % 2 == 0 else 1


def _mm_cols(name, a, w3, out_dtype=F32, dep=None):
    M, K = a.shape
    J, _, n = w3.shape
    g = _shards_per_dot(J, n)
    tm = M if g == 1 else _row_tile(M, M // g)

    def fn(a_ref, w_ref):
        w = w_ref[0] if g == 1 else jnp.concatenate([w_ref[s] for s in range(g)], axis=1)
        x = a_ref[...] if tm == M else a_ref[pl.ds(pl.multiple_of(pl.program_id(1) * tm, 16), tm), :]
        return _nn(_bf(x), w)

    return _mm_step(
        name, fn, [a, w3],
        [pl.BlockSpec((M, K), lambda j, i: (0, 0), pipeline_mode=RESIDENT), pl.BlockSpec((g, K, n), lambda j, i: (j, 0, 0))],
        jax.ShapeDtypeStruct((M, J * n), out_dtype), pl.BlockSpec((tm, g * n), lambda j, i: (i, j)), (J // g, M // tm), dep)


def _mm_plain(name, a, b, transpose_b, out_dtype=F32, tn=512, dep=None, res=None):
    M, K = a.shape
    N = b.shape[0] if transpose_b else b.shape[1]
    tn = _col_tile(N, tn)
    tm = M if res is None else _row_tile(M, 1088)
    a_spec = pl.BlockSpec((M, K), lambda j, i: (0, 0), pipeline_mode=RESIDENT)

    def rows(a_ref):
        return a_ref[...] if tm == M else a_ref[pl.ds(pl.multiple_of(pl.program_id(1) * tm, 16), tm), :]

    if transpose_b:
        b_spec = pl.BlockSpec((tn, K), lambda j, i: (j, 0))
        fn = lambda a_ref, b_ref: _nt(_bf(rows(a_ref)), _bf(b_ref[...]))
    else:
        b_spec = pl.BlockSpec((K, tn), lambda j, i: (0, j))
        fn = lambda a_ref, b_ref: _nn(_bf(rows(a_ref)), _bf(b_ref[...]))
    return _mm_step(name, fn, [a, b], [a_spec, b_spec],
                    jax.ShapeDtypeStruct((M, N), out_dtype), pl.BlockSpec((tm, tn), lambda j, i: (i, j)),
                    (N // tn, M // tm), dep, res)


def _mm_shards_nn(name, a3, w3, tn=512, dep=None, res=None):
    J, M, k = a3.shape
    N = w3.shape[2]
    tm = _row_tile(M, 544)
    tn = _col_tile(N, tn)

    def fn(a_ref, w_ref):
        acc = _nn(a_ref[0], w_ref[0])
        for j in range(1, J):
            acc += _nn(a_ref[j], w_ref[j])
        return acc

    return _mm_step(name, fn, [a3, w3],
                    [pl.BlockSpec((J, tm, k), lambda jn, i: (0, i, 0)), pl.BlockSpec((J, k, tn), lambda jn, i: (0, 0, jn))],
                    jax.ShapeDtypeStruct((M, N), F32), pl.BlockSpec((tm, tn), lambda jn, i: (i, jn)), (N // tn, M // tm), dep, res)


def _mm_shards_nn2(name, a3, w3a, b3, w3b, tn=512):
    J, M, k = a3.shape
    N = w3a.shape[2]
    tm = _row_tile(M, 544)
    tn = _col_tile(N, tn)

    def fn(a_ref, wa_ref, b_ref, wb_ref):
        acc = _nn(a_ref[0], wa_ref[0]) + _nn(b_ref[0], wb_ref[0])
        for j in range(1, J):
            acc += _nn(a_ref[j], wa_ref[j]) + _nn(b_ref[j], wb_ref[j])
        return acc

    act = pl.BlockSpec((J, tm, k), lambda jn, i: (0, i, 0))
    wsp = pl.BlockSpec((J, k, tn), lambda jn, i: (0, 0, jn))
    return _mm_step(name, fn, [a3, w3a, b3, w3b], [act, wsp, act, wsp],
                    jax.ShapeDtypeStruct((M, N), F32), pl.BlockSpec((tm, tn), lambda jn, i: (i, jn)), (N // tn, M // tm))


def _mm_cols_nt(name, a, w3, tn=512):
    M = a.shape[0]
    J, N, n = w3.shape
    tm = _row_tile(M, 544)
    tn = _col_tile(N, tn)
    g = _shards_per_dot(J, n)

    def fn(a_ref, w_ref):
        acc = None
        for j in range(0, J, g):
            w = w_ref[j] if g == 1 else jnp.concatenate([w_ref[j + s] for s in range(g)], axis=1)
            part = _nt(a_ref[:, j * n:(j + g) * n], w)
            acc = part if acc is None else acc + part
        return acc

    return _mm_step(name, fn, [a, w3],
                    [pl.BlockSpec((tm, J * n), lambda jn, i: (i, 0)), pl.BlockSpec((J, tn, n), lambda jn, i: (0, jn, 0))],
                    jax.ShapeDtypeStruct((M, N), F32), pl.BlockSpec((tm, tn), lambda jn, i: (i, jn)), (N // tn, M // tm))


def _wgrad_cols(name, a, b, J):
    T, K = a.shape
    n = b.shape[1] // J
    return _mm_tn(name, a, b, pl.BlockSpec((T, K), lambda j, t: (0, 0), pipeline_mode=RESIDENT), pl.BlockSpec((T, n), lambda j, t: (0, j)),
                  jax.ShapeDtypeStruct((J, K, n), BF16), pl.BlockSpec((None, K, n), lambda j, t: (j, 0, 0)), (K, n), (J, 1))


def _wgrad_rows(name, a, b, tk=512):
    T, K = a.shape
    N = b.shape[1]
    tk = _col_tile(K, tk)
    return _mm_tn(name, a, b, pl.BlockSpec((T, tk), lambda kb, t: (0, kb)), pl.BlockSpec((T, N), lambda kb, t: (0, 0), pipeline_mode=RESIDENT),
                  jax.ShapeDtypeStruct((K, N), BF16), pl.BlockSpec((tk, N), lambda kb, t: (kb, 0)), (tk, N), (K // tk, 1))


def _wgrad_down(name, a3, b):
    J, T, k = a3.shape
    N = b.shape[1]
    return _mm_tn(name, a3, b, pl.BlockSpec((None, T, k), lambda j, t: (j, 0, 0)),
                  pl.BlockSpec((T, N), lambda j, t: (0, 0), pipeline_mode=RESIDENT),
                  jax.ShapeDtypeStruct((J, k, N), BF16), pl.BlockSpec((None, k, N), lambda j, t: (j, 0, 0)), (k, N), (J, 1))


def _seg(i):
    return jnp.minimum(i, 1)


def _rstd(x):
    return lax.rsqrt(jnp.mean(x * x, axis=-1, keepdims=True) + EPS)


def _norm_mod(name, h, g, mod, which):
    T, D = h.shape

    def body(h_ref, g_ref, mod_ref, o_ref):
        x = h_ref[...]
        n = x * _rstd(x) * g_ref[...]
        shift = mod_ref[3 * which:3 * which + 1, :]
        scale = mod_ref[3 * which + 1:3 * which + 2, :]
        o_ref[...] = (n * (1 + scale) + shift).astype(o_ref.dtype)

    row = pl.BlockSpec((ROW_TILE, D), lambda i: (i, 0))
    return pl.pallas_call(
        body, name=name, grid=(T // ROW_TILE,),
        in_specs=[row, pl.BlockSpec((1, D), lambda i: (0, 0)), pl.BlockSpec((None, 8, D), lambda i: (_seg(i), 0, 0))],
        out_specs=row, out_shape=jax.ShapeDtypeStruct((T, D), BF16), compiler_params=_params())(h, g, mod)


def _norm_mod_bwd(name, dxn, h, g, mod, which, dres, latent_only=False, gate=None):
    T, D = h.shape
    n_gate = 0 if gate is None else 2

    def body(dxn_ref, h_ref, g_ref, mod_ref, dres_ref, *rest):
        dh_ref, dmod_ref, dg_ref = rest[n_gate:n_gate + 3]
        i = pl.program_id(0)
        x = h_ref[...]
        r = _rstd(x)
        xhat = x * r
        g = g_ref[...]
        n = xhat * g
        scale = mod_ref[3 * which + 1:3 * which + 2, :]
        dxn = dxn_ref[...]
        dn = dxn * (1 + scale)
        dxh = dn * g
        dh = dres_ref[...] + r * (dxh - xhat * jnp.mean(dxh * xhat, axis=-1, keepdims=True))
        if latent_only:
            @pl.when(i > 0)
            def _():
                dh_ref[...] = dh
        else:
            dh_ref[...] = dh

        @pl.when(i <= 1)
        def _():
            dmod_ref[...] = jnp.zeros_like(dmod_ref)

        @pl.when(i == 0)
        def _():
            dg_ref[...] = jnp.zeros_like(dg_ref)

        dmod_ref[3 * which:3 * which + 1, :] += jnp.sum(dxn, axis=0, keepdims=True)
        dmod_ref[3 * which + 1:3 * which + 2, :] += jnp.sum(dxn * n, axis=0, keepdims=True)
        dg_ref[0:1, :] += jnp.sum(dn * xhat, axis=0, keepdims=True)

        if gate is not None:
            y_ref, gmod_ref = rest[:2]
            dy_ref, dgmod_ref = rest[5:7]
            dy_ref[...] = (dh * gmod_ref[gate[2]:gate[2] + 1, :]).astype(dy_ref.dtype)

            @pl.when(i <= 1)
            def _():
                dgmod_ref[...] = jnp.zeros_like(dgmod_ref)

            dgmod_ref[gate[2]:gate[2] + 1, :] += jnp.sum(dh * y_ref[...], axis=0, keepdims=True)

    row = pl.BlockSpec((ROW_TILE, D), lambda i: (i, 0))
    modspec = pl.BlockSpec((None, 8, D), lambda i: (_seg(i), 0, 0))
    dh_rows = T - ROW_TILE if latent_only else T
    dh_spec = pl.BlockSpec((ROW_TILE, D), lambda i: (jnp.maximum(i - 1, 0), 0)) if latent_only else row
    gate_ins, gate_specs, gate_outs, gate_shapes = [], [], [], []
    if gate is not None:
        gate_ins, gate_specs = [gate[0], gate[1]], [row, modspec]
        gate_outs, gate_shapes = [row, modspec], [jax.ShapeDtypeStruct((T, D), BF16), jax.ShapeDtypeStruct((2, 8, D), F32)]
    return pl.pallas_call(
        body, name=name, grid=(T // ROW_TILE,),
        in_specs=[row, row, pl.BlockSpec((1, D), lambda i: (0, 0)), modspec, row] + gate_specs,
        out_specs=[dh_spec, modspec, pl.BlockSpec((8, D), lambda i: (0, 0))] + gate_outs,
        out_shape=[jax.ShapeDtypeStruct((dh_rows, D), F32), jax.ShapeDtypeStruct((2, 8, D), F32), jax.ShapeDtypeStruct((8, D), F32)] + gate_shapes,
        compiler_params=_params())(dxn, h, g, mod, dres, *gate_ins)


def _gate_bwd(name, dh, y, mod, row_idx):
    T, D = dh.shape

    def body(dh_ref, y_ref, mod_ref, dy_ref, dmod_ref):
        i = pl.program_id(0)
        dh = dh_ref[...]
        dy_ref[...] = (dh * mod_ref[row_idx:row_idx + 1, :]).astype(dy_ref.dtype)

        @pl.when(i <= 1)
        def _():
            dmod_ref[...] = jnp.zeros_like(dmod_ref)

        dmod_ref[row_idx:row_idx + 1, :] += jnp.sum(dh * y_ref[...], axis=0, keepdims=True)

    row = pl.BlockSpec((ROW_TILE, D), lambda i: (i, 0))
    modspec = pl.BlockSpec((None, 8, D), lambda i: (_seg(i), 0, 0))
    return pl.pallas_call(
        body, name=name, grid=(T // ROW_TILE,), in_specs=[row, row, modspec], out_specs=[row, modspec],
        out_shape=[jax.ShapeDtypeStruct((T, D), BF16), jax.ShapeDtypeStruct((2, 8, D), F32)],
        compiler_params=_params())(dh, y, mod)


def _rot(y):
    lane = lax.broadcasted_iota(jnp.int32, y.shape, 1)
    return jnp.where((lane & 32) == 0, pltpu.roll(y, 96, 1), pltpu.roll(y, 32, 1))


def _qk_prep(name, P, q_g, k_g, rope_c, rope_s, cfg):
    T = P.shape[0]
    ATT, KVW = cfg['ATT'], cfg['KVW']

    def body(q_ref, k_ref, v_ref, qg_ref, kg_ref, c_ref, s_ref, qo_ref, ko_ref, vo_ref):
        cc, ss = c_ref[...], s_ref[...]

        def head(x, g):
            y = x * _rstd(x) * g
            return y * cc + _rot(y) * ss

        for hh in range(ATT // HEAD_DIM):
            sl = slice(hh * HEAD_DIM, (hh + 1) * HEAD_DIM)
            qo_ref[:, sl] = (head(q_ref[:, sl], qg_ref[...]) * Q_SCALE).astype(qo_ref.dtype)
        for hh in range(KVW // HEAD_DIM):
            sl = slice(hh * HEAD_DIM, (hh + 1) * HEAD_DIM)
            ko_ref[:, sl] = head(k_ref[:, sl], kg_ref[...]).astype(ko_ref.dtype)
        vo_ref[...] = v_ref[...].astype(vo_ref.dtype)

    kb = ATT // KVW
    gain = pl.BlockSpec((1, HEAD_DIM), lambda i: (0, 0))
    tab = pl.BlockSpec((ROW_TILE, HEAD_DIM), lambda i: (i, 0))
    qs = pl.BlockSpec((ROW_TILE, ATT), lambda i: (i, 0))
    ks = pl.BlockSpec((ROW_TILE, KVW), lambda i: (i, 0))
    return pl.pallas_call(
        body, name=name, grid=(T // ROW_TILE,),
        in_specs=[qs, pl.BlockSpec((ROW_TILE, KVW), lambda i: (i, kb)), pl.BlockSpec((ROW_TILE, KVW), lambda i: (i, kb + 1)),
                  gain, gain, tab, tab],
        out_specs=[qs, ks, ks],
        out_shape=[jax.ShapeDtypeStruct((T, ATT), BF16), jax.ShapeDtypeStruct((T, KVW), BF16), jax.ShapeDtypeStruct((T, KVW), BF16)],
        compiler_params=_params())(P, P, P, q_g, k_g, rope_c, rope_s)


def _qk_prep_bwd(name, dqr, dkr, P, q_g, k_g, rope_c, rope_s, cfg):
    T = P.shape[0]
    ATT, KVW = cfg['ATT'], cfg['KVW']

    def body(dq_ref, dk_ref, q_ref, k_ref, qg_ref, kg_ref, c_ref, s_ref, dqo_ref, dko_ref, dqg_ref, dkg_ref):
        i = pl.program_id(0)
        cc, ss = c_ref[...], s_ref[...]

        @pl.when(i == 0)
        def _():
            dqg_ref[...] = jnp.zeros_like(dqg_ref)
            dkg_ref[...] = jnp.zeros_like(dkg_ref)

        def head(x, g, dout):
            dy = dout * cc + _rot(dout * ss)
            r = _rstd(x)
            xhat = x * r
            dxh = dy * g
            dx = r * (dxh - xhat * jnp.mean(dxh * xhat, axis=-1, keepdims=True))
            return dx, jnp.sum(dy * xhat, axis=0, keepdims=True)

        dg = jnp.zeros((1, HEAD_DIM), F32)
        for hh in range(ATT // HEAD_DIM):
            sl = slice(hh * HEAD_DIM, (hh + 1) * HEAD_DIM)
            dx, d = head(q_ref[:, sl], qg_ref[...], dq_ref[:, sl] * ATT_SCALE)
            dqo_ref[:, sl] = dx.astype(dqo_ref.dtype)
            dg += d
        dqg_ref[0:1, :] += dg
        dg = jnp.zeros((1, HEAD_DIM), F32)
        for hh in range(KVW // HEAD_DIM):
            sl = slice(hh * HEAD_DIM, (hh + 1) * HEAD_DIM)
            dx, d = head(k_ref[:, sl], kg_ref[...], dk_ref[:, sl] * (1.0 / LOG2_E))
            dko_ref[:, sl] = dx.astype(dko_ref.dtype)
            dg += d
        dkg_ref[0:1, :] += dg

    kb = ATT // KVW
    gain = pl.BlockSpec((1, HEAD_DIM), lambda i: (0, 0))
    dgain = pl.BlockSpec((8, HEAD_DIM), lambda i: (0, 0))
    tab = pl.BlockSpec((ROW_TILE, HEAD_DIM), lambda i: (i, 0))
    qs = pl.BlockSpec((ROW_TILE, ATT), lambda i: (i, 0))
    ks = pl.BlockSpec((ROW_TILE, KVW), lambda i: (i, 0))
    return pl.pallas_call(
        body, name=name, grid=(T // ROW_TILE,),
        in_specs=[qs, ks, qs, pl.BlockSpec((ROW_TILE, KVW), lambda i: (i, kb)), gain, gain, tab, tab],
        out_specs=[qs, ks, dgain, dgain],
        out_shape=[jax.ShapeDtypeStruct((T, ATT), BF16), jax.ShapeDtypeStruct((T, KVW), BF16),
                   jax.ShapeDtypeStruct((8, HEAD_DIM), F32), jax.ShapeDtypeStruct((8, HEAD_DIM), F32)],
        compiler_params=_params())(dqr, dkr, P, P, q_g, k_g, rope_c, rope_s)


def _att_specs(T, G):
    qs = pl.BlockSpec((ROW_TILE, G * HEAD_DIM), lambda h, i: (i, h))
    kvs = pl.BlockSpec((T, HEAD_DIM), lambda h, i: (0, h))
    return qs, kvs


def _attn_dense_fwd(name, q, k, v, cfg):
    T, G, Lc = q.shape[0], cfg['G'], cfg['Lc']

    def body(q_ref, k_ref, v_ref, o_ref, lse_ref):
        def attend(rows):
            kk, vv = k_ref[0:rows, :], v_ref[0:rows, :]
            for g in range(G):
                sl = slice(g * HEAD_DIM, (g + 1) * HEAD_DIM)
                s = _nt(q_ref[:, sl], kk)
                m = jnp.max(s, axis=1, keepdims=True)
                p = jnp.exp2(s - m)
                l = jnp.sum(p, axis=1, keepdims=True)
                o_ref[:, sl] = _nn(_bf(p), vv) / l
                lse_ref[:, sl] = jnp.broadcast_to(m + jnp.log2(l), (ROW_TILE, HEAD_DIM))

        @pl.when(pl.program_id(1) == 0)
        def _():
            attend(Lc)

        @pl.when(pl.program_id(1) > 0)
        def _():
            attend(T)

    qs, kvs = _att_specs(T, G)
    return pl.pallas_call(
        body, name=name, grid=(cfg['NKV'], T // ROW_TILE), in_specs=[qs, kvs, kvs], out_specs=[qs, qs],
        out_shape=[jax.ShapeDtypeStruct(q.shape, F32), jax.ShapeDtypeStruct(q.shape, F32)],
        compiler_params=_params())(q, k, v)


def _attn_dense_bwd(name, q, k, v, o, lse, dmix, cfg):
    T, G, Lc = q.shape[0], cfg['G'], cfg['Lc']

    def body(q_ref, k_ref, v_ref, o_ref, lse_ref, do_ref, dq_ref, dk_ref, dv_ref):
        i = pl.program_id(1)

        @pl.when(i == 0)
        def _():
            dk_ref[...] = jnp.zeros_like(dk_ref)
            dv_ref[...] = jnp.zeros_like(dv_ref)

        def attend(rows):
            kk, vv = k_ref[0:rows, :], v_ref[0:rows, :]
            for g in range(G):
                sl = slice(g * HEAD_DIM, (g + 1) * HEAD_DIM)
                qg, do = q_ref[:, sl], do_ref[:, sl]
                delta = jnp.sum(do * o_ref[:, sl], axis=1, keepdims=True)
                p = jnp.exp2(_nt(qg, kk) - lse_ref[:, g * HEAD_DIM:g * HEAD_DIM + 1])
                dob = _bf(do)
                dv_ref[0:rows, :] += _tn(_bf(p), dob)
                ds = _bf(p * (_nt(dob, vv) - delta))
                dq_ref[:, sl] = _nn(ds, kk)
                dk_ref[0:rows, :] += _tn(ds, qg)

        @pl.when(i == 0)
        def _():
            attend(Lc)

        @pl.when(i > 0)
        def _():
            attend(T)

    qs, kvs = _att_specs(T, G)
    return pl.pallas_call(
        body, name=name, grid=(cfg['NKV'], T // ROW_TILE), in_specs=[qs, kvs, kvs, qs, qs, qs], out_specs=[qs, kvs, kvs],
        out_shape=[jax.ShapeDtypeStruct(q.shape, F32), jax.ShapeDtypeStruct(k.shape, F32), jax.ShapeDtypeStruct(k.shape, F32)],
        compiler_params=_params())(q, k, v, o, lse, dmix)


def _band(i, T, Lc):
    start = pl.multiple_of(jnp.clip(WINDOW + (i - 1) * ROW_TILE, 0, T - BAND), WINDOW)
    qpos = (i - 1) * ROW_TILE + lax.broadcasted_iota(jnp.int32, (ROW_TILE, 1), 0)
    kpos = start - Lc + lax.broadcasted_iota(jnp.int32, (1, BAND), 1)
    ok = (jnp.abs(kpos - qpos) <= WINDOW) & (kpos >= 0) & (i > 0)
    return start, jnp.where(ok, 0.0, NEG_INF).astype(F32)


def _attn_win_fwd(name, q, k, v, sink, cfg):
    T, G, Lc = q.shape[0], cfg['G'], cfg['Lc']

    def body(sink_ref, q_ref, k_ref, v_ref, o_ref, lse_ref):
        h, i = pl.program_id(0), pl.program_id(1)
        start, bias = _band(i, T, Lc)
        kc, vc = k_ref[0:Lc, :], v_ref[0:Lc, :]
        kb, vb = k_ref[pl.ds(start, BAND), :], v_ref[pl.ds(start, BAND), :]
        for g in range(G):
            sl = slice(g * HEAD_DIM, (g + 1) * HEAD_DIM)
            qg = q_ref[:, sl]
            sk = sink_ref[h * G + g] * LOG2_E
            sc = _nt(qg, kc)
            sb = _nt(qg, kb) + bias
            m = jnp.maximum(jnp.maximum(jnp.max(sc, axis=1, keepdims=True), jnp.max(sb, axis=1, keepdims=True)), sk)
            pc, pb = jnp.exp2(sc - m), jnp.exp2(sb - m)
            l = jnp.sum(pc, axis=1, keepdims=True) + jnp.sum(pb, axis=1, keepdims=True) + jnp.exp2(sk - m)
            o_ref[:, sl] = (_nn(_bf(pc), vc) + _nn(_bf(pb), vb)) / l
            lse_ref[:, sl] = jnp.broadcast_to(m + jnp.log2(l), (ROW_TILE, HEAD_DIM))

    qs, kvs = _att_specs(T, G)
    return pl.pallas_call(
        body, name=name, grid=(cfg['NKV'], T // ROW_TILE),
        in_specs=[pl.BlockSpec(memory_space=pltpu.SMEM), qs, kvs, kvs], out_specs=[qs, qs],
        out_shape=[jax.ShapeDtypeStruct(q.shape, F32), jax.ShapeDtypeStruct(q.shape, F32)],
        compiler_params=_params())(sink, q, k, v)


def _attn_win_bwd(name, q, k, v, o, lse, dmix, sink, cfg):
    T, G, Lc = q.shape[0], cfg['G'], cfg['Lc']

    def body(sink_ref, q_ref, k_ref, v_ref, o_ref, lse_ref, do_ref, dq_ref, dk_ref, dv_ref, dsink_ref):
        h, i = pl.program_id(0), pl.program_id(1)
        start, bias = _band(i, T, Lc)
        kc, vc = k_ref[0:Lc, :], v_ref[0:Lc, :]
        kb, vb = k_ref[pl.ds(start, BAND), :], v_ref[pl.ds(start, BAND), :]

        @pl.when(i == 0)
        def _():
            dk_ref[...] = jnp.zeros_like(dk_ref)
            dv_ref[...] = jnp.zeros_like(dv_ref)
            dsink_ref[...] = jnp.zeros_like(dsink_ref)

        for g in range(G):
            sl = slice(g * HEAD_DIM, (g + 1) * HEAD_DIM)
            qg, do = q_ref[:, sl], do_ref[:, sl]
            lse = lse_ref[:, g * HEAD_DIM:g * HEAD_DIM + 1]
            delta = jnp.sum(do * o_ref[:, sl], axis=1, keepdims=True)
            pc = jnp.exp2(_nt(qg, kc) - lse)
            pb = jnp.exp2(_nt(qg, kb) + bias - lse)
            ps = jnp.exp2(sink_ref[h * G + g] * LOG2_E - lse)
            dob = _bf(do)
            dv_ref[0:Lc, :] += _tn(_bf(pc), dob)
            dv_ref[pl.ds(start, BAND), :] += _tn(_bf(pb), dob)
            dsc = _bf(pc * (_nt(dob, vc) - delta))
            dsb = _bf(pb * (_nt(dob, vb) - delta))
            dq_ref[:, sl] = _nn(dsc, kc) + _nn(dsb, kb)
            dk_ref[0:Lc, :] += _tn(dsc, qg)
            dk_ref[pl.ds(start, BAND), :] += _tn(dsb, qg)
            dsk = jnp.where(i > 0, -jnp.sum(ps * delta, axis=0, keepdims=True), 0.0)
            dsink_ref[:, sl] += jnp.broadcast_to(dsk, (8, HEAD_DIM))

    qs, kvs = _att_specs(T, G)
    return pl.pallas_call(
        body, name=name, grid=(cfg['NKV'], T // ROW_TILE),
        in_specs=[pl.BlockSpec(memory_space=pltpu.SMEM), qs, kvs, kvs, qs, qs, qs],
        out_specs=[qs, kvs, kvs, pl.BlockSpec((None, 8, G * HEAD_DIM), lambda h, i: (h, 0, 0))],
        out_shape=[jax.ShapeDtypeStruct(q.shape, F32), jax.ShapeDtypeStruct(k.shape, F32), jax.ShapeDtypeStruct(k.shape, F32),
                   jax.ShapeDtypeStruct((cfg['NKV'], 8, G * HEAD_DIM), F32)],
        compiler_params=_params())(sink, q, k, v, o, lse, dmix)


def _seq_pos(T, Lc):
    row = lax.broadcasted_iota(jnp.int32, (T, 1), 0)
    return jnp.where(row < Lc, row, row - Lc), jnp.where(row < Lc, Lc, T - Lc)


def _fw(x, k, pos, seglen):
    return jnp.where(pos + k < seglen, pltpu.roll(x, x.shape[0] - k, 0), 0.0)


def _bw(x, k, pos):
    return jnp.where(pos - k >= 0, pltpu.roll(x, k, 0), 0.0)


def _conv_fwd(name, P, conv_w8, cfg):
    T, Lc = P.shape[0], cfg['Lc']
    cb = (cfg['ATT'] + 2 * cfg['KVW']) // HEAD_DIM
    na = AUX_WIDTH // HEAD_DIM

    def body(gb_ref, gc_ref, u_ref, w_ref, o_ref):
        pos, seglen = _seq_pos(T, Lc)
        z = gc_ref[...] * u_ref[...]
        conv = w_ref[0:1, :] * _bw(z, 1, pos) + w_ref[1:2, :] * z + w_ref[2:3, :] * _fw(z, 1, pos, seglen)
        o_ref[...] = gb_ref[...] * conv

    col = lambda off: pl.BlockSpec((T, HEAD_DIM), lambda c: (0, cb + off + c))
    return pl.pallas_call(
        body, name=name, grid=(na,),
        in_specs=[col(0), col(na), col(2 * na), pl.BlockSpec((8, HEAD_DIM), lambda c: (0, c))],
        out_specs=pl.BlockSpec((T, HEAD_DIM), lambda c: (0, c)),
        out_shape=jax.ShapeDtypeStruct((T, AUX_WIDTH), F32), compiler_params=_params())(P, P, P, conv_w8)


def _conv_bwd(name, P, conv_w8, dmix, cfg):
    T, Lc = P.shape[0], cfg['Lc']
    cb = (cfg['ATT'] + 2 * cfg['KVW']) // HEAD_DIM
    ob = cfg['ATT'] // HEAD_DIM
    na = AUX_WIDTH // HEAD_DIM

    def body(gb_ref, gc_ref, u_ref, w_ref, do_ref, dgb_ref, dgc_ref, du_ref, dw_ref):
        pos, seglen = _seq_pos(T, Lc)
        gc, u, do = gc_ref[...], u_ref[...], do_ref[...]
        z = gc * u
        zm, zp = _bw(z, 1, pos), _fw(z, 1, pos, seglen)
        w0, w1, w2 = w_ref[0:1, :], w_ref[1:2, :], w_ref[2:3, :]
        dgb_ref[...] = (do * (w0 * zm + w1 * z + w2 * zp)).astype(dgb_ref.dtype)
        dc = do * gb_ref[...]
        dz = w0 * _fw(dc, 1, pos, seglen) + w1 * dc + w2 * _bw(dc, 1, pos)
        dgc_ref[...] = (dz * u).astype(dgc_ref.dtype)
        du_ref[...] = (dz * gc).astype(du_ref.dtype)
        dw_ref[...] = jnp.zeros_like(dw_ref)
        dw_ref[0:1, :] = jnp.sum(dc * zm, axis=0, keepdims=True)
        dw_ref[1:2, :] = jnp.sum(dc * z, axis=0, keepdims=True)
        dw_ref[2:3, :] = jnp.sum(dc * zp, axis=0, keepdims=True)

    col = lambda off: pl.BlockSpec((T, HEAD_DIM), lambda c: (0, cb + off + c))
    wspec = pl.BlockSpec((8, HEAD_DIM), lambda c: (0, c))
    ocol = lambda off: pl.BlockSpec((T, HEAD_DIM), lambda c: (0, off + c))
    return pl.pallas_call(
        body, name=name, grid=(na,),
        in_specs=[col(0), col(na), col(2 * na), wspec, ocol(ob)],
        out_specs=[ocol(0), ocol(0), ocol(0), wspec],
        out_shape=[jax.ShapeDtypeStruct((T, AUX_WIDTH), BF16)] * 3 + [jax.ShapeDtypeStruct((8, AUX_WIDTH), F32)],
        compiler_params=_params())(P, P, P, conv_w8, dmix)


def _window_sums(x, half, pos, seglen):
    fwd, bwd = x, x
    s = 1
    while s < half:
        fwd = fwd + _fw(fwd, s, pos, seglen)
        bwd = bwd + _bw(bwd, s, pos)
        s *= 2
    return fwd, bwd


def _pooled(u, half, pos, seglen):
    fwd, bwd = _window_sums(u, half, pos, seglen)
    cnt = (jnp.minimum(pos + half, seglen) - jnp.maximum(pos - half, 0)).astype(F32)
    return (fwd + _bw(bwd, 1, pos)) / cnt - u, cnt


def _pool_fwd(name, P, pool_w, pool_scale, cfg):
    T, Lc = P.shape[0], cfg['Lc']
    cb = (cfg['ATT'] + 2 * cfg['KVW']) // HEAD_DIM

    def body(u_ref, w_ref, s_ref, o_ref):
        g = pl.program_id(0)
        pos, seglen = _seq_pos(T, Lc)
        for k, half in enumerate(POOL_HALF):
            @pl.when(g == k)
            def _(half=half):
                pooled, _ = _pooled(u_ref[...], half, pos, seglen)
                o_ref[...] = _nn(_bf(pooled), _bf(w_ref[...])) * s_ref[...]

    return pl.pallas_call(
        body, name=name, grid=(AUX_GROUPS,),
        in_specs=[pl.BlockSpec((T, HEAD_DIM), lambda g: (0, cb + g)), pl.BlockSpec((None, HEAD_DIM, HEAD_DIM), lambda g: (g, 0, 0)),
                  pl.BlockSpec((1, HEAD_DIM), lambda g: (0, g))],
        out_specs=pl.BlockSpec((T, HEAD_DIM), lambda g: (0, g)),
        out_shape=jax.ShapeDtypeStruct((T, AUX_WIDTH), F32), compiler_params=_params())(P, pool_w, pool_scale)


def _pool_bwd(name, P, pool_w, pool_scale, dmix, cfg):
    T, Lc = P.shape[0], cfg['Lc']
    cb = (cfg['ATT'] + 2 * cfg['KVW']) // HEAD_DIM
    ob = cfg['ATT'] // HEAD_DIM

    def body(u_ref, w_ref, s_ref, do_ref, du_ref, dw_ref, ds_ref):
        g = pl.program_id(0)
        pos, seglen = _seq_pos(T, Lc)
        for k, half in enumerate(POOL_HALF):
            @pl.when(g == k)
            def _(half=half):
                do = do_ref[...]
                pooled, cnt = _pooled(u_ref[...], half, pos, seglen)
                wb = _bf(w_ref[...])
                mixed = _nn(_bf(pooled), wb)
                ds_ref[...] = jnp.broadcast_to(jnp.sum(do * mixed, axis=0, keepdims=True), ds_ref.shape)
                dmixed = _bf(do * s_ref[...])
                dw_ref[...] = _tn(_bf(pooled), dmixed)
                dpooled = _nt(dmixed, wb)
                e = dpooled / cnt
                fwd, bwd = _window_sums(e, half, pos, seglen)
                adj = fwd + _fw(e, half, pos, seglen) + _bw(bwd, 1, pos) - _bw(e, half, pos)
                du_ref[...] = (adj - dpooled).astype(du_ref.dtype)

    wspec = pl.BlockSpec((None, HEAD_DIM, HEAD_DIM), lambda g: (g, 0, 0))
    return pl.pallas_call(
        body, name=name, grid=(AUX_GROUPS,),
        in_specs=[pl.BlockSpec((T, HEAD_DIM), lambda g: (0, cb + g)), wspec, pl.BlockSpec((1, HEAD_DIM), lambda g: (0, g)),
                  pl.BlockSpec((T, HEAD_DIM), lambda g: (0, ob + g))],
        out_specs=[pl.BlockSpec((T, HEAD_DIM), lambda g: (0, g)), wspec, pl.BlockSpec((8, HEAD_DIM), lambda g: (0, g))],
        out_shape=[jax.ShapeDtypeStruct((T, AUX_WIDTH), BF16), jax.ShapeDtypeStruct(pool_w.shape, F32),
                   jax.ShapeDtypeStruct((8, AUX_WIDTH), F32)],
        compiler_params=_params())(P, pool_w, pool_scale, dmix)


def _ffn_up(name, hn, wg3, wu3, dep=None):
    T, D = hn.shape
    J, k, _ = wg3.shape
    tm = _row_tile(T, 1088)
    dep_ins, dep_specs = _dep(dep, 2)

    def body(x_ref, wg_ref, wu_ref, *rest):
        s_ref, ud_ref, a_ref = rest[len(dep_ins):]
        x = x_ref[pl.ds(pl.multiple_of(pl.program_id(1) * tm, 16), tm), :]
        g, u = _nt(x, wg_ref[...]), _nt(x, wu_ref[...])
        sig = jax.nn.sigmoid(g)
        silu = g * sig
        s_ref[...] = silu.astype(s_ref.dtype)
        ud_ref[...] = (u * (sig * (1 + g * (1 - sig)))).astype(ud_ref.dtype)
        a_ref[...] = (silu * u).astype(a_ref.dtype)

    wspec = pl.BlockSpec((None, k, D), lambda j, i: (j, 0, 0))
    ospec = pl.BlockSpec((None, tm, k), lambda j, i: (j, i, 0))
    return pl.pallas_call(
        body, name=name, grid=(J, T // tm),
        in_specs=[pl.BlockSpec((T, D), lambda j, i: (0, 0), pipeline_mode=RESIDENT), wspec, wspec] + dep_specs,
        out_specs=[ospec, ospec, ospec],
        out_shape=[jax.ShapeDtypeStruct((J, T, k), BF16)] * 3,
        compiler_params=_params())(hn, wg3, wu3, *dep_ins)


def _ffn_dact(name, dF, wd3, silu_g, u_dsilu):
    T, D = dF.shape
    J, k, _ = wd3.shape
    tm = _row_tile(T, 1088)

    def body(df_ref, wd_ref, s_ref, ud_ref, dg_ref, du_ref):
        rows = pl.ds(pl.multiple_of(pl.program_id(1) * tm, 16), tm)
        da = _nt(df_ref[rows, :], wd_ref[...])
        du_ref[...] = (da * s_ref[...].astype(F32)).astype(du_ref.dtype)
        dg_ref[...] = (da * ud_ref[...].astype(F32)).astype(dg_ref.dtype)

    aspec = pl.BlockSpec((None, tm, k), lambda j, i: (j, i, 0))
    return pl.pallas_call(
        body, name=name, grid=(J, T // tm),
        in_specs=[pl.BlockSpec((T, D), lambda j, i: (0, 0), pipeline_mode=RESIDENT), pl.BlockSpec((None, k, D), lambda j, i: (j, 0, 0)), aspec, aspec],
        out_specs=[aspec, aspec],
        out_shape=[jax.ShapeDtypeStruct((J, T, k), BF16), jax.ShapeDtypeStruct((J, T, k), BF16)],
        compiler_params=_params())(dF, wd3, silu_g, u_dsilu)


def _loss_head(name, h, g, target, cfg):
    T, D = h.shape

    def body(h_ref, g_ref, t_ref, dh_ref, loss_ref, dg_ref):
        i = pl.program_id(0)

        @pl.when(i == 0)
        def _():
            dh_ref[...] = jnp.zeros_like(dh_ref)
            loss_ref[...] = jnp.zeros_like(loss_ref)
            dg_ref[...] = jnp.zeros_like(dg_ref)

        @pl.when(i > 0)
        def _():
            x = h_ref[...]
            r = _rstd(x)
            xhat = x * r
            gg = g_ref[...]
            err = xhat * gg - t_ref[...]
            loss_ref[...] += 0.5 * jnp.sum(jnp.sum(err * err, axis=1, keepdims=True) / D, axis=0, keepdims=True)
            dy = err / D
            dg_ref[0:1, :] += jnp.sum(dy * xhat, axis=0, keepdims=True)
            dxh = dy * gg
            dh_ref[...] = r * (dxh - xhat * jnp.mean(dxh * xhat, axis=-1, keepdims=True))

    row = pl.BlockSpec((ROW_TILE, D), lambda i: (i, 0))
    return pl.pallas_call(
        body, name=name, grid=(T // ROW_TILE,),
        in_specs=[row, pl.BlockSpec((1, D), lambda i: (0, 0)), pl.BlockSpec((ROW_TILE, D), lambda i: (jnp.maximum(i - 1, 0), 0))],
        out_specs=[row, pl.BlockSpec((8, 128), lambda i: (0, 0)), pl.BlockSpec((8, D), lambda i: (0, 0))],
        out_shape=[jax.ShapeDtypeStruct((T, D), F32), jax.ShapeDtypeStruct((8, 128), F32), jax.ShapeDtypeStruct((8, D), F32)],
        compiler_params=_params())(h, g, target)


def _adamw(name, parts, w, m, v, dep=None):
    R, C = w.shape
    n_parts = parts.shape[0]
    tr = _row_tile(R, max(16, ADAM_BLOCK_ELEMS // C)) if R % 16 == 0 else R
    bc1 = 1.0 - ADAM_B1 ** ADAM_STEP
    bc2 = 1.0 - ADAM_B2 ** ADAM_STEP
    dep_ins, dep_specs = _dep(dep, 1)

    def body(p_ref, w_ref, m_ref, v_ref, *rest):
        g_ref, d_ref, nm_ref, nv_ref = rest[len(dep_ins):]
        g = p_ref[0].astype(F32)
        for k in range(1, n_parts):
            g = g + p_ref[k].astype(F32)
        nm = ADAM_B1 * m_ref[...] + (1.0 - ADAM_B1) * g
        nv = ADAM_B2 * v_ref[...] + (1.0 - ADAM_B2) * (g * g)
        g_ref[...] = g
        nm_ref[...] = nm
        nv_ref[...] = nv
        d_ref[...] = -ADAM_LR * ((nm / bc1) / (jnp.sqrt(nv / bc2) + ADAM_EPS) + ADAM_WD * w_ref[...])

    blk = pl.BlockSpec((tr, C), lambda i: (i, 0))
    return pl.pallas_call(
        body, name=name, grid=(R // tr,), in_specs=[pl.BlockSpec((n_parts, tr, C), lambda i: (0, i, 0)), blk, blk, blk] + dep_specs,
        out_specs=[blk] * 4, out_shape=[jax.ShapeDtypeStruct((R, C), F32)] * 4, compiler_params=_params())(parts, w, m, v, *dep_ins)


class _WeightStream:
    def __init__(self, cast):
        self.cast, self.handles = cast, {}

    @staticmethod
    def _tag(l, group):
        return ("ffn" if group is FFN_WEIGHTS else group[0]) + str(l)

    def start(self, l, group, after=None):
        self.handles[l, group], token = _gather_start(f"gather_{self._tag(l, group)}_start", [self.cast(l, n) for n in group], after)
        return token

    def relay(self, l, group, after):
        self.handles[l, group], token = _gather_relay(f"gather_{self._tag(l, group)}_relay", self.handles[l, group], after)
        return token

    def get(self, l, group, after):
        got = dict(zip(group, _gather_wait(f"gather_{self._tag(l, group)}_wait", self.handles[l, group], after)))
        if 'w_out' in got:
            rows, cols = got['w_out'].shape[1:]
            got['w_out'] = got['w_out'].reshape(N_DEV * rows, cols)
        return got


def _layer_fwd(l, h, p, stream, mod, rope, conv_w8, cfg):
    nm = f"l{l}_"
    mod = mod + stream.relay(l, IN_WEIGHT, h)[0, 0]
    xn = _norm_mod(nm + "norm1", h, p['norm1_g'], mod, 0)
    W = stream.get(l, IN_WEIGHT, xn)
    token = None
    if l == 0:
        token = stream.start(0, OUT_WEIGHT, after=W['w_in']) + stream.start(0, FFN_WEIGHTS, after=W['w_in'])
    P = _mm_cols(nm + "w_in", xn, W['w_in'], dep=token)
    qr, kr, vb = _qk_prep(nm + "qk_prep", P, p['q_norm_g'], p['k_norm_g'], rope[0], rope[1], cfg)
    if l == 0:
        o, lse = _attn_dense_fwd(nm + "attn", qr, kr, vb, cfg)
        aux = _conv_fwd(nm + "conv", P, conv_w8, cfg)
    else:
        o, lse = _attn_win_fwd(nm + "attn", qr, kr, vb, p['sink'], cfg)
        aux = _pool_fwd(nm + "pool", P, p['pool_w'], p['pool_scale'], cfg)
    mix = jnp.concatenate([o, aux], axis=1).astype(BF16)
    W.update(stream.get(l, OUT_WEIGHT, stream.relay(l, OUT_WEIGHT, mix)))
    y, h2 = _mm_plain(nm + "w_out", mix, W['w_out'], False, dep=stream.relay(l, FFN_WEIGHTS, o), res=(h, mod, 2, cfg['Lc']))
    hn = _norm_mod(nm + "norm2", h2, p['norm2_g'], mod, 1)
    W.update(stream.get(l, FFN_WEIGHTS, hn))
    token = stream.start(1, IN_WEIGHT, after=W['w_down']) if l == 0 else None
    silu_g, u_dsilu, A = _ffn_up(nm + "ffn_up", hn, W['w_gate'], W['w_up'], dep=token)
    if l == 0:
        token = stream.start(1, OUT_WEIGHT, after=A) + stream.start(1, FFN_WEIGHTS, after=A)
    F, h3 = _mm_shards_nn(nm + "w_down", A, W['w_down'], dep=token, res=(h2, mod, 5, cfg['Lc']))
    saved = dict(h=h, xn=xn, P=P, qr=qr, kr=kr, vb=vb, o=o, lse=lse, mix=mix, y=y, h2=h2, hn=hn, silu_g=silu_g, u_dsilu=u_dsilu, A=A, F=F)
    return h3, saved, W


def _layer_bwd(l, dh3, s, p, W, mod, rope, conv_w8, cfg, res2_bwd=None, below=None):
    nm = f"l{l}_bwd_"
    J = N_DEV
    dF, dmod = _gate_bwd(nm + "res2", dh3, s['F'], mod, 5) if res2_bwd is None else res2_bwd
    dG, dU = _ffn_dact(nm + "ffn_act", dF, W['w_down'], s['silu_g'], s['u_dsilu'])
    big = {'w_down': _wgrad_down(nm + "dw_down", s['A'], dF),
           'w_gate': _wgrad_down(nm + "dw_gate", dG, s['hn']),
           'w_up': _wgrad_down(nm + "dw_up", dU, s['hn'])}
    handles = {}
    handles['ffn'], token = _exchange_start(f"scatter_ffn{l}_start", [big[n] for n in FFN_WEIGHTS], True)
    mod = mod + token[0, 0]
    dhn = _mm_shards_nn2(nm + "dhn", dG, W['w_gate'], dU, W['w_up'])
    dh2, dm, dg2, dY, dm_gate = _norm_mod_bwd(nm + "norm2", dhn, s['h2'], p['norm2_g'], mod, 1, dh3, gate=(s['y'], mod, 2))
    dmod += dm + dm_gate
    dwo = _wgrad_rows(nm + "dw_out", s['mix'], dY)
    handles['w_out'], token = _exchange_start(f"scatter_w_out{l}_start", [dwo.reshape((J, dwo.shape[0] // J, dwo.shape[1]))], True)
    dmix = _mm_plain(nm + "dmix", dY, W['w_out'], True, dep=token)
    small = {'norm2_g': dg2[0]}
    if l == 0:
        dqr, dkr, dv = _attn_dense_bwd(nm + "attn", s['qr'], s['kr'], s['vb'], s['o'], s['lse'], dmix, cfg)
        *daux, dcw = _conv_bwd(nm + "conv", s['P'], conv_w8, dmix, cfg)
        small['conv_w'] = dcw[0:3]
    else:
        dqr, dkr, dv, dsk = _attn_win_bwd(nm + "attn", s['qr'], s['kr'], s['vb'], s['o'], s['lse'], dmix, p['sink'], cfg)
        du, dpw, dps = _pool_bwd(nm + "pool", s['P'], p['pool_w'], p['pool_scale'], dmix, cfg)
        daux = [du]
        small.update(sink=dsk[:, 0, ::HEAD_DIM].reshape(-1), pool_w=dpw, pool_scale=dps[0])
    dq, dk, dqg, dkg = _qk_prep_bwd(nm + "qk_prep", dqr, dkr, s['P'], p['q_norm_g'], p['k_norm_g'], rope[0], rope[1], cfg)
    small.update(q_norm_g=dqg[0], k_norm_g=dkg[0])
    dP = jnp.concatenate([dq, dk, dv.astype(BF16), *daux], axis=1)
    handles['w_in'], token = _exchange_start(f"scatter_w_in{l}_start", [_wgrad_cols(nm + "dw_in", s['xn'], dP, J)], True)
    mod = mod + token[0, 0]
    dxn = _mm_cols_nt(nm + "dxn", dP, W['w_in'])
    gate = None if below is None else (below[0], below[1], 5)
    dh, dm, dg1, *res2_below = _norm_mod_bwd(nm + "norm1", dxn, s['h'], p['norm1_g'], mod, 0, dh2, latent_only=(l == 0), gate=gate)
    dmod += dm
    small['norm1_g'] = dg1[0]
    return dh, dmod, small, handles, token, (tuple(res2_below) or None)


def _rope_tables(S, Lc):
    half = HEAD_DIM // 4
    pos = np.arange(S)
    inv = ROPE_THETA ** (-np.arange(0, 2 * half, 2, dtype=np.float32) / (2 * half))
    inv = jnp.asarray(inv, F32)
    ang_r = jnp.asarray(pos // GRID_W, F32)[:, None] * inv
    ang_c = jnp.asarray(pos % GRID_W, F32)[:, None] * inv
    cos = jnp.concatenate([jnp.cos(ang_r)] * 2 + [jnp.cos(ang_c)] * 2, axis=1)
    sin = jnp.concatenate([-jnp.sin(ang_r), jnp.sin(ang_r), -jnp.sin(ang_c), jnp.sin(ang_c)], axis=1)
    return (jnp.concatenate([jnp.ones((Lc, HEAD_DIM), F32), cos], axis=0),
            jnp.concatenate([jnp.zeros((Lc, HEAD_DIM), F32), sin], axis=0))


def _pad_rows(a, rows):
    return jnp.concatenate([a, jnp.zeros((rows - a.shape[0],) + a.shape[1:], a.dtype)], axis=0)


def _flat128(a, nlead):
    lead = a.shape[:nlead]
    f = a.reshape(lead + (-1,))
    pad = (-f.shape[-1]) % 128
    if pad:
        f = jnp.concatenate([f, jnp.zeros(lead + (pad,), f.dtype)], axis=-1)
    return f.reshape(lead + (-1, 128))


def _pack(named, nlead=0):
    rows, layout, at = [], {}, 0
    for name, a in named:
        f = _flat128(a, nlead)
        n = f.shape[-2]
        pad = (-n) % 8
        if pad:
            f = jnp.concatenate([f, jnp.zeros(f.shape[:-2] + (pad, 128), f.dtype)], axis=-2)
        layout[name] = (at, n, a.shape[nlead:])
        rows.append(f)
        at += n + pad
    return jnp.concatenate(rows, axis=-2), layout


def _unpack(arr, layout, name):
    at, n, shape = layout[name]
    return arr[..., at:at + n, :].reshape(arr.shape[:-2] + (-1,))[..., :math.prod(shape)].reshape(arr.shape[:-2] + tuple(shape))


def kernel(*args):
    A = dict(zip(INPUT_NAMES, args, strict=True))
    x, ctx = A['x'][0], A['ctx'][0]
    S, D = x.shape
    Lc = ctx.shape[0]
    T = Lc + S
    ATT = D - AUX_WIDTH
    KVW = (A['l1_w_in'].shape[1] * N_DEV - ATT - AUX_WIDTH) // 2
    cfg = dict(ATT=ATT, KVW=KVW, NKV=KVW // HEAD_DIM, G=ATT // KVW, Lc=Lc)
    assert Lc == ROW_TILE and S % ROW_TILE == 0 and T >= BAND and S % GRID_W == 0
    cw = A['l0_conv_w'].shape[1]
    me = 4 * lax.axis_index("x") + 2 * lax.axis_index("y") + lax.axis_index("c")

    def layer_params(l):
        pre = f"l{l}_"
        return {k[len(pre):]: (v.reshape(1, -1) if v.ndim == 1 and k != 'l1_sink' else v) for k, v in A.items() if k.startswith(pre)}

    params = [layer_params(0), layer_params(1)]

    def cast(l, n):
        w = A[f'l{l}_{n}']
        return (w.T if n in TRANSPOSED else w).astype(BF16)

    stream = _WeightStream(cast)
    token = stream.start(0, IN_WEIGHT)

    big_names = [n for n in WEIGHT_NAMES if n[3:] in BIG_WEIGHTS + ('w_mod',)]
    rest = [n for n in WEIGHT_NAMES if n not in big_names]
    early = ['x', 'ctx'] + rest + ['m_' + n for n in rest] + ['v_' + n for n in rest]
    token, held = lax.optimization_barrier((token, [A[n] for n in early]))
    A.update(zip(early, held))
    x, ctx = A['x'][0], A['ctx'][0]
    wp, layw = _pack([(n, A[n]) for n in rest])
    mp, _ = _pack([(n, A['m_' + n]) for n in rest])
    vp, _ = _pack([(n, A['v_' + n]) for n in rest])
    rope = _rope_tables(S, Lc)
    h = jnp.concatenate([ctx, x], axis=0)

    sc_own = jax.nn.silu(A['c']) + token[0, 0]
    first, lay0 = _pack([('sc', sc_own), ('conv_w', A['l0_conv_w'])])
    first, h, wp, mp, vp = lax.optimization_barrier((first, h, wp, mp, vp))
    first_all = _exchange("gather_cond", [first], False)[0]
    sc_all = _unpack(first_all, lay0, 'sc')[:, 0]
    conv_w = _unpack(first_all, lay0, 'conv_w').transpose(1, 0, 2).reshape(3, N_DEV * cw)
    conv_w8 = _pad_rows(conv_w, 8)
    sc_ctx = jax.nn.silu(A['c_ctx'])
    s16 = _pad_rows(jnp.concatenate([sc_all, sc_ctx[None]], axis=0), 16)

    nmod = A['l0_w_mod'].shape[1]
    modp = jnp.concatenate([_mm_plain(f"l{l}_mod", s16, A[f'l{l}_w_mod'], False) for l in range(2)], axis=1)
    modp_all = _exchange("gather_mod", [modp], False)[0]
    mods = []
    for l in range(2):
        full = modp_all[:, :, l * nmod:(l + 1) * nmod].transpose(1, 0, 2).reshape(16, N_MOD * D) + A[f'l{l}_b_mod'][None]
        both = jnp.stack([full[8], lax.dynamic_index_in_dim(full, me, 0, keepdims=False)]).reshape(2, N_MOD, D)
        mods.append(jnp.concatenate([both, jnp.zeros((2, 8 - N_MOD, D), F32)], axis=1))

    saved, W = [], []
    for l in range(2):
        h, s, Wl = _layer_fwd(l, h, params[l], stream, mods[l], rope, conv_w8, cfg)
        saved.append(s)
        W.append(Wl)

    dh, loss_blk, dgf = _loss_head("loss_head", h, A['final_norm_g'].reshape(1, -1), A['loss_target'][0], cfg)
    loss = lax.psum(loss_blk[0, 0], ("x", "y", "c"))

    grads, small, dmods, scatters = {}, {'final_norm_g': dgf[0]}, [None, None], [None, None]
    token, res2_bwd = jnp.zeros((8, 128), F32), None
    for l in (1, 0):
        below = (saved[0]['F'], mods[0]) if l == 1 else None
        dh, dmods[l], sm, scatters[l], token, res2_bwd = _layer_bwd(
            l, dh, saved[l], params[l], W[l], mods[l] + token[0, 0], rope, conv_w8, cfg, res2_bwd, below)
        small.update({f'l{l}_{k}': v for k, v in sm.items()})
    grad_x = dh[None]

    def landed(l, key, after):
        group = FFN_WEIGHTS if key == 'ffn' else (key,)
        for n, parts in zip(group, _exchange_wait(f"scatter_{key}{l}_wait", scatters[l][key], after)):
            shape = A[f'l{l}_{n}'].shape
            grads[f'l{l}_{n}'] = (parts.reshape((N_DEV,) + (shape[::-1] if n in TRANSPOSED else shape)), None)

    out = {}

    def adam(n, dep=None):
        w, m, v = A[n], A['m_' + n], A['v_' + n]
        if n[3:] in TRANSPOSED:
            res = _adamw("adamw_" + n, grads[n][0], w.T, m.T, v.T, dep)
            out[n] = tuple(r.T for r in res)
        else:
            res = out[n] = _adamw("adamw_" + n, grads[n][0], w, m, v, dep)
        return res[1]

    small_names = [n for n in WEIGHT_NAMES if n in small]
    pieces = [(n, small[n]) for n in small_names]
    for l in range(2):
        pieces += [(f'dmod{l}', dmods[l][1, :N_MOD]), (f'dcmod{l}', dmods[l][0, :N_MOD])]
    second, lay1 = _pack(pieces)
    small_handle, small_token = _exchange_start("gather_small_start", [second], False, after=token)

    last, chain = dh, small_token
    for l in (1, 0):
        for key in ('ffn', 'w_out') + (('w_in',) if l == 1 else ()):
            landed(l, key, last)
            for n in (FFN_WEIGHTS if key == 'ffn' else (key,)):
                last = adam(f'l{l}_{n}', chain)
                chain = last[:8, :128]
    second_all = _exchange_wait("gather_small_wait", small_handle, last)[0]

    dsc_part = jnp.zeros((16, D), F32)
    for l in range(2):
        dm16 = _pad_rows(jnp.concatenate([_unpack(second_all, lay1, f'dmod{l}').reshape(N_DEV, N_MOD * D),
                                          jnp.sum(_unpack(second_all, lay1, f'dcmod{l}'), axis=0).reshape(1, N_MOD * D)], axis=0), 16)
        mine = lax.dynamic_slice_in_dim(dm16, me * nmod, nmod, axis=1)
        tk = _col_tile(D, 512)
        gw = _mm_tn(f"l{l}_dw_mod", s16, mine, pl.BlockSpec((16, tk), lambda kb, t: (0, kb)), pl.BlockSpec((16, nmod), lambda kb, t: (0, 0)),
                    jax.ShapeDtypeStruct((D, nmod), F32), pl.BlockSpec((tk, nmod), lambda kb, t: (kb, 0)), (tk, nmod), (D // tk, 1))
        grads[f'l{l}_w_mod'] = (gw[None], None)
        dsc_part += _mm_plain(f"l{l}_dsc", mine, A[f'l{l}_w_mod'], True)
        dmod_dev = _unpack(second_all, lay1, f'dmod{l}') + _unpack(second_all, lay1, f'dcmod{l}')
        grads[f'l{l}_b_mod'] = (dmod_dev.reshape(N_DEV, N_MOD * D), None)
    dsig = jax.nn.sigmoid(A['c_ctx'])
    dsilu = dsig * (1 + A['c_ctx'] * (1 - dsig))
    third_all = _exchange("gather_dsc", [dsc_part[8:9]], False)[0]
    grads['c_ctx'] = (third_all[:, 0] * dsilu[None], None)
    for n in small_names:
        g8 = _unpack(second_all, lay1, n)
        if n == 'l0_conv_w':
            g8 = lax.dynamic_slice_in_dim(g8, me * cw, cw, axis=2)
        grads[n] = (g8, None)

    adam('l0_w_mod')
    last = adam('l1_w_mod')
    gp, _ = _pack([(n, grads[n][0]) for n in rest], nlead=1)
    res = _adamw("adamw_small", gp, wp, mp, vp)
    for n in rest:
        out[n] = tuple(_unpack(r, layw, n) for r in res)
    landed(0, 'w_in', last)
    adam('l0_w_in')

    outs = [loss, grad_x]
    for k in range(4):
        outs += [out[n][k] for n in WEIGHT_NAMES]
    return tuple(outs)
```

```python
import math

import numpy as np
import jax
import jax.numpy as jnp
from jax import lax
from jax.experimental import pallas as pl
from jax.experimental.pallas import tpu as pltpu

F32 = jnp.float32
BF16 = jnp.bfloat16
HEAD_DIM = 128
AUX_WIDTH = 512
AUX_GROUPS = 4
POOL_HALF = (1, 2, 4, 8)
WINDOW = 128
GRID_W = 64
ROPE_THETA = 10000.0
EPS = 1e-6
NEG_INF = -1e30
ATT_SCALE = HEAD_DIM ** -0.5
LOG2_E = math.log2(math.e)
Q_SCALE = ATT_SCALE * LOG2_E
N_MOD = 6
N_DEV = 8
ROW_TILE = 256
BAND = ROW_TILE + 2 * WINDOW
ADAM_LR, ADAM_B1, ADAM_B2, ADAM_EPS, ADAM_WD, ADAM_STEP = 0.001, 0.9, 0.999, 1e-08, 0.01, 10
VMEM_LIMIT_MB = 56
ADAM_BLOCK_ELEMS = 3 << 17
MESH = pl.DeviceIdType.MESH
RESIDENT = pl.Buffered(buffer_count=1)

WEIGHT_NAMES = ['c_ctx', 'l0_norm1_g', 'l0_w_mod', 'l0_b_mod', 'l0_w_in', 'l0_q_norm_g', 'l0_k_norm_g', 'l0_conv_w', 'l0_w_out', 'l0_norm2_g', 'l0_w_gate', 'l0_w_up', 'l0_w_down', 'l1_norm1_g', 'l1_w_mod', 'l1_b_mod', 'l1_w_in', 'l1_q_norm_g', 'l1_k_norm_g', 'l1_sink', 'l1_pool_w', 'l1_pool_scale', 'l1_w_out', 'l1_norm2_g', 'l1_w_gate', 'l1_w_up', 'l1_w_down', 'final_norm_g']
INPUT_NAMES = (['x', 'c', 'ctx'] + WEIGHT_NAMES + ['loss_target'] + ['m_' + n for n in WEIGHT_NAMES]
               + ['v_' + n for n in WEIGHT_NAMES])
IN_WEIGHT = ('w_in',)
OUT_WEIGHT = ('w_out',)
MIXER_WEIGHTS = OUT_WEIGHT + IN_WEIGHT
FFN_WEIGHTS = ('w_down', 'w_gate', 'w_up')
TRANSPOSED = ('w_gate', 'w_up')
BIG_WEIGHTS = MIXER_WEIGHTS + FFN_WEIGHTS


def _params(vmem_mb=VMEM_LIMIT_MB):
    return pltpu.CompilerParams(vmem_limit_bytes=vmem_mb << 20)


def _row_tile(n, cap):
    best = None
    for t in range(16, min(n, cap) + 1, 16):
        if n % t == 0:
            best = t
    assert best is not None, (n, cap)
    return best


def _col_tile(n, cap):
    best = n
    for t in range(128, min(n, cap) + 1, 128):
        if n % t == 0:
            best = t
    return best


def _dot(a, b, ca, cb):
    return lax.dot_general(a, b, (((ca,), (cb,)), ((), ())), preferred_element_type=F32)


def _nn(a, b):
    return _dot(a, b, 1, 0)


def _nt(a, b):
    return _dot(a, b, 1, 1)


def _tn(a, b):
    return _dot(a, b, 0, 0)


def _bf(x):
    return x.astype(BF16)


def _exchange(name, arrs, scatter, after=None):
    n = len(arrs)
    extra = [] if after is None else [after]
    if scatter:
        out_shape = [jax.ShapeDtypeStruct(a.shape, a.dtype) for a in arrs]
    else:
        out_shape = [jax.ShapeDtypeStruct((N_DEV,) + a.shape, a.dtype) for a in arrs]

    def body(*refs):
        ins, outs = refs[:n], refs[n + len(extra):2 * n + len(extra)]
        send_sems, recv_sems, local_sems = refs[2 * n + len(extra):]
        x, y, c = lax.axis_index("x"), lax.axis_index("y"), lax.axis_index("c")
        me = 4 * x + 2 * y + c
        local, remote = [], []
        for a in range(n):
            own = ins[a].at[me] if scatter else ins[a]
            cp = pltpu.make_async_copy(own, outs[a].at[me], local_sems.at[a])
            cp.start()
            local.append(cp)
            for r in range(1, N_DEV):
                px = 1 - x if r & 4 else x
                py = 1 - y if r & 2 else y
                pc = 1 - c if r & 1 else c
                src = ins[a].at[4 * px + 2 * py + pc] if scatter else ins[a]
                cp = pltpu.make_async_remote_copy(
                    src_ref=src, dst_ref=outs[a].at[me], send_sem=send_sems.at[a, r - 1],
                    recv_sem=recv_sems.at[a, r - 1], device_id=(px, py, pc), device_id_type=MESH)
                cp.start()
                remote.append(cp)
        for cp in remote:
            cp.wait()
        for cp in local:
            cp.wait()

    any_spec = pl.BlockSpec(memory_space=pl.ANY)
    return pl.pallas_call(
        body, name=name, out_shape=out_shape,
        in_specs=[any_spec] * (n + len(extra)), out_specs=[any_spec] * n,
        scratch_shapes=[pltpu.SemaphoreType.DMA((n, N_DEV - 1)), pltpu.SemaphoreType.DMA((n, N_DEV - 1)),
                        pltpu.SemaphoreType.DMA((n,))],
    )(*arrs, *extra)


HBM_SPEC = pl.BlockSpec(memory_space=pltpu.HBM)
SEM_SPEC = pl.BlockSpec(memory_space=pltpu.SEMAPHORE)
EFFECT = pltpu.SideEffectType.DATAFLOW_SIDE_EFFECTING


def _split_copies(srcs, lands, send_sems, recv_sems, local_sems, scatter):
    x, y, c = lax.axis_index("x"), lax.axis_index("y"), lax.axis_index("c")
    me = 4 * x + 2 * y + c
    local, remote = [], []
    for a in range(len(srcs)):
        own = srcs[a].at[me] if scatter else srcs[a]
        local.append(pltpu.make_async_copy(own, lands[a].at[me], local_sems.at[a]))
        for r in range(1, N_DEV):
            px = 1 - x if r & 4 else x
            py = 1 - y if r & 2 else y
            pc = 1 - c if r & 1 else c
            src = srcs[a].at[4 * px + 2 * py + pc] if scatter else srcs[a]
            remote.append(pltpu.make_async_remote_copy(
                src_ref=src, dst_ref=lands[a].at[me], send_sem=send_sems.at[a * (N_DEV - 1) + r - 1],
                recv_sem=recv_sems.at[a * (N_DEV - 1) + r - 1], device_id=(px, py, pc), device_id_type=MESH))
    return local, remote


def _exchange_start(name, arrs, scatter, after=None):
    n = len(arrs)
    extra = [] if after is None else [after]
    shapes = [a.shape if scatter else (N_DEV,) + a.shape for a in arrs]
    lands = [pltpu.with_memory_space_constraint(lax.empty(s, a.dtype), pltpu.HBM) for s, a in zip(shapes, arrs)]
    srcs = [pltpu.with_memory_space_constraint(a, pltpu.HBM) for a in arrs]

    def body(*refs):
        src_refs, land_refs = refs[:n], refs[n:2 * n]
        send_sems, recv_sems, local_sems = refs[2 * n + len(extra):2 * n + len(extra) + 3]
        token = refs[-1]
        local, remote = _split_copies(src_refs, land_refs, send_sems, recv_sems, local_sems, scatter)
        for cp in local + remote:
            cp.start()
        token[...] = jnp.zeros_like(token)

    res = pl.pallas_call(
        body, name=name,
        out_shape=[pltpu.SemaphoreType.DMA((n * (N_DEV - 1),)), pltpu.SemaphoreType.DMA((n * (N_DEV - 1),)), pltpu.SemaphoreType.DMA((n,))]
        + [pltpu.HBM(a.shape, a.dtype) for a in arrs] + [pltpu.HBM(s, a.dtype) for s, a in zip(shapes, arrs)]
        + [jax.ShapeDtypeStruct((8, 128), F32)],
        in_specs=[HBM_SPEC] * (2 * n) + [pl.BlockSpec(memory_space=pl.ANY)] * len(extra),
        out_specs=[SEM_SPEC] * 3 + [HBM_SPEC] * (2 * n) + [pl.BlockSpec(memory_space=pltpu.VMEM)],
        input_output_aliases={i: 3 + i for i in range(2 * n)},
        compiler_params=pltpu.CompilerParams(has_side_effects=EFFECT),
    )(*srcs, *lands, *extra)
    return (scatter, res[:3], res[3:3 + n], res[3 + n:3 + 2 * n]), res[-1]


def _exchange_wait(name, handle, after):
    scatter, sems, srcs, lands = handle
    n = len(srcs)

    def body(*refs):
        src_refs, land_refs = refs[:n], refs[n:2 * n]
        send_sems, recv_sems, local_sems = refs[2 * n:2 * n + 3]
        local, remote = _split_copies(src_refs, land_refs, send_sems, recv_sems, local_sems, scatter)
        for cp in remote:
            cp.wait_send()
            cp.wait_recv()
        for cp in local:
            cp.wait()

    res = pl.pallas_call(
        body, name=name,
        out_shape=[pltpu.HBM(a.shape, a.dtype) for a in srcs] + [pltpu.HBM(a.shape, a.dtype) for a in lands],
        in_specs=[HBM_SPEC] * (2 * n) + [SEM_SPEC] * 3 + [pl.BlockSpec(memory_space=pl.ANY)], out_specs=[HBM_SPEC] * (2 * n),
        input_output_aliases={i: i for i in range(2 * n)},
        compiler_params=pltpu.CompilerParams(has_side_effects=EFFECT),
    )(*srcs, *lands, *sems, after)
    return list(res[n:])


FIRST_COPIES = 4
RELAY_COPIES = 3


def _gather_copies(srcs, lands, sems):
    send_sems, recv_sems, local_sems = sems[:3]
    x, y, c = lax.axis_index("x"), lax.axis_index("y"), lax.axis_index("c")
    me = 4 * x + 2 * y + c
    chips = [(1 - x, y), (x, 1 - y), (1 - x, 1 - y)]
    local, first, relay = [], [], []
    for a in range(len(srcs)):
        local.append(pltpu.make_async_copy(srcs[a], lands[a].at[me], local_sems.at[a]))
        targets = [(x, y, 1 - c)] + [(px, py, c) for px, py in chips]
        first.append([pltpu.make_async_remote_copy(
            src_ref=srcs[a], dst_ref=lands[a].at[me], send_sem=send_sems.at[FIRST_COPIES * a + k],
            recv_sem=recv_sems.at[FIRST_COPIES * a + k], device_id=t, device_id_type=MESH) for k, t in enumerate(targets)])
        if len(sems) > 3:
            rsend, rrecv = sems[3:]
            slots = [lands[a].at[4 * px + 2 * py + c] for px, py in chips]
            relay.append([pltpu.make_async_remote_copy(
                src_ref=slot, dst_ref=slot, send_sem=rsend.at[RELAY_COPIES * a + j], recv_sem=rrecv.at[RELAY_COPIES * a + j],
                device_id=(x, y, 1 - c), device_id_type=MESH) for j, slot in enumerate(slots)])
    return local, first, relay


def _gather_start(name, arrs, after=None):
    n = len(arrs)
    extra = [] if after is None else [after]
    lands = [pltpu.with_memory_space_constraint(lax.empty((N_DEV,) + a.shape, a.dtype), pltpu.HBM) for a in arrs]
    srcs = [pltpu.with_memory_space_constraint(a, pltpu.HBM) for a in arrs]

    def body(*refs):
        at = 2 * n + len(extra)
        local, first, _ = _gather_copies(refs[:n], refs[n:2 * n], refs[at:at + 3])
        for cp in local + [cp for cps in first for cp in cps]:
            cp.start()
        refs[-1][...] = jnp.zeros_like(refs[-1])

    res = pl.pallas_call(
        body, name=name,
        out_shape=[pltpu.SemaphoreType.DMA((FIRST_COPIES * n,)), pltpu.SemaphoreType.DMA((FIRST_COPIES * n,)), pltpu.SemaphoreType.DMA((n,))]
        + [pltpu.HBM(a.shape, a.dtype) for a in arrs] + [pltpu.HBM((N_DEV,) + a.shape, a.dtype) for a in arrs]
        + [jax.ShapeDtypeStruct((8, 128), F32)],
        in_specs=[HBM_SPEC] * (2 * n) + [pl.BlockSpec(memory_space=pl.ANY)] * len(extra),
        out_specs=[SEM_SPEC] * 3 + [HBM_SPEC] * (2 * n) + [pl.BlockSpec(memory_space=pltpu.VMEM)],
        input_output_aliases={i: 3 + i for i in range(2 * n)},
        compiler_params=pltpu.CompilerParams(has_side_effects=EFFECT),
    )(*srcs, *lands, *extra)
    return (list(res[:3]), list(res[3:3 + n]), list(res[3 + n:3 + 2 * n])), res[-1]


def _gather_relay(name, handle, after):
    sems, srcs, lands = handle
    n = len(srcs)

    def body(*refs):
        in_sems = refs[2 * n:2 * n + 3]
        out_sems = refs[2 * n + 4 + 2 * n:2 * n + 4 + 2 * n + 2]
        _, first, relay = _gather_copies(refs[:n], refs[n:2 * n], list(in_sems) + list(out_sems))
        for a in range(n):
            for j in range(RELAY_COPIES):
                first[a][1 + j].wait_recv()
                relay[a][j].start()
        refs[-1][...] = jnp.zeros_like(refs[-1])

    res = pl.pallas_call(
        body, name=name,
        out_shape=[pltpu.HBM(a.shape, a.dtype) for a in srcs] + [pltpu.HBM(a.shape, a.dtype) for a in lands]
        + [pltpu.SemaphoreType.DMA((RELAY_COPIES * n,)), pltpu.SemaphoreType.DMA((RELAY_COPIES * n,)), jax.ShapeDtypeStruct((8, 128), F32)],
        in_specs=[HBM_SPEC] * (2 * n) + [SEM_SPEC] * 3 + [pl.BlockSpec(memory_space=pl.ANY)],
        out_specs=[HBM_SPEC] * (2 * n) + [SEM_SPEC] * 2 + [pl.BlockSpec(memory_space=pltpu.VMEM)],
        input_output_aliases={i: i for i in range(2 * n)},
        compiler_params=pltpu.CompilerParams(has_side_effects=EFFECT),
    )(*srcs, *lands, *sems, after)
    return (sems + list(res[2 * n:2 * n + 2]), list(res[:n]), list(res[n:2 * n])), res[-1]


def _gather_wait(name, handle, after):
    sems, srcs, lands = handle
    n = len(srcs)

    def body(*refs):
        local, first, relay = _gather_copies(refs[:n], refs[n:2 * n], refs[2 * n:2 * n + 5])
        for a in range(n):
            for cp in first[a]:
                cp.wait_send()
            first[a][0].wait_recv()
            for cp in relay[a]:
                cp.wait_send()
                cp.wait_recv()
            local[a].wait()

    res = pl.pallas_call(
        body, name=name,
        out_shape=[pltpu.HBM(a.shape, a.dtype) for a in srcs] + [pltpu.HBM(a.shape, a.dtype) for a in lands],
        in_specs=[HBM_SPEC] * (2 * n) + [SEM_SPEC] * 5 + [pl.BlockSpec(memory_space=pl.ANY)], out_specs=[HBM_SPEC] * (2 * n),
        input_output_aliases={i: i for i in range(2 * n)},
        compiler_params=pltpu.CompilerParams(has_side_effects=EFFECT),
    )(*srcs, *lands, *sems, after)
    return list(res[n:])


def _dep(dep, grid_rank):
    if dep is None:
        return [], []
    return [dep], [pl.BlockSpec((8, 128), (lambda i, j: (0, 0)) if grid_rank == 2 else (lambda i: (0, 0)))]


def _mm_step(name, fn, ins, in_specs, out_shape, out_spec, grid, dep=None, res=None):
    n = len(ins)
    dep_ins, dep_specs = _dep(dep, len(grid))
    res_ins, res_specs, out_shapes, out_specs = [], [], out_shape, out_spec
    if res is not None:
        h, mod, row_idx, lc = res
        tm, tn = out_spec.block_shape
        res_ins = [h, mod]
        res_specs = [pl.BlockSpec((tm, tn), lambda j, i: (i, j)), pl.BlockSpec((2, 8, tn), lambda j, i: (0, 0, j))]
        out_shapes, out_specs = [out_shape, jax.ShapeDtypeStruct(h.shape, h.dtype)], [out_spec, res_specs[0]]

    def body(*refs):
        outs = refs[n + len(res_ins) + len(dep_ins):]
        acc = fn(*refs[:n])
        outs[0][...] = acc.astype(outs[0].dtype)
        if res is not None:
            h_ref, mod_ref = refs[n:n + 2]
            row = pl.program_id(1) * tm + lax.broadcasted_iota(jnp.int32, (tm, 1), 0)
            gate = jnp.where(row < lc, mod_ref[0, row_idx:row_idx + 1, :], mod_ref[1, row_idx:row_idx + 1, :])
            outs[1][...] = h_ref[...] + gate * acc

    return pl.pallas_call(body, name=name, grid=grid, in_specs=list(in_specs) + res_specs + dep_specs, out_specs=out_specs,
                          out_shape=out_shapes, compiler_params=_params())(*ins, *res_ins, *dep_ins)


def _mm_tn(name, a, b, a_spec, b_spec, out_shape, out_spec, acc_shape, grid):
    nk = grid[-1]
    kax = len(grid) - 1
    if nk == 1:
        def whole(a_ref, b_ref, o_ref):
            o_ref[...] = _tn(_bf(a_ref[...]), _bf(b_ref[...])).astype(o_ref.dtype)

        return pl.pallas_call(whole, name=name, grid=grid, in_specs=[a_spec, b_spec], out_specs=out_spec,
                              out_shape=out_shape, compiler_params=_params())(a, b)

    def body(a_ref, b_ref, o_ref, acc_ref):
        k = pl.program_id(kax)

        @pl.when(k == 0)
        def _():
            acc_ref[...] = jnp.zeros_like(acc_ref)

        acc_ref[...] += _tn(_bf(a_ref[...]), _bf(b_ref[...]))

        @pl.when(k == nk - 1)
        def _():
            o_ref[...] = acc_ref[...].astype(o_ref.dtype)

    return pl.pallas_call(body, name=name, grid=grid, in_specs=[a_spec, b_spec], out_specs=out_spec,
                          out_shape=out_shape, scratch_shapes=[pltpu.VMEM(acc_shape, F32)],
                          compiler_params=_params())(a, b)


def _shards_per_dot(J, n):
    return 2 if n % 256 and J % 2 == 0 else 1


def _mm_cols(name, a, w3, out_dtype=F32, dep=None):
    M, K = a.shape
    J, _, n = w3.shape
    g = _shards_per_dot(J, n)
    tm = M if g == 1 else _row_tile(M, M // g)

    def fn(a_ref, w_ref):
        w = w_ref[0] if g == 1 else jnp.concatenate([w_ref[s] for s in range(g)], axis=1)
        x = a_ref[...] if tm == M else a_ref[pl.ds(pl.multiple_of(pl.program_id(1) * tm, 16), tm), :]
        return _nn(_bf(x), w)

    return _mm_step(
        name, fn, [a, w3],
        [pl.BlockSpec((M, K), lambda j, i: (0, 0), pipeline_mode=RESIDENT), pl.BlockSpec((g, K, n), lambda j, i: (j, 0, 0))],
        jax.ShapeDtypeStruct((M, J * n), out_dtype), pl.BlockSpec((tm, g * n), lambda j, i: (i, j)), (J // g, M // tm), dep)


def _mm_plain(name, a, b, transpose_b, out_dtype=F32, tn=512, dep=None, res=None):
    M, K = a.shape
    N = b.shape[0] if transpose_b else b.shape[1]
    tn = _col_tile(N, tn)
    tm = M if res is None else _row_tile(M, 1088)
    a_spec = pl.BlockSpec((M, K), lambda j, i: (0, 0), pipeline_mode=RESIDENT)

    def rows(a_ref):
        return a_ref[...] if tm == M else a_ref[pl.ds(pl.multiple_of(pl.program_id(1) * tm, 16), tm), :]

    if transpose_b:
        b_spec = pl.BlockSpec((tn, K), lambda j, i: (j, 0))
        fn = lambda a_ref, b_ref: _nt(_bf(rows(a_ref)), _bf(b_ref[...]))
    else:
        b_spec = pl.BlockSpec((K, tn), lambda j, i: (0, j))
        fn = lambda a_ref, b_ref: _nn(_bf(rows(a_ref)), _bf(b_ref[...]))
    return _mm_step(name, fn, [a, b], [a_spec, b_spec],
                    jax.ShapeDtypeStruct((M, N), out_dtype), pl.BlockSpec((tm, tn), lambda j, i: (i, j)),
                    (N // tn, M // tm), dep, res)


def _mm_shards_nn(name, a3, w3, tn=512, dep=None, res=None):
    J, M, k = a3.shape
    N = w3.shape[2]
    tm = _row_tile(M, 544)
    tn = _col_tile(N, tn)

    def fn(a_ref, w_ref):
        acc = _nn(a_ref[0], w_ref[0])
        for j in range(1, J):
            acc += _nn(a_ref[j], w_ref[j])
        return acc

    return _mm_step(name, fn, [a3, w3],
                    [pl.BlockSpec((J, tm, k), lambda jn, i: (0, i, 0)), pl.BlockSpec((J, k, tn), lambda jn, i: (0, 0, jn))],
                    jax.ShapeDtypeStruct((M, N), F32), pl.BlockSpec((tm, tn), lambda jn, i: (i, jn)), (N // tn, M // tm), dep, res)


def _mm_shards_nn2(name, a3, w3a, b3, w3b, tn=512):
    J, M, k = a3.shape
    N = w3a.shape[2]
    tm = _row_tile(M, 544)
    tn = _col_tile(N, tn)

    def fn(a_ref, wa_ref, b_ref, wb_ref):
        acc = _nn(a_ref[0], wa_ref[0]) + _nn(b_ref[0], wb_ref[0])
        for j in range(1, J):
            acc += _nn(a_ref[j], wa_ref[j]) + _nn(b_ref[j], wb_ref[j])
        return acc

    act = pl.BlockSpec((J, tm, k), lambda jn, i: (0, i, 0))
    wsp = pl.BlockSpec((J, k, tn), lambda jn, i: (0, 0, jn))
    return _mm_step(name, fn, [a3, w3a, b3, w3b], [act, wsp, act, wsp],
                    jax.ShapeDtypeStruct((M, N), F32), pl.BlockSpec((tm, tn), lambda jn, i: (i, jn)), (N // tn, M // tm))


def _mm_cols_nt(name, a, w3, tn=512):
    M = a.shape[0]
    J, N, n = w3.shape
    tm = _row_tile(M, 544)
    tn = _col_tile(N, tn)
    g = _shards_per_dot(J, n)

    def fn(a_ref, w_ref):
        acc = None
        for j in range(0, J, g):
            w = w_ref[j] if g == 1 else jnp.concatenate([w_ref[j + s] for s in range(g)], axis=1)
            part = _nt(a_ref[:, j * n:(j + g) * n], w)
            acc = part if acc is None else acc + part
        return acc

    return _mm_step(name, fn, [a, w3],
                    [pl.BlockSpec((tm, J * n), lambda jn, i: (i, 0)), pl.BlockSpec((J, tn, n), lambda jn, i: (0, jn, 0))],
                    jax.ShapeDtypeStruct((M, N), F32), pl.BlockSpec((tm, tn), lambda jn, i: (i, jn)), (N // tn, M // tm))


def _wgrad_cols(name, a, b, J):
    T, K = a.shape
    n = b.shape[1] // J
    return _mm_tn(name, a, b, pl.BlockSpec((T, K), lambda j, t: (0, 0), pipeline_mode=RESIDENT), pl.BlockSpec((T, n), lambda j, t: (0, j)),
                  jax.ShapeDtypeStruct((J, K, n), BF16), pl.BlockSpec((None, K, n), lambda j, t: (j, 0, 0)), (K, n), (J, 1))


def _wgrad_rows(name, a, b, tk=512):
    T, K = a.shape
    N = b.shape[1]
    tk = _col_tile(K, tk)
    return _mm_tn(name, a, b, pl.BlockSpec((T, tk), lambda kb, t: (0, kb)), pl.BlockSpec((T, N), lambda kb, t: (0, 0), pipeline_mode=RESIDENT),
                  jax.ShapeDtypeStruct((K, N), BF16), pl.BlockSpec((tk, N), lambda kb, t: (kb, 0)), (tk, N), (K // tk, 1))


def _wgrad_down(name, a3, b):
    J, T, k = a3.shape
    N = b.shape[1]
    return _mm_tn(name, a3, b, pl.BlockSpec((None, T, k), lambda j, t: (j, 0, 0)),
                  pl.BlockSpec((T, N), lambda j, t: (0, 0), pipeline_mode=RESIDENT),
                  jax.ShapeDtypeStruct((J, k, N), BF16), pl.BlockSpec((None, k, N), lambda j, t: (j, 0, 0)), (k, N), (J, 1))


def _seg(i):
    return jnp.minimum(i, 1)


def _rstd(x):
    return lax.rsqrt(jnp.mean(x * x, axis=-1, keepdims=True) + EPS)


def _norm_mod(name, h, g, mod, which):
    T, D = h.shape

    def body(h_ref, g_ref, mod_ref, o_ref):
        x = h_ref[...]
        n = x * _rstd(x) * g_ref[...]
        shift = mod_ref[3 * which:3 * which + 1, :]
        scale = mod_ref[3 * which + 1:3 * which + 2, :]
        o_ref[...] = (n * (1 + scale) + shift).astype(o_ref.dtype)

    row = pl.BlockSpec((ROW_TILE, D), lambda i: (i, 0))
    return pl.pallas_call(
        body, name=name, grid=(T // ROW_TILE,),
        in_specs=[row, pl.BlockSpec((1, D), lambda i: (0, 0)), pl.BlockSpec((None, 8, D), lambda i: (_seg(i), 0, 0))],
        out_specs=row, out_shape=jax.ShapeDtypeStruct((T, D), BF16), compiler_params=_params())(h, g, mod)


def _norm_mod_bwd(name, dxn, h, g, mod, which, dres, latent_only=False, gate=None):
    T, D = h.shape
    n_gate = 0 if gate is None else 2

    def body(dxn_ref, h_ref, g_ref, mod_ref, dres_ref, *rest):
        dh_ref, dmod_ref, dg_ref = rest[n_gate:n_gate + 3]
        i = pl.program_id(0)
        x = h_ref[...]
        r = _rstd(x)
        xhat = x * r
        g = g_ref[...]
        n = xhat * g
        scale = mod_ref[3 * which + 1:3 * which + 2, :]
        dxn = dxn_ref[...]
        dn = dxn * (1 + scale)
        dxh = dn * g
        dh = dres_ref[...] + r * (dxh - xhat * jnp.mean(dxh * xhat, axis=-1, keepdims=True))
        if latent_only:
            @pl.when(i > 0)
            def _():
                dh_ref[...] = dh
        else:
            dh_ref[...] = dh

        @pl.when(i <= 1)
        def _():
            dmod_ref[...] = jnp.zeros_like(dmod_ref)

        @pl.when(i == 0)
        def _():
            dg_ref[...] = jnp.zeros_like(dg_ref)

        dmod_ref[3 * which:3 * which + 1, :] += jnp.sum(dxn, axis=0, keepdims=True)
        dmod_ref[3 * which + 1:3 * which + 2, :] += jnp.sum(dxn * n, axis=0, keepdims=True)
        dg_ref[0:1, :] += jnp.sum(dn * xhat, axis=0, keepdims=True)

        if gate is not None:
            y_ref, gmod_ref = rest[:2]
            dy_ref, dgmod_ref = rest[5:7]
            dy_ref[...] = (dh * gmod_ref[gate[2]:gate[2] + 1, :]).astype(dy_ref.dtype)

            @pl.when(i <= 1)
            def _():
                dgmod_ref[...] = jnp.zeros_like(dgmod_ref)

            dgmod_ref[gate[2]:gate[2] + 1, :] += jnp.sum(dh * y_ref[...], axis=0, keepdims=True)

    row = pl.BlockSpec((ROW_TILE, D), lambda i: (i, 0))
    modspec = pl.BlockSpec((None, 8, D), lambda i: (_seg(i), 0, 0))
    dh_rows = T - ROW_TILE if latent_only else T
    dh_spec = pl.BlockSpec((ROW_TILE, D), lambda i: (jnp.maximum(i - 1, 0), 0)) if latent_only else row
    gate_ins, gate_specs, gate_outs, gate_shapes = [], [], [], []
    if gate is not None:
        gate_ins, gate_specs = [gate[0], gate[1]], [row, modspec]
        gate_outs, gate_shapes = [row, modspec], [jax.ShapeDtypeStruct((T, D), BF16), jax.ShapeDtypeStruct((2, 8, D), F32)]
    return pl.pallas_call(
        body, name=name, grid=(T // ROW_TILE,),
        in_specs=[row, row, pl.BlockSpec((1, D), lambda i: (0, 0)), modspec, row] + gate_specs,
        out_specs=[dh_spec, modspec, pl.BlockSpec((8, D), lambda i: (0, 0))] + gate_outs,
        out_shape=[jax.ShapeDtypeStruct((dh_rows, D), F32), jax.ShapeDtypeStruct((2, 8, D), F32), jax.ShapeDtypeStruct((8, D), F32)] + gate_shapes,
        compiler_params=_params())(dxn, h, g, mod, dres, *gate_ins)


def _gate_bwd(name, dh, y, mod, row_idx):
    T, D = dh.shape

    def body(dh_ref, y_ref, mod_ref, dy_ref, dmod_ref):
        i = pl.program_id(0)
        dh = dh_ref[...]
        dy_ref[...] = (dh * mod_ref[row_idx:row_idx + 1, :]).astype(dy_ref.dtype)

        @pl.when(i <= 1)
        def _():
            dmod_ref[...] = jnp.zeros_like(dmod_ref)

        dmod_ref[row_idx:row_idx + 1, :] += jnp.sum(dh * y_ref[...], axis=0, keepdims=True)

    row = pl.BlockSpec((ROW_TILE, D), lambda i: (i, 0))
    modspec = pl.BlockSpec((None, 8, D), lambda i: (_seg(i), 0, 0))
    return pl.pallas_call(
        body, name=name, grid=(T // ROW_TILE,), in_specs=[row, row, modspec], out_specs=[row, modspec],
        out_shape=[jax.ShapeDtypeStruct((T, D), BF16), jax.ShapeDtypeStruct((2, 8, D), F32)],
        compiler_params=_params())(dh, y, mod)


def _rot(y):
    lane = lax.broadcasted_iota(jnp.int32, y.shape, 1)
    return jnp.where((lane & 32) == 0, pltpu.roll(y, 96, 1), pltpu.roll(y, 32, 1))


def _qk_prep(name, P, q_g, k_g, rope_c, rope_s, cfg):
    T = P.shape[0]
    ATT, KVW = cfg['ATT'], cfg['KVW']

    def body(q_ref, k_ref, v_ref, qg_ref, kg_ref, c_ref, s_ref, qo_ref, ko_ref, vo_ref):
        cc, ss = c_ref[...], s_ref[...]

        def head(x, g):
            y = x * _rstd(x) * g
            return y * cc + _rot(y) * ss

        for hh in range(ATT // HEAD_DIM):
            sl = slice(hh * HEAD_DIM, (hh + 1) * HEAD_DIM)
            qo_ref[:, sl] = (head(q_ref[:, sl], qg_ref[...]) * Q_SCALE).astype(qo_ref.dtype)
        for hh in range(KVW // HEAD_DIM):
            sl = slice(hh * HEAD_DIM, (hh + 1) * HEAD_DIM)
            ko_ref[:, sl] = head(k_ref[:, sl], kg_ref[...]).astype(ko_ref.dtype)
        vo_ref[...] = v_ref[...].astype(vo_ref.dtype)

    kb = ATT // KVW
    gain = pl.BlockSpec((1, HEAD_DIM), lambda i: (0, 0))
    tab = pl.BlockSpec((ROW_TILE, HEAD_DIM), lambda i: (i, 0))
    qs = pl.BlockSpec((ROW_TILE, ATT), lambda i: (i, 0))
    ks = pl.BlockSpec((ROW_TILE, KVW), lambda i: (i, 0))
    return pl.pallas_call(
        body, name=name, grid=(T // ROW_TILE,),
        in_specs=[qs, pl.BlockSpec((ROW_TILE, KVW), lambda i: (i, kb)), pl.BlockSpec((ROW_TILE, KVW), lambda i: (i, kb + 1)),
                  gain, gain, tab, tab],
        out_specs=[qs, ks, ks],
        out_shape=[jax.ShapeDtypeStruct((T, ATT), BF16), jax.ShapeDtypeStruct((T, KVW), BF16), jax.ShapeDtypeStruct((T, KVW), BF16)],
        compiler_params=_params())(P, P, P, q_g, k_g, rope_c, rope_s)


def _qk_prep_bwd(name, dqr, dkr, P, q_g, k_g, rope_c, rope_s, cfg):
    T = P.shape[0]
    ATT, KVW = cfg['ATT'], cfg['KVW']

    def body(dq_ref, dk_ref, q_ref, k_ref, qg_ref, kg_ref, c_ref, s_ref, dqo_ref, dko_ref, dqg_ref, dkg_ref):
        i = pl.program_id(0)
        cc, ss = c_ref[...], s_ref[...]

        @pl.when(i == 0)
        def _():
            dqg_ref[...] = jnp.zeros_like(dqg_ref)
            dkg_ref[...] = jnp.zeros_like(dkg_ref)

        def head(x, g, dout):
            dy = dout * cc + _rot(dout * ss)
            r = _rstd(x)
            xhat = x * r
            dxh = dy * g
            dx = r * (dxh - xhat * jnp.mean(dxh * xhat, axis=-1, keepdims=True))
            return dx, jnp.sum(dy * xhat, axis=0, keepdims=True)

        dg = jnp.zeros((1, HEAD_DIM), F32)
        for hh in range(ATT // HEAD_DIM):
            sl = slice(hh * HEAD_DIM, (hh + 1) * HEAD_DIM)
            dx, d = head(q_ref[:, sl], qg_ref[...], dq_ref[:, sl] * ATT_SCALE)
            dqo_ref[:, sl] = dx.astype(dqo_ref.dtype)
            dg += d
        dqg_ref[0:1, :] += dg
        dg = jnp.zeros((1, HEAD_DIM), F32)
        for hh in range(KVW // HEAD_DIM):
            sl = slice(hh * HEAD_DIM, (hh + 1) * HEAD_DIM)
            dx, d = head(k_ref[:, sl], kg_ref[...], dk_ref[:, sl] * (1.0 / LOG2_E))
            dko_ref[:, sl] = dx.astype(dko_ref.dtype)
            dg += d
        dkg_ref[0:1, :] += dg

    kb = ATT // KVW
    gain = pl.BlockSpec((1, HEAD_DIM), lambda i: (0, 0))
    dgain = pl.BlockSpec((8, HEAD_DIM), lambda i: (0, 0))
    tab = pl.BlockSpec((ROW_TILE, HEAD_DIM), lambda i: (i, 0))
    qs = pl.BlockSpec((ROW_TILE, ATT), lambda i: (i, 0))
    ks = pl.BlockSpec((ROW_TILE, KVW), lambda i: (i, 0))
    return pl.pallas_call(
        body, name=name, grid=(T // ROW_TILE,),
        in_specs=[qs, ks, qs, pl.BlockSpec((ROW_TILE, KVW), lambda i: (i, kb)), gain, gain, tab, tab],
        out_specs=[qs, ks, dgain, dgain],
        out_shape=[jax.ShapeDtypeStruct((T, ATT), BF16), jax.ShapeDtypeStruct((T, KVW), BF16),
                   jax.ShapeDtypeStruct((8, HEAD_DIM), F32), jax.ShapeDtypeStruct((8, HEAD_DIM), F32)],
        compiler_params=_params())(dqr, dkr, P, P, q_g, k_g, rope_c, rope_s)


def _mix_shape(q):
    return jax.ShapeDtypeStruct((q.shape[0], q.shape[1] + AUX_WIDTH), BF16)


def _att_specs(T, G):
    qs = pl.BlockSpec((ROW_TILE, G * HEAD_DIM), lambda h, i: (i, h))
    kvs = pl.BlockSpec((T, HEAD_DIM), lambda h, i: (0, h))
    return qs, kvs


def _attn_dense_fwd(name, q, k, v, cfg):
    T, G, Lc = q.shape[0], cfg['G'], cfg['Lc']

    def body(q_ref, k_ref, v_ref, o_ref, lse_ref, mix_ref):
        def attend(rows):
            kk, vv = k_ref[0:rows, :], v_ref[0:rows, :]
            for g in range(G):
                sl = slice(g * HEAD_DIM, (g + 1) * HEAD_DIM)
                s = _nt(q_ref[:, sl], kk)
                m = jnp.max(s, axis=1, keepdims=True)
                p = jnp.exp2(s - m)
                l = jnp.sum(p, axis=1, keepdims=True)
                o = _nn(_bf(p), vv) / l
                o_ref[:, sl] = o
                mix_ref[:, sl] = _bf(o)
                lse_ref[:, sl] = jnp.broadcast_to(m + jnp.log2(l), (ROW_TILE, HEAD_DIM))

        @pl.when(pl.program_id(1) == 0)
        def _():
            attend(Lc)

        @pl.when(pl.program_id(1) > 0)
        def _():
            attend(T)

    qs, kvs = _att_specs(T, G)
    return pl.pallas_call(
        body, name=name, grid=(cfg['NKV'], T // ROW_TILE), in_specs=[qs, kvs, kvs], out_specs=[qs, qs, qs],
        out_shape=[jax.ShapeDtypeStruct(q.shape, F32), jax.ShapeDtypeStruct(q.shape, F32), _mix_shape(q)],
        compiler_params=_params())(q, k, v)


def _attn_dense_bwd(name, q, k, v, o, lse, dmix, cfg):
    T, G, Lc = q.shape[0], cfg['G'], cfg['Lc']

    def body(q_ref, k_ref, v_ref, o_ref, lse_ref, do_ref, dq_ref, dk_ref, dv_ref):
        i = pl.program_id(1)

        @pl.when(i == 0)
        def _():
            dk_ref[...] = jnp.zeros_like(dk_ref)
            dv_ref[...] = jnp.zeros_like(dv_ref)

        def attend(rows):
            kk, vv = k_ref[0:rows, :], v_ref[0:rows, :]
            for g in range(G):
                sl = slice(g * HEAD_DIM, (g + 1) * HEAD_DIM)
                qg, do = q_ref[:, sl], do_ref[:, sl]
                delta = jnp.sum(do * o_ref[:, sl], axis=1, keepdims=True)
                p = jnp.exp2(_nt(qg, kk) - lse_ref[:, g * HEAD_DIM:g * HEAD_DIM + 1])
                dob = _bf(do)
                dv_ref[0:rows, :] += _tn(_bf(p), dob)
                ds = _bf(p * (_nt(dob, vv) - delta))
                dq_ref[:, sl] = _nn(ds, kk)
                dk_ref[0:rows, :] += _tn(ds, qg)

        @pl.when(i == 0)
        def _():
            attend(Lc)

        @pl.when(i > 0)
        def _():
            attend(T)

    qs, kvs = _att_specs(T, G)
    return pl.pallas_call(
        body, name=name, grid=(cfg['NKV'], T // ROW_TILE), in_specs=[qs, kvs, kvs, qs, qs, qs], out_specs=[qs, kvs, kvs],
        out_shape=[jax.ShapeDtypeStruct(q.shape, F32), jax.ShapeDtypeStruct(k.shape, F32), jax.ShapeDtypeStruct(k.shape, F32)],
        compiler_params=_params())(q, k, v, o, lse, dmix)


def _band(i, T, Lc):
    start = pl.multiple_of(jnp.clip(WINDOW + (i - 1) * ROW_TILE, 0, T - BAND), WINDOW)
    qpos = (i - 1) * ROW_TILE + lax.broadcasted_iota(jnp.int32, (ROW_TILE, 1), 0)
    kpos = start - Lc + lax.broadcasted_iota(jnp.int32, (1, BAND), 1)
    ok = (jnp.abs(kpos - qpos) <= WINDOW) & (kpos >= 0) & (i > 0)
    return start, jnp.where(ok, 0.0, NEG_INF).astype(F32)


def _attn_win_fwd(name, q, k, v, sink, cfg):
    T, G, Lc = q.shape[0], cfg['G'], cfg['Lc']

    def body(sink_ref, q_ref, k_ref, v_ref, o_ref, lse_ref, mix_ref):
        h, i = pl.program_id(0), pl.program_id(1)
        start, bias = _band(i, T, Lc)
        kc, vc = k_ref[0:Lc, :], v_ref[0:Lc, :]
        kb, vb = k_ref[pl.ds(start, BAND), :], v_ref[pl.ds(start, BAND), :]
        for g in range(G):
            sl = slice(g * HEAD_DIM, (g + 1) * HEAD_DIM)
            qg = q_ref[:, sl]
            sk = sink_ref[h * G + g] * LOG2_E
            sc = _nt(qg, kc)
            sb = _nt(qg, kb) + bias
            m = jnp.maximum(jnp.maximum(jnp.max(sc, axis=1, keepdims=True), jnp.max(sb, axis=1, keepdims=True)), sk)
            pc, pb = jnp.exp2(sc - m), jnp.exp2(sb - m)
            l = jnp.sum(pc, axis=1, keepdims=True) + jnp.sum(pb, axis=1, keepdims=True) + jnp.exp2(sk - m)
            o = (_nn(_bf(pc), vc) + _nn(_bf(pb), vb)) / l
            o_ref[:, sl] = o
            mix_ref[:, sl] = _bf(o)
            lse_ref[:, sl] = jnp.broadcast_to(m + jnp.log2(l), (ROW_TILE, HEAD_DIM))

    qs, kvs = _att_specs(T, G)
    return pl.pallas_call(
        body, name=name, grid=(cfg['NKV'], T // ROW_TILE),
        in_specs=[pl.BlockSpec(memory_space=pltpu.SMEM), qs, kvs, kvs], out_specs=[qs, qs, qs],
        out_shape=[jax.ShapeDtypeStruct(q.shape, F32), jax.ShapeDtypeStruct(q.shape, F32), _mix_shape(q)],
        compiler_params=_params())(sink, q, k, v)


def _attn_win_bwd(name, q, k, v, o, lse, dmix, sink, cfg):
    T, G, Lc = q.shape[0], cfg['G'], cfg['Lc']

    def body(sink_ref, q_ref, k_ref, v_ref, o_ref, lse_ref, do_ref, dq_ref, dk_ref, dv_ref, dsink_ref):
        h, i = pl.program_id(0), pl.program_id(1)
        start, bias = _band(i, T, Lc)
        kc, vc = k_ref[0:Lc, :], v_ref[0:Lc, :]
        kb, vb = k_ref[pl.ds(start, BAND), :], v_ref[pl.ds(start, BAND), :]

        @pl.when(i == 0)
        def _():
            dk_ref[...] = jnp.zeros_like(dk_ref)
            dv_ref[...] = jnp.zeros_like(dv_ref)
            dsink_ref[...] = jnp.zeros_like(dsink_ref)

        for g in range(G):
            sl = slice(g * HEAD_DIM, (g + 1) * HEAD_DIM)
            qg, do = q_ref[:, sl], do_ref[:, sl]
            lse = lse_ref[:, g * HEAD_DIM:g * HEAD_DIM + 1]
            delta = jnp.sum(do * o_ref[:, sl], axis=1, keepdims=True)
            pc = jnp.exp2(_nt(qg, kc) - lse)
            pb = jnp.exp2(_nt(qg, kb) + bias - lse)
            ps = jnp.exp2(sink_ref[h * G + g] * LOG2_E - lse)
            dob = _bf(do)
            dv_ref[0:Lc, :] += _tn(_bf(pc), dob)
            dv_ref[pl.ds(start, BAND), :] += _tn(_bf(pb), dob)
            dsc = _bf(pc * (_nt(dob, vc) - delta))
            dsb = _bf(pb * (_nt(dob, vb) - delta))
            dq_ref[:, sl] = _nn(dsc, kc) + _nn(dsb, kb)
            dk_ref[0:Lc, :] += _tn(dsc, qg)
            dk_ref[pl.ds(start, BAND), :] += _tn(dsb, qg)
            dsk = jnp.where(i > 0, -jnp.sum(ps * delta, axis=0, keepdims=True), 0.0)
            dsink_ref[:, sl] += jnp.broadcast_to(dsk, (8, HEAD_DIM))

    qs, kvs = _att_specs(T, G)
    return pl.pallas_call(
        body, name=name, grid=(cfg['NKV'], T // ROW_TILE),
        in_specs=[pl.BlockSpec(memory_space=pltpu.SMEM), qs, kvs, kvs, qs, qs, qs],
        out_specs=[qs, kvs, kvs, pl.BlockSpec((None, 8, G * HEAD_DIM), lambda h, i: (h, 0, 0))],
        out_shape=[jax.ShapeDtypeStruct(q.shape, F32), jax.ShapeDtypeStruct(k.shape, F32), jax.ShapeDtypeStruct(k.shape, F32),
                   jax.ShapeDtypeStruct((cfg['NKV'], 8, G * HEAD_DIM), F32)],
        compiler_params=_params())(sink, q, k, v, o, lse, dmix)


def _seq_pos(T, Lc):
    row = lax.broadcasted_iota(jnp.int32, (T, 1), 0)
    return jnp.where(row < Lc, row, row - Lc), jnp.where(row < Lc, Lc, T - Lc)


def _fw(x, k, pos, seglen):
    return jnp.where(pos + k < seglen, pltpu.roll(x, x.shape[0] - k, 0), 0.0)


def _bw(x, k, pos):
    return jnp.where(pos - k >= 0, pltpu.roll(x, k, 0), 0.0)


def _conv_fwd(name, P, conv_w8, mix, cfg):
    T, Lc = P.shape[0], cfg['Lc']
    cb = (cfg['ATT'] + 2 * cfg['KVW']) // HEAD_DIM
    ob = cfg['ATT'] // HEAD_DIM
    na = AUX_WIDTH // HEAD_DIM

    def body(gb_ref, gc_ref, u_ref, w_ref, mix_ref, o_ref):
        pos, seglen = _seq_pos(T, Lc)
        z = gc_ref[...] * u_ref[...]
        conv = w_ref[0:1, :] * _bw(z, 1, pos) + w_ref[1:2, :] * z + w_ref[2:3, :] * _fw(z, 1, pos, seglen)
        o_ref[...] = (gb_ref[...] * conv).astype(o_ref.dtype)

    col = lambda off: pl.BlockSpec((T, HEAD_DIM), lambda c: (0, cb + off + c))
    return pl.pallas_call(
        body, name=name, grid=(na,),
        in_specs=[col(0), col(na), col(2 * na), pl.BlockSpec((8, HEAD_DIM), lambda c: (0, c)), pl.BlockSpec(memory_space=pl.ANY)],
        out_specs=pl.BlockSpec((T, HEAD_DIM), lambda c: (0, ob + c)),
        out_shape=jax.ShapeDtypeStruct(mix.shape, mix.dtype), input_output_aliases={4: 0},
        compiler_params=_params())(P, P, P, conv_w8, mix)


def _conv_bwd(name, P, conv_w8, dmix, cfg):
    T, Lc = P.shape[0], cfg['Lc']
    cb = (cfg['ATT'] + 2 * cfg['KVW']) // HEAD_DIM
    ob = cfg['ATT'] // HEAD_DIM
    na = AUX_WIDTH // HEAD_DIM

    def body(gb_ref, gc_ref, u_ref, w_ref, do_ref, dgb_ref, dgc_ref, du_ref, dw_ref):
        pos, seglen = _seq_pos(T, Lc)
        gc, u, do = gc_ref[...], u_ref[...], do_ref[...]
        z = gc * u
        zm, zp = _bw(z, 1, pos), _fw(z, 1, pos, seglen)
        w0, w1, w2 = w_ref[0:1, :], w_ref[1:2, :], w_ref[2:3, :]
        dgb_ref[...] = (do * (w0 * zm + w1 * z + w2 * zp)).astype(dgb_ref.dtype)
        dc = do * gb_ref[...]
        dz = w0 * _fw(dc, 1, pos, seglen) + w1 * dc + w2 * _bw(dc, 1, pos)
        dgc_ref[...] = (dz * u).astype(dgc_ref.dtype)
        du_ref[...] = (dz * gc).astype(du_ref.dtype)
        dw_ref[...] = jnp.zeros_like(dw_ref)
        dw_ref[0:1, :] = jnp.sum(dc * zm, axis=0, keepdims=True)
        dw_ref[1:2, :] = jnp.sum(dc * z, axis=0, keepdims=True)
        dw_ref[2:3, :] = jnp.sum(dc * zp, axis=0, keepdims=True)

    col = lambda off: pl.BlockSpec((T, HEAD_DIM), lambda c: (0, cb + off + c))
    wspec = pl.BlockSpec((8, HEAD_DIM), lambda c: (0, c))
    ocol = lambda off: pl.BlockSpec((T, HEAD_DIM), lambda c: (0, off + c))
    return pl.pallas_call(
        body, name=name, grid=(na,),
        in_specs=[col(0), col(na), col(2 * na), wspec, ocol(ob)],
        out_specs=[ocol(0), ocol(0), ocol(0), wspec],
        out_shape=[jax.ShapeDtypeStruct((T, AUX_WIDTH), BF16)] * 3 + [jax.ShapeDtypeStruct((8, AUX_WIDTH), F32)],
        compiler_params=_params())(P, P, P, conv_w8, dmix)


def _window_sums(x, half, pos, seglen):
    fwd, bwd = x, x
    s = 1
    while s < half:
        fwd = fwd + _fw(fwd, s, pos, seglen)
        bwd = bwd + _bw(bwd, s, pos)
        s *= 2
    return fwd, bwd


def _pooled(u, half, pos, seglen):
    fwd, bwd = _window_sums(u, half, pos, seglen)
    cnt = (jnp.minimum(pos + half, seglen) - jnp.maximum(pos - half, 0)).astype(F32)
    return (fwd + _bw(bwd, 1, pos)) / cnt - u, cnt


def _pool_fwd(name, P, pool_w, pool_scale, mix, cfg):
    T, Lc = P.shape[0], cfg['Lc']
    cb = (cfg['ATT'] + 2 * cfg['KVW']) // HEAD_DIM
    ob = cfg['ATT'] // HEAD_DIM

    def body(u_ref, w_ref, s_ref, mix_ref, o_ref):
        g = pl.program_id(0)
        pos, seglen = _seq_pos(T, Lc)
        for k, half in enumerate(POOL_HALF):
            @pl.when(g == k)
            def _(half=half):
                pooled, _ = _pooled(u_ref[...], half, pos, seglen)
                o_ref[...] = (_nn(_bf(pooled), _bf(w_ref[...])) * s_ref[...]).astype(o_ref.dtype)

    return pl.pallas_call(
        body, name=name, grid=(AUX_GROUPS,),
        in_specs=[pl.BlockSpec((T, HEAD_DIM), lambda g: (0, cb + g)), pl.BlockSpec((None, HEAD_DIM, HEAD_DIM), lambda g: (g, 0, 0)),
                  pl.BlockSpec((1, HEAD_DIM), lambda g: (0, g)), pl.BlockSpec(memory_space=pl.ANY)],
        out_specs=pl.BlockSpec((T, HEAD_DIM), lambda g: (0, ob + g)),
        out_shape=jax.ShapeDtypeStruct(mix.shape, mix.dtype), input_output_aliases={3: 0},
        compiler_params=_params())(P, pool_w, pool_scale, mix)


def _pool_bwd(name, P, pool_w, pool_scale, dmix, cfg):
    T, Lc = P.shape[0], cfg['Lc']
    cb = (cfg['ATT'] + 2 * cfg['KVW']) // HEAD_DIM
    ob = cfg['ATT'] // HEAD_DIM

    def body(u_ref, w_ref, s_ref, do_ref, du_ref, dw_ref, ds_ref):
        g = pl.program_id(0)
        pos, seglen = _seq_pos(T, Lc)
        for k, half in enumerate(POOL_HALF):
            @pl.when(g == k)
            def _(half=half):
                do = do_ref[...]
                pooled, cnt = _pooled(u_ref[...], half, pos, seglen)
                wb = _bf(w_ref[...])
                mixed = _nn(_bf(pooled), wb)
                ds_ref[...] = jnp.broadcast_to(jnp.sum(do * mixed, axis=0, keepdims=True), ds_ref.shape)
                dmixed = _bf(do * s_ref[...])
                dw_ref[...] = _tn(_bf(pooled), dmixed)
                dpooled = _nt(dmixed, wb)
                e = dpooled / cnt
                fwd, bwd = _window_sums(e, half, pos, seglen)
                adj = fwd + _fw(e, half, pos, seglen) + _bw(bwd, 1, pos) - _bw(e, half, pos)
                du_ref[...] = (adj - dpooled).astype(du_ref.dtype)

    wspec = pl.BlockSpec((None, HEAD_DIM, HEAD_DIM), lambda g: (g, 0, 0))
    return pl.pallas_call(
        body, name=name, grid=(AUX_GROUPS,),
        in_specs=[pl.BlockSpec((T, HEAD_DIM), lambda g: (0, cb + g)), wspec, pl.BlockSpec((1, HEAD_DIM), lambda g: (0, g)),
                  pl.BlockSpec((T, HEAD_DIM), lambda g: (0, ob + g))],
        out_specs=[pl.BlockSpec((T, HEAD_DIM), lambda g: (0, g)), wspec, pl.BlockSpec((8, HEAD_DIM), lambda g: (0, g))],
        out_shape=[jax.ShapeDtypeStruct((T, AUX_WIDTH), BF16), jax.ShapeDtypeStruct(pool_w.shape, F32),
                   jax.ShapeDtypeStruct((8, AUX_WIDTH), F32)],
        compiler_params=_params())(P, pool_w, pool_scale, dmix)


def _ffn_up(name, hn, wg3, wu3, dep=None):
    T, D = hn.shape
    J, k, _ = wg3.shape
    tm = _row_tile(T, 1088)
    dep_ins, dep_specs = _dep(dep, 2)

    def body(x_ref, wg_ref, wu_ref, *rest):
        s_ref, ud_ref, a_ref = rest[len(dep_ins):]
        x = x_ref[pl.ds(pl.multiple_of(pl.program_id(1) * tm, 16), tm), :]
        g, u = _nt(x, wg_ref[...]), _nt(x, wu_ref[...])
        sig = jax.nn.sigmoid(g)
        silu = g * sig
        s_ref[...] = silu.astype(s_ref.dtype)
        ud_ref[...] = (u * (sig * (1 + g * (1 - sig)))).astype(ud_ref.dtype)
        a_ref[...] = (silu * u).astype(a_ref.dtype)

    wspec = pl.BlockSpec((None, k, D), lambda j, i: (j, 0, 0))
    ospec = pl.BlockSpec((None, tm, k), lambda j, i: (j, i, 0))
    return pl.pallas_call(
        body, name=name, grid=(J, T // tm),
        in_specs=[pl.BlockSpec((T, D), lambda j, i: (0, 0), pipeline_mode=RESIDENT), wspec, wspec] + dep_specs,
        out_specs=[ospec, ospec, ospec],
        out_shape=[jax.ShapeDtypeStruct((J, T, k), BF16)] * 3,
        compiler_params=_params())(hn, wg3, wu3, *dep_ins)


def _ffn_dact(name, dF, wd3, silu_g, u_dsilu):
    T, D = dF.shape
    J, k, _ = wd3.shape
    tm = _row_tile(T, 1088)

    def body(df_ref, wd_ref, s_ref, ud_ref, dg_ref, du_ref):
        rows = pl.ds(pl.multiple_of(pl.program_id(1) * tm, 16), tm)
        da = _nt(df_ref[rows, :], wd_ref[...])
        du_ref[...] = (da * s_ref[...].astype(F32)).astype(du_ref.dtype)
        dg_ref[...] = (da * ud_ref[...].astype(F32)).astype(dg_ref.dtype)

    aspec = pl.BlockSpec((None, tm, k), lambda j, i: (j, i, 0))
    return pl.pallas_call(
        body, name=name, grid=(J, T // tm),
        in_specs=[pl.BlockSpec((T, D), lambda j, i: (0, 0), pipeline_mode=RESIDENT), pl.BlockSpec((None, k, D), lambda j, i: (j, 0, 0)), aspec, aspec],
        out_specs=[aspec, aspec],
        out_shape=[jax.ShapeDtypeStruct((J, T, k), BF16), jax.ShapeDtypeStruct((J, T, k), BF16)],
        compiler_params=_params())(dF, wd3, silu_g, u_dsilu)


def _loss_head(name, h, g, target, cfg):
    T, D = h.shape

    def body(h_ref, g_ref, t_ref, dh_ref, loss_ref, dg_ref):
        i = pl.program_id(0)

        @pl.when(i == 0)
        def _():
            dh_ref[...] = jnp.zeros_like(dh_ref)
            loss_ref[...] = jnp.zeros_like(loss_ref)
            dg_ref[...] = jnp.zeros_like(dg_ref)

        @pl.when(i > 0)
        def _():
            x = h_ref[...]
            r = _rstd(x)
            xhat = x * r
            gg = g_ref[...]
            err = xhat * gg - t_ref[...]
            loss_ref[...] += 0.5 * jnp.sum(jnp.sum(err * err, axis=1, keepdims=True) / D, axis=0, keepdims=True)
            dy = err / D
            dg_ref[0:1, :] += jnp.sum(dy * xhat, axis=0, keepdims=True)
            dxh = dy * gg
            dh_ref[...] = r * (dxh - xhat * jnp.mean(dxh * xhat, axis=-1, keepdims=True))

    row = pl.BlockSpec((ROW_TILE, D), lambda i: (i, 0))
    return pl.pallas_call(
        body, name=name, grid=(T // ROW_TILE,),
        in_specs=[row, pl.BlockSpec((1, D), lambda i: (0, 0)), pl.BlockSpec((ROW_TILE, D), lambda i: (jnp.maximum(i - 1, 0), 0))],
        out_specs=[row, pl.BlockSpec((8, 128), lambda i: (0, 0)), pl.BlockSpec((8, D), lambda i: (0, 0))],
        out_shape=[jax.ShapeDtypeStruct((T, D), F32), jax.ShapeDtypeStruct((8, 128), F32), jax.ShapeDtypeStruct((8, D), F32)],
        compiler_params=_params())(h, g, target)


def _adamw(name, parts, w, m, v, dep=None):
    R, C = w.shape
    n_parts = parts.shape[0]
    tr = _row_tile(R, max(16, ADAM_BLOCK_ELEMS // C)) if R % 16 == 0 else R
    bc1 = 1.0 - ADAM_B1 ** ADAM_STEP
    bc2 = 1.0 - ADAM_B2 ** ADAM_STEP
    dep_ins, dep_specs = _dep(dep, 1)

    def body(p_ref, w_ref, m_ref, v_ref, *rest):
        g_ref, d_ref, nm_ref, nv_ref = rest[len(dep_ins):]
        g = p_ref[0].astype(F32)
        for k in range(1, n_parts):
            g = g + p_ref[k].astype(F32)
        nm = ADAM_B1 * m_ref[...] + (1.0 - ADAM_B1) * g
        nv = ADAM_B2 * v_ref[...] + (1.0 - ADAM_B2) * (g * g)
        g_ref[...] = g
        nm_ref[...] = nm
        nv_ref[...] = nv
        d_ref[...] = -ADAM_LR * ((nm / bc1) / (jnp.sqrt(nv / bc2) + ADAM_EPS) + ADAM_WD * w_ref[...])

    blk = pl.BlockSpec((tr, C), lambda i: (i, 0))
    return pl.pallas_call(
        body, name=name, grid=(R // tr,), in_specs=[pl.BlockSpec((n_parts, tr, C), lambda i: (0, i, 0)), blk, blk, blk] + dep_specs,
        out_specs=[blk] * 4, out_shape=[jax.ShapeDtypeStruct((R, C), F32)] * 4, compiler_params=_params())(parts, w, m, v, *dep_ins)


class _WeightStream:
    def __init__(self, cast):
        self.cast, self.handles = cast, {}

    @staticmethod
    def _tag(l, group):
        return ("ffn" if group is FFN_WEIGHTS else group[0]) + str(l)

    def start(self, l, group, after=None):
        self.handles[l, group], token = _gather_start(f"gather_{self._tag(l, group)}_start", [self.cast(l, n) for n in group], after)
        return token

    def relay(self, l, group, after):
        self.handles[l, group], token = _gather_relay(f"gather_{self._tag(l, group)}_relay", self.handles[l, group], after)
        return token

    def get(self, l, group, after):
        got = dict(zip(group, _gather_wait(f"gather_{self._tag(l, group)}_wait", self.handles[l, group], after)))
        if 'w_out' in got:
            rows, cols = got['w_out'].shape[1:]
            got['w_out'] = got['w_out'].reshape(N_DEV * rows, cols)
        return got


def _layer_fwd(l, h, p, stream, mod, rope, conv_w8, cfg):
    nm = f"l{l}_"
    mod = mod + stream.relay(l, IN_WEIGHT, h)[0, 0]
    xn = _norm_mod(nm + "norm1", h, p['norm1_g'], mod, 0)
    W = stream.get(l, IN_WEIGHT, xn)
    token = None
    if l == 0:
        token = stream.start(0, OUT_WEIGHT, after=W['w_in']) + stream.start(0, FFN_WEIGHTS, after=W['w_in'])
    P = _mm_cols(nm + "w_in", xn, W['w_in'], dep=token)
    qr, kr, vb = _qk_prep(nm + "qk_prep", P, p['q_norm_g'], p['k_norm_g'], rope[0], rope[1], cfg)
    if l == 0:
        o, lse, mix = _attn_dense_fwd(nm + "attn", qr, kr, vb, cfg)
        mix = _conv_fwd(nm + "conv", P, conv_w8, mix, cfg)
    else:
        o, lse, mix = _attn_win_fwd(nm + "attn", qr, kr, vb, p['sink'], cfg)
        mix = _pool_fwd(nm + "pool", P, p['pool_w'], p['pool_scale'], mix, cfg)
    W.update(stream.get(l, OUT_WEIGHT, stream.relay(l, OUT_WEIGHT, mix)))
    y, h2 = _mm_plain(nm + "w_out", mix, W['w_out'], False, dep=stream.relay(l, FFN_WEIGHTS, o), res=(h, mod, 2, cfg['Lc']))
    hn = _norm_mod(nm + "norm2", h2, p['norm2_g'], mod, 1)
    W.update(stream.get(l, FFN_WEIGHTS, hn))
    token = stream.start(1, IN_WEIGHT, after=W['w_down']) if l == 0 else None
    silu_g, u_dsilu, A = _ffn_up(nm + "ffn_up", hn, W['w_gate'], W['w_up'], dep=token)
    if l == 0:
        token = stream.start(1, OUT_WEIGHT, after=A) + stream.start(1, FFN_WEIGHTS, after=A)
    F, h3 = _mm_shards_nn(nm + "w_down", A, W['w_down'], dep=token, res=(h2, mod, 5, cfg['Lc']))
    saved = dict(h=h, xn=xn, P=P, qr=qr, kr=kr, vb=vb, o=o, lse=lse, mix=mix, y=y, h2=h2, hn=hn, silu_g=silu_g, u_dsilu=u_dsilu, A=A, F=F)
    return h3, saved, W


def _layer_bwd(l, dh3, s, p, W, mod, rope, conv_w8, cfg, res2_bwd=None, below=None):
    nm = f"l{l}_bwd_"
    J = N_DEV
    dF, dmod = _gate_bwd(nm + "res2", dh3, s['F'], mod, 5) if res2_bwd is None else res2_bwd
    dG, dU = _ffn_dact(nm + "ffn_act", dF, W['w_down'], s['silu_g'], s['u_dsilu'])
    big = {'w_down': _wgrad_down(nm + "dw_down", s['A'], dF),
           'w_gate': _wgrad_down(nm + "dw_gate", dG, s['hn']),
           'w_up': _wgrad_down(nm + "dw_up", dU, s['hn'])}
    handles = {}
    handles['ffn'], token = _exchange_start(f"scatter_ffn{l}_start", [big[n] for n in FFN_WEIGHTS], True)
    mod = mod + token[0, 0]
    dhn = _mm_shards_nn2(nm + "dhn", dG, W['w_gate'], dU, W['w_up'])
    dh2, dm, dg2, dY, dm_gate = _norm_mod_bwd(nm + "norm2", dhn, s['h2'], p['norm2_g'], mod, 1, dh3, gate=(s['y'], mod, 2))
    dmod += dm + dm_gate
    dwo = _wgrad_rows(nm + "dw_out", s['mix'], dY)
    handles['w_out'], token = _exchange_start(f"scatter_w_out{l}_start", [dwo.reshape((J, dwo.shape[0] // J, dwo.shape[1]))], True)
    dmix = _mm_plain(nm + "dmix", dY, W['w_out'], True, dep=token)
    small = {'norm2_g': dg2[0]}
    if l == 0:
        dqr, dkr, dv = _attn_dense_bwd(nm + "attn", s['qr'], s['kr'], s['vb'], s['o'], s['lse'], dmix, cfg)
        *daux, dcw = _conv_bwd(nm + "conv", s['P'], conv_w8, dmix, cfg)
        small['conv_w'] = dcw[0:3]
    else:
        dqr, dkr, dv, dsk = _attn_win_bwd(nm + "attn", s['qr'], s['kr'], s['vb'], s['o'], s['lse'], dmix, p['sink'], cfg)
        du, dpw, dps = _pool_bwd(nm + "pool", s['P'], p['pool_w'], p['pool_scale'], dmix, cfg)
        daux = [du]
        small.update(sink=dsk[:, 0, ::HEAD_DIM].reshape(-1), pool_w=dpw, pool_scale=dps[0])
    dq, dk, dqg, dkg = _qk_prep_bwd(nm + "qk_prep", dqr, dkr, s['P'], p['q_norm_g'], p['k_norm_g'], rope[0], rope[1], cfg)
    small.update(q_norm_g=dqg[0], k_norm_g=dkg[0])
    dP = jnp.concatenate([dq, dk, dv.astype(BF16), *daux], axis=1)
    handles['w_in'], token = _exchange_start(f"scatter_w_in{l}_start", [_wgrad_cols(nm + "dw_in", s['xn'], dP, J)], True)
    mod = mod + token[0, 0]
    dxn = _mm_cols_nt(nm + "dxn", dP, W['w_in'])
    gate = None if below is None else (below[0], below[1], 5)
    dh, dm, dg1, *res2_below = _norm_mod_bwd(nm + "norm1", dxn, s['h'], p['norm1_g'], mod, 0, dh2, latent_only=(l == 0), gate=gate)
    dmod += dm
    small['norm1_g'] = dg1[0]
    return dh, dmod, small, handles, token, (tuple(res2_below) or None)


def _rope_tables(S, Lc):
    half = HEAD_DIM // 4
    pos = np.arange(S)
    inv = ROPE_THETA ** (-np.arange(0, 2 * half, 2, dtype=np.float32) / (2 * half))
    inv = jnp.asarray(inv, F32)
    ang_r = jnp.asarray(pos // GRID_W, F32)[:, None] * inv
    ang_c = jnp.asarray(pos % GRID_W, F32)[:, None] * inv
    cos = jnp.concatenate([jnp.cos(ang_r)] * 2 + [jnp.cos(ang_c)] * 2, axis=1)
    sin = jnp.concatenate([-jnp.sin(ang_r), jnp.sin(ang_r), -jnp.sin(ang_c), jnp.sin(ang_c)], axis=1)
    return (jnp.concatenate([jnp.ones((Lc, HEAD_DIM), F32), cos], axis=0),
            jnp.concatenate([jnp.zeros((Lc, HEAD_DIM), F32), sin], axis=0))


def _pad_rows(a, rows):
    return jnp.concatenate([a, jnp.zeros((rows - a.shape[0],) + a.shape[1:], a.dtype)], axis=0)


def _flat128(a, nlead):
    lead = a.shape[:nlead]
    f = a.reshape(lead + (-1,))
    pad = (-f.shape[-1]) % 128
    if pad:
        f = jnp.concatenate([f, jnp.zeros(lead + (pad,), f.dtype)], axis=-1)
    return f.reshape(lead + (-1, 128))


def _pack(named, nlead=0):
    rows, layout, at = [], {}, 0
    for name, a in named:
        f = _flat128(a, nlead)
        n = f.shape[-2]
        pad = (-n) % 8
        if pad:
            f = jnp.concatenate([f, jnp.zeros(f.shape[:-2] + (pad, 128), f.dtype)], axis=-2)
        layout[name] = (at, n, a.shape[nlead:])
        rows.append(f)
        at += n + pad
    return jnp.concatenate(rows, axis=-2), layout


def _unpack(arr, layout, name):
    at, n, shape = layout[name]
    return arr[..., at:at + n, :].reshape(arr.shape[:-2] + (-1,))[..., :math.prod(shape)].reshape(arr.shape[:-2] + tuple(shape))


def kernel(*args):
    A = dict(zip(INPUT_NAMES, args, strict=True))
    x, ctx = A['x'][0], A['ctx'][0]
    S, D = x.shape
    Lc = ctx.shape[0]
    T = Lc + S
    ATT = D - AUX_WIDTH
    KVW = (A['l1_w_in'].shape[1] * N_DEV - ATT - AUX_WIDTH) // 2
    cfg = dict(ATT=ATT, KVW=KVW, NKV=KVW // HEAD_DIM, G=ATT // KVW, Lc=Lc)
    assert Lc == ROW_TILE and S % ROW_TILE == 0 and T >= BAND and S % GRID_W == 0
    cw = A['l0_conv_w'].shape[1]
    me = 4 * lax.axis_index("x") + 2 * lax.axis_index("y") + lax.axis_index("c")

    def layer_params(l):
        pre = f"l{l}_"
        return {k[len(pre):]: (v.reshape(1, -1) if v.ndim == 1 and k != 'l1_sink' else v) for k, v in A.items() if k.startswith(pre)}

    params = [layer_params(0), layer_params(1)]

    def cast(l, n):
        w = A[f'l{l}_{n}']
        return (w.T if n in TRANSPOSED else w).astype(BF16)

    stream = _WeightStream(cast)
    token = stream.start(0, IN_WEIGHT)

    big_names = [n for n in WEIGHT_NAMES if n[3:] in BIG_WEIGHTS + ('w_mod',)]
    rest = [n for n in WEIGHT_NAMES if n not in big_names]
    early = ['x', 'ctx'] + rest + ['m_' + n for n in rest] + ['v_' + n for n in rest]
    token, held = lax.optimization_barrier((token, [A[n] for n in early]))
    A.update(zip(early, held))
    x, ctx = A['x'][0], A['ctx'][0]
    wp, layw = _pack([(n, A[n]) for n in rest])
    mp, _ = _pack([(n, A['m_' + n]) for n in rest])
    vp, _ = _pack([(n, A['v_' + n]) for n in rest])
    rope = _rope_tables(S, Lc)
    h = jnp.concatenate([ctx, x], axis=0)

    sc_own = jax.nn.silu(A['c']) + token[0, 0]
    first, lay0 = _pack([('sc', sc_own), ('conv_w', A['l0_conv_w'])])
    first, h, wp, mp, vp = lax.optimization_barrier((first, h, wp, mp, vp))
    first_all = _exchange("gather_cond", [first], False)[0]
    sc_all = _unpack(first_all, lay0, 'sc')[:, 0]
    conv_w = _unpack(first_all, lay0, 'conv_w').transpose(1, 0, 2).reshape(3, N_DEV * cw)
    conv_w8 = _pad_rows(conv_w, 8)
    sc_ctx = jax.nn.silu(A['c_ctx'])
    s16 = _pad_rows(jnp.concatenate([sc_all, sc_ctx[None]], axis=0), 16)

    nmod = A['l0_w_mod'].shape[1]
    modp = jnp.concatenate([_mm_plain(f"l{l}_mod", s16, A[f'l{l}_w_mod'], False) for l in range(2)], axis=1)
    modp_all = _exchange("gather_mod", [modp], False)[0]
    mods = []
    for l in range(2):
        full = modp_all[:, :, l * nmod:(l + 1) * nmod].transpose(1, 0, 2).reshape(16, N_MOD * D) + A[f'l{l}_b_mod'][None]
        both = jnp.stack([full[8], lax.dynamic_index_in_dim(full, me, 0, keepdims=False)]).reshape(2, N_MOD, D)
        mods.append(jnp.concatenate([both, jnp.zeros((2, 8 - N_MOD, D), F32)], axis=1))

    saved, W = [], []
    for l in range(2):
        h, s, Wl = _layer_fwd(l, h, params[l], stream, mods[l], rope, conv_w8, cfg)
        saved.append(s)
        W.append(Wl)

    dh, loss_blk, dgf = _loss_head("loss_head", h, A['final_norm_g'].reshape(1, -1), A['loss_target'][0], cfg)
    loss = lax.psum(loss_blk[0, 0], ("x", "y", "c"))

    grads, small, dmods, scatters = {}, {'final_norm_g': dgf[0]}, [None, None], [None, None]
    token, res2_bwd = jnp.zeros((8, 128), F32), None
    for l in (1, 0):
        below = (saved[0]['F'], mods[0]) if l == 1 else None
        dh, dmods[l], sm, scatters[l], token, res2_bwd = _layer_bwd(
            l, dh, saved[l], params[l], W[l], mods[l] + token[0, 0], rope, conv_w8, cfg, res2_bwd, below)
        small.update({f'l{l}_{k}': v for k, v in sm.items()})
    grad_x = dh[None]

    def landed(l, key, after):
        group = FFN_WEIGHTS if key == 'ffn' else (key,)
        for n, parts in zip(group, _exchange_wait(f"scatter_{key}{l}_wait", scatters[l][key], after)):
            shape = A[f'l{l}_{n}'].shape
            grads[f'l{l}_{n}'] = (parts.reshape((N_DEV,) + (shape[::-1] if n in TRANSPOSED else shape)), None)

    out = {}

    def adam(n, dep=None):
        w, m, v = A[n], A['m_' + n], A['v_' + n]
        if n[3:] in TRANSPOSED:
            res = _adamw("adamw_" + n, grads[n][0], w.T, m.T, v.T, dep)
            out[n] = tuple(r.T for r in res)
        else:
            res = out[n] = _adamw("adamw_" + n, grads[n][0], w, m, v, dep)
        return res[1]

    small_names = [n for n in WEIGHT_NAMES if n in small]
    pieces = [(n, small[n]) for n in small_names]
    for l in range(2):
        pieces += [(f'dmod{l}', dmods[l][1, :N_MOD]), (f'dcmod{l}', dmods[l][0, :N_MOD])]
    second, lay1 = _pack(pieces)
    small_handle, small_token = _exchange_start("gather_small_start", [second], False, after=token)

    last, chain = dh, small_token
    for l in (1, 0):
        for key in ('ffn', 'w_out') + (('w_in',) if l == 1 else ()):
            landed(l, key, last)
            for n in (FFN_WEIGHTS if key == 'ffn' else (key,)):
                last = adam(f'l{l}_{n}', chain)
                chain = last[:8, :128]
    second_all = _exchange_wait("gather_small_wait", small_handle, last)[0]

    dsc_part = jnp.zeros((16, D), F32)
    for l in range(2):
        dm16 = _pad_rows(jnp.concatenate([_unpack(second_all, lay1, f'dmod{l}').reshape(N_DEV, N_MOD * D),
                                          jnp.sum(_unpack(second_all, lay1, f'dcmod{l}'), axis=0).reshape(1, N_MOD * D)], axis=0), 16)
        mine = lax.dynamic_slice_in_dim(dm16, me * nmod, nmod, axis=1)
        tk = _col_tile(D, 512)
        gw = _mm_tn(f"l{l}_dw_mod", s16, mine, pl.BlockSpec((16, tk), lambda kb, t: (0, kb)), pl.BlockSpec((16, nmod), lambda kb, t: (0, 0)),
                    jax.ShapeDtypeStruct((D, nmod), F32), pl.BlockSpec((tk, nmod), lambda kb, t: (kb, 0)), (tk, nmod), (D // tk, 1))
        grads[f'l{l}_w_mod'] = (gw[None], None)
        dsc_part += _mm_plain(f"l{l}_dsc", mine, A[f'l{l}_w_mod'], True)
        dmod_dev = _unpack(second_all, lay1, f'dmod{l}') + _unpack(second_all, lay1, f'dcmod{l}')
        grads[f'l{l}_b_mod'] = (dmod_dev.reshape(N_DEV, N_MOD * D), None)
    dsig = jax.nn.sigmoid(A['c_ctx'])
    dsilu = dsig * (1 + A['c_ctx'] * (1 - dsig))
    third_all = _exchange("gather_dsc", [dsc_part[8:9]], False)[0]
    grads['c_ctx'] = (third_all[:, 0] * dsilu[None], None)
    for n in small_names:
        g8 = _unpack(second_all, lay1, n)
        if n == 'l0_conv_w':
            g8 = lax.dynamic_slice_in_dim(g8, me * cw, cw, axis=2)
        grads[n] = (g8, None)

    adam('l0_w_mod')
    last = adam('l1_w_mod')
    gp, _ = _pack([(n, grads[n][0]) for n in rest], nlead=1)
    res = _adamw("adamw_small", gp, wp, mp, vp)
    for n in rest:
        out[n] = tuple(_unpack(r, layw, n) for r in res)
    landed(0, 'w_in', last)
    adam('l0_w_in')

    outs = [loss, grad_x]
    for k in range(4):
        outs += [out[n][k] for n in WEIGHT_NAMES]
    return tuple(outs)
```

```python
import math

import numpy as np
import jax
import jax.numpy as jnp
from jax import lax
from jax.experimental import pallas as pl
from jax.experimental.pallas import tpu as pltpu

F32 = jnp.float32
BF16 = jnp.bfloat16
HEAD_DIM = 128
AUX_WIDTH = 512
AUX_GROUPS = 4
POOL_HALF = (1, 2, 4, 8)
WINDOW = 128
GRID_W = 64
ROPE_THETA = 10000.0
EPS = 1e-6
NEG_INF = -1e30
ATT_SCALE = HEAD_DIM ** -0.5
LOG2_E = math.log2(math.e)
Q_SCALE = ATT_SCALE * LOG2_E
N_MOD = 6
N_DEV = 8
ROW_TILE = 256
BAND = ROW_TILE + 2 * WINDOW
ADAM_LR, ADAM_B1, ADAM_B2, ADAM_EPS, ADAM_WD, ADAM_STEP = 0.001, 0.9, 0.999, 1e-08, 0.01, 10
VMEM_LIMIT_MB = 56
ADAM_BLOCK_ELEMS = 3 << 17
MESH = pl.DeviceIdType.MESH
RESIDENT = pl.Buffered(buffer_count=1)

WEIGHT_NAMES = ['c_ctx', 'l0_norm1_g', 'l0_w_mod', 'l0_b_mod', 'l0_w_in', 'l0_q_norm_g', 'l0_k_norm_g', 'l0_conv_w', 'l0_w_out', 'l0_norm2_g', 'l0_w_gate', 'l0_w_up', 'l0_w_down', 'l1_norm1_g', 'l1_w_mod', 'l1_b_mod', 'l1_w_in', 'l1_q_norm_g', 'l1_k_norm_g', 'l1_sink', 'l1_pool_w', 'l1_pool_scale', 'l1_w_out', 'l1_norm2_g', 'l1_w_gate', 'l1_w_up', 'l1_w_down', 'final_norm_g']
INPUT_NAMES = (['x', 'c', 'ctx'] + WEIGHT_NAMES + ['loss_target'] + ['m_' + n for n in WEIGHT_NAMES]
               + ['v_' + n for n in WEIGHT_NAMES])
IN_WEIGHT = ('w_in',)
OUT_WEIGHT = ('w_out',)
MIXER_WEIGHTS = OUT_WEIGHT + IN_WEIGHT
FFN_WEIGHTS = ('w_down', 'w_gate', 'w_up')
TRANSPOSED = ('w_gate', 'w_up')
BIG_WEIGHTS = MIXER_WEIGHTS + FFN_WEIGHTS


def _params(vmem_mb=VMEM_LIMIT_MB):
    return pltpu.CompilerParams(vmem_limit_bytes=vmem_mb << 20)


def _row_tile(n, cap):
    best = None
    for t in range(16, min(n, cap) + 1, 16):
        if n % t == 0:
            best = t
    assert best is not None, (n, cap)
    return best


def _col_tile(n, cap):
    best = n
    for t in range(128, min(n, cap) + 1, 128):
        if n % t == 0:
            best = t
    return best


def _dot(a, b, ca, cb):
    return lax.dot_general(a, b, (((ca,), (cb,)), ((), ())), preferred_element_type=F32)


def _nn(a, b):
    return _dot(a, b, 1, 0)


def _nt(a, b):
    return _dot(a, b, 1, 1)


def _tn(a, b):
    return _dot(a, b, 0, 0)


def _bf(x):
    return x.astype(BF16)


def _exchange(name, arrs, scatter, after=None):
    n = len(arrs)
    extra = [] if after is None else [after]
    if scatter:
        out_shape = [jax.ShapeDtypeStruct(a.shape, a.dtype) for a in arrs]
    else:
        out_shape = [jax.ShapeDtypeStruct((N_DEV,) + a.shape, a.dtype) for a in arrs]

    def body(*refs):
        ins, outs = refs[:n], refs[n + len(extra):2 * n + len(extra)]
        send_sems, recv_sems, local_sems = refs[2 * n + len(extra):]
        x, y, c = lax.axis_index("x"), lax.axis_index("y"), lax.axis_index("c")
        me = 4 * x + 2 * y + c
        local, remote = [], []
        for a in range(n):
            own = ins[a].at[me] if scatter else ins[a]
            cp = pltpu.make_async_copy(own, outs[a].at[me], local_sems.at[a])
            cp.start()
            local.append(cp)
            for r in range(1, N_DEV):
                px = 1 - x if r & 4 else x
                py = 1 - y if r & 2 else y
                pc = 1 - c if r & 1 else c
                src = ins[a].at[4 * px + 2 * py + pc] if scatter else ins[a]
                cp = pltpu.make_async_remote_copy(
                    src_ref=src, dst_ref=outs[a].at[me], send_sem=send_sems.at[a, r - 1],
                    recv_sem=recv_sems.at[a, r - 1], device_id=(px, py, pc), device_id_type=MESH)
                cp.start()
                remote.append(cp)
        for cp in remote:
            cp.wait()
        for cp in local:
            cp.wait()

    any_spec = pl.BlockSpec(memory_space=pl.ANY)
    return pl.pallas_call(
        body, name=name, out_shape=out_shape,
        in_specs=[any_spec] * (n + len(extra)), out_specs=[any_spec] * n,
        scratch_shapes=[pltpu.SemaphoreType.DMA((n, N_DEV - 1)), pltpu.SemaphoreType.DMA((n, N_DEV - 1)),
                        pltpu.SemaphoreType.DMA((n,))],
    )(*arrs, *extra)


HBM_SPEC = pl.BlockSpec(memory_space=pltpu.HBM)
SEM_SPEC = pl.BlockSpec(memory_space=pltpu.SEMAPHORE)
EFFECT = pltpu.SideEffectType.DATAFLOW_SIDE_EFFECTING


def _split_copies(srcs, lands, send_sems, recv_sems, local_sems, scatter):
    x, y, c = lax.axis_index("x"), lax.axis_index("y"), lax.axis_index("c")
    me = 4 * x + 2 * y + c
    local, remote = [], []
    for a in range(len(srcs)):
        own = srcs[a].at[me] if scatter else srcs[a]
        local.append(pltpu.make_async_copy(own, lands[a].at[me], local_sems.at[a]))
        for r in range(1, N_DEV):
            px = 1 - x if r & 4 else x
            py = 1 - y if r & 2 else y
            pc = 1 - c if r & 1 else c
            src = srcs[a].at[4 * px + 2 * py + pc] if scatter else srcs[a]
            remote.append(pltpu.make_async_remote_copy(
                src_ref=src, dst_ref=lands[a].at[me], send_sem=send_sems.at[a * (N_DEV - 1) + r - 1],
                recv_sem=recv_sems.at[a * (N_DEV - 1) + r - 1], device_id=(px, py, pc), device_id_type=MESH))
    return local, remote


def _exchange_start(name, arrs, scatter, after=None):
    n = len(arrs)
    extra = [] if after is None else [after]
    shapes = [a.shape if scatter else (N_DEV,) + a.shape for a in arrs]
    lands = [pltpu.with_memory_space_constraint(lax.empty(s, a.dtype), pltpu.HBM) for s, a in zip(shapes, arrs)]
    srcs = [pltpu.with_memory_space_constraint(a, pltpu.HBM) for a in arrs]

    def body(*refs):
        src_refs, land_refs = refs[:n], refs[n:2 * n]
        send_sems, recv_sems, local_sems = refs[2 * n + len(extra):2 * n + len(extra) + 3]
        token = refs[-1]
        local, remote = _split_copies(src_refs, land_refs, send_sems, recv_sems, local_sems, scatter)
        for cp in local + remote:
            cp.start()
        token[...] = jnp.zeros_like(token)

    res = pl.pallas_call(
        body, name=name,
        out_shape=[pltpu.SemaphoreType.DMA((n * (N_DEV - 1),)), pltpu.SemaphoreType.DMA((n * (N_DEV - 1),)), pltpu.SemaphoreType.DMA((n,))]
        + [pltpu.HBM(a.shape, a.dtype) for a in arrs] + [pltpu.HBM(s, a.dtype) for s, a in zip(shapes, arrs)]
        + [jax.ShapeDtypeStruct((8, 128), F32)],
        in_specs=[HBM_SPEC] * (2 * n) + [pl.BlockSpec(memory_space=pl.ANY)] * len(extra),
        out_specs=[SEM_SPEC] * 3 + [HBM_SPEC] * (2 * n) + [pl.BlockSpec(memory_space=pltpu.VMEM)],
        input_output_aliases={i: 3 + i for i in range(2 * n)},
        compiler_params=pltpu.CompilerParams(has_side_effects=EFFECT),
    )(*srcs, *lands, *extra)
    return (scatter, res[:3], res[3:3 + n], res[3 + n:3 + 2 * n]), res[-1]


def _exchange_wait(name, handle, after):
    scatter, sems, srcs, lands = handle
    n = len(srcs)

    def body(*refs):
        src_refs, land_refs = refs[:n], refs[n:2 * n]
        send_sems, recv_sems, local_sems = refs[2 * n:2 * n + 3]
        local, remote = _split_copies(src_refs, land_refs, send_sems, recv_sems, local_sems, scatter)
        for cp in remote:
            cp.wait_send()
            cp.wait_recv()
        for cp in local:
            cp.wait()

    res = pl.pallas_call(
        body, name=name,
        out_shape=[pltpu.HBM(a.shape, a.dtype) for a in srcs] + [pltpu.HBM(a.shape, a.dtype) for a in lands],
        in_specs=[HBM_SPEC] * (2 * n) + [SEM_SPEC] * 3 + [pl.BlockSpec(memory_space=pl.ANY)], out_specs=[HBM_SPEC] * (2 * n),
        input_output_aliases={i: i for i in range(2 * n)},
        compiler_params=pltpu.CompilerParams(has_side_effects=EFFECT),
    )(*srcs, *lands, *sems, after)
    return list(res[n:])


FIRST_COPIES = 4
RELAY_COPIES = 3


def _gather_copies(srcs, lands, sems):
    send_sems, recv_sems, local_sems = sems[:3]
    x, y, c = lax.axis_index("x"), lax.axis_index("y"), lax.axis_index("c")
    me = 4 * x + 2 * y + c
    chips = [(1 - x, y), (x, 1 - y), (1 - x, 1 - y)]
    local, first, relay = [], [], []
    for a in range(len(srcs)):
        local.append(pltpu.make_async_copy(srcs[a], lands[a].at[me], local_sems.at[a]))
        targets = [(x, y, 1 - c)] + [(px, py, c) for px, py in chips]
        first.append([pltpu.make_async_remote_copy(
            src_ref=srcs[a], dst_ref=lands[a].at[me], send_sem=send_sems.at[FIRST_COPIES * a + k],
            recv_sem=recv_sems.at[FIRST_COPIES * a + k], device_id=t, device_id_type=MESH) for k, t in enumerate(targets)])
        if len(sems) > 3:
            rsend, rrecv = sems[3:]
            slots = [lands[a].at[4 * px + 2 * py + c] for px, py in chips]
            relay.append([pltpu.make_async_remote_copy(
                src_ref=slot, dst_ref=slot, send_sem=rsend.at[RELAY_COPIES * a + j], recv_sem=rrecv.at[RELAY_COPIES * a + j],
                device_id=(x, y, 1 - c), device_id_type=MESH) for j, slot in enumerate(slots)])
    return local, first, relay


def _gather_start(name, arrs, after=None):
    n = len(arrs)
    extra = [] if after is None else [after]
    lands = [pltpu.with_memory_space_constraint(lax.empty((N_DEV,) + a.shape, a.dtype), pltpu.HBM) for a in arrs]
    srcs = [pltpu.with_memory_space_constraint(a, pltpu.HBM) for a in arrs]

    def body(*refs):
        at = 2 * n + len(extra)
        local, first, _ = _gather_copies(refs[:n], refs[n:2 * n], refs[at:at + 3])
        for cp in local + [cp for cps in first for cp in cps]:
            cp.start()
        refs[-1][...] = jnp.zeros_like(refs[-1])

    res = pl.pallas_call(
        body, name=name,
        out_shape=[pltpu.SemaphoreType.DMA((FIRST_COPIES * n,)), pltpu.SemaphoreType.DMA((FIRST_COPIES * n,)), pltpu.SemaphoreType.DMA((n,))]
        + [pltpu.HBM(a.shape, a.dtype) for a in arrs] + [pltpu.HBM((N_DEV,) + a.shape, a.dtype) for a in arrs]
        + [jax.ShapeDtypeStruct((8, 128), F32)],
        in_specs=[HBM_SPEC] * (2 * n) + [pl.BlockSpec(memory_space=pl.ANY)] * len(extra),
        out_specs=[SEM_SPEC] * 3 + [HBM_SPEC] * (2 * n) + [pl.BlockSpec(memory_space=pltpu.VMEM)],
        input_output_aliases={i: 3 + i for i in range(2 * n)},
        compiler_params=pltpu.CompilerParams(has_side_effects=EFFECT),
    )(*srcs, *lands, *extra)
    return (list(res[:3]), list(res[3:3 + n]), list(res[3 + n:3 + 2 * n])), res[-1]


def _gather_relay(name, handle, after):
    sems, srcs, lands = handle
    n = len(srcs)

    def body(*refs):
        in_sems = refs[2 * n:2 * n + 3]
        out_sems = refs[2 * n + 4 + 2 * n:2 * n + 4 + 2 * n + 2]
        _, first, relay = _gather_copies(refs[:n], refs[n:2 * n], list(in_sems) + list(out_sems))
        for a in range(n):
            for j in range(RELAY_COPIES):
                first[a][1 + j].wait_recv()
                relay[a][j].start()
        refs[-1][...] = jnp.zeros_like(refs[-1])

    res = pl.pallas_call(
        body, name=name,
        out_shape=[pltpu.HBM(a.shape, a.dtype) for a in srcs] + [pltpu.HBM(a.shape, a.dtype) for a in lands]
        + [pltpu.SemaphoreType.DMA((RELAY_COPIES * n,)), pltpu.SemaphoreType.DMA((RELAY_COPIES * n,)), jax.ShapeDtypeStruct((8, 128), F32)],
        in_specs=[HBM_SPEC] * (2 * n) + [SEM_SPEC] * 3 + [pl.BlockSpec(memory_space=pl.ANY)],
        out_specs=[HBM_SPEC] * (2 * n) + [SEM_SPEC] * 2 + [pl.BlockSpec(memory_space=pltpu.VMEM)],
        input_output_aliases={i: i for i in range(2 * n)},
        compiler_params=pltpu.CompilerParams(has_side_effects=EFFECT),
    )(*srcs, *lands, *sems, after)
    return (sems + list(res[2 * n:2 * n + 2]), list(res[:n]), list(res[n:2 * n])), res[-1]


def _gather_wait(name, handle, after):
    sems, srcs, lands = handle
    n = len(srcs)

    def body(*refs):
        local, first, relay = _gather_copies(refs[:n], refs[n:2 * n], refs[2 * n:2 * n + 5])
        for a in range(n):
            for cp in first[a]:
                cp.wait_send()
            first[a][0].wait_recv()
            for cp in relay[a]:
                cp.wait_send()
                cp.wait_recv()
            local[a].wait()

    res = pl.pallas_call(
        body, name=name,
        out_shape=[pltpu.HBM(a.shape, a.dtype) for a in srcs] + [pltpu.HBM(a.shape, a.dtype) for a in lands],
        in_specs=[HBM_SPEC] * (2 * n) + [SEM_SPEC] * 5 + [pl.BlockSpec(memory_space=pl.ANY)], out_specs=[HBM_SPEC] * (2 * n),
        input_output_aliases={i: i for i in range(2 * n)},
        compiler_params=pltpu.CompilerParams(has_side_effects=EFFECT),
    )(*srcs, *lands, *sems, after)
    return list(res[n:])


def _dep(dep, grid_rank):
    if dep is None:
        return [], []
    return [dep], [pl.BlockSpec((8, 128), (lambda i, j: (0, 0)) if grid_rank == 2 else (lambda i: (0, 0)))]


def _mm_step(name, fn, ins, in_specs, out_shape, out_spec, grid, dep=None, res=None):
    n = len(ins)
    dep_ins, dep_specs = _dep(dep, len(grid))
    res_ins, res_specs, out_shapes, out_specs = [], [], out_shape, out_spec
    if res is not None:
        h, mod, row_idx, lc = res
        tm, tn = out_spec.block_shape
        res_ins = [h, mod]
        res_specs = [pl.BlockSpec((tm, tn), lambda j, i: (i, j)), pl.BlockSpec((2, 8, tn), lambda j, i: (0, 0, j))]
        out_shapes, out_specs = [out_shape, jax.ShapeDtypeStruct(h.shape, h.dtype)], [out_spec, res_specs[0]]

    def body(*refs):
        outs = refs[n + len(res_ins) + len(dep_ins):]
        acc = fn(*refs[:n])
        outs[0][...] = acc.astype(outs[0].dtype)
        if res is not None:
            h_ref, mod_ref = refs[n:n + 2]
            row = pl.program_id(1) * tm + lax.broadcasted_iota(jnp.int32, (tm, 1), 0)
            gate = jnp.where(row < lc, mod_ref[0, row_idx:row_idx + 1, :], mod_ref[1, row_idx:row_idx + 1, :])
            outs[1][...] = h_ref[...] + gate * acc

    return pl.pallas_call(body, name=name, grid=grid, in_specs=list(in_specs) + res_specs + dep_specs, out_specs=out_specs,
                          out_shape=out_shapes, compiler_params=_params())(*ins, *res_ins, *dep_ins)


def _mm_tn(name, a, b, a_spec, b_spec, out_shape, out_spec, acc_shape, grid):
    nk = grid[-1]
    kax = len(grid) - 1
    if nk == 1:
        def whole(a_ref, b_ref, o_ref):
            o_ref[...] = _tn(_bf(a_ref[...]), _bf(b_ref[...])).astype(o_ref.dtype)

        return pl.pallas_call(whole, name=name, grid=grid, in_specs=[a_spec, b_spec], out_specs=out_spec,
                              out_shape=out_shape, compiler_params=_params())(a, b)

    def body(a_ref, b_ref, o_ref, acc_ref):
        k = pl.program_id(kax)

        @pl.when(k == 0)
        def _():
            acc_ref[...] = jnp.zeros_like(acc_ref)

        acc_ref[...] += _tn(_bf(a_ref[...]), _bf(b_ref[...]))

        @pl.when(k == nk - 1)
        def _():
            o_ref[...] = acc_ref[...].astype(o_ref.dtype)

    return pl.pallas_call(body, name=name, grid=grid, in_specs=[a_spec, b_spec], out_specs=out_spec,
                          out_shape=out_shape, scratch_shapes=[pltpu.VMEM(acc_shape, F32)],
                          compiler_params=_params())(a, b)


def _shards_per_dot(J, n):
    return 2 if n % 256 and J % 2 == 0 else 1


def _mm_cols(name, a, w3, out_dtype=F32, dep=None):
    M, K = a.shape
    J, _, n = w3.shape
    g = _shards_per_dot(J, n)
    tm = M if g == 1 else _row_tile(M, M // g)

    def fn(a_ref, w_ref):
        w = w_ref[0] if g == 1 else jnp.concatenate([w_ref[s] for s in range(g)], axis=1)
        x = a_ref[...] if tm == M else a_ref[pl.ds(pl.multiple_of(pl.program_id(1) * tm, 16), tm), :]
        return _nn(_bf(x), w)

    return _mm_step(
        name, fn, [a, w3],
        [pl.BlockSpec((M, K), lambda j, i: (0, 0), pipeline_mode=RESIDENT), pl.BlockSpec((g, K, n), lambda j, i: (j, 0, 0))],
        jax.ShapeDtypeStruct((M, J * n), out_dtype), pl.BlockSpec((tm, g * n), lambda j, i: (i, j)), (J // g, M // tm), dep)


def _mm_plain(name, a, b, transpose_b, out_dtype=F32, tn=512, dep=None, res=None):
    M, K = a.shape
    N = b.shape[0] if transpose_b else b.shape[1]
    tn = _col_tile(N, tn)
    tm = M if res is None else _row_tile(M, 1088)
    a_spec = pl.BlockSpec((M, K), lambda j, i: (0, 0), pipeline_mode=RESIDENT)

    def rows(a_ref):
        return a_ref[...] if tm == M else a_ref[pl.ds(pl.multiple_of(pl.program_id(1) * tm, 16), tm), :]

    if transpose_b:
        b_spec = pl.BlockSpec((tn, K), lambda j, i: (j, 0))
        fn = lambda a_ref, b_ref: _nt(_bf(rows(a_ref)), _bf(b_ref[...]))
    else:
        b_spec = pl.BlockSpec((K, tn), lambda j, i: (0, j))
        fn = lambda a_ref, b_ref: _nn(_bf(rows(a_ref)), _bf(b_ref[...]))
    return _mm_step(name, fn, [a, b], [a_spec, b_spec],
                    jax.ShapeDtypeStruct((M, N), out_dtype), pl.BlockSpec((tm, tn), lambda j, i: (i, j)),
                    (N // tn, M // tm), dep, res)


def _mm_shards_nn(name, a3, w3, tn=512, dep=None, res=None):
    J, M, k = a3.shape
    N = w3.shape[2]
    tm = _row_tile(M, 544)
    tn = _col_tile(N, tn)

    def fn(a_ref, w_ref):
        acc = _nn(a_ref[0], w_ref[0])
        for j in range(1, J):
            acc += _nn(a_ref[j], w_ref[j])
        return acc

    return _mm_step(name, fn, [a3, w3],
                    [pl.BlockSpec((J, tm, k), lambda jn, i: (0, i, 0)), pl.BlockSpec((J, k, tn), lambda jn, i: (0, 0, jn))],
                    jax.ShapeDtypeStruct((M, N), F32), pl.BlockSpec((tm, tn), lambda jn, i: (i, jn)), (N // tn, M // tm), dep, res)


def _mm_shards_nn2(name, a3, w3a, b3, w3b, tn=512):
    J, M, k = a3.shape
    N = w3a.shape[2]
    tm = _row_tile(M, 544)
    tn = _col_tile(N, tn)

    def fn(a_ref, wa_ref, b_ref, wb_ref):
        acc = _nn(a_ref[0], wa_ref[0]) + _nn(b_ref[0], wb_ref[0])
        for j in range(1, J):
            acc += _nn(a_ref[j], wa_ref[j]) + _nn(b_ref[j], wb_ref[j])
        return acc

    act = pl.BlockSpec((J, tm, k), lambda jn, i: (0, i, 0))
    wsp = pl.BlockSpec((J, k, tn), lambda jn, i: (0, 0, jn))
    return _mm_step(name, fn, [a3, w3a, b3, w3b], [act, wsp, act, wsp],
                    jax.ShapeDtypeStruct((M, N), F32), pl.BlockSpec((tm, tn), lambda jn, i: (i, jn)), (N // tn, M // tm))


def _mm_cols_nt(name, a, w3, tn=512):
    M = a.shape[0]
    J, N, n = w3.shape
    tm = _row_tile(M, 544)
    tn = _col_tile(N, tn)
    g = _shards_per_dot(J, n)

    def fn(a_ref, w_ref):
        acc = None
        for j in range(0, J, g):
            w = w_ref[j] if g == 1 else jnp.concatenate([w_ref[j + s] for s in range(g)], axis=1)
            part = _nt(a_ref[:, j * n:(j + g) * n], w)
            acc = part if acc is None else acc + part
        return acc

    return _mm_step(name, fn, [a, w3],
                    [pl.BlockSpec((tm, J * n), lambda jn, i: (i, 0)), pl.BlockSpec((J, tn, n), lambda jn, i: (0, jn, 0))],
                    jax.ShapeDtypeStruct((M, N), F32), pl.BlockSpec((tm, tn), lambda jn, i: (i, jn)), (N // tn, M // tm))


def _wgrad_cols(name, a, b, J):
    T, K = a.shape
    n = b.shape[1] // J
    return _mm_tn(name, a, b, pl.BlockSpec((T, K), lambda j, t: (0, 0), pipeline_mode=RESIDENT), pl.BlockSpec((T, n), lambda j, t: (0, j)),
                  jax.ShapeDtypeStruct((J, K, n), BF16), pl.BlockSpec((None, K, n), lambda j, t: (j, 0, 0)), (K, n), (J, 1))


def _wgrad_rows(name, a, b, tk=512):
    T, K = a.shape
    N = b.shape[1]
    tk = _col_tile(K, tk)
    return _mm_tn(name, a, b, pl.BlockSpec((T, tk), lambda kb, t: (0, kb)), pl.BlockSpec((T, N), lambda kb, t: (0, 0), pipeline_mode=RESIDENT),
                  jax.ShapeDtypeStruct((K, N), BF16), pl.BlockSpec((tk, N), lambda kb, t: (kb, 0)), (tk, N), (K // tk, 1))


def _wgrad_down(name, a3, b):
    J, T, k = a3.shape
    N = b.shape[1]
    return _mm_tn(name, a3, b, pl.BlockSpec((None, T, k), lambda j, t: (j, 0, 0)),
                  pl.BlockSpec((T, N), lambda j, t: (0, 0), pipeline_mode=RESIDENT),
                  jax.ShapeDtypeStruct((J, k, N), BF16), pl.BlockSpec((None, k, N), lambda j, t: (j, 0, 0)), (k, N), (J, 1))


def _seg(i):
    return jnp.minimum(i, 1)


def _rstd(x):
    return lax.rsqrt(jnp.mean(x * x, axis=-1, keepdims=True) + EPS)


def _norm_mod(name, h, g, mod, which):
    T, D = h.shape

    def body(h_ref, g_ref, mod_ref, o_ref):
        x = h_ref[...]
        n = x * _rstd(x) * g_ref[...]
        shift = mod_ref[3 * which:3 * which + 1, :]
        scale = mod_ref[3 * which + 1:3 * which + 2, :]
        o_ref[...] = (n * (1 + scale) + shift).astype(o_ref.dtype)

    row = pl.BlockSpec((ROW_TILE, D), lambda i: (i, 0))
    return pl.pallas_call(
        body, name=name, grid=(T // ROW_TILE,),
        in_specs=[row, pl.BlockSpec((1, D), lambda i: (0, 0)), pl.BlockSpec((None, 8, D), lambda i: (_seg(i), 0, 0))],
        out_specs=row, out_shape=jax.ShapeDtypeStruct((T, D), BF16), compiler_params=_params())(h, g, mod)


def _norm_mod_bwd(name, dxn, h, g, mod, which, dres, latent_only=False, gate=None):
    T, D = h.shape
    n_gate = 0 if gate is None else 2

    def body(dxn_ref, h_ref, g_ref, mod_ref, dres_ref, *rest):
        dh_ref, dmod_ref, dg_ref = rest[n_gate:n_gate + 3]
        i = pl.program_id(0)
        x = h_ref[...]
        r = _rstd(x)
        xhat = x * r
        g = g_ref[...]
        n = xhat * g
        scale = mod_ref[3 * which + 1:3 * which + 2, :]
        dxn = dxn_ref[...]
        dn = dxn * (1 + scale)
        dxh = dn * g
        dh = dres_ref[...] + r * (dxh - xhat * jnp.mean(dxh * xhat, axis=-1, keepdims=True))
        if latent_only:
            @pl.when(i > 0)
            def _():
                dh_ref[...] = dh
        else:
            dh_ref[...] = dh

        @pl.when(i <= 1)
        def _():
            dmod_ref[...] = jnp.zeros_like(dmod_ref)

        @pl.when(i == 0)
        def _():
            dg_ref[...] = jnp.zeros_like(dg_ref)

        dmod_ref[3 * which:3 * which + 1, :] += jnp.sum(dxn, axis=0, keepdims=True)
        dmod_ref[3 * which + 1:3 * which + 2, :] += jnp.sum(dxn * n, axis=0, keepdims=True)
        dg_ref[0:1, :] += jnp.sum(dn * xhat, axis=0, keepdims=True)

        if gate is not None:
            y_ref, gmod_ref = rest[:2]
            dy_ref, dgmod_ref = rest[5:7]
            dy_ref[...] = (dh * gmod_ref[gate[2]:gate[2] + 1, :]).astype(dy_ref.dtype)

            @pl.when(i <= 1)
            def _():
                dgmod_ref[...] = jnp.zeros_like(dgmod_ref)

            dgmod_ref[gate[2]:gate[2] + 1, :] += jnp.sum(dh * y_ref[...], axis=0, keepdims=True)

    row = pl.BlockSpec((ROW_TILE, D), lambda i: (i, 0))
    modspec = pl.BlockSpec((None, 8, D), lambda i: (_seg(i), 0, 0))
    dh_rows = T - ROW_TILE if latent_only else T
    dh_spec = pl.BlockSpec((ROW_TILE, D), lambda i: (jnp.maximum(i - 1, 0), 0)) if latent_only else row
    gate_ins, gate_specs, gate_outs, gate_shapes = [], [], [], []
    if gate is not None:
        gate_ins, gate_specs = [gate[0], gate[1]], [row, modspec]
        gate_outs, gate_shapes = [row, modspec], [jax.ShapeDtypeStruct((T, D), BF16), jax.ShapeDtypeStruct((2, 8, D), F32)]
    return pl.pallas_call(
        body, name=name, grid=(T // ROW_TILE,),
        in_specs=[row, row, pl.BlockSpec((1, D), lambda i: (0, 0)), modspec, row] + gate_specs,
        out_specs=[dh_spec, modspec, pl.BlockSpec((8, D), lambda i: (0, 0))] + gate_outs,
        out_shape=[jax.ShapeDtypeStruct((dh_rows, D), F32), jax.ShapeDtypeStruct((2, 8, D), F32), jax.ShapeDtypeStruct((8, D), F32)] + gate_shapes,
        compiler_params=_params())(dxn, h, g, mod, dres, *gate_ins)


def _gate_bwd(name, dh, y, mod, row_idx):
    T, D = dh.shape

    def body(dh_ref, y_ref, mod_ref, dy_ref, dmod_ref):
        i = pl.program_id(0)
        dh = dh_ref[...]
        dy_ref[...] = (dh * mod_ref[row_idx:row_idx + 1, :]).astype(dy_ref.dtype)

        @pl.when(i <= 1)
        def _():
            dmod_ref[...] = jnp.zeros_like(dmod_ref)

        dmod_ref[row_idx:row_idx + 1, :] += jnp.sum(dh * y_ref[...], axis=0, keepdims=True)

    row = pl.BlockSpec((ROW_TILE, D), lambda i: (i, 0))
    modspec = pl.BlockSpec((None, 8, D), lambda i: (_seg(i), 0, 0))
    return pl.pallas_call(
        body, name=name, grid=(T // ROW_TILE,), in_specs=[row, row, modspec], out_specs=[row, modspec],
        out_shape=[jax.ShapeDtypeStruct((T, D), BF16), jax.ShapeDtypeStruct((2, 8, D), F32)],
        compiler_params=_params())(dh, y, mod)


def _rot(y):
    lane = lax.broadcasted_iota(jnp.int32, y.shape, 1)
    return jnp.where((lane & 32) == 0, pltpu.roll(y, 96, 1), pltpu.roll(y, 32, 1))


def _qk_prep(name, P, q_g, k_g, rope_c, rope_s, cfg):
    T = P.shape[0]
    ATT, KVW = cfg['ATT'], cfg['KVW']

    def body(q_ref, k_ref, v_ref, qg_ref, kg_ref, c_ref, s_ref, qo_ref, ko_ref, vo_ref):
        cc, ss = c_ref[...], s_ref[...]

        def head(x, g):
            y = x * _rstd(x) * g
            return y * cc + _rot(y) * ss

        for hh in range(ATT // HEAD_DIM):
            sl = slice(hh * HEAD_DIM, (hh + 1) * HEAD_DIM)
            qo_ref[:, sl] = (head(q_ref[:, sl], qg_ref[...]) * Q_SCALE).astype(qo_ref.dtype)
        for hh in range(KVW // HEAD_DIM):
            sl = slice(hh * HEAD_DIM, (hh + 1) * HEAD_DIM)
            ko_ref[:, sl] = head(k_ref[:, sl], kg_ref[...]).astype(ko_ref.dtype)
        vo_ref[...] = v_ref[...].astype(vo_ref.dtype)

    kb = ATT // KVW
    gain = pl.BlockSpec((1, HEAD_DIM), lambda i: (0, 0))
    tab = pl.BlockSpec((ROW_TILE, HEAD_DIM), lambda i: (i, 0))
    qs = pl.BlockSpec((ROW_TILE, ATT), lambda i: (i, 0))
    ks = pl.BlockSpec((ROW_TILE, KVW), lambda i: (i, 0))
    return pl.pallas_call(
        body, name=name, grid=(T // ROW_TILE,),
        in_specs=[qs, pl.BlockSpec((ROW_TILE, KVW), lambda i: (i, kb)), pl.BlockSpec((ROW_TILE, KVW), lambda i: (i, kb + 1)),
                  gain, gain, tab, tab],
        out_specs=[qs, ks, ks],
        out_shape=[jax.ShapeDtypeStruct((T, ATT), BF16), jax.ShapeDtypeStruct((T, KVW), BF16), jax.ShapeDtypeStruct((T, KVW), BF16)],
        compiler_params=_params())(P, P, P, q_g, k_g, rope_c, rope_s)


def _qk_prep_bwd(name, dqr, dkr, P, q_g, k_g, rope_c, rope_s, cfg):
    T = P.shape[0]
    ATT, KVW = cfg['ATT'], cfg['KVW']

    def body(dq_ref, dk_ref, q_ref, k_ref, qg_ref, kg_ref, c_ref, s_ref, dqo_ref, dko_ref, dqg_ref, dkg_ref):
        i = pl.program_id(0)
        cc, ss = c_ref[...], s_ref[...]

        @pl.when(i == 0)
        def _():
            dqg_ref[...] = jnp.zeros_like(dqg_ref)
            dkg_ref[...] = jnp.zeros_like(dkg_ref)

        def head(x, g, dout):
            dy = dout * cc + _rot(dout * ss)
            r = _rstd(x)
            xhat = x * r
            dxh = dy * g
            dx = r * (dxh - xhat * jnp.mean(dxh * xhat, axis=-1, keepdims=True))
            return dx, jnp.sum(dy * xhat, axis=0, keepdims=True)

        dg = jnp.zeros((1, HEAD_DIM), F32)
        for hh in range(ATT // HEAD_DIM):
            sl = slice(hh * HEAD_DIM, (hh + 1) * HEAD_DIM)
            dx, d = head(q_ref[:, sl], qg_ref[...], dq_ref[:, sl] * ATT_SCALE)
            dqo_ref[:, sl] = dx.astype(dqo_ref.dtype)
            dg += d
        dqg_ref[0:1, :] += dg
        dg = jnp.zeros((1, HEAD_DIM), F32)
        for hh in range(KVW // HEAD_DIM):
            sl = slice(hh * HEAD_DIM, (hh + 1) * HEAD_DIM)
            dx, d = head(k_ref[:, sl], kg_ref[...], dk_ref[:, sl] * (1.0 / LOG2_E))
            dko_ref[:, sl] = dx.astype(dko_ref.dtype)
            dg += d
        dkg_ref[0:1, :] += dg

    kb = ATT // KVW
    gain = pl.BlockSpec((1, HEAD_DIM), lambda i: (0, 0))
    dgain = pl.BlockSpec((8, HEAD_DIM), lambda i: (0, 0))
    tab = pl.BlockSpec((ROW_TILE, HEAD_DIM), lambda i: (i, 0))
    qs = pl.BlockSpec((ROW_TILE, ATT), lambda i: (i, 0))
    ks = pl.BlockSpec((ROW_TILE, KVW), lambda i: (i, 0))
    return pl.pallas_call(
        body, name=name, grid=(T // ROW_TILE,),
        in_specs=[qs, ks, qs, pl.BlockSpec((ROW_TILE, KVW), lambda i: (i, kb)), gain, gain, tab, tab],
        out_specs=[qs, ks, dgain, dgain],
        out_shape=[jax.ShapeDtypeStruct((T, ATT), BF16), jax.ShapeDtypeStruct((T, KVW), BF16),
                   jax.ShapeDtypeStruct((8, HEAD_DIM), F32), jax.ShapeDtypeStruct((8, HEAD_DIM), F32)],
        compiler_params=_params())(dqr, dkr, P, P, q_g, k_g, rope_c, rope_s)


def _mix_shape(q):
    return jax.ShapeDtypeStruct((q.shape[0], q.shape[1] + AUX_WIDTH), BF16)


def _att_specs(T, G):
    qs = pl.BlockSpec((ROW_TILE, G * HEAD_DIM), lambda h, i: (i, h))
    kvs = pl.BlockSpec((T, HEAD_DIM), lambda h, i: (0, h))
    return qs, kvs


def _attn_dense_fwd(name, q, k, v, cfg):
    T, G, Lc = q.shape[0], cfg['G'], cfg['Lc']

    def body(q_ref, k_ref, v_ref, o_ref, lse_ref, mix_ref):
        def attend(rows):
            kk, vv = k_ref[0:rows, :], v_ref[0:rows, :]
            for g in range(G):
                sl = slice(g * HEAD_DIM, (g + 1) * HEAD_DIM)
                s = _nt(q_ref[:, sl], kk)
                m = jnp.max(s, axis=1, keepdims=True)
                p = jnp.exp2(s - m)
                l = jnp.sum(p, axis=1, keepdims=True)
                o = _nn(_bf(p), vv) / l
                o_ref[:, sl] = o
                mix_ref[:, sl] = _bf(o)
                lse_ref[:, sl] = jnp.broadcast_to(m + jnp.log2(l), (ROW_TILE, HEAD_DIM))

        @pl.when(pl.program_id(1) == 0)
        def _():
            attend(Lc)

        @pl.when(pl.program_id(1) > 0)
        def _():
            attend(T)

    qs, kvs = _att_specs(T, G)
    return pl.pallas_call(
        body, name=name, grid=(cfg['NKV'], T // ROW_TILE), in_specs=[qs, kvs, kvs], out_specs=[qs, qs, qs],
        out_shape=[jax.ShapeDtypeStruct(q.shape, F32), jax.ShapeDtypeStruct(q.shape, F32), _mix_shape(q)],
        compiler_params=_params())(q, k, v)


def _attn_dense_bwd(name, q, k, v, o, lse, dmix, cfg):
    T, G, Lc = q.shape[0], cfg['G'], cfg['Lc']

    def body(q_ref, k_ref, v_ref, o_ref, lse_ref, do_ref, dq_ref, dk_ref, dv_ref):
        i = pl.program_id(1)

        @pl.when(i == 0)
        def _():
            dk_ref[...] = jnp.zeros_like(dk_ref)
            dv_ref[...] = jnp.zeros_like(dv_ref)

        def attend(rows):
            kk, vv = k_ref[0:rows, :], v_ref[0:rows, :]
            for g in range(G):
                sl = slice(g * HEAD_DIM, (g + 1) * HEAD_DIM)
                qg, do = q_ref[:, sl], do_ref[:, sl]
                delta = jnp.sum(do * o_ref[:, sl], axis=1, keepdims=True)
                p = jnp.exp2(_nt(qg, kk) - lse_ref[:, g * HEAD_DIM:g * HEAD_DIM + 1])
                dob = _bf(do)
                dv_ref[0:rows, :] += _tn(_bf(p), dob)
                ds = _bf(p * (_nt(dob, vv) - delta))
                dq_ref[:, sl] = _nn(ds, kk)
                dk_ref[0:rows, :] += _tn(ds, qg)

        @pl.when(i == 0)
        def _():
            attend(Lc)

        @pl.when(i > 0)
        def _():
            attend(T)

    qs, kvs = _att_specs(T, G)
    return pl.pallas_call(
        body, name=name, grid=(cfg['NKV'], T // ROW_TILE), in_specs=[qs, kvs, kvs, qs, qs, qs], out_specs=[qs, kvs, kvs],
        out_shape=[jax.ShapeDtypeStruct(q.shape, F32), jax.ShapeDtypeStruct(k.shape, F32), jax.ShapeDtypeStruct(k.shape, F32)],
        compiler_params=_params())(q, k, v, o, lse, dmix)


def _band(i, T, Lc):
    start = pl.multiple_of(jnp.clip(WINDOW + (i - 1) * ROW_TILE, 0, T - BAND), WINDOW)
    qpos = (i - 1) * ROW_TILE + lax.broadcasted_iota(jnp.int32, (ROW_TILE, 1), 0)
    kpos = start - Lc + lax.broadcasted_iota(jnp.int32, (1, BAND), 1)
    ok = (jnp.abs(kpos - qpos) <= WINDOW) & (kpos >= 0) & (i > 0)
    return start, jnp.where(ok, 0.0, NEG_INF).astype(F32)


def _attn_win_fwd(name, q, k, v, sink, cfg):
    T, G, Lc = q.shape[0], cfg['G'], cfg['Lc']

    def body(sink_ref, q_ref, k_ref, v_ref, o_ref, lse_ref, mix_ref):
        h, i = pl.program_id(0), pl.program_id(1)
        start, bias = _band(i, T, Lc)
        kc, vc = k_ref[0:Lc, :], v_ref[0:Lc, :]
        kb, vb = k_ref[pl.ds(start, BAND), :], v_ref[pl.ds(start, BAND), :]
        for g in range(G):
            sl = slice(g * HEAD_DIM, (g + 1) * HEAD_DIM)
            qg = q_ref[:, sl]
            sk = sink_ref[h * G + g] * LOG2_E
            sc = _nt(qg, kc)
            sb = _nt(qg, kb) + bias
            m = jnp.maximum(jnp.maximum(jnp.max(sc, axis=1, keepdims=True), jnp.max(sb, axis=1, keepdims=True)), sk)
            pc, pb = jnp.exp2(sc - m), jnp.exp2(sb - m)
            l = jnp.sum(pc, axis=1, keepdims=True) + jnp.sum(pb, axis=1, keepdims=True) + jnp.exp2(sk - m)
            o = (_nn(_bf(pc), vc) + _nn(_bf(pb), vb)) / l
            o_ref[:, sl] = o
            mix_ref[:, sl] = _bf(o)
            lse_ref[:, sl] = jnp.broadcast_to(m + jnp.log2(l), (ROW_TILE, HEAD_DIM))

    qs, kvs = _att_specs(T, G)
    return pl.pallas_call(
        body, name=name, grid=(cfg['NKV'], T // ROW_TILE),
        in_specs=[pl.BlockSpec(memory_space=pltpu.SMEM), qs, kvs, kvs], out_specs=[qs, qs, qs],
        out_shape=[jax.ShapeDtypeStruct(q.shape, F32), jax.ShapeDtypeStruct(q.shape, F32), _mix_shape(q)],
        compiler_params=_params())(sink, q, k, v)


def _attn_win_bwd(name, q, k, v, o, lse, dmix, sink, cfg):
    T, G, Lc = q.shape[0], cfg['G'], cfg['Lc']

    def body(sink_ref, q_ref, k_ref, v_ref, o_ref, lse_ref, do_ref, dq_ref, dk_ref, dv_ref, dsink_ref):
        h, i = pl.program_id(0), pl.program_id(1)
        start, bias = _band(i, T, Lc)
        kc, vc = k_ref[0:Lc, :], v_ref[0:Lc, :]
        kb, vb = k_ref[pl.ds(start, BAND), :], v_ref[pl.ds(start, BAND), :]

        @pl.when(i == 0)
        def _():
            dk_ref[...] = jnp.zeros_like(dk_ref)
            dv_ref[...] = jnp.zeros_like(dv_ref)
            dsink_ref[...] = jnp.zeros_like(dsink_ref)

        for g in range(G):
            sl = slice(g * HEAD_DIM, (g + 1) * HEAD_DIM)
            qg, do = q_ref[:, sl], do_ref[:, sl]
            lse = lse_ref[:, g * HEAD_DIM:g * HEAD_DIM + 1]
            delta = jnp.sum(do * o_ref[:, sl], axis=1, keepdims=True)
            pc = jnp.exp2(_nt(qg, kc) - lse)
            pb = jnp.exp2(_nt(qg, kb) + bias - lse)
            ps = jnp.exp2(sink_ref[h * G + g] * LOG2_E - lse)
            dob = _bf(do)
            dv_ref[0:Lc, :] += _tn(_bf(pc), dob)
            dv_ref[pl.ds(start, BAND), :] += _tn(_bf(pb), dob)
            dsc = _bf(pc * (_nt(dob, vc) - delta))
            dsb = _bf(pb * (_nt(dob, vb) - delta))
            dq_ref[:, sl] = _nn(dsc, kc) + _nn(dsb, kb)
            dk_ref[0:Lc, :] += _tn(dsc, qg)
            dk_ref[pl.ds(start, BAND), :] += _tn(dsb, qg)
            dsk = jnp.where(i > 0, -jnp.sum(ps * delta, axis=0, keepdims=True), 0.0)
            dsink_ref[:, sl] += jnp.broadcast_to(dsk, (8, HEAD_DIM))

    qs, kvs = _att_specs(T, G)
    return pl.pallas_call(
        body, name=name, grid=(cfg['NKV'], T // ROW_TILE),
        in_specs=[pl.BlockSpec(memory_space=pltpu.SMEM), qs, kvs, kvs, qs, qs, qs],
        out_specs=[qs, kvs, kvs, pl.BlockSpec((None, 8, G * HEAD_DIM), lambda h, i: (h, 0, 0))],
        out_shape=[jax.ShapeDtypeStruct(q.shape, F32), jax.ShapeDtypeStruct(k.shape, F32), jax.ShapeDtypeStruct(k.shape, F32),
                   jax.ShapeDtypeStruct((cfg['NKV'], 8, G * HEAD_DIM), F32)],
        compiler_params=_params())(sink, q, k, v, o, lse, dmix)


def _seq_pos(T, Lc):
    row = lax.broadcasted_iota(jnp.int32, (T, 1), 0)
    return jnp.where(row < Lc, row, row - Lc), jnp.where(row < Lc, Lc, T - Lc)


def _fw(x, k, pos, seglen):
    return jnp.where(pos + k < seglen, pltpu.roll(x, x.shape[0] - k, 0), 0.0)


def _bw(x, k, pos):
    return jnp.where(pos - k >= 0, pltpu.roll(x, k, 0), 0.0)


def _conv_fwd(name, P, conv_w8, mix, cfg):
    T, Lc = P.shape[0], cfg['Lc']
    cb = (cfg['ATT'] + 2 * cfg['KVW']) // HEAD_DIM
    ob = cfg['ATT'] // HEAD_DIM
    na = AUX_WIDTH // HEAD_DIM

    def body(gb_ref, gc_ref, u_ref, w_ref, mix_ref, o_ref):
        pos, seglen = _seq_pos(T, Lc)
        z = gc_ref[...] * u_ref[...]
        conv = w_ref[0:1, :] * _bw(z, 1, pos) + w_ref[1:2, :] * z + w_ref[2:3, :] * _fw(z, 1, pos, seglen)
        o_ref[...] = (gb_ref[...] * conv).astype(o_ref.dtype)

    col = lambda off: pl.BlockSpec((T, HEAD_DIM), lambda c: (0, cb + off + c))
    return pl.pallas_call(
        body, name=name, grid=(na,),
        in_specs=[col(0), col(na), col(2 * na), pl.BlockSpec((8, HEAD_DIM), lambda c: (0, c)), pl.BlockSpec(memory_space=pl.ANY)],
        out_specs=pl.BlockSpec((T, HEAD_DIM), lambda c: (0, ob + c)),
        out_shape=jax.ShapeDtypeStruct(mix.shape, mix.dtype), input_output_aliases={4: 0},
        compiler_params=_params())(P, P, P, conv_w8, mix)


def _conv_bwd(name, P, conv_w8, dmix, cfg):
    T, Lc = P.shape[0], cfg['Lc']
    cb = (cfg['ATT'] + 2 * cfg['KVW']) // HEAD_DIM
    ob = cfg['ATT'] // HEAD_DIM
    na = AUX_WIDTH // HEAD_DIM

    def body(gb_ref, gc_ref, u_ref, w_ref, do_ref, dgb_ref, dgc_ref, du_ref, dw_ref):
        pos, seglen = _seq_pos(T, Lc)
        gc, u, do = gc_ref[...], u_ref[...], do_ref[...]
        z = gc * u
        zm, zp = _bw(z, 1, pos), _fw(z, 1, pos, seglen)
        w0, w1, w2 = w_ref[0:1, :], w_ref[1:2, :], w_ref[2:3, :]
        dgb_ref[...] = (do * (w0 * zm + w1 * z + w2 * zp)).astype(dgb_ref.dtype)
        dc = do * gb_ref[...]
        dz = w0 * _fw(dc, 1, pos, seglen) + w1 * dc + w2 * _bw(dc, 1, pos)
        dgc_ref[...] = (dz * u).astype(dgc_ref.dtype)
        du_ref[...] = (dz * gc).astype(du_ref.dtype)
        dw_ref[...] = jnp.zeros_like(dw_ref)
        dw_ref[0:1, :] = jnp.sum(dc * zm, axis=0, keepdims=True)
        dw_ref[1:2, :] = jnp.sum(dc * z, axis=0, keepdims=True)
        dw_ref[2:3, :] = jnp.sum(dc * zp, axis=0, keepdims=True)

    col = lambda off: pl.BlockSpec((T, HEAD_DIM), lambda c: (0, cb + off + c))
    wspec = pl.BlockSpec((8, HEAD_DIM), lambda c: (0, c))
    ocol = lambda off: pl.BlockSpec((T, HEAD_DIM), lambda c: (0, off + c))
    return pl.pallas_call(
        body, name=name, grid=(na,),
        in_specs=[col(0), col(na), col(2 * na), wspec, ocol(ob)],
        out_specs=[ocol(0), ocol(0), ocol(0), wspec],
        out_shape=[jax.ShapeDtypeStruct((T, AUX_WIDTH), BF16)] * 3 + [jax.ShapeDtypeStruct((8, AUX_WIDTH), F32)],
        compiler_params=_params())(P, P, P, conv_w8, dmix)


def _window_sums(x, half, pos, seglen):
    fwd, bwd = x, x
    s = 1
    while s < half:
        fwd = fwd + _fw(fwd, s, pos, seglen)
        bwd = bwd + _bw(bwd, s, pos)
        s *= 2
    return fwd, bwd


def _pooled(u, half, pos, seglen):
    fwd, bwd = _window_sums(u, half, pos, seglen)
    cnt = (jnp.minimum(pos + half, seglen) - jnp.maximum(pos - half, 0)).astype(F32)
    return (fwd + _bw(bwd, 1, pos)) / cnt - u, cnt


def _pool_fwd(name, P, pool_w, pool_scale, mix, cfg):
    T, Lc = P.shape[0], cfg['Lc']
    cb = (cfg['ATT'] + 2 * cfg['KVW']) // HEAD_DIM
    ob = cfg['ATT'] // HEAD_DIM

    def body(u_ref, w_ref, s_ref, mix_ref, o_ref):
        g = pl.program_id(0)
        pos, seglen = _seq_pos(T, Lc)
        for k, half in enumerate(POOL_HALF):
            @pl.when(g == k)
            def _(half=half):
                pooled, _ = _pooled(u_ref[...], half, pos, seglen)
                o_ref[...] = (_nn(_bf(pooled), _bf(w_ref[...])) * s_ref[...]).astype(o_ref.dtype)

    return pl.pallas_call(
        body, name=name, grid=(AUX_GROUPS,),
        in_specs=[pl.BlockSpec((T, HEAD_DIM), lambda g: (0, cb + g)), pl.BlockSpec((None, HEAD_DIM, HEAD_DIM), lambda g: (g, 0, 0)),
                  pl.BlockSpec((1, HEAD_DIM), lambda g: (0, g)), pl.BlockSpec(memory_space=pl.ANY)],
        out_specs=pl.BlockSpec((T, HEAD_DIM), lambda g: (0, ob + g)),
        out_shape=jax.ShapeDtypeStruct(mix.shape, mix.dtype), input_output_aliases={3: 0},
        compiler_params=_params())(P, pool_w, pool_scale, mix)


def _pool_bwd(name, P, pool_w, pool_scale, dmix, cfg):
    T, Lc = P.shape[0], cfg['Lc']
    cb = (cfg['ATT'] + 2 * cfg['KVW']) // HEAD_DIM
    ob = cfg['ATT'] // HEAD_DIM

    def body(u_ref, w_ref, s_ref, do_ref, du_ref, dw_ref, ds_ref):
        g = pl.program_id(0)
        pos, seglen = _seq_pos(T, Lc)
        for k, half in enumerate(POOL_HALF):
            @pl.when(g == k)
            def _(half=half):
                do = do_ref[...]
                pooled, cnt = _pooled(u_ref[...], half, pos, seglen)
                wb = _bf(w_ref[...])
                mixed = _nn(_bf(pooled), wb)
                ds_ref[...] = jnp.broadcast_to(jnp.sum(do * mixed, axis=0, keepdims=True), ds_ref.shape)
                dmixed = _bf(do * s_ref[...])
                dw_ref[...] = _tn(_bf(pooled), dmixed)
                dpooled = _nt(dmixed, wb)
                e = dpooled / cnt
                fwd, bwd = _window_sums(e, half, pos, seglen)
                adj = fwd + _fw(e, half, pos, seglen) + _bw(bwd, 1, pos) - _bw(e, half, pos)
                du_ref[...] = (adj - dpooled).astype(du_ref.dtype)

    wspec = pl.BlockSpec((None, HEAD_DIM, HEAD_DIM), lambda g: (g, 0, 0))
    return pl.pallas_call(
        body, name=name, grid=(AUX_GROUPS,),
        in_specs=[pl.BlockSpec((T, HEAD_DIM), lambda g: (0, cb + g)), wspec, pl.BlockSpec((1, HEAD_DIM), lambda g: (0, g)),
                  pl.BlockSpec((T, HEAD_DIM), lambda g: (0, ob + g))],
        out_specs=[pl.BlockSpec((T, HEAD_DIM), lambda g: (0, g)), wspec, pl.BlockSpec((8, HEAD_DIM), lambda g: (0, g))],
        out_shape=[jax.ShapeDtypeStruct((T, AUX_WIDTH), BF16), jax.ShapeDtypeStruct(pool_w.shape, F32),
                   jax.ShapeDtypeStruct((8, AUX_WIDTH), F32)],
        compiler_params=_params())(P, pool_w, pool_scale, dmix)


def _ffn_up(name, hn, wg3, wu3, dep=None):
    T, D = hn.shape
    J, k, _ = wg3.shape
    tm = _row_tile(T, 1088)
    dep_ins, dep_specs = _dep(dep, 2)

    def body(x_ref, wg_ref, wu_ref, *rest):
        s_ref, ud_ref, a_ref = rest[len(dep_ins):]
        x = x_ref[pl.ds(pl.multiple_of(pl.program_id(1) * tm, 16), tm), :]
        g, u = _nt(x, wg_ref[...]), _nt(x, wu_ref[...])
        sig = jax.nn.sigmoid(g)
        silu = g * sig
        s_ref[...] = silu.astype(s_ref.dtype)
        ud_ref[...] = (u * (sig * (1 + g * (1 - sig)))).astype(ud_ref.dtype)
        a_ref[...] = (silu * u).astype(a_ref.dtype)

    wspec = pl.BlockSpec((None, k, D), lambda j, i: (j, 0, 0))
    ospec = pl.BlockSpec((None, tm, k), lambda j, i: (j, i, 0))
    return pl.pallas_call(
        body, name=name, grid=(J, T // tm),
        in_specs=[pl.BlockSpec((T, D), lambda j, i: (0, 0), pipeline_mode=RESIDENT), wspec, wspec] + dep_specs,
        out_specs=[ospec, ospec, ospec],
        out_shape=[jax.ShapeDtypeStruct((J, T, k), BF16)] * 3,
        compiler_params=_params())(hn, wg3, wu3, *dep_ins)


def _ffn_dact(name, dF, wd3, silu_g, u_dsilu):
    T, D = dF.shape
    J, k, _ = wd3.shape
    tm = _row_tile(T, 1088)

    def body(df_ref, wd_ref, s_ref, ud_ref, dg_ref, du_ref):
        rows = pl.ds(pl.multiple_of(pl.program_id(1) * tm, 16), tm)
        da = _nt(df_ref[rows, :], wd_ref[...])
        du_ref[...] = (da * s_ref[...].astype(F32)).astype(du_ref.dtype)
        dg_ref[...] = (da * ud_ref[...].astype(F32)).astype(dg_ref.dtype)

    aspec = pl.BlockSpec((None, tm, k), lambda j, i: (j, i, 0))
    return pl.pallas_call(
        body, name=name, grid=(J, T // tm),
        in_specs=[pl.BlockSpec((T, D), lambda j, i: (0, 0), pipeline_mode=RESIDENT), pl.BlockSpec((None, k, D), lambda j, i: (j, 0, 0)), aspec, aspec],
        out_specs=[aspec, aspec],
        out_shape=[jax.ShapeDtypeStruct((J, T, k), BF16), jax.ShapeDtypeStruct((J, T, k), BF16)],
        compiler_params=_params())(dF, wd3, silu_g, u_dsilu)


def _loss_head(name, h, g, target, cfg):
    T, D = h.shape

    def body(h_ref, g_ref, t_ref, dh_ref, loss_ref, dg_ref):
        i = pl.program_id(0)

        @pl.when(i == 0)
        def _():
            dh_ref[...] = jnp.zeros_like(dh_ref)
            loss_ref[...] = jnp.zeros_like(loss_ref)
            dg_ref[...] = jnp.zeros_like(dg_ref)

        @pl.when(i > 0)
        def _():
            x = h_ref[...]
            r = _rstd(x)
            xhat = x * r
            gg = g_ref[...]
            err = xhat * gg - t_ref[...]
            loss_ref[...] += 0.5 * jnp.sum(jnp.sum(err * err, axis=1, keepdims=True) / D, axis=0, keepdims=True)
            dy = err / D
            dg_ref[0:1, :] += jnp.sum(dy * xhat, axis=0, keepdims=True)
            dxh = dy * gg
            dh_ref[...] = r * (dxh - xhat * jnp.mean(dxh * xhat, axis=-1, keepdims=True))

    row = pl.BlockSpec((ROW_TILE, D), lambda i: (i, 0))
    return pl.pallas_call(
        body, name=name, grid=(T // ROW_TILE,),
        in_specs=[row, pl.BlockSpec((1, D), lambda i: (0, 0)), pl.BlockSpec((ROW_TILE, D), lambda i: (jnp.maximum(i - 1, 0), 0))],
        out_specs=[row, pl.BlockSpec((8, 128), lambda i: (0, 0)), pl.BlockSpec((8, D), lambda i: (0, 0))],
        out_shape=[jax.ShapeDtypeStruct((T, D), F32), jax.ShapeDtypeStruct((8, 128), F32), jax.ShapeDtypeStruct((8, D), F32)],
        compiler_params=_params())(h, g, target)


def _adamw(name, parts, w, m, v, dep=None):
    R, C = w.shape
    n_parts = parts.shape[0]
    tr = _row_tile(R, max(16, ADAM_BLOCK_ELEMS // C)) if R % 16 == 0 else R
    bc1 = 1.0 - ADAM_B1 ** ADAM_STEP
    bc2 = 1.0 - ADAM_B2 ** ADAM_STEP
    dep_ins, dep_specs = _dep(dep, 1)

    def body(p_ref, w_ref, m_ref, v_ref, *rest):
        g_ref, d_ref, nm_ref, nv_ref = rest[len(dep_ins):]
        g = p_ref[0].astype(F32)
        for k in range(1, n_parts):
            g = g + p_ref[k].astype(F32)
        nm = ADAM_B1 * m_ref[...] + (1.0 - ADAM_B1) * g
        nv = ADAM_B2 * v_ref[...] + (1.0 - ADAM_B2) * (g * g)
        g_ref[...] = g
        nm_ref[...] = nm
        nv_ref[...] = nv
        d_ref[...] = -ADAM_LR * ((nm / bc1) / (jnp.sqrt(nv / bc2) + ADAM_EPS) + ADAM_WD * w_ref[...])

    blk = pl.BlockSpec((tr, C), lambda i: (i, 0))
    return pl.pallas_call(
        body, name=name, grid=(R // tr,), in_specs=[pl.BlockSpec((n_parts, tr, C), lambda i: (0, i, 0)), blk, blk, blk] + dep_specs,
        out_specs=[blk] * 4, out_shape=[jax.ShapeDtypeStruct((R, C), F32)] * 4, compiler_params=_params())(parts, w, m, v, *dep_ins)


class _WeightStream:
    def __init__(self, cast):
        self.cast, self.handles = cast, {}

    @staticmethod
    def _tag(l, group):
        return ("ffn" if group is FFN_WEIGHTS else group[0]) + str(l)

    def start(self, l, group, after=None):
        self.handles[l, group], token = _gather_start(f"gather_{self._tag(l, group)}_start", [self.cast(l, n) for n in group], after)
        return token

    def relay(self, l, group, after):
        self.handles[l, group], token = _gather_relay(f"gather_{self._tag(l, group)}_relay", self.handles[l, group], after)
        return token

    def get(self, l, group, after):
        got = dict(zip(group, _gather_wait(f"gather_{self._tag(l, group)}_wait", self.handles[l, group], after)))
        if 'w_out' in got:
            rows, cols = got['w_out'].shape[1:]
            got['w_out'] = got['w_out'].reshape(N_DEV * rows, cols)
        return got


def _layer_fwd(l, h, p, stream, mod, rope, conv_w8, cfg):
    nm = f"l{l}_"
    mod = mod + stream.relay(l, IN_WEIGHT, h)[0, 0]
    xn = _norm_mod(nm + "norm1", h, p['norm1_g'], mod, 0)
    W = stream.get(l, IN_WEIGHT, xn)
    token = None
    if l == 0:
        token = stream.start(0, OUT_WEIGHT, after=W['w_in']) + stream.start(0, FFN_WEIGHTS, after=W['w_in'])
    P = _mm_cols(nm + "w_in", xn, W['w_in'], dep=token)
    qr, kr, vb = _qk_prep(nm + "qk_prep", P, p['q_norm_g'], p['k_norm_g'], rope[0], rope[1], cfg)
    if l == 0:
        o, lse, mix = _attn_dense_fwd(nm + "attn", qr, kr, vb, cfg)
        mix = _conv_fwd(nm + "conv", P, conv_w8, mix, cfg)
    else:
        o, lse, mix = _attn_win_fwd(nm + "attn", qr, kr, vb, p['sink'], cfg)
        mix = _pool_fwd(nm + "pool", P, p['pool_w'], p['pool_scale'], mix, cfg)
    W.update(stream.get(l, OUT_WEIGHT, stream.relay(l, OUT_WEIGHT, mix)))
    y, h2 = _mm_plain(nm + "w_out", mix, W['w_out'], False, dep=stream.relay(l, FFN_WEIGHTS, o), res=(h, mod, 2, cfg['Lc']))
    hn = _norm_mod(nm + "norm2", h2, p['norm2_g'], mod, 1)
    W.update(stream.get(l, FFN_WEIGHTS, hn))
    token = stream.start(1, IN_WEIGHT, after=W['w_down']) if l == 0 else None
    silu_g, u_dsilu, A = _ffn_up(nm + "ffn_up", hn, W['w_gate'], W['w_up'], dep=token)
    if l == 0:
        token = stream.start(1, OUT_WEIGHT, after=A) + stream.start(1, FFN_WEIGHTS, after=A)
    F, h3 = _mm_shards_nn(nm + "w_down", A, W['w_down'], dep=token, res=(h2, mod, 5, cfg['Lc']))
    saved = dict(h=h, xn=xn, P=P, qr=qr, kr=kr, vb=vb, o=o, lse=lse, mix=mix, y=y, h2=h2, hn=hn, silu_g=silu_g, u_dsilu=u_dsilu, A=A, F=F)
    return h3, saved, W


def _layer_bwd(l, dh3, s, p, W, mod, rope, conv_w8, cfg, res2_bwd=None, below=None):
    nm = f"l{l}_bwd_"
    J = N_DEV
    dF, dmod = _gate_bwd(nm + "res2", dh3, s['F'], mod, 5) if res2_bwd is None else res2_bwd
    dG, dU = _ffn_dact(nm + "ffn_act", dF, W['w_down'], s['silu_g'], s['u_dsilu'])
    big = {'w_down': _wgrad_down(nm + "dw_down", s['A'], dF),
           'w_gate': _wgrad_down(nm + "dw_gate", dG, s['hn']),
           'w_up': _wgrad_down(nm + "dw_up", dU, s['hn'])}
    handles = {}
    handles['ffn'], token = _exchange_start(f"scatter_ffn{l}_start", [big[n] for n in FFN_WEIGHTS], True)
    mod = mod + token[0, 0]
    dhn = _mm_shards_nn2(nm + "dhn", dG, W['w_gate'], dU, W['w_up'])
    dh2, dm, dg2, dY, dm_gate = _norm_mod_bwd(nm + "norm2", dhn, s['h2'], p['norm2_g'], mod, 1, dh3, gate=(s['y'], mod, 2))
    dmod += dm + dm_gate
    dwo = _wgrad_rows(nm + "dw_out", s['mix'], dY)
    handles['w_out'], token = _exchange_start(f"scatter_w_out{l}_start", [dwo.reshape((J, dwo.shape[0] // J, dwo.shape[1]))], True)
    dmix = _mm_plain(nm + "dmix", dY, W['w_out'], True, dep=token)
    small = {'norm2_g': dg2[0]}
    if l == 0:
        dqr, dkr, dv = _attn_dense_bwd(nm + "attn", s['qr'], s['kr'], s['vb'], s['o'], s['lse'], dmix, cfg)
        *daux, dcw = _conv_bwd(nm + "conv", s['P'], conv_w8, dmix, cfg)
        small['conv_w'] = dcw[0:3]
    else:
        dqr, dkr, dv, dsk = _attn_win_bwd(nm + "attn", s['qr'], s['kr'], s['vb'], s['o'], s['lse'], dmix, p['sink'], cfg)
        du, dpw, dps = _pool_bwd(nm + "pool", s['P'], p['pool_w'], p['pool_scale'], dmix, cfg)
        daux = [du]
        small.update(sink=dsk[:, 0, ::HEAD_DIM].reshape(-1), pool_w=dpw, pool_scale=dps[0])
    dq, dk, dqg, dkg = _qk_prep_bwd(nm + "qk_prep", dqr, dkr, s['P'], p['q_norm_g'], p['k_norm_g'], rope[0], rope[1], cfg)
    small.update(q_norm_g=dqg[0], k_norm_g=dkg[0])
    dP = jnp.concatenate([dq, dk, dv.astype(BF16), *daux], axis=1)
    handles['w_in'], token = _exchange_start(f"scatter_w_in{l}_start", [_wgrad_cols(nm + "dw_in", s['xn'], dP, J)], True)
    mod = mod + token[0, 0]
    dxn = _mm_cols_nt(nm + "dxn", dP, W['w_in'])
    gate = None if below is None else (below[0], below[1], 5)
    dh, dm, dg1, *res2_below = _norm_mod_bwd(nm + "norm1", dxn, s['h'], p['norm1_g'], mod, 0, dh2, latent_only=(l == 0), gate=gate)
    dmod += dm
    small['norm1_g'] = dg1[0]
    return dh, dmod, small, handles, token, (tuple(res2_below) or None)


def _rope_tables(S, Lc):
    half = HEAD_DIM // 4
    pos = np.arange(S)
    inv = ROPE_THETA ** (-np.arange(0, 2 * half, 2, dtype=np.float32) / (2 * half))
    inv = jnp.asarray(inv, F32)
    ang_r = jnp.asarray(pos // GRID_W, F32)[:, None] * inv
    ang_c = jnp.asarray(pos % GRID_W, F32)[:, None] * inv
    cos = jnp.concatenate([jnp.cos(ang_r)] * 2 + [jnp.cos(ang_c)] * 2, axis=1)
    sin = jnp.concatenate([-jnp.sin(ang_r), jnp.sin(ang_r), -jnp.sin(ang_c), jnp.sin(ang_c)], axis=1)
    return (jnp.concatenate([jnp.ones((Lc, HEAD_DIM), F32), cos], axis=0),
            jnp.concatenate([jnp.zeros((Lc, HEAD_DIM), F32), sin], axis=0))


def _pad_rows(a, rows):
    return jnp.concatenate([a, jnp.zeros((rows - a.shape[0],) + a.shape[1:], a.dtype)], axis=0)


def _flat128(a, nlead):
    lead = a.shape[:nlead]
    f = a.reshape(lead + (-1,))
    pad = (-f.shape[-1]) % 128
    if pad:
        f = jnp.concatenate([f, jnp.zeros(lead + (pad,), f.dtype)], axis=-1)
    return f.reshape(lead + (-1, 128))


def _pack(named, nlead=0):
    rows, layout, at = [], {}, 0
    for name, a in named:
        f = _flat128(a, nlead)
        n = f.shape[-2]
        pad = (-n) % 8
        if pad:
            f = jnp.concatenate([f, jnp.zeros(f.shape[:-2] + (pad, 128), f.dtype)], axis=-2)
        layout[name] = (at, n, a.shape[nlead:])
        rows.append(f)
        at += n + pad
    return jnp.concatenate(rows, axis=-2), layout


def _unpack(arr, layout, name):
    at, n, shape = layout[name]
    return arr[..., at:at + n, :].reshape(arr.shape[:-2] + (-1,))[..., :math.prod(shape)].reshape(arr.shape[:-2] + tuple(shape))


def kernel(*args):
    A = dict(zip(INPUT_NAMES, args, strict=True))
    x, ctx = A['x'][0], A['ctx'][0]
    S, D = x.shape
    Lc = ctx.shape[0]
    T = Lc + S
    ATT = D - AUX_WIDTH
    KVW = (A['l1_w_in'].shape[1] * N_DEV - ATT - AUX_WIDTH) // 2
    cfg = dict(ATT=ATT, KVW=KVW, NKV=KVW // HEAD_DIM, G=ATT // KVW, Lc=Lc)
    assert Lc == ROW_TILE and S % ROW_TILE == 0 and T >= BAND and S % GRID_W == 0
    cw = A['l0_conv_w'].shape[1]
    me = 4 * lax.axis_index("x") + 2 * lax.axis_index("y") + lax.axis_index("c")

    def layer_params(l):
        pre = f"l{l}_"
        return {k[len(pre):]: (v.reshape(1, -1) if v.ndim == 1 and k != 'l1_sink' else v) for k, v in A.items() if k.startswith(pre)}

    params = [layer_params(0), layer_params(1)]

    def to_wire(l, n):
        w = A[f'l{l}_{n}']
        return (w.T if n in TRANSPOSED else w).astype(BF16)

    wire = {(0, 'w_in'): to_wire(0, 'w_in')}
    stream = _WeightStream(lambda l, n: wire[l, n])
    token = stream.start(0, IN_WEIGHT)

    big_names = [n for n in WEIGHT_NAMES if n[3:] in BIG_WEIGHTS + ('w_mod',)]
    rest = [n for n in WEIGHT_NAMES if n not in big_names]
    early = ['x', 'ctx'] + rest + ['m_' + n for n in rest] + ['v_' + n for n in rest]
    token, held = lax.optimization_barrier((token, [A[n] for n in early]))
    A.update(zip(early, held))
    x, ctx = A['x'][0], A['ctx'][0]
    wp, layw = _pack([(n, A[n]) for n in rest])
    mp, _ = _pack([(n, A['m_' + n]) for n in rest])
    vp, _ = _pack([(n, A['v_' + n]) for n in rest])
    rope = _rope_tables(S, Lc)
    h = jnp.concatenate([ctx, x], axis=0)

    sc_own = jax.nn.silu(A['c']) + token[0, 0]
    first, lay0 = _pack([('sc', sc_own), ('conv_w', A['l0_conv_w'])])
    others = [(l, n) for l in range(2) for n in BIG_WEIGHTS if (l, n) not in wire]
    first, h, wp, mp, vp, cast = lax.optimization_barrier((first, h, wp, mp, vp, [to_wire(l, n) for l, n in others]))
    wire.update(zip(others, cast))
    first_all = _exchange("gather_cond", [first], False)[0]
    sc_all = _unpack(first_all, lay0, 'sc')[:, 0]
    conv_w = _unpack(first_all, lay0, 'conv_w').transpose(1, 0, 2).reshape(3, N_DEV * cw)
    conv_w8 = _pad_rows(conv_w, 8)
    sc_ctx = jax.nn.silu(A['c_ctx'])
    s16 = _pad_rows(jnp.concatenate([sc_all, sc_ctx[None]], axis=0), 16)

    nmod = A['l0_w_mod'].shape[1]
    modp = jnp.concatenate([_mm_plain(f"l{l}_mod", s16, A[f'l{l}_w_mod'], False) for l in range(2)], axis=1)
    modp_all = _exchange("gather_mod", [modp], False)[0]
    mods = []
    for l in range(2):
        full = modp_all[:, :, l * nmod:(l + 1) * nmod].transpose(1, 0, 2).reshape(16, N_MOD * D) + A[f'l{l}_b_mod'][None]
        both = jnp.stack([full[8], lax.dynamic_index_in_dim(full, me, 0, keepdims=False)]).reshape(2, N_MOD, D)
        mods.append(jnp.concatenate([both, jnp.zeros((2, 8 - N_MOD, D), F32)], axis=1))

    saved, W = [], []
    for l in range(2):
        h, s, Wl = _layer_fwd(l, h, params[l], stream, mods[l], rope, conv_w8, cfg)
        saved.append(s)
        W.append(Wl)

    dh, loss_blk, dgf = _loss_head("loss_head", h, A['final_norm_g'].reshape(1, -1), A['loss_target'][0], cfg)
    loss = lax.psum(loss_blk[0, 0], ("x", "y", "c"))

    grads, small, dmods, scatters = {}, {'final_norm_g': dgf[0]}, [None, None], [None, None]
    token, res2_bwd = jnp.zeros((8, 128), F32), None
    for l in (1, 0):
        below = (saved[0]['F'], mods[0]) if l == 1 else None
        dh, dmods[l], sm, scatters[l], token, res2_bwd = _layer_bwd(
            l, dh, saved[l], params[l], W[l], mods[l] + token[0, 0], rope, conv_w8, cfg, res2_bwd, below)
        small.update({f'l{l}_{k}': v for k, v in sm.items()})
    grad_x = dh[None]

    def landed(l, key, after):
        group = FFN_WEIGHTS if key == 'ffn' else (key,)
        for n, parts in zip(group, _exchange_wait(f"scatter_{key}{l}_wait", scatters[l][key], after)):
            shape = A[f'l{l}_{n}'].shape
            grads[f'l{l}_{n}'] = (parts.reshape((N_DEV,) + (shape[::-1] if n in TRANSPOSED else shape)), None)

    out = {}

    def adam(n, dep=None):
        w, m, v = A[n], A['m_' + n], A['v_' + n]
        if n[3:] in TRANSPOSED:
            res = _adamw("adamw_" + n, grads[n][0], w.T, m.T, v.T, dep)
            out[n] = tuple(r.T for r in res)
        else:
            res = out[n] = _adamw("adamw_" + n, grads[n][0], w, m, v, dep)
        return res[1]

    small_names = [n for n in WEIGHT_NAMES if n in small]
    pieces = [(n, small[n]) for n in small_names]
    for l in range(2):
        pieces += [(f'dmod{l}', dmods[l][1, :N_MOD]), (f'dcmod{l}', dmods[l][0, :N_MOD])]
    second, lay1 = _pack(pieces)
    small_handle, small_token = _exchange_start("gather_small_start", [second], False, after=token)

    last, chain = dh, small_token
    for l in (1, 0):
        for key in ('ffn', 'w_out') + (('w_in',) if l == 1 else ()):
            landed(l, key, last)
            for n in (FFN_WEIGHTS if key == 'ffn' else (key,)):
                last = adam(f'l{l}_{n}', chain)
                chain = last[:8, :128]
    second_all = _exchange_wait("gather_small_wait", small_handle, last)[0]

    dsc_part = jnp.zeros((16, D), F32)
    for l in range(2):
        dm16 = _pad_rows(jnp.concatenate([_unpack(second_all, lay1, f'dmod{l}').reshape(N_DEV, N_MOD * D),
                                          jnp.sum(_unpack(second_all, lay1, f'dcmod{l}'), axis=0).reshape(1, N_MOD * D)], axis=0), 16)
        mine = lax.dynamic_slice_in_dim(dm16, me * nmod, nmod, axis=1)
        tk = _col_tile(D, 512)
        gw = _mm_tn(f"l{l}_dw_mod", s16, mine, pl.BlockSpec((16, tk), lambda kb, t: (0, kb)), pl.BlockSpec((16, nmod), lambda kb, t: (0, 0)),
                    jax.ShapeDtypeStruct((D, nmod), F32), pl.BlockSpec((tk, nmod), lambda kb, t: (kb, 0)), (tk, nmod), (D // tk, 1))
        grads[f'l{l}_w_mod'] = (gw[None], None)
        dsc_part += _mm_plain(f"l{l}_dsc", mine, A[f'l{l}_w_mod'], True)
        dmod_dev = _unpack(second_all, lay1, f'dmod{l}') + _unpack(second_all, lay1, f'dcmod{l}')
        grads[f'l{l}_b_mod'] = (dmod_dev.reshape(N_DEV, N_MOD * D), None)
    dsig = jax.nn.sigmoid(A['c_ctx'])
    dsilu = dsig * (1 + A['c_ctx'] * (1 - dsig))
    third_all = _exchange("gather_dsc", [dsc_part[8:9]], False)[0]
    grads['c_ctx'] = (third_all[:, 0] * dsilu[None], None)
    for n in small_names:
        g8 = _unpack(second_all, lay1, n)
        if n == 'l0_conv_w':
            g8 = lax.dynamic_slice_in_dim(g8, me * cw, cw, axis=2)
        grads[n] = (g8, None)

    adam('l0_w_mod')
    last = adam('l1_w_mod')
    gp, _ = _pack([(n, grads[n][0]) for n in rest], nlead=1)
    res = _adamw("adamw_small", gp, wp, mp, vp)
    for n in rest:
        out[n] = tuple(_unpack(r, layw, n) for r in res)
    landed(0, 'w_in', last)
    adam('l0_w_in')

    outs = [loss, grad_x]
    for k in range(4):
        outs += [out[n][k] for n in WEIGHT_NAMES]
    return tuple(outs)
```
